```python
import math
import jax, jax.numpy as jnp
from jax import lax
import numpy as np

D_MODEL = 1024
BATCH = 4
SEQ = 4096
DEPTH = 4

GROUP_WIDTH = 256
MIX_WIDTH = 4 * GROUP_WIDTH
Q_BLOCK = 128
EPS = 1e-6
D_FF = 2816

A_HEADS = 4
A_HEAD_DIM = 64
IDX_HEADS = 8
IDX_DIM = 32
TOPK_MAX = 256
REL_BUCKETS = 32
REL_MAX_DIST = 128

B_HEADS = 4
B_KEY_DIM = 32
B_VAL_DIM = 64
B_GATE_RANK = 16
B_GATE_TAU = 16.0
B_CHUNK = 64

C_CHANNELS = 256
C_KERNEL = 31

D_HEADS = 4
D_Q_RANK = 256
D_KV_RANK = 128
D_NOPE = 64
D_ROPE = 32
D_V = 64
D_QK = D_NOPE + D_ROPE
ROPE_THETA = 10000.0

IN_WIDTHS = (
    A_HEADS * A_HEAD_DIM, A_HEADS * A_HEAD_DIM, A_HEADS * A_HEAD_DIM,
    IDX_HEADS * IDX_DIM, IDX_DIM, IDX_HEADS,
    B_HEADS * B_KEY_DIM, B_HEADS * B_KEY_DIM, B_HEADS * B_VAL_DIM,
    B_GATE_RANK, B_HEADS * B_VAL_DIM,
    2 * C_CHANNELS,
    D_Q_RANK, D_KV_RANK, D_ROPE,
)
IN_WIDTH = sum(IN_WIDTHS)

kernel_name = "hybrid_parallel_heads_dsa_gla_conv_mla"


def rmsnorm(x, g):
    x32 = x.astype(jnp.float32)
    y = x32 * lax.rsqrt(jnp.mean(x32 * x32, axis=-1, keepdims=True) + EPS)
    return (y * g.astype(jnp.float32)).astype(x.dtype)


def swiglu(x, w_gate, w_up, w_down):
    return (jax.nn.silu(x @ w_gate) * (x @ w_up)) @ w_down


def t5_bucket(dist):
    max_exact = REL_BUCKETS // 2
    d = jnp.maximum(dist, 0)
    df = jnp.maximum(d, 1).astype(jnp.float32)
    large = max_exact + (jnp.log(df / max_exact) / math.log(REL_MAX_DIST / max_exact)
                         * (REL_BUCKETS - max_exact)).astype(jnp.int32)
    large = jnp.minimum(large, REL_BUCKETS - 1)
    return jnp.where(d < max_exact, d, large)


def rope(x, pos):
    half = x.shape[-1] // 2
    freqs = ROPE_THETA ** (-jnp.arange(half, dtype=jnp.float32) / half)
    ang = pos.astype(jnp.float32)[:, None] * freqs[None, :]
    cos = jnp.cos(ang)[:, None, :]
    sin = jnp.sin(ang)[:, None, :]
    x32 = x.astype(jnp.float32)
    x1, x2 = x32[..., :half], x32[..., half:]
    return jnp.concatenate([x1 * cos - x2 * sin, x2 * cos + x1 * sin], axis=-1).astype(x.dtype)


def dsa_mixer(q, k, v, iq, ik, iw, g_q, g_k, rel_bias):
    bsz, L = q.shape[0], q.shape[1]
    topk = min(TOPK_MAX, L // 4)
    q = rmsnorm(q, g_q)
    k = rmsnorm(k, g_k)
    scale = A_HEAD_DIM ** -0.5
    idx_scale = IDX_DIM ** -0.5
    w = iw.astype(jnp.float32) * (IDX_HEADS ** -0.5)
    key_pos = jnp.arange(L)

    def block(i):
        t0 = i * Q_BLOCK
        qb = lax.dynamic_slice_in_dim(q, t0, Q_BLOCK, axis=1)
        iqb = lax.dynamic_slice_in_dim(iq, t0, Q_BLOCK, axis=1)
        wb = lax.dynamic_slice_in_dim(w, t0, Q_BLOCK, axis=1)
        qpos = t0 + jnp.arange(Q_BLOCK)
        causal = key_pos[None, :] <= qpos[:, None]
        s = jax.nn.relu(jnp.einsum('bthd,bsd->bths', iqb, ik).astype(jnp.float32) * idx_scale)
        score = jnp.einsum('bths,bth->bts', s, wb)
        score = jnp.where(causal[None], score, -jnp.inf)
        _, sel = lax.top_k(score, topk)
        ks = jax.vmap(lambda kb, ib: kb[ib])(k, sel)
        vs = jax.vmap(lambda vb, ib: vb[ib])(v, sel)
        valid = sel <= qpos[None, :, None]
        bias = rel_bias[t5_bucket(qpos[None, :, None] - sel)]
        logits = (jnp.einsum('bthd,btkhd->bthk', qb, ks).astype(jnp.float32) * scale
                  + jnp.swapaxes(bias, 2, 3).astype(jnp.float32))
        logits = jnp.where(valid[:, :, None, :], logits, -jnp.inf)
        p = jax.nn.softmax(logits, axis=-1)
        return jnp.einsum('bthk,btkhd->bthd', p.astype(v.dtype), vs)

    out = lax.map(block, jnp.arange(L // Q_BLOCK))
    return jnp.moveaxis(out, 0, 1).reshape(bsz, L, A_HEADS * A_HEAD_DIM)


def gla_mixer(q, k, v, g_lat, r, w_gate_up, b_gate, g_out):
    bsz, L = q.shape[0], q.shape[1]
    f32 = jnp.float32
    log_a = jax.nn.log_sigmoid((g_lat @ w_gate_up + b_gate).astype(f32)) / B_GATE_TAU
    log_a = log_a.reshape(bsz, L, B_HEADS, B_KEY_DIM)
    qf = q.astype(f32) * (B_KEY_DIM ** -0.5)
    kf = k.astype(f32)
    vf = v.astype(f32)
    n = L // B_CHUNK

    def to_chunks(t):
        return jnp.moveaxis(t.reshape(bsz, n, B_CHUNK, t.shape[2], t.shape[3]), 1, 0)

    tri = jnp.tril(jnp.ones((B_CHUNK, B_CHUNK), dtype=bool))

    def step(S, inp):
        qc, kc, vc, gc = inp
        b = jnp.cumsum(gc, axis=1)
        o_inter = jnp.einsum('bchk,bhkv->bchv', qc * jnp.exp(b), S)
        diff = b[:, :, None] - b[:, None, :]
        decay = jnp.exp(jnp.where(tri[None, :, :, None, None], diff, -jnp.inf))
        A = jnp.einsum('bihk,bjhk,bijhk->bhij', qc, kc, decay)
        o_intra = jnp.einsum('bhij,bjhv->bihv', A, vc)
        b_last = b[:, -1]
        k_dec = kc * jnp.exp(b_last[:, None] - b)
        S = jnp.exp(b_last)[..., None] * S + jnp.einsum('bchk,bchv->bhkv', k_dec, vc)
        return S, o_inter + o_intra

    S0 = jnp.zeros((bsz, B_HEADS, B_KEY_DIM, B_VAL_DIM), f32)
    _, o = lax.scan(step, S0, (to_chunks(qf), to_chunks(kf), to_chunks(vf), to_chunks(log_a)))
    o = jnp.moveaxis(o, 0, 1).reshape(bsz, L, B_HEADS, B_VAL_DIM)
    o = rmsnorm(o, g_out).reshape(bsz, L, B_HEADS * B_VAL_DIM)
    return (o * jax.nn.silu(r.astype(f32))).astype(q.dtype)


def conv_mixer(u, w_dw, b_dw, g_norm):
    a, gate = jnp.split(u, 2, axis=-1)
    h = a * jax.nn.sigmoid(gate)
    h = lax.conv_general_dilated(h, w_dw.astype(h.dtype), window_strides=(1,),
                                 padding=[(C_KERNEL - 1, 0)],
                                 dimension_numbers=('NWC', 'WIO', 'NWC'),
                                 feature_group_count=C_CHANNELS) + b_dw
    return jax.nn.silu(rmsnorm(h, g_norm))


def mla_mixer(c_q, c_kv, k_pe, g_qa, w_uq, g_kva, w_ukv, g_q, g_k, pos):
    bsz, L = c_q.shape[0], c_q.shape[1]
    q = (rmsnorm(c_q, g_qa) @ w_uq).reshape(bsz, L, D_HEADS, D_QK)
    kv = (rmsnorm(c_kv, g_kva) @ w_ukv).reshape(bsz, L, D_HEADS, D_NOPE + D_V)
    k_nope, v = kv[..., :D_NOPE], kv[..., D_NOPE:]
    k = jnp.concatenate([k_nope, jnp.broadcast_to(k_pe[:, :, None, :], (bsz, L, D_HEADS, D_ROPE))], axis=-1)
    q = rmsnorm(q, g_q)
    k = rmsnorm(k, g_k)
    q = jnp.concatenate([q[..., :D_NOPE], rope(q[..., D_NOPE:], pos)], axis=-1)
    k = jnp.concatenate([k[..., :D_NOPE], rope(k[..., D_NOPE:], pos)], axis=-1)
    scale = D_QK ** -0.5
    key_pos = jnp.arange(L)

    def block(i):
        t0 = i * Q_BLOCK
        qb = lax.dynamic_slice_in_dim(q, t0, Q_BLOCK, axis=1)
        qpos = t0 + jnp.arange(Q_BLOCK)
        causal = key_pos[None, :] <= qpos[:, None]
        logits = jnp.einsum('bthd,bshd->bhts', qb, k).astype(jnp.float32) * scale
        logits = jnp.where(causal[None, None], logits, -jnp.inf)
        p = jax.nn.softmax(logits, axis=-1)
        return jnp.einsum('bhts,bshd->bthd', p.astype(v.dtype), v)

    out = lax.map(block, jnp.arange(L // Q_BLOCK))
    return jnp.moveaxis(out, 0, 1).reshape(bsz, L, D_HEADS * D_V)


def hybrid_layer(x, ffn1_norm, ffn1_gate, ffn1_up, ffn1_down, mix_norm, w_in,
                 a_q_norm, a_k_norm, rel_bias, b_gate_up, b_gate_bias, b_out_norm,
                 c_dw_w, c_dw_b, c_norm, d_qa_norm, d_uq, d_kva_norm, d_ukv, d_q_norm, d_k_norm,
                 w_out, ffn2_norm, ffn2_gate, ffn2_up, ffn2_down, pos):
    bsz, L = x.shape[0], x.shape[1]
    x = x + 0.5 * swiglu(rmsnorm(x, ffn1_norm), ffn1_gate, ffn1_up, ffn1_down)

    z = rmsnorm(x, mix_norm) @ w_in
    (aq, ak, av, iq, ik, iw, bq, bk, bv, bg, br, cu, dcq, dckv, dkpe) = jnp.split(
        z, np.cumsum(IN_WIDTHS)[:-1].tolist(), axis=-1)

    y_a = dsa_mixer(aq.reshape(bsz, L, A_HEADS, A_HEAD_DIM), ak.reshape(bsz, L, A_HEADS, A_HEAD_DIM),
                    av.reshape(bsz, L, A_HEADS, A_HEAD_DIM), iq.reshape(bsz, L, IDX_HEADS, IDX_DIM),
                    ik, iw, a_q_norm, a_k_norm, rel_bias)
    y_b = gla_mixer(bq.reshape(bsz, L, B_HEADS, B_KEY_DIM), bk.reshape(bsz, L, B_HEADS, B_KEY_DIM),
                    bv.reshape(bsz, L, B_HEADS, B_VAL_DIM), bg, br, b_gate_up, b_gate_bias, b_out_norm)
    y_c = conv_mixer(cu, c_dw_w, c_dw_b, c_norm)
    y_d = mla_mixer(dcq, dckv, dkpe, d_qa_norm, d_uq, d_kva_norm, d_ukv, d_q_norm, d_k_norm, pos)

    y = jnp.concatenate([y_a, y_b.astype(x.dtype), y_c.astype(x.dtype), y_d], axis=-1) @ w_out
    x = x + y

    x = x + 0.5 * swiglu(rmsnorm(x, ffn2_norm), ffn2_gate, ffn2_up, ffn2_down)
    return x


def setup_inputs(seed: int = 0) -> dict:
    key = jax.random.key(seed)
    ks = iter(jax.random.split(key, 40))
    f32 = jnp.float32

    def w(shape, fan_in):
        return jax.random.normal(next(ks), shape, f32) * (fan_in ** -0.5)

    def gain(shape):
        return 1.0 + 0.05 * jax.random.normal(next(ks), shape, f32)

    def small(shape, s):
        return s * jax.random.normal(next(ks), shape, f32)

    Ld = DEPTH
    return {
        "x": jax.random.normal(next(ks), (BATCH, SEQ, D_MODEL), f32),
        "ffn1_norm": gain((Ld, D_MODEL)),
        "ffn1_gate": w((Ld, D_MODEL, D_FF), D_MODEL),
        "ffn1_up": w((Ld, D_MODEL, D_FF), D_MODEL),
        "ffn1_down": w((Ld, D_FF, D_MODEL), D_FF),
        "mix_norm": gain((Ld, D_MODEL)),
        "w_in": w((Ld, D_MODEL, IN_WIDTH), D_MODEL),
        "a_q_norm": gain((Ld, A_HEAD_DIM)),
        "a_k_norm": gain((Ld, A_HEAD_DIM)),
        "rel_bias": small((REL_BUCKETS, A_HEADS), 0.2),
        "b_gate_up": w((Ld, B_GATE_RANK, B_HEADS * B_KEY_DIM), B_GATE_RANK),
        "b_gate_bias": small((Ld, B_HEADS * B_KEY_DIM), 0.1),
        "b_out_norm": gain((Ld, B_VAL_DIM)),
        "c_dw_w": w((Ld, C_KERNEL, 1, C_CHANNELS), C_KERNEL),
        "c_dw_b": small((Ld, C_CHANNELS), 0.02),
        "c_norm": gain((Ld, C_CHANNELS)),
        "d_qa_norm": gain((Ld, D_Q_RANK)),
        "d_uq": w((Ld, D_Q_RANK, D_HEADS * D_QK), D_Q_RANK),
        "d_kva_norm": gain((Ld, D_KV_RANK)),
        "d_ukv": w((Ld, D_KV_RANK, D_HEADS * (D_NOPE + D_V)), D_KV_RANK),
        "d_q_norm": gain((Ld, D_QK)),
        "d_k_norm": gain((Ld, D_QK)),
        "w_out": w((Ld, MIX_WIDTH, D_MODEL), MIX_WIDTH),
        "ffn2_norm": gain((Ld, D_MODEL)),
        "ffn2_gate": w((Ld, D_MODEL, D_FF), D_MODEL),
        "ffn2_up": w((Ld, D_MODEL, D_FF), D_MODEL),
        "ffn2_down": w((Ld, D_FF, D_MODEL), D_FF),
    }


def reference(x, ffn1_norm, ffn1_gate, ffn1_up, ffn1_down, mix_norm, w_in, a_q_norm, a_k_norm,
              rel_bias, b_gate_up, b_gate_bias, b_out_norm, c_dw_w, c_dw_b, c_norm,
              d_qa_norm, d_uq, d_kva_norm, d_ukv, d_q_norm, d_k_norm, w_out,
              ffn2_norm, ffn2_gate, ffn2_up, ffn2_down):
    pos = jnp.arange(x.shape[1])
    for l in range(DEPTH):
        x = hybrid_layer(x, ffn1_norm[l], ffn1_gate[l], ffn1_up[l], ffn1_down[l], mix_norm[l], w_in[l],
                         a_q_norm[l], a_k_norm[l], rel_bias, b_gate_up[l], b_gate_bias[l], b_out_norm[l],
                         c_dw_w[l], c_dw_b[l], c_norm[l], d_qa_norm[l], d_uq[l], d_kva_norm[l], d_ukv[l],
                         d_q_norm[l], d_k_norm[l], w_out[l], ffn2_norm[l], ffn2_gate[l], ffn2_up[l],
                         ffn2_down[l], pos)
    return x
```

```python
import functools
import math

import jax
import jax.numpy as jnp
import numpy as np
from jax import lax
from jax.experimental import pallas as pl
from jax.experimental.pallas import tpu as pltpu

F32 = jnp.float32
BF16 = jnp.bfloat16
I32 = jnp.int32

EPS = 1e-6
GROUP_WIDTH = 256
A_HEADS, A_HEAD_DIM = 4, 64
IDX_HEADS, IDX_DIM = 8, 32
TOPK_MAX = 256
REL_BUCKETS, REL_MAX_DIST = 32, 128
B_HEADS, B_KEY_DIM, B_VAL_DIM, B_GATE_RANK = 4, 32, 64, 16
B_GATE_TAU = 16.0
B_CHUNK = 64
C_CHANNELS, C_KERNEL = 256, 31
D_HEADS, D_Q_RANK, D_KV_RANK, D_NOPE, D_ROPE, D_V = 4, 256, 128, 64, 32, 64
D_QK = D_NOPE + D_ROPE
ROPE_THETA = 10000.0

LANE = 128
HEAD_PAD = 128
ATT_BLK = 256
INT_MIN = -2 ** 31
NEG = -1e30
VMEM_LIMIT = 56 * 1024 * 1024

N_AK, N_IK, N_BQ, N_BK, N_BV, N_BG, N_BR, N_CU, N_DCQ, N_DCKV, N_DKPE, N_TOT = (
    0, 256, 384, 512, 640, 896, 1024, 1280, 1792, 2048, 2176, 2304)
T_AQ, T_AV, T_IQ, T_IW, T_TOT = 0, 512, 768, 1024, 1040


def _dot(a, b):
    return jnp.dot(a, b, preferred_element_type=F32)


def _dot_nt(a, b):
    return lax.dot_general(a, b, (((1,), (1,)), ((), ())), preferred_element_type=F32)


def _dot_tn(a, b):
    return lax.dot_general(a, b, (((0,), (0,)), ((), ())), preferred_element_type=F32)


def _group_sum(x2, bd):
    hi = x2.astype(BF16)
    lo = (x2 - hi.astype(F32)).astype(BF16)
    return _dot(hi, bd) + _dot(lo, bd)


def _const_spec(shape):
    nd = len(shape)
    return pl.BlockSpec(shape, lambda *_: (0,) * nd, pipeline_mode=pl.Buffered(1))


def _cparams(sem):
    return pltpu.CompilerParams(dimension_semantics=sem, vmem_limit_bytes=VMEM_LIMIT)


def _ffn_body(has_mix, fc, *refs):
    if has_mix:
        x_ref, ya, yb, yc, yd, wo_ref, g_ref, wg_ref, wu_ref, wd_ref, o_ref, h_scr = refs
    else:
        x_ref, g_ref, wg_ref, wu_ref, wd_ref, o_ref, h_scr = refs
    x = x_ref[...]
    if has_mix:
        y = jnp.concatenate([ya[...], yb[...], yc[...], yd[...]], axis=-1)
        x = x + _dot(y, wo_ref[...])
    ms = jnp.mean(x * x, axis=-1, keepdims=True)
    xn = (x * lax.rsqrt(ms + EPS) * g_ref[...]).astype(BF16)
    d_ff = wg_ref.shape[1]
    for c in range(d_ff // fc):
        sl = slice(c * fc, (c + 1) * fc)
        gate = _dot(xn, wg_ref[:, sl])
        up = _dot(xn, wu_ref[:, sl])
        h_scr[:, sl] = (gate * jax.nn.sigmoid(gate) * up).astype(BF16)
    o_ref[...] = x + 0.5 * _dot(h_scr[...], wd_ref[...])


def _ffn(x2, g, wg, wu, wd, mix=None, tm=512, fc=256):
    m, d = x2.shape
    d_ff = wg.shape[1]
    tm = min(tm, m)
    row = lambda w: pl.BlockSpec((tm, w), lambda i: (i, 0))
    in_specs = [row(d)]
    args = [x2]
    if mix is not None:
        ys, wo = mix
        in_specs += [row(GROUP_WIDTH)] * 4 + [_const_spec(wo.shape)]
        args += list(ys) + [wo]
    in_specs += [_const_spec((1, d)), _const_spec(wg.shape), _const_spec(wu.shape), _const_spec(wd.shape)]
    args += [g, wg, wu, wd]
    return pl.pallas_call(
        functools.partial(_ffn_body, mix is not None, fc),
        grid=(m // tm,),
        in_specs=in_specs,
        out_specs=row(d),
        out_shape=jax.ShapeDtypeStruct((m, d), F32),
        scratch_shapes=[pltpu.VMEM((tm, d_ff), BF16)],
        compiler_params=_cparams(("parallel",)),
        name="ffn_mix" if mix is not None else "ffn",
    )(*args)


def _mix_in_body(tm, blk,
                 x_ref, gmix_ref, wn_ref, wt_ref, bd_ref,
                 gaq_ref, gak_ref,
                 wgu_ref, bgb_ref,
                 gqa_ref, wuq_ref, gdq_ref, cosT_ref, sinT_ref,
                 gkva_ref, wuk_ref, wuvT_ref, gdk_ref, gdkpe_ref, cpe_ref, spe_ref,
                 aqT_ref, ak_ref, avT_ref, iqT_ref, ik_ref, iwT_ref,
                 bq_ref, bk_ref, bv_ref, bla_ref, br_ref,
                 ch_ref,
                 dqT_ref, dk_ref, dvT_ref):
    nlt = tm // LANE
    x = x_ref[0]
    ms = jnp.mean(x * x, axis=-1, keepdims=True)
    xn = (x * lax.rsqrt(ms + EPS) * gmix_ref[...]).astype(BF16)
    bd = bd_ref[...]
    lane = lax.broadcasted_iota(I32, (tm, LANE), 1)

    def zs(off, width):
        return _dot(xn, wn_ref[:, off:off + width])

    def lanes(g):
        return jnp.tile(g, (1, nlt))

    zt = _dot_nt(wt_ref[...], xn)

    aq = zt[T_AQ:T_AQ + A_HEADS * HEAD_PAD].reshape(A_HEADS, HEAD_PAD, tm)
    aq_ms = jnp.sum(aq * aq, axis=1, keepdims=True) * (1.0 / A_HEAD_DIM)
    aq = aq * lax.rsqrt(aq_ms + EPS) * lanes(gaq_ref[...])[None] * (A_HEAD_DIM ** -0.5)
    aqT_ref[0] = aq.reshape(A_HEADS * HEAD_PAD, tm).astype(BF16)
    av = zt[T_AV:T_AV + GROUP_WIDTH].astype(BF16)
    for c in range(tm // blk):
        avT_ref[0, c] = av[:, c * blk:(c + 1) * blk]
    iqT_ref[0] = zt[T_IQ:T_IQ + IDX_HEADS * IDX_DIM].astype(BF16)
    iwT_ref[0] = zt[T_IW:T_IW + IDX_HEADS] * ((IDX_HEADS ** -0.5) * (IDX_DIM ** -0.5))

    ak = zs(N_AK, GROUP_WIDTH)
    ak_ms = _group_sum(ak * ak, bd) * (1.0 / A_HEAD_DIM)
    ak = ak * lax.rsqrt(ak_ms + EPS) * gak_ref[...]
    for h in range(A_HEADS):
        pair = ak[:, LANE * (h // 2):LANE * (h // 2) + LANE]
        if h % 2 == 1:
            pair = pltpu.roll(pair, 64, 1)
        ak_ref[0, h] = jnp.where(lane < A_HEAD_DIM, pair, 0.0).astype(BF16)
    ik_ref[0] = zs(N_IK, LANE)[:, :IDX_DIM].astype(BF16)

    bq_ref[0] = zs(N_BQ, LANE) * (B_KEY_DIM ** -0.5)
    bk_ref[0] = zs(N_BK, LANE)
    bv_ref[0] = zs(N_BV, GROUP_WIDTH).astype(BF16)
    glat = zs(N_BG, LANE).astype(BF16)
    gate = _dot(glat, wgu_ref[...]) + bgb_ref[...]
    bla_ref[0] = (jnp.minimum(gate, 0.0) - jnp.log(1.0 + jnp.exp(-jnp.abs(gate)))) * (1.0 / B_GATE_TAU)
    br_ref[0] = zs(N_BR, GROUP_WIDTH)

    ca = zs(N_CU, C_CHANNELS)
    cg = zs(N_CU + C_CHANNELS, C_CHANNELS)
    ch_ref[0] = ca * jax.nn.sigmoid(cg)

    cq = zs(N_DCQ, D_Q_RANK)
    cq_ms = jnp.mean(cq * cq, axis=-1, keepdims=True)
    cqn = (cq * lax.rsqrt(cq_ms + EPS) * gqa_ref[...]).astype(BF16)
    dq = _dot_nt(wuq_ref[...], cqn).reshape(D_HEADS, HEAD_PAD, tm)
    dq_ms = jnp.sum(dq * dq, axis=1, keepdims=True) * (1.0 / D_QK)
    dq = dq * lax.rsqrt(dq_ms + EPS) * lanes(gdq_ref[...])[None] * (D_QK ** -0.5)
    half = D_ROPE // 2
    x1 = dq[:, D_NOPE:D_NOPE + half]
    x2 = dq[:, D_NOPE + half:D_QK]
    cs = cosT_ref[...][None]
    sn = sinT_ref[...][None]
    dq = jnp.concatenate([dq[:, :D_NOPE], x1 * cs - x2 * sn, x2 * cs + x1 * sn, dq[:, D_QK:]], axis=1)
    dqT_ref[0] = dq.reshape(D_HEADS * HEAD_PAD, tm).astype(BF16)

    ckv = zs(N_DCKV, D_KV_RANK)
    ckv_ms = jnp.mean(ckv * ckv, axis=-1, keepdims=True)
    ckvn = (ckv * lax.rsqrt(ckv_ms + EPS) * gkva_ref[...]).astype(BF16)
    kn = _dot(ckvn, wuk_ref[...])
    dv = _dot_nt(wuvT_ref[...], ckvn).astype(BF16)
    for c in range(tm // blk):
        dvT_ref[0, c] = dv[:, c * blk:(c + 1) * blk]
    kpe = zs(N_DKPE, LANE)
    ss = _group_sum(kn * kn, bd) + jnp.sum(kpe * kpe, axis=-1, keepdims=True)
    rinv = lax.rsqrt(ss * (1.0 / D_QK) + EPS)
    kn = kn * rinv * gdk_ref[...]
    pe = kpe * gdkpe_ref[...]
    partner = jnp.where(lane < half, pltpu.roll(pe, LANE - half, 1), pltpu.roll(pe, half, 1))
    pe = pe * cpe_ref[...] + partner * spe_ref[...]
    pe = pltpu.roll(pe, D_NOPE, 1)
    for h in range(D_HEADS):
        pair = kn[:, LANE * (h // 2):LANE * (h // 2) + LANE]
        rpair = rinv[:, LANE * (h // 2):LANE * (h // 2) + LANE]
        if h % 2 == 1:
            pair = pltpu.roll(pair, 64, 1)
        else:
            rpair = pltpu.roll(rpair, 64, 1)
        dk_ref[0, h] = jnp.where(lane < D_NOPE, pair, pe * rpair).astype(BF16)


def _mix_in(x3, p, tm=512, blk=ATT_BLK):
    b, l, d = x3.shape
    tm = min(tm, l)
    grid = (b, l // tm)
    nck = l // blk
    tok = lambda w: pl.BlockSpec((1, tm, w), lambda bi, i: (bi, i, 0))
    tokT = lambda r: pl.BlockSpec((1, r, tm), lambda bi, i: (bi, 0, i))
    headk = pl.BlockSpec((1, 4, tm, HEAD_PAD), lambda bi, i: (bi, 0, i, 0))
    chunkT = pl.BlockSpec((1, tm // blk, GROUP_WIDTH, blk), lambda bi, i: (bi, i, 0, 0))
    postab = lambda r: pl.BlockSpec((r, tm), lambda bi, i: (0, i))
    posrow = pl.BlockSpec((tm, LANE), lambda bi, i: (i, 0))
    consts = [p["gmix"], p["wn"], p["wt"], p["bd"], p["gaq"], p["gak"], p["wgu"], p["bgb"],
              p["gqa"], p["wuq"], p["gdq"]]
    consts2 = [p["gkva"], p["wuk"], p["wuvT"], p["gdk"], p["gdkpe"]]
    in_specs = ([tok(d)] + [_const_spec(a.shape) for a in consts]
                + [postab(D_ROPE // 2), postab(D_ROPE // 2)]
                + [_const_spec(a.shape) for a in consts2] + [posrow, posrow])
    args = [x3] + consts + [p["cosT"], p["sinT"]] + consts2 + [p["cpe"], p["spe"]]
    sd = jax.ShapeDtypeStruct
    out_shape = [
        sd((b, A_HEADS * HEAD_PAD, l), BF16), sd((b, A_HEADS, l, HEAD_PAD), BF16),
        sd((b, nck, GROUP_WIDTH, blk), BF16), sd((b, IDX_HEADS * IDX_DIM, l), BF16),
        sd((b, l, IDX_DIM), BF16), sd((b, IDX_HEADS, l), F32),
        sd((b, l, LANE), F32), sd((b, l, LANE), F32), sd((b, l, GROUP_WIDTH), BF16),
        sd((b, l, LANE), F32), sd((b, l, GROUP_WIDTH), F32),
        sd((b, l, C_CHANNELS), F32),
        sd((b, D_HEADS * HEAD_PAD, l), BF16), sd((b, D_HEADS, l, HEAD_PAD), BF16),
        sd((b, nck, GROUP_WIDTH, blk), BF16),
    ]
    out_specs = [
        tokT(A_HEADS * HEAD_PAD), headk, chunkT, tokT(IDX_HEADS * IDX_DIM),
        tok(IDX_DIM), tokT(IDX_HEADS),
        tok(LANE), tok(LANE), tok(GROUP_WIDTH), tok(LANE), tok(GROUP_WIDTH),
        tok(C_CHANNELS),
        tokT(D_HEADS * HEAD_PAD), headk, chunkT,
    ]
    return pl.pallas_call(
        functools.partial(_mix_in_body, tm, blk),
        grid=grid, in_specs=in_specs, out_specs=out_specs, out_shape=out_shape,
        compiler_params=_cparams(("parallel", "parallel")),
        name="mix_in",
    )(*args)


def _attn_body(nh, dv, blk, topk, is_dsa, *refs):
    if is_dsa:
        (qT_ref, k_ref, vT_ref, iqT_ref, wT_ref, ik_ref, d0_ref, d1_ref, bfar_ref,
         o_ref, m_scr, l_scr, acc_scr, key_scr, run_scr) = refs
    else:
        qT_ref, k_ref, vT_ref, o_ref, m_scr, l_scr, acc_scr = refs
    i = pl.program_id(1)
    t = blk
    row = lax.broadcasted_iota(I32, (t, t), 0)
    col = lax.broadcasted_iota(I32, (t, t), 1)
    causal_pen = jnp.where(row <= col, 0.0, NEG)

    m_scr[...] = jnp.full(m_scr.shape, NEG, F32)
    l_scr[...] = jnp.zeros(l_scr.shape, F32)
    acc_scr[...] = jnp.zeros(acc_scr.shape, F32)

    if is_dsa:
        def score_chunk(j, carry):
            r0 = pl.multiple_of(j * t, t)
            ikc = ik_ref[0, pl.ds(r0, t), :]
            s = jnp.zeros((t, t), F32)
            for h in range(IDX_HEADS):
                d = _dot(ikc, iqT_ref[0, IDX_DIM * h:IDX_DIM * (h + 1), :])
                s = s + jnp.maximum(d, 0.0) * wT_ref[0, h:h + 1, :]
            bits = lax.bitcast_convert_type(s, I32)
            key = jnp.where(bits < 0, bits ^ 0x7FFFFFFF, bits)
            key = jnp.where(row + j * t <= col + i * t, key, INT_MIN)
            key_scr[pl.ds(r0, t), :] = key
            return carry

        lax.fori_loop(0, i + 1, score_chunk, 0)

        def count_ge(cand):
            def body(j, acc):
                kc = key_scr[pl.ds(pl.multiple_of(j * t, t), t), :]
                sel = jnp.where(kc >= cand, 1, 0)
                return acc + jnp.sum(sel.reshape(t // 8, 8, t), axis=0)
            acc = lax.fori_loop(0, i + 1, body, jnp.zeros((8, t), I32))
            return jnp.sum(acc, axis=0, keepdims=True)

        ans = jnp.where(count_ge(jnp.zeros((1, t), I32)) >= topk, 0, INT_MIN)

        def bit_body(b, ans):
            cand = ans | jnp.left_shift(jnp.int32(1), 30 - b)
            return jnp.where(count_ge(cand) >= topk, cand, ans)

        ans = lax.fori_loop(0, 31, bit_body, ans)
        need = (topk - count_ge(ans + 1)).astype(F32)
        run_scr[...] = jnp.zeros(run_scr.shape, F32)
        stri = jnp.where(col < row, 1.0, 0.0).astype(BF16)

    def chunk(j, kind):
        r0 = pl.multiple_of(j * t, t)
        if is_dsa:
            kc = key_scr[pl.ds(r0, t), :]
            eq = kc == ans
            eqf = jnp.where(eq, 1.0, 0.0)
            run = run_scr[0:1, :]
            rank = _dot(stri, eqf.astype(BF16)) + run
            run_scr[0:1, :] = run + jnp.sum(eqf, axis=0, keepdims=True)
            pen = jnp.where(kc > ans, 0.0, jnp.where(eq, jnp.where(rank < need, 0.0, NEG), NEG))
            if kind == "diag":
                pen = pen + causal_pen
        else:
            pen = causal_pen if kind == "diag" else None
        for h in range(nh):
            kh = k_ref[0, h, pl.ds(r0, t), :]
            lg = _dot(kh, qT_ref[0, HEAD_PAD * h:HEAD_PAD * (h + 1), :])
            if is_dsa:
                if kind == "far":
                    lg = lg + (pen + bfar_ref[h])
                elif kind == "near":
                    lg = lg + (pen + d1_ref[h])
                else:
                    lg = lg + (pen + d0_ref[h])
            elif pen is not None:
                lg = lg + pen
            m_old = m_scr[h, 0:1, :]
            m_new = jnp.maximum(m_old, jnp.max(lg, axis=0, keepdims=True))
            alpha = jnp.exp(m_old - m_new)
            p = jnp.exp(lg - m_new)
            l_scr[h, 0:1, :] = alpha * l_scr[h, 0:1, :] + jnp.sum(p, axis=0, keepdims=True)
            vs = slice(dv * h, dv * (h + 1))
            acc_scr[vs, :] = alpha * acc_scr[vs, :] + _dot(vT_ref[0, j, vs, :], p.astype(BF16))
            m_scr[h, 0:1, :] = m_new

    def far_body(j, carry):
        chunk(j, "far")
        return carry

    if is_dsa:
        lax.fori_loop(0, jnp.maximum(i - 1, 0), far_body, 0)

        @pl.when(i >= 1)
        def _():
            chunk(i - 1, "near")
    else:
        lax.fori_loop(0, i, far_body, 0)
    chunk(i, "diag")

    outs = []
    for h in range(nh):
        outs.append(acc_scr[dv * h:dv * (h + 1), :] / l_scr[h, 0:1, :])
    o_ref[0] = jnp.transpose(jnp.concatenate(outs, axis=0)).astype(o_ref.dtype)


def _attention(qT, k, vT, dsa=None, blk=ATT_BLK):
    b, nh, l, _ = k.shape
    dv = vT.shape[2] // nh
    grid = (b, l // blk)
    qspec = lambda r: pl.BlockSpec((1, r, blk), lambda bi, i: (bi, 0, i))
    kspec = pl.BlockSpec((1, nh, l, HEAD_PAD), lambda bi, i: (bi, 0, 0, 0))
    vspec = pl.BlockSpec((1, l // blk, nh * dv, blk), lambda bi, i: (bi, 0, 0, 0))
    in_specs = [qspec(nh * HEAD_PAD), kspec, vspec]
    args = [qT, k, vT]
    scratch = [pltpu.VMEM((nh, 8, blk), F32), pltpu.VMEM((nh, 8, blk), F32),
               pltpu.VMEM((nh * dv, blk), F32)]
    topk = 0
    if dsa is not None:
        iqT, wT, ik, d0, d1, bfar = dsa
        topk = min(TOPK_MAX, l // 4)
        in_specs += [qspec(IDX_HEADS * IDX_DIM), qspec(IDX_HEADS),
                     pl.BlockSpec((1, l, IDX_DIM), lambda bi, i: (bi, 0, 0)),
                     _const_spec(d0.shape), _const_spec(d1.shape),
                     pl.BlockSpec(memory_space=pltpu.SMEM)]
        args += [iqT, wT, ik, d0, d1, bfar]
        scratch += [pltpu.VMEM((l, blk), I32), pltpu.VMEM((8, blk), F32)]
    return pl.pallas_call(
        functools.partial(_attn_body, nh, dv, blk, topk, dsa is not None),
        grid=grid, in_specs=in_specs,
        out_specs=pl.BlockSpec((1, blk, nh * dv), lambda bi, i: (bi, i, 0)),
        out_shape=jax.ShapeDtypeStruct((b, l, nh * dv), BF16),
        scratch_shapes=scratch,
        compiler_params=_cparams(("parallel", "arbitrary")),
        name="dsa_attn" if dsa is not None else "mla_attn",
    )(*args)


def _gla_body(tg, q_ref, k_ref, v_ref, la_ref, r_ref, go_ref, bd_ref, o_ref, st_scr, o_scr):
    @pl.when(pl.program_id(1) == 0)
    def _():
        st_scr[...] = jnp.zeros(st_scr.shape, F32)

    cs = B_CHUNK
    la = la_ref[0]
    rl = lax.broadcasted_iota(I32, (tg, LANE), 0) & (cs - 1)
    b = la
    s = 1
    while s < cs:
        b = b + jnp.where(rl >= s, pltpu.roll(b, s, 0), 0.0)
        s *= 2
    q = q_ref[0]
    k = k_ref[0]
    qb = q * jnp.exp(b)
    ci = lax.broadcasted_iota(I32, (cs, cs), 0)
    cj = lax.broadcasted_iota(I32, (cs, cs), 1)
    for c in range(tg // cs):
        sl = slice(c * cs, (c + 1) * cs)
        bc = b[sl]
        mid = bc[cs // 2:cs // 2 + 1]
        last = bc[cs - 1:cs]
        qe = (q[sl] * jnp.exp(bc - mid)).astype(BF16)
        ke = (k[sl] * jnp.exp(mid - bc)).astype(BF16)
        kd = (k[sl] * jnp.exp(last - bc)).astype(BF16)
        qbc = qb[sl].astype(BF16)
        dl = jnp.exp(last)
        vc = v_ref[0, sl, :]
        for h in range(B_HEADS):
            ks = slice(B_KEY_DIM * h, B_KEY_DIM * (h + 1))
            vs = slice(B_VAL_DIM * h, B_VAL_DIM * (h + 1))
            a = jnp.where(cj <= ci, _dot_nt(qe[:, ks], ke[:, ks]), 0.0)
            st = st_scr[h]
            o_scr[sl, vs] = _dot(a.astype(BF16), vc[:, vs]) + _dot_nt(qbc[:, ks], st.astype(BF16))
            st_scr[h] = st * dl[:, ks] + _dot_tn(vc[:, vs], kd[:, ks])
    o = o_scr[...]
    ms = _group_sum(o * o, bd_ref[...]) * (1.0 / B_VAL_DIM)
    r = r_ref[0]
    o_ref[0] = (o * lax.rsqrt(ms + EPS) * go_ref[...] * (r * jax.nn.sigmoid(r))).astype(o_ref.dtype)


def _gla(bq, bk, bv, bla, br, go, bd, tg=512):
    b, l, _ = bq.shape
    tg = min(tg, l)
    tok = lambda w: pl.BlockSpec((1, tg, w), lambda bi, i: (bi, i, 0))
    return pl.pallas_call(
        functools.partial(_gla_body, tg),
        grid=(b, l // tg),
        in_specs=[tok(LANE), tok(LANE), tok(GROUP_WIDTH), tok(LANE), tok(GROUP_WIDTH),
                  _const_spec(go.shape), _const_spec(bd.shape)],
        out_specs=tok(GROUP_WIDTH),
        out_shape=jax.ShapeDtypeStruct((b, l, GROUP_WIDTH), BF16),
        scratch_shapes=[pltpu.VMEM((B_HEADS, B_VAL_DIM, B_KEY_DIM), F32),
                        pltpu.VMEM((tg, GROUP_WIDTH), F32)],
        compiler_params=_cparams(("parallel", "arbitrary")),
        name="gla",
    )(bq, bk, bv, bla, br, go, bd)


CONV_HIST = 32


def _conv_body(tc, h_ref, w_ref, b_ref, g_ref, o_ref, buf):
    @pl.when(pl.program_id(1) == 0)
    def _():
        buf[0:CONV_HIST, :] = jnp.zeros((CONV_HIST, C_CHANNELS), F32)

    @pl.when(pl.program_id(1) > 0)
    def _():
        buf[0:CONV_HIST, :] = buf[tc:tc + CONV_HIST, :]

    buf[CONV_HIST:CONV_HIST + tc, :] = h_ref[0]
    acc = jnp.zeros((tc, C_CHANNELS), F32) + b_ref[...]
    base = CONV_HIST - (C_KERNEL - 1)
    for j in range(C_KERNEL):
        acc = acc + buf[base + j:base + j + tc, :] * w_ref[j:j + 1, :]
    ms = jnp.mean(acc * acc, axis=-1, keepdims=True)
    y = acc * lax.rsqrt(ms + EPS) * g_ref[...]
    o_ref[0] = (y * jax.nn.sigmoid(y)).astype(o_ref.dtype)


def _conv(ch, w, bias, g, tc=512):
    b, l, c = ch.shape
    tc = min(tc, l)
    tok = pl.BlockSpec((1, tc, c), lambda bi, i: (bi, i, 0))
    return pl.pallas_call(
        functools.partial(_conv_body, tc),
        grid=(b, l // tc),
        in_specs=[tok, _const_spec(w.shape), _const_spec(bias.shape), _const_spec(g.shape)],
        out_specs=tok,
        out_shape=jax.ShapeDtypeStruct((b, l, c), BF16),
        scratch_shapes=[pltpu.VMEM((tc + CONV_HIST, c), F32)],
        compiler_params=_cparams(("parallel", "arbitrary")),
        name="conv",
    )(ch, w, bias, g)


def _t5_bucket(dist):
    max_exact = REL_BUCKETS // 2
    d = jnp.maximum(dist, 0)
    df = jnp.maximum(d, 1).astype(F32)
    large = max_exact + (jnp.log(df / max_exact) / math.log(REL_MAX_DIST / max_exact)
                         * (REL_BUCKETS - max_exact)).astype(I32)
    large = jnp.minimum(large, REL_BUCKETS - 1)
    return jnp.where(d < max_exact, d, large)


def _pad_cols(w, width):
    return jnp.pad(w, ((0, 0), (0, width - w.shape[1])))


def _lane_rep(v):
    return jnp.broadcast_to(v[:, None], (v.shape[0], LANE))


def _pad_heads_rows(w, heads, dim):
    w = w.reshape(heads, dim, w.shape[1])
    return jnp.pad(w, ((0, 0), (0, HEAD_PAD - dim), (0, 0))).reshape(heads * HEAD_PAD, -1)


def _split_w_in(w_in):
    widths = (256, 256, 256, 256, 32, 8, 128, 128, 256, 16, 256, 512, 256, 128, 32)
    offs = np.cumsum((0,) + widths)
    return [w_in[:, offs[n]:offs[n + 1]] for n in range(len(widths))]


def _layer_params(l, seq, blk, w):
    (aq, ak, av, iq, ik, iw, bq, bk, bv, bg, br, cu, dcq, dckv, dkpe) = _split_w_in(w["w_in"][l])
    wn = jnp.concatenate([ak, _pad_cols(ik, LANE), bq, bk, bv, _pad_cols(bg, LANE), br, cu, dcq, dckv,
                          _pad_cols(dkpe, LANE)], axis=1).astype(BF16)
    wt = jnp.concatenate([_pad_heads_rows(aq.T, A_HEADS, A_HEAD_DIM), av.T, iq.T,
                          jnp.pad(iw.T, ((0, T_TOT - T_IW - IDX_HEADS), (0, 0)))], axis=0).astype(BF16)
    hid = np.arange(GROUP_WIDTH) // 64
    bd = jnp.asarray(hid[:, None] == hid[None, :], dtype=BF16)
    row = lambda v: v[None, :].astype(F32)
    pad_to = lambda v, n: jnp.pad(v, (0, n - v.shape[0]))
    half = D_ROPE // 2
    freqs = ROPE_THETA ** (-jnp.arange(half, dtype=F32) / half)
    ang = jnp.arange(seq).astype(F32)[:, None] * freqs[None, :]
    cos, sin = jnp.cos(ang), jnp.sin(ang)
    zeros = jnp.zeros((seq, LANE - D_ROPE), F32)
    cpe = jnp.concatenate([cos, cos, zeros], axis=1)
    spe = jnp.concatenate([-sin, sin, zeros], axis=1)
    ukv = w["d_ukv"][l].reshape(D_KV_RANK, D_HEADS, D_NOPE + D_V)
    wuk = ukv[:, :, :D_NOPE].reshape(D_KV_RANK, D_HEADS * D_NOPE)
    wuv = ukv[:, :, D_NOPE:].reshape(D_KV_RANK, D_HEADS * D_V)
    gdk = w["d_k_norm"][l]
    p = dict(
        gmix=row(w["mix_norm"][l]), wn=wn, wt=wt, bd=bd,
        gaq=_lane_rep(pad_to(w["a_q_norm"][l], HEAD_PAD)),
        gak=row(jnp.tile(w["a_k_norm"][l], A_HEADS)),
        wgu=jnp.pad(w["b_gate_up"][l], ((0, LANE - B_GATE_RANK), (0, 0))).astype(BF16),
        bgb=row(w["b_gate_bias"][l]),
        gqa=row(w["d_qa_norm"][l]),
        wuq=_pad_heads_rows(w["d_uq"][l].T, D_HEADS, D_QK).astype(BF16),
        gdq=_lane_rep(pad_to(w["d_q_norm"][l], HEAD_PAD)),
        cosT=cos.T, sinT=sin.T,
        gkva=row(w["d_kva_norm"][l]), wuk=wuk.astype(BF16), wuvT=wuv.T.astype(BF16),
        gdk=row(jnp.tile(gdk[:D_NOPE], D_HEADS)), gdkpe=row(pad_to(gdk[D_NOPE:], LANE)),
        cpe=cpe, spe=spe,
    )
    return p


def _bias_tiles(rel_bias, blk):
    assert REL_MAX_DIST <= blk + 1
    kk = jnp.arange(blk)[:, None]
    qq = jnp.arange(blk)[None, :]
    d0 = jnp.moveaxis(rel_bias[_t5_bucket(qq - kk)], -1, 0).astype(F32)
    d1 = jnp.moveaxis(rel_bias[_t5_bucket(blk + qq - kk)], -1, 0).astype(F32)
    bfar = rel_bias[_t5_bucket(jnp.int32(2 * blk))].astype(F32)
    return d0, d1, bfar


def kernel(x, ffn1_norm, ffn1_gate, ffn1_up, ffn1_down, mix_norm, w_in, a_q_norm, a_k_norm, rel_bias,
           b_gate_up, b_gate_bias, b_out_norm, c_dw_w, c_dw_b, c_norm, d_qa_norm, d_uq, d_kva_norm,
           d_ukv, d_q_norm, d_k_norm, w_out, ffn2_norm, ffn2_gate, ffn2_up, ffn2_down):
    w = dict(mix_norm=mix_norm, w_in=w_in, a_q_norm=a_q_norm, a_k_norm=a_k_norm, b_gate_up=b_gate_up,
             b_gate_bias=b_gate_bias, d_qa_norm=d_qa_norm, d_uq=d_uq, d_kva_norm=d_kva_norm, d_ukv=d_ukv,
             d_q_norm=d_q_norm, d_k_norm=d_k_norm)
    bsz, seq, dm = x.shape
    depth = w_in.shape[0]
    blk = min(ATT_BLK, seq)
    d0, d1, bfar = _bias_tiles(rel_bias, blk)
    row = lambda v: v[None, :].astype(F32)
    x2 = x.reshape(bsz * seq, dm)
    for l in range(depth):
        x2 = _ffn(x2, row(ffn1_norm[l]), ffn1_gate[l].astype(BF16), ffn1_up[l].astype(BF16),
                  ffn1_down[l].astype(BF16))
        p = _layer_params(l, seq, blk, w)
        (aqT, akh, avT, iqT, aik, iwT, bq, bk, bv, bla, br, ch, dqT, dkh, dvT) = _mix_in(
            x2.reshape(bsz, seq, dm), p, blk=blk)
        y_a = _attention(aqT, akh, avT, dsa=(iqT, iwT, aik, d0, d1, bfar), blk=blk)
        y_b = _gla(bq, bk, bv, bla, br, row(jnp.tile(b_out_norm[l], B_HEADS)), p["bd"])
        cw = jnp.pad(c_dw_w[l][:, 0, :], ((0, CONV_HIST - C_KERNEL), (0, 0))).astype(F32)
        y_c = _conv(ch, cw, row(c_dw_b[l]), row(c_norm[l]))
        y_d = _attention(dqT, dkh, dvT, blk=blk)
        ys = [y.reshape(bsz * seq, GROUP_WIDTH) for y in (y_a, y_b, y_c, y_d)]
        x2 = _ffn(x2, row(ffn2_norm[l]), ffn2_gate[l].astype(BF16), ffn2_up[l].astype(BF16),
                  ffn2_down[l].astype(BF16), mix=(ys, w_out[l].astype(BF16)))
    return x2.reshape(bsz, seq, dm)
```

```python
import functools
import math

import jax
import jax.numpy as jnp
import numpy as np
from jax import lax
from jax.experimental import pallas as pl
from jax.experimental.pallas import tpu as pltpu

F32 = jnp.float32
BF16 = jnp.bfloat16
I32 = jnp.int32

EPS = 1e-6
GROUP_WIDTH = 256
A_HEADS, A_HEAD_DIM = 4, 64
IDX_HEADS, IDX_DIM = 8, 32
TOPK_MAX = 256
REL_BUCKETS, REL_MAX_DIST = 32, 128
B_HEADS, B_KEY_DIM, B_VAL_DIM, B_GATE_RANK = 4, 32, 64, 16
B_GATE_TAU = 16.0
B_CHUNK = 64
C_CHANNELS, C_KERNEL = 256, 31
D_HEADS, D_Q_RANK, D_KV_RANK, D_NOPE, D_ROPE, D_V = 4, 256, 128, 64, 32, 64
D_QK = D_NOPE + D_ROPE
ROPE_THETA = 10000.0

LANE = 128
HEAD_PAD = 128
ATT_BLK = 256
INT_MIN = -2 ** 31
NEG = -1e30
VMEM_LIMIT = 56 * 1024 * 1024

N_AK, N_IK, N_BQ, N_BK, N_BV, N_BG, N_BR, N_CU, N_DCQ, N_DCKV, N_DKPE, N_TOT = (
    0, 256, 384, 512, 640, 896, 1024, 1280, 1792, 2048, 2176, 2304)
T_AQ, T_AV, T_IQ, T_IW, T_TOT = 0, 512, 768, 1024, 1040


def _dot(a, b):
    return jnp.dot(a, b, preferred_element_type=F32)


def _dot_nt(a, b):
    return lax.dot_general(a, b, (((1,), (1,)), ((), ())), preferred_element_type=F32)


def _dot_tn(a, b):
    return lax.dot_general(a, b, (((0,), (0,)), ((), ())), preferred_element_type=F32)


def _group_sum(x2, bd):
    hi = x2.astype(BF16)
    lo = (x2 - hi.astype(F32)).astype(BF16)
    return _dot(hi, bd) + _dot(lo, bd)


def _const_spec(shape):
    nd = len(shape)
    return pl.BlockSpec(shape, lambda *_: (0,) * nd, pipeline_mode=pl.Buffered(1))


def _cparams(sem):
    return pltpu.CompilerParams(dimension_semantics=sem, vmem_limit_bytes=VMEM_LIMIT)


def _ffn_body(has_mix, fc, *refs):
    if has_mix:
        x_ref, ya, yb, yc, yd, wo_ref, g_ref, wg_ref, wu_ref, wd_ref, o_ref, h_scr = refs
    else:
        x_ref, g_ref, wg_ref, wu_ref, wd_ref, o_ref, h_scr = refs
    x = x_ref[...]
    if has_mix:
        y = jnp.concatenate([ya[...], yb[...], yc[...], yd[...]], axis=-1)
        x = x + _dot(y, wo_ref[...])
    ms = jnp.mean(x * x, axis=-1, keepdims=True)
    xn = (x * lax.rsqrt(ms + EPS) * g_ref[...]).astype(BF16)
    d_ff = wg_ref.shape[1]
    for c in range(d_ff // fc):
        sl = slice(c * fc, (c + 1) * fc)
        gate = _dot(xn, wg_ref[:, sl])
        up = _dot(xn, wu_ref[:, sl])
        h_scr[:, sl] = (gate * jax.nn.sigmoid(gate) * up).astype(BF16)
    o_ref[...] = x + 0.5 * _dot(h_scr[...], wd_ref[...])


def _ffn(x2, g, wg, wu, wd, mix=None, tm=512, fc=256):
    m, d = x2.shape
    d_ff = wg.shape[1]
    tm = min(tm, m)
    row = lambda w: pl.BlockSpec((tm, w), lambda i: (i, 0))
    in_specs = [row(d)]
    args = [x2]
    if mix is not None:
        ys, wo = mix
        in_specs += [row(GROUP_WIDTH)] * 4 + [_const_spec(wo.shape)]
        args += list(ys) + [wo]
    in_specs += [_const_spec((1, d)), _const_spec(wg.shape), _const_spec(wu.shape), _const_spec(wd.shape)]
    args += [g, wg, wu, wd]
    return pl.pallas_call(
        functools.partial(_ffn_body, mix is not None, fc),
        grid=(m // tm,),
        in_specs=in_specs,
        out_specs=row(d),
        out_shape=jax.ShapeDtypeStruct((m, d), F32),
        scratch_shapes=[pltpu.VMEM((tm, d_ff), BF16)],
        compiler_params=_cparams(("parallel",)),
        name="ffn_mix" if mix is not None else "ffn",
    )(*args)


def _mix_in_body(tm, blk,
                 x_ref, gmix_ref, wn_ref, wt_ref, bd_ref,
                 gaq_ref, gak_ref,
                 wgu_ref, bgb_ref,
                 gqa_ref, wuq_ref, gdq_ref, cosT_ref, sinT_ref,
                 gkva_ref, wuk_ref, wuvT_ref, gdk_ref, gdkpe_ref, cpe_ref, spe_ref,
                 aqT_ref, ak_ref, avT_ref, iqT_ref, ik_ref, iwT_ref,
                 bq_ref, bk_ref, bv_ref, bla_ref, br_ref,
                 ch_ref,
                 dqT_ref, dk_ref, dvT_ref):
    nlt = tm // LANE
    x = x_ref[0]
    ms = jnp.mean(x * x, axis=-1, keepdims=True)
    xn = (x * lax.rsqrt(ms + EPS) * gmix_ref[...]).astype(BF16)
    bd = bd_ref[...]
    lane = lax.broadcasted_iota(I32, (tm, LANE), 1)

    def zs(off, width):
        return _dot(xn, wn_ref[:, off:off + width])

    def lanes(g):
        return jnp.tile(g, (1, nlt))

    zt = _dot_nt(wt_ref[...], xn)

    aq = zt[T_AQ:T_AQ + A_HEADS * HEAD_PAD].reshape(A_HEADS, HEAD_PAD, tm)
    aq_ms = jnp.sum(aq * aq, axis=1, keepdims=True) * (1.0 / A_HEAD_DIM)
    aq = aq * lax.rsqrt(aq_ms + EPS) * lanes(gaq_ref[...])[None] * (A_HEAD_DIM ** -0.5)
    aqT_ref[0] = aq.reshape(A_HEADS * HEAD_PAD, tm).astype(BF16)
    av = zt[T_AV:T_AV + GROUP_WIDTH].astype(BF16)
    for c in range(tm // blk):
        avT_ref[0, c] = av[:, c * blk:(c + 1) * blk]
    iqT_ref[0] = zt[T_IQ:T_IQ + IDX_HEADS * IDX_DIM].astype(BF16)
    iwT_ref[0] = zt[T_IW:T_IW + IDX_HEADS] * ((IDX_HEADS ** -0.5) * (IDX_DIM ** -0.5))

    ak = zs(N_AK, GROUP_WIDTH)
    ak_ms = _group_sum(ak * ak, bd) * (1.0 / A_HEAD_DIM)
    ak = ak * lax.rsqrt(ak_ms + EPS) * gak_ref[...]
    for h in range(A_HEADS):
        pair = ak[:, LANE * (h // 2):LANE * (h // 2) + LANE]
        if h % 2 == 1:
            pair = pltpu.roll(pair, 64, 1)
        ak_ref[0, h] = jnp.where(lane < A_HEAD_DIM, pair, 0.0).astype(BF16)
    ik_ref[0] = zs(N_IK, LANE)[:, :IDX_DIM].astype(BF16)

    bq_ref[0] = zs(N_BQ, LANE) * (B_KEY_DIM ** -0.5)
    bk_ref[0] = zs(N_BK, LANE)
    bv_ref[0] = zs(N_BV, GROUP_WIDTH).astype(BF16)
    glat = zs(N_BG, LANE).astype(BF16)
    gate = _dot(glat, wgu_ref[...]) + bgb_ref[...]
    bla_ref[0] = (jnp.minimum(gate, 0.0) - jnp.log(1.0 + jnp.exp(-jnp.abs(gate)))) * (1.0 / B_GATE_TAU)
    br_ref[0] = zs(N_BR, GROUP_WIDTH)

    ca = zs(N_CU, C_CHANNELS)
    cg = zs(N_CU + C_CHANNELS, C_CHANNELS)
    ch_ref[0] = ca * jax.nn.sigmoid(cg)

    cq = zs(N_DCQ, D_Q_RANK)
    cq_ms = jnp.mean(cq * cq, axis=-1, keepdims=True)
    cqn = (cq * lax.rsqrt(cq_ms + EPS) * gqa_ref[...]).astype(BF16)
    dq = _dot_nt(wuq_ref[...], cqn).reshape(D_HEADS, HEAD_PAD, tm)
    dq_ms = jnp.sum(dq * dq, axis=1, keepdims=True) * (1.0 / D_QK)
    dq = dq * lax.rsqrt(dq_ms + EPS) * lanes(gdq_ref[...])[None] * (D_QK ** -0.5)
    half = D_ROPE // 2
    x1 = dq[:, D_NOPE:D_NOPE + half]
    x2 = dq[:, D_NOPE + half:D_QK]
    cs = cosT_ref[...][None]
    sn = sinT_ref[...][None]
    dq = jnp.concatenate([dq[:, :D_NOPE], x1 * cs - x2 * sn, x2 * cs + x1 * sn, dq[:, D_QK:]], axis=1)
    dqT_ref[0] = dq.reshape(D_HEADS * HEAD_PAD, tm).astype(BF16)

    ckv = zs(N_DCKV, D_KV_RANK)
    ckv_ms = jnp.mean(ckv * ckv, axis=-1, keepdims=True)
    ckvn = (ckv * lax.rsqrt(ckv_ms + EPS) * gkva_ref[...]).astype(BF16)
    kn = _dot(ckvn, wuk_ref[...])
    dv = _dot_nt(wuvT_ref[...], ckvn).astype(BF16)
    for c in range(tm // blk):
        dvT_ref[0, c] = dv[:, c * blk:(c + 1) * blk]
    kpe = zs(N_DKPE, LANE)
    ss = _group_sum(kn * kn, bd) + jnp.sum(kpe * kpe, axis=-1, keepdims=True)
    rinv = lax.rsqrt(ss * (1.0 / D_QK) + EPS)
    kn = kn * rinv * gdk_ref[...]
    pe = kpe * gdkpe_ref[...]
    partner = jnp.where(lane < half, pltpu.roll(pe, LANE - half, 1), pltpu.roll(pe, half, 1))
    pe = pe * cpe_ref[...] + partner * spe_ref[...]
    pe = pltpu.roll(pe, D_NOPE, 1)
    for h in range(D_HEADS):
        pair = kn[:, LANE * (h // 2):LANE * (h // 2) + LANE]
        rpair = rinv[:, LANE * (h // 2):LANE * (h // 2) + LANE]
        if h % 2 == 1:
            pair = pltpu.roll(pair, 64, 1)
        else:
            rpair = pltpu.roll(rpair, 64, 1)
        dk_ref[0, h] = jnp.where(lane < D_NOPE, pair, pe * rpair).astype(BF16)


def _mix_in(x3, p, tm=512, blk=ATT_BLK):
    b, l, d = x3.shape
    tm = min(tm, l)
    grid = (b, l // tm)
    nck = l // blk
    tok = lambda w: pl.BlockSpec((1, tm, w), lambda bi, i: (bi, i, 0))
    tokT = lambda r: pl.BlockSpec((1, r, tm), lambda bi, i: (bi, 0, i))
    headk = pl.BlockSpec((1, 4, tm, HEAD_PAD), lambda bi, i: (bi, 0, i, 0))
    chunkT = pl.BlockSpec((1, tm // blk, GROUP_WIDTH, blk), lambda bi, i: (bi, i, 0, 0))
    postab = lambda r: pl.BlockSpec((r, tm), lambda bi, i: (0, i))
    posrow = pl.BlockSpec((tm, LANE), lambda bi, i: (i, 0))
    consts = [p["gmix"], p["wn"], p["wt"], p["bd"], p["gaq"], p["gak"], p["wgu"], p["bgb"],
              p["gqa"], p["wuq"], p["gdq"]]
    consts2 = [p["gkva"], p["wuk"], p["wuvT"], p["gdk"], p["gdkpe"]]
    in_specs = ([tok(d)] + [_const_spec(a.shape) for a in consts]
                + [postab(D_ROPE // 2), postab(D_ROPE // 2)]
                + [_const_spec(a.shape) for a in consts2] + [posrow, posrow])
    args = [x3] + consts + [p["cosT"], p["sinT"]] + consts2 + [p["cpe"], p["spe"]]
    sd = jax.ShapeDtypeStruct
    out_shape = [
        sd((b, A_HEADS * HEAD_PAD, l), BF16), sd((b, A_HEADS, l, HEAD_PAD), BF16),
        sd((b, nck, GROUP_WIDTH, blk), BF16), sd((b, IDX_HEADS * IDX_DIM, l), BF16),
        sd((b, l, IDX_DIM), BF16), sd((b, IDX_HEADS, l), F32),
        sd((b, l, LANE), F32), sd((b, l, LANE), F32), sd((b, l, GROUP_WIDTH), BF16),
        sd((b, l, LANE), F32), sd((b, l, GROUP_WIDTH), F32),
        sd((b, l, C_CHANNELS), F32),
        sd((b, D_HEADS * HEAD_PAD, l), BF16), sd((b, D_HEADS, l, HEAD_PAD), BF16),
        sd((b, nck, GROUP_WIDTH, blk), BF16),
    ]
    out_specs = [
        tokT(A_HEADS * HEAD_PAD), headk, chunkT, tokT(IDX_HEADS * IDX_DIM),
        tok(IDX_DIM), tokT(IDX_HEADS),
        tok(LANE), tok(LANE), tok(GROUP_WIDTH), tok(LANE), tok(GROUP_WIDTH),
        tok(C_CHANNELS),
        tokT(D_HEADS * HEAD_PAD), headk, chunkT,
    ]
    return pl.pallas_call(
        functools.partial(_mix_in_body, tm, blk),
        grid=grid, in_specs=in_specs, out_specs=out_specs, out_shape=out_shape,
        compiler_params=_cparams(("parallel", "parallel")),
        name="mix_in",
    )(*args)


def _attn_body(nh, dv, blk, topk, is_dsa, *refs):
    if is_dsa:
        (qT_ref, k_ref, vT_ref, iqT_ref, wT_ref, ik_ref, bt_ref,
         o_ref, m_scr, l_scr, acc_scr, key_scr, run_scr) = refs
    else:
        qT_ref, k_ref, vT_ref, o_ref, m_scr, l_scr, acc_scr = refs
    i = pl.program_id(1)
    t = blk
    row = lax.broadcasted_iota(I32, (t, t), 0)
    col = lax.broadcasted_iota(I32, (t, t), 1)
    causal_pen = jnp.where(row <= col, 0.0, NEG)

    m_scr[...] = jnp.full(m_scr.shape, NEG, F32)
    l_scr[...] = jnp.zeros(l_scr.shape, F32)
    acc_scr[...] = jnp.zeros(acc_scr.shape, F32)

    if is_dsa:
        def score_chunk(j, carry):
            r0 = pl.multiple_of(j * t, t)
            ikc = ik_ref[0, pl.ds(r0, t), :]
            s = jnp.zeros((t, t), F32)
            for h in range(IDX_HEADS):
                d = _dot(ikc, iqT_ref[0, IDX_DIM * h:IDX_DIM * (h + 1), :])
                s = s + jnp.maximum(d, 0.0) * wT_ref[0, h:h + 1, :]
            bits = lax.bitcast_convert_type(s, I32)
            key = jnp.where(bits < 0, bits ^ 0x7FFFFFFF, bits)
            key = jnp.where(row + j * t <= col + i * t, key, INT_MIN)
            key_scr[pl.ds(r0, t), :] = key
            return carry

        lax.fori_loop(0, i + 1, score_chunk, 0)

        def count_ge(cand):
            def body(j, acc):
                kc = key_scr[pl.ds(pl.multiple_of(j * t, t), t), :]
                sel = jnp.where(kc >= cand, 1, 0)
                return acc + jnp.sum(sel.reshape(t // 8, 8, t), axis=0)
            acc = lax.fori_loop(0, i + 1, body, jnp.zeros((8, t), I32))
            return jnp.sum(acc, axis=0, keepdims=True)

        ans = jnp.where(count_ge(jnp.zeros((1, t), I32)) >= topk, 0, INT_MIN)

        def bit_body(b, ans):
            cand = ans | jnp.left_shift(jnp.int32(1), 30 - b)
            return jnp.where(count_ge(cand) >= topk, cand, ans)

        ans = lax.fori_loop(0, 31, bit_body, ans)
        need = (topk - count_ge(ans + 1)).astype(F32)
        run_scr[...] = jnp.zeros(run_scr.shape, F32)
        stri = jnp.where(col < row, 1.0, 0.0).astype(BF16)

    def super_chunk(js, last_is_diag):
        n = len(js)
        r0s = [pl.multiple_of(j * t, t) for j in js]
        pens = []
        if is_dsa:
            for c in range(n):
                kc = key_scr[pl.ds(r0s[c], t), :]
                eq = kc == ans
                eqf = jnp.where(eq, 1.0, 0.0)
                run = run_scr[0:1, :]
                rank = _dot(stri, eqf.astype(BF16)) + run
                run_scr[0:1, :] = run + jnp.sum(eqf, axis=0, keepdims=True)
                pens.append(jnp.where(kc > ans, 0.0, jnp.where(eq, jnp.where(rank < need, 0.0, NEG), NEG)))
        lgs = [[_dot(k_ref[0, h, pl.ds(r0s[c], t), :], qT_ref[0, HEAD_PAD * h:HEAD_PAD * (h + 1), :])
                for c in range(n)] for h in range(nh)]
        ps, alphas = [], []
        for h in range(nh):
            xs = []
            for c, j in enumerate(js):
                diag = last_is_diag and c == n - 1
                lg = lgs[h][c]
                if is_dsa:
                    tile = 0 if diag else jnp.minimum(i - j, 2)
                    lg = lg + (pens[c] + bt_ref[tile, h])
                elif diag:
                    lg = lg + causal_pen
                xs.append(lg)
            m_old = m_scr[h, 0:1, :]
            m_new = m_old
            for x in xs:
                m_new = jnp.maximum(m_new, jnp.max(x, axis=0, keepdims=True))
            alpha = jnp.exp(m_old - m_new)
            pf = [jnp.exp(x - m_new) for x in xs]
            psum = jnp.sum(pf[0], axis=0, keepdims=True)
            for p in pf[1:]:
                psum = psum + jnp.sum(p, axis=0, keepdims=True)
            l_scr[h, 0:1, :] = alpha * l_scr[h, 0:1, :] + psum
            m_scr[h, 0:1, :] = m_new
            ps.append([p.astype(BF16) for p in pf])
            alphas.append(alpha)
        for h in range(nh):
            vs = slice(dv * h, dv * (h + 1))
            pv = _dot(vT_ref[0, js[0], vs, :], ps[h][0])
            for c in range(1, n):
                pv = pv + _dot(vT_ref[0, js[c], vs, :], ps[h][c])
            acc_scr[vs, :] = alphas[h] * acc_scr[vs, :] + pv

    def pair_body(u, carry):
        super_chunk([2 * u, 2 * u + 1], False)
        return carry

    lax.fori_loop(0, i // 2, pair_body, 0)

    @pl.when(i % 2 == 1)
    def _():
        super_chunk([i - 1, i], True)

    @pl.when(i % 2 == 0)
    def _():
        super_chunk([i], True)

    outs = []
    for h in range(nh):
        outs.append(acc_scr[dv * h:dv * (h + 1), :] / l_scr[h, 0:1, :])
    o_ref[0] = jnp.transpose(jnp.concatenate(outs, axis=0)).astype(o_ref.dtype)


def _attention(qT, k, vT, dsa=None, blk=ATT_BLK):
    b, nh, l, _ = k.shape
    dv = vT.shape[2] // nh
    grid = (b, l // blk)
    qspec = lambda r: pl.BlockSpec((1, r, blk), lambda bi, i: (bi, 0, i))
    kspec = pl.BlockSpec((1, nh, l, HEAD_PAD), lambda bi, i: (bi, 0, 0, 0))
    vspec = pl.BlockSpec((1, l // blk, nh * dv, blk), lambda bi, i: (bi, 0, 0, 0))
    in_specs = [qspec(nh * HEAD_PAD), kspec, vspec]
    args = [qT, k, vT]
    scratch = [pltpu.VMEM((nh, 8, blk), F32), pltpu.VMEM((nh, 8, blk), F32),
               pltpu.VMEM((nh * dv, blk), F32)]
    topk = 0
    if dsa is not None:
        iqT, wT, ik, bt = dsa
        topk = min(TOPK_MAX, l // 4)
        in_specs += [qspec(IDX_HEADS * IDX_DIM), qspec(IDX_HEADS),
                     pl.BlockSpec((1, l, IDX_DIM), lambda bi, i: (bi, 0, 0)),
                     _const_spec(bt.shape)]
        args += [iqT, wT, ik, bt]
        scratch += [pltpu.VMEM((l, blk), I32), pltpu.VMEM((8, blk), F32)]
    return pl.pallas_call(
        functools.partial(_attn_body, nh, dv, blk, topk, dsa is not None),
        grid=grid, in_specs=in_specs,
        out_specs=pl.BlockSpec((1, blk, nh * dv), lambda bi, i: (bi, i, 0)),
        out_shape=jax.ShapeDtypeStruct((b, l, nh * dv), BF16),
        scratch_shapes=scratch,
        compiler_params=_cparams(("parallel", "arbitrary")),
        name="dsa_attn" if dsa is not None else "mla_attn",
    )(*args)


def _gla_body(tg, q_ref, k_ref, v_ref, la_ref, r_ref, go_ref, bd_ref, o_ref, st_scr, o_scr):
    @pl.when(pl.program_id(1) == 0)
    def _():
        st_scr[...] = jnp.zeros(st_scr.shape, F32)

    cs = B_CHUNK
    la = la_ref[0]
    rl = lax.broadcasted_iota(I32, (tg, LANE), 0) & (cs - 1)
    b = la
    s = 1
    while s < cs:
        b = b + jnp.where(rl >= s, pltpu.roll(b, s, 0), 0.0)
        s *= 2
    q = q_ref[0]
    k = k_ref[0]
    qb = q * jnp.exp(b)
    ci = lax.broadcasted_iota(I32, (cs, cs), 0)
    cj = lax.broadcasted_iota(I32, (cs, cs), 1)
    for c in range(tg // cs):
        sl = slice(c * cs, (c + 1) * cs)
        bc = b[sl]
        mid = bc[cs // 2:cs // 2 + 1]
        last = bc[cs - 1:cs]
        qe = (q[sl] * jnp.exp(bc - mid)).astype(BF16)
        ke = (k[sl] * jnp.exp(mid - bc)).astype(BF16)
        kd = (k[sl] * jnp.exp(last - bc)).astype(BF16)
        qbc = qb[sl].astype(BF16)
        dl = jnp.exp(last)
        vc = v_ref[0, sl, :]
        for h in range(B_HEADS):
            ks = slice(B_KEY_DIM * h, B_KEY_DIM * (h + 1))
            vs = slice(B_VAL_DIM * h, B_VAL_DIM * (h + 1))
            a = jnp.where(cj <= ci, _dot_nt(qe[:, ks], ke[:, ks]), 0.0)
            st = st_scr[h]
            o_scr[sl, vs] = _dot(a.astype(BF16), vc[:, vs]) + _dot_nt(qbc[:, ks], st.astype(BF16))
            st_scr[h] = st * dl[:, ks] + _dot_tn(vc[:, vs], kd[:, ks])
    o = o_scr[...]
    ms = _group_sum(o * o, bd_ref[...]) * (1.0 / B_VAL_DIM)
    r = r_ref[0]
    o_ref[0] = (o * lax.rsqrt(ms + EPS) * go_ref[...] * (r * jax.nn.sigmoid(r))).astype(o_ref.dtype)


def _gla(bq, bk, bv, bla, br, go, bd, tg=512):
    b, l, _ = bq.shape
    tg = min(tg, l)
    tok = lambda w: pl.BlockSpec((1, tg, w), lambda bi, i: (bi, i, 0))
    return pl.pallas_call(
        functools.partial(_gla_body, tg),
        grid=(b, l // tg),
        in_specs=[tok(LANE), tok(LANE), tok(GROUP_WIDTH), tok(LANE), tok(GROUP_WIDTH),
                  _const_spec(go.shape), _const_spec(bd.shape)],
        out_specs=tok(GROUP_WIDTH),
        out_shape=jax.ShapeDtypeStruct((b, l, GROUP_WIDTH), BF16),
        scratch_shapes=[pltpu.VMEM((B_HEADS, B_VAL_DIM, B_KEY_DIM), F32),
                        pltpu.VMEM((tg, GROUP_WIDTH), F32)],
        compiler_params=_cparams(("parallel", "arbitrary")),
        name="gla",
    )(bq, bk, bv, bla, br, go, bd)


CONV_HIST = 32


def _conv_body(tc, h_ref, w_ref, b_ref, g_ref, o_ref, buf):
    @pl.when(pl.program_id(1) == 0)
    def _():
        buf[0:CONV_HIST, :] = jnp.zeros((CONV_HIST, C_CHANNELS), F32)

    @pl.when(pl.program_id(1) > 0)
    def _():
        buf[0:CONV_HIST, :] = buf[tc:tc + CONV_HIST, :]

    buf[CONV_HIST:CONV_HIST + tc, :] = h_ref[0]
    acc = jnp.zeros((tc, C_CHANNELS), F32) + b_ref[...]
    base = CONV_HIST - (C_KERNEL - 1)
    for j in range(C_KERNEL):
        acc = acc + buf[base + j:base + j + tc, :] * w_ref[j:j + 1, :]
    ms = jnp.mean(acc * acc, axis=-1, keepdims=True)
    y = acc * lax.rsqrt(ms + EPS) * g_ref[...]
    o_ref[0] = (y * jax.nn.sigmoid(y)).astype(o_ref.dtype)


def _conv(ch, w, bias, g, tc=512):
    b, l, c = ch.shape
    tc = min(tc, l)
    tok = pl.BlockSpec((1, tc, c), lambda bi, i: (bi, i, 0))
    return pl.pallas_call(
        functools.partial(_conv_body, tc),
        grid=(b, l // tc),
        in_specs=[tok, _const_spec(w.shape), _const_spec(bias.shape), _const_spec(g.shape)],
        out_specs=tok,
        out_shape=jax.ShapeDtypeStruct((b, l, c), BF16),
        scratch_shapes=[pltpu.VMEM((tc + CONV_HIST, c), F32)],
        compiler_params=_cparams(("parallel", "arbitrary")),
        name="conv",
    )(ch, w, bias, g)


def _t5_bucket(dist):
    max_exact = REL_BUCKETS // 2
    d = jnp.maximum(dist, 0)
    df = jnp.maximum(d, 1).astype(F32)
    large = max_exact + (jnp.log(df / max_exact) / math.log(REL_MAX_DIST / max_exact)
                         * (REL_BUCKETS - max_exact)).astype(I32)
    large = jnp.minimum(large, REL_BUCKETS - 1)
    return jnp.where(d < max_exact, d, large)


def _pad_cols(w, width):
    return jnp.pad(w, ((0, 0), (0, width - w.shape[1])))


def _lane_rep(v):
    return jnp.broadcast_to(v[:, None], (v.shape[0], LANE))


def _pad_heads_rows(w, heads, dim):
    w = w.reshape(heads, dim, w.shape[1])
    return jnp.pad(w, ((0, 0), (0, HEAD_PAD - dim), (0, 0))).reshape(heads * HEAD_PAD, -1)


def _split_w_in(w_in):
    widths = (256, 256, 256, 256, 32, 8, 128, 128, 256, 16, 256, 512, 256, 128, 32)
    offs = np.cumsum((0,) + widths)
    return [w_in[:, offs[n]:offs[n + 1]] for n in range(len(widths))]


def _layer_params(l, seq, blk, w):
    (aq, ak, av, iq, ik, iw, bq, bk, bv, bg, br, cu, dcq, dckv, dkpe) = _split_w_in(w["w_in"][l])
    wn = jnp.concatenate([ak, _pad_cols(ik, LANE), bq, bk, bv, _pad_cols(bg, LANE), br, cu, dcq, dckv,
                          _pad_cols(dkpe, LANE)], axis=1).astype(BF16)
    wt = jnp.concatenate([_pad_heads_rows(aq.T, A_HEADS, A_HEAD_DIM), av.T, iq.T,
                          jnp.pad(iw.T, ((0, T_TOT - T_IW - IDX_HEADS), (0, 0)))], axis=0).astype(BF16)
    hid = np.arange(GROUP_WIDTH) // 64
    bd = jnp.asarray(hid[:, None] == hid[None, :], dtype=BF16)
    row = lambda v: v[None, :].astype(F32)
    pad_to = lambda v, n: jnp.pad(v, (0, n - v.shape[0]))
    half = D_ROPE // 2
    freqs = ROPE_THETA ** (-jnp.arange(half, dtype=F32) / half)
    ang = jnp.arange(seq).astype(F32)[:, None] * freqs[None, :]
    cos, sin = jnp.cos(ang), jnp.sin(ang)
    zeros = jnp.zeros((seq, LANE - D_ROPE), F32)
    cpe = jnp.concatenate([cos, cos, zeros], axis=1)
    spe = jnp.concatenate([-sin, sin, zeros], axis=1)
    ukv = w["d_ukv"][l].reshape(D_KV_RANK, D_HEADS, D_NOPE + D_V)
    wuk = ukv[:, :, :D_NOPE].reshape(D_KV_RANK, D_HEADS * D_NOPE)
    wuv = ukv[:, :, D_NOPE:].reshape(D_KV_RANK, D_HEADS * D_V)
    gdk = w["d_k_norm"][l]
    p = dict(
        gmix=row(w["mix_norm"][l]), wn=wn, wt=wt, bd=bd,
        gaq=_lane_rep(pad_to(w["a_q_norm"][l], HEAD_PAD)),
        gak=row(jnp.tile(w["a_k_norm"][l], A_HEADS)),
        wgu=jnp.pad(w["b_gate_up"][l], ((0, LANE - B_GATE_RANK), (0, 0))).astype(BF16),
        bgb=row(w["b_gate_bias"][l]),
        gqa=row(w["d_qa_norm"][l]),
        wuq=_pad_heads_rows(w["d_uq"][l].T, D_HEADS, D_QK).astype(BF16),
        gdq=_lane_rep(pad_to(w["d_q_norm"][l], HEAD_PAD)),
        cosT=cos.T, sinT=sin.T,
        gkva=row(w["d_kva_norm"][l]), wuk=wuk.astype(BF16), wuvT=wuv.T.astype(BF16),
        gdk=row(jnp.tile(gdk[:D_NOPE], D_HEADS)), gdkpe=row(pad_to(gdk[D_NOPE:], LANE)),
        cpe=cpe, spe=spe,
    )
    return p


def _bias_tiles(rel_bias, blk):
    assert REL_MAX_DIST <= blk + 1
    kk = jnp.arange(blk)[:, None]
    qq = jnp.arange(blk)[None, :]
    rb = rel_bias.astype(F32).T

    def lookup(bucket):
        onehot = bucket[None, :, :, None] == jnp.arange(REL_BUCKETS)
        return jnp.sum(jnp.where(onehot, rb[:, None, None, :], 0.0), axis=-1)

    d0 = jnp.where(kk <= qq, lookup(_t5_bucket(qq - kk)), NEG)
    d1 = lookup(_t5_bucket(blk + qq - kk))
    far = jnp.broadcast_to(lookup(_t5_bucket(jnp.full((1, 1), 2 * blk, I32))), d1.shape)
    return jnp.stack([d0, d1, far])


def kernel(x, ffn1_norm, ffn1_gate, ffn1_up, ffn1_down, mix_norm, w_in, a_q_norm, a_k_norm, rel_bias,
           b_gate_up, b_gate_bias, b_out_norm, c_dw_w, c_dw_b, c_norm, d_qa_norm, d_uq, d_kva_norm,
           d_ukv, d_q_norm, d_k_norm, w_out, ffn2_norm, ffn2_gate, ffn2_up, ffn2_down):
    w = dict(mix_norm=mix_norm, w_in=w_in, a_q_norm=a_q_norm, a_k_norm=a_k_norm, b_gate_up=b_gate_up,
             b_gate_bias=b_gate_bias, d_qa_norm=d_qa_norm, d_uq=d_uq, d_kva_norm=d_kva_norm, d_ukv=d_ukv,
             d_q_norm=d_q_norm, d_k_norm=d_k_norm)
    bsz, seq, dm = x.shape
    depth = w_in.shape[0]
    blk = min(ATT_BLK, seq)
    bt = _bias_tiles(rel_bias, blk)
    row = lambda v: v[None, :].astype(F32)
    x2 = x.reshape(bsz * seq, dm)
    for l in range(depth):
        x2 = _ffn(x2, row(ffn1_norm[l]), ffn1_gate[l].astype(BF16), ffn1_up[l].astype(BF16),
                  ffn1_down[l].astype(BF16))
        p = _layer_params(l, seq, blk, w)
        (aqT, akh, avT, iqT, aik, iwT, bq, bk, bv, bla, br, ch, dqT, dkh, dvT) = _mix_in(
            x2.reshape(bsz, seq, dm), p, blk=blk)
        y_a = _attention(aqT, akh, avT, dsa=(iqT, iwT, aik, bt), blk=blk)
        y_b = _gla(bq, bk, bv, bla, br, row(jnp.tile(b_out_norm[l], B_HEADS)), p["bd"])
        cw = jnp.pad(c_dw_w[l][:, 0, :], ((0, CONV_HIST - C_KERNEL), (0, 0))).astype(F32)
        y_c = _conv(ch, cw, row(c_dw_b[l]), row(c_norm[l]))
        y_d = _attention(dqT, dkh, dvT, blk=blk)
        ys = [y.reshape(bsz * seq, GROUP_WIDTH) for y in (y_a, y_b, y_c, y_d)]
        x2 = _ffn(x2, row(ffn2_norm[l]), ffn2_gate[l].astype(BF16), ffn2_up[l].astype(BF16),
                  ffn2_down[l].astype(BF16), mix=(ys, w_out[l].astype(BF16)))
    return x2.reshape(bsz, seq, dm)
```

```python
import functools
import math

import jax
import jax.numpy as jnp
import numpy as np
from jax import lax
from jax.experimental import pallas as pl
from jax.experimental.pallas import tpu as pltpu

F32 = jnp.float32
BF16 = jnp.bfloat16
I32 = jnp.int32
I16 = jnp.int16

EPS = 1e-6
GROUP_WIDTH = 256
A_HEADS, A_HEAD_DIM = 4, 64
IDX_HEADS, IDX_DIM = 8, 32
TOPK_MAX = 256
REL_BUCKETS, REL_MAX_DIST = 32, 128
B_HEADS, B_KEY_DIM, B_VAL_DIM, B_GATE_RANK = 4, 32, 64, 16
B_GATE_TAU = 16.0
B_CHUNK = 64
C_CHANNELS, C_KERNEL = 256, 31
D_HEADS, D_Q_RANK, D_KV_RANK, D_NOPE, D_ROPE, D_V = 4, 256, 128, 64, 32, 64
D_QK = D_NOPE + D_ROPE
ROPE_THETA = 10000.0

LANE = 128
HEAD_PAD = 128
ATT_BLK = 256
INT_MIN = -2 ** 31
I16_MIN = -2 ** 15
NEG = -1e30
LOG2E = math.log2(math.e)
VMEM_LIMIT = 56 * 1024 * 1024

N_AK, N_IK, N_BQ, N_BK, N_BV, N_BG, N_BR, N_CU, N_DCQ, N_DCKV, N_DKPE, N_TOT = (
    0, 256, 384, 512, 640, 896, 1024, 1280, 1792, 2048, 2176, 2304)
T_AQ, T_AV, T_IQ, T_IW, T_TOT = 0, 512, 768, 1024, 1040


def _dot(a, b):
    return jnp.dot(a, b, preferred_element_type=F32)


def _dot_nt(a, b):
    return lax.dot_general(a, b, (((1,), (1,)), ((), ())), preferred_element_type=F32)


def _dot_tn(a, b):
    return lax.dot_general(a, b, (((0,), (0,)), ((), ())), preferred_element_type=F32)


def _group_sum(x2, bd):
    hi = x2.astype(BF16)
    lo = (x2 - hi.astype(F32)).astype(BF16)
    return _dot(hi, bd) + _dot(lo, bd)


def _const_spec(shape):
    nd = len(shape)
    return pl.BlockSpec(shape, lambda *_: (0,) * nd, pipeline_mode=pl.Buffered(1))


def _cparams(sem):
    return pltpu.CompilerParams(dimension_semantics=sem, vmem_limit_bytes=VMEM_LIMIT)


def _ffn_body(has_mix, fc, *refs):
    if has_mix:
        x_ref, ya, yb, yc, yd, wo_ref, g_ref, wg_ref, wu_ref, wd_ref, o_ref, h_scr = refs
    else:
        x_ref, g_ref, wg_ref, wu_ref, wd_ref, o_ref, h_scr = refs
    x = x_ref[...]
    if has_mix:
        y = jnp.concatenate([ya[...], yb[...], yc[...], yd[...]], axis=-1)
        x = x + _dot(y, wo_ref[...])
    ms = jnp.mean(x * x, axis=-1, keepdims=True)
    xn = (x * lax.rsqrt(ms + EPS) * g_ref[...]).astype(BF16)
    d_ff = wg_ref.shape[1]
    for c in range(d_ff // fc):
        sl = slice(c * fc, (c + 1) * fc)
        gate = _dot(xn, wg_ref[:, sl])
        up = _dot(xn, wu_ref[:, sl])
        h_scr[:, sl] = (gate * jax.nn.sigmoid(gate) * up).astype(BF16)
    o_ref[...] = x + 0.5 * _dot(h_scr[...], wd_ref[...])


def _ffn(x2, g, wg, wu, wd, mix=None, tm=512, fc=256):
    m, d = x2.shape
    d_ff = wg.shape[1]
    tm = min(tm, m)
    row = lambda w: pl.BlockSpec((tm, w), lambda i: (i, 0))
    in_specs = [row(d)]
    args = [x2]
    if mix is not None:
        ys, wo = mix
        in_specs += [row(GROUP_WIDTH)] * 4 + [_const_spec(wo.shape)]
        args += list(ys) + [wo]
    in_specs += [_const_spec((1, d)), _const_spec(wg.shape), _const_spec(wu.shape), _const_spec(wd.shape)]
    args += [g, wg, wu, wd]
    return pl.pallas_call(
        functools.partial(_ffn_body, mix is not None, fc),
        grid=(m // tm,),
        in_specs=in_specs,
        out_specs=row(d),
        out_shape=jax.ShapeDtypeStruct((m, d), F32),
        scratch_shapes=[pltpu.VMEM((tm, d_ff), BF16)],
        compiler_params=_cparams(("parallel",)),
        name="ffn_mix" if mix is not None else "ffn",
    )(*args)


def _mix_in_body(tm, blk,
                 x_ref, gmix_ref, wn_ref, wt_ref, bd_ref,
                 gaq_ref, gak_ref,
                 wgu_ref, bgb_ref,
                 gqa_ref, wuq_ref, gdq_ref, cosT_ref, sinT_ref,
                 gkva_ref, wuk_ref, wuvT_ref, gdk_ref, gdkpe_ref, cpe_ref, spe_ref,
                 aqT_ref, ak_ref, avT_ref, iqT_ref, ik_ref, iwT_ref,
                 bq_ref, bk_ref, bv_ref, bla_ref, br_ref,
                 ch_ref,
                 dqT_ref, dk_ref, dvT_ref):
    nlt = tm // LANE
    x = x_ref[0]
    ms = jnp.mean(x * x, axis=-1, keepdims=True)
    xn = (x * lax.rsqrt(ms + EPS) * gmix_ref[...]).astype(BF16)
    bd = bd_ref[...]
    lane = lax.broadcasted_iota(I32, (tm, LANE), 1)

    def zs(off, width):
        return _dot(xn, wn_ref[:, off:off + width])

    def lanes(g):
        return jnp.tile(g, (1, nlt))

    zt = _dot_nt(wt_ref[...], xn)

    aq = zt[T_AQ:T_AQ + A_HEADS * HEAD_PAD].reshape(A_HEADS, HEAD_PAD, tm)
    aq_ms = jnp.sum(aq * aq, axis=1, keepdims=True) * (1.0 / A_HEAD_DIM)
    aq = aq * lax.rsqrt(aq_ms + EPS) * lanes(gaq_ref[...])[None] * (A_HEAD_DIM ** -0.5 * LOG2E)
    aqT_ref[0] = aq.reshape(A_HEADS * HEAD_PAD, tm).astype(BF16)
    av = zt[T_AV:T_AV + GROUP_WIDTH].astype(BF16)
    for c in range(tm // blk):
        avT_ref[0, c] = av[:, c * blk:(c + 1) * blk]
    iqT_ref[0] = zt[T_IQ:T_IQ + IDX_HEADS * IDX_DIM].astype(BF16)
    iwT_ref[0] = zt[T_IW:T_IW + IDX_HEADS] * ((IDX_HEADS ** -0.5) * (IDX_DIM ** -0.5))

    ak = zs(N_AK, GROUP_WIDTH)
    ak_ms = _group_sum(ak * ak, bd) * (1.0 / A_HEAD_DIM)
    ak = ak * lax.rsqrt(ak_ms + EPS) * gak_ref[...]
    for h in range(A_HEADS):
        pair = ak[:, LANE * (h // 2):LANE * (h // 2) + LANE]
        if h % 2 == 1:
            pair = pltpu.roll(pair, 64, 1)
        ak_ref[0, h] = jnp.where(lane < A_HEAD_DIM, pair, 0.0).astype(BF16)
    ik_ref[0] = zs(N_IK, LANE)[:, :IDX_DIM].astype(BF16)

    bq_ref[0] = zs(N_BQ, LANE) * (B_KEY_DIM ** -0.5)
    bk_ref[0] = zs(N_BK, LANE)
    bv_ref[0] = zs(N_BV, GROUP_WIDTH).astype(BF16)
    glat = zs(N_BG, LANE).astype(BF16)
    gate = _dot(glat, wgu_ref[...]) + bgb_ref[...]
    bla_ref[0] = (jnp.minimum(gate, 0.0) - jnp.log(1.0 + jnp.exp(-jnp.abs(gate)))) * (1.0 / B_GATE_TAU)
    br_ref[0] = zs(N_BR, GROUP_WIDTH)

    ca = zs(N_CU, C_CHANNELS)
    cg = zs(N_CU + C_CHANNELS, C_CHANNELS)
    ch_ref[0] = ca * jax.nn.sigmoid(cg)

    cq = zs(N_DCQ, D_Q_RANK)
    cq_ms = jnp.mean(cq * cq, axis=-1, keepdims=True)
    cqn = (cq * lax.rsqrt(cq_ms + EPS) * gqa_ref[...]).astype(BF16)
    dq = _dot_nt(wuq_ref[...], cqn).reshape(D_HEADS, HEAD_PAD, tm)
    dq_ms = jnp.sum(dq * dq, axis=1, keepdims=True) * (1.0 / D_QK)
    dq = dq * lax.rsqrt(dq_ms + EPS) * lanes(gdq_ref[...])[None] * (D_QK ** -0.5 * LOG2E)
    half = D_ROPE // 2
    x1 = dq[:, D_NOPE:D_NOPE + half]
    x2 = dq[:, D_NOPE + half:D_QK]
    cs = cosT_ref[...][None]
    sn = sinT_ref[...][None]
    dq = jnp.concatenate([dq[:, :D_NOPE], x1 * cs - x2 * sn, x2 * cs + x1 * sn, dq[:, D_QK:]], axis=1)
    dqT_ref[0] = dq.reshape(D_HEADS * HEAD_PAD, tm).astype(BF16)

    ckv = zs(N_DCKV, D_KV_RANK)
    ckv_ms = jnp.mean(ckv * ckv, axis=-1, keepdims=True)
    ckvn = (ckv * lax.rsqrt(ckv_ms + EPS) * gkva_ref[...]).astype(BF16)
    kn = _dot(ckvn, wuk_ref[...])
    dv = _dot_nt(wuvT_ref[...], ckvn).astype(BF16)
    for c in range(tm // blk):
        dvT_ref[0, c] = dv[:, c * blk:(c + 1) * blk]
    kpe = zs(N_DKPE, LANE)
    ss = _group_sum(kn * kn, bd) + jnp.sum(kpe * kpe, axis=-1, keepdims=True)
    rinv = lax.rsqrt(ss * (1.0 / D_QK) + EPS)
    kn = kn * rinv * gdk_ref[...]
    pe = kpe * gdkpe_ref[...]
    partner = jnp.where(lane < half, pltpu.roll(pe, LANE - half, 1), pltpu.roll(pe, half, 1))
    pe = pe * cpe_ref[...] + partner * spe_ref[...]
    pe = pltpu.roll(pe, D_NOPE, 1)
    for h in range(D_HEADS):
        pair = kn[:, LANE * (h // 2):LANE * (h // 2) + LANE]
        rpair = rinv[:, LANE * (h // 2):LANE * (h // 2) + LANE]
        if h % 2 == 1:
            pair = pltpu.roll(pair, 64, 1)
        else:
            rpair = pltpu.roll(rpair, 64, 1)
        dk_ref[0, h] = jnp.where(lane < D_NOPE, pair, pe * rpair).astype(BF16)


def _mix_in(x3, p, tm=512, blk=ATT_BLK):
    b, l, d = x3.shape
    tm = min(tm, l)
    grid = (b, l // tm)
    nck = l // blk
    tok = lambda w: pl.BlockSpec((1, tm, w), lambda bi, i: (bi, i, 0))
    tokT = lambda r: pl.BlockSpec((1, r, tm), lambda bi, i: (bi, 0, i))
    headk = pl.BlockSpec((1, 4, tm, HEAD_PAD), lambda bi, i: (bi, 0, i, 0))
    chunkT = pl.BlockSpec((1, tm // blk, GROUP_WIDTH, blk), lambda bi, i: (bi, i, 0, 0))
    postab = lambda r: pl.BlockSpec((r, tm), lambda bi, i: (0, i))
    posrow = pl.BlockSpec((tm, LANE), lambda bi, i: (i, 0))
    consts = [p["gmix"], p["wn"], p["wt"], p["bd"], p["gaq"], p["gak"], p["wgu"], p["bgb"],
              p["gqa"], p["wuq"], p["gdq"]]
    consts2 = [p["gkva"], p["wuk"], p["wuvT"], p["gdk"], p["gdkpe"]]
    in_specs = ([tok(d)] + [_const_spec(a.shape) for a in consts]
                + [postab(D_ROPE // 2), postab(D_ROPE // 2)]
                + [_const_spec(a.shape) for a in consts2] + [posrow, posrow])
    args = [x3] + consts + [p["cosT"], p["sinT"]] + consts2 + [p["cpe"], p["spe"]]
    sd = jax.ShapeDtypeStruct
    out_shape = [
        sd((b, A_HEADS * HEAD_PAD, l), BF16), sd((b, A_HEADS, l, HEAD_PAD), BF16),
        sd((b, nck, GROUP_WIDTH, blk), BF16), sd((b, IDX_HEADS * IDX_DIM, l), BF16),
        sd((b, l, IDX_DIM), BF16), sd((b, IDX_HEADS, l), F32),
        sd((b, l, LANE), F32), sd((b, l, LANE), F32), sd((b, l, GROUP_WIDTH), BF16),
        sd((b, l, LANE), F32), sd((b, l, GROUP_WIDTH), F32),
        sd((b, l, C_CHANNELS), F32),
        sd((b, D_HEADS * HEAD_PAD, l), BF16), sd((b, D_HEADS, l, HEAD_PAD), BF16),
        sd((b, nck, GROUP_WIDTH, blk), BF16),
    ]
    out_specs = [
        tokT(A_HEADS * HEAD_PAD), headk, chunkT, tokT(IDX_HEADS * IDX_DIM),
        tok(IDX_DIM), tokT(IDX_HEADS),
        tok(LANE), tok(LANE), tok(GROUP_WIDTH), tok(LANE), tok(GROUP_WIDTH),
        tok(C_CHANNELS),
        tokT(D_HEADS * HEAD_PAD), headk, chunkT,
    ]
    return pl.pallas_call(
        functools.partial(_mix_in_body, tm, blk),
        grid=grid, in_specs=in_specs, out_specs=out_specs, out_shape=out_shape,
        compiler_params=_cparams(("parallel", "parallel")),
        name="mix_in",
    )(*args)


def _attn_body(nh, dv, blk, topk, is_dsa, *refs):
    if is_dsa:
        (qT_ref, k_ref, vT_ref, iqT_ref, wT_ref, ik_ref, bt_ref,
         o_ref, m_scr, l_scr, acc_scr, key_scr, run_scr, hi_scr, lo_scr) = refs
    else:
        qT_ref, k_ref, vT_ref, o_ref, m_scr, l_scr, acc_scr = refs
    i = pl.program_id(1)
    t = blk
    row = lax.broadcasted_iota(I32, (t, t), 0)
    col = lax.broadcasted_iota(I32, (t, t), 1)
    causal_pen = jnp.where(row <= col, 0.0, NEG)

    m_scr[...] = jnp.full(m_scr.shape, NEG, F32)
    l_scr[...] = jnp.zeros(l_scr.shape, F32)
    acc_scr[...] = jnp.zeros(acc_scr.shape, F32)

    if is_dsa:
        def score_chunk(j, carry):
            r0 = pl.multiple_of(j * t, t)
            ikc = ik_ref[0, pl.ds(r0, t), :]
            s = jnp.zeros((t, t), F32)
            for h in range(IDX_HEADS):
                d = _dot(ikc, iqT_ref[0, IDX_DIM * h:IDX_DIM * (h + 1), :])
                s = s + jnp.maximum(d, 0.0) * wT_ref[0, h:h + 1, :]
            bits = lax.bitcast_convert_type(s, I32)
            key = jnp.where(bits < 0, bits ^ 0x7FFFFFFF, bits)
            key = jnp.where(row + j * t <= col + i * t, key, INT_MIN)
            key_scr[pl.ds(r0, t), :] = key
            hi_scr[pl.ds(r0, t), :] = jnp.right_shift(key, 16).astype(I16)
            lo_scr[pl.ds(r0, t), :] = ((key & 0xFFFF) + I16_MIN).astype(I16)
            return carry

        lax.fori_loop(0, i + 1, score_chunk, 0)

        def count_ge16(ref, cand):
            c16 = jnp.broadcast_to(cand.astype(I16), (16, t))
            one, zero = jnp.int16(1), jnp.int16(0)

            def body(j, accs):
                a0, a1 = accs
                kc = ref[pl.ds(pl.multiple_of(j * t, t), t), :]
                for r in range(0, t // 16, 2):
                    a0 = a0 + jnp.where(kc[16 * r:16 * r + 16] >= c16, one, zero)
                    a1 = a1 + jnp.where(kc[16 * r + 16:16 * r + 32] >= c16, one, zero)
                return a0, a1

            z = jnp.zeros((16, t), I16)
            a0, a1 = lax.fori_loop(0, i + 1, body, (z, z))
            return jnp.sum(a0.astype(I32) + a1.astype(I32), axis=0, keepdims=True)

        def bisect16(ref, k):
            ans = jnp.where(count_ge16(ref, jnp.zeros((1, t), I32)) >= k, 0, I16_MIN)

            def bit_body(b, ans):
                cand = ans | jnp.left_shift(jnp.int32(1), 14 - b)
                return jnp.where(count_ge16(ref, cand) >= k, cand, ans)

            return lax.fori_loop(0, 15, bit_body, ans)

        ans_hi = bisect16(hi_scr, jnp.full((1, t), topk, I32))
        above = count_ge16(hi_scr, ans_hi + 1)
        h16 = jnp.broadcast_to(ans_hi.astype(I16), (t, t))

        def mask_lo(j, carry):
            rows = pl.ds(pl.multiple_of(j * t, t), t)
            lo_scr[rows, :] = jnp.where(hi_scr[rows, :] == h16, lo_scr[rows, :], jnp.int16(I16_MIN))
            return carry

        lax.fori_loop(0, i + 1, mask_lo, 0)
        ans_lo = bisect16(lo_scr, topk - above)
        n_gt = above + count_ge16(lo_scr, ans_lo + 1)
        ans = jnp.left_shift(ans_hi, 16) | (ans_lo - I16_MIN)
        need = (topk - n_gt).astype(F32)
        run_scr[...] = jnp.zeros(run_scr.shape, F32)
        stri = jnp.where(col < row, 1.0, 0.0).astype(BF16)

    def super_chunk(js, last_is_diag):
        n = len(js)
        r0s = [pl.multiple_of(j * t, t) for j in js]
        pens = []
        if is_dsa:
            for c in range(n):
                kc = key_scr[pl.ds(r0s[c], t), :]
                eq = kc == ans
                eqf = jnp.where(eq, 1.0, 0.0)
                run = run_scr[0:1, :]
                rank = _dot(stri, eqf.astype(BF16)) + run
                run_scr[0:1, :] = run + jnp.sum(eqf, axis=0, keepdims=True)
                pens.append(jnp.where(kc > ans, 0.0, jnp.where(eq, jnp.where(rank < need, 0.0, NEG), NEG)))
        lgs = [[_dot(k_ref[0, h, pl.ds(r0s[c], t), :], qT_ref[0, HEAD_PAD * h:HEAD_PAD * (h + 1), :])
                for c in range(n)] for h in range(nh)]
        ps, alphas = [], []
        for h in range(nh):
            xs = []
            for c, j in enumerate(js):
                diag = last_is_diag and c == n - 1
                lg = lgs[h][c]
                if is_dsa:
                    tile = 0 if diag else jnp.minimum(i - j, 2)
                    lg = lg + (pens[c] + bt_ref[tile, h])
                elif diag:
                    lg = lg + causal_pen
                xs.append(lg)
            m_old = m_scr[h, 0:1, :]
            m_new = m_old
            for x in xs:
                m_new = jnp.maximum(m_new, jnp.max(x, axis=0, keepdims=True))
            alpha = jnp.exp2(m_old - m_new)
            pf = [jnp.exp2(x - m_new) for x in xs]
            psum = jnp.sum(pf[0], axis=0, keepdims=True)
            for p in pf[1:]:
                psum = psum + jnp.sum(p, axis=0, keepdims=True)
            l_scr[h, 0:1, :] = alpha * l_scr[h, 0:1, :] + psum
            m_scr[h, 0:1, :] = m_new
            ps.append([p.astype(BF16) for p in pf])
            alphas.append(alpha)
        for h in range(nh):
            vs = slice(dv * h, dv * (h + 1))
            pv = _dot(vT_ref[0, js[0], vs, :], ps[h][0])
            for c in range(1, n):
                pv = pv + _dot(vT_ref[0, js[c], vs, :], ps[h][c])
            acc_scr[vs, :] = alphas[h] * acc_scr[vs, :] + pv

    def pair_body(u, carry):
        super_chunk([2 * u, 2 * u + 1], False)
        return carry

    lax.fori_loop(0, i // 2, pair_body, 0)

    @pl.when(i % 2 == 1)
    def _():
        super_chunk([i - 1, i], True)

    @pl.when(i % 2 == 0)
    def _():
        super_chunk([i], True)

    outs = []
    for h in range(nh):
        outs.append(acc_scr[dv * h:dv * (h + 1), :] / l_scr[h, 0:1, :])
    o_ref[0] = jnp.transpose(jnp.concatenate(outs, axis=0)).astype(o_ref.dtype)


def _attention(qT, k, vT, dsa=None, blk=ATT_BLK):
    b, nh, l, _ = k.shape
    dv = vT.shape[2] // nh
    grid = (b, l // blk)
    qspec = lambda r: pl.BlockSpec((1, r, blk), lambda bi, i: (bi, 0, i))
    kspec = pl.BlockSpec((1, nh, l, HEAD_PAD), lambda bi, i: (bi, 0, 0, 0))
    vspec = pl.BlockSpec((1, l // blk, nh * dv, blk), lambda bi, i: (bi, 0, 0, 0))
    in_specs = [qspec(nh * HEAD_PAD), kspec, vspec]
    args = [qT, k, vT]
    scratch = [pltpu.VMEM((nh, 8, blk), F32), pltpu.VMEM((nh, 8, blk), F32),
               pltpu.VMEM((nh * dv, blk), F32)]
    topk = 0
    if dsa is not None:
        iqT, wT, ik, bt = dsa
        topk = min(TOPK_MAX, l // 4)
        in_specs += [qspec(IDX_HEADS * IDX_DIM), qspec(IDX_HEADS),
                     pl.BlockSpec((1, l, IDX_DIM), lambda bi, i: (bi, 0, 0)),
                     _const_spec(bt.shape)]
        args += [iqT, wT, ik, bt]
        scratch += [pltpu.VMEM((l, blk), I32), pltpu.VMEM((8, blk), F32),
                    pltpu.VMEM((l, blk), I16), pltpu.VMEM((l, blk), I16)]
    return pl.pallas_call(
        functools.partial(_attn_body, nh, dv, blk, topk, dsa is not None),
        grid=grid, in_specs=in_specs,
        out_specs=pl.BlockSpec((1, blk, nh * dv), lambda bi, i: (bi, i, 0)),
        out_shape=jax.ShapeDtypeStruct((b, l, nh * dv), BF16),
        scratch_shapes=scratch,
        compiler_params=_cparams(("parallel", "arbitrary")),
        name="dsa_attn" if dsa is not None else "mla_attn",
    )(*args)


def _gla_body(tg, q_ref, k_ref, v_ref, la_ref, r_ref, go_ref, bd_ref, o_ref, st_scr, o_scr):
    @pl.when(pl.program_id(1) == 0)
    def _():
        st_scr[...] = jnp.zeros(st_scr.shape, F32)

    cs = B_CHUNK
    la = la_ref[0]
    rl = lax.broadcasted_iota(I32, (tg, LANE), 0) & (cs - 1)
    b = la
    s = 1
    while s < cs:
        b = b + jnp.where(rl >= s, pltpu.roll(b, s, 0), 0.0)
        s *= 2
    q = q_ref[0]
    k = k_ref[0]
    qb = q * jnp.exp(b)
    ci = lax.broadcasted_iota(I32, (cs, cs), 0)
    cj = lax.broadcasted_iota(I32, (cs, cs), 1)
    for c in range(tg // cs):
        sl = slice(c * cs, (c + 1) * cs)
        bc = b[sl]
        mid = bc[cs // 2:cs // 2 + 1]
        last = bc[cs - 1:cs]
        qe = (q[sl] * jnp.exp(bc - mid)).astype(BF16)
        ke = (k[sl] * jnp.exp(mid - bc)).astype(BF16)
        kd = (k[sl] * jnp.exp(last - bc)).astype(BF16)
        qbc = qb[sl].astype(BF16)
        dl = jnp.exp(last)
        vc = v_ref[0, sl, :]
        for h in range(B_HEADS):
            ks = slice(B_KEY_DIM * h, B_KEY_DIM * (h + 1))
            vs = slice(B_VAL_DIM * h, B_VAL_DIM * (h + 1))
            a = jnp.where(cj <= ci, _dot_nt(qe[:, ks], ke[:, ks]), 0.0)
            st = st_scr[h]
            o_scr[sl, vs] = _dot(a.astype(BF16), vc[:, vs]) + _dot_nt(qbc[:, ks], st.astype(BF16))
            st_scr[h] = st * dl[:, ks] + _dot_tn(vc[:, vs], kd[:, ks])
    o = o_scr[...]
    ms = _group_sum(o * o, bd_ref[...]) * (1.0 / B_VAL_DIM)
    r = r_ref[0]
    o_ref[0] = (o * lax.rsqrt(ms + EPS) * go_ref[...] * (r * jax.nn.sigmoid(r))).astype(o_ref.dtype)


def _gla(bq, bk, bv, bla, br, go, bd, tg=512):
    b, l, _ = bq.shape
    tg = min(tg, l)
    tok = lambda w: pl.BlockSpec((1, tg, w), lambda bi, i: (bi, i, 0))
    return pl.pallas_call(
        functools.partial(_gla_body, tg),
        grid=(b, l // tg),
        in_specs=[tok(LANE), tok(LANE), tok(GROUP_WIDTH), tok(LANE), tok(GROUP_WIDTH),
                  _const_spec(go.shape), _const_spec(bd.shape)],
        out_specs=tok(GROUP_WIDTH),
        out_shape=jax.ShapeDtypeStruct((b, l, GROUP_WIDTH), BF16),
        scratch_shapes=[pltpu.VMEM((B_HEADS, B_VAL_DIM, B_KEY_DIM), F32),
                        pltpu.VMEM((tg, GROUP_WIDTH), F32)],
        compiler_params=_cparams(("parallel", "arbitrary")),
        name="gla",
    )(bq, bk, bv, bla, br, go, bd)


CONV_HIST = 32


def _conv_body(tc, h_ref, w_ref, b_ref, g_ref, o_ref, buf):
    @pl.when(pl.program_id(1) == 0)
    def _():
        buf[0:CONV_HIST, :] = jnp.zeros((CONV_HIST, C_CHANNELS), F32)

    @pl.when(pl.program_id(1) > 0)
    def _():
        buf[0:CONV_HIST, :] = buf[tc:tc + CONV_HIST, :]

    buf[CONV_HIST:CONV_HIST + tc, :] = h_ref[0]
    acc = jnp.zeros((tc, C_CHANNELS), F32) + b_ref[...]
    base = CONV_HIST - (C_KERNEL - 1)
    for j in range(C_KERNEL):
        acc = acc + buf[base + j:base + j + tc, :] * w_ref[j:j + 1, :]
    ms = jnp.mean(acc * acc, axis=-1, keepdims=True)
    y = acc * lax.rsqrt(ms + EPS) * g_ref[...]
    o_ref[0] = (y * jax.nn.sigmoid(y)).astype(o_ref.dtype)


def _conv(ch, w, bias, g, tc=512):
    b, l, c = ch.shape
    tc = min(tc, l)
    tok = pl.BlockSpec((1, tc, c), lambda bi, i: (bi, i, 0))
    return pl.pallas_call(
        functools.partial(_conv_body, tc),
        grid=(b, l // tc),
        in_specs=[tok, _const_spec(w.shape), _const_spec(bias.shape), _const_spec(g.shape)],
        out_specs=tok,
        out_shape=jax.ShapeDtypeStruct((b, l, c), BF16),
        scratch_shapes=[pltpu.VMEM((tc + CONV_HIST, c), F32)],
        compiler_params=_cparams(("parallel", "arbitrary")),
        name="conv",
    )(ch, w, bias, g)


def _t5_bucket(dist):
    max_exact = REL_BUCKETS // 2
    d = jnp.maximum(dist, 0)
    df = jnp.maximum(d, 1).astype(F32)
    large = max_exact + (jnp.log(df / max_exact) / math.log(REL_MAX_DIST / max_exact)
                         * (REL_BUCKETS - max_exact)).astype(I32)
    large = jnp.minimum(large, REL_BUCKETS - 1)
    return jnp.where(d < max_exact, d, large)


def _pad_cols(w, width):
    return jnp.pad(w, ((0, 0), (0, width - w.shape[1])))


def _lane_rep(v):
    return jnp.broadcast_to(v[:, None], (v.shape[0], LANE))


def _pad_heads_rows(w, heads, dim):
    w = w.reshape(heads, dim, w.shape[1])
    return jnp.pad(w, ((0, 0), (0, HEAD_PAD - dim), (0, 0))).reshape(heads * HEAD_PAD, -1)


def _split_w_in(w_in):
    widths = (256, 256, 256, 256, 32, 8, 128, 128, 256, 16, 256, 512, 256, 128, 32)
    offs = np.cumsum((0,) + widths)
    return [w_in[:, offs[n]:offs[n + 1]] for n in range(len(widths))]


def _layer_params(l, seq, blk, w):
    (aq, ak, av, iq, ik, iw, bq, bk, bv, bg, br, cu, dcq, dckv, dkpe) = _split_w_in(w["w_in"][l])
    wn = jnp.concatenate([ak, _pad_cols(ik, LANE), bq, bk, bv, _pad_cols(bg, LANE), br, cu, dcq, dckv,
                          _pad_cols(dkpe, LANE)], axis=1).astype(BF16)
    wt = jnp.concatenate([_pad_heads_rows(aq.T, A_HEADS, A_HEAD_DIM), av.T, iq.T,
                          jnp.pad(iw.T, ((0, T_TOT - T_IW - IDX_HEADS), (0, 0)))], axis=0).astype(BF16)
    hid = np.arange(GROUP_WIDTH) // 64
    bd = jnp.asarray(hid[:, None] == hid[None, :], dtype=BF16)
    row = lambda v: v[None, :].astype(F32)
    pad_to = lambda v, n: jnp.pad(v, (0, n - v.shape[0]))
    half = D_ROPE // 2
    freqs = ROPE_THETA ** (-jnp.arange(half, dtype=F32) / half)
    ang = jnp.arange(seq).astype(F32)[:, None] * freqs[None, :]
    cos, sin = jnp.cos(ang), jnp.sin(ang)
    zeros = jnp.zeros((seq, LANE - D_ROPE), F32)
    cpe = jnp.concatenate([cos, cos, zeros], axis=1)
    spe = jnp.concatenate([-sin, sin, zeros], axis=1)
    ukv = w["d_ukv"][l].reshape(D_KV_RANK, D_HEADS, D_NOPE + D_V)
    wuk = ukv[:, :, :D_NOPE].reshape(D_KV_RANK, D_HEADS * D_NOPE)
    wuv = ukv[:, :, D_NOPE:].reshape(D_KV_RANK, D_HEADS * D_V)
    gdk = w["d_k_norm"][l]
    p = dict(
        gmix=row(w["mix_norm"][l]), wn=wn, wt=wt, bd=bd,
        gaq=_lane_rep(pad_to(w["a_q_norm"][l], HEAD_PAD)),
        gak=row(jnp.tile(w["a_k_norm"][l], A_HEADS)),
        wgu=jnp.pad(w["b_gate_up"][l], ((0, LANE - B_GATE_RANK), (0, 0))).astype(BF16),
        bgb=row(w["b_gate_bias"][l]),
        gqa=row(w["d_qa_norm"][l]),
        wuq=_pad_heads_rows(w["d_uq"][l].T, D_HEADS, D_QK).astype(BF16),
        gdq=_lane_rep(pad_to(w["d_q_norm"][l], HEAD_PAD)),
        cosT=cos.T, sinT=sin.T,
        gkva=row(w["d_kva_norm"][l]), wuk=wuk.astype(BF16), wuvT=wuv.T.astype(BF16),
        gdk=row(jnp.tile(gdk[:D_NOPE], D_HEADS)), gdkpe=row(pad_to(gdk[D_NOPE:], LANE)),
        cpe=cpe, spe=spe,
    )
    return p


def _bias_tiles(rel_bias, blk):
    assert REL_MAX_DIST <= blk + 1
    kk = jnp.arange(blk)[:, None]
    qq = jnp.arange(blk)[None, :]
    rb = rel_bias.astype(F32).T

    def lookup(bucket):
        onehot = bucket[None, :, :, None] == jnp.arange(REL_BUCKETS)
        return jnp.sum(jnp.where(onehot, rb[:, None, None, :], 0.0), axis=-1) * LOG2E

    d0 = jnp.where(kk <= qq, lookup(_t5_bucket(qq - kk)), NEG)
    d1 = lookup(_t5_bucket(blk + qq - kk))
    far = jnp.broadcast_to(lookup(_t5_bucket(jnp.full((1, 1), 2 * blk, I32))), d1.shape)
    return jnp.stack([d0, d1, far])


def kernel(x, ffn1_norm, ffn1_gate, ffn1_up, ffn1_down, mix_norm, w_in, a_q_norm, a_k_norm, rel_bias,
           b_gate_up, b_gate_bias, b_out_norm, c_dw_w, c_dw_b, c_norm, d_qa_norm, d_uq, d_kva_norm,
           d_ukv, d_q_norm, d_k_norm, w_out, ffn2_norm, ffn2_gate, ffn2_up, ffn2_down):
    w = dict(mix_norm=mix_norm, w_in=w_in, a_q_norm=a_q_norm, a_k_norm=a_k_norm, b_gate_up=b_gate_up,
             b_gate_bias=b_gate_bias, d_qa_norm=d_qa_norm, d_uq=d_uq, d_kva_norm=d_kva_norm, d_ukv=d_ukv,
             d_q_norm=d_q_norm, d_k_norm=d_k_norm)
    bsz, seq, dm = x.shape
    depth = w_in.shape[0]
    blk = min(ATT_BLK, seq)
    bt = _bias_tiles(rel_bias, blk)
    row = lambda v: v[None, :].astype(F32)
    x2 = x.reshape(bsz * seq, dm)
    for l in range(depth):
        x2 = _ffn(x2, row(ffn1_norm[l]), ffn1_gate[l].astype(BF16), ffn1_up[l].astype(BF16),
                  ffn1_down[l].astype(BF16))
        p = _layer_params(l, seq, blk, w)
        (aqT, akh, avT, iqT, aik, iwT, bq, bk, bv, bla, br, ch, dqT, dkh, dvT) = _mix_in(
            x2.reshape(bsz, seq, dm), p, blk=blk)
        y_a = _attention(aqT, akh, avT, dsa=(iqT, iwT, aik, bt), blk=blk)
        y_b = _gla(bq, bk, bv, bla, br, row(jnp.tile(b_out_norm[l], B_HEADS)), p["bd"])
        cw = jnp.pad(c_dw_w[l][:, 0, :], ((0, CONV_HIST - C_KERNEL), (0, 0))).astype(F32)
        y_c = _conv(ch, cw, row(c_dw_b[l]), row(c_norm[l]))
        y_d = _attention(dqT, dkh, dvT, blk=blk)
        ys = [y.reshape(bsz * seq, GROUP_WIDTH) for y in (y_a, y_b, y_c, y_d)]
        x2 = _ffn(x2, row(ffn2_norm[l]), ffn2_gate[l].astype(BF16), ffn2_up[l].astype(BF16),
                  ffn2_down[l].astype(BF16), mix=(ys, w_out[l].astype(BF16)))
    return x2.reshape(bsz, seq, dm)
```

```python
import functools
import math

import jax
import jax.numpy as jnp
import numpy as np
from jax import lax
from jax.experimental import pallas as pl
from jax.experimental.pallas import tpu as pltpu

F32 = jnp.float32
BF16 = jnp.bfloat16
I32 = jnp.int32
I16 = jnp.int16

EPS = 1e-6
GROUP_WIDTH = 256
A_HEADS, A_HEAD_DIM = 4, 64
IDX_HEADS, IDX_DIM = 8, 32
TOPK_MAX = 256
REL_BUCKETS, REL_MAX_DIST = 32, 128
B_HEADS, B_KEY_DIM, B_VAL_DIM, B_GATE_RANK = 4, 32, 64, 16
B_GATE_TAU = 16.0
B_CHUNK = 64
C_CHANNELS, C_KERNEL = 256, 31
D_HEADS, D_Q_RANK, D_KV_RANK, D_NOPE, D_ROPE, D_V = 4, 256, 128, 64, 32, 64
D_QK = D_NOPE + D_ROPE
ROPE_THETA = 10000.0

LANE = 128
HEAD_PAD = 128
ATT_BLK = 256
INT_MIN = -2 ** 31
I16_MIN = -2 ** 15
NEG = -1e30
LOG2E = math.log2(math.e)
VMEM_LIMIT = 56 * 1024 * 1024

N_AK, N_IK, N_BQ, N_BK, N_BV, N_BG, N_BR, N_CU, N_DCQ, N_DCKV, N_DKPE, N_TOT = (
    0, 256, 384, 512, 640, 896, 1024, 1280, 1792, 2048, 2176, 2304)
T_AQ, T_AV, T_IQ, T_IW, T_TOT = 0, 512, 768, 1024, 1040


def _dot(a, b):
    return jnp.dot(a, b, preferred_element_type=F32)


def _dot_nt(a, b):
    return lax.dot_general(a, b, (((1,), (1,)), ((), ())), preferred_element_type=F32)


def _dot_tn(a, b):
    return lax.dot_general(a, b, (((0,), (0,)), ((), ())), preferred_element_type=F32)


def _group_sum(x2, bd):
    hi = x2.astype(BF16)
    lo = (x2 - hi.astype(F32)).astype(BF16)
    return _dot(hi, bd) + _dot(lo, bd)


def _const_spec(shape):
    nd = len(shape)
    return pl.BlockSpec(shape, lambda *_: (0,) * nd, pipeline_mode=pl.Buffered(1))


def _cparams(sem):
    return pltpu.CompilerParams(dimension_semantics=sem, vmem_limit_bytes=VMEM_LIMIT)


def _ffn_body(has_mix, fc, *refs):
    if has_mix:
        x_ref, ya, yb, yc, yd, wo_ref, g_ref, wg_ref, wu_ref, wd_ref, o_ref, h_scr = refs
    else:
        x_ref, g_ref, wg_ref, wu_ref, wd_ref, o_ref, h_scr = refs
    x = x_ref[...]
    if has_mix:
        y = jnp.concatenate([ya[...], yb[...], yc[...], yd[...]], axis=-1)
        x = x + _dot(y, wo_ref[...])
    ms = jnp.mean(x * x, axis=-1, keepdims=True)
    xn = (x * lax.rsqrt(ms + EPS) * g_ref[...]).astype(BF16)
    d_ff = wg_ref.shape[1]
    for c in range(d_ff // fc):
        sl = slice(c * fc, (c + 1) * fc)
        gate = _dot(xn, wg_ref[:, sl])
        up = _dot(xn, wu_ref[:, sl])
        h_scr[:, sl] = (gate * jax.nn.sigmoid(gate) * up).astype(BF16)
    o_ref[...] = x + 0.5 * _dot(h_scr[...], wd_ref[...])


def _ffn(x2, g, wg, wu, wd, mix=None, tm=512, fc=256):
    m, d = x2.shape
    d_ff = wg.shape[1]
    tm = min(tm, m)
    row = lambda w: pl.BlockSpec((tm, w), lambda i: (i, 0))
    in_specs = [row(d)]
    args = [x2]
    if mix is not None:
        ys, wo = mix
        in_specs += [row(GROUP_WIDTH)] * 4 + [_const_spec(wo.shape)]
        args += list(ys) + [wo]
    in_specs += [_const_spec((1, d)), _const_spec(wg.shape), _const_spec(wu.shape), _const_spec(wd.shape)]
    args += [g, wg, wu, wd]
    return pl.pallas_call(
        functools.partial(_ffn_body, mix is not None, fc),
        grid=(m // tm,),
        in_specs=in_specs,
        out_specs=row(d),
        out_shape=jax.ShapeDtypeStruct((m, d), F32),
        scratch_shapes=[pltpu.VMEM((tm, d_ff), BF16)],
        compiler_params=_cparams(("parallel",)),
        name="ffn_mix" if mix is not None else "ffn",
    )(*args)


def _mix_in_body(tm, blk,
                 x_ref, gmix_ref, wn_ref, wt_ref, bd_ref,
                 gaq_ref, gak_ref,
                 wgu_ref, bgb_ref,
                 gqa_ref, wuq_ref, gdq_ref, cosT_ref, sinT_ref,
                 gkva_ref, wuk_ref, wuvT_ref, gdk_ref, gdkpe_ref, cpe_ref, spe_ref,
                 aqT_ref, ak_ref, avT_ref, iqT_ref, ik_ref, iwT_ref,
                 bq_ref, bk_ref, bv_ref, bla_ref, br_ref,
                 ch_ref,
                 dqT_ref, dk_ref, dvT_ref):
    nlt = tm // LANE
    x = x_ref[0]
    ms = jnp.mean(x * x, axis=-1, keepdims=True)
    xn = (x * lax.rsqrt(ms + EPS) * gmix_ref[...]).astype(BF16)
    bd = bd_ref[...]
    lane = lax.broadcasted_iota(I32, (tm, LANE), 1)

    def zs(off, width):
        return _dot(xn, wn_ref[:, off:off + width])

    def lanes(g):
        return jnp.tile(g, (1, nlt))

    zt = _dot_nt(wt_ref[...], xn)

    aq = zt[T_AQ:T_AQ + A_HEADS * HEAD_PAD].reshape(A_HEADS, HEAD_PAD, tm)
    aq_ms = jnp.sum(aq * aq, axis=1, keepdims=True) * (1.0 / A_HEAD_DIM)
    aq = aq * lax.rsqrt(aq_ms + EPS) * lanes(gaq_ref[...])[None] * (A_HEAD_DIM ** -0.5 * LOG2E)
    aqT_ref[0] = aq.reshape(A_HEADS * HEAD_PAD, tm).astype(BF16)
    av = zt[T_AV:T_AV + GROUP_WIDTH].astype(BF16)
    for c in range(tm // blk):
        avT_ref[0, c] = av[:, c * blk:(c + 1) * blk]
    iqT_ref[0] = zt[T_IQ:T_IQ + IDX_HEADS * IDX_DIM].astype(BF16)
    iwT_ref[0] = zt[T_IW:T_IW + IDX_HEADS] * ((IDX_HEADS ** -0.5) * (IDX_DIM ** -0.5))

    ak = zs(N_AK, GROUP_WIDTH)
    ak_ms = _group_sum(ak * ak, bd) * (1.0 / A_HEAD_DIM)
    ak = ak * lax.rsqrt(ak_ms + EPS) * gak_ref[...]
    for h in range(A_HEADS):
        pair = ak[:, LANE * (h // 2):LANE * (h // 2) + LANE]
        if h % 2 == 1:
            pair = pltpu.roll(pair, 64, 1)
        ak_ref[0, h] = jnp.where(lane < A_HEAD_DIM, pair, 0.0).astype(BF16)
    ik_ref[0] = zs(N_IK, LANE)[:, :IDX_DIM].astype(BF16)

    bq_ref[0] = zs(N_BQ, LANE) * (B_KEY_DIM ** -0.5)
    bk_ref[0] = zs(N_BK, LANE)
    bv_ref[0] = zs(N_BV, GROUP_WIDTH).astype(BF16)
    glat = zs(N_BG, LANE).astype(BF16)
    gate = _dot(glat, wgu_ref[...]) + bgb_ref[...]
    bla_ref[0] = (jnp.minimum(gate, 0.0) - jnp.log(1.0 + jnp.exp(-jnp.abs(gate)))) * (1.0 / B_GATE_TAU)
    br_ref[0] = zs(N_BR, GROUP_WIDTH)

    ca = zs(N_CU, C_CHANNELS)
    cg = zs(N_CU + C_CHANNELS, C_CHANNELS)
    ch_ref[0] = ca * jax.nn.sigmoid(cg)

    cq = zs(N_DCQ, D_Q_RANK)
    cq_ms = jnp.mean(cq * cq, axis=-1, keepdims=True)
    cqn = (cq * lax.rsqrt(cq_ms + EPS) * gqa_ref[...]).astype(BF16)
    dq = _dot_nt(wuq_ref[...], cqn).reshape(D_HEADS, HEAD_PAD, tm)
    dq_ms = jnp.sum(dq * dq, axis=1, keepdims=True) * (1.0 / D_QK)
    dq = dq * lax.rsqrt(dq_ms + EPS) * lanes(gdq_ref[...])[None] * (D_QK ** -0.5 * LOG2E)
    half = D_ROPE // 2
    x1 = dq[:, D_NOPE:D_NOPE + half]
    x2 = dq[:, D_NOPE + half:D_QK]
    cs = cosT_ref[...][None]
    sn = sinT_ref[...][None]
    dq = jnp.concatenate([dq[:, :D_NOPE], x1 * cs - x2 * sn, x2 * cs + x1 * sn, dq[:, D_QK:]], axis=1)
    dqT_ref[0] = dq.reshape(D_HEADS * HEAD_PAD, tm).astype(BF16)

    ckv = zs(N_DCKV, D_KV_RANK)
    ckv_ms = jnp.mean(ckv * ckv, axis=-1, keepdims=True)
    ckvn = (ckv * lax.rsqrt(ckv_ms + EPS) * gkva_ref[...]).astype(BF16)
    kn = _dot(ckvn, wuk_ref[...])
    dv = _dot_nt(wuvT_ref[...], ckvn).astype(BF16)
    for c in range(tm // blk):
        dvT_ref[0, c] = dv[:, c * blk:(c + 1) * blk]
    kpe = zs(N_DKPE, LANE)
    ss = _group_sum(kn * kn, bd) + jnp.sum(kpe * kpe, axis=-1, keepdims=True)
    rinv = lax.rsqrt(ss * (1.0 / D_QK) + EPS)
    kn = kn * rinv * gdk_ref[...]
    pe = kpe * gdkpe_ref[...]
    partner = jnp.where(lane < half, pltpu.roll(pe, LANE - half, 1), pltpu.roll(pe, half, 1))
    pe = pe * cpe_ref[...] + partner * spe_ref[...]
    pe = pltpu.roll(pe, D_NOPE, 1)
    for h in range(D_HEADS):
        pair = kn[:, LANE * (h // 2):LANE * (h // 2) + LANE]
        rpair = rinv[:, LANE * (h // 2):LANE * (h // 2) + LANE]
        if h % 2 == 1:
            pair = pltpu.roll(pair, 64, 1)
        else:
            rpair = pltpu.roll(rpair, 64, 1)
        dk_ref[0, h] = jnp.where(lane < D_NOPE, pair, pe * rpair).astype(BF16)


def _mix_in(x3, p, tm=512, blk=ATT_BLK):
    b, l, d = x3.shape
    tm = min(tm, l)
    grid = (b, l // tm)
    nck = l // blk
    tok = lambda w: pl.BlockSpec((1, tm, w), lambda bi, i: (bi, i, 0))
    tokT = lambda r: pl.BlockSpec((1, r, tm), lambda bi, i: (bi, 0, i))
    headk = pl.BlockSpec((1, 4, tm, HEAD_PAD), lambda bi, i: (bi, 0, i, 0))
    chunkT = pl.BlockSpec((1, tm // blk, GROUP_WIDTH, blk), lambda bi, i: (bi, i, 0, 0))
    postab = lambda r: pl.BlockSpec((r, tm), lambda bi, i: (0, i))
    posrow = pl.BlockSpec((tm, LANE), lambda bi, i: (i, 0))
    consts = [p["gmix"], p["wn"], p["wt"], p["bd"], p["gaq"], p["gak"], p["wgu"], p["bgb"],
              p["gqa"], p["wuq"], p["gdq"]]
    consts2 = [p["gkva"], p["wuk"], p["wuvT"], p["gdk"], p["gdkpe"]]
    in_specs = ([tok(d)] + [_const_spec(a.shape) for a in consts]
                + [postab(D_ROPE // 2), postab(D_ROPE // 2)]
                + [_const_spec(a.shape) for a in consts2] + [posrow, posrow])
    args = [x3] + consts + [p["cosT"], p["sinT"]] + consts2 + [p["cpe"], p["spe"]]
    sd = jax.ShapeDtypeStruct
    out_shape = [
        sd((b, A_HEADS * HEAD_PAD, l), BF16), sd((b, A_HEADS, l, HEAD_PAD), BF16),
        sd((b, nck, GROUP_WIDTH, blk), BF16), sd((b, IDX_HEADS * IDX_DIM, l), BF16),
        sd((b, l, IDX_DIM), BF16), sd((b, IDX_HEADS, l), F32),
        sd((b, l, LANE), F32), sd((b, l, LANE), F32), sd((b, l, GROUP_WIDTH), BF16),
        sd((b, l, LANE), F32), sd((b, l, GROUP_WIDTH), F32),
        sd((b, l, C_CHANNELS), F32),
        sd((b, D_HEADS * HEAD_PAD, l), BF16), sd((b, D_HEADS, l, HEAD_PAD), BF16),
        sd((b, nck, GROUP_WIDTH, blk), BF16),
    ]
    out_specs = [
        tokT(A_HEADS * HEAD_PAD), headk, chunkT, tokT(IDX_HEADS * IDX_DIM),
        tok(IDX_DIM), tokT(IDX_HEADS),
        tok(LANE), tok(LANE), tok(GROUP_WIDTH), tok(LANE), tok(GROUP_WIDTH),
        tok(C_CHANNELS),
        tokT(D_HEADS * HEAD_PAD), headk, chunkT,
    ]
    return pl.pallas_call(
        functools.partial(_mix_in_body, tm, blk),
        grid=grid, in_specs=in_specs, out_specs=out_specs, out_shape=out_shape,
        compiler_params=_cparams(("parallel", "parallel")),
        name="mix_in",
    )(*args)


def _attn_body(nh, dv, blk, topk, is_dsa, *refs):
    lg_scr = [refs[len(refs) - 2 * nh + 2 * h:len(refs) - 2 * nh + 2 * h + 2] for h in range(nh)]
    refs = refs[:len(refs) - 2 * nh]
    if is_dsa:
        (qT_ref, k_ref, vT_ref, iqT_ref, wT_ref, ik_ref, bt_ref,
         o_ref, m_scr, l_scr, acc_scr, key_scr, run_scr, hi_scr, lo_scr) = refs
    else:
        qT_ref, k_ref, vT_ref, o_ref, m_scr, l_scr, acc_scr = refs
    i = pl.program_id(1)
    t = blk
    row = lax.broadcasted_iota(I32, (t, t), 0)
    col = lax.broadcasted_iota(I32, (t, t), 1)
    causal_pen = jnp.where(row <= col, 0.0, NEG)

    m_scr[...] = jnp.full(m_scr.shape, NEG, F32)
    l_scr[...] = jnp.zeros(l_scr.shape, F32)
    acc_scr[...] = jnp.zeros(acc_scr.shape, F32)

    if is_dsa:
        def score_chunk(j, carry):
            r0 = pl.multiple_of(j * t, t)
            ikc = ik_ref[0, pl.ds(r0, t), :]
            s = jnp.zeros((t, t), F32)
            for h in range(IDX_HEADS):
                d = _dot(ikc, iqT_ref[0, IDX_DIM * h:IDX_DIM * (h + 1), :])
                s = s + jnp.maximum(d, 0.0) * wT_ref[0, h:h + 1, :]
            bits = lax.bitcast_convert_type(s, I32)
            key = jnp.where(bits < 0, bits ^ 0x7FFFFFFF, bits)
            key = jnp.where(row + j * t <= col + i * t, key, INT_MIN)
            key_scr[pl.ds(r0, t), :] = key
            hi_scr[pl.ds(r0, t), :] = jnp.right_shift(key, 16).astype(I16)
            lo_scr[pl.ds(r0, t), :] = ((key & 0xFFFF) + I16_MIN).astype(I16)
            return carry

        lax.fori_loop(0, i + 1, score_chunk, 0)

        def count_ge16(ref, cand):
            c16 = jnp.broadcast_to(cand.astype(I16), (16, t))
            one, zero = jnp.int16(1), jnp.int16(0)

            def body(j, accs):
                a0, a1 = accs
                kc = ref[pl.ds(pl.multiple_of(j * t, t), t), :]
                for r in range(0, t // 16, 2):
                    a0 = a0 + jnp.where(kc[16 * r:16 * r + 16] >= c16, one, zero)
                    a1 = a1 + jnp.where(kc[16 * r + 16:16 * r + 32] >= c16, one, zero)
                return a0, a1

            z = jnp.zeros((16, t), I16)
            a0, a1 = lax.fori_loop(0, i + 1, body, (z, z))
            return jnp.sum(a0.astype(I32) + a1.astype(I32), axis=0, keepdims=True)

        def bisect16(ref, k):
            ans = jnp.where(count_ge16(ref, jnp.zeros((1, t), I32)) >= k, 0, I16_MIN)

            def bit_body(b, ans):
                cand = ans | jnp.left_shift(jnp.int32(1), 14 - b)
                return jnp.where(count_ge16(ref, cand) >= k, cand, ans)

            return lax.fori_loop(0, 15, bit_body, ans)

        ans_hi = bisect16(hi_scr, jnp.full((1, t), topk, I32))
        above = count_ge16(hi_scr, ans_hi + 1)
        h16 = jnp.broadcast_to(ans_hi.astype(I16), (t, t))

        def mask_lo(j, carry):
            rows = pl.ds(pl.multiple_of(j * t, t), t)
            lo_scr[rows, :] = jnp.where(hi_scr[rows, :] == h16, lo_scr[rows, :], jnp.int16(I16_MIN))
            return carry

        lax.fori_loop(0, i + 1, mask_lo, 0)
        ans_lo = bisect16(lo_scr, topk - above)
        n_gt = above + count_ge16(lo_scr, ans_lo + 1)
        ans = jnp.left_shift(ans_hi, 16) | (ans_lo - I16_MIN)
        need = (topk - n_gt).astype(F32)
        run_scr[...] = jnp.zeros(run_scr.shape, F32)
        stri = jnp.where(col < row, 1.0, 0.0).astype(BF16)

    def logits(h, r0):
        return _dot(k_ref[0, h, pl.ds(r0, t), :], qT_ref[0, HEAD_PAD * h:HEAD_PAD * (h + 1), :])

    def stage_logits(u, heads=None):
        for h in (range(nh) if heads is None else heads):
            for c in range(2):
                lg_scr[h][c][...] = logits(h, pl.multiple_of((2 * u + c) * t, t))

    def super_chunk(js, last_is_diag, staged=False, stage_next=None):
        n = len(js)
        r0s = [pl.multiple_of(j * t, t) for j in js]
        pens = []
        if is_dsa:
            for c in range(n):
                kc = key_scr[pl.ds(r0s[c], t), :]
                eq = kc == ans
                eqf = jnp.where(eq, 1.0, 0.0)
                run = run_scr[0:1, :]
                rank = _dot(stri, eqf.astype(BF16)) + run
                run_scr[0:1, :] = run + jnp.sum(eqf, axis=0, keepdims=True)
                pens.append(jnp.where(kc > ans, 0.0, jnp.where(eq, jnp.where(rank < need, 0.0, NEG), NEG)))
        if staged:
            lgs = [[lg_scr[h][c][...] for c in range(n)] for h in range(nh)]
        else:
            lgs = [[logits(h, r0s[c]) for c in range(n)] for h in range(nh)]
        ps, alphas = [], []
        for h in range(nh):
            xs = []
            for c, j in enumerate(js):
                diag = last_is_diag and c == n - 1
                lg = lgs[h][c]
                if is_dsa:
                    tile = 0 if diag else jnp.minimum(i - j, 2)
                    lg = lg + (pens[c] + bt_ref[tile, h])
                elif diag:
                    lg = lg + causal_pen
                xs.append(lg)
            m_old = m_scr[h, 0:1, :]
            m_new = m_old
            for x in xs:
                m_new = jnp.maximum(m_new, jnp.max(x, axis=0, keepdims=True))
            alpha = jnp.exp2(m_old - m_new)
            pf = [jnp.exp2(x - m_new) for x in xs]
            psum = jnp.sum(pf[0], axis=0, keepdims=True)
            for p in pf[1:]:
                psum = psum + jnp.sum(p, axis=0, keepdims=True)
            l_scr[h, 0:1, :] = alpha * l_scr[h, 0:1, :] + psum
            m_scr[h, 0:1, :] = m_new
            ps.append([p.astype(BF16) for p in pf])
            alphas.append(alpha)
            if stage_next is not None:
                stage_logits(stage_next, heads=[h])
        for h in range(nh):
            vs = slice(dv * h, dv * (h + 1))
            pv = _dot(vT_ref[0, js[0], vs, :], ps[h][0])
            for c in range(1, n):
                pv = pv + _dot(vT_ref[0, js[c], vs, :], ps[h][c])
            acc_scr[vs, :] = alphas[h] * acc_scr[vs, :] + pv

    npairs = i // 2
    odd = i % 2 == 1

    @pl.when(npairs > 0)
    def _():
        stage_logits(0)

    def pair_body(u, carry):
        super_chunk([2 * u, 2 * u + 1], False, staged=True, stage_next=u + 1)
        return carry

    lax.fori_loop(0, npairs - 1, pair_body, 0)

    def last_steps(has_pairs, with_pair):
        if has_pairs:
            u = npairs - 1
            super_chunk([2 * u, 2 * u + 1], False, staged=True)
        super_chunk([i - 1, i] if with_pair else [i], True)

    for has_pairs in (True, False):
        for with_pair in (True, False):
            cond = jnp.logical_and((npairs > 0) == has_pairs, odd == with_pair)
            pl.when(cond)(functools.partial(last_steps, has_pairs, with_pair))

    outs = []
    for h in range(nh):
        outs.append(acc_scr[dv * h:dv * (h + 1), :] / l_scr[h, 0:1, :])
    o_ref[0] = jnp.transpose(jnp.concatenate(outs, axis=0)).astype(o_ref.dtype)


def _attention(qT, k, vT, dsa=None, blk=ATT_BLK):
    b, nh, l, _ = k.shape
    dv = vT.shape[2] // nh
    grid = (b, l // blk)
    qspec = lambda r: pl.BlockSpec((1, r, blk), lambda bi, i: (bi, 0, i))
    kspec = pl.BlockSpec((1, nh, l, HEAD_PAD), lambda bi, i: (bi, 0, 0, 0))
    vspec = pl.BlockSpec((1, l // blk, nh * dv, blk), lambda bi, i: (bi, 0, 0, 0))
    in_specs = [qspec(nh * HEAD_PAD), kspec, vspec]
    args = [qT, k, vT]
    scratch = [pltpu.VMEM((nh, 8, blk), F32), pltpu.VMEM((nh, 8, blk), F32),
               pltpu.VMEM((nh * dv, blk), F32)]
    topk = 0
    if dsa is not None:
        iqT, wT, ik, bt = dsa
        topk = min(TOPK_MAX, l // 4)
        in_specs += [qspec(IDX_HEADS * IDX_DIM), qspec(IDX_HEADS),
                     pl.BlockSpec((1, l, IDX_DIM), lambda bi, i: (bi, 0, 0)),
                     _const_spec(bt.shape)]
        args += [iqT, wT, ik, bt]
        scratch += [pltpu.VMEM((l, blk), I32), pltpu.VMEM((8, blk), F32),
                    pltpu.VMEM((l, blk), I16), pltpu.VMEM((l, blk), I16)]
    return pl.pallas_call(
        functools.partial(_attn_body, nh, dv, blk, topk, dsa is not None),
        grid=grid, in_specs=in_specs,
        out_specs=pl.BlockSpec((1, blk, nh * dv), lambda bi, i: (bi, i, 0)),
        out_shape=jax.ShapeDtypeStruct((b, l, nh * dv), BF16),
        scratch_shapes=scratch + [pltpu.VMEM((blk, blk), F32)] * (2 * nh),
        compiler_params=_cparams(("parallel", "arbitrary")),
        name="dsa_attn" if dsa is not None else "mla_attn",
    )(*args)


def _gla_body(tg, q_ref, k_ref, v_ref, la_ref, r_ref, go_ref, bd_ref, o_ref, st_scr, o_scr):
    @pl.when(pl.program_id(1) == 0)
    def _():
        st_scr[...] = jnp.zeros(st_scr.shape, F32)

    cs = B_CHUNK
    la = la_ref[0]
    rl = lax.broadcasted_iota(I32, (tg, LANE), 0) & (cs - 1)
    b = la
    s = 1
    while s < cs:
        b = b + jnp.where(rl >= s, pltpu.roll(b, s, 0), 0.0)
        s *= 2
    q = q_ref[0]
    k = k_ref[0]
    qb = q * jnp.exp(b)
    ci = lax.broadcasted_iota(I32, (cs, cs), 0)
    cj = lax.broadcasted_iota(I32, (cs, cs), 1)
    for c in range(tg // cs):
        sl = slice(c * cs, (c + 1) * cs)
        bc = b[sl]
        mid = bc[cs // 2:cs // 2 + 1]
        last = bc[cs - 1:cs]
        qe = (q[sl] * jnp.exp(bc - mid)).astype(BF16)
        ke = (k[sl] * jnp.exp(mid - bc)).astype(BF16)
        kd = (k[sl] * jnp.exp(last - bc)).astype(BF16)
        qbc = qb[sl].astype(BF16)
        dl = jnp.exp(last)
        vc = v_ref[0, sl, :]
        for h in range(B_HEADS):
            ks = slice(B_KEY_DIM * h, B_KEY_DIM * (h + 1))
            vs = slice(B_VAL_DIM * h, B_VAL_DIM * (h + 1))
            a = jnp.where(cj <= ci, _dot_nt(qe[:, ks], ke[:, ks]), 0.0)
            st = st_scr[h]
            o_scr[sl, vs] = _dot(a.astype(BF16), vc[:, vs]) + _dot_nt(qbc[:, ks], st.astype(BF16))
            st_scr[h] = st * dl[:, ks] + _dot_tn(vc[:, vs], kd[:, ks])
    o = o_scr[...]
    ms = _group_sum(o * o, bd_ref[...]) * (1.0 / B_VAL_DIM)
    r = r_ref[0]
    o_ref[0] = (o * lax.rsqrt(ms + EPS) * go_ref[...] * (r * jax.nn.sigmoid(r))).astype(o_ref.dtype)


def _gla(bq, bk, bv, bla, br, go, bd, tg=512):
    b, l, _ = bq.shape
    tg = min(tg, l)
    tok = lambda w: pl.BlockSpec((1, tg, w), lambda bi, i: (bi, i, 0))
    return pl.pallas_call(
        functools.partial(_gla_body, tg),
        grid=(b, l // tg),
        in_specs=[tok(LANE), tok(LANE), tok(GROUP_WIDTH), tok(LANE), tok(GROUP_WIDTH),
                  _const_spec(go.shape), _const_spec(bd.shape)],
        out_specs=tok(GROUP_WIDTH),
        out_shape=jax.ShapeDtypeStruct((b, l, GROUP_WIDTH), BF16),
        scratch_shapes=[pltpu.VMEM((B_HEADS, B_VAL_DIM, B_KEY_DIM), F32),
                        pltpu.VMEM((tg, GROUP_WIDTH), F32)],
        compiler_params=_cparams(("parallel", "arbitrary")),
        name="gla",
    )(bq, bk, bv, bla, br, go, bd)


CONV_HIST = 32


def _conv_body(tc, h_ref, w_ref, b_ref, g_ref, o_ref, buf):
    @pl.when(pl.program_id(1) == 0)
    def _():
        buf[0:CONV_HIST, :] = jnp.zeros((CONV_HIST, C_CHANNELS), F32)

    @pl.when(pl.program_id(1) > 0)
    def _():
        buf[0:CONV_HIST, :] = buf[tc:tc + CONV_HIST, :]

    buf[CONV_HIST:CONV_HIST + tc, :] = h_ref[0]
    acc = jnp.zeros((tc, C_CHANNELS), F32) + b_ref[...]
    base = CONV_HIST - (C_KERNEL - 1)
    for j in range(C_KERNEL):
        acc = acc + buf[base + j:base + j + tc, :] * w_ref[j:j + 1, :]
    ms = jnp.mean(acc * acc, axis=-1, keepdims=True)
    y = acc * lax.rsqrt(ms + EPS) * g_ref[...]
    o_ref[0] = (y * jax.nn.sigmoid(y)).astype(o_ref.dtype)


def _conv(ch, w, bias, g, tc=512):
    b, l, c = ch.shape
    tc = min(tc, l)
    tok = pl.BlockSpec((1, tc, c), lambda bi, i: (bi, i, 0))
    return pl.pallas_call(
        functools.partial(_conv_body, tc),
        grid=(b, l // tc),
        in_specs=[tok, _const_spec(w.shape), _const_spec(bias.shape), _const_spec(g.shape)],
        out_specs=tok,
        out_shape=jax.ShapeDtypeStruct((b, l, c), BF16),
        scratch_shapes=[pltpu.VMEM((tc + CONV_HIST, c), F32)],
        compiler_params=_cparams(("parallel", "arbitrary")),
        name="conv",
    )(ch, w, bias, g)


def _t5_bucket(dist):
    max_exact = REL_BUCKETS // 2
    d = jnp.maximum(dist, 0)
    df = jnp.maximum(d, 1).astype(F32)
    large = max_exact + (jnp.log(df / max_exact) / math.log(REL_MAX_DIST / max_exact)
                         * (REL_BUCKETS - max_exact)).astype(I32)
    large = jnp.minimum(large, REL_BUCKETS - 1)
    return jnp.where(d < max_exact, d, large)


def _pad_cols(w, width):
    return jnp.pad(w, ((0, 0), (0, width - w.shape[1])))


def _lane_rep(v):
    return jnp.broadcast_to(v[:, None], (v.shape[0], LANE))


def _pad_heads_rows(w, heads, dim):
    w = w.reshape(heads, dim, w.shape[1])
    return jnp.pad(w, ((0, 0), (0, HEAD_PAD - dim), (0, 0))).reshape(heads * HEAD_PAD, -1)


def _split_w_in(w_in):
    widths = (256, 256, 256, 256, 32, 8, 128, 128, 256, 16, 256, 512, 256, 128, 32)
    offs = np.cumsum((0,) + widths)
    return [w_in[:, offs[n]:offs[n + 1]] for n in range(len(widths))]


def _layer_params(l, seq, blk, w):
    (aq, ak, av, iq, ik, iw, bq, bk, bv, bg, br, cu, dcq, dckv, dkpe) = _split_w_in(w["w_in"][l])
    wn = jnp.concatenate([ak, _pad_cols(ik, LANE), bq, bk, bv, _pad_cols(bg, LANE), br, cu, dcq, dckv,
                          _pad_cols(dkpe, LANE)], axis=1).astype(BF16)
    wt = jnp.concatenate([_pad_heads_rows(aq.T, A_HEADS, A_HEAD_DIM), av.T, iq.T,
                          jnp.pad(iw.T, ((0, T_TOT - T_IW - IDX_HEADS), (0, 0)))], axis=0).astype(BF16)
    hid = np.arange(GROUP_WIDTH) // 64
    bd = jnp.asarray(hid[:, None] == hid[None, :], dtype=BF16)
    row = lambda v: v[None, :].astype(F32)
    pad_to = lambda v, n: jnp.pad(v, (0, n - v.shape[0]))
    half = D_ROPE // 2
    freqs = ROPE_THETA ** (-jnp.arange(half, dtype=F32) / half)
    ang = jnp.arange(seq).astype(F32)[:, None] * freqs[None, :]
    cos, sin = jnp.cos(ang), jnp.sin(ang)
    zeros = jnp.zeros((seq, LANE - D_ROPE), F32)
    cpe = jnp.concatenate([cos, cos, zeros], axis=1)
    spe = jnp.concatenate([-sin, sin, zeros], axis=1)
    ukv = w["d_ukv"][l].reshape(D_KV_RANK, D_HEADS, D_NOPE + D_V)
    wuk = ukv[:, :, :D_NOPE].reshape(D_KV_RANK, D_HEADS * D_NOPE)
    wuv = ukv[:, :, D_NOPE:].reshape(D_KV_RANK, D_HEADS * D_V)
    gdk = w["d_k_norm"][l]
    p = dict(
        gmix=row(w["mix_norm"][l]), wn=wn, wt=wt, bd=bd,
        gaq=_lane_rep(pad_to(w["a_q_norm"][l], HEAD_PAD)),
        gak=row(jnp.tile(w["a_k_norm"][l], A_HEADS)),
        wgu=jnp.pad(w["b_gate_up"][l], ((0, LANE - B_GATE_RANK), (0, 0))).astype(BF16),
        bgb=row(w["b_gate_bias"][l]),
        gqa=row(w["d_qa_norm"][l]),
        wuq=_pad_heads_rows(w["d_uq"][l].T, D_HEADS, D_QK).astype(BF16),
        gdq=_lane_rep(pad_to(w["d_q_norm"][l], HEAD_PAD)),
        cosT=cos.T, sinT=sin.T,
        gkva=row(w["d_kva_norm"][l]), wuk=wuk.astype(BF16), wuvT=wuv.T.astype(BF16),
        gdk=row(jnp.tile(gdk[:D_NOPE], D_HEADS)), gdkpe=row(pad_to(gdk[D_NOPE:], LANE)),
        cpe=cpe, spe=spe,
    )
    return p


def _bias_tiles(rel_bias, blk):
    assert REL_MAX_DIST <= blk + 1
    kk = jnp.arange(blk)[:, None]
    qq = jnp.arange(blk)[None, :]
    rb = rel_bias.astype(F32).T

    def lookup(bucket):
        onehot = bucket[None, :, :, None] == jnp.arange(REL_BUCKETS)
        return jnp.sum(jnp.where(onehot, rb[:, None, None, :], 0.0), axis=-1) * LOG2E

    d0 = jnp.where(kk <= qq, lookup(_t5_bucket(qq - kk)), NEG)
    d1 = lookup(_t5_bucket(blk + qq - kk))
    far = jnp.broadcast_to(lookup(_t5_bucket(jnp.full((1, 1), 2 * blk, I32))), d1.shape)
    return jnp.stack([d0, d1, far])


def kernel(x, ffn1_norm, ffn1_gate, ffn1_up, ffn1_down, mix_norm, w_in, a_q_norm, a_k_norm, rel_bias,
           b_gate_up, b_gate_bias, b_out_norm, c_dw_w, c_dw_b, c_norm, d_qa_norm, d_uq, d_kva_norm,
           d_ukv, d_q_norm, d_k_norm, w_out, ffn2_norm, ffn2_gate, ffn2_up, ffn2_down):
    w = dict(mix_norm=mix_norm, w_in=w_in, a_q_norm=a_q_norm, a_k_norm=a_k_norm, b_gate_up=b_gate_up,
             b_gate_bias=b_gate_bias, d_qa_norm=d_qa_norm, d_uq=d_uq, d_kva_norm=d_kva_norm, d_ukv=d_ukv,
             d_q_norm=d_q_norm, d_k_norm=d_k_norm)
    bsz, seq, dm = x.shape
    depth = w_in.shape[0]
    blk = min(ATT_BLK, seq)
    bt = _bias_tiles(rel_bias, blk)
    row = lambda v: v[None, :].astype(F32)
    x2 = x.reshape(bsz * seq, dm)
    for l in range(depth):
        x2 = _ffn(x2, row(ffn1_norm[l]), ffn1_gate[l].astype(BF16), ffn1_up[l].astype(BF16),
                  ffn1_down[l].astype(BF16))
        p = _layer_params(l, seq, blk, w)
        (aqT, akh, avT, iqT, aik, iwT, bq, bk, bv, bla, br, ch, dqT, dkh, dvT) = _mix_in(
            x2.reshape(bsz, seq, dm), p, blk=blk)
        y_a = _attention(aqT, akh, avT, dsa=(iqT, iwT, aik, bt), blk=blk)
        y_b = _gla(bq, bk, bv, bla, br, row(jnp.tile(b_out_norm[l], B_HEADS)), p["bd"])
        cw = jnp.pad(c_dw_w[l][:, 0, :], ((0, CONV_HIST - C_KERNEL), (0, 0))).astype(F32)
        y_c = _conv(ch, cw, row(c_dw_b[l]), row(c_norm[l]))
        y_d = _attention(dqT, dkh, dvT, blk=blk)
        ys = [y.reshape(bsz * seq, GROUP_WIDTH) for y in (y_a, y_b, y_c, y_d)]
        x2 = _ffn(x2, row(ffn2_norm[l]), ffn2_gate[l].astype(BF16), ffn2_up[l].astype(BF16),
                  ffn2_down[l].astype(BF16), mix=(ys, w_out[l].astype(BF16)))
    return x2.reshape(bsz, seq, dm)
```

```python
import functools
import math

import jax
import jax.numpy as jnp
import numpy as np
from jax import lax
from jax.experimental import pallas as pl
from jax.experimental.pallas import tpu as pltpu

F32 = jnp.float32
BF16 = jnp.bfloat16
I32 = jnp.int32
I16 = jnp.int16

EPS = 1e-6
GROUP_WIDTH = 256
A_HEADS, A_HEAD_DIM = 4, 64
IDX_HEADS, IDX_DIM = 8, 32
TOPK_MAX = 256
REL_BUCKETS, REL_MAX_DIST = 32, 128
B_HEADS, B_KEY_DIM, B_VAL_DIM, B_GATE_RANK = 4, 32, 64, 16
B_GATE_TAU = 16.0
B_CHUNK = 64
C_CHANNELS, C_KERNEL = 256, 31
D_HEADS, D_Q_RANK, D_KV_RANK, D_NOPE, D_ROPE, D_V = 4, 256, 128, 64, 32, 64
D_QK = D_NOPE + D_ROPE
ROPE_THETA = 10000.0

LANE = 128
SUBLANES = 8
HEAD_PAD = 128
ATT_BLK = 256
INT_MIN = -2 ** 31
I16_MIN = -2 ** 15
NEG = -1e30
LOG2E = math.log2(math.e)
VMEM_LIMIT = 56 * 1024 * 1024

N_AK, N_IK, N_BQ, N_BK, N_BV, N_BG, N_BR, N_CU, N_DCQ, N_DCKV, N_DKPE, N_TOT = (
    0, 256, 384, 512, 640, 896, 1024, 1280, 1792, 2048, 2176, 2304)
T_AQ, T_AV, T_IQ, T_IW, T_TOT = 0, 512, 768, 1024, 1040


def _dot(a, b):
    return jnp.dot(a, b, preferred_element_type=F32)


def _dot_nt(a, b):
    return lax.dot_general(a, b, (((1,), (1,)), ((), ())), preferred_element_type=F32)


def _dot_tn(a, b):
    return lax.dot_general(a, b, (((0,), (0,)), ((), ())), preferred_element_type=F32)


def _group_sum(x2, bd):
    hi = x2.astype(BF16)
    lo = (x2 - hi.astype(F32)).astype(BF16)
    return _dot(hi, bd) + _dot(lo, bd)


def _const_spec(shape, layer=None):
    nd = len(shape)
    if layer is None:
        return pl.BlockSpec(shape, lambda *_: (0,) * nd, pipeline_mode=pl.Buffered(1))
    return pl.BlockSpec((None,) + tuple(shape[1:]), lambda *_: (layer,) + (0,) * (nd - 1),
                        pipeline_mode=pl.Buffered(1))


def _cparams(sem):
    return pltpu.CompilerParams(dimension_semantics=sem, vmem_limit_bytes=VMEM_LIMIT)


def _ffn_body(has_mix, fc, *refs):
    if has_mix:
        x_ref, ya, yb, yc, yd, wo_ref, g_ref, wg_ref, wu_ref, wd_ref, o_ref, h_scr = refs
    else:
        x_ref, g_ref, wg_ref, wu_ref, wd_ref, o_ref, h_scr = refs
    x = x_ref[...]
    if has_mix:
        y = jnp.concatenate([ya[...], yb[...], yc[...], yd[...]], axis=-1)
        x = x + _dot(y, wo_ref[...])
    ms = jnp.mean(x * x, axis=-1, keepdims=True)
    xn = (x * lax.rsqrt(ms + EPS) * g_ref[...]).astype(BF16)
    d_ff = wg_ref.shape[1]
    for c in range(d_ff // fc):
        sl = slice(c * fc, (c + 1) * fc)
        gate = _dot(xn, wg_ref[:, sl])
        up = _dot(xn, wu_ref[:, sl])
        h_scr[:, sl] = (gate * jax.nn.sigmoid(gate) * up).astype(BF16)
    o_ref[...] = x + 0.5 * _dot(h_scr[...], wd_ref[...])


def _ffn(x2, g, wg, wu, wd, layer, mix=None, tm=512, fc=256):
    m, d = x2.shape
    d_ff = wg.shape[2]
    tm = min(tm, m)
    row = lambda w: pl.BlockSpec((tm, w), lambda i: (i, 0))
    in_specs = [row(d)]
    args = [x2]
    if mix is not None:
        ys, wo = mix
        in_specs += [row(GROUP_WIDTH)] * 4 + [_const_spec(wo.shape, layer)]
        args += list(ys) + [wo]
    in_specs += [_const_spec(a.shape, layer) for a in (g, wg, wu, wd)]
    args += [g, wg, wu, wd]
    return pl.pallas_call(
        functools.partial(_ffn_body, mix is not None, fc),
        grid=(m // tm,),
        in_specs=in_specs,
        out_specs=row(d),
        out_shape=jax.ShapeDtypeStruct((m, d), F32),
        scratch_shapes=[pltpu.VMEM((tm, d_ff), BF16)],
        compiler_params=_cparams(("parallel",)),
        name="ffn_mix" if mix is not None else "ffn",
    )(*args)


def _mix_in_body(tm, blk,
                 x_ref, gmix_ref, wn_ref, wt_ref, bd_ref,
                 gaq_ref, gak_ref,
                 wgu_ref, bgb_ref,
                 gqa_ref, wuq_ref, gdq_ref, cosT_ref, sinT_ref,
                 gkva_ref, wuk_ref, wuvT_ref, gdk_ref, gdkpe_ref, cpe_ref, spe_ref,
                 aqT_ref, ak_ref, avT_ref, iqT_ref, ik_ref, iwT_ref,
                 bq_ref, bk_ref, bv_ref, bla_ref, br_ref,
                 ch_ref,
                 dqT_ref, dk_ref, dvT_ref):
    nlt = tm // LANE
    x = x_ref[0]
    ms = jnp.mean(x * x, axis=-1, keepdims=True)
    xn = (x * lax.rsqrt(ms + EPS) * gmix_ref[...]).astype(BF16)
    bd = bd_ref[...]
    lane = lax.broadcasted_iota(I32, (tm, LANE), 1)

    def zs(off, width):
        return _dot(xn, wn_ref[:, off:off + width])

    def lanes(g):
        return jnp.tile(g, (1, nlt))

    zt = _dot_nt(wt_ref[...], xn)

    aq = zt[T_AQ:T_AQ + A_HEADS * HEAD_PAD].reshape(A_HEADS, HEAD_PAD, tm)
    aq_ms = jnp.sum(aq * aq, axis=1, keepdims=True) * (1.0 / A_HEAD_DIM)
    aq = aq * lax.rsqrt(aq_ms + EPS) * lanes(gaq_ref[...])[None] * (A_HEAD_DIM ** -0.5 * LOG2E)
    aqT_ref[0] = aq.reshape(A_HEADS * HEAD_PAD, tm).astype(BF16)
    av = zt[T_AV:T_AV + GROUP_WIDTH].astype(BF16)
    for c in range(tm // blk):
        avT_ref[0, c] = av[:, c * blk:(c + 1) * blk]
    iqT_ref[0] = zt[T_IQ:T_IQ + IDX_HEADS * IDX_DIM].astype(BF16)
    iwT_ref[0] = zt[T_IW:T_IW + IDX_HEADS] * ((IDX_HEADS ** -0.5) * (IDX_DIM ** -0.5))

    ak = zs(N_AK, GROUP_WIDTH)
    ak_ms = _group_sum(ak * ak, bd) * (1.0 / A_HEAD_DIM)
    ak = ak * lax.rsqrt(ak_ms + EPS) * gak_ref[...]
    for h in range(A_HEADS):
        pair = ak[:, LANE * (h // 2):LANE * (h // 2) + LANE]
        if h % 2 == 1:
            pair = pltpu.roll(pair, 64, 1)
        ak_ref[0, h] = jnp.where(lane < A_HEAD_DIM, pair, 0.0).astype(BF16)
    ik_ref[0] = zs(N_IK, LANE)[:, :IDX_DIM].astype(BF16)

    bq_ref[0] = zs(N_BQ, LANE) * (B_KEY_DIM ** -0.5)
    bk_ref[0] = zs(N_BK, LANE)
    bv_ref[0] = zs(N_BV, GROUP_WIDTH).astype(BF16)
    glat = zs(N_BG, LANE).astype(BF16)
    gate = _dot(glat, wgu_ref[...]) + bgb_ref[...]
    bla_ref[0] = (jnp.minimum(gate, 0.0) - jnp.log(1.0 + jnp.exp(-jnp.abs(gate)))) * (1.0 / B_GATE_TAU)
    br_ref[0] = zs(N_BR, GROUP_WIDTH)

    ca = zs(N_CU, C_CHANNELS)
    cg = zs(N_CU + C_CHANNELS, C_CHANNELS)
    ch_ref[0] = ca * jax.nn.sigmoid(cg)

    cq = zs(N_DCQ, D_Q_RANK)
    cq_ms = jnp.mean(cq * cq, axis=-1, keepdims=True)
    cqn = (cq * lax.rsqrt(cq_ms + EPS) * gqa_ref[...]).astype(BF16)
    dq = _dot_nt(wuq_ref[...], cqn).reshape(D_HEADS, HEAD_PAD, tm)
    dq_ms = jnp.sum(dq * dq, axis=1, keepdims=True) * (1.0 / D_QK)
    dq = dq * lax.rsqrt(dq_ms + EPS) * lanes(gdq_ref[...])[None] * (D_QK ** -0.5 * LOG2E)
    half = D_ROPE // 2
    x1 = dq[:, D_NOPE:D_NOPE + half]
    x2 = dq[:, D_NOPE + half:D_QK]
    cs = cosT_ref[...][None]
    sn = sinT_ref[...][None]
    dq = jnp.concatenate([dq[:, :D_NOPE], x1 * cs - x2 * sn, x2 * cs + x1 * sn, dq[:, D_QK:]], axis=1)
    dqT_ref[0] = dq.reshape(D_HEADS * HEAD_PAD, tm).astype(BF16)

    ckv = zs(N_DCKV, D_KV_RANK)
    ckv_ms = jnp.mean(ckv * ckv, axis=-1, keepdims=True)
    ckvn = (ckv * lax.rsqrt(ckv_ms + EPS) * gkva_ref[...]).astype(BF16)
    kn = _dot(ckvn, wuk_ref[...])
    dv = _dot_nt(wuvT_ref[...], ckvn).astype(BF16)
    for c in range(tm // blk):
        dvT_ref[0, c] = dv[:, c * blk:(c + 1) * blk]
    kpe = zs(N_DKPE, LANE)
    ss = _group_sum(kn * kn, bd) + jnp.sum(kpe * kpe, axis=-1, keepdims=True)
    rinv = lax.rsqrt(ss * (1.0 / D_QK) + EPS)
    kn = kn * rinv * gdk_ref[...]
    pe = kpe * gdkpe_ref[...]
    partner = jnp.where(lane < half, pltpu.roll(pe, LANE - half, 1), pltpu.roll(pe, half, 1))
    pe = pe * cpe_ref[...] + partner * spe_ref[...]
    pe = pltpu.roll(pe, D_NOPE, 1)
    for h in range(D_HEADS):
        pair = kn[:, LANE * (h // 2):LANE * (h // 2) + LANE]
        rpair = rinv[:, LANE * (h // 2):LANE * (h // 2) + LANE]
        if h % 2 == 1:
            pair = pltpu.roll(pair, 64, 1)
        else:
            rpair = pltpu.roll(rpair, 64, 1)
        dk_ref[0, h] = jnp.where(lane < D_NOPE, pair, pe * rpair).astype(BF16)


def _mix_in(x3, p, layer, tm=512, blk=ATT_BLK):
    b, l, d = x3.shape
    tm = min(tm, l)
    grid = (b, l // tm)
    nck = l // blk
    tok = lambda w: pl.BlockSpec((1, tm, w), lambda bi, i: (bi, i, 0))
    tokT = lambda r: pl.BlockSpec((1, r, tm), lambda bi, i: (bi, 0, i))
    headk = pl.BlockSpec((1, 4, tm, HEAD_PAD), lambda bi, i: (bi, 0, i, 0))
    chunkT = pl.BlockSpec((1, tm // blk, GROUP_WIDTH, blk), lambda bi, i: (bi, i, 0, 0))
    postab = lambda r: pl.BlockSpec((r, tm), lambda bi, i: (0, i))
    posrow = pl.BlockSpec((tm, LANE), lambda bi, i: (i, 0))
    consts = [p["gmix"], p["wn"], p["wt"], p["bd"], p["gaq"], p["gak"], p["wgu"], p["bgb"],
              p["gqa"], p["wuq"], p["gdq"]]
    consts2 = [p["gkva"], p["wuk"], p["wuvT"], p["gdk"], p["gdkpe"]]
    lspec = lambda a: _const_spec(a.shape) if a is p["bd"] else _const_spec(a.shape, layer)
    in_specs = ([tok(d)] + [lspec(a) for a in consts]
                + [postab(D_ROPE // 2), postab(D_ROPE // 2)]
                + [lspec(a) for a in consts2] + [posrow, posrow])
    args = [x3] + consts + [p["cosT"], p["sinT"]] + consts2 + [p["cpe"], p["spe"]]
    sd = jax.ShapeDtypeStruct
    out_shape = [
        sd((b, A_HEADS * HEAD_PAD, l), BF16), sd((b, A_HEADS, l, HEAD_PAD), BF16),
        sd((b, nck, GROUP_WIDTH, blk), BF16), sd((b, IDX_HEADS * IDX_DIM, l), BF16),
        sd((b, l, IDX_DIM), BF16), sd((b, IDX_HEADS, l), F32),
        sd((b, l, LANE), F32), sd((b, l, LANE), F32), sd((b, l, GROUP_WIDTH), BF16),
        sd((b, l, LANE), F32), sd((b, l, GROUP_WIDTH), F32),
        sd((b, l, C_CHANNELS), F32),
        sd((b, D_HEADS * HEAD_PAD, l), BF16), sd((b, D_HEADS, l, HEAD_PAD), BF16),
        sd((b, nck, GROUP_WIDTH, blk), BF16),
    ]
    out_specs = [
        tokT(A_HEADS * HEAD_PAD), headk, chunkT, tokT(IDX_HEADS * IDX_DIM),
        tok(IDX_DIM), tokT(IDX_HEADS),
        tok(LANE), tok(LANE), tok(GROUP_WIDTH), tok(LANE), tok(GROUP_WIDTH),
        tok(C_CHANNELS),
        tokT(D_HEADS * HEAD_PAD), headk, chunkT,
    ]
    return pl.pallas_call(
        functools.partial(_mix_in_body, tm, blk),
        grid=grid, in_specs=in_specs, out_specs=out_specs, out_shape=out_shape,
        compiler_params=_cparams(("parallel", "parallel")),
        name="mix_in",
    )(*args)


def _attn_body(nh, dv, blk, topk, is_dsa, *refs):
    if is_dsa:
        (qT_ref, k_ref, vT_ref, iqT_ref, wT_ref, ik_ref, bt_ref,
         o_ref, m_scr, l_scr, acc_scr, key_scr, run_scr, hi_scr, lo_scr) = refs
    else:
        qT_ref, k_ref, vT_ref, o_ref, m_scr, l_scr, acc_scr = refs
    i = pl.program_id(1)
    t = blk
    row = lax.broadcasted_iota(I32, (t, t), 0)
    col = lax.broadcasted_iota(I32, (t, t), 1)
    causal_pen = jnp.where(row <= col, 0.0, NEG)

    m_scr[...] = jnp.full(m_scr.shape, NEG, F32)
    l_scr[...] = jnp.zeros(l_scr.shape, F32)
    acc_scr[...] = jnp.zeros(acc_scr.shape, F32)

    if is_dsa:
        def score_chunk(j):
            r0 = pl.multiple_of(j * t, t)
            ikc = ik_ref[0, pl.ds(r0, t), :]
            s = jnp.zeros((t, t), F32)
            for h in range(IDX_HEADS):
                d = _dot(ikc, iqT_ref[0, IDX_DIM * h:IDX_DIM * (h + 1), :])
                s = s + jnp.maximum(d, 0.0) * wT_ref[0, h:h + 1, :]
            bits = lax.bitcast_convert_type(s, I32)
            key = jnp.where(bits < 0, bits ^ 0x7FFFFFFF, bits)
            key = jnp.where(row + j * t <= col + i * t, key, INT_MIN)
            key_scr[pl.ds(r0, t), :] = key
            hi_scr[pl.ds(r0, t), :] = jnp.right_shift(key, 16).astype(I16)
            lo_scr[pl.ds(r0, t), :] = ((key & 0xFFFF) + I16_MIN).astype(I16)

        def score_pair(u, carry):
            score_chunk(2 * u)
            score_chunk(2 * u + 1)
            return carry

        lax.fori_loop(0, (i + 1) // 2, score_pair, 0)

        @pl.when(i % 2 == 0)
        def _():
            score_chunk(i)

        def count_ge16(ref, cand):
            c16 = jnp.broadcast_to(cand.astype(I16), (16, t))
            one, zero = jnp.int16(1), jnp.int16(0)

            def body(j, accs):
                a0, a1 = accs
                kc = ref[pl.ds(pl.multiple_of(j * t, t), t), :]
                for r in range(0, t // 16, 2):
                    a0 = a0 + jnp.where(kc[16 * r:16 * r + 16] >= c16, one, zero)
                    a1 = a1 + jnp.where(kc[16 * r + 16:16 * r + 32] >= c16, one, zero)
                return a0, a1

            z = jnp.zeros((16, t), I16)
            a0, a1 = lax.fori_loop(0, i + 1, body, (z, z))
            return jnp.sum(a0.astype(I32) + a1.astype(I32), axis=0, keepdims=True)

        def bisect16(ref, k):
            ans = jnp.where(count_ge16(ref, jnp.zeros((1, t), I32)) >= k, 0, I16_MIN)

            def bit_body(b, ans):
                cand = ans | jnp.left_shift(jnp.int32(1), 14 - b)
                return jnp.where(count_ge16(ref, cand) >= k, cand, ans)

            return lax.fori_loop(0, 15, bit_body, ans)

        ans_hi = bisect16(hi_scr, jnp.full((1, t), topk, I32))
        above = count_ge16(hi_scr, ans_hi + 1)
        h16 = jnp.broadcast_to(ans_hi.astype(I16), (t, t))

        def mask_lo(j, carry):
            rows = pl.ds(pl.multiple_of(j * t, t), t)
            lo_scr[rows, :] = jnp.where(hi_scr[rows, :] == h16, lo_scr[rows, :], jnp.int16(I16_MIN))
            return carry

        lax.fori_loop(0, i + 1, mask_lo, 0)
        ans_lo = bisect16(lo_scr, topk - above)
        n_gt = above + count_ge16(lo_scr, ans_lo + 1)
        ans = jnp.left_shift(ans_hi, 16) | (ans_lo - I16_MIN)
        need = (topk - n_gt).astype(F32)
        run_scr[...] = jnp.zeros(run_scr.shape, F32)
        stri = jnp.where(col < row, 1.0, 0.0).astype(BF16)

    def logits(h, r0):
        return _dot(k_ref[0, h, pl.ds(r0, t), :], qT_ref[0, HEAD_PAD * h:HEAD_PAD * (h + 1), :])

    def super_chunk(js, last_is_diag):
        n = len(js)
        r0s = [pl.multiple_of(j * t, t) for j in js]
        pens = []
        if is_dsa:
            for c in range(n):
                kc = key_scr[pl.ds(r0s[c], t), :]
                eq = kc == ans
                eqf = jnp.where(eq, 1.0, 0.0)
                run = run_scr[0:1, :]
                rank = _dot(stri, eqf.astype(BF16)) + run
                run_scr[0:1, :] = run + jnp.sum(eqf, axis=0, keepdims=True)
                pens.append(jnp.where(kc > ans, 0.0, jnp.where(eq, jnp.where(rank < need, 0.0, NEG), NEG)))
        lgs = [[logits(h, r0s[c]) for c in range(n)] for h in range(nh)]
        ps, alphas = [], []
        for h in range(nh):
            xs = []
            for c, j in enumerate(js):
                diag = last_is_diag and c == n - 1
                lg = lgs[h][c]
                if is_dsa:
                    tile = 0 if diag else jnp.minimum(i - j, 2)
                    lg = lg + (pens[c] + bt_ref[tile, h])
                elif diag:
                    lg = lg + causal_pen
                xs.append(lg)
            m_old = m_scr[h, 0:1, :]
            m_new = m_old
            for x in xs:
                m_new = jnp.maximum(m_new, jnp.max(x, axis=0, keepdims=True))
            alpha = jnp.exp2(m_old - m_new)
            pf = [jnp.exp2(x - m_new) for x in xs]
            psum = jnp.sum(pf[0], axis=0, keepdims=True)
            for p in pf[1:]:
                psum = psum + jnp.sum(p, axis=0, keepdims=True)
            l_scr[h, 0:1, :] = alpha * l_scr[h, 0:1, :] + psum
            m_scr[h, 0:1, :] = m_new
            ps.append([p.astype(BF16) for p in pf])
            alphas.append(alpha)
        for h in range(nh):
            vs = slice(dv * h, dv * (h + 1))
            pv = _dot(vT_ref[0, js[0], vs, :], ps[h][0])
            for c in range(1, n):
                pv = pv + _dot(vT_ref[0, js[c], vs, :], ps[h][c])
            acc_scr[vs, :] = alphas[h] * acc_scr[vs, :] + pv

    def pair_body(u, carry):
        super_chunk([2 * u, 2 * u + 1], False)
        return carry

    lax.fori_loop(0, i // 2, pair_body, 0)

    @pl.when(i % 2 == 1)
    def _():
        super_chunk([i - 1, i], True)

    @pl.when(i % 2 == 0)
    def _():
        super_chunk([i], True)

    outs = []
    for h in range(nh):
        outs.append(acc_scr[dv * h:dv * (h + 1), :] / l_scr[h, 0:1, :])
    o_ref[0] = jnp.transpose(jnp.concatenate(outs, axis=0)).astype(o_ref.dtype)


def _attention(qT, k, vT, dsa=None, blk=ATT_BLK):
    b, nh, l, _ = k.shape
    dv = vT.shape[2] // nh
    grid = (b, l // blk)
    qspec = lambda r: pl.BlockSpec((1, r, blk), lambda bi, i: (bi, 0, i))
    kspec = pl.BlockSpec((1, nh, l, HEAD_PAD), lambda bi, i: (bi, 0, 0, 0))
    vspec = pl.BlockSpec((1, l // blk, nh * dv, blk), lambda bi, i: (bi, 0, 0, 0))
    in_specs = [qspec(nh * HEAD_PAD), kspec, vspec]
    args = [qT, k, vT]
    scratch = [pltpu.VMEM((nh, 8, blk), F32), pltpu.VMEM((nh, 8, blk), F32),
               pltpu.VMEM((nh * dv, blk), F32)]
    topk = 0
    if dsa is not None:
        iqT, wT, ik, bt = dsa
        topk = min(TOPK_MAX, l // 4)
        in_specs += [qspec(IDX_HEADS * IDX_DIM), qspec(IDX_HEADS),
                     pl.BlockSpec((1, l, IDX_DIM), lambda bi, i: (bi, 0, 0)),
                     _const_spec(bt.shape)]
        args += [iqT, wT, ik, bt]
        scratch += [pltpu.VMEM((l, blk), I32), pltpu.VMEM((8, blk), F32),
                    pltpu.VMEM((l, blk), I16), pltpu.VMEM((l, blk), I16)]
    return pl.pallas_call(
        functools.partial(_attn_body, nh, dv, blk, topk, dsa is not None),
        grid=grid, in_specs=in_specs,
        out_specs=pl.BlockSpec((1, blk, nh * dv), lambda bi, i: (bi, i, 0)),
        out_shape=jax.ShapeDtypeStruct((b, l, nh * dv), BF16),
        scratch_shapes=scratch,
        compiler_params=_cparams(("parallel", "arbitrary")),
        name="dsa_attn" if dsa is not None else "mla_attn",
    )(*args)


def _gla_body(tg, q_ref, k_ref, v_ref, la_ref, r_ref, go_ref, bd_ref, o_ref, st_scr, o_scr):
    @pl.when(pl.program_id(1) == 0)
    def _():
        st_scr[...] = jnp.zeros(st_scr.shape, F32)

    cs = B_CHUNK
    la = la_ref[0]
    rl = lax.broadcasted_iota(I32, (tg, LANE), 0) & (cs - 1)
    b = la
    s = 1
    while s < cs:
        b = b + jnp.where(rl >= s, pltpu.roll(b, s, 0), 0.0)
        s *= 2
    q = q_ref[0]
    k = k_ref[0]
    qb = q * jnp.exp(b)
    ci = lax.broadcasted_iota(I32, (cs, cs), 0)
    cj = lax.broadcasted_iota(I32, (cs, cs), 1)
    for c in range(tg // cs):
        sl = slice(c * cs, (c + 1) * cs)
        bc = b[sl]
        mid = bc[cs // 2:cs // 2 + 1]
        last = bc[cs - 1:cs]
        qe = (q[sl] * jnp.exp(bc - mid)).astype(BF16)
        ke = (k[sl] * jnp.exp(mid - bc)).astype(BF16)
        kd = (k[sl] * jnp.exp(last - bc)).astype(BF16)
        qbc = qb[sl].astype(BF16)
        dl = jnp.exp(last)
        vc = v_ref[0, sl, :]
        for h in range(B_HEADS):
            ks = slice(B_KEY_DIM * h, B_KEY_DIM * (h + 1))
            vs = slice(B_VAL_DIM * h, B_VAL_DIM * (h + 1))
            a = jnp.where(cj <= ci, _dot_nt(qe[:, ks], ke[:, ks]), 0.0)
            st = st_scr[h]
            o_scr[sl, vs] = _dot(a.astype(BF16), vc[:, vs]) + _dot_nt(qbc[:, ks], st.astype(BF16))
            st_scr[h] = st * dl[:, ks] + _dot_tn(vc[:, vs], kd[:, ks])
    o = o_scr[...]
    ms = _group_sum(o * o, bd_ref[...]) * (1.0 / B_VAL_DIM)
    r = r_ref[0]
    o_ref[0] = (o * lax.rsqrt(ms + EPS) * go_ref[...] * (r * jax.nn.sigmoid(r))).astype(o_ref.dtype)


def _gla(bq, bk, bv, bla, br, go, bd, layer, tg=512):
    b, l, _ = bq.shape
    tg = min(tg, l)
    tok = lambda w: pl.BlockSpec((1, tg, w), lambda bi, i: (bi, i, 0))
    return pl.pallas_call(
        functools.partial(_gla_body, tg),
        grid=(b, l // tg),
        in_specs=[tok(LANE), tok(LANE), tok(GROUP_WIDTH), tok(LANE), tok(GROUP_WIDTH),
                  _const_spec(go.shape, layer), _const_spec(bd.shape)],
        out_specs=tok(GROUP_WIDTH),
        out_shape=jax.ShapeDtypeStruct((b, l, GROUP_WIDTH), BF16),
        scratch_shapes=[pltpu.VMEM((B_HEADS, B_VAL_DIM, B_KEY_DIM), F32),
                        pltpu.VMEM((tg, GROUP_WIDTH), F32)],
        compiler_params=_cparams(("parallel", "arbitrary")),
        name="gla",
    )(bq, bk, bv, bla, br, go, bd)


CONV_HIST = 32


def _conv_body(tc, h_ref, w_ref, b_ref, g_ref, o_ref, buf):
    @pl.when(pl.program_id(1) == 0)
    def _():
        buf[0:CONV_HIST, :] = jnp.zeros((CONV_HIST, C_CHANNELS), F32)

    @pl.when(pl.program_id(1) > 0)
    def _():
        buf[0:CONV_HIST, :] = buf[tc:tc + CONV_HIST, :]

    buf[CONV_HIST:CONV_HIST + tc, :] = h_ref[0]
    acc = jnp.zeros((tc, C_CHANNELS), F32) + b_ref[...]
    base = CONV_HIST - (C_KERNEL - 1)
    hb = buf[...]
    rows = tc + CONV_HIST
    for r in range(SUBLANES):
        shifted = hb if r == 0 else pltpu.roll(hb, rows - r, 0)
        for j in range(C_KERNEL):
            if (base + j) % SUBLANES == r:
                a0 = base + j - r
                acc = acc + shifted[a0:a0 + tc, :] * w_ref[j:j + 1, :]
    ms = jnp.mean(acc * acc, axis=-1, keepdims=True)
    y = acc * lax.rsqrt(ms + EPS) * g_ref[...]
    o_ref[0] = (y * jax.nn.sigmoid(y)).astype(o_ref.dtype)


def _conv(ch, w, bias, g, layer, tc=512):
    b, l, c = ch.shape
    tc = min(tc, l)
    tok = pl.BlockSpec((1, tc, c), lambda bi, i: (bi, i, 0))
    return pl.pallas_call(
        functools.partial(_conv_body, tc),
        grid=(b, l // tc),
        in_specs=[tok] + [_const_spec(a.shape, layer) for a in (w, bias, g)],
        out_specs=tok,
        out_shape=jax.ShapeDtypeStruct((b, l, c), BF16),
        scratch_shapes=[pltpu.VMEM((tc + CONV_HIST, c), F32)],
        compiler_params=_cparams(("parallel", "arbitrary")),
        name="conv",
    )(ch, w, bias, g)


def _t5_bucket(dist):
    max_exact = REL_BUCKETS // 2
    d = jnp.maximum(dist, 0)
    df = jnp.maximum(d, 1).astype(F32)
    large = max_exact + (jnp.log(df / max_exact) / math.log(REL_MAX_DIST / max_exact)
                         * (REL_BUCKETS - max_exact)).astype(I32)
    large = jnp.minimum(large, REL_BUCKETS - 1)
    return jnp.where(d < max_exact, d, large)


def _pad_cols(w, width):
    return jnp.pad(w, ((0, 0), (0, width - w.shape[1])))


def _lane_rep(v):
    return jnp.broadcast_to(v[:, None], (v.shape[0], LANE))


def _pad_heads_rows(w, heads, dim):
    w = w.reshape(heads, dim, w.shape[1])
    return jnp.pad(w, ((0, 0), (0, HEAD_PAD - dim), (0, 0))).reshape(heads * HEAD_PAD, -1)


def _split_w_in(w_in):
    widths = (256, 256, 256, 256, 32, 8, 128, 128, 256, 16, 256, 512, 256, 128, 32)
    offs = np.cumsum((0,) + widths)
    return [w_in[:, offs[n]:offs[n + 1]] for n in range(len(widths))]


def _row(v):
    return v[None, :].astype(F32)


def _one_layer_params(w):
    (aq, ak, av, iq, ik, iw, bq, bk, bv, bg, br, cu, dcq, dckv, dkpe) = _split_w_in(w["w_in"])
    wn = jnp.concatenate([ak, _pad_cols(ik, LANE), bq, bk, bv, _pad_cols(bg, LANE), br, cu, dcq, dckv,
                          _pad_cols(dkpe, LANE)], axis=1).astype(BF16)
    wt = jnp.concatenate([_pad_heads_rows(aq.T, A_HEADS, A_HEAD_DIM), av.T, iq.T,
                          jnp.pad(iw.T, ((0, T_TOT - T_IW - IDX_HEADS), (0, 0)))], axis=0).astype(BF16)
    pad_to = lambda v, n: jnp.pad(v, (0, n - v.shape[0]))
    ukv = w["d_ukv"].reshape(D_KV_RANK, D_HEADS, D_NOPE + D_V)
    wuk = ukv[:, :, :D_NOPE].reshape(D_KV_RANK, D_HEADS * D_NOPE)
    wuv = ukv[:, :, D_NOPE:].reshape(D_KV_RANK, D_HEADS * D_V)
    gdk = w["d_k_norm"]
    return dict(
        gmix=_row(w["mix_norm"]), wn=wn, wt=wt,
        gaq=_lane_rep(pad_to(w["a_q_norm"], HEAD_PAD)),
        gak=_row(jnp.tile(w["a_k_norm"], A_HEADS)),
        wgu=jnp.pad(w["b_gate_up"], ((0, LANE - B_GATE_RANK), (0, 0))).astype(BF16),
        bgb=_row(w["b_gate_bias"]),
        gqa=_row(w["d_qa_norm"]),
        wuq=_pad_heads_rows(w["d_uq"].T, D_HEADS, D_QK).astype(BF16),
        gdq=_lane_rep(pad_to(w["d_q_norm"], HEAD_PAD)),
        gkva=_row(w["d_kva_norm"]), wuk=wuk.astype(BF16), wuvT=wuv.T.astype(BF16),
        gdk=_row(jnp.tile(gdk[:D_NOPE], D_HEADS)), gdkpe=_row(pad_to(gdk[D_NOPE:], LANE)),
        gbo=_row(jnp.tile(w["b_out_norm"], B_HEADS)),
        cw=jnp.pad(w["c_dw_w"][:, 0, :], ((0, CONV_HIST - C_KERNEL), (0, 0))).astype(F32),
        cb=_row(w["c_dw_b"]), cg=_row(w["c_norm"]),
    )


def _shared_tables(seq):
    hid = np.arange(GROUP_WIDTH) // 64
    bd = jnp.asarray(hid[:, None] == hid[None, :], dtype=BF16)
    half = D_ROPE // 2
    freqs = ROPE_THETA ** (-jnp.arange(half, dtype=F32) / half)
    ang = jnp.arange(seq).astype(F32)[:, None] * freqs[None, :]
    cos, sin = jnp.cos(ang), jnp.sin(ang)
    zeros = jnp.zeros((seq, LANE - D_ROPE), F32)
    cpe = jnp.concatenate([cos, cos, zeros], axis=1)
    spe = jnp.concatenate([-sin, sin, zeros], axis=1)
    return dict(bd=bd, cosT=cos.T, sinT=sin.T, cpe=cpe, spe=spe)


def _bias_tiles(rel_bias, blk):
    assert REL_MAX_DIST <= blk + 1
    kk = jnp.arange(blk)[:, None]
    qq = jnp.arange(blk)[None, :]
    rb = rel_bias.astype(F32).T

    def lookup(bucket):
        onehot = bucket[None, :, :, None] == jnp.arange(REL_BUCKETS)
        return jnp.sum(jnp.where(onehot, rb[:, None, None, :], 0.0), axis=-1) * LOG2E

    d0 = jnp.where(kk <= qq, lookup(_t5_bucket(qq - kk)), NEG)
    d1 = lookup(_t5_bucket(blk + qq - kk))
    far = jnp.broadcast_to(lookup(_t5_bucket(jnp.full((1, 1), 2 * blk, I32))), d1.shape)
    return jnp.stack([d0, d1, far])


def kernel(x, ffn1_norm, ffn1_gate, ffn1_up, ffn1_down, mix_norm, w_in, a_q_norm, a_k_norm, rel_bias,
           b_gate_up, b_gate_bias, b_out_norm, c_dw_w, c_dw_b, c_norm, d_qa_norm, d_uq, d_kva_norm,
           d_ukv, d_q_norm, d_k_norm, w_out, ffn2_norm, ffn2_gate, ffn2_up, ffn2_down):
    w = dict(mix_norm=mix_norm, w_in=w_in, a_q_norm=a_q_norm, a_k_norm=a_k_norm, b_gate_up=b_gate_up,
             b_gate_bias=b_gate_bias, b_out_norm=b_out_norm, c_dw_w=c_dw_w, c_dw_b=c_dw_b, c_norm=c_norm,
             d_qa_norm=d_qa_norm, d_uq=d_uq, d_kva_norm=d_kva_norm, d_ukv=d_ukv,
             d_q_norm=d_q_norm, d_k_norm=d_k_norm)
    bsz, seq, dm = x.shape
    depth = w_in.shape[0]
    blk = min(ATT_BLK, seq)
    bt = _bias_tiles(rel_bias, blk)
    p = {**jax.vmap(_one_layer_params)(w), **_shared_tables(seq)}
    stacked_row = lambda v: v[:, None, :].astype(F32)
    ffn1 = (stacked_row(ffn1_norm), ffn1_gate.astype(BF16), ffn1_up.astype(BF16), ffn1_down.astype(BF16))
    ffn2 = (stacked_row(ffn2_norm), ffn2_gate.astype(BF16), ffn2_up.astype(BF16), ffn2_down.astype(BF16))
    wo = w_out.astype(BF16)
    x2 = x.reshape(bsz * seq, dm)
    for l in range(depth):
        x2 = _ffn(x2, *ffn1, l)
        (aqT, akh, avT, iqT, aik, iwT, bq, bk, bv, bla, br, ch, dqT, dkh, dvT) = _mix_in(
            x2.reshape(bsz, seq, dm), p, l, blk=blk)
        y_a = _attention(aqT, akh, avT, dsa=(iqT, iwT, aik, bt), blk=blk)
        y_b = _gla(bq, bk, bv, bla, br, p["gbo"], p["bd"], l)
        y_c = _conv(ch, p["cw"], p["cb"], p["cg"], l)
        y_d = _attention(dqT, dkh, dvT, blk=blk)
        ys = [y.reshape(bsz * seq, GROUP_WIDTH) for y in (y_a, y_b, y_c, y_d)]
        x2 = _ffn(x2, *ffn2, l, mix=(ys, wo))
    return x2.reshape(bsz, seq, dm)
```

```python
import functools
import math

import jax
import jax.numpy as jnp
import numpy as np
from jax import lax
from jax.experimental import pallas as pl
from jax.experimental.pallas import tpu as pltpu

F32 = jnp.float32
BF16 = jnp.bfloat16
I32 = jnp.int32
I16 = jnp.int16

EPS = 1e-6
GROUP_WIDTH = 256
A_HEADS, A_HEAD_DIM = 4, 64
IDX_HEADS, IDX_DIM = 8, 32
TOPK_MAX = 256
REL_BUCKETS, REL_MAX_DIST = 32, 128
B_HEADS, B_KEY_DIM, B_VAL_DIM, B_GATE_RANK = 4, 32, 64, 16
B_GATE_TAU = 16.0
B_CHUNK = 64
C_CHANNELS, C_KERNEL = 256, 31
D_HEADS, D_Q_RANK, D_KV_RANK, D_NOPE, D_ROPE, D_V = 4, 256, 128, 64, 32, 64
D_QK = D_NOPE + D_ROPE
ROPE_THETA = 10000.0

LANE = 128
SUBLANES = 8
HEAD_PAD = 128
ATT_BLK = 256
INT_MIN = -2 ** 31
I16_MIN = -2 ** 15
NEG = -1e30
LOG2E = math.log2(math.e)
VMEM_LIMIT = 56 * 1024 * 1024

N_AK, N_IK, N_BQ, N_BK, N_BV, N_BG, N_BR, N_CU, N_DCQ, N_DCKV, N_DKPE, N_TOT = (
    0, 256, 384, 512, 640, 896, 1024, 1280, 1792, 2048, 2176, 2304)
T_AQ, T_AV, T_IQ, T_IW, T_TOT = 0, 512, 768, 1024, 1040


def _dot(a, b):
    return jnp.dot(a, b, preferred_element_type=F32)


def _dot_nt(a, b):
    return lax.dot_general(a, b, (((1,), (1,)), ((), ())), preferred_element_type=F32)


def _dot_tn(a, b):
    return lax.dot_general(a, b, (((0,), (0,)), ((), ())), preferred_element_type=F32)


def _group_sum(x2, bd):
    hi = x2.astype(BF16)
    lo = (x2 - hi.astype(F32)).astype(BF16)
    return _dot(hi, bd) + _dot(lo, bd)


def _const_spec(shape, layer=None):
    nd = len(shape)
    if layer is None:
        return pl.BlockSpec(shape, lambda *_: (0,) * nd, pipeline_mode=pl.Buffered(1))
    return pl.BlockSpec((None,) + tuple(shape[1:]), lambda *_: (layer,) + (0,) * (nd - 1),
                        pipeline_mode=pl.Buffered(1))


def _cparams(sem):
    return pltpu.CompilerParams(dimension_semantics=sem, vmem_limit_bytes=VMEM_LIMIT)


def _ffn_body(has_mix, fc, *refs):
    if has_mix:
        x_ref, ya, yb, yc, yd, wo_ref, g_ref, wg_ref, wu_ref, wd_ref, o_ref, h_scr = refs
    else:
        x_ref, g_ref, wg_ref, wu_ref, wd_ref, o_ref, h_scr = refs
    x = x_ref[...]
    if has_mix:
        y = jnp.concatenate([ya[...], yb[...], yc[...], yd[...]], axis=-1)
        x = x + _dot(y, wo_ref[...])
    ms = jnp.mean(x * x, axis=-1, keepdims=True)
    xn = (x * lax.rsqrt(ms + EPS) * g_ref[...]).astype(BF16)
    d_ff = wg_ref.shape[1]
    for c in range(d_ff // fc):
        sl = slice(c * fc, (c + 1) * fc)
        gate = _dot(xn, wg_ref[:, sl])
        up = _dot(xn, wu_ref[:, sl])
        h_scr[:, sl] = (gate * jax.nn.sigmoid(gate) * up).astype(BF16)
    o_ref[...] = x + 0.5 * _dot(h_scr[...], wd_ref[...])


def _ffn(x2, g, wg, wu, wd, layer, mix=None, tm=512, fc=256):
    m, d = x2.shape
    d_ff = wg.shape[2]
    tm = min(tm, m)
    row = lambda w: pl.BlockSpec((tm, w), lambda i: (i, 0))
    in_specs = [row(d)]
    args = [x2]
    if mix is not None:
        ys, wo = mix
        in_specs += [row(GROUP_WIDTH)] * 4 + [_const_spec(wo.shape, layer)]
        args += list(ys) + [wo]
    in_specs += [_const_spec(a.shape, layer) for a in (g, wg, wu, wd)]
    args += [g, wg, wu, wd]
    return pl.pallas_call(
        functools.partial(_ffn_body, mix is not None, fc),
        grid=(m // tm,),
        in_specs=in_specs,
        out_specs=row(d),
        out_shape=jax.ShapeDtypeStruct((m, d), F32),
        scratch_shapes=[pltpu.VMEM((tm, d_ff), BF16)],
        compiler_params=_cparams(("parallel",)),
        name="ffn_mix" if mix is not None else "ffn",
    )(*args)


def _mix_in_body(tm, blk,
                 x_ref, gmix_ref, wn_ref, wt_ref, bd_ref,
                 gaq_ref, gak_ref,
                 wgu_ref, bgb_ref,
                 gqa_ref, wuq_ref, gdq_ref, cosT_ref, sinT_ref,
                 gkva_ref, wuk_ref, wuvT_ref, gdk_ref, gdkpe_ref, cpe_ref, spe_ref,
                 aqT_ref, ak_ref, avT_ref, iqT_ref, ik_ref, iwT_ref,
                 bq_ref, bk_ref, bv_ref, bla_ref, br_ref,
                 ch_ref,
                 dqT_ref, dk_ref, dvT_ref):
    nlt = tm // LANE
    x = x_ref[0]
    ms = jnp.mean(x * x, axis=-1, keepdims=True)
    xn = (x * lax.rsqrt(ms + EPS) * gmix_ref[...]).astype(BF16)
    bd = bd_ref[...]
    lane = lax.broadcasted_iota(I32, (tm, LANE), 1)

    def lanes(g):
        return jnp.tile(g, (1, nlt))

    z = _dot(xn, wn_ref[...])
    zt = _dot_nt(wt_ref[...], xn)

    def zs(off, width):
        return z[:, off:off + width]

    cq = zs(N_DCQ, D_Q_RANK)
    cq_ms = jnp.mean(cq * cq, axis=-1, keepdims=True)
    cqn = (cq * lax.rsqrt(cq_ms + EPS) * gqa_ref[...]).astype(BF16)
    ckv = zs(N_DCKV, D_KV_RANK)
    ckv_ms = jnp.mean(ckv * ckv, axis=-1, keepdims=True)
    ckvn = (ckv * lax.rsqrt(ckv_ms + EPS) * gkva_ref[...]).astype(BF16)
    dq = _dot_nt(wuq_ref[...], cqn).reshape(D_HEADS, HEAD_PAD, tm)
    kn = _dot(ckvn, wuk_ref[...])
    dv = _dot_nt(wuvT_ref[...], ckvn).astype(BF16)
    gate = _dot(zs(N_BG, LANE).astype(BF16), wgu_ref[...]) + bgb_ref[...]
    ak = zs(N_AK, GROUP_WIDTH)
    ak_ms = _group_sum(ak * ak, bd) * (1.0 / A_HEAD_DIM)
    kn_ss = _group_sum(kn * kn, bd)

    aq = zt[T_AQ:T_AQ + A_HEADS * HEAD_PAD].reshape(A_HEADS, HEAD_PAD, tm)
    aq_ms = jnp.sum(aq * aq, axis=1, keepdims=True) * (1.0 / A_HEAD_DIM)
    aq = aq * lax.rsqrt(aq_ms + EPS) * lanes(gaq_ref[...])[None] * (A_HEAD_DIM ** -0.5 * LOG2E)
    aqT_ref[0] = aq.reshape(A_HEADS * HEAD_PAD, tm).astype(BF16)
    av = zt[T_AV:T_AV + GROUP_WIDTH].astype(BF16)
    for c in range(tm // blk):
        avT_ref[0, c] = av[:, c * blk:(c + 1) * blk]
    iqT_ref[0] = zt[T_IQ:T_IQ + IDX_HEADS * IDX_DIM].astype(BF16)
    iwT_ref[0] = zt[T_IW:T_IW + IDX_HEADS] * ((IDX_HEADS ** -0.5) * (IDX_DIM ** -0.5))

    ak = ak * lax.rsqrt(ak_ms + EPS) * gak_ref[...]
    for h in range(A_HEADS):
        pair = ak[:, LANE * (h // 2):LANE * (h // 2) + LANE]
        if h % 2 == 1:
            pair = pltpu.roll(pair, 64, 1)
        ak_ref[0, h] = jnp.where(lane < A_HEAD_DIM, pair, 0.0).astype(BF16)
    ik_ref[0] = zs(N_IK, LANE)[:, :IDX_DIM].astype(BF16)

    bq_ref[0] = zs(N_BQ, LANE) * (B_KEY_DIM ** -0.5)
    bk_ref[0] = zs(N_BK, LANE)
    bv_ref[0] = zs(N_BV, GROUP_WIDTH).astype(BF16)
    bla_ref[0] = (jnp.minimum(gate, 0.0) - jnp.log(1.0 + jnp.exp(-jnp.abs(gate)))) * (1.0 / B_GATE_TAU)
    br_ref[0] = zs(N_BR, GROUP_WIDTH)

    ca = zs(N_CU, C_CHANNELS)
    cg = zs(N_CU + C_CHANNELS, C_CHANNELS)
    ch_ref[0] = ca * jax.nn.sigmoid(cg)

    dq_ms = jnp.sum(dq * dq, axis=1, keepdims=True) * (1.0 / D_QK)
    dq = dq * lax.rsqrt(dq_ms + EPS) * lanes(gdq_ref[...])[None] * (D_QK ** -0.5 * LOG2E)
    half = D_ROPE // 2
    x1 = dq[:, D_NOPE:D_NOPE + half]
    x2 = dq[:, D_NOPE + half:D_QK]
    cs = cosT_ref[...][None]
    sn = sinT_ref[...][None]
    dq = jnp.concatenate([dq[:, :D_NOPE], x1 * cs - x2 * sn, x2 * cs + x1 * sn, dq[:, D_QK:]], axis=1)
    dqT_ref[0] = dq.reshape(D_HEADS * HEAD_PAD, tm).astype(BF16)

    for c in range(tm // blk):
        dvT_ref[0, c] = dv[:, c * blk:(c + 1) * blk]
    kpe = zs(N_DKPE, LANE)
    ss = kn_ss + jnp.sum(kpe * kpe, axis=-1, keepdims=True)
    rinv = lax.rsqrt(ss * (1.0 / D_QK) + EPS)
    kn = kn * rinv * gdk_ref[...]
    pe = kpe * gdkpe_ref[...]
    partner = jnp.where(lane < half, pltpu.roll(pe, LANE - half, 1), pltpu.roll(pe, half, 1))
    pe = pe * cpe_ref[...] + partner * spe_ref[...]
    pe = pltpu.roll(pe, D_NOPE, 1)
    for h in range(D_HEADS):
        pair = kn[:, LANE * (h // 2):LANE * (h // 2) + LANE]
        rpair = rinv[:, LANE * (h // 2):LANE * (h // 2) + LANE]
        if h % 2 == 1:
            pair = pltpu.roll(pair, 64, 1)
        else:
            rpair = pltpu.roll(rpair, 64, 1)
        dk_ref[0, h] = jnp.where(lane < D_NOPE, pair, pe * rpair).astype(BF16)


def _mix_in(x3, p, layer, tm=512, blk=ATT_BLK):
    b, l, d = x3.shape
    tm = min(tm, l)
    grid = (b, l // tm)
    nck = l // blk
    tok = lambda w: pl.BlockSpec((1, tm, w), lambda bi, i: (bi, i, 0))
    tokT = lambda r: pl.BlockSpec((1, r, tm), lambda bi, i: (bi, 0, i))
    headk = pl.BlockSpec((1, 4, tm, HEAD_PAD), lambda bi, i: (bi, 0, i, 0))
    chunkT = pl.BlockSpec((1, tm // blk, GROUP_WIDTH, blk), lambda bi, i: (bi, i, 0, 0))
    postab = lambda r: pl.BlockSpec((r, tm), lambda bi, i: (0, i))
    posrow = pl.BlockSpec((tm, LANE), lambda bi, i: (i, 0))
    consts = [p["gmix"], p["wn"], p["wt"], p["bd"], p["gaq"], p["gak"], p["wgu"], p["bgb"],
              p["gqa"], p["wuq"], p["gdq"]]
    consts2 = [p["gkva"], p["wuk"], p["wuvT"], p["gdk"], p["gdkpe"]]
    lspec = lambda a: _const_spec(a.shape) if a is p["bd"] else _const_spec(a.shape, layer)
    in_specs = ([tok(d)] + [lspec(a) for a in consts]
                + [postab(D_ROPE // 2), postab(D_ROPE // 2)]
                + [lspec(a) for a in consts2] + [posrow, posrow])
    args = [x3] + consts + [p["cosT"], p["sinT"]] + consts2 + [p["cpe"], p["spe"]]
    sd = jax.ShapeDtypeStruct
    out_shape = [
        sd((b, A_HEADS * HEAD_PAD, l), BF16), sd((b, A_HEADS, l, HEAD_PAD), BF16),
        sd((b, nck, GROUP_WIDTH, blk), BF16), sd((b, IDX_HEADS * IDX_DIM, l), BF16),
        sd((b, l, IDX_DIM), BF16), sd((b, IDX_HEADS, l), F32),
        sd((b, l, LANE), F32), sd((b, l, LANE), F32), sd((b, l, GROUP_WIDTH), BF16),
        sd((b, l, LANE), F32), sd((b, l, GROUP_WIDTH), F32),
        sd((b, l, C_CHANNELS), F32),
        sd((b, D_HEADS * HEAD_PAD, l), BF16), sd((b, D_HEADS, l, HEAD_PAD), BF16),
        sd((b, nck, GROUP_WIDTH, blk), BF16),
    ]
    out_specs = [
        tokT(A_HEADS * HEAD_PAD), headk, chunkT, tokT(IDX_HEADS * IDX_DIM),
        tok(IDX_DIM), tokT(IDX_HEADS),
        tok(LANE), tok(LANE), tok(GROUP_WIDTH), tok(LANE), tok(GROUP_WIDTH),
        tok(C_CHANNELS),
        tokT(D_HEADS * HEAD_PAD), headk, chunkT,
    ]
    return pl.pallas_call(
        functools.partial(_mix_in_body, tm, blk),
        grid=grid, in_specs=in_specs, out_specs=out_specs, out_shape=out_shape,
        compiler_params=_cparams(("parallel", "parallel")),
        name="mix_in",
    )(*args)


def _attn_body(nh, dv, blk, topk, is_dsa, *refs):
    if is_dsa:
        (qT_ref, k_ref, vT_ref, iqT_ref, wT_ref, ik_ref, bt_ref,
         o_ref, m_scr, l_scr, acc_scr, key_scr, run_scr, hi_scr, lo_scr) = refs
    else:
        qT_ref, k_ref, vT_ref, o_ref, m_scr, l_scr, acc_scr = refs
    i = pl.program_id(1)
    t = blk
    row = lax.broadcasted_iota(I32, (t, t), 0)
    col = lax.broadcasted_iota(I32, (t, t), 1)
    causal_pen = jnp.where(row <= col, 0.0, NEG)

    m_scr[...] = jnp.full(m_scr.shape, NEG, F32)
    l_scr[...] = jnp.zeros(l_scr.shape, F32)
    acc_scr[...] = jnp.zeros(acc_scr.shape, F32)

    if is_dsa:
        def score_chunk(j):
            r0 = pl.multiple_of(j * t, t)
            ikc = ik_ref[0, pl.ds(r0, t), :]
            s = jnp.zeros((t, t), F32)
            for h in range(IDX_HEADS):
                d = _dot(ikc, iqT_ref[0, IDX_DIM * h:IDX_DIM * (h + 1), :])
                s = s + jnp.maximum(d, 0.0) * wT_ref[0, h:h + 1, :]
            bits = lax.bitcast_convert_type(s, I32)
            key = jnp.where(bits < 0, bits ^ 0x7FFFFFFF, bits)
            key = jnp.where(row + j * t <= col + i * t, key, INT_MIN)
            key_scr[pl.ds(r0, t), :] = key
            hi_scr[pl.ds(r0, t), :] = jnp.right_shift(key, 16).astype(I16)
            lo_scr[pl.ds(r0, t), :] = ((key & 0xFFFF) + I16_MIN).astype(I16)

        def score_pair(u, carry):
            score_chunk(2 * u)
            score_chunk(2 * u + 1)
            return carry

        lax.fori_loop(0, (i + 1) // 2, score_pair, 0)

        @pl.when(i % 2 == 0)
        def _():
            score_chunk(i)

        def count_ge16(ref, cand):
            c16 = jnp.broadcast_to(cand.astype(I16), (16, t))
            one, zero = jnp.int16(1), jnp.int16(0)

            def body(j, accs):
                a0, a1 = accs
                kc = ref[pl.ds(pl.multiple_of(j * t, t), t), :]
                for r in range(0, t // 16, 2):
                    a0 = a0 + jnp.where(kc[16 * r:16 * r + 16] >= c16, one, zero)
                    a1 = a1 + jnp.where(kc[16 * r + 16:16 * r + 32] >= c16, one, zero)
                return a0, a1

            z = jnp.zeros((16, t), I16)
            a0, a1 = lax.fori_loop(0, i + 1, body, (z, z))
            return jnp.sum(a0.astype(I32) + a1.astype(I32), axis=0, keepdims=True)

        def bisect16(ref, k):
            ans = jnp.where(count_ge16(ref, jnp.zeros((1, t), I32)) >= k, 0, I16_MIN)

            def bit_body(b, ans):
                cand = ans | jnp.left_shift(jnp.int32(1), 14 - b)
                return jnp.where(count_ge16(ref, cand) >= k, cand, ans)

            return lax.fori_loop(0, 15, bit_body, ans)

        ans_hi = bisect16(hi_scr, jnp.full((1, t), topk, I32))
        above = count_ge16(hi_scr, ans_hi + 1)
        h16 = jnp.broadcast_to(ans_hi.astype(I16), (t, t))

        def mask_lo(j, carry):
            rows = pl.ds(pl.multiple_of(j * t, t), t)
            lo_scr[rows, :] = jnp.where(hi_scr[rows, :] == h16, lo_scr[rows, :], jnp.int16(I16_MIN))
            return carry

        lax.fori_loop(0, i + 1, mask_lo, 0)
        ans_lo = bisect16(lo_scr, topk - above)
        n_gt = above + count_ge16(lo_scr, ans_lo + 1)
        ans = jnp.left_shift(ans_hi, 16) | (ans_lo - I16_MIN)
        need = (topk - n_gt).astype(F32)
        run_scr[...] = jnp.zeros(run_scr.shape, F32)
        stri = jnp.where(col < row, 1.0, 0.0).astype(BF16)

    def logits(h, r0):
        return _dot(k_ref[0, h, pl.ds(r0, t), :], qT_ref[0, HEAD_PAD * h:HEAD_PAD * (h + 1), :])

    def super_chunk(js, last_is_diag):
        n = len(js)
        r0s = [pl.multiple_of(j * t, t) for j in js]
        lgs = [[logits(h, r0s[c]) for c in range(n)] for h in range(nh)]
        pens = []
        if is_dsa:
            for c in range(n):
                kc = key_scr[pl.ds(r0s[c], t), :]
                eq = kc == ans
                eqf = jnp.where(eq, 1.0, 0.0)
                run = run_scr[0:1, :]
                rank = _dot(stri, eqf.astype(BF16)) + run
                run_scr[0:1, :] = run + jnp.sum(eqf, axis=0, keepdims=True)
                pens.append(jnp.where(kc > ans, 0.0, jnp.where(eq, jnp.where(rank < need, 0.0, NEG), NEG)))
        ps, alphas = [], []
        for h in range(nh):
            xs = []
            for c, j in enumerate(js):
                diag = last_is_diag and c == n - 1
                lg = lgs[h][c]
                if is_dsa:
                    tile = 0 if diag else jnp.minimum(i - j, 2)
                    lg = lg + (pens[c] + bt_ref[tile, h])
                elif diag:
                    lg = lg + causal_pen
                xs.append(lg)
            m_old = m_scr[h, 0:1, :]
            m_new = m_old
            for x in xs:
                m_new = jnp.maximum(m_new, jnp.max(x, axis=0, keepdims=True))
            alpha = jnp.exp2(m_old - m_new)
            pf = [jnp.exp2(x - m_new) for x in xs]
            psum = jnp.sum(pf[0], axis=0, keepdims=True)
            for p in pf[1:]:
                psum = psum + jnp.sum(p, axis=0, keepdims=True)
            l_scr[h, 0:1, :] = alpha * l_scr[h, 0:1, :] + psum
            m_scr[h, 0:1, :] = m_new
            ps.append([p.astype(BF16) for p in pf])
            alphas.append(alpha)
        for h in range(nh):
            vs = slice(dv * h, dv * (h + 1))
            pv = _dot(vT_ref[0, js[0], vs, :], ps[h][0])
            for c in range(1, n):
                pv = pv + _dot(vT_ref[0, js[c], vs, :], ps[h][c])
            acc_scr[vs, :] = alphas[h] * acc_scr[vs, :] + pv

    def pair_body(u, carry):
        super_chunk([2 * u, 2 * u + 1], False)
        return carry

    lax.fori_loop(0, i // 2, pair_body, 0)

    @pl.when(i % 2 == 1)
    def _():
        super_chunk([i - 1, i], True)

    @pl.when(i % 2 == 0)
    def _():
        super_chunk([i], True)

    outs = []
    for h in range(nh):
        outs.append(acc_scr[dv * h:dv * (h + 1), :] / l_scr[h, 0:1, :])
    o_ref[0] = jnp.transpose(jnp.concatenate(outs, axis=0)).astype(o_ref.dtype)


def _attention(qT, k, vT, dsa=None, blk=ATT_BLK):
    b, nh, l, _ = k.shape
    dv = vT.shape[2] // nh
    grid = (b, l // blk)
    qspec = lambda r: pl.BlockSpec((1, r, blk), lambda bi, i: (bi, 0, i))
    kspec = pl.BlockSpec((1, nh, l, HEAD_PAD), lambda bi, i: (bi, 0, 0, 0))
    vspec = pl.BlockSpec((1, l // blk, nh * dv, blk), lambda bi, i: (bi, 0, 0, 0))
    in_specs = [qspec(nh * HEAD_PAD), kspec, vspec]
    args = [qT, k, vT]
    scratch = [pltpu.VMEM((nh, 8, blk), F32), pltpu.VMEM((nh, 8, blk), F32),
               pltpu.VMEM((nh * dv, blk), F32)]
    topk = 0
    if dsa is not None:
        iqT, wT, ik, bt = dsa
        topk = min(TOPK_MAX, l // 4)
        in_specs += [qspec(IDX_HEADS * IDX_DIM), qspec(IDX_HEADS),
                     pl.BlockSpec((1, l, IDX_DIM), lambda bi, i: (bi, 0, 0)),
                     _const_spec(bt.shape)]
        args += [iqT, wT, ik, bt]
        scratch += [pltpu.VMEM((l, blk), I32), pltpu.VMEM((8, blk), F32),
                    pltpu.VMEM((l, blk), I16), pltpu.VMEM((l, blk), I16)]
    return pl.pallas_call(
        functools.partial(_attn_body, nh, dv, blk, topk, dsa is not None),
        grid=grid, in_specs=in_specs,
        out_specs=pl.BlockSpec((1, blk, nh * dv), lambda bi, i: (bi, i, 0)),
        out_shape=jax.ShapeDtypeStruct((b, l, nh * dv), BF16),
        scratch_shapes=scratch,
        compiler_params=_cparams(("parallel", "arbitrary")),
        name="dsa_attn" if dsa is not None else "mla_attn",
    )(*args)


def _gla_body(tg, q_ref, k_ref, v_ref, la_ref, r_ref, go_ref, bd_ref, o_ref, st_scr, o_scr):
    @pl.when(pl.program_id(1) == 0)
    def _():
        st_scr[...] = jnp.zeros(st_scr.shape, F32)

    cs = B_CHUNK
    la = la_ref[0]
    rl = lax.broadcasted_iota(I32, (tg, LANE), 0) & (cs - 1)
    b = la
    s = 1
    while s < cs:
        b = b + jnp.where(rl >= s, pltpu.roll(b, s, 0), 0.0)
        s *= 2
    q = q_ref[0]
    k = k_ref[0]
    qb = q * jnp.exp(b)
    ci = lax.broadcasted_iota(I32, (cs, cs), 0)
    cj = lax.broadcasted_iota(I32, (cs, cs), 1)
    for c in range(tg // cs):
        sl = slice(c * cs, (c + 1) * cs)
        bc = b[sl]
        mid = bc[cs // 2:cs // 2 + 1]
        last = bc[cs - 1:cs]
        qe = (q[sl] * jnp.exp(bc - mid)).astype(BF16)
        ke = (k[sl] * jnp.exp(mid - bc)).astype(BF16)
        kd = (k[sl] * jnp.exp(last - bc)).astype(BF16)
        qbc = qb[sl].astype(BF16)
        dl = jnp.exp(last)
        vc = v_ref[0, sl, :]
        for h in range(B_HEADS):
            ks = slice(B_KEY_DIM * h, B_KEY_DIM * (h + 1))
            vs = slice(B_VAL_DIM * h, B_VAL_DIM * (h + 1))
            a = jnp.where(cj <= ci, _dot_nt(qe[:, ks], ke[:, ks]), 0.0)
            st = st_scr[h]
            o_scr[sl, vs] = _dot(a.astype(BF16), vc[:, vs]) + _dot_nt(qbc[:, ks], st.astype(BF16))
            st_scr[h] = st * dl[:, ks] + _dot_tn(vc[:, vs], kd[:, ks])
    o = o_scr[...]
    ms = _group_sum(o * o, bd_ref[...]) * (1.0 / B_VAL_DIM)
    r = r_ref[0]
    o_ref[0] = (o * lax.rsqrt(ms + EPS) * go_ref[...] * (r * jax.nn.sigmoid(r))).astype(o_ref.dtype)


def _gla(bq, bk, bv, bla, br, go, bd, layer, tg=512):
    b, l, _ = bq.shape
    tg = min(tg, l)
    tok = lambda w: pl.BlockSpec((1, tg, w), lambda bi, i: (bi, i, 0))
    return pl.pallas_call(
        functools.partial(_gla_body, tg),
        grid=(b, l // tg),
        in_specs=[tok(LANE), tok(LANE), tok(GROUP_WIDTH), tok(LANE), tok(GROUP_WIDTH),
                  _const_spec(go.shape, layer), _const_spec(bd.shape)],
        out_specs=tok(GROUP_WIDTH),
        out_shape=jax.ShapeDtypeStruct((b, l, GROUP_WIDTH), BF16),
        scratch_shapes=[pltpu.VMEM((B_HEADS, B_VAL_DIM, B_KEY_DIM), F32),
                        pltpu.VMEM((tg, GROUP_WIDTH), F32)],
        compiler_params=_cparams(("parallel", "arbitrary")),
        name="gla",
    )(bq, bk, bv, bla, br, go, bd)


CONV_HIST = 32


def _conv_body(tc, h_ref, w_ref, b_ref, g_ref, o_ref, buf):
    @pl.when(pl.program_id(1) == 0)
    def _():
        buf[0:CONV_HIST, :] = jnp.zeros((CONV_HIST, C_CHANNELS), F32)

    @pl.when(pl.program_id(1) > 0)
    def _():
        buf[0:CONV_HIST, :] = buf[tc:tc + CONV_HIST, :]

    buf[CONV_HIST:CONV_HIST + tc, :] = h_ref[0]
    acc = jnp.zeros((tc, C_CHANNELS), F32) + b_ref[...]
    base = CONV_HIST - (C_KERNEL - 1)
    hb = buf[...]
    rows = tc + CONV_HIST
    for r in range(SUBLANES):
        shifted = hb if r == 0 else pltpu.roll(hb, rows - r, 0)
        for j in range(C_KERNEL):
            if (base + j) % SUBLANES == r:
                a0 = base + j - r
                acc = acc + shifted[a0:a0 + tc, :] * w_ref[j:j + 1, :]
    ms = jnp.mean(acc * acc, axis=-1, keepdims=True)
    y = acc * lax.rsqrt(ms + EPS) * g_ref[...]
    o_ref[0] = (y * jax.nn.sigmoid(y)).astype(o_ref.dtype)


def _conv(ch, w, bias, g, layer, tc=512):
    b, l, c = ch.shape
    tc = min(tc, l)
    tok = pl.BlockSpec((1, tc, c), lambda bi, i: (bi, i, 0))
    return pl.pallas_call(
        functools.partial(_conv_body, tc),
        grid=(b, l // tc),
        in_specs=[tok] + [_const_spec(a.shape, layer) for a in (w, bias, g)],
        out_specs=tok,
        out_shape=jax.ShapeDtypeStruct((b, l, c), BF16),
        scratch_shapes=[pltpu.VMEM((tc + CONV_HIST, c), F32)],
        compiler_params=_cparams(("parallel", "arbitrary")),
        name="conv",
    )(ch, w, bias, g)


def _t5_bucket(dist):
    max_exact = REL_BUCKETS // 2
    d = jnp.maximum(dist, 0)
    df = jnp.maximum(d, 1).astype(F32)
    large = max_exact + (jnp.log(df / max_exact) / math.log(REL_MAX_DIST / max_exact)
                         * (REL_BUCKETS - max_exact)).astype(I32)
    large = jnp.minimum(large, REL_BUCKETS - 1)
    return jnp.where(d < max_exact, d, large)


def _pad_cols(w, width):
    return jnp.pad(w, ((0, 0), (0, width - w.shape[1])))


def _lane_rep(v):
    return jnp.broadcast_to(v[:, None], (v.shape[0], LANE))


def _pad_heads_rows(w, heads, dim):
    w = w.reshape(heads, dim, w.shape[1])
    return jnp.pad(w, ((0, 0), (0, HEAD_PAD - dim), (0, 0))).reshape(heads * HEAD_PAD, -1)


def _split_w_in(w_in):
    widths = (256, 256, 256, 256, 32, 8, 128, 128, 256, 16, 256, 512, 256, 128, 32)
    offs = np.cumsum((0,) + widths)
    return [w_in[:, offs[n]:offs[n + 1]] for n in range(len(widths))]


def _row(v):
    return v[None, :].astype(F32)


def _one_layer_params(w):
    (aq, ak, av, iq, ik, iw, bq, bk, bv, bg, br, cu, dcq, dckv, dkpe) = _split_w_in(w["w_in"])
    wn = jnp.concatenate([ak, _pad_cols(ik, LANE), bq, bk, bv, _pad_cols(bg, LANE), br, cu, dcq, dckv,
                          _pad_cols(dkpe, LANE)], axis=1).astype(BF16)
    wt = jnp.concatenate([_pad_heads_rows(aq.T, A_HEADS, A_HEAD_DIM), av.T, iq.T,
                          jnp.pad(iw.T, ((0, T_TOT - T_IW - IDX_HEADS), (0, 0)))], axis=0).astype(BF16)
    pad_to = lambda v, n: jnp.pad(v, (0, n - v.shape[0]))
    ukv = w["d_ukv"].reshape(D_KV_RANK, D_HEADS, D_NOPE + D_V)
    wuk = ukv[:, :, :D_NOPE].reshape(D_KV_RANK, D_HEADS * D_NOPE)
    wuv = ukv[:, :, D_NOPE:].reshape(D_KV_RANK, D_HEADS * D_V)
    gdk = w["d_k_norm"]
    return dict(
        gmix=_row(w["mix_norm"]), wn=wn, wt=wt,
        gaq=_lane_rep(pad_to(w["a_q_norm"], HEAD_PAD)),
        gak=_row(jnp.tile(w["a_k_norm"], A_HEADS)),
        wgu=jnp.pad(w["b_gate_up"], ((0, LANE - B_GATE_RANK), (0, 0))).astype(BF16),
        bgb=_row(w["b_gate_bias"]),
        gqa=_row(w["d_qa_norm"]),
        wuq=_pad_heads_rows(w["d_uq"].T, D_HEADS, D_QK).astype(BF16),
        gdq=_lane_rep(pad_to(w["d_q_norm"], HEAD_PAD)),
        gkva=_row(w["d_kva_norm"]), wuk=wuk.astype(BF16), wuvT=wuv.T.astype(BF16),
        gdk=_row(jnp.tile(gdk[:D_NOPE], D_HEADS)), gdkpe=_row(pad_to(gdk[D_NOPE:], LANE)),
        gbo=_row(jnp.tile(w["b_out_norm"], B_HEADS)),
        cw=jnp.pad(w["c_dw_w"][:, 0, :], ((0, CONV_HIST - C_KERNEL), (0, 0))).astype(F32),
        cb=_row(w["c_dw_b"]), cg=_row(w["c_norm"]),
    )


def _shared_tables(seq):
    hid = np.arange(GROUP_WIDTH) // 64
    bd = jnp.asarray(hid[:, None] == hid[None, :], dtype=BF16)
    half = D_ROPE // 2
    freqs = ROPE_THETA ** (-jnp.arange(half, dtype=F32) / half)
    ang = jnp.arange(seq).astype(F32)[:, None] * freqs[None, :]
    cos, sin = jnp.cos(ang), jnp.sin(ang)
    zeros = jnp.zeros((seq, LANE - D_ROPE), F32)
    cpe = jnp.concatenate([cos, cos, zeros], axis=1)
    spe = jnp.concatenate([-sin, sin, zeros], axis=1)
    return dict(bd=bd, cosT=cos.T, sinT=sin.T, cpe=cpe, spe=spe)


def _bias_tiles(rel_bias, blk):
    assert REL_MAX_DIST <= blk + 1
    kk = jnp.arange(blk)[:, None]
    qq = jnp.arange(blk)[None, :]
    rb = rel_bias.astype(F32).T

    def lookup(bucket):
        onehot = bucket[None, :, :, None] == jnp.arange(REL_BUCKETS)
        return jnp.sum(jnp.where(onehot, rb[:, None, None, :], 0.0), axis=-1) * LOG2E

    d0 = jnp.where(kk <= qq, lookup(_t5_bucket(qq - kk)), NEG)
    d1 = lookup(_t5_bucket(blk + qq - kk))
    far = jnp.broadcast_to(lookup(_t5_bucket(jnp.full((1, 1), 2 * blk, I32))), d1.shape)
    return jnp.stack([d0, d1, far])


def kernel(x, ffn1_norm, ffn1_gate, ffn1_up, ffn1_down, mix_norm, w_in, a_q_norm, a_k_norm, rel_bias,
           b_gate_up, b_gate_bias, b_out_norm, c_dw_w, c_dw_b, c_norm, d_qa_norm, d_uq, d_kva_norm,
           d_ukv, d_q_norm, d_k_norm, w_out, ffn2_norm, ffn2_gate, ffn2_up, ffn2_down):
    w = dict(mix_norm=mix_norm, w_in=w_in, a_q_norm=a_q_norm, a_k_norm=a_k_norm, b_gate_up=b_gate_up,
             b_gate_bias=b_gate_bias, b_out_norm=b_out_norm, c_dw_w=c_dw_w, c_dw_b=c_dw_b, c_norm=c_norm,
             d_qa_norm=d_qa_norm, d_uq=d_uq, d_kva_norm=d_kva_norm, d_ukv=d_ukv,
             d_q_norm=d_q_norm, d_k_norm=d_k_norm)
    bsz, seq, dm = x.shape
    depth = w_in.shape[0]
    blk = min(ATT_BLK, seq)
    bt = _bias_tiles(rel_bias, blk)
    p = {**jax.vmap(_one_layer_params)(w), **_shared_tables(seq)}
    stacked_row = lambda v: v[:, None, :].astype(F32)
    ffn1 = (stacked_row(ffn1_norm), ffn1_gate.astype(BF16), ffn1_up.astype(BF16), ffn1_down.astype(BF16))
    ffn2 = (stacked_row(ffn2_norm), ffn2_gate.astype(BF16), ffn2_up.astype(BF16), ffn2_down.astype(BF16))
    wo = w_out.astype(BF16)
    x2 = x.reshape(bsz * seq, dm)
    for l in range(depth):
        x2 = _ffn(x2, *ffn1, l)
        (aqT, akh, avT, iqT, aik, iwT, bq, bk, bv, bla, br, ch, dqT, dkh, dvT) = _mix_in(
            x2.reshape(bsz, seq, dm), p, l, blk=blk)
        y_a = _attention(aqT, akh, avT, dsa=(iqT, iwT, aik, bt), blk=blk)
        y_b = _gla(bq, bk, bv, bla, br, p["gbo"], p["bd"], l)
        y_c = _conv(ch, p["cw"], p["cb"], p["cg"], l)
        y_d = _attention(dqT, dkh, dvT, blk=blk)
        ys = [y.reshape(bsz * seq, GROUP_WIDTH) for y in (y_a, y_b, y_c, y_d)]
        x2 = _ffn(x2, *ffn2, l, mix=(ys, wo))
    return x2.reshape(bsz, seq, dm)
```

```python
import functools
import math

import jax
import jax.numpy as jnp
import numpy as np
from jax import lax
from jax.experimental import pallas as pl
from jax.experimental.pallas import tpu as pltpu

F32 = jnp.float32
BF16 = jnp.bfloat16
I32 = jnp.int32
I16 = jnp.int16

EPS = 1e-6
GROUP_WIDTH = 256
A_HEADS, A_HEAD_DIM = 4, 64
IDX_HEADS, IDX_DIM = 8, 32
TOPK_MAX = 256
REL_BUCKETS, REL_MAX_DIST = 32, 128
B_HEADS, B_KEY_DIM, B_VAL_DIM, B_GATE_RANK = 4, 32, 64, 16
B_GATE_TAU = 16.0
B_CHUNK = 64
C_CHANNELS, C_KERNEL = 256, 31
D_HEADS, D_Q_RANK, D_KV_RANK, D_NOPE, D_ROPE, D_V = 4, 256, 128, 64, 32, 64
D_QK = D_NOPE + D_ROPE
ROPE_THETA = 10000.0

LANE = 128
SUBLANES = 8
HEAD_PAD = 128
ONES_ROWS = 16
ATT_BLK = 256
ATT_GROUP = 4
INT_MIN = -2 ** 31
I16_MIN = -2 ** 15
NEG = -1e30
LOG2E = math.log2(math.e)
VMEM_LIMIT = 56 * 1024 * 1024

N_AK, N_IK, N_BQ, N_BK, N_BV, N_BG, N_BR, N_CU, N_DCQ, N_DCKV, N_DKPE, N_TOT = (
    0, 256, 384, 512, 640, 896, 1024, 1280, 1792, 2048, 2176, 2304)
T_AQ, T_AV, T_IQ, T_IW, T_TOT = 0, 512, 768, 1024, 1040


def _dot(a, b):
    return jnp.dot(a, b, preferred_element_type=F32)


def _dot_nt(a, b):
    return lax.dot_general(a, b, (((1,), (1,)), ((), ())), preferred_element_type=F32)


def _dot_tn(a, b):
    return lax.dot_general(a, b, (((0,), (0,)), ((), ())), preferred_element_type=F32)


def _group_sum(x2, bd):
    hi = x2.astype(BF16)
    lo = (x2 - hi.astype(F32)).astype(BF16)
    return _dot(hi, bd) + _dot(lo, bd)


def _const_spec(shape, layer=None):
    nd = len(shape)
    if layer is None:
        return pl.BlockSpec(shape, lambda *_: (0,) * nd, pipeline_mode=pl.Buffered(1))
    return pl.BlockSpec((None,) + tuple(shape[1:]), lambda *_: (layer,) + (0,) * (nd - 1),
                        pipeline_mode=pl.Buffered(1))


def _cparams(sem):
    return pltpu.CompilerParams(dimension_semantics=sem, vmem_limit_bytes=VMEM_LIMIT)


def _ffn_body(has_mix, fc, *refs):
    if has_mix:
        x_ref, ya, yb, yc, yd, wo_ref, g_ref, wg_ref, wu_ref, wd_ref, o_ref, h_scr = refs
    else:
        x_ref, g_ref, wg_ref, wu_ref, wd_ref, o_ref, h_scr = refs
    x = x_ref[...]
    if has_mix:
        y = jnp.concatenate([ya[...], yb[...], yc[...], yd[...]], axis=-1)
        x = x + _dot(y, wo_ref[...])
    ms = jnp.mean(x * x, axis=-1, keepdims=True)
    xn = (x * lax.rsqrt(ms + EPS) * g_ref[...]).astype(BF16)
    d_ff = wg_ref.shape[1]
    for c in range(d_ff // fc):
        sl = slice(c * fc, (c + 1) * fc)
        gate = _dot(xn, wg_ref[:, sl])
        up = _dot(xn, wu_ref[:, sl])
        h_scr[:, sl] = (gate * jax.nn.sigmoid(gate) * up).astype(BF16)
    o_ref[...] = x + 0.5 * _dot(h_scr[...], wd_ref[...])


def _ffn(x2, g, wg, wu, wd, layer, mix=None, tm=512, fc=256):
    m, d = x2.shape
    d_ff = wg.shape[2]
    tm = min(tm, m)
    row = lambda w: pl.BlockSpec((tm, w), lambda i: (i, 0))
    in_specs = [row(d)]
    args = [x2]
    if mix is not None:
        ys, wo = mix
        in_specs += [row(GROUP_WIDTH)] * 4 + [_const_spec(wo.shape, layer)]
        args += list(ys) + [wo]
    in_specs += [_const_spec(a.shape, layer) for a in (g, wg, wu, wd)]
    args += [g, wg, wu, wd]
    return pl.pallas_call(
        functools.partial(_ffn_body, mix is not None, fc),
        grid=(m // tm,),
        in_specs=in_specs,
        out_specs=row(d),
        out_shape=jax.ShapeDtypeStruct((m, d), F32),
        scratch_shapes=[pltpu.VMEM((tm, d_ff), BF16)],
        compiler_params=_cparams(("parallel",)),
        name="ffn_mix" if mix is not None else "ffn",
    )(*args)


def _mix_in_body(tm, blk,
                 x_ref, gmix_ref, wn_ref, wt_ref, bd_ref,
                 gaq_ref, gak_ref,
                 wgu_ref, bgb_ref,
                 gqa_ref, wuq_ref, gdq_ref, cosT_ref, sinT_ref,
                 gkva_ref, wuk_ref, wuvT_ref, gdk_ref, gdkpe_ref, cpe_ref, spe_ref,
                 aqT_ref, ak_ref, avT_ref, iqT_ref, ik_ref, iwT_ref,
                 bq_ref, bk_ref, bv_ref, bla_ref, br_ref,
                 ch_ref,
                 dqT_ref, dk_ref, dvT_ref):
    nlt = tm // LANE
    x = x_ref[0]
    ms = jnp.mean(x * x, axis=-1, keepdims=True)
    xn = (x * lax.rsqrt(ms + EPS) * gmix_ref[...]).astype(BF16)
    bd = bd_ref[...]
    lane = lax.broadcasted_iota(I32, (tm, LANE), 1)

    def lanes(g):
        return jnp.tile(g, (1, nlt))

    z = _dot(xn, wn_ref[...])
    zt = _dot_nt(wt_ref[...], xn)

    def zs(off, width):
        return z[:, off:off + width]

    cq = zs(N_DCQ, D_Q_RANK)
    cq_ms = jnp.mean(cq * cq, axis=-1, keepdims=True)
    cqn = (cq * lax.rsqrt(cq_ms + EPS) * gqa_ref[...]).astype(BF16)
    ckv = zs(N_DCKV, D_KV_RANK)
    ckv_ms = jnp.mean(ckv * ckv, axis=-1, keepdims=True)
    ckvn = (ckv * lax.rsqrt(ckv_ms + EPS) * gkva_ref[...]).astype(BF16)
    dq = _dot_nt(wuq_ref[...], cqn).reshape(D_HEADS, HEAD_PAD, tm)
    kn = _dot(ckvn, wuk_ref[...])
    dv = _dot_nt(wuvT_ref[...], ckvn).astype(BF16)
    gate = _dot(zs(N_BG, LANE).astype(BF16), wgu_ref[...]) + bgb_ref[...]
    ak = zs(N_AK, GROUP_WIDTH)
    ak_ms = _group_sum(ak * ak, bd) * (1.0 / A_HEAD_DIM)
    kn_ss = _group_sum(kn * kn, bd)

    aq = zt[T_AQ:T_AQ + A_HEADS * HEAD_PAD].reshape(A_HEADS, HEAD_PAD, tm)
    aq_ms = jnp.sum(aq * aq, axis=1, keepdims=True) * (1.0 / A_HEAD_DIM)
    aq = aq * lax.rsqrt(aq_ms + EPS) * lanes(gaq_ref[...])[None] * (A_HEAD_DIM ** -0.5 * LOG2E)
    aqT_ref[0] = aq.reshape(A_HEADS * HEAD_PAD, tm).astype(BF16)
    def with_ones_rows(vt, heads, dim):
        ones = jnp.ones((ONES_ROWS, tm), BF16)
        return jnp.concatenate([r for h in range(heads) for r in (vt[dim * h:dim * (h + 1)], ones)], axis=0)

    av = with_ones_rows(zt[T_AV:T_AV + GROUP_WIDTH].astype(BF16), A_HEADS, A_HEAD_DIM)
    for c in range(tm // blk):
        avT_ref[0, c] = av[:, c * blk:(c + 1) * blk]
    iqT_ref[0] = zt[T_IQ:T_IQ + IDX_HEADS * IDX_DIM].astype(BF16)
    iwT_ref[0] = zt[T_IW:T_IW + IDX_HEADS] * ((IDX_HEADS ** -0.5) * (IDX_DIM ** -0.5))

    ak = ak * lax.rsqrt(ak_ms + EPS) * gak_ref[...]
    for h in range(A_HEADS):
        pair = ak[:, LANE * (h // 2):LANE * (h // 2) + LANE]
        if h % 2 == 1:
            pair = pltpu.roll(pair, 64, 1)
        ak_ref[0, h] = jnp.where(lane < A_HEAD_DIM, pair, 0.0).astype(BF16)
    ik_ref[0] = zs(N_IK, LANE)[:, :IDX_DIM].astype(BF16)

    bq_ref[0] = zs(N_BQ, LANE) * (B_KEY_DIM ** -0.5)
    bk_ref[0] = zs(N_BK, LANE)
    bv_ref[0] = zs(N_BV, GROUP_WIDTH).astype(BF16)
    bla_ref[0] = (jnp.minimum(gate, 0.0) - jnp.log(1.0 + jnp.exp(-jnp.abs(gate)))) * (1.0 / B_GATE_TAU)
    br_ref[0] = zs(N_BR, GROUP_WIDTH)

    ca = zs(N_CU, C_CHANNELS)
    cg = zs(N_CU + C_CHANNELS, C_CHANNELS)
    ch_ref[0] = ca * jax.nn.sigmoid(cg)

    dq_ms = jnp.sum(dq * dq, axis=1, keepdims=True) * (1.0 / D_QK)
    dq = dq * lax.rsqrt(dq_ms + EPS) * lanes(gdq_ref[...])[None] * (D_QK ** -0.5 * LOG2E)
    half = D_ROPE // 2
    x1 = dq[:, D_NOPE:D_NOPE + half]
    x2 = dq[:, D_NOPE + half:D_QK]
    cs = cosT_ref[...][None]
    sn = sinT_ref[...][None]
    dq = jnp.concatenate([dq[:, :D_NOPE], x1 * cs - x2 * sn, x2 * cs + x1 * sn, dq[:, D_QK:]], axis=1)
    dqT_ref[0] = dq.reshape(D_HEADS * HEAD_PAD, tm).astype(BF16)

    dv = with_ones_rows(dv, D_HEADS, D_V)
    for c in range(tm // blk):
        dvT_ref[0, c] = dv[:, c * blk:(c + 1) * blk]
    kpe = zs(N_DKPE, LANE)
    ss = kn_ss + jnp.sum(kpe * kpe, axis=-1, keepdims=True)
    rinv = lax.rsqrt(ss * (1.0 / D_QK) + EPS)
    kn = kn * rinv * gdk_ref[...]
    pe = kpe * gdkpe_ref[...]
    partner = jnp.where(lane < half, pltpu.roll(pe, LANE - half, 1), pltpu.roll(pe, half, 1))
    pe = pe * cpe_ref[...] + partner * spe_ref[...]
    pe = pltpu.roll(pe, D_NOPE, 1)
    for h in range(D_HEADS):
        pair = kn[:, LANE * (h // 2):LANE * (h // 2) + LANE]
        rpair = rinv[:, LANE * (h // 2):LANE * (h // 2) + LANE]
        if h % 2 == 1:
            pair = pltpu.roll(pair, 64, 1)
        else:
            rpair = pltpu.roll(rpair, 64, 1)
        dk_ref[0, h] = jnp.where(lane < D_NOPE, pair, pe * rpair).astype(BF16)


def _mix_in(x3, p, layer, tm=512, blk=ATT_BLK):
    b, l, d = x3.shape
    tm = min(tm, l)
    grid = (b, l // tm)
    nck = l // blk
    tok = lambda w: pl.BlockSpec((1, tm, w), lambda bi, i: (bi, i, 0))
    tokT = lambda r: pl.BlockSpec((1, r, tm), lambda bi, i: (bi, 0, i))
    headk = pl.BlockSpec((1, 4, tm, HEAD_PAD), lambda bi, i: (bi, 0, i, 0))
    vrows = GROUP_WIDTH + 4 * ONES_ROWS
    chunkT = pl.BlockSpec((1, tm // blk, vrows, blk), lambda bi, i: (bi, i, 0, 0))
    postab = lambda r: pl.BlockSpec((r, tm), lambda bi, i: (0, i))
    posrow = pl.BlockSpec((tm, LANE), lambda bi, i: (i, 0))
    consts = [p["gmix"], p["wn"], p["wt"], p["bd"], p["gaq"], p["gak"], p["wgu"], p["bgb"],
              p["gqa"], p["wuq"], p["gdq"]]
    consts2 = [p["gkva"], p["wuk"], p["wuvT"], p["gdk"], p["gdkpe"]]
    lspec = lambda a: _const_spec(a.shape) if a is p["bd"] else _const_spec(a.shape, layer)
    in_specs = ([tok(d)] + [lspec(a) for a in consts]
                + [postab(D_ROPE // 2), postab(D_ROPE // 2)]
                + [lspec(a) for a in consts2] + [posrow, posrow])
    args = [x3] + consts + [p["cosT"], p["sinT"]] + consts2 + [p["cpe"], p["spe"]]
    sd = jax.ShapeDtypeStruct
    out_shape = [
        sd((b, A_HEADS * HEAD_PAD, l), BF16), sd((b, A_HEADS, l, HEAD_PAD), BF16),
        sd((b, nck, vrows, blk), BF16), sd((b, IDX_HEADS * IDX_DIM, l), BF16),
        sd((b, l, IDX_DIM), BF16), sd((b, IDX_HEADS, l), F32),
        sd((b, l, LANE), F32), sd((b, l, LANE), F32), sd((b, l, GROUP_WIDTH), BF16),
        sd((b, l, LANE), F32), sd((b, l, GROUP_WIDTH), F32),
        sd((b, l, C_CHANNELS), F32),
        sd((b, D_HEADS * HEAD_PAD, l), BF16), sd((b, D_HEADS, l, HEAD_PAD), BF16),
        sd((b, nck, vrows, blk), BF16),
    ]
    out_specs = [
        tokT(A_HEADS * HEAD_PAD), headk, chunkT, tokT(IDX_HEADS * IDX_DIM),
        tok(IDX_DIM), tokT(IDX_HEADS),
        tok(LANE), tok(LANE), tok(GROUP_WIDTH), tok(LANE), tok(GROUP_WIDTH),
        tok(C_CHANNELS),
        tokT(D_HEADS * HEAD_PAD), headk, chunkT,
    ]
    return pl.pallas_call(
        functools.partial(_mix_in_body, tm, blk),
        grid=grid, in_specs=in_specs, out_specs=out_specs, out_shape=out_shape,
        compiler_params=_cparams(("parallel", "parallel")),
        name="mix_in",
    )(*args)


def _attn_body(nh, dv, blk, topk, is_dsa, *refs):
    if is_dsa:
        (qT_ref, k_ref, vT_ref, iqT_ref, wT_ref, ik_ref, bt_ref,
         o_ref, m_scr, acc_scr, key_scr, run_scr, hi_scr, lo_scr) = refs
    else:
        qT_ref, k_ref, vT_ref, o_ref, m_scr, acc_scr = refs
    vr = dv + ONES_ROWS
    i = pl.program_id(1)
    t = blk
    row = lax.broadcasted_iota(I32, (t, t), 0)
    col = lax.broadcasted_iota(I32, (t, t), 1)
    causal_pen = jnp.where(row <= col, 0.0, NEG)

    m_scr[...] = jnp.full(m_scr.shape, NEG, F32)
    acc_scr[...] = jnp.zeros(acc_scr.shape, F32)

    if is_dsa:
        def score_chunk(j):
            r0 = pl.multiple_of(j * t, t)
            ikc = ik_ref[0, pl.ds(r0, t), :]
            s = jnp.zeros((t, t), F32)
            for h in range(IDX_HEADS):
                d = _dot(ikc, iqT_ref[0, IDX_DIM * h:IDX_DIM * (h + 1), :])
                s = s + jnp.maximum(d, 0.0) * wT_ref[0, h:h + 1, :]
            bits = lax.bitcast_convert_type(s, I32)
            key = jnp.where(bits < 0, bits ^ 0x7FFFFFFF, bits)
            key = jnp.where(row + j * t <= col + i * t, key, INT_MIN)
            key_scr[pl.ds(r0, t), :] = key
            hi_scr[pl.ds(r0, t), :] = jnp.right_shift(key, 16).astype(I16)
            lo_scr[pl.ds(r0, t), :] = ((key & 0xFFFF) + I16_MIN).astype(I16)

        def score_pair(u, carry):
            score_chunk(2 * u)
            score_chunk(2 * u + 1)
            return carry

        lax.fori_loop(0, (i + 1) // 2, score_pair, 0)

        @pl.when(i % 2 == 0)
        def _():
            score_chunk(i)

        def count_ge16(ref, cand):
            c16 = jnp.broadcast_to(cand.astype(I16), (16, t))
            one, zero = jnp.int16(1), jnp.int16(0)

            def body(j, accs):
                a0, a1 = accs
                kc = ref[pl.ds(pl.multiple_of(j * t, t), t), :]
                for r in range(0, t // 16, 2):
                    a0 = a0 + jnp.where(kc[16 * r:16 * r + 16] >= c16, one, zero)
                    a1 = a1 + jnp.where(kc[16 * r + 16:16 * r + 32] >= c16, one, zero)
                return a0, a1

            z = jnp.zeros((16, t), I16)
            a0, a1 = lax.fori_loop(0, i + 1, body, (z, z))
            return jnp.sum(a0.astype(I32) + a1.astype(I32), axis=0, keepdims=True)

        def bisect16(ref, k):
            ans = jnp.where(count_ge16(ref, jnp.zeros((1, t), I32)) >= k, 0, I16_MIN)

            def bit_body(b, ans):
                cand = ans | jnp.left_shift(jnp.int32(1), 14 - b)
                return jnp.where(count_ge16(ref, cand) >= k, cand, ans)

            return lax.fori_loop(0, 15, bit_body, ans)

        ans_hi = bisect16(hi_scr, jnp.full((1, t), topk, I32))
        above = count_ge16(hi_scr, ans_hi + 1)
        h16 = jnp.broadcast_to(ans_hi.astype(I16), (t, t))

        def mask_lo(j, carry):
            rows = pl.ds(pl.multiple_of(j * t, t), t)
            lo_scr[rows, :] = jnp.where(hi_scr[rows, :] == h16, lo_scr[rows, :], jnp.int16(I16_MIN))
            return carry

        lax.fori_loop(0, i + 1, mask_lo, 0)
        ans_lo = bisect16(lo_scr, topk - above)
        n_gt = above + count_ge16(lo_scr, ans_lo + 1)
        ans = jnp.left_shift(ans_hi, 16) | (ans_lo - I16_MIN)
        need = (topk - n_gt).astype(F32)
        run_scr[...] = jnp.zeros(run_scr.shape, F32)
        stri = jnp.where(col < row, 1.0, 0.0).astype(BF16)

    def logits(h, r0):
        return _dot(k_ref[0, h, pl.ds(r0, t), :], qT_ref[0, HEAD_PAD * h:HEAD_PAD * (h + 1), :])

    def super_chunk(js, last_is_diag):
        n = len(js)
        r0s = [pl.multiple_of(j * t, t) for j in js]
        lgs = [[logits(h, r0s[c]) for c in range(n)] for h in range(nh)]
        pens = []
        if is_dsa:
            for c in range(n):
                kc = key_scr[pl.ds(r0s[c], t), :]
                eq = kc == ans
                eqf = jnp.where(eq, 1.0, 0.0)
                run = run_scr[0:1, :]
                rank = _dot(stri, eqf.astype(BF16)) + run
                run_scr[0:1, :] = run + jnp.sum(eqf, axis=0, keepdims=True)
                pens.append(jnp.where(kc > ans, 0.0, jnp.where(eq, jnp.where(rank < need, 0.0, NEG), NEG)))
        ps, alphas = [], []
        for h in range(nh):
            xs = []
            for c, j in enumerate(js):
                diag = last_is_diag and c == n - 1
                lg = lgs[h][c]
                if is_dsa:
                    tile = 0 if diag else jnp.minimum(i - j, 2)
                    lg = lg + (pens[c] + bt_ref[tile, h])
                elif diag:
                    lg = lg + causal_pen
                xs.append(lg)
            m_old = m_scr[h, 0:1, :]
            m_new = m_old
            for x in xs:
                m_new = jnp.maximum(m_new, jnp.max(x, axis=0, keepdims=True))
            alpha = jnp.exp2(m_old - m_new)
            m_scr[h, 0:1, :] = m_new
            ps.append([jnp.exp2((x - m_new).astype(BF16)) for x in xs])
            alphas.append(alpha)
        for h in range(nh):
            vs = slice(vr * h, vr * (h + 1))
            pv = _dot(vT_ref[0, js[0], vs, :], ps[h][0])
            for c in range(1, n):
                pv = pv + _dot(vT_ref[0, js[c], vs, :], ps[h][c])
            acc_scr[vs, :] = alphas[h] * acc_scr[vs, :] + pv

    def group_body(u, carry):
        super_chunk([ATT_GROUP * u + c for c in range(ATT_GROUP)], False)
        return carry

    lax.fori_loop(0, i // ATT_GROUP, group_body, 0)

    def last_step(rem):
        super_chunk([i - rem + c for c in range(rem + 1)], True)

    for rem in range(ATT_GROUP):
        pl.when(i % ATT_GROUP == rem)(functools.partial(last_step, rem))

    outs = []
    for h in range(nh):
        outs.append(acc_scr[vr * h:vr * h + dv, :] / acc_scr[vr * h + dv:vr * h + dv + 1, :])
    o_ref[0] = jnp.transpose(jnp.concatenate(outs, axis=0)).astype(o_ref.dtype)


def _attention(qT, k, vT, dsa=None, blk=ATT_BLK):
    b, nh, l, _ = k.shape
    vr = vT.shape[2] // nh
    dv = vr - ONES_ROWS
    grid = (b, l // blk)
    qspec = lambda r: pl.BlockSpec((1, r, blk), lambda bi, i: (bi, 0, i))
    kspec = pl.BlockSpec((1, nh, l, HEAD_PAD), lambda bi, i: (bi, 0, 0, 0))
    vspec = pl.BlockSpec((1, l // blk, nh * vr, blk), lambda bi, i: (bi, 0, 0, 0))
    in_specs = [qspec(nh * HEAD_PAD), kspec, vspec]
    args = [qT, k, vT]
    scratch = [pltpu.VMEM((nh, 8, blk), F32), pltpu.VMEM((nh * vr, blk), F32)]
    topk = 0
    if dsa is not None:
        iqT, wT, ik, bt = dsa
        topk = min(TOPK_MAX, l // 4)
        in_specs += [qspec(IDX_HEADS * IDX_DIM), qspec(IDX_HEADS),
                     pl.BlockSpec((1, l, IDX_DIM), lambda bi, i: (bi, 0, 0)),
                     _const_spec(bt.shape)]
        args += [iqT, wT, ik, bt]
        scratch += [pltpu.VMEM((l, blk), I32), pltpu.VMEM((8, blk), F32),
                    pltpu.VMEM((l, blk), I16), pltpu.VMEM((l, blk), I16)]
    return pl.pallas_call(
        functools.partial(_attn_body, nh, dv, blk, topk, dsa is not None),
        grid=grid, in_specs=in_specs,
        out_specs=pl.BlockSpec((1, blk, nh * dv), lambda bi, i: (bi, i, 0)),
        out_shape=jax.ShapeDtypeStruct((b, l, nh * dv), BF16),
        scratch_shapes=scratch,
        compiler_params=_cparams(("parallel", "arbitrary")),
        name="dsa_attn" if dsa is not None else "mla_attn",
    )(*args)


def _gla_body(tg, q_ref, k_ref, v_ref, la_ref, r_ref, go_ref, bd_ref, o_ref, st_scr, o_scr):
    @pl.when(pl.program_id(1) == 0)
    def _():
        st_scr[...] = jnp.zeros(st_scr.shape, F32)

    cs = B_CHUNK
    la = la_ref[0]
    rl = lax.broadcasted_iota(I32, (tg, LANE), 0) & (cs - 1)
    b = la
    s = 1
    while s < cs:
        b = b + jnp.where(rl >= s, pltpu.roll(b, s, 0), 0.0)
        s *= 2
    q = q_ref[0]
    k = k_ref[0]
    qb = q * jnp.exp(b)
    ci = lax.broadcasted_iota(I32, (cs, cs), 0)
    cj = lax.broadcasted_iota(I32, (cs, cs), 1)
    for c in range(tg // cs):
        sl = slice(c * cs, (c + 1) * cs)
        bc = b[sl]
        mid = bc[cs // 2:cs // 2 + 1]
        last = bc[cs - 1:cs]
        qe = (q[sl] * jnp.exp(bc - mid)).astype(BF16)
        ke = (k[sl] * jnp.exp(mid - bc)).astype(BF16)
        kd = (k[sl] * jnp.exp(last - bc)).astype(BF16)
        qbc = qb[sl].astype(BF16)
        dl = jnp.exp(last)
        vc = v_ref[0, sl, :]
        for h in range(B_HEADS):
            ks = slice(B_KEY_DIM * h, B_KEY_DIM * (h + 1))
            vs = slice(B_VAL_DIM * h, B_VAL_DIM * (h + 1))
            a = jnp.where(cj <= ci, _dot_nt(qe[:, ks], ke[:, ks]), 0.0)
            st = st_scr[h]
            o_scr[sl, vs] = _dot(a.astype(BF16), vc[:, vs]) + _dot_nt(qbc[:, ks], st.astype(BF16))
            st_scr[h] = st * dl[:, ks] + _dot_tn(vc[:, vs], kd[:, ks])
    o = o_scr[...]
    ms = _group_sum(o * o, bd_ref[...]) * (1.0 / B_VAL_DIM)
    r = r_ref[0]
    o_ref[0] = (o * lax.rsqrt(ms + EPS) * go_ref[...] * (r * jax.nn.sigmoid(r))).astype(o_ref.dtype)


def _gla(bq, bk, bv, bla, br, go, bd, layer, tg=512):
    b, l, _ = bq.shape
    tg = min(tg, l)
    tok = lambda w: pl.BlockSpec((1, tg, w), lambda bi, i: (bi, i, 0))
    return pl.pallas_call(
        functools.partial(_gla_body, tg),
        grid=(b, l // tg),
        in_specs=[tok(LANE), tok(LANE), tok(GROUP_WIDTH), tok(LANE), tok(GROUP_WIDTH),
                  _const_spec(go.shape, layer), _const_spec(bd.shape)],
        out_specs=tok(GROUP_WIDTH),
        out_shape=jax.ShapeDtypeStruct((b, l, GROUP_WIDTH), BF16),
        scratch_shapes=[pltpu.VMEM((B_HEADS, B_VAL_DIM, B_KEY_DIM), F32),
                        pltpu.VMEM((tg, GROUP_WIDTH), F32)],
        compiler_params=_cparams(("parallel", "arbitrary")),
        name="gla",
    )(bq, bk, bv, bla, br, go, bd)


CONV_HIST = 32


def _conv_body(tc, h_ref, w_ref, b_ref, g_ref, o_ref, buf):
    @pl.when(pl.program_id(1) == 0)
    def _():
        buf[0:CONV_HIST, :] = jnp.zeros((CONV_HIST, C_CHANNELS), F32)

    @pl.when(pl.program_id(1) > 0)
    def _():
        buf[0:CONV_HIST, :] = buf[tc:tc + CONV_HIST, :]

    buf[CONV_HIST:CONV_HIST + tc, :] = h_ref[0]
    acc = jnp.zeros((tc, C_CHANNELS), F32) + b_ref[...]
    base = CONV_HIST - (C_KERNEL - 1)
    hb = buf[...]
    rows = tc + CONV_HIST
    for r in range(SUBLANES):
        shifted = hb if r == 0 else pltpu.roll(hb, rows - r, 0)
        for j in range(C_KERNEL):
            if (base + j) % SUBLANES == r:
                a0 = base + j - r
                acc = acc + shifted[a0:a0 + tc, :] * w_ref[j:j + 1, :]
    ms = jnp.mean(acc * acc, axis=-1, keepdims=True)
    y = acc * lax.rsqrt(ms + EPS) * g_ref[...]
    o_ref[0] = (y * jax.nn.sigmoid(y)).astype(o_ref.dtype)


def _conv(ch, w, bias, g, layer, tc=512):
    b, l, c = ch.shape
    tc = min(tc, l)
    tok = pl.BlockSpec((1, tc, c), lambda bi, i: (bi, i, 0))
    return pl.pallas_call(
        functools.partial(_conv_body, tc),
        grid=(b, l // tc),
        in_specs=[tok] + [_const_spec(a.shape, layer) for a in (w, bias, g)],
        out_specs=tok,
        out_shape=jax.ShapeDtypeStruct((b, l, c), BF16),
        scratch_shapes=[pltpu.VMEM((tc + CONV_HIST, c), F32)],
        compiler_params=_cparams(("parallel", "arbitrary")),
        name="conv",
    )(ch, w, bias, g)


def _t5_bucket(dist):
    max_exact = REL_BUCKETS // 2
    d = jnp.maximum(dist, 0)
    df = jnp.maximum(d, 1).astype(F32)
    large = max_exact + (jnp.log(df / max_exact) / math.log(REL_MAX_DIST / max_exact)
                         * (REL_BUCKETS - max_exact)).astype(I32)
    large = jnp.minimum(large, REL_BUCKETS - 1)
    return jnp.where(d < max_exact, d, large)


def _pad_cols(w, width):
    return jnp.pad(w, ((0, 0), (0, width - w.shape[1])))


def _lane_rep(v):
    return jnp.broadcast_to(v[:, None], (v.shape[0], LANE))


def _pad_heads_rows(w, heads, dim):
    w = w.reshape(heads, dim, w.shape[1])
    return jnp.pad(w, ((0, 0), (0, HEAD_PAD - dim), (0, 0))).reshape(heads * HEAD_PAD, -1)


def _split_w_in(w_in):
    widths = (256, 256, 256, 256, 32, 8, 128, 128, 256, 16, 256, 512, 256, 128, 32)
    offs = np.cumsum((0,) + widths)
    return [w_in[:, offs[n]:offs[n + 1]] for n in range(len(widths))]


def _row(v):
    return v[None, :].astype(F32)


def _one_layer_params(w):
    (aq, ak, av, iq, ik, iw, bq, bk, bv, bg, br, cu, dcq, dckv, dkpe) = _split_w_in(w["w_in"])
    wn = jnp.concatenate([ak, _pad_cols(ik, LANE), bq, bk, bv, _pad_cols(bg, LANE), br, cu, dcq, dckv,
                          _pad_cols(dkpe, LANE)], axis=1).astype(BF16)
    wt = jnp.concatenate([_pad_heads_rows(aq.T, A_HEADS, A_HEAD_DIM), av.T, iq.T,
                          jnp.pad(iw.T, ((0, T_TOT - T_IW - IDX_HEADS), (0, 0)))], axis=0).astype(BF16)
    pad_to = lambda v, n: jnp.pad(v, (0, n - v.shape[0]))
    ukv = w["d_ukv"].reshape(D_KV_RANK, D_HEADS, D_NOPE + D_V)
    wuk = ukv[:, :, :D_NOPE].reshape(D_KV_RANK, D_HEADS * D_NOPE)
    wuv = ukv[:, :, D_NOPE:].reshape(D_KV_RANK, D_HEADS * D_V)
    gdk = w["d_k_norm"]
    return dict(
        gmix=_row(w["mix_norm"]), wn=wn, wt=wt,
        gaq=_lane_rep(pad_to(w["a_q_norm"], HEAD_PAD)),
        gak=_row(jnp.tile(w["a_k_norm"], A_HEADS)),
        wgu=jnp.pad(w["b_gate_up"], ((0, LANE - B_GATE_RANK), (0, 0))).astype(BF16),
        bgb=_row(w["b_gate_bias"]),
        gqa=_row(w["d_qa_norm"]),
        wuq=_pad_heads_rows(w["d_uq"].T, D_HEADS, D_QK).astype(BF16),
        gdq=_lane_rep(pad_to(w["d_q_norm"], HEAD_PAD)),
        gkva=_row(w["d_kva_norm"]), wuk=wuk.astype(BF16), wuvT=wuv.T.astype(BF16),
        gdk=_row(jnp.tile(gdk[:D_NOPE], D_HEADS)), gdkpe=_row(pad_to(gdk[D_NOPE:], LANE)),
        gbo=_row(jnp.tile(w["b_out_norm"], B_HEADS)),
        cw=jnp.pad(w["c_dw_w"][:, 0, :], ((0, CONV_HIST - C_KERNEL), (0, 0))).astype(F32),
        cb=_row(w["c_dw_b"]), cg=_row(w["c_norm"]),
    )


def _shared_tables(seq):
    hid = np.arange(GROUP_WIDTH) // 64
    bd = jnp.asarray(hid[:, None] == hid[None, :], dtype=BF16)
    half = D_ROPE // 2
    freqs = ROPE_THETA ** (-jnp.arange(half, dtype=F32) / half)
    ang = jnp.arange(seq).astype(F32)[:, None] * freqs[None, :]
    cos, sin = jnp.cos(ang), jnp.sin(ang)
    zeros = jnp.zeros((seq, LANE - D_ROPE), F32)
    cpe = jnp.concatenate([cos, cos, zeros], axis=1)
    spe = jnp.concatenate([-sin, sin, zeros], axis=1)
    return dict(bd=bd, cosT=cos.T, sinT=sin.T, cpe=cpe, spe=spe)


def _bias_tiles(rel_bias, blk):
    assert REL_MAX_DIST <= blk + 1
    kk = jnp.arange(blk)[:, None]
    qq = jnp.arange(blk)[None, :]
    rb = rel_bias.astype(F32).T

    def lookup(bucket):
        onehot = bucket[None, :, :, None] == jnp.arange(REL_BUCKETS)
        return jnp.sum(jnp.where(onehot, rb[:, None, None, :], 0.0), axis=-1) * LOG2E

    d0 = jnp.where(kk <= qq, lookup(_t5_bucket(qq - kk)), NEG)
    d1 = lookup(_t5_bucket(blk + qq - kk))
    far = jnp.broadcast_to(lookup(_t5_bucket(jnp.full((1, 1), 2 * blk, I32))), d1.shape)
    return jnp.stack([d0, d1, far])


def kernel(x, ffn1_norm, ffn1_gate, ffn1_up, ffn1_down, mix_norm, w_in, a_q_norm, a_k_norm, rel_bias,
           b_gate_up, b_gate_bias, b_out_norm, c_dw_w, c_dw_b, c_norm, d_qa_norm, d_uq, d_kva_norm,
           d_ukv, d_q_norm, d_k_norm, w_out, ffn2_norm, ffn2_gate, ffn2_up, ffn2_down):
    w = dict(mix_norm=mix_norm, w_in=w_in, a_q_norm=a_q_norm, a_k_norm=a_k_norm, b_gate_up=b_gate_up,
             b_gate_bias=b_gate_bias, b_out_norm=b_out_norm, c_dw_w=c_dw_w, c_dw_b=c_dw_b, c_norm=c_norm,
             d_qa_norm=d_qa_norm, d_uq=d_uq, d_kva_norm=d_kva_norm, d_ukv=d_ukv,
             d_q_norm=d_q_norm, d_k_norm=d_k_norm)
    bsz, seq, dm = x.shape
    depth = w_in.shape[0]
    blk = min(ATT_BLK, seq)
    bt = _bias_tiles(rel_bias, blk)
    p = {**jax.vmap(_one_layer_params)(w), **_shared_tables(seq)}
    stacked_row = lambda v: v[:, None, :].astype(F32)
    ffn1 = (stacked_row(ffn1_norm), ffn1_gate.astype(BF16), ffn1_up.astype(BF16), ffn1_down.astype(BF16))
    ffn2 = (stacked_row(ffn2_norm), ffn2_gate.astype(BF16), ffn2_up.astype(BF16), ffn2_down.astype(BF16))
    wo = w_out.astype(BF16)
    x2 = x.reshape(bsz * seq, dm)
    for l in range(depth):
        x2 = _ffn(x2, *ffn1, l)
        (aqT, akh, avT, iqT, aik, iwT, bq, bk, bv, bla, br, ch, dqT, dkh, dvT) = _mix_in(
            x2.reshape(bsz, seq, dm), p, l, blk=blk)
        y_a = _attention(aqT, akh, avT, dsa=(iqT, iwT, aik, bt), blk=blk)
        y_b = _gla(bq, bk, bv, bla, br, p["gbo"], p["bd"], l)
        y_c = _conv(ch, p["cw"], p["cb"], p["cg"], l)
        y_d = _attention(dqT, dkh, dvT, blk=blk)
        ys = [y.reshape(bsz * seq, GROUP_WIDTH) for y in (y_a, y_b, y_c, y_d)]
        x2 = _ffn(x2, *ffn2, l, mix=(ys, wo))
    return x2.reshape(bsz, seq, dm)
```

```python
import functools
import math

import jax
import jax.numpy as jnp
import numpy as np
from jax import lax
from jax.experimental import pallas as pl
from jax.experimental.pallas import tpu as pltpu

F32 = jnp.float32
BF16 = jnp.bfloat16
I32 = jnp.int32
I16 = jnp.int16

EPS = 1e-6
GROUP_WIDTH = 256
A_HEADS, A_HEAD_DIM = 4, 64
IDX_HEADS, IDX_DIM = 8, 32
TOPK_MAX = 256
REL_BUCKETS, REL_MAX_DIST = 32, 128
B_HEADS, B_KEY_DIM, B_VAL_DIM, B_GATE_RANK = 4, 32, 64, 16
B_GATE_TAU = 16.0
B_CHUNK = 64
C_CHANNELS, C_KERNEL = 256, 31
D_HEADS, D_Q_RANK, D_KV_RANK, D_NOPE, D_ROPE, D_V = 4, 256, 128, 64, 32, 64
D_QK = D_NOPE + D_ROPE
ROPE_THETA = 10000.0

LANE = 128
SUBLANES = 8
HEAD_PAD = 128
ONES_ROWS = 16
ATT_BLK = 256
ATT_GROUP = 4
INT_MIN = -2 ** 31
I16_MIN = -2 ** 15
NEG = -1e30
LOG2E = math.log2(math.e)
VMEM_LIMIT = 56 * 1024 * 1024

N_AK, N_IK, N_BQ, N_BK, N_BV, N_BG, N_BR, N_CU, N_DCQ, N_DCKV, N_DKPE, N_TOT = (
    0, 256, 384, 512, 640, 896, 1024, 1280, 1792, 2048, 2176, 2304)
T_AQ, T_AV, T_IQ, T_IW, T_TOT = 0, 512, 768, 1024, 1040


def _dot(a, b):
    return jnp.dot(a, b, preferred_element_type=F32)


def _dot_nt(a, b):
    return lax.dot_general(a, b, (((1,), (1,)), ((), ())), preferred_element_type=F32)


def _dot_tn(a, b):
    return lax.dot_general(a, b, (((0,), (0,)), ((), ())), preferred_element_type=F32)


def _group_sum(x2, bd):
    hi = x2.astype(BF16)
    lo = (x2 - hi.astype(F32)).astype(BF16)
    return _dot(hi, bd) + _dot(lo, bd)


def _const_spec(shape, layer=None):
    nd = len(shape)
    if layer is None:
        return pl.BlockSpec(shape, lambda *_: (0,) * nd, pipeline_mode=pl.Buffered(1))
    return pl.BlockSpec((None,) + tuple(shape[1:]), lambda *_: (layer,) + (0,) * (nd - 1),
                        pipeline_mode=pl.Buffered(1))


def _cparams(sem):
    return pltpu.CompilerParams(dimension_semantics=sem, vmem_limit_bytes=VMEM_LIMIT)


def _ffn_body(has_mix, fc, *refs):
    if has_mix:
        x_ref, ya, yb, yc, yd, wo_ref, g_ref, wg_ref, wu_ref, wd_ref, o_ref, h_scr = refs
    else:
        x_ref, g_ref, wg_ref, wu_ref, wd_ref, o_ref, h_scr = refs
    x = x_ref[...]
    if has_mix:
        y = jnp.concatenate([ya[...], yb[...], yc[...], yd[...]], axis=-1)
        x = x + _dot(y, wo_ref[...])
    ms = jnp.mean(x * x, axis=-1, keepdims=True)
    xn = (x * lax.rsqrt(ms + EPS) * g_ref[...]).astype(BF16)
    d_ff = wg_ref.shape[1]
    for c in range(d_ff // fc):
        sl = slice(c * fc, (c + 1) * fc)
        gate = _dot(xn, wg_ref[:, sl].astype(BF16))
        up = _dot(xn, wu_ref[:, sl].astype(BF16))
        h_scr[:, sl] = (gate * jax.nn.sigmoid(gate) * up).astype(BF16)
    o_ref[...] = x + 0.5 * _dot(h_scr[...], wd_ref[...].astype(BF16))


def _ffn(x2, g, wg, wu, wd, layer, mix=None, tm=512, fc=256):
    m, d = x2.shape
    d_ff = wg.shape[2]
    tm = min(tm, m)
    row = lambda w: pl.BlockSpec((tm, w), lambda i: (i, 0))
    in_specs = [row(d)]
    args = [x2]
    if mix is not None:
        ys, wo = mix
        in_specs += [row(GROUP_WIDTH)] * 4 + [_const_spec(wo.shape, layer)]
        args += list(ys) + [wo]
    in_specs += [_const_spec(a.shape, layer) for a in (g, wg, wu, wd)]
    args += [g, wg, wu, wd]
    return pl.pallas_call(
        functools.partial(_ffn_body, mix is not None, fc),
        grid=(m // tm,),
        in_specs=in_specs,
        out_specs=row(d),
        out_shape=jax.ShapeDtypeStruct((m, d), F32),
        scratch_shapes=[pltpu.VMEM((tm, d_ff), BF16)],
        compiler_params=_cparams(("parallel",)),
        name="ffn_mix" if mix is not None else "ffn",
    )(*args)


def _mix_in_body(tm, blk,
                 x_ref, gmix_ref, wn_ref, wt_ref, bd_ref,
                 gaq_ref, gak_ref,
                 wgu_ref, bgb_ref,
                 gqa_ref, wuq_ref, gdq_ref, cosT_ref, sinT_ref,
                 gkva_ref, wuk_ref, wuvT_ref, gdk_ref, gdkpe_ref, cpe_ref, spe_ref,
                 aqT_ref, ak_ref, avT_ref, iqT_ref, ik_ref, iwT_ref,
                 bq_ref, bk_ref, bv_ref, bla_ref, br_ref,
                 ch_ref,
                 dqT_ref, dk_ref, dvT_ref):
    nlt = tm // LANE
    x = x_ref[0]
    ms = jnp.mean(x * x, axis=-1, keepdims=True)
    xn = (x * lax.rsqrt(ms + EPS) * gmix_ref[...]).astype(BF16)
    bd = bd_ref[...]
    lane = lax.broadcasted_iota(I32, (tm, LANE), 1)

    def lanes(g):
        return jnp.tile(g, (1, nlt))

    z = _dot(xn, wn_ref[...])
    zt = _dot_nt(wt_ref[...], xn)

    def zs(off, width):
        return z[:, off:off + width]

    cq = zs(N_DCQ, D_Q_RANK)
    cq_ms = jnp.mean(cq * cq, axis=-1, keepdims=True)
    cqn = (cq * lax.rsqrt(cq_ms + EPS) * gqa_ref[...]).astype(BF16)
    ckv = zs(N_DCKV, D_KV_RANK)
    ckv_ms = jnp.mean(ckv * ckv, axis=-1, keepdims=True)
    ckvn = (ckv * lax.rsqrt(ckv_ms + EPS) * gkva_ref[...]).astype(BF16)
    dq = _dot_nt(wuq_ref[...], cqn).reshape(D_HEADS, HEAD_PAD, tm)
    kn = _dot(ckvn, wuk_ref[...])
    dv = _dot_nt(wuvT_ref[...], ckvn).astype(BF16)
    gate = _dot(zs(N_BG, LANE).astype(BF16), wgu_ref[...]) + bgb_ref[...]
    ak = zs(N_AK, GROUP_WIDTH)
    ak_ms = _group_sum(ak * ak, bd) * (1.0 / A_HEAD_DIM)
    kn_ss = _group_sum(kn * kn, bd)

    aq = zt[T_AQ:T_AQ + A_HEADS * HEAD_PAD].reshape(A_HEADS, HEAD_PAD, tm)
    aq_ms = jnp.sum(aq * aq, axis=1, keepdims=True) * (1.0 / A_HEAD_DIM)
    aq = aq * lax.rsqrt(aq_ms + EPS) * lanes(gaq_ref[...])[None] * (A_HEAD_DIM ** -0.5 * LOG2E)
    aqT_ref[0] = aq.reshape(A_HEADS * HEAD_PAD, tm).astype(BF16)
    def with_ones_rows(vt, heads, dim):
        ones = jnp.ones((ONES_ROWS, tm), BF16)
        return jnp.concatenate([r for h in range(heads) for r in (vt[dim * h:dim * (h + 1)], ones)], axis=0)

    av = with_ones_rows(zt[T_AV:T_AV + GROUP_WIDTH].astype(BF16), A_HEADS, A_HEAD_DIM)
    for c in range(tm // blk):
        avT_ref[0, c] = av[:, c * blk:(c + 1) * blk]
    iqT_ref[0] = zt[T_IQ:T_IQ + IDX_HEADS * IDX_DIM].astype(BF16)
    iwT_ref[0] = zt[T_IW:T_IW + IDX_HEADS] * ((IDX_HEADS ** -0.5) * (IDX_DIM ** -0.5))

    ak = ak * lax.rsqrt(ak_ms + EPS) * gak_ref[...]
    for h in range(A_HEADS):
        pair = ak[:, LANE * (h // 2):LANE * (h // 2) + LANE]
        if h % 2 == 1:
            pair = pltpu.roll(pair, 64, 1)
        ak_ref[0, h] = jnp.where(lane < A_HEAD_DIM, pair, 0.0).astype(BF16)
    ik_ref[0] = zs(N_IK, LANE)[:, :IDX_DIM].astype(BF16)

    bq_ref[0] = zs(N_BQ, LANE) * (B_KEY_DIM ** -0.5)
    bk_ref[0] = zs(N_BK, LANE)
    bv_ref[0] = zs(N_BV, GROUP_WIDTH).astype(BF16)
    bla_ref[0] = (jnp.minimum(gate, 0.0) - jnp.log(1.0 + jnp.exp(-jnp.abs(gate)))) * (1.0 / B_GATE_TAU)
    br_ref[0] = zs(N_BR, GROUP_WIDTH)

    ca = zs(N_CU, C_CHANNELS)
    cg = zs(N_CU + C_CHANNELS, C_CHANNELS)
    ch_ref[0] = ca * jax.nn.sigmoid(cg)

    dq_ms = jnp.sum(dq * dq, axis=1, keepdims=True) * (1.0 / D_QK)
    dq = dq * lax.rsqrt(dq_ms + EPS) * lanes(gdq_ref[...])[None] * (D_QK ** -0.5 * LOG2E)
    half = D_ROPE // 2
    x1 = dq[:, D_NOPE:D_NOPE + half]
    x2 = dq[:, D_NOPE + half:D_QK]
    cs = cosT_ref[...][None]
    sn = sinT_ref[...][None]
    dq = jnp.concatenate([dq[:, :D_NOPE], x1 * cs - x2 * sn, x2 * cs + x1 * sn, dq[:, D_QK:]], axis=1)
    dqT_ref[0] = dq.reshape(D_HEADS * HEAD_PAD, tm).astype(BF16)

    dv = with_ones_rows(dv, D_HEADS, D_V)
    for c in range(tm // blk):
        dvT_ref[0, c] = dv[:, c * blk:(c + 1) * blk]
    kpe = zs(N_DKPE, LANE)
    ss = kn_ss + jnp.sum(kpe * kpe, axis=-1, keepdims=True)
    rinv = lax.rsqrt(ss * (1.0 / D_QK) + EPS)
    kn = kn * rinv * gdk_ref[...]
    pe = kpe * gdkpe_ref[...]
    partner = jnp.where(lane < half, pltpu.roll(pe, LANE - half, 1), pltpu.roll(pe, half, 1))
    pe = pe * cpe_ref[...] + partner * spe_ref[...]
    pe = pltpu.roll(pe, D_NOPE, 1)
    for h in range(D_HEADS):
        pair = kn[:, LANE * (h // 2):LANE * (h // 2) + LANE]
        rpair = rinv[:, LANE * (h // 2):LANE * (h // 2) + LANE]
        if h % 2 == 1:
            pair = pltpu.roll(pair, 64, 1)
        else:
            rpair = pltpu.roll(rpair, 64, 1)
        dk_ref[0, h] = jnp.where(lane < D_NOPE, pair, pe * rpair).astype(BF16)


def _mix_in(x3, p, layer, tm=512, blk=ATT_BLK):
    b, l, d = x3.shape
    tm = min(tm, l)
    grid = (b, l // tm)
    nck = l // blk
    tok = lambda w: pl.BlockSpec((1, tm, w), lambda bi, i: (bi, i, 0))
    tokT = lambda r: pl.BlockSpec((1, r, tm), lambda bi, i: (bi, 0, i))
    headk = pl.BlockSpec((1, 4, tm, HEAD_PAD), lambda bi, i: (bi, 0, i, 0))
    vrows = GROUP_WIDTH + 4 * ONES_ROWS
    chunkT = pl.BlockSpec((1, tm // blk, vrows, blk), lambda bi, i: (bi, i, 0, 0))
    postab = lambda r: pl.BlockSpec((r, tm), lambda bi, i: (0, i))
    posrow = pl.BlockSpec((tm, LANE), lambda bi, i: (i, 0))
    consts = [p["gmix"], p["wn"], p["wt"], p["bd"], p["gaq"], p["gak"], p["wgu"], p["bgb"],
              p["gqa"], p["wuq"], p["gdq"]]
    consts2 = [p["gkva"], p["wuk"], p["wuvT"], p["gdk"], p["gdkpe"]]
    lspec = lambda a: _const_spec(a.shape) if a is p["bd"] else _const_spec(a.shape, layer)
    in_specs = ([tok(d)] + [lspec(a) for a in consts]
                + [postab(D_ROPE // 2), postab(D_ROPE // 2)]
                + [lspec(a) for a in consts2] + [posrow, posrow])
    args = [x3] + consts + [p["cosT"], p["sinT"]] + consts2 + [p["cpe"], p["spe"]]
    sd = jax.ShapeDtypeStruct
    out_shape = [
        sd((b, A_HEADS * HEAD_PAD, l), BF16), sd((b, A_HEADS, l, HEAD_PAD), BF16),
        sd((b, nck, vrows, blk), BF16), sd((b, IDX_HEADS * IDX_DIM, l), BF16),
        sd((b, l, IDX_DIM), BF16), sd((b, IDX_HEADS, l), F32),
        sd((b, l, LANE), F32), sd((b, l, LANE), F32), sd((b, l, GROUP_WIDTH), BF16),
        sd((b, l, LANE), F32), sd((b, l, GROUP_WIDTH), F32),
        sd((b, l, C_CHANNELS), F32),
        sd((b, D_HEADS * HEAD_PAD, l), BF16), sd((b, D_HEADS, l, HEAD_PAD), BF16),
        sd((b, nck, vrows, blk), BF16),
    ]
    out_specs = [
        tokT(A_HEADS * HEAD_PAD), headk, chunkT, tokT(IDX_HEADS * IDX_DIM),
        tok(IDX_DIM), tokT(IDX_HEADS),
        tok(LANE), tok(LANE), tok(GROUP_WIDTH), tok(LANE), tok(GROUP_WIDTH),
        tok(C_CHANNELS),
        tokT(D_HEADS * HEAD_PAD), headk, chunkT,
    ]
    return pl.pallas_call(
        functools.partial(_mix_in_body, tm, blk),
        grid=grid, in_specs=in_specs, out_specs=out_specs, out_shape=out_shape,
        compiler_params=_cparams(("parallel", "parallel")),
        name="mix_in",
    )(*args)


def _attn_body(nh, dv, blk, topk, is_dsa, *refs):
    if is_dsa:
        (qT_ref, k_ref, vT_ref, iqT_ref, wT_ref, ik_ref, bt_ref,
         o_ref, m_scr, acc_scr, key_scr, run_scr, hi_scr, lo_scr) = refs
    else:
        qT_ref, k_ref, vT_ref, o_ref, m_scr, acc_scr = refs
    vr = dv + ONES_ROWS
    i = pl.program_id(1)
    t = blk
    row = lax.broadcasted_iota(I32, (t, t), 0)
    col = lax.broadcasted_iota(I32, (t, t), 1)
    causal_pen = jnp.where(row <= col, 0.0, NEG)

    m_scr[...] = jnp.full(m_scr.shape, NEG, F32)
    acc_scr[...] = jnp.zeros(acc_scr.shape, F32)

    if is_dsa:
        def score_chunk(j):
            r0 = pl.multiple_of(j * t, t)
            ikc = ik_ref[0, pl.ds(r0, t), :]
            s = jnp.zeros((t, t), F32)
            for h in range(IDX_HEADS):
                d = _dot(ikc, iqT_ref[0, IDX_DIM * h:IDX_DIM * (h + 1), :])
                s = s + jnp.maximum(d, 0.0) * wT_ref[0, h:h + 1, :]
            bits = lax.bitcast_convert_type(s, I32)
            key = jnp.where(bits < 0, bits ^ 0x7FFFFFFF, bits)
            key = jnp.where(row + j * t <= col + i * t, key, INT_MIN)
            key_scr[pl.ds(r0, t), :] = key
            hi_scr[pl.ds(r0, t), :] = jnp.right_shift(key, 16).astype(I16)
            lo_scr[pl.ds(r0, t), :] = ((key & 0xFFFF) + I16_MIN).astype(I16)

        def score_pair(u, carry):
            score_chunk(2 * u)
            score_chunk(2 * u + 1)
            return carry

        lax.fori_loop(0, (i + 1) // 2, score_pair, 0)

        @pl.when(i % 2 == 0)
        def _():
            score_chunk(i)

        def count_ge16(ref, cand):
            c16 = jnp.broadcast_to(cand.astype(I16), (16, t))
            one, zero = jnp.int16(1), jnp.int16(0)

            def body(j, accs):
                a0, a1 = accs
                kc = ref[pl.ds(pl.multiple_of(j * t, t), t), :]
                for r in range(0, t // 16, 2):
                    a0 = a0 + jnp.where(kc[16 * r:16 * r + 16] >= c16, one, zero)
                    a1 = a1 + jnp.where(kc[16 * r + 16:16 * r + 32] >= c16, one, zero)
                return a0, a1

            z = jnp.zeros((16, t), I16)
            a0, a1 = lax.fori_loop(0, i + 1, body, (z, z))
            return jnp.sum(a0.astype(I32) + a1.astype(I32), axis=0, keepdims=True)

        def bisect16(ref, k):
            ans = jnp.where(count_ge16(ref, jnp.zeros((1, t), I32)) >= k, 0, I16_MIN)

            def bit_body(b, ans):
                cand = ans | jnp.left_shift(jnp.int32(1), 14 - b)
                return jnp.where(count_ge16(ref, cand) >= k, cand, ans)

            return lax.fori_loop(0, 15, bit_body, ans)

        ans_hi = bisect16(hi_scr, jnp.full((1, t), topk, I32))
        above = count_ge16(hi_scr, ans_hi + 1)
        h16 = jnp.broadcast_to(ans_hi.astype(I16), (t, t))

        def mask_lo(j, carry):
            rows = pl.ds(pl.multiple_of(j * t, t), t)
            lo_scr[rows, :] = jnp.where(hi_scr[rows, :] == h16, lo_scr[rows, :], jnp.int16(I16_MIN))
            return carry

        lax.fori_loop(0, i + 1, mask_lo, 0)
        ans_lo = bisect16(lo_scr, topk - above)
        n_gt = above + count_ge16(lo_scr, ans_lo + 1)
        ans = jnp.left_shift(ans_hi, 16) | (ans_lo - I16_MIN)
        need = (topk - n_gt).astype(F32)
        run_scr[...] = jnp.zeros(run_scr.shape, F32)
        stri = jnp.where(col < row, 1.0, 0.0).astype(BF16)

    def logits(h, r0):
        return _dot(k_ref[0, h, pl.ds(r0, t), :], qT_ref[0, HEAD_PAD * h:HEAD_PAD * (h + 1), :])

    def super_chunk(js, last_is_diag):
        n = len(js)
        r0s = [pl.multiple_of(j * t, t) for j in js]
        lgs = [[logits(h, r0s[c]) for c in range(n)] for h in range(nh)]
        pens = []
        if is_dsa:
            for c in range(n):
                kc = key_scr[pl.ds(r0s[c], t), :]
                eq = kc == ans
                eqf = jnp.where(eq, 1.0, 0.0)
                run = run_scr[0:1, :]
                rank = _dot(stri, eqf.astype(BF16)) + run
                run_scr[0:1, :] = run + jnp.sum(eqf, axis=0, keepdims=True)
                pens.append(jnp.where(kc > ans, 0.0, jnp.where(eq, jnp.where(rank < need, 0.0, NEG), NEG)))
        ps, alphas = [], []
        for h in range(nh):
            xs = []
            for c, j in enumerate(js):
                diag = last_is_diag and c == n - 1
                lg = lgs[h][c]
                if is_dsa:
                    tile = 0 if diag else jnp.minimum(i - j, 2)
                    lg = lg + (pens[c] + bt_ref[tile, h])
                elif diag:
                    lg = lg + causal_pen
                xs.append(lg)
            m_old = m_scr[h, 0:1, :]
            m_new = m_old
            for x in xs:
                m_new = jnp.maximum(m_new, jnp.max(x, axis=0, keepdims=True))
            alpha = jnp.exp2(m_old - m_new)
            m_scr[h, 0:1, :] = m_new
            ps.append([jnp.exp2((x - m_new).astype(BF16)) for x in xs])
            alphas.append(alpha)
        for h in range(nh):
            vs = slice(vr * h, vr * (h + 1))
            pv = _dot(vT_ref[0, js[0], vs, :], ps[h][0])
            for c in range(1, n):
                pv = pv + _dot(vT_ref[0, js[c], vs, :], ps[h][c])
            acc_scr[vs, :] = alphas[h] * acc_scr[vs, :] + pv

    def group_body(u, carry):
        super_chunk([ATT_GROUP * u + c for c in range(ATT_GROUP)], False)
        return carry

    lax.fori_loop(0, i // ATT_GROUP, group_body, 0)

    def last_step(rem):
        super_chunk([i - rem + c for c in range(rem + 1)], True)

    for rem in range(ATT_GROUP):
        pl.when(i % ATT_GROUP == rem)(functools.partial(last_step, rem))

    outs = []
    for h in range(nh):
        outs.append(acc_scr[vr * h:vr * h + dv, :] / acc_scr[vr * h + dv:vr * h + dv + 1, :])
    o_ref[0] = jnp.transpose(jnp.concatenate(outs, axis=0)).astype(o_ref.dtype)


def _attention(qT, k, vT, dsa=None, blk=ATT_BLK):
    b, nh, l, _ = k.shape
    vr = vT.shape[2] // nh
    dv = vr - ONES_ROWS
    grid = (b, l // blk)
    qspec = lambda r: pl.BlockSpec((1, r, blk), lambda bi, i: (bi, 0, i))
    kspec = pl.BlockSpec((1, nh, l, HEAD_PAD), lambda bi, i: (bi, 0, 0, 0))
    vspec = pl.BlockSpec((1, l // blk, nh * vr, blk), lambda bi, i: (bi, 0, 0, 0))
    in_specs = [qspec(nh * HEAD_PAD), kspec, vspec]
    args = [qT, k, vT]
    scratch = [pltpu.VMEM((nh, 8, blk), F32), pltpu.VMEM((nh * vr, blk), F32)]
    topk = 0
    if dsa is not None:
        iqT, wT, ik, bt = dsa
        topk = min(TOPK_MAX, l // 4)
        in_specs += [qspec(IDX_HEADS * IDX_DIM), qspec(IDX_HEADS),
                     pl.BlockSpec((1, l, IDX_DIM), lambda bi, i: (bi, 0, 0)),
                     _const_spec(bt.shape)]
        args += [iqT, wT, ik, bt]
        scratch += [pltpu.VMEM((l, blk), I32), pltpu.VMEM((8, blk), F32),
                    pltpu.VMEM((l, blk), I16), pltpu.VMEM((l, blk), I16)]
    return pl.pallas_call(
        functools.partial(_attn_body, nh, dv, blk, topk, dsa is not None),
        grid=grid, in_specs=in_specs,
        out_specs=pl.BlockSpec((1, blk, nh * dv), lambda bi, i: (bi, i, 0)),
        out_shape=jax.ShapeDtypeStruct((b, l, nh * dv), BF16),
        scratch_shapes=scratch,
        compiler_params=_cparams(("parallel", "arbitrary")),
        name="dsa_attn" if dsa is not None else "mla_attn",
    )(*args)


def _gla_body(tg, q_ref, k_ref, v_ref, la_ref, r_ref, go_ref, bd_ref, o_ref, st_scr, o_scr):
    @pl.when(pl.program_id(1) == 0)
    def _():
        st_scr[...] = jnp.zeros(st_scr.shape, F32)

    cs = B_CHUNK
    la = la_ref[0]
    rl = lax.broadcasted_iota(I32, (tg, LANE), 0) & (cs - 1)
    b = la
    s = 1
    while s < cs:
        b = b + jnp.where(rl >= s, pltpu.roll(b, s, 0), 0.0)
        s *= 2
    q = q_ref[0]
    k = k_ref[0]
    qb = q * jnp.exp(b)
    ci = lax.broadcasted_iota(I32, (cs, cs), 0)
    cj = lax.broadcasted_iota(I32, (cs, cs), 1)
    for c in range(tg // cs):
        sl = slice(c * cs, (c + 1) * cs)
        bc = b[sl]
        mid = bc[cs // 2:cs // 2 + 1]
        last = bc[cs - 1:cs]
        qe = (q[sl] * jnp.exp(bc - mid)).astype(BF16)
        ke = (k[sl] * jnp.exp(mid - bc)).astype(BF16)
        kd = (k[sl] * jnp.exp(last - bc)).astype(BF16)
        qbc = qb[sl].astype(BF16)
        dl = jnp.exp(last)
        vc = v_ref[0, sl, :]
        for h in range(B_HEADS):
            ks = slice(B_KEY_DIM * h, B_KEY_DIM * (h + 1))
            vs = slice(B_VAL_DIM * h, B_VAL_DIM * (h + 1))
            a = jnp.where(cj <= ci, _dot_nt(qe[:, ks], ke[:, ks]), 0.0)
            st = st_scr[h]
            o_scr[sl, vs] = _dot(a.astype(BF16), vc[:, vs]) + _dot_nt(qbc[:, ks], st.astype(BF16))
            st_scr[h] = st * dl[:, ks] + _dot_tn(vc[:, vs], kd[:, ks])
    o = o_scr[...]
    ms = _group_sum(o * o, bd_ref[...]) * (1.0 / B_VAL_DIM)
    r = r_ref[0]
    o_ref[0] = (o * lax.rsqrt(ms + EPS) * go_ref[...] * (r * jax.nn.sigmoid(r))).astype(o_ref.dtype)


def _gla(bq, bk, bv, bla, br, go, bd, layer, tg=512):
    b, l, _ = bq.shape
    tg = min(tg, l)
    tok = lambda w: pl.BlockSpec((1, tg, w), lambda bi, i: (bi, i, 0))
    return pl.pallas_call(
        functools.partial(_gla_body, tg),
        grid=(b, l // tg),
        in_specs=[tok(LANE), tok(LANE), tok(GROUP_WIDTH), tok(LANE), tok(GROUP_WIDTH),
                  _const_spec(go.shape, layer), _const_spec(bd.shape)],
        out_specs=tok(GROUP_WIDTH),
        out_shape=jax.ShapeDtypeStruct((b, l, GROUP_WIDTH), BF16),
        scratch_shapes=[pltpu.VMEM((B_HEADS, B_VAL_DIM, B_KEY_DIM), F32),
                        pltpu.VMEM((tg, GROUP_WIDTH), F32)],
        compiler_params=_cparams(("parallel", "arbitrary")),
        name="gla",
    )(bq, bk, bv, bla, br, go, bd)


CONV_HIST = 32


def _conv_body(tc, h_ref, w_ref, b_ref, g_ref, o_ref, buf):
    @pl.when(pl.program_id(1) == 0)
    def _():
        buf[0:CONV_HIST, :] = jnp.zeros((CONV_HIST, C_CHANNELS), F32)

    @pl.when(pl.program_id(1) > 0)
    def _():
        buf[0:CONV_HIST, :] = buf[tc:tc + CONV_HIST, :]

    buf[CONV_HIST:CONV_HIST + tc, :] = h_ref[0]
    acc = jnp.zeros((tc, C_CHANNELS), F32) + b_ref[...]
    base = CONV_HIST - (C_KERNEL - 1)
    hb = buf[...]
    rows = tc + CONV_HIST
    for r in range(SUBLANES):
        shifted = hb if r == 0 else pltpu.roll(hb, rows - r, 0)
        for j in range(C_KERNEL):
            if (base + j) % SUBLANES == r:
                a0 = base + j - r
                acc = acc + shifted[a0:a0 + tc, :] * w_ref[j:j + 1, :]
    ms = jnp.mean(acc * acc, axis=-1, keepdims=True)
    y = acc * lax.rsqrt(ms + EPS) * g_ref[...]
    o_ref[0] = (y * jax.nn.sigmoid(y)).astype(o_ref.dtype)


def _conv(ch, w, bias, g, layer, tc=512):
    b, l, c = ch.shape
    tc = min(tc, l)
    tok = pl.BlockSpec((1, tc, c), lambda bi, i: (bi, i, 0))
    return pl.pallas_call(
        functools.partial(_conv_body, tc),
        grid=(b, l // tc),
        in_specs=[tok] + [_const_spec(a.shape, layer) for a in (w, bias, g)],
        out_specs=tok,
        out_shape=jax.ShapeDtypeStruct((b, l, c), BF16),
        scratch_shapes=[pltpu.VMEM((tc + CONV_HIST, c), F32)],
        compiler_params=_cparams(("parallel", "arbitrary")),
        name="conv",
    )(ch, w, bias, g)


def _t5_bucket(dist):
    max_exact = REL_BUCKETS // 2
    d = jnp.maximum(dist, 0)
    df = jnp.maximum(d, 1).astype(F32)
    large = max_exact + (jnp.log(df / max_exact) / math.log(REL_MAX_DIST / max_exact)
                         * (REL_BUCKETS - max_exact)).astype(I32)
    large = jnp.minimum(large, REL_BUCKETS - 1)
    return jnp.where(d < max_exact, d, large)


def _pad_cols(w, width):
    return jnp.pad(w, ((0, 0), (0, width - w.shape[1])))


def _lane_rep(v):
    return jnp.broadcast_to(v[:, None], (v.shape[0], LANE))


def _pad_heads_rows(w, heads, dim):
    w = w.reshape(heads, dim, w.shape[1])
    return jnp.pad(w, ((0, 0), (0, HEAD_PAD - dim), (0, 0))).reshape(heads * HEAD_PAD, -1)


def _split_w_in(w_in):
    widths = (256, 256, 256, 256, 32, 8, 128, 128, 256, 16, 256, 512, 256, 128, 32)
    offs = np.cumsum((0,) + widths)
    return [w_in[:, offs[n]:offs[n + 1]] for n in range(len(widths))]


def _row(v):
    return v[None, :].astype(F32)


def _one_layer_params(w):
    (aq, ak, av, iq, ik, iw, bq, bk, bv, bg, br, cu, dcq, dckv, dkpe) = _split_w_in(w["w_in"])
    wn = jnp.concatenate([ak, _pad_cols(ik, LANE), bq, bk, bv, _pad_cols(bg, LANE), br, cu, dcq, dckv,
                          _pad_cols(dkpe, LANE)], axis=1).astype(BF16)
    wt = jnp.concatenate([_pad_heads_rows(aq.T, A_HEADS, A_HEAD_DIM), av.T, iq.T,
                          jnp.pad(iw.T, ((0, T_TOT - T_IW - IDX_HEADS), (0, 0)))], axis=0).astype(BF16)
    pad_to = lambda v, n: jnp.pad(v, (0, n - v.shape[0]))
    ukv = w["d_ukv"].reshape(D_KV_RANK, D_HEADS, D_NOPE + D_V)
    wuk = ukv[:, :, :D_NOPE].reshape(D_KV_RANK, D_HEADS * D_NOPE)
    wuv = ukv[:, :, D_NOPE:].reshape(D_KV_RANK, D_HEADS * D_V)
    gdk = w["d_k_norm"]
    return dict(
        gmix=_row(w["mix_norm"]), wn=wn, wt=wt,
        gaq=_lane_rep(pad_to(w["a_q_norm"], HEAD_PAD)),
        gak=_row(jnp.tile(w["a_k_norm"], A_HEADS)),
        wgu=jnp.pad(w["b_gate_up"], ((0, LANE - B_GATE_RANK), (0, 0))).astype(BF16),
        bgb=_row(w["b_gate_bias"]),
        gqa=_row(w["d_qa_norm"]),
        wuq=_pad_heads_rows(w["d_uq"].T, D_HEADS, D_QK).astype(BF16),
        gdq=_lane_rep(pad_to(w["d_q_norm"], HEAD_PAD)),
        gkva=_row(w["d_kva_norm"]), wuk=wuk.astype(BF16), wuvT=wuv.T.astype(BF16),
        gdk=_row(jnp.tile(gdk[:D_NOPE], D_HEADS)), gdkpe=_row(pad_to(gdk[D_NOPE:], LANE)),
        gbo=_row(jnp.tile(w["b_out_norm"], B_HEADS)),
        cw=jnp.pad(w["c_dw_w"][:, 0, :], ((0, CONV_HIST - C_KERNEL), (0, 0))).astype(F32),
        cb=_row(w["c_dw_b"]), cg=_row(w["c_norm"]),
    )


def _shared_tables(seq):
    hid = np.arange(GROUP_WIDTH) // 64
    bd = jnp.asarray(hid[:, None] == hid[None, :], dtype=BF16)
    half = D_ROPE // 2
    freqs = ROPE_THETA ** (-jnp.arange(half, dtype=F32) / half)
    ang = jnp.arange(seq).astype(F32)[:, None] * freqs[None, :]
    cos, sin = jnp.cos(ang), jnp.sin(ang)
    zeros = jnp.zeros((seq, LANE - D_ROPE), F32)
    cpe = jnp.concatenate([cos, cos, zeros], axis=1)
    spe = jnp.concatenate([-sin, sin, zeros], axis=1)
    return dict(bd=bd, cosT=cos.T, sinT=sin.T, cpe=cpe, spe=spe)


def _bias_tiles(rel_bias, blk):
    assert REL_MAX_DIST <= blk + 1
    kk = jnp.arange(blk)[:, None]
    qq = jnp.arange(blk)[None, :]
    rb = rel_bias.astype(F32).T

    def lookup(bucket):
        onehot = bucket[None, :, :, None] == jnp.arange(REL_BUCKETS)
        return jnp.sum(jnp.where(onehot, rb[:, None, None, :], 0.0), axis=-1) * LOG2E

    d0 = jnp.where(kk <= qq, lookup(_t5_bucket(qq - kk)), NEG)
    d1 = lookup(_t5_bucket(blk + qq - kk))
    far = jnp.broadcast_to(lookup(_t5_bucket(jnp.full((1, 1), 2 * blk, I32))), d1.shape)
    return jnp.stack([d0, d1, far])


def kernel(x, ffn1_norm, ffn1_gate, ffn1_up, ffn1_down, mix_norm, w_in, a_q_norm, a_k_norm, rel_bias,
           b_gate_up, b_gate_bias, b_out_norm, c_dw_w, c_dw_b, c_norm, d_qa_norm, d_uq, d_kva_norm,
           d_ukv, d_q_norm, d_k_norm, w_out, ffn2_norm, ffn2_gate, ffn2_up, ffn2_down):
    w = dict(mix_norm=mix_norm, w_in=w_in, a_q_norm=a_q_norm, a_k_norm=a_k_norm, b_gate_up=b_gate_up,
             b_gate_bias=b_gate_bias, b_out_norm=b_out_norm, c_dw_w=c_dw_w, c_dw_b=c_dw_b, c_norm=c_norm,
             d_qa_norm=d_qa_norm, d_uq=d_uq, d_kva_norm=d_kva_norm, d_ukv=d_ukv,
             d_q_norm=d_q_norm, d_k_norm=d_k_norm)
    bsz, seq, dm = x.shape
    depth = w_in.shape[0]
    blk = min(ATT_BLK, seq)
    bt = _bias_tiles(rel_bias, blk)
    p = {**jax.vmap(_one_layer_params)(w), **_shared_tables(seq)}
    stacked_row = lambda v: v[:, None, :].astype(F32)
    ffn1 = (stacked_row(ffn1_norm), ffn1_gate, ffn1_up, ffn1_down)
    ffn2 = (stacked_row(ffn2_norm), ffn2_gate, ffn2_up, ffn2_down)
    wo = w_out.astype(BF16)
    x2 = x.reshape(bsz * seq, dm)
    for l in range(depth):
        x2 = _ffn(x2, *ffn1, l)
        (aqT, akh, avT, iqT, aik, iwT, bq, bk, bv, bla, br, ch, dqT, dkh, dvT) = _mix_in(
            x2.reshape(bsz, seq, dm), p, l, blk=blk)
        y_a = _attention(aqT, akh, avT, dsa=(iqT, iwT, aik, bt), blk=blk)
        y_b = _gla(bq, bk, bv, bla, br, p["gbo"], p["bd"], l)
        y_c = _conv(ch, p["cw"], p["cb"], p["cg"], l)
        y_d = _attention(dqT, dkh, dvT, blk=blk)
        ys = [y.reshape(bsz * seq, GROUP_WIDTH) for y in (y_a, y_b, y_c, y_d)]
        x2 = _ffn(x2, *ffn2, l, mix=(ys, wo))
    return x2.reshape(bsz, seq, dm)
```

```python
import functools
import math

import jax
import jax.numpy as jnp
import numpy as np
from jax import lax
from jax.experimental import pallas as pl
from jax.experimental.pallas import tpu as pltpu

F32 = jnp.float32
BF16 = jnp.bfloat16
I32 = jnp.int32
I16 = jnp.int16

EPS = 1e-6
GROUP_WIDTH = 256
A_HEADS, A_HEAD_DIM = 4, 64
IDX_HEADS, IDX_DIM = 8, 32
TOPK_MAX = 256
REL_BUCKETS, REL_MAX_DIST = 32, 128
B_HEADS, B_KEY_DIM, B_VAL_DIM, B_GATE_RANK = 4, 32, 64, 16
B_GATE_TAU = 16.0
B_CHUNK = 64
C_CHANNELS, C_KERNEL = 256, 31
D_HEADS, D_Q_RANK, D_KV_RANK, D_NOPE, D_ROPE, D_V = 4, 256, 128, 64, 32, 64
D_QK = D_NOPE + D_ROPE
ROPE_THETA = 10000.0

LANE = 128
SUBLANES = 8
HEAD_PAD = 128
ONES_ROWS = 16
ATT_BLK = 256
ATT_GROUP = 4
BOUND_SLACK = 1.01
FIXED_REF_MAX = 40.0
INT_MIN = -2 ** 31
I16_MIN = -2 ** 15
NEG = -1e30
LOG2E = math.log2(math.e)
VMEM_LIMIT = 56 * 1024 * 1024

N_AK, N_IK, N_BQ, N_BK, N_BV, N_BG, N_BR, N_CU, N_DCQ, N_DCKV, N_DKPE, N_TOT = (
    0, 256, 384, 512, 640, 896, 1024, 1280, 1792, 2048, 2176, 2304)
T_AQ, T_AV, T_IQ, T_IW, T_TOT = 0, 512, 768, 1024, 1040


def _dot(a, b):
    return jnp.dot(a, b, preferred_element_type=F32)


def _dot_nt(a, b):
    return lax.dot_general(a, b, (((1,), (1,)), ((), ())), preferred_element_type=F32)


def _dot_tn(a, b):
    return lax.dot_general(a, b, (((0,), (0,)), ((), ())), preferred_element_type=F32)


def _group_sum(x2, bd):
    hi = x2.astype(BF16)
    lo = (x2 - hi.astype(F32)).astype(BF16)
    return _dot(hi, bd) + _dot(lo, bd)


def _const_spec(shape, layer=None):
    nd = len(shape)
    if layer is None:
        return pl.BlockSpec(shape, lambda *_: (0,) * nd, pipeline_mode=pl.Buffered(1))
    return pl.BlockSpec((None,) + tuple(shape[1:]), lambda *_: (layer,) + (0,) * (nd - 1),
                        pipeline_mode=pl.Buffered(1))


def _cparams(sem):
    return pltpu.CompilerParams(dimension_semantics=sem, vmem_limit_bytes=VMEM_LIMIT)


def _ffn_body(has_mix, fc, *refs):
    if has_mix:
        x_ref, ya, yb, yc, yd, wo_ref, g_ref, wg_ref, wu_ref, wd_ref, o_ref, h_scr = refs
    else:
        x_ref, g_ref, wg_ref, wu_ref, wd_ref, o_ref, h_scr = refs
    x = x_ref[...]
    if has_mix:
        y = jnp.concatenate([ya[...], yb[...], yc[...], yd[...]], axis=-1)
        x = x + _dot(y, wo_ref[...])
    ms = jnp.mean(x * x, axis=-1, keepdims=True)
    xn = (x * lax.rsqrt(ms + EPS) * g_ref[...]).astype(BF16)
    d_ff = wg_ref.shape[1]
    for c in range(d_ff // fc):
        sl = slice(c * fc, (c + 1) * fc)
        gate = _dot(xn, wg_ref[:, sl].astype(BF16))
        up = _dot(xn, wu_ref[:, sl].astype(BF16))
        h_scr[:, sl] = (gate * jax.nn.sigmoid(gate) * up).astype(BF16)
    o_ref[...] = x + 0.5 * _dot(h_scr[...], wd_ref[...].astype(BF16))


def _ffn(x2, g, wg, wu, wd, layer, mix=None, tm=512, fc=256):
    m, d = x2.shape
    d_ff = wg.shape[2]
    tm = min(tm, m)
    row = lambda w: pl.BlockSpec((tm, w), lambda i: (i, 0))
    in_specs = [row(d)]
    args = [x2]
    if mix is not None:
        ys, wo = mix
        in_specs += [row(GROUP_WIDTH)] * 4 + [_const_spec(wo.shape, layer)]
        args += list(ys) + [wo]
    in_specs += [_const_spec(a.shape, layer) for a in (g, wg, wu, wd)]
    args += [g, wg, wu, wd]
    return pl.pallas_call(
        functools.partial(_ffn_body, mix is not None, fc),
        grid=(m // tm,),
        in_specs=in_specs,
        out_specs=row(d),
        out_shape=jax.ShapeDtypeStruct((m, d), F32),
        scratch_shapes=[pltpu.VMEM((tm, d_ff), BF16)],
        compiler_params=_cparams(("parallel",)),
        name="ffn_mix" if mix is not None else "ffn",
    )(*args)


def _mix_in_body(tm, blk,
                 x_ref, gmix_ref, wn_ref, wt_ref, bd_ref,
                 gaq_ref, gak_ref,
                 wgu_ref, bgb_ref,
                 gqa_ref, wuq_ref, gdq_ref, cosT_ref, sinT_ref,
                 gkva_ref, wuk_ref, wuvT_ref, gdk_ref, gdkpe_ref, cpe_ref, spe_ref,
                 aqT_ref, ak_ref, avT_ref, iqT_ref, ik_ref, iwT_ref,
                 bq_ref, bk_ref, bv_ref, bla_ref, br_ref,
                 ch_ref,
                 dqT_ref, dk_ref, dvT_ref, knorm_ref):
    nlt = tm // LANE
    x = x_ref[0]
    ms = jnp.mean(x * x, axis=-1, keepdims=True)
    xn = (x * lax.rsqrt(ms + EPS) * gmix_ref[...]).astype(BF16)
    bd = bd_ref[...]
    lane = lax.broadcasted_iota(I32, (tm, LANE), 1)

    def lanes(g):
        return jnp.tile(g, (1, nlt))

    z = _dot(xn, wn_ref[...])
    zt = _dot_nt(wt_ref[...], xn)

    def zs(off, width):
        return z[:, off:off + width]

    cq = zs(N_DCQ, D_Q_RANK)
    cq_ms = jnp.mean(cq * cq, axis=-1, keepdims=True)
    cqn = (cq * lax.rsqrt(cq_ms + EPS) * gqa_ref[...]).astype(BF16)
    ckv = zs(N_DCKV, D_KV_RANK)
    ckv_ms = jnp.mean(ckv * ckv, axis=-1, keepdims=True)
    ckvn = (ckv * lax.rsqrt(ckv_ms + EPS) * gkva_ref[...]).astype(BF16)
    dq = _dot_nt(wuq_ref[...], cqn).reshape(D_HEADS, HEAD_PAD, tm)
    kn = _dot(ckvn, wuk_ref[...])
    dv = _dot_nt(wuvT_ref[...], ckvn).astype(BF16)
    gate = _dot(zs(N_BG, LANE).astype(BF16), wgu_ref[...]) + bgb_ref[...]
    ak = zs(N_AK, GROUP_WIDTH)
    ak_ms = _group_sum(ak * ak, bd) * (1.0 / A_HEAD_DIM)
    kn_ss = _group_sum(kn * kn, bd)

    aq = zt[T_AQ:T_AQ + A_HEADS * HEAD_PAD].reshape(A_HEADS, HEAD_PAD, tm)
    aq_ms = jnp.sum(aq * aq, axis=1, keepdims=True) * (1.0 / A_HEAD_DIM)
    aq = aq * lax.rsqrt(aq_ms + EPS) * lanes(gaq_ref[...])[None] * (A_HEAD_DIM ** -0.5 * LOG2E)
    aqT_ref[0] = aq.reshape(A_HEADS * HEAD_PAD, tm).astype(BF16)
    def max_sq_norm(kh):
        f = kh.astype(F32)
        n2 = jnp.max(jnp.sum(f * f, axis=-1, keepdims=True), axis=0, keepdims=True)
        return jnp.broadcast_to(n2, (1, LANE))

    def with_ones_rows(vt, heads, dim):
        ones = jnp.ones((ONES_ROWS, tm), BF16)
        return jnp.concatenate([r for h in range(heads) for r in (vt[dim * h:dim * (h + 1)], ones)], axis=0)

    av = with_ones_rows(zt[T_AV:T_AV + GROUP_WIDTH].astype(BF16), A_HEADS, A_HEAD_DIM)
    for c in range(tm // blk):
        avT_ref[0, c] = av[:, c * blk:(c + 1) * blk]
    iqT_ref[0] = zt[T_IQ:T_IQ + IDX_HEADS * IDX_DIM].astype(BF16)
    iwT_ref[0] = zt[T_IW:T_IW + IDX_HEADS] * ((IDX_HEADS ** -0.5) * (IDX_DIM ** -0.5))

    ak = ak * lax.rsqrt(ak_ms + EPS) * gak_ref[...]
    for h in range(A_HEADS):
        pair = ak[:, LANE * (h // 2):LANE * (h // 2) + LANE]
        if h % 2 == 1:
            pair = pltpu.roll(pair, 64, 1)
        kh = jnp.where(lane < A_HEAD_DIM, pair, 0.0).astype(BF16)
        ak_ref[0, h] = kh
        knorm_ref[0, 0, h:h + 1, :] = max_sq_norm(kh)
    ik_ref[0] = zs(N_IK, LANE)[:, :IDX_DIM].astype(BF16)

    bq_ref[0] = zs(N_BQ, LANE) * (B_KEY_DIM ** -0.5)
    bk_ref[0] = zs(N_BK, LANE)
    bv_ref[0] = zs(N_BV, GROUP_WIDTH).astype(BF16)
    bla_ref[0] = (jnp.minimum(gate, 0.0) - jnp.log(1.0 + jnp.exp(-jnp.abs(gate)))) * (1.0 / B_GATE_TAU)
    br_ref[0] = zs(N_BR, GROUP_WIDTH)

    ca = zs(N_CU, C_CHANNELS)
    cg = zs(N_CU + C_CHANNELS, C_CHANNELS)
    ch_ref[0] = ca * jax.nn.sigmoid(cg)

    dq_ms = jnp.sum(dq * dq, axis=1, keepdims=True) * (1.0 / D_QK)
    dq = dq * lax.rsqrt(dq_ms + EPS) * lanes(gdq_ref[...])[None] * (D_QK ** -0.5 * LOG2E)
    half = D_ROPE // 2
    x1 = dq[:, D_NOPE:D_NOPE + half]
    x2 = dq[:, D_NOPE + half:D_QK]
    cs = cosT_ref[...][None]
    sn = sinT_ref[...][None]
    dq = jnp.concatenate([dq[:, :D_NOPE], x1 * cs - x2 * sn, x2 * cs + x1 * sn, dq[:, D_QK:]], axis=1)
    dqT_ref[0] = dq.reshape(D_HEADS * HEAD_PAD, tm).astype(BF16)

    dv = with_ones_rows(dv, D_HEADS, D_V)
    for c in range(tm // blk):
        dvT_ref[0, c] = dv[:, c * blk:(c + 1) * blk]
    kpe = zs(N_DKPE, LANE)
    ss = kn_ss + jnp.sum(kpe * kpe, axis=-1, keepdims=True)
    rinv = lax.rsqrt(ss * (1.0 / D_QK) + EPS)
    kn = kn * rinv * gdk_ref[...]
    pe = kpe * gdkpe_ref[...]
    partner = jnp.where(lane < half, pltpu.roll(pe, LANE - half, 1), pltpu.roll(pe, half, 1))
    pe = pe * cpe_ref[...] + partner * spe_ref[...]
    pe = pltpu.roll(pe, D_NOPE, 1)
    for h in range(D_HEADS):
        pair = kn[:, LANE * (h // 2):LANE * (h // 2) + LANE]
        rpair = rinv[:, LANE * (h // 2):LANE * (h // 2) + LANE]
        if h % 2 == 1:
            pair = pltpu.roll(pair, 64, 1)
        else:
            rpair = pltpu.roll(rpair, 64, 1)
        kh = jnp.where(lane < D_NOPE, pair, pe * rpair).astype(BF16)
        dk_ref[0, h] = kh
        knorm_ref[0, 0, A_HEADS + h:A_HEADS + h + 1, :] = max_sq_norm(kh)


def _mix_in(x3, p, layer, tm=512, blk=ATT_BLK):
    b, l, d = x3.shape
    tm = min(tm, l)
    grid = (b, l // tm)
    nck = l // blk
    tok = lambda w: pl.BlockSpec((1, tm, w), lambda bi, i: (bi, i, 0))
    tokT = lambda r: pl.BlockSpec((1, r, tm), lambda bi, i: (bi, 0, i))
    headk = pl.BlockSpec((1, 4, tm, HEAD_PAD), lambda bi, i: (bi, 0, i, 0))
    vrows = GROUP_WIDTH + 4 * ONES_ROWS
    chunkT = pl.BlockSpec((1, tm // blk, vrows, blk), lambda bi, i: (bi, i, 0, 0))
    postab = lambda r: pl.BlockSpec((r, tm), lambda bi, i: (0, i))
    posrow = pl.BlockSpec((tm, LANE), lambda bi, i: (i, 0))
    consts = [p["gmix"], p["wn"], p["wt"], p["bd"], p["gaq"], p["gak"], p["wgu"], p["bgb"],
              p["gqa"], p["wuq"], p["gdq"]]
    consts2 = [p["gkva"], p["wuk"], p["wuvT"], p["gdk"], p["gdkpe"]]
    lspec = lambda a: _const_spec(a.shape) if a is p["bd"] else _const_spec(a.shape, layer)
    in_specs = ([tok(d)] + [lspec(a) for a in consts]
                + [postab(D_ROPE // 2), postab(D_ROPE // 2)]
                + [lspec(a) for a in consts2] + [posrow, posrow])
    args = [x3] + consts + [p["cosT"], p["sinT"]] + consts2 + [p["cpe"], p["spe"]]
    sd = jax.ShapeDtypeStruct
    out_shape = [
        sd((b, A_HEADS * HEAD_PAD, l), BF16), sd((b, A_HEADS, l, HEAD_PAD), BF16),
        sd((b, nck, vrows, blk), BF16), sd((b, IDX_HEADS * IDX_DIM, l), BF16),
        sd((b, l, IDX_DIM), BF16), sd((b, IDX_HEADS, l), F32),
        sd((b, l, LANE), F32), sd((b, l, LANE), F32), sd((b, l, GROUP_WIDTH), BF16),
        sd((b, l, LANE), F32), sd((b, l, GROUP_WIDTH), F32),
        sd((b, l, C_CHANNELS), F32),
        sd((b, D_HEADS * HEAD_PAD, l), BF16), sd((b, D_HEADS, l, HEAD_PAD), BF16),
        sd((b, nck, vrows, blk), BF16),
        sd((b, l // tm, A_HEADS + D_HEADS, LANE), F32),
    ]
    out_specs = [
        tokT(A_HEADS * HEAD_PAD), headk, chunkT, tokT(IDX_HEADS * IDX_DIM),
        tok(IDX_DIM), tokT(IDX_HEADS),
        tok(LANE), tok(LANE), tok(GROUP_WIDTH), tok(LANE), tok(GROUP_WIDTH),
        tok(C_CHANNELS),
        tokT(D_HEADS * HEAD_PAD), headk, chunkT,
        pl.BlockSpec((1, 1, A_HEADS + D_HEADS, LANE), lambda bi, i: (bi, i, 0, 0)),
    ]
    return pl.pallas_call(
        functools.partial(_mix_in_body, tm, blk),
        grid=grid, in_specs=in_specs, out_specs=out_specs, out_shape=out_shape,
        compiler_params=_cparams(("parallel", "parallel")),
        name="mix_in",
    )(*args)


def _attn_body(nh, dv, blk, topk, is_dsa, *refs):
    if is_dsa:
        (kb_ref, qT_ref, k_ref, vT_ref, iqT_ref, wT_ref, ik_ref, bt_ref,
         o_ref, m_scr, acc_scr, key_scr, run_scr, hi_scr, lo_scr) = refs
    else:
        kb_ref, qT_ref, k_ref, vT_ref, o_ref, m_scr, acc_scr = refs
    vr = dv + ONES_ROWS
    i = pl.program_id(1)
    t = blk
    row = lax.broadcasted_iota(I32, (t, t), 0)
    col = lax.broadcasted_iota(I32, (t, t), 1)
    causal_pen = jnp.where(row <= col, 0.0, NEG)

    m_scr[...] = jnp.full(m_scr.shape, NEG, F32)
    acc_scr[...] = jnp.zeros(acc_scr.shape, F32)

    if is_dsa:
        def score_chunk(j):
            r0 = pl.multiple_of(j * t, t)
            ikc = ik_ref[0, pl.ds(r0, t), :]
            s = jnp.zeros((t, t), F32)
            for h in range(IDX_HEADS):
                d = _dot(ikc, iqT_ref[0, IDX_DIM * h:IDX_DIM * (h + 1), :])
                s = s + jnp.maximum(d, 0.0) * wT_ref[0, h:h + 1, :]
            bits = lax.bitcast_convert_type(s, I32)
            key = jnp.where(bits < 0, bits ^ 0x7FFFFFFF, bits)
            key = jnp.where(row + j * t <= col + i * t, key, INT_MIN)
            key_scr[pl.ds(r0, t), :] = key
            hi_scr[pl.ds(r0, t), :] = jnp.right_shift(key, 16).astype(I16)
            lo_scr[pl.ds(r0, t), :] = ((key & 0xFFFF) + I16_MIN).astype(I16)

        def score_pair(u, carry):
            score_chunk(2 * u)
            score_chunk(2 * u + 1)
            return carry

        lax.fori_loop(0, (i + 1) // 2, score_pair, 0)

        @pl.when(i % 2 == 0)
        def _():
            score_chunk(i)

        def count_ge16(ref, cand):
            c16 = jnp.broadcast_to(cand.astype(I16), (16, t))
            one, zero = jnp.int16(1), jnp.int16(0)

            def body(j, accs):
                a0, a1 = accs
                kc = ref[pl.ds(pl.multiple_of(j * t, t), t), :]
                for r in range(0, t // 16, 2):
                    a0 = a0 + jnp.where(kc[16 * r:16 * r + 16] >= c16, one, zero)
                    a1 = a1 + jnp.where(kc[16 * r + 16:16 * r + 32] >= c16, one, zero)
                return a0, a1

            z = jnp.zeros((16, t), I16)
            a0, a1 = lax.fori_loop(0, i + 1, body, (z, z))
            return jnp.sum(a0.astype(I32) + a1.astype(I32), axis=0, keepdims=True)

        def bisect16(ref, k):
            ans = jnp.where(count_ge16(ref, jnp.zeros((1, t), I32)) >= k, 0, I16_MIN)

            def bit_body(b, ans):
                cand = ans | jnp.left_shift(jnp.int32(1), 14 - b)
                return jnp.where(count_ge16(ref, cand) >= k, cand, ans)

            return lax.fori_loop(0, 15, bit_body, ans)

        ans_hi = bisect16(hi_scr, jnp.full((1, t), topk, I32))
        above = count_ge16(hi_scr, ans_hi + 1)
        h16 = jnp.broadcast_to(ans_hi.astype(I16), (t, t))

        def mask_lo(j, carry):
            rows = pl.ds(pl.multiple_of(j * t, t), t)
            lo_scr[rows, :] = jnp.where(hi_scr[rows, :] == h16, lo_scr[rows, :], jnp.int16(I16_MIN))
            return carry

        lax.fori_loop(0, i + 1, mask_lo, 0)
        ans_lo = bisect16(lo_scr, topk - above)
        n_gt = above + count_ge16(lo_scr, ans_lo + 1)
        ans = jnp.left_shift(ans_hi, 16) | (ans_lo - I16_MIN)
        need = (topk - n_gt).astype(F32)
        run_scr[...] = jnp.zeros(run_scr.shape, F32)
        stri = jnp.where(col < row, 1.0, 0.0).astype(BF16)

    def logits(h, r0):
        return _dot(k_ref[0, h, pl.ds(r0, t), :], qT_ref[0, HEAD_PAD * h:HEAD_PAD * (h + 1), :])

    def selection_pens(r0s):
        pens = []
        if is_dsa:
            for r0 in r0s:
                kc = key_scr[pl.ds(r0, t), :]
                eq = kc == ans
                eqf = jnp.where(eq, 1.0, 0.0)
                run = run_scr[0:1, :]
                rank = _dot(stri, eqf.astype(BF16)) + run
                run_scr[0:1, :] = run + jnp.sum(eqf, axis=0, keepdims=True)
                pens.append(jnp.where(kc > ans, 0.0, jnp.where(eq, jnp.where(rank < need, 0.0, NEG), NEG)))
        return pens

    b_idx = pl.program_id(0)
    bounds = []
    for h in range(nh):
        qh = qT_ref[0, HEAD_PAD * h:HEAD_PAD * (h + 1), :].astype(F32)
        bound = jnp.sqrt(jnp.sum(qh * qh, axis=0, keepdims=True)) * (kb_ref[b_idx * nh + h] * BOUND_SLACK)
        if is_dsa:
            bound = bound + kb_ref[kb_ref.shape[0] - 1]
        bounds.append(bound)
    bound_max = jnp.max(functools.reduce(jnp.maximum, bounds))
    fixed_ref_ok = bound_max <= FIXED_REF_MAX

    def fixed_ref_chunks(js, last_is_diag):
        n = len(js)
        r0s = [pl.multiple_of(j * t, t) for j in js]
        lgs = [[logits(h, r0s[c]) for c in range(n)] for h in range(nh)]
        pens = selection_pens(r0s)
        ps = []
        for h in range(nh):
            row_ps = []
            for c, j in enumerate(js):
                diag = last_is_diag and c == n - 1
                x = lgs[h][c]
                if is_dsa:
                    tile = 0 if diag else jnp.minimum(i - j, 2)
                    x = x + (pens[c] + bt_ref[tile, h])
                elif diag:
                    x = x + causal_pen
                row_ps.append(jnp.exp2(x - bounds[h]).astype(BF16))
            ps.append(row_ps[0] if n == 1 else jnp.concatenate(row_ps, axis=0))
        for h in range(nh):
            vs = slice(vr * h, vr * (h + 1))
            vt = [vT_ref[0, js[c], vs, :] for c in range(n)]
            acc_scr[vs, :] += _dot(vt[0] if n == 1 else jnp.concatenate(vt, axis=1), ps[h])

    def super_chunk(js, last_is_diag):
        n = len(js)
        r0s = [pl.multiple_of(j * t, t) for j in js]
        lgs = [[logits(h, r0s[c]) for c in range(n)] for h in range(nh)]
        pens = selection_pens(r0s)
        ps, alphas = [], []
        for h in range(nh):
            xs = []
            for c, j in enumerate(js):
                diag = last_is_diag and c == n - 1
                lg = lgs[h][c]
                if is_dsa:
                    tile = 0 if diag else jnp.minimum(i - j, 2)
                    lg = lg + (pens[c] + bt_ref[tile, h])
                elif diag:
                    lg = lg + causal_pen
                xs.append(lg)
            m_old = m_scr[h, 0:1, :]
            m_new = m_old
            for x in xs:
                m_new = jnp.maximum(m_new, jnp.max(x, axis=0, keepdims=True))
            alpha = jnp.exp2(m_old - m_new)
            m_scr[h, 0:1, :] = m_new
            ps.append([jnp.exp2((x - m_new).astype(BF16)) for x in xs])
            alphas.append(alpha)
        for h in range(nh):
            vs = slice(vr * h, vr * (h + 1))
            pv = _dot(vT_ref[0, js[0], vs, :], ps[h][0])
            for c in range(1, n):
                pv = pv + _dot(vT_ref[0, js[c], vs, :], ps[h][c])
            acc_scr[vs, :] = alphas[h] * acc_scr[vs, :] + pv

    def run_chunks(step):
        def group_body(u, carry):
            step([ATT_GROUP * u + c for c in range(ATT_GROUP)], False)
            return carry

        lax.fori_loop(0, i // ATT_GROUP, group_body, 0)
        for rem in range(ATT_GROUP):
            @pl.when(i % ATT_GROUP == rem)
            def _():
                step([i - rem + c for c in range(rem + 1)], True)

    pl.when(fixed_ref_ok)(functools.partial(run_chunks, fixed_ref_chunks))
    pl.when(jnp.logical_not(fixed_ref_ok))(functools.partial(run_chunks, super_chunk))

    outs = []
    for h in range(nh):
        outs.append(acc_scr[vr * h:vr * h + dv, :] / acc_scr[vr * h + dv:vr * h + dv + 1, :])
    o_ref[0] = jnp.transpose(jnp.concatenate(outs, axis=0)).astype(o_ref.dtype)


def _attention(qT, k, vT, kbound, dsa=None, blk=ATT_BLK):
    b, nh, l, _ = k.shape
    vr = vT.shape[2] // nh
    dv = vr - ONES_ROWS
    grid = (b, l // blk)
    qspec = lambda r: pl.BlockSpec((1, r, blk), lambda bi, i: (bi, 0, i))
    kspec = pl.BlockSpec((1, nh, l, HEAD_PAD), lambda bi, i: (bi, 0, 0, 0))
    vspec = pl.BlockSpec((1, l // blk, nh * vr, blk), lambda bi, i: (bi, 0, 0, 0))
    in_specs = [pl.BlockSpec(memory_space=pltpu.SMEM), qspec(nh * HEAD_PAD), kspec, vspec]
    args = [kbound, qT, k, vT]
    scratch = [pltpu.VMEM((nh, 8, blk), F32), pltpu.VMEM((nh * vr, blk), F32)]
    topk = 0
    if dsa is not None:
        iqT, wT, ik, bt = dsa
        topk = min(TOPK_MAX, l // 4)
        in_specs += [qspec(IDX_HEADS * IDX_DIM), qspec(IDX_HEADS),
                     pl.BlockSpec((1, l, IDX_DIM), lambda bi, i: (bi, 0, 0)),
                     _const_spec(bt.shape)]
        args += [iqT, wT, ik, bt]
        scratch += [pltpu.VMEM((l, blk), I32), pltpu.VMEM((8, blk), F32),
                    pltpu.VMEM((l, blk), I16), pltpu.VMEM((l, blk), I16)]
    return pl.pallas_call(
        functools.partial(_attn_body, nh, dv, blk, topk, dsa is not None),
        grid=grid, in_specs=in_specs,
        out_specs=pl.BlockSpec((1, blk, nh * dv), lambda bi, i: (bi, i, 0)),
        out_shape=jax.ShapeDtypeStruct((b, l, nh * dv), BF16),
        scratch_shapes=scratch,
        compiler_params=_cparams(("parallel", "arbitrary")),
        name="dsa_attn" if dsa is not None else "mla_attn",
    )(*args)


def _gla_body(tg, q_ref, k_ref, v_ref, la_ref, r_ref, go_ref, bd_ref, o_ref, st_scr, o_scr):
    @pl.when(pl.program_id(1) == 0)
    def _():
        st_scr[...] = jnp.zeros(st_scr.shape, F32)

    cs = B_CHUNK
    la = la_ref[0]
    rl = lax.broadcasted_iota(I32, (tg, LANE), 0) & (cs - 1)
    b = la
    s = 1
    while s < cs:
        b = b + jnp.where(rl >= s, pltpu.roll(b, s, 0), 0.0)
        s *= 2
    q = q_ref[0]
    k = k_ref[0]
    qb = q * jnp.exp(b)
    ci = lax.broadcasted_iota(I32, (cs, cs), 0)
    cj = lax.broadcasted_iota(I32, (cs, cs), 1)
    for c in range(tg // cs):
        sl = slice(c * cs, (c + 1) * cs)
        bc = b[sl]
        mid = bc[cs // 2:cs // 2 + 1]
        last = bc[cs - 1:cs]
        qe = (q[sl] * jnp.exp(bc - mid)).astype(BF16)
        ke = (k[sl] * jnp.exp(mid - bc)).astype(BF16)
        kd = (k[sl] * jnp.exp(last - bc)).astype(BF16)
        qbc = qb[sl].astype(BF16)
        dl = jnp.exp(last)
        vc = v_ref[0, sl, :]
        for h in range(B_HEADS):
            ks = slice(B_KEY_DIM * h, B_KEY_DIM * (h + 1))
            vs = slice(B_VAL_DIM * h, B_VAL_DIM * (h + 1))
            a = jnp.where(cj <= ci, _dot_nt(qe[:, ks], ke[:, ks]), 0.0)
            st = st_scr[h]
            o_scr[sl, vs] = _dot(a.astype(BF16), vc[:, vs]) + _dot_nt(qbc[:, ks], st.astype(BF16))
            st_scr[h] = st * dl[:, ks] + _dot_tn(vc[:, vs], kd[:, ks])
    o = o_scr[...]
    ms = _group_sum(o * o, bd_ref[...]) * (1.0 / B_VAL_DIM)
    r = r_ref[0]
    o_ref[0] = (o * lax.rsqrt(ms + EPS) * go_ref[...] * (r * jax.nn.sigmoid(r))).astype(o_ref.dtype)


def _gla(bq, bk, bv, bla, br, go, bd, layer, tg=512):
    b, l, _ = bq.shape
    tg = min(tg, l)
    tok = lambda w: pl.BlockSpec((1, tg, w), lambda bi, i: (bi, i, 0))
    return pl.pallas_call(
        functools.partial(_gla_body, tg),
        grid=(b, l // tg),
        in_specs=[tok(LANE), tok(LANE), tok(GROUP_WIDTH), tok(LANE), tok(GROUP_WIDTH),
                  _const_spec(go.shape, layer), _const_spec(bd.shape)],
        out_specs=tok(GROUP_WIDTH),
        out_shape=jax.ShapeDtypeStruct((b, l, GROUP_WIDTH), BF16),
        scratch_shapes=[pltpu.VMEM((B_HEADS, B_VAL_DIM, B_KEY_DIM), F32),
                        pltpu.VMEM((tg, GROUP_WIDTH), F32)],
        compiler_params=_cparams(("parallel", "arbitrary")),
        name="gla",
    )(bq, bk, bv, bla, br, go, bd)


CONV_HIST = 32


def _conv_body(tc, h_ref, w_ref, b_ref, g_ref, o_ref, buf):
    @pl.when(pl.program_id(1) == 0)
    def _():
        buf[0:CONV_HIST, :] = jnp.zeros((CONV_HIST, C_CHANNELS), F32)

    @pl.when(pl.program_id(1) > 0)
    def _():
        buf[0:CONV_HIST, :] = buf[tc:tc + CONV_HIST, :]

    buf[CONV_HIST:CONV_HIST + tc, :] = h_ref[0]
    acc = jnp.zeros((tc, C_CHANNELS), F32) + b_ref[...]
    base = CONV_HIST - (C_KERNEL - 1)
    hb = buf[...]
    rows = tc + CONV_HIST
    for r in range(SUBLANES):
        shifted = hb if r == 0 else pltpu.roll(hb, rows - r, 0)
        for j in range(C_KERNEL):
            if (base + j) % SUBLANES == r:
                a0 = base + j - r
                acc = acc + shifted[a0:a0 + tc, :] * w_ref[j:j + 1, :]
    ms = jnp.mean(acc * acc, axis=-1, keepdims=True)
    y = acc * lax.rsqrt(ms + EPS) * g_ref[...]
    o_ref[0] = (y * jax.nn.sigmoid(y)).astype(o_ref.dtype)


def _conv(ch, w, bias, g, layer, tc=512):
    b, l, c = ch.shape
    tc = min(tc, l)
    tok = pl.BlockSpec((1, tc, c), lambda bi, i: (bi, i, 0))
    return pl.pallas_call(
        functools.partial(_conv_body, tc),
        grid=(b, l // tc),
        in_specs=[tok] + [_const_spec(a.shape, layer) for a in (w, bias, g)],
        out_specs=tok,
        out_shape=jax.ShapeDtypeStruct((b, l, c), BF16),
        scratch_shapes=[pltpu.VMEM((tc + CONV_HIST, c), F32)],
        compiler_params=_cparams(("parallel", "arbitrary")),
        name="conv",
    )(ch, w, bias, g)


def _t5_bucket(dist):
    max_exact = REL_BUCKETS // 2
    d = jnp.maximum(dist, 0)
    df = jnp.maximum(d, 1).astype(F32)
    large = max_exact + (jnp.log(df / max_exact) / math.log(REL_MAX_DIST / max_exact)
                         * (REL_BUCKETS - max_exact)).astype(I32)
    large = jnp.minimum(large, REL_BUCKETS - 1)
    return jnp.where(d < max_exact, d, large)


def _pad_cols(w, width):
    return jnp.pad(w, ((0, 0), (0, width - w.shape[1])))


def _lane_rep(v):
    return jnp.broadcast_to(v[:, None], (v.shape[0], LANE))


def _pad_heads_rows(w, heads, dim):
    w = w.reshape(heads, dim, w.shape[1])
    return jnp.pad(w, ((0, 0), (0, HEAD_PAD - dim), (0, 0))).reshape(heads * HEAD_PAD, -1)


def _split_w_in(w_in):
    widths = (256, 256, 256, 256, 32, 8, 128, 128, 256, 16, 256, 512, 256, 128, 32)
    offs = np.cumsum((0,) + widths)
    return [w_in[:, offs[n]:offs[n + 1]] for n in range(len(widths))]


def _row(v):
    return v[None, :].astype(F32)


def _one_layer_params(w):
    (aq, ak, av, iq, ik, iw, bq, bk, bv, bg, br, cu, dcq, dckv, dkpe) = _split_w_in(w["w_in"])
    wn = jnp.concatenate([ak, _pad_cols(ik, LANE), bq, bk, bv, _pad_cols(bg, LANE), br, cu, dcq, dckv,
                          _pad_cols(dkpe, LANE)], axis=1).astype(BF16)
    wt = jnp.concatenate([_pad_heads_rows(aq.T, A_HEADS, A_HEAD_DIM), av.T, iq.T,
                          jnp.pad(iw.T, ((0, T_TOT - T_IW - IDX_HEADS), (0, 0)))], axis=0).astype(BF16)
    pad_to = lambda v, n: jnp.pad(v, (0, n - v.shape[0]))
    ukv = w["d_ukv"].reshape(D_KV_RANK, D_HEADS, D_NOPE + D_V)
    wuk = ukv[:, :, :D_NOPE].reshape(D_KV_RANK, D_HEADS * D_NOPE)
    wuv = ukv[:, :, D_NOPE:].reshape(D_KV_RANK, D_HEADS * D_V)
    gdk = w["d_k_norm"]
    return dict(
        gmix=_row(w["mix_norm"]), wn=wn, wt=wt,
        gaq=_lane_rep(pad_to(w["a_q_norm"], HEAD_PAD)),
        gak=_row(jnp.tile(w["a_k_norm"], A_HEADS)),
        wgu=jnp.pad(w["b_gate_up"], ((0, LANE - B_GATE_RANK), (0, 0))).astype(BF16),
        bgb=_row(w["b_gate_bias"]),
        gqa=_row(w["d_qa_norm"]),
        wuq=_pad_heads_rows(w["d_uq"].T, D_HEADS, D_QK).astype(BF16),
        gdq=_lane_rep(pad_to(w["d_q_norm"], HEAD_PAD)),
        gkva=_row(w["d_kva_norm"]), wuk=wuk.astype(BF16), wuvT=wuv.T.astype(BF16),
        gdk=_row(jnp.tile(gdk[:D_NOPE], D_HEADS)), gdkpe=_row(pad_to(gdk[D_NOPE:], LANE)),
        gbo=_row(jnp.tile(w["b_out_norm"], B_HEADS)),
        cw=jnp.pad(w["c_dw_w"][:, 0, :], ((0, CONV_HIST - C_KERNEL), (0, 0))).astype(F32),
        cb=_row(w["c_dw_b"]), cg=_row(w["c_norm"]),
    )


def _shared_tables(seq):
    hid = np.arange(GROUP_WIDTH) // 64
    bd = jnp.asarray(hid[:, None] == hid[None, :], dtype=BF16)
    half = D_ROPE // 2
    freqs = ROPE_THETA ** (-jnp.arange(half, dtype=F32) / half)
    ang = jnp.arange(seq).astype(F32)[:, None] * freqs[None, :]
    cos, sin = jnp.cos(ang), jnp.sin(ang)
    zeros = jnp.zeros((seq, LANE - D_ROPE), F32)
    cpe = jnp.concatenate([cos, cos, zeros], axis=1)
    spe = jnp.concatenate([-sin, sin, zeros], axis=1)
    return dict(bd=bd, cosT=cos.T, sinT=sin.T, cpe=cpe, spe=spe)


def _bias_tiles(rel_bias, blk):
    assert REL_MAX_DIST <= blk + 1
    kk = jnp.arange(blk)[:, None]
    qq = jnp.arange(blk)[None, :]
    rb = rel_bias.astype(F32).T

    def lookup(bucket):
        onehot = bucket[None, :, :, None] == jnp.arange(REL_BUCKETS)
        return jnp.sum(jnp.where(onehot, rb[:, None, None, :], 0.0), axis=-1) * LOG2E

    d0 = jnp.where(kk <= qq, lookup(_t5_bucket(qq - kk)), NEG)
    d1 = lookup(_t5_bucket(blk + qq - kk))
    far = jnp.broadcast_to(lookup(_t5_bucket(jnp.full((1, 1), 2 * blk, I32))), d1.shape)
    return jnp.stack([d0, d1, far])


def _key_bounds(knorm, bias_max):
    km = jnp.sqrt(jnp.max(knorm[..., 0], axis=1))
    kb_a = jnp.concatenate([km[:, :A_HEADS].reshape(-1), bias_max.reshape(1)])
    return kb_a, km[:, A_HEADS:].reshape(-1)


def kernel(x, ffn1_norm, ffn1_gate, ffn1_up, ffn1_down, mix_norm, w_in, a_q_norm, a_k_norm, rel_bias,
           b_gate_up, b_gate_bias, b_out_norm, c_dw_w, c_dw_b, c_norm, d_qa_norm, d_uq, d_kva_norm,
           d_ukv, d_q_norm, d_k_norm, w_out, ffn2_norm, ffn2_gate, ffn2_up, ffn2_down):
    w = dict(mix_norm=mix_norm, w_in=w_in, a_q_norm=a_q_norm, a_k_norm=a_k_norm, b_gate_up=b_gate_up,
             b_gate_bias=b_gate_bias, b_out_norm=b_out_norm, c_dw_w=c_dw_w, c_dw_b=c_dw_b, c_norm=c_norm,
             d_qa_norm=d_qa_norm, d_uq=d_uq, d_kva_norm=d_kva_norm, d_ukv=d_ukv,
             d_q_norm=d_q_norm, d_k_norm=d_k_norm)
    bsz, seq, dm = x.shape
    depth = w_in.shape[0]
    blk = min(ATT_BLK, seq)
    bt = _bias_tiles(rel_bias, blk)
    bias_max = jnp.max(bt)
    p = {**jax.vmap(_one_layer_params)(w), **_shared_tables(seq)}
    stacked_row = lambda v: v[:, None, :].astype(F32)
    ffn1 = (stacked_row(ffn1_norm), ffn1_gate, ffn1_up, ffn1_down)
    ffn2 = (stacked_row(ffn2_norm), ffn2_gate, ffn2_up, ffn2_down)
    wo = w_out.astype(BF16)
    x2 = x.reshape(bsz * seq, dm)
    for l in range(depth):
        x2 = _ffn(x2, *ffn1, l)
        (aqT, akh, avT, iqT, aik, iwT, bq, bk, bv, bla, br, ch, dqT, dkh, dvT, knorm) = _mix_in(
            x2.reshape(bsz, seq, dm), p, l, blk=blk)
        kb_a, kb_d = _key_bounds(knorm, bias_max)
        y_a = _attention(aqT, akh, avT, kb_a, dsa=(iqT, iwT, aik, bt), blk=blk)
        y_b = _gla(bq, bk, bv, bla, br, p["gbo"], p["bd"], l)
        y_c = _conv(ch, p["cw"], p["cb"], p["cg"], l)
        y_d = _attention(dqT, dkh, dvT, kb_d, blk=blk)
        ys = [y.reshape(bsz * seq, GROUP_WIDTH) for y in (y_a, y_b, y_c, y_d)]
        x2 = _ffn(x2, *ffn2, l, mix=(ys, wo))
    return x2.reshape(bsz, seq, dm)
```

```python
import functools
import math

import jax
import jax.numpy as jnp
import numpy as np
from jax import lax
from jax.experimental import pallas as pl
from jax.experimental.pallas import tpu as pltpu

F32 = jnp.float32
BF16 = jnp.bfloat16
I32 = jnp.int32
I16 = jnp.int16

EPS = 1e-6
GROUP_WIDTH = 256
A_HEADS, A_HEAD_DIM = 4, 64
IDX_HEADS, IDX_DIM = 8, 32
TOPK_MAX = 256
REL_BUCKETS, REL_MAX_DIST = 32, 128
B_HEADS, B_KEY_DIM, B_VAL_DIM, B_GATE_RANK = 4, 32, 64, 16
B_GATE_TAU = 16.0
B_CHUNK = 64
C_CHANNELS, C_KERNEL = 256, 31
D_HEADS, D_Q_RANK, D_KV_RANK, D_NOPE, D_ROPE, D_V = 4, 256, 128, 64, 32, 64
D_QK = D_NOPE + D_ROPE
ROPE_THETA = 10000.0

LANE = 128
SUBLANES = 8
HEAD_PAD = 128
ONES_ROWS = 16
ATT_BLK = 256
ATT_GROUP = 4
BOUND_SLACK = 1.01
FIXED_REF_MAX = 40.0
INT_MIN = -2 ** 31
I16_MIN = -2 ** 15
NEG = -1e30
LOG2E = math.log2(math.e)
VMEM_LIMIT = 56 * 1024 * 1024

N_AK, N_IK, N_BQ, N_BK, N_BV, N_BG, N_BR, N_CU, N_DCQ, N_DCKV, N_DKPE, N_TOT = (
    0, 256, 384, 512, 640, 896, 1024, 1280, 1792, 2048, 2176, 2304)
T_AQ, T_AV, T_IQ, T_IW, T_TOT = 0, 512, 768, 1024, 1040


def _dot(a, b):
    return jnp.dot(a, b, preferred_element_type=F32)


def _dot_nt(a, b):
    return lax.dot_general(a, b, (((1,), (1,)), ((), ())), preferred_element_type=F32)


def _dot_tn(a, b):
    return lax.dot_general(a, b, (((0,), (0,)), ((), ())), preferred_element_type=F32)


def _group_sum(x2, bd):
    hi = x2.astype(BF16)
    lo = (x2 - hi.astype(F32)).astype(BF16)
    return _dot(hi, bd) + _dot(lo, bd)


def _const_spec(shape, layer=None):
    nd = len(shape)
    if layer is None:
        return pl.BlockSpec(shape, lambda *_: (0,) * nd, pipeline_mode=pl.Buffered(1))
    return pl.BlockSpec((None,) + tuple(shape[1:]), lambda *_: (layer,) + (0,) * (nd - 1),
                        pipeline_mode=pl.Buffered(1))


def _cparams(sem):
    return pltpu.CompilerParams(dimension_semantics=sem, vmem_limit_bytes=VMEM_LIMIT)


def _ffn_body(has_mix, fc, *refs):
    if has_mix:
        x_ref, ya, yb, yc, yd, wo_ref, g_ref, wg_ref, wu_ref, wd_ref, o_ref, h_scr = refs
    else:
        x_ref, g_ref, wg_ref, wu_ref, wd_ref, o_ref, h_scr = refs
    x = x_ref[...]
    if has_mix:
        y = jnp.concatenate([ya[...], yb[...], yc[...], yd[...]], axis=-1)
        x = x + _dot(y, wo_ref[...])
    ms = jnp.mean(x * x, axis=-1, keepdims=True)
    xn = (x * lax.rsqrt(ms + EPS) * g_ref[...]).astype(BF16)
    d_ff = wg_ref.shape[1]
    for c in range(d_ff // fc):
        sl = slice(c * fc, (c + 1) * fc)
        gate = _dot(xn, wg_ref[:, sl].astype(BF16))
        up = _dot(xn, wu_ref[:, sl].astype(BF16))
        h_scr[:, sl] = (gate * jax.nn.sigmoid(gate) * up).astype(BF16)
    o_ref[...] = x + 0.5 * _dot(h_scr[...], wd_ref[...].astype(BF16))


def _ffn(x2, g, wg, wu, wd, layer, mix=None, tm=512, fc=256):
    m, d = x2.shape
    d_ff = wg.shape[2]
    tm = min(tm, m)
    row = lambda w: pl.BlockSpec((tm, w), lambda i: (i, 0))
    in_specs = [row(d)]
    args = [x2]
    if mix is not None:
        ys, wo = mix
        in_specs += [row(GROUP_WIDTH)] * 4 + [_const_spec(wo.shape, layer)]
        args += list(ys) + [wo]
    in_specs += [_const_spec(a.shape, layer) for a in (g, wg, wu, wd)]
    args += [g, wg, wu, wd]
    return pl.pallas_call(
        functools.partial(_ffn_body, mix is not None, fc),
        grid=(m // tm,),
        in_specs=in_specs,
        out_specs=row(d),
        out_shape=jax.ShapeDtypeStruct((m, d), F32),
        scratch_shapes=[pltpu.VMEM((tm, d_ff), BF16)],
        compiler_params=_cparams(("parallel",)),
        name="ffn_mix" if mix is not None else "ffn",
    )(*args)


def _mix_in_body(tm, blk,
                 x_ref, gmix_ref, wn_ref, wt_ref, bd_ref,
                 gaq_ref, gak_ref,
                 wgu_ref, bgb_ref,
                 gqa_ref, wuq_ref, gdq_ref, cosT_ref, sinT_ref,
                 gkva_ref, wuk_ref, wuvT_ref, gdk_ref, gdkpe_ref, cpe_ref, spe_ref,
                 aqT_ref, ak_ref, avT_ref, iqT_ref, ik_ref, iwT_ref,
                 bq_ref, bk_ref, bv_ref, bla_ref, br_ref,
                 ch_ref,
                 dqT_ref, dk_ref, dvT_ref, knorm_ref):
    nlt = tm // LANE
    x = x_ref[0]
    ms = jnp.mean(x * x, axis=-1, keepdims=True)
    xn = (x * lax.rsqrt(ms + EPS) * gmix_ref[...]).astype(BF16)
    bd = bd_ref[...]
    lane = lax.broadcasted_iota(I32, (tm, LANE), 1)

    def lanes(g):
        return jnp.tile(g, (1, nlt))

    z = _dot(xn, wn_ref[...])
    zt = _dot_nt(wt_ref[...], xn)

    def zs(off, width):
        return z[:, off:off + width]

    cq = zs(N_DCQ, D_Q_RANK)
    cq_ms = jnp.mean(cq * cq, axis=-1, keepdims=True)
    cqn = (cq * lax.rsqrt(cq_ms + EPS) * gqa_ref[...]).astype(BF16)
    ckv = zs(N_DCKV, D_KV_RANK)
    ckv_ms = jnp.mean(ckv * ckv, axis=-1, keepdims=True)
    ckvn = (ckv * lax.rsqrt(ckv_ms + EPS) * gkva_ref[...]).astype(BF16)
    dq = _dot_nt(wuq_ref[...], cqn).reshape(D_HEADS, HEAD_PAD, tm)
    kn = _dot(ckvn, wuk_ref[...])
    dv = _dot_nt(wuvT_ref[...], ckvn).astype(BF16)
    gate = _dot(zs(N_BG, LANE).astype(BF16), wgu_ref[...]) + bgb_ref[...]
    ak = zs(N_AK, GROUP_WIDTH)
    ak_ms = _group_sum(ak * ak, bd) * (1.0 / A_HEAD_DIM)
    kn_ss = _group_sum(kn * kn, bd)

    aq = zt[T_AQ:T_AQ + A_HEADS * HEAD_PAD].reshape(A_HEADS, HEAD_PAD, tm)
    aq_ms = jnp.sum(aq * aq, axis=1, keepdims=True) * (1.0 / A_HEAD_DIM)
    aq = aq * lax.rsqrt(aq_ms + EPS) * lanes(gaq_ref[...])[None] * (A_HEAD_DIM ** -0.5 * LOG2E)
    aqT_ref[0] = aq.reshape(A_HEADS * HEAD_PAD, tm).astype(BF16)
    def max_sq_norm(kh):
        f = kh.astype(F32)
        n2 = _dot((f * f).astype(BF16), jnp.ones((LANE, LANE), BF16))
        return jnp.max(n2, axis=0, keepdims=True)

    def with_ones_rows(vt, heads, dim):
        ones = jnp.ones((ONES_ROWS, tm), BF16)
        return jnp.concatenate([r for h in range(heads) for r in (vt[dim * h:dim * (h + 1)], ones)], axis=0)

    av = with_ones_rows(zt[T_AV:T_AV + GROUP_WIDTH].astype(BF16), A_HEADS, A_HEAD_DIM)
    for c in range(tm // blk):
        avT_ref[0, c] = av[:, c * blk:(c + 1) * blk]
    iqT_ref[0] = zt[T_IQ:T_IQ + IDX_HEADS * IDX_DIM].astype(BF16)
    iwT_ref[0] = zt[T_IW:T_IW + IDX_HEADS] * ((IDX_HEADS ** -0.5) * (IDX_DIM ** -0.5))

    ak = ak * lax.rsqrt(ak_ms + EPS) * gak_ref[...]
    for h in range(A_HEADS):
        pair = ak[:, LANE * (h // 2):LANE * (h // 2) + LANE]
        if h % 2 == 1:
            pair = pltpu.roll(pair, 64, 1)
        kh = jnp.where(lane < A_HEAD_DIM, pair, 0.0).astype(BF16)
        ak_ref[0, h] = kh
        knorm_ref[0, 0, h:h + 1, :] = max_sq_norm(kh)
    ik_ref[0] = zs(N_IK, LANE)[:, :IDX_DIM].astype(BF16)

    bq_ref[0] = zs(N_BQ, LANE) * (B_KEY_DIM ** -0.5)
    bk_ref[0] = zs(N_BK, LANE)
    bv_ref[0] = zs(N_BV, GROUP_WIDTH).astype(BF16)
    bla_ref[0] = (jnp.minimum(gate, 0.0) - jnp.log(1.0 + jnp.exp(-jnp.abs(gate)))) * (1.0 / B_GATE_TAU)
    br_ref[0] = zs(N_BR, GROUP_WIDTH)

    ca = zs(N_CU, C_CHANNELS)
    cg = zs(N_CU + C_CHANNELS, C_CHANNELS)
    ch_ref[0] = ca * jax.nn.sigmoid(cg)

    dq_ms = jnp.sum(dq * dq, axis=1, keepdims=True) * (1.0 / D_QK)
    dq = dq * lax.rsqrt(dq_ms + EPS) * lanes(gdq_ref[...])[None] * (D_QK ** -0.5 * LOG2E)
    half = D_ROPE // 2
    x1 = dq[:, D_NOPE:D_NOPE + half]
    x2 = dq[:, D_NOPE + half:D_QK]
    cs = cosT_ref[...][None]
    sn = sinT_ref[...][None]
    dq = jnp.concatenate([dq[:, :D_NOPE], x1 * cs - x2 * sn, x2 * cs + x1 * sn, dq[:, D_QK:]], axis=1)
    dqT_ref[0] = dq.reshape(D_HEADS * HEAD_PAD, tm).astype(BF16)

    dv = with_ones_rows(dv, D_HEADS, D_V)
    for c in range(tm // blk):
        dvT_ref[0, c] = dv[:, c * blk:(c + 1) * blk]
    kpe = zs(N_DKPE, LANE)
    ss = kn_ss + jnp.sum(kpe * kpe, axis=-1, keepdims=True)
    rinv = lax.rsqrt(ss * (1.0 / D_QK) + EPS)
    kn = kn * rinv * gdk_ref[...]
    pe = kpe * gdkpe_ref[...]
    partner = jnp.where(lane < half, pltpu.roll(pe, LANE - half, 1), pltpu.roll(pe, half, 1))
    pe = pe * cpe_ref[...] + partner * spe_ref[...]
    pe = pltpu.roll(pe, D_NOPE, 1)
    for h in range(D_HEADS):
        pair = kn[:, LANE * (h // 2):LANE * (h // 2) + LANE]
        rpair = rinv[:, LANE * (h // 2):LANE * (h // 2) + LANE]
        if h % 2 == 1:
            pair = pltpu.roll(pair, 64, 1)
        else:
            rpair = pltpu.roll(rpair, 64, 1)
        kh = jnp.where(lane < D_NOPE, pair, pe * rpair).astype(BF16)
        dk_ref[0, h] = kh
        knorm_ref[0, 0, A_HEADS + h:A_HEADS + h + 1, :] = max_sq_norm(kh)


def _mix_in(x3, p, layer, tm=512, blk=ATT_BLK):
    b, l, d = x3.shape
    tm = min(tm, l)
    grid = (b, l // tm)
    nck = l // blk
    tok = lambda w: pl.BlockSpec((1, tm, w), lambda bi, i: (bi, i, 0))
    tokT = lambda r: pl.BlockSpec((1, r, tm), lambda bi, i: (bi, 0, i))
    headk = pl.BlockSpec((1, 4, tm, HEAD_PAD), lambda bi, i: (bi, 0, i, 0))
    vrows = GROUP_WIDTH + 4 * ONES_ROWS
    chunkT = pl.BlockSpec((1, tm // blk, vrows, blk), lambda bi, i: (bi, i, 0, 0))
    postab = lambda r: pl.BlockSpec((r, tm), lambda bi, i: (0, i))
    posrow = pl.BlockSpec((tm, LANE), lambda bi, i: (i, 0))
    consts = [p["gmix"], p["wn"], p["wt"], p["bd"], p["gaq"], p["gak"], p["wgu"], p["bgb"],
              p["gqa"], p["wuq"], p["gdq"]]
    consts2 = [p["gkva"], p["wuk"], p["wuvT"], p["gdk"], p["gdkpe"]]
    lspec = lambda a: _const_spec(a.shape) if a is p["bd"] else _const_spec(a.shape, layer)
    in_specs = ([tok(d)] + [lspec(a) for a in consts]
                + [postab(D_ROPE // 2), postab(D_ROPE // 2)]
                + [lspec(a) for a in consts2] + [posrow, posrow])
    args = [x3] + consts + [p["cosT"], p["sinT"]] + consts2 + [p["cpe"], p["spe"]]
    sd = jax.ShapeDtypeStruct
    out_shape = [
        sd((b, A_HEADS * HEAD_PAD, l), BF16), sd((b, A_HEADS, l, HEAD_PAD), BF16),
        sd((b, nck, vrows, blk), BF16), sd((b, IDX_HEADS * IDX_DIM, l), BF16),
        sd((b, l, IDX_DIM), BF16), sd((b, IDX_HEADS, l), F32),
        sd((b, l, LANE), F32), sd((b, l, LANE), F32), sd((b, l, GROUP_WIDTH), BF16),
        sd((b, l, LANE), F32), sd((b, l, GROUP_WIDTH), F32),
        sd((b, l, C_CHANNELS), F32),
        sd((b, D_HEADS * HEAD_PAD, l), BF16), sd((b, D_HEADS, l, HEAD_PAD), BF16),
        sd((b, nck, vrows, blk), BF16),
        sd((b, l // tm, A_HEADS + D_HEADS, LANE), F32),
    ]
    out_specs = [
        tokT(A_HEADS * HEAD_PAD), headk, chunkT, tokT(IDX_HEADS * IDX_DIM),
        tok(IDX_DIM), tokT(IDX_HEADS),
        tok(LANE), tok(LANE), tok(GROUP_WIDTH), tok(LANE), tok(GROUP_WIDTH),
        tok(C_CHANNELS),
        tokT(D_HEADS * HEAD_PAD), headk, chunkT,
        pl.BlockSpec((1, 1, A_HEADS + D_HEADS, LANE), lambda bi, i: (bi, i, 0, 0)),
    ]
    return pl.pallas_call(
        functools.partial(_mix_in_body, tm, blk),
        grid=grid, in_specs=in_specs, out_specs=out_specs, out_shape=out_shape,
        compiler_params=_cparams(("parallel", "parallel")),
        name="mix_in",
    )(*args)


def _attn_body(nh, dv, blk, topk, is_dsa, *refs):
    if is_dsa:
        (kb_ref, qT_ref, k_ref, vT_ref, iqT_ref, wT_ref, ik_ref, bt_ref,
         o_ref, m_scr, acc_scr, key_scr, run_scr, hi_scr, lo_scr) = refs
    else:
        kb_ref, qT_ref, k_ref, vT_ref, o_ref, m_scr, acc_scr = refs
    vr = dv + ONES_ROWS
    i = pl.program_id(1)
    t = blk
    row = lax.broadcasted_iota(I32, (t, t), 0)
    col = lax.broadcasted_iota(I32, (t, t), 1)
    causal_pen = jnp.where(row <= col, 0.0, NEG)

    m_scr[...] = jnp.full(m_scr.shape, NEG, F32)
    acc_scr[...] = jnp.zeros(acc_scr.shape, F32)

    if is_dsa:
        def score_chunk(j, diag):
            r0 = pl.multiple_of(j * t, t)
            ikc = ik_ref[0, pl.ds(r0, t), :]
            s = jnp.zeros((t, t), F32)
            for h in range(IDX_HEADS):
                d = _dot(ikc, iqT_ref[0, IDX_DIM * h:IDX_DIM * (h + 1), :])
                s = s + jnp.maximum(d, 0.0) * wT_ref[0, h:h + 1, :]
            bits = lax.bitcast_convert_type(s, I32)
            key = jnp.where(bits < 0, bits ^ 0x7FFFFFFF, bits)
            if diag:
                key = jnp.where(row <= col, key, INT_MIN)
            key_scr[pl.ds(r0, t), :] = key
            hi_scr[pl.ds(r0, t), :] = jnp.right_shift(key, 16).astype(I16)
            lo_scr[pl.ds(r0, t), :] = ((key & 0xFFFF) + I16_MIN).astype(I16)

        def score_pair(u, carry):
            score_chunk(2 * u, False)
            score_chunk(2 * u + 1, False)
            return carry

        lax.fori_loop(0, i // 2, score_pair, 0)

        @pl.when(i % 2 == 1)
        def _():
            score_chunk(i - 1, False)
            score_chunk(i, True)

        @pl.when(i % 2 == 0)
        def _():
            score_chunk(i, True)

        def count_ge16(ref, cand):
            c16 = jnp.broadcast_to(cand.astype(I16), (16, t))
            one, zero = jnp.int16(1), jnp.int16(0)

            def body(j, accs):
                a0, a1 = accs
                kc = ref[pl.ds(pl.multiple_of(j * t, t), t), :]
                for r in range(0, t // 16, 2):
                    a0 = a0 + jnp.where(kc[16 * r:16 * r + 16] >= c16, one, zero)
                    a1 = a1 + jnp.where(kc[16 * r + 16:16 * r + 32] >= c16, one, zero)
                return a0, a1

            z = jnp.zeros((16, t), I16)
            a0, a1 = lax.fori_loop(0, i + 1, body, (z, z))
            return jnp.sum(a0.astype(I32) + a1.astype(I32), axis=0, keepdims=True)

        def bisect16(ref, k):
            ans = jnp.where(count_ge16(ref, jnp.zeros((1, t), I32)) >= k, 0, I16_MIN)

            def bit_body(b, ans):
                cand = ans | jnp.left_shift(jnp.int32(1), 14 - b)
                return jnp.where(count_ge16(ref, cand) >= k, cand, ans)

            return lax.fori_loop(0, 15, bit_body, ans)

        ans_hi = bisect16(hi_scr, jnp.full((1, t), topk, I32))
        above = count_ge16(hi_scr, ans_hi + 1)
        h16 = jnp.broadcast_to(ans_hi.astype(I16), (t, t))

        def mask_lo(j, carry):
            rows = pl.ds(pl.multiple_of(j * t, t), t)
            lo_scr[rows, :] = jnp.where(hi_scr[rows, :] == h16, lo_scr[rows, :], jnp.int16(I16_MIN))
            return carry

        lax.fori_loop(0, i + 1, mask_lo, 0)
        ans_lo = bisect16(lo_scr, topk - above)
        n_gt = above + count_ge16(lo_scr, ans_lo + 1)
        ans = jnp.left_shift(ans_hi, 16) | (ans_lo - I16_MIN)
        need = (topk - n_gt).astype(F32)
        run_scr[...] = jnp.zeros(run_scr.shape, F32)
        stri = jnp.where(col < row, 1.0, 0.0).astype(BF16)

    def logits(h, r0):
        return _dot(k_ref[0, h, pl.ds(r0, t), :], qT_ref[0, HEAD_PAD * h:HEAD_PAD * (h + 1), :])

    def selection_pens(r0s):
        pens = []
        if is_dsa:
            for r0 in r0s:
                kc = key_scr[pl.ds(r0, t), :]
                eq = kc == ans
                eqf = jnp.where(eq, 1.0, 0.0)
                run = run_scr[0:1, :]
                rank = _dot(stri, eqf.astype(BF16)) + run
                run_scr[0:1, :] = run + jnp.sum(eqf, axis=0, keepdims=True)
                pens.append(jnp.where(kc > ans, 0.0, jnp.where(eq, jnp.where(rank < need, 0.0, NEG), NEG)))
        return pens

    b_idx = pl.program_id(0)
    bounds = []
    for h in range(nh):
        qh = qT_ref[0, HEAD_PAD * h:HEAD_PAD * (h + 1), :].astype(F32)
        bound = jnp.sqrt(jnp.sum(qh * qh, axis=0, keepdims=True)) * (kb_ref[b_idx * nh + h] * BOUND_SLACK)
        if is_dsa:
            bound = bound + kb_ref[kb_ref.shape[0] - 1]
        bounds.append(bound)
    bound_max = jnp.max(functools.reduce(jnp.maximum, bounds))
    fixed_ref_ok = bound_max <= FIXED_REF_MAX

    def fixed_ref_chunks(js, last_is_diag):
        n = len(js)
        r0s = [pl.multiple_of(j * t, t) for j in js]
        lgs = [[logits(h, r0s[c]) for c in range(n)] for h in range(nh)]
        pens = selection_pens(r0s)
        ps = []
        for h in range(nh):
            row_ps = []
            for c, j in enumerate(js):
                diag = last_is_diag and c == n - 1
                x = lgs[h][c]
                if is_dsa:
                    tile = 0 if diag else jnp.minimum(i - j, 2)
                    x = x + (pens[c] + bt_ref[tile, h])
                elif diag:
                    x = x + causal_pen
                row_ps.append(jnp.exp2(x - bounds[h]).astype(BF16))
            ps.append(row_ps[0] if n == 1 else jnp.concatenate(row_ps, axis=0))
        for h in range(nh):
            vs = slice(vr * h, vr * (h + 1))
            vt = [vT_ref[0, js[c], vs, :] for c in range(n)]
            acc_scr[vs, :] += _dot(vt[0] if n == 1 else jnp.concatenate(vt, axis=1), ps[h])

    def super_chunk(js, last_is_diag):
        n = len(js)
        r0s = [pl.multiple_of(j * t, t) for j in js]
        lgs = [[logits(h, r0s[c]) for c in range(n)] for h in range(nh)]
        pens = selection_pens(r0s)
        ps, alphas = [], []
        for h in range(nh):
            xs = []
            for c, j in enumerate(js):
                diag = last_is_diag and c == n - 1
                lg = lgs[h][c]
                if is_dsa:
                    tile = 0 if diag else jnp.minimum(i - j, 2)
                    lg = lg + (pens[c] + bt_ref[tile, h])
                elif diag:
                    lg = lg + causal_pen
                xs.append(lg)
            m_old = m_scr[h, 0:1, :]
            m_new = m_old
            for x in xs:
                m_new = jnp.maximum(m_new, jnp.max(x, axis=0, keepdims=True))
            alpha = jnp.exp2(m_old - m_new)
            m_scr[h, 0:1, :] = m_new
            ps.append([jnp.exp2((x - m_new).astype(BF16)) for x in xs])
            alphas.append(alpha)
        for h in range(nh):
            vs = slice(vr * h, vr * (h + 1))
            pv = _dot(vT_ref[0, js[0], vs, :], ps[h][0])
            for c in range(1, n):
                pv = pv + _dot(vT_ref[0, js[c], vs, :], ps[h][c])
            acc_scr[vs, :] = alphas[h] * acc_scr[vs, :] + pv

    def run_chunks(step):
        def group_body(u, carry):
            step([ATT_GROUP * u + c for c in range(ATT_GROUP)], False)
            return carry

        lax.fori_loop(0, i // ATT_GROUP, group_body, 0)
        for rem in range(ATT_GROUP):
            @pl.when(i % ATT_GROUP == rem)
            def _():
                step([i - rem + c for c in range(rem + 1)], True)

    pl.when(fixed_ref_ok)(functools.partial(run_chunks, fixed_ref_chunks))
    pl.when(jnp.logical_not(fixed_ref_ok))(functools.partial(run_chunks, super_chunk))

    outs = []
    for h in range(nh):
        outs.append(acc_scr[vr * h:vr * h + dv, :] / acc_scr[vr * h + dv:vr * h + dv + 1, :])
    o_ref[0] = jnp.transpose(jnp.concatenate(outs, axis=0)).astype(o_ref.dtype)


def _attention(qT, k, vT, kbound, dsa=None, blk=ATT_BLK):
    b, nh, l, _ = k.shape
    vr = vT.shape[2] // nh
    dv = vr - ONES_ROWS
    grid = (b, l // blk)
    qspec = lambda r: pl.BlockSpec((1, r, blk), lambda bi, i: (bi, 0, i))
    kspec = pl.BlockSpec((1, nh, l, HEAD_PAD), lambda bi, i: (bi, 0, 0, 0))
    vspec = pl.BlockSpec((1, l // blk, nh * vr, blk), lambda bi, i: (bi, 0, 0, 0))
    in_specs = [pl.BlockSpec(memory_space=pltpu.SMEM), qspec(nh * HEAD_PAD), kspec, vspec]
    args = [kbound, qT, k, vT]
    scratch = [pltpu.VMEM((nh, 8, blk), F32), pltpu.VMEM((nh * vr, blk), F32)]
    topk = 0
    if dsa is not None:
        iqT, wT, ik, bt = dsa
        topk = min(TOPK_MAX, l // 4)
        in_specs += [qspec(IDX_HEADS * IDX_DIM), qspec(IDX_HEADS),
                     pl.BlockSpec((1, l, IDX_DIM), lambda bi, i: (bi, 0, 0)),
                     _const_spec(bt.shape)]
        args += [iqT, wT, ik, bt]
        scratch += [pltpu.VMEM((l, blk), I32), pltpu.VMEM((8, blk), F32),
                    pltpu.VMEM((l, blk), I16), pltpu.VMEM((l, blk), I16)]
    return pl.pallas_call(
        functools.partial(_attn_body, nh, dv, blk, topk, dsa is not None),
        grid=grid, in_specs=in_specs,
        out_specs=pl.BlockSpec((1, blk, nh * dv), lambda bi, i: (bi, i, 0)),
        out_shape=jax.ShapeDtypeStruct((b, l, nh * dv), BF16),
        scratch_shapes=scratch,
        compiler_params=_cparams(("parallel", "arbitrary")),
        name="dsa_attn" if dsa is not None else "mla_attn",
    )(*args)


def _gla_body(tg, q_ref, k_ref, v_ref, la_ref, r_ref, go_ref, bd_ref, o_ref, st_scr, o_scr):
    @pl.when(pl.program_id(1) == 0)
    def _():
        st_scr[...] = jnp.zeros(st_scr.shape, F32)

    cs = B_CHUNK
    la = la_ref[0]
    rl = lax.broadcasted_iota(I32, (tg, LANE), 0) & (cs - 1)
    b = la
    s = 1
    while s < cs:
        b = b + jnp.where(rl >= s, pltpu.roll(b, s, 0), 0.0)
        s *= 2
    q = q_ref[0]
    k = k_ref[0]
    qb = q * jnp.exp(b)
    ci = lax.broadcasted_iota(I32, (cs, cs), 0)
    cj = lax.broadcasted_iota(I32, (cs, cs), 1)
    for c in range(tg // cs):
        sl = slice(c * cs, (c + 1) * cs)
        bc = b[sl]
        mid = bc[cs // 2:cs // 2 + 1]
        last = bc[cs - 1:cs]
        qe = (q[sl] * jnp.exp(bc - mid)).astype(BF16)
        ke = (k[sl] * jnp.exp(mid - bc)).astype(BF16)
        kd = (k[sl] * jnp.exp(last - bc)).astype(BF16)
        qbc = qb[sl].astype(BF16)
        dl = jnp.exp(last)
        vc = v_ref[0, sl, :]
        for h in range(B_HEADS):
            ks = slice(B_KEY_DIM * h, B_KEY_DIM * (h + 1))
            vs = slice(B_VAL_DIM * h, B_VAL_DIM * (h + 1))
            a = jnp.where(cj <= ci, _dot_nt(qe[:, ks], ke[:, ks]), 0.0)
            st = st_scr[h]
            o_scr[sl, vs] = _dot(a.astype(BF16), vc[:, vs]) + _dot_nt(qbc[:, ks], st.astype(BF16))
            st_scr[h] = st * dl[:, ks] + _dot_tn(vc[:, vs], kd[:, ks])
    o = o_scr[...]
    ms = _group_sum(o * o, bd_ref[...]) * (1.0 / B_VAL_DIM)
    r = r_ref[0]
    o_ref[0] = (o * lax.rsqrt(ms + EPS) * go_ref[...] * (r * jax.nn.sigmoid(r))).astype(o_ref.dtype)


def _gla(bq, bk, bv, bla, br, go, bd, layer, tg=512):
    b, l, _ = bq.shape
    tg = min(tg, l)
    tok = lambda w: pl.BlockSpec((1, tg, w), lambda bi, i: (bi, i, 0))
    return pl.pallas_call(
        functools.partial(_gla_body, tg),
        grid=(b, l // tg),
        in_specs=[tok(LANE), tok(LANE), tok(GROUP_WIDTH), tok(LANE), tok(GROUP_WIDTH),
                  _const_spec(go.shape, layer), _const_spec(bd.shape)],
        out_specs=tok(GROUP_WIDTH),
        out_shape=jax.ShapeDtypeStruct((b, l, GROUP_WIDTH), BF16),
        scratch_shapes=[pltpu.VMEM((B_HEADS, B_VAL_DIM, B_KEY_DIM), F32),
                        pltpu.VMEM((tg, GROUP_WIDTH), F32)],
        compiler_params=_cparams(("parallel", "arbitrary")),
        name="gla",
    )(bq, bk, bv, bla, br, go, bd)


CONV_HIST = 32


def _conv_body(tc, h_ref, w_ref, b_ref, g_ref, o_ref, buf):
    @pl.when(pl.program_id(1) == 0)
    def _():
        buf[0:CONV_HIST, :] = jnp.zeros((CONV_HIST, C_CHANNELS), F32)

    @pl.when(pl.program_id(1) > 0)
    def _():
        buf[0:CONV_HIST, :] = buf[tc:tc + CONV_HIST, :]

    buf[CONV_HIST:CONV_HIST + tc, :] = h_ref[0]
    acc = jnp.zeros((tc, C_CHANNELS), F32) + b_ref[...]
    base = CONV_HIST - (C_KERNEL - 1)
    hb = buf[...]
    rows = tc + CONV_HIST
    for r in range(SUBLANES):
        shifted = hb if r == 0 else pltpu.roll(hb, rows - r, 0)
        for j in range(C_KERNEL):
            if (base + j) % SUBLANES == r:
                a0 = base + j - r
                acc = acc + shifted[a0:a0 + tc, :] * w_ref[j:j + 1, :]
    ms = jnp.mean(acc * acc, axis=-1, keepdims=True)
    y = acc * lax.rsqrt(ms + EPS) * g_ref[...]
    o_ref[0] = (y * jax.nn.sigmoid(y)).astype(o_ref.dtype)


def _conv(ch, w, bias, g, layer, tc=512):
    b, l, c = ch.shape
    tc = min(tc, l)
    tok = pl.BlockSpec((1, tc, c), lambda bi, i: (bi, i, 0))
    return pl.pallas_call(
        functools.partial(_conv_body, tc),
        grid=(b, l // tc),
        in_specs=[tok] + [_const_spec(a.shape, layer) for a in (w, bias, g)],
        out_specs=tok,
        out_shape=jax.ShapeDtypeStruct((b, l, c), BF16),
        scratch_shapes=[pltpu.VMEM((tc + CONV_HIST, c), F32)],
        compiler_params=_cparams(("parallel", "arbitrary")),
        name="conv",
    )(ch, w, bias, g)


def _t5_bucket(dist):
    max_exact = REL_BUCKETS // 2
    d = jnp.maximum(dist, 0)
    df = jnp.maximum(d, 1).astype(F32)
    large = max_exact + (jnp.log(df / max_exact) / math.log(REL_MAX_DIST / max_exact)
                         * (REL_BUCKETS - max_exact)).astype(I32)
    large = jnp.minimum(large, REL_BUCKETS - 1)
    return jnp.where(d < max_exact, d, large)


def _pad_cols(w, width):
    return jnp.pad(w, ((0, 0), (0, width - w.shape[1])))


def _lane_rep(v):
    return jnp.broadcast_to(v[:, None], (v.shape[0], LANE))


def _pad_heads_rows(w, heads, dim):
    w = w.reshape(heads, dim, w.shape[1])
    return jnp.pad(w, ((0, 0), (0, HEAD_PAD - dim), (0, 0))).reshape(heads * HEAD_PAD, -1)


def _split_w_in(w_in):
    widths = (256, 256, 256, 256, 32, 8, 128, 128, 256, 16, 256, 512, 256, 128, 32)
    offs = np.cumsum((0,) + widths)
    return [w_in[:, offs[n]:offs[n + 1]] for n in range(len(widths))]


def _row(v):
    return v[None, :].astype(F32)


def _one_layer_params(w):
    (aq, ak, av, iq, ik, iw, bq, bk, bv, bg, br, cu, dcq, dckv, dkpe) = _split_w_in(w["w_in"])
    wn = jnp.concatenate([ak, _pad_cols(ik, LANE), bq, bk, bv, _pad_cols(bg, LANE), br, cu, dcq, dckv,
                          _pad_cols(dkpe, LANE)], axis=1).astype(BF16)
    wt = jnp.concatenate([_pad_heads_rows(aq.T, A_HEADS, A_HEAD_DIM), av.T, iq.T,
                          jnp.pad(iw.T, ((0, T_TOT - T_IW - IDX_HEADS), (0, 0)))], axis=0).astype(BF16)
    pad_to = lambda v, n: jnp.pad(v, (0, n - v.shape[0]))
    ukv = w["d_ukv"].reshape(D_KV_RANK, D_HEADS, D_NOPE + D_V)
    wuk = ukv[:, :, :D_NOPE].reshape(D_KV_RANK, D_HEADS * D_NOPE)
    wuv = ukv[:, :, D_NOPE:].reshape(D_KV_RANK, D_HEADS * D_V)
    gdk = w["d_k_norm"]
    return dict(
        gmix=_row(w["mix_norm"]), wn=wn, wt=wt,
        gaq=_lane_rep(pad_to(w["a_q_norm"], HEAD_PAD)),
        gak=_row(jnp.tile(w["a_k_norm"], A_HEADS)),
        wgu=jnp.pad(w["b_gate_up"], ((0, LANE - B_GATE_RANK), (0, 0))).astype(BF16),
        bgb=_row(w["b_gate_bias"]),
        gqa=_row(w["d_qa_norm"]),
        wuq=_pad_heads_rows(w["d_uq"].T, D_HEADS, D_QK).astype(BF16),
        gdq=_lane_rep(pad_to(w["d_q_norm"], HEAD_PAD)),
        gkva=_row(w["d_kva_norm"]), wuk=wuk.astype(BF16), wuvT=wuv.T.astype(BF16),
        gdk=_row(jnp.tile(gdk[:D_NOPE], D_HEADS)), gdkpe=_row(pad_to(gdk[D_NOPE:], LANE)),
        gbo=_row(jnp.tile(w["b_out_norm"], B_HEADS)),
        cw=jnp.pad(w["c_dw_w"][:, 0, :], ((0, CONV_HIST - C_KERNEL), (0, 0))).astype(F32),
        cb=_row(w["c_dw_b"]), cg=_row(w["c_norm"]),
    )


def _shared_tables(seq):
    hid = np.arange(GROUP_WIDTH) // 64
    bd = jnp.asarray(hid[:, None] == hid[None, :], dtype=BF16)
    half = D_ROPE // 2
    freqs = ROPE_THETA ** (-jnp.arange(half, dtype=F32) / half)
    ang = jnp.arange(seq).astype(F32)[:, None] * freqs[None, :]
    cos, sin = jnp.cos(ang), jnp.sin(ang)
    zeros = jnp.zeros((seq, LANE - D_ROPE), F32)
    cpe = jnp.concatenate([cos, cos, zeros], axis=1)
    spe = jnp.concatenate([-sin, sin, zeros], axis=1)
    return dict(bd=bd, cosT=cos.T, sinT=sin.T, cpe=cpe, spe=spe)


def _bias_tiles(rel_bias, blk):
    assert REL_MAX_DIST <= blk + 1
    kk = jnp.arange(blk)[:, None]
    qq = jnp.arange(blk)[None, :]
    rb = rel_bias.astype(F32).T

    def lookup(bucket):
        onehot = bucket[None, :, :, None] == jnp.arange(REL_BUCKETS)
        return jnp.sum(jnp.where(onehot, rb[:, None, None, :], 0.0), axis=-1) * LOG2E

    d0 = jnp.where(kk <= qq, lookup(_t5_bucket(qq - kk)), NEG)
    d1 = lookup(_t5_bucket(blk + qq - kk))
    far = jnp.broadcast_to(lookup(_t5_bucket(jnp.full((1, 1), 2 * blk, I32))), d1.shape)
    return jnp.stack([d0, d1, far])


def _key_bounds(knorm, bias_max):
    km = jnp.sqrt(jnp.max(knorm[..., 0], axis=1))
    kb_a = jnp.concatenate([km[:, :A_HEADS].reshape(-1), bias_max.reshape(1)])
    return kb_a, km[:, A_HEADS:].reshape(-1)


def kernel(x, ffn1_norm, ffn1_gate, ffn1_up, ffn1_down, mix_norm, w_in, a_q_norm, a_k_norm, rel_bias,
           b_gate_up, b_gate_bias, b_out_norm, c_dw_w, c_dw_b, c_norm, d_qa_norm, d_uq, d_kva_norm,
           d_ukv, d_q_norm, d_k_norm, w_out, ffn2_norm, ffn2_gate, ffn2_up, ffn2_down):
    w = dict(mix_norm=mix_norm, w_in=w_in, a_q_norm=a_q_norm, a_k_norm=a_k_norm, b_gate_up=b_gate_up,
             b_gate_bias=b_gate_bias, b_out_norm=b_out_norm, c_dw_w=c_dw_w, c_dw_b=c_dw_b, c_norm=c_norm,
             d_qa_norm=d_qa_norm, d_uq=d_uq, d_kva_norm=d_kva_norm, d_ukv=d_ukv,
             d_q_norm=d_q_norm, d_k_norm=d_k_norm)
    bsz, seq, dm = x.shape
    depth = w_in.shape[0]
    blk = min(ATT_BLK, seq)
    bt = _bias_tiles(rel_bias, blk)
    bias_max = jnp.max(bt)
    p = {**jax.vmap(_one_layer_params)(w), **_shared_tables(seq)}
    stacked_row = lambda v: v[:, None, :].astype(F32)
    ffn1 = (stacked_row(ffn1_norm), ffn1_gate, ffn1_up, ffn1_down)
    ffn2 = (stacked_row(ffn2_norm), ffn2_gate, ffn2_up, ffn2_down)
    wo = w_out.astype(BF16)
    x2 = x.reshape(bsz * seq, dm)
    for l in range(depth):
        x2 = _ffn(x2, *ffn1, l)
        (aqT, akh, avT, iqT, aik, iwT, bq, bk, bv, bla, br, ch, dqT, dkh, dvT, knorm) = _mix_in(
            x2.reshape(bsz, seq, dm), p, l, blk=blk)
        kb_a, kb_d = _key_bounds(knorm, bias_max)
        y_a = _attention(aqT, akh, avT, kb_a, dsa=(iqT, iwT, aik, bt), blk=blk)
        y_b = _gla(bq, bk, bv, bla, br, p["gbo"], p["bd"], l)
        y_c = _conv(ch, p["cw"], p["cb"], p["cg"], l)
        y_d = _attention(dqT, dkh, dvT, kb_d, blk=blk)
        ys = [y.reshape(bsz * seq, GROUP_WIDTH) for y in (y_a, y_b, y_c, y_d)]
        x2 = _ffn(x2, *ffn2, l, mix=(ys, wo))
    return x2.reshape(bsz, seq, dm)
```

```python
import functools
import math

import jax
import jax.numpy as jnp
import numpy as np
from jax import lax
from jax.experimental import pallas as pl
from jax.experimental.pallas import tpu as pltpu

F32 = jnp.float32
BF16 = jnp.bfloat16
I32 = jnp.int32
I16 = jnp.int16

EPS = 1e-6
GROUP_WIDTH = 256
A_HEADS, A_HEAD_DIM = 4, 64
IDX_HEADS, IDX_DIM = 8, 32
TOPK_MAX = 256
REL_BUCKETS, REL_MAX_DIST = 32, 128
B_HEADS, B_KEY_DIM, B_VAL_DIM, B_GATE_RANK = 4, 32, 64, 16
B_GATE_TAU = 16.0
B_CHUNK = 64
C_CHANNELS, C_KERNEL = 256, 31
D_HEADS, D_Q_RANK, D_KV_RANK, D_NOPE, D_ROPE, D_V = 4, 256, 128, 64, 32, 64
D_QK = D_NOPE + D_ROPE
ROPE_THETA = 10000.0

LANE = 128
SUBLANES = 8
HEAD_PAD = 128
ONES_ROWS = 16
ATT_BLK = 256
ATT_GROUP = 4
BOUND_SLACK = 1.01
FIXED_REF_MAX = 40.0
INT_MIN = -2 ** 31
I16_MIN = -2 ** 15
NEG = -1e30
LOG2E = math.log2(math.e)
VMEM_LIMIT = 56 * 1024 * 1024

N_AK, N_IK, N_BQ, N_BK, N_BV, N_BG, N_BR, N_CU, N_DCQ, N_DCKV, N_DKPE, N_TOT = (
    0, 256, 384, 512, 640, 896, 1024, 1280, 1792, 2048, 2176, 2304)
T_AQ, T_AV, T_IQ, T_IW, T_TOT = 0, 512, 768, 1024, 1040


def _dot(a, b):
    return jnp.dot(a, b, preferred_element_type=F32)


def _dot_nt(a, b):
    return lax.dot_general(a, b, (((1,), (1,)), ((), ())), preferred_element_type=F32)


def _dot_tn(a, b):
    return lax.dot_general(a, b, (((0,), (0,)), ((), ())), preferred_element_type=F32)


def _group_sum(x2, bd):
    hi = x2.astype(BF16)
    lo = (x2 - hi.astype(F32)).astype(BF16)
    return _dot(hi, bd) + _dot(lo, bd)


def _const_spec(shape, layer=None):
    nd = len(shape)
    if layer is None:
        return pl.BlockSpec(shape, lambda *_: (0,) * nd, pipeline_mode=pl.Buffered(1))
    return pl.BlockSpec((None,) + tuple(shape[1:]), lambda *_: (layer,) + (0,) * (nd - 1),
                        pipeline_mode=pl.Buffered(1))


def _cparams(sem):
    return pltpu.CompilerParams(dimension_semantics=sem, vmem_limit_bytes=VMEM_LIMIT)


def _ffn_body(has_mix, fc, *refs):
    if has_mix:
        x_ref, ya, yb, yc, yd, wo_ref, g_ref, wg_ref, wu_ref, wd_ref, o_ref, h_scr = refs
    else:
        x_ref, g_ref, wg_ref, wu_ref, wd_ref, o_ref, h_scr = refs
    x = x_ref[...]
    if has_mix:
        y = jnp.concatenate([ya[...], yb[...], yc[...], yd[...]], axis=-1)
        x = x + _dot(y, wo_ref[...])
    ms = jnp.mean(x * x, axis=-1, keepdims=True)
    xn = (x * lax.rsqrt(ms + EPS) * g_ref[...]).astype(BF16)
    d_ff = wg_ref.shape[1]
    for c in range(d_ff // fc):
        sl = slice(c * fc, (c + 1) * fc)
        gate = _dot(xn, wg_ref[:, sl].astype(BF16))
        up = _dot(xn, wu_ref[:, sl].astype(BF16))
        h_scr[:, sl] = (gate * jax.nn.sigmoid(gate) * up).astype(BF16)
    o_ref[...] = x + 0.5 * _dot(h_scr[...], wd_ref[...].astype(BF16))


def _ffn(x2, g, wg, wu, wd, layer, mix=None, tm=512, fc=256):
    m, d = x2.shape
    d_ff = wg.shape[2]
    tm = min(tm, m)
    row = lambda w: pl.BlockSpec((tm, w), lambda i: (i, 0))
    in_specs = [row(d)]
    args = [x2]
    if mix is not None:
        ys, wo = mix
        in_specs += [row(GROUP_WIDTH)] * 4 + [_const_spec(wo.shape, layer)]
        args += list(ys) + [wo]
    in_specs += [_const_spec(a.shape, layer) for a in (g, wg, wu, wd)]
    args += [g, wg, wu, wd]
    return pl.pallas_call(
        functools.partial(_ffn_body, mix is not None, fc),
        grid=(m // tm,),
        in_specs=in_specs,
        out_specs=row(d),
        out_shape=jax.ShapeDtypeStruct((m, d), F32),
        scratch_shapes=[pltpu.VMEM((tm, d_ff), BF16)],
        compiler_params=_cparams(("parallel",)),
        name="ffn_mix" if mix is not None else "ffn",
    )(*args)


def _mix_in_body(tm, blk,
                 x_ref, gmix_ref, wn_ref, wt_ref, bd_ref,
                 gaq_ref, gak_ref,
                 wgu_ref, bgb_ref,
                 gqa_ref, wuq_ref, gdq_ref, cosT_ref, sinT_ref,
                 gkva_ref, wuk_ref, wuvT_ref, gdk_ref, gdkpe_ref, cpe_ref, spe_ref,
                 aqT_ref, ak_ref, avT_ref, iqT_ref, ik_ref, iwT_ref,
                 bq_ref, bk_ref, bv_ref, bla_ref, br_ref,
                 ch_ref,
                 dqT_ref, dk_ref, dvT_ref, knorm_ref):
    nlt = tm // LANE
    x = x_ref[0]
    ms = jnp.mean(x * x, axis=-1, keepdims=True)
    xn = (x * lax.rsqrt(ms + EPS) * gmix_ref[...]).astype(BF16)
    bd = bd_ref[...]
    lane = lax.broadcasted_iota(I32, (tm, LANE), 1)

    def lanes(g):
        return jnp.tile(g, (1, nlt))

    z = _dot(xn, wn_ref[...])
    zt = _dot_nt(wt_ref[...], xn)

    def zs(off, width):
        return z[:, off:off + width]

    cq = zs(N_DCQ, D_Q_RANK)
    cq_ms = jnp.mean(cq * cq, axis=-1, keepdims=True)
    cqn = (cq * lax.rsqrt(cq_ms + EPS) * gqa_ref[...]).astype(BF16)
    ckv = zs(N_DCKV, D_KV_RANK)
    ckv_ms = jnp.mean(ckv * ckv, axis=-1, keepdims=True)
    ckvn = (ckv * lax.rsqrt(ckv_ms + EPS) * gkva_ref[...]).astype(BF16)
    dq = _dot_nt(wuq_ref[...], cqn).reshape(D_HEADS, HEAD_PAD, tm)
    kn = _dot(ckvn, wuk_ref[...])
    dv = _dot_nt(wuvT_ref[...], ckvn).astype(BF16)
    gate = _dot(zs(N_BG, LANE).astype(BF16), wgu_ref[...]) + bgb_ref[...]
    ak = zs(N_AK, GROUP_WIDTH)
    ak_ms = _group_sum(ak * ak, bd) * (1.0 / A_HEAD_DIM)
    kn_ss = _group_sum(kn * kn, bd)

    aq = zt[T_AQ:T_AQ + A_HEADS * HEAD_PAD].reshape(A_HEADS, HEAD_PAD, tm)
    aq_ms = jnp.sum(aq * aq, axis=1, keepdims=True) * (1.0 / A_HEAD_DIM)
    aq = aq * lax.rsqrt(aq_ms + EPS) * lanes(gaq_ref[...])[None] * (A_HEAD_DIM ** -0.5 * LOG2E)
    aqT_ref[0] = aq.reshape(A_HEADS * HEAD_PAD, tm).astype(BF16)
    def max_sq_norm(kh):
        f = kh.astype(F32)
        n2 = _dot((f * f).astype(BF16), jnp.ones((LANE, LANE), BF16))
        return jnp.max(n2, axis=0, keepdims=True)

    def with_ones_rows(vt, heads, dim):
        ones = jnp.ones((ONES_ROWS, tm), BF16)
        return jnp.concatenate([r for h in range(heads) for r in (vt[dim * h:dim * (h + 1)], ones)], axis=0)

    av = with_ones_rows(zt[T_AV:T_AV + GROUP_WIDTH].astype(BF16), A_HEADS, A_HEAD_DIM)
    for c in range(tm // blk):
        avT_ref[0, c] = av[:, c * blk:(c + 1) * blk]
    iqT_ref[0] = zt[T_IQ:T_IQ + IDX_HEADS * IDX_DIM].astype(BF16)
    iwT_ref[0] = zt[T_IW:T_IW + IDX_HEADS] * ((IDX_HEADS ** -0.5) * (IDX_DIM ** -0.5))

    ak = ak * lax.rsqrt(ak_ms + EPS) * gak_ref[...]
    for h in range(A_HEADS):
        pair = ak[:, LANE * (h // 2):LANE * (h // 2) + LANE]
        if h % 2 == 1:
            pair = pltpu.roll(pair, 64, 1)
        kh = jnp.where(lane < A_HEAD_DIM, pair, 0.0).astype(BF16)
        ak_ref[0, h] = kh
        knorm_ref[0, 0, h:h + 1, :] = max_sq_norm(kh)
    ik_ref[0] = zs(N_IK, LANE)[:, :IDX_DIM].astype(BF16)

    bq_ref[0] = zs(N_BQ, LANE) * (B_KEY_DIM ** -0.5)
    bk_ref[0] = zs(N_BK, LANE)
    bv_ref[0] = zs(N_BV, GROUP_WIDTH).astype(BF16)
    bla_ref[0] = (jnp.minimum(gate, 0.0) - jnp.log(1.0 + jnp.exp(-jnp.abs(gate)))) * (1.0 / B_GATE_TAU)
    br_ref[0] = zs(N_BR, GROUP_WIDTH)

    ca = zs(N_CU, C_CHANNELS)
    cg = zs(N_CU + C_CHANNELS, C_CHANNELS)
    ch_ref[0] = ca * jax.nn.sigmoid(cg)

    dq_ms = jnp.sum(dq * dq, axis=1, keepdims=True) * (1.0 / D_QK)
    dq = dq * lax.rsqrt(dq_ms + EPS) * lanes(gdq_ref[...])[None] * (D_QK ** -0.5 * LOG2E)
    half = D_ROPE // 2
    x1 = dq[:, D_NOPE:D_NOPE + half]
    x2 = dq[:, D_NOPE + half:D_QK]
    cs = cosT_ref[...][None]
    sn = sinT_ref[...][None]
    dq = jnp.concatenate([dq[:, :D_NOPE], x1 * cs - x2 * sn, x2 * cs + x1 * sn, dq[:, D_QK:]], axis=1)
    dqT_ref[0] = dq.reshape(D_HEADS * HEAD_PAD, tm).astype(BF16)

    dv = with_ones_rows(dv, D_HEADS, D_V)
    for c in range(tm // blk):
        dvT_ref[0, c] = dv[:, c * blk:(c + 1) * blk]
    kpe = zs(N_DKPE, LANE)
    ss = kn_ss + jnp.sum(kpe * kpe, axis=-1, keepdims=True)
    rinv = lax.rsqrt(ss * (1.0 / D_QK) + EPS)
    kn = kn * rinv * gdk_ref[...]
    pe = kpe * gdkpe_ref[...]
    partner = jnp.where(lane < half, pltpu.roll(pe, LANE - half, 1), pltpu.roll(pe, half, 1))
    pe = pe * cpe_ref[...] + partner * spe_ref[...]
    pe = pltpu.roll(pe, D_NOPE, 1)
    for h in range(D_HEADS):
        pair = kn[:, LANE * (h // 2):LANE * (h // 2) + LANE]
        rpair = rinv[:, LANE * (h // 2):LANE * (h // 2) + LANE]
        if h % 2 == 1:
            pair = pltpu.roll(pair, 64, 1)
        else:
            rpair = pltpu.roll(rpair, 64, 1)
        kh = jnp.where(lane < D_NOPE, pair, pe * rpair).astype(BF16)
        dk_ref[0, h] = kh
        knorm_ref[0, 0, A_HEADS + h:A_HEADS + h + 1, :] = max_sq_norm(kh)


def _mix_in(x3, p, layer, tm=512, blk=ATT_BLK):
    b, l, d = x3.shape
    tm = min(tm, l)
    grid = (b, l // tm)
    nck = l // blk
    tok = lambda w: pl.BlockSpec((1, tm, w), lambda bi, i: (bi, i, 0))
    tokT = lambda r: pl.BlockSpec((1, r, tm), lambda bi, i: (bi, 0, i))
    headk = pl.BlockSpec((1, 4, tm, HEAD_PAD), lambda bi, i: (bi, 0, i, 0))
    vrows = GROUP_WIDTH + 4 * ONES_ROWS
    chunkT = pl.BlockSpec((1, tm // blk, vrows, blk), lambda bi, i: (bi, i, 0, 0))
    postab = lambda r: pl.BlockSpec((r, tm), lambda bi, i: (0, i))
    posrow = pl.BlockSpec((tm, LANE), lambda bi, i: (i, 0))
    consts = [p["gmix"], p["wn"], p["wt"], p["bd"], p["gaq"], p["gak"], p["wgu"], p["bgb"],
              p["gqa"], p["wuq"], p["gdq"]]
    consts2 = [p["gkva"], p["wuk"], p["wuvT"], p["gdk"], p["gdkpe"]]
    lspec = lambda a: _const_spec(a.shape) if a is p["bd"] else _const_spec(a.shape, layer)
    in_specs = ([tok(d)] + [lspec(a) for a in consts]
                + [postab(D_ROPE // 2), postab(D_ROPE // 2)]
                + [lspec(a) for a in consts2] + [posrow, posrow])
    args = [x3] + consts + [p["cosT"], p["sinT"]] + consts2 + [p["cpe"], p["spe"]]
    sd = jax.ShapeDtypeStruct
    out_shape = [
        sd((b, A_HEADS * HEAD_PAD, l), BF16), sd((b, A_HEADS, l, HEAD_PAD), BF16),
        sd((b, nck, vrows, blk), BF16), sd((b, IDX_HEADS * IDX_DIM, l), BF16),
        sd((b, l, IDX_DIM), BF16), sd((b, IDX_HEADS, l), F32),
        sd((b, l, LANE), F32), sd((b, l, LANE), F32), sd((b, l, GROUP_WIDTH), BF16),
        sd((b, l, LANE), F32), sd((b, l, GROUP_WIDTH), F32),
        sd((b, l, C_CHANNELS), F32),
        sd((b, D_HEADS * HEAD_PAD, l), BF16), sd((b, D_HEADS, l, HEAD_PAD), BF16),
        sd((b, nck, vrows, blk), BF16),
        sd((b, l // tm, A_HEADS + D_HEADS, LANE), F32),
    ]
    out_specs = [
        tokT(A_HEADS * HEAD_PAD), headk, chunkT, tokT(IDX_HEADS * IDX_DIM),
        tok(IDX_DIM), tokT(IDX_HEADS),
        tok(LANE), tok(LANE), tok(GROUP_WIDTH), tok(LANE), tok(GROUP_WIDTH),
        tok(C_CHANNELS),
        tokT(D_HEADS * HEAD_PAD), headk, chunkT,
        pl.BlockSpec((1, 1, A_HEADS + D_HEADS, LANE), lambda bi, i: (bi, i, 0, 0)),
    ]
    return pl.pallas_call(
        functools.partial(_mix_in_body, tm, blk),
        grid=grid, in_specs=in_specs, out_specs=out_specs, out_shape=out_shape,
        compiler_params=_cparams(("parallel", "parallel")),
        name="mix_in",
    )(*args)


def _attn_body(nh, dv, blk, topk, is_dsa, *refs):
    if is_dsa:
        (kb_ref, qT_ref, k_ref, vT_ref, iqT_ref, wT_ref, ik_ref, bt_ref,
         o_ref, m_scr, acc_scr, key_scr, run_scr, plane_scr, active_scr) = refs
    else:
        kb_ref, qT_ref, k_ref, vT_ref, o_ref, m_scr, acc_scr = refs
    vr = dv + ONES_ROWS
    i = pl.program_id(1)
    t = blk
    row = lax.broadcasted_iota(I32, (t, t), 0)
    col = lax.broadcasted_iota(I32, (t, t), 1)
    causal_pen = jnp.where(row <= col, 0.0, NEG)

    m_scr[...] = jnp.full(m_scr.shape, NEG, F32)
    acc_scr[...] = jnp.zeros(acc_scr.shape, F32)

    if is_dsa:
        @pl.when(i == 0)
        def _():
            plane_scr[...] = jnp.zeros(plane_scr.shape, I32)

        def score_chunk(j, diag):
            r0 = pl.multiple_of(j * t, t)
            ikc = ik_ref[0, pl.ds(r0, t), :]
            s = jnp.zeros((t, t), F32)
            for h in range(IDX_HEADS):
                d = _dot(ikc, iqT_ref[0, IDX_DIM * h:IDX_DIM * (h + 1), :])
                s = s + jnp.maximum(d, 0.0) * wT_ref[0, h:h + 1, :]
            bits = lax.bitcast_convert_type(s, I32)
            key = jnp.where(bits < 0, bits ^ 0x7FFFFFFF, bits)
            if diag:
                key = jnp.where(row <= col, key, INT_MIN)
            key_scr[pl.ds(r0, t), :] = key
            u = key ^ INT_MIN
            w = [u[SUBLANES * r:SUBLANES * (r + 1), :] for r in range(32)]
            step, mask = 16, 0x0000FFFF
            while step:
                for lo in range(32):
                    if lo & step == 0:
                        hi = lo + step
                        swap = (w[lo] ^ jnp.right_shift(w[hi], step)) & mask
                        w[lo] = w[lo] ^ swap
                        w[hi] = w[hi] ^ jnp.left_shift(swap, step)
                step //= 2
                mask ^= (mask << step) & 0xFFFFFFFF
            c0 = pl.multiple_of(j * SUBLANES, SUBLANES)
            for p in range(32):
                plane_scr[p, pl.ds(c0, SUBLANES), :] = w[p]

        def score_pair(u, carry):
            score_chunk(2 * u, False)
            score_chunk(2 * u + 1, False)
            return carry

        lax.fori_loop(0, i // 2, score_pair, 0)

        @pl.when(i % 2 == 1)
        def _():
            score_chunk(i - 1, False)
            score_chunk(i, True)

        @pl.when(i % 2 == 0)
        def _():
            score_chunk(i, True)

        nrow = plane_scr.shape[1]
        in_range = lax.broadcasted_iota(I32, (nrow, t), 0) < (i + 1) * SUBLANES

        def col_count(words):
            pc = lax.population_count(words).reshape(nrow // SUBLANES, SUBLANES, t)
            return jnp.sum(jnp.sum(pc, axis=0), axis=0, keepdims=True)

        def decide(plane, active, n_gt, ans_u):
            ones = col_count(active & plane_scr[plane])
            take = n_gt + ones >= topk
            bit = lax.shift_right_logical(jnp.int32(INT_MIN), jnp.int32(plane))
            return (jnp.where(take, 0, -1), n_gt + jnp.where(take, 0, ones), ans_u | jnp.where(take, bit, 0))

        active_scr[...] = jnp.where(in_range, -1, 0)
        state = decide(0, active_scr[...], jnp.zeros((1, t), I32), jnp.zeros((1, t), I32))

        def plane_body(plane, state):
            flip, n_gt, ans_u = state
            active = active_scr[...] & (plane_scr[plane - 1] ^ flip)
            active_scr[...] = active
            return decide(plane, active, n_gt, ans_u)

        _, n_gt, ans_u = lax.fori_loop(1, 32, plane_body, state)
        ans = ans_u ^ INT_MIN
        need = (topk - n_gt).astype(F32)
        run_scr[...] = jnp.zeros(run_scr.shape, F32)
        stri = jnp.where(col < row, 1.0, 0.0).astype(BF16)

    def logits(h, r0):
        return _dot(k_ref[0, h, pl.ds(r0, t), :], qT_ref[0, HEAD_PAD * h:HEAD_PAD * (h + 1), :])

    def selection_pens(r0s):
        pens = []
        if is_dsa:
            for r0 in r0s:
                kc = key_scr[pl.ds(r0, t), :]
                eq = kc == ans
                eqf = jnp.where(eq, 1.0, 0.0)
                run = run_scr[0:1, :]
                rank = _dot(stri, eqf.astype(BF16)) + run
                run_scr[0:1, :] = run + jnp.sum(eqf, axis=0, keepdims=True)
                pens.append(jnp.where(kc > ans, 0.0, jnp.where(eq, jnp.where(rank < need, 0.0, NEG), NEG)))
        return pens

    b_idx = pl.program_id(0)
    bounds = []
    for h in range(nh):
        qh = qT_ref[0, HEAD_PAD * h:HEAD_PAD * (h + 1), :].astype(F32)
        bound = jnp.sqrt(jnp.sum(qh * qh, axis=0, keepdims=True)) * (kb_ref[b_idx * nh + h] * BOUND_SLACK)
        if is_dsa:
            bound = bound + kb_ref[kb_ref.shape[0] - 1]
        bounds.append(bound)
    bound_max = jnp.max(functools.reduce(jnp.maximum, bounds))
    fixed_ref_ok = bound_max <= FIXED_REF_MAX

    def fixed_ref_chunks(js, last_is_diag):
        n = len(js)
        r0s = [pl.multiple_of(j * t, t) for j in js]
        lgs = [[logits(h, r0s[c]) for c in range(n)] for h in range(nh)]
        pens = selection_pens(r0s)
        ps = []
        for h in range(nh):
            row_ps = []
            for c, j in enumerate(js):
                diag = last_is_diag and c == n - 1
                x = lgs[h][c]
                if is_dsa:
                    tile = 0 if diag else jnp.minimum(i - j, 2)
                    x = x + (pens[c] + bt_ref[tile, h])
                elif diag:
                    x = x + causal_pen
                row_ps.append(jnp.exp2(x - bounds[h]).astype(BF16))
            ps.append(row_ps[0] if n == 1 else jnp.concatenate(row_ps, axis=0))
        for h in range(nh):
            vs = slice(vr * h, vr * (h + 1))
            vt = [vT_ref[0, js[c], vs, :] for c in range(n)]
            acc_scr[vs, :] += _dot(vt[0] if n == 1 else jnp.concatenate(vt, axis=1), ps[h])

    def super_chunk(js, last_is_diag):
        n = len(js)
        r0s = [pl.multiple_of(j * t, t) for j in js]
        lgs = [[logits(h, r0s[c]) for c in range(n)] for h in range(nh)]
        pens = selection_pens(r0s)
        ps, alphas = [], []
        for h in range(nh):
            xs = []
            for c, j in enumerate(js):
                diag = last_is_diag and c == n - 1
                lg = lgs[h][c]
                if is_dsa:
                    tile = 0 if diag else jnp.minimum(i - j, 2)
                    lg = lg + (pens[c] + bt_ref[tile, h])
                elif diag:
                    lg = lg + causal_pen
                xs.append(lg)
            m_old = m_scr[h, 0:1, :]
            m_new = m_old
            for x in xs:
                m_new = jnp.maximum(m_new, jnp.max(x, axis=0, keepdims=True))
            alpha = jnp.exp2(m_old - m_new)
            m_scr[h, 0:1, :] = m_new
            ps.append([jnp.exp2((x - m_new).astype(BF16)) for x in xs])
            alphas.append(alpha)
        for h in range(nh):
            vs = slice(vr * h, vr * (h + 1))
            pv = _dot(vT_ref[0, js[0], vs, :], ps[h][0])
            for c in range(1, n):
                pv = pv + _dot(vT_ref[0, js[c], vs, :], ps[h][c])
            acc_scr[vs, :] = alphas[h] * acc_scr[vs, :] + pv

    def run_chunks(step):
        def group_body(u, carry):
            step([ATT_GROUP * u + c for c in range(ATT_GROUP)], False)
            return carry

        lax.fori_loop(0, i // ATT_GROUP, group_body, 0)
        for rem in range(ATT_GROUP):
            @pl.when(i % ATT_GROUP == rem)
            def _():
                step([i - rem + c for c in range(rem + 1)], True)

    pl.when(fixed_ref_ok)(functools.partial(run_chunks, fixed_ref_chunks))
    pl.when(jnp.logical_not(fixed_ref_ok))(functools.partial(run_chunks, super_chunk))

    outs = []
    for h in range(nh):
        outs.append(acc_scr[vr * h:vr * h + dv, :] / acc_scr[vr * h + dv:vr * h + dv + 1, :])
    o_ref[0] = jnp.transpose(jnp.concatenate(outs, axis=0)).astype(o_ref.dtype)


def _attention(qT, k, vT, kbound, dsa=None, blk=ATT_BLK):
    b, nh, l, _ = k.shape
    vr = vT.shape[2] // nh
    dv = vr - ONES_ROWS
    grid = (b, l // blk)
    qspec = lambda r: pl.BlockSpec((1, r, blk), lambda bi, i: (bi, 0, i))
    kspec = pl.BlockSpec((1, nh, l, HEAD_PAD), lambda bi, i: (bi, 0, 0, 0))
    vspec = pl.BlockSpec((1, l // blk, nh * vr, blk), lambda bi, i: (bi, 0, 0, 0))
    in_specs = [pl.BlockSpec(memory_space=pltpu.SMEM), qspec(nh * HEAD_PAD), kspec, vspec]
    args = [kbound, qT, k, vT]
    scratch = [pltpu.VMEM((nh, 8, blk), F32), pltpu.VMEM((nh * vr, blk), F32)]
    topk = 0
    if dsa is not None:
        iqT, wT, ik, bt = dsa
        topk = min(TOPK_MAX, l // 4)
        in_specs += [qspec(IDX_HEADS * IDX_DIM), qspec(IDX_HEADS),
                     pl.BlockSpec((1, l, IDX_DIM), lambda bi, i: (bi, 0, 0)),
                     _const_spec(bt.shape)]
        args += [iqT, wT, ik, bt]
        assert blk == 32 * SUBLANES
        nrow = (l // blk) * SUBLANES
        scratch += [pltpu.VMEM((l, blk), I32), pltpu.VMEM((8, blk), F32),
                    pltpu.VMEM((32, nrow, blk), I32), pltpu.VMEM((nrow, blk), I32)]
    return pl.pallas_call(
        functools.partial(_attn_body, nh, dv, blk, topk, dsa is not None),
        grid=grid, in_specs=in_specs,
        out_specs=pl.BlockSpec((1, blk, nh * dv), lambda bi, i: (bi, i, 0)),
        out_shape=jax.ShapeDtypeStruct((b, l, nh * dv), BF16),
        scratch_shapes=scratch,
        compiler_params=_cparams(("parallel", "arbitrary")),
        name="dsa_attn" if dsa is not None else "mla_attn",
    )(*args)


def _gla_body(tg, q_ref, k_ref, v_ref, la_ref, r_ref, go_ref, bd_ref, o_ref, st_scr, o_scr):
    @pl.when(pl.program_id(1) == 0)
    def _():
        st_scr[...] = jnp.zeros(st_scr.shape, F32)

    cs = B_CHUNK
    la = la_ref[0]
    rl = lax.broadcasted_iota(I32, (tg, LANE), 0) & (cs - 1)
    b = la
    s = 1
    while s < cs:
        b = b + jnp.where(rl >= s, pltpu.roll(b, s, 0), 0.0)
        s *= 2
    q = q_ref[0]
    k = k_ref[0]
    qb = q * jnp.exp(b)
    ci = lax.broadcasted_iota(I32, (cs, cs), 0)
    cj = lax.broadcasted_iota(I32, (cs, cs), 1)
    for c in range(tg // cs):
        sl = slice(c * cs, (c + 1) * cs)
        bc = b[sl]
        mid = bc[cs // 2:cs // 2 + 1]
        last = bc[cs - 1:cs]
        qe = (q[sl] * jnp.exp(bc - mid)).astype(BF16)
        ke = (k[sl] * jnp.exp(mid - bc)).astype(BF16)
        kd = (k[sl] * jnp.exp(last - bc)).astype(BF16)
        qbc = qb[sl].astype(BF16)
        dl = jnp.exp(last)
        vc = v_ref[0, sl, :]
        for h in range(B_HEADS):
            ks = slice(B_KEY_DIM * h, B_KEY_DIM * (h + 1))
            vs = slice(B_VAL_DIM * h, B_VAL_DIM * (h + 1))
            a = jnp.where(cj <= ci, _dot_nt(qe[:, ks], ke[:, ks]), 0.0)
            st = st_scr[h]
            o_scr[sl, vs] = _dot(a.astype(BF16), vc[:, vs]) + _dot_nt(qbc[:, ks], st.astype(BF16))
            st_scr[h] = st * dl[:, ks] + _dot_tn(vc[:, vs], kd[:, ks])
    o = o_scr[...]
    ms = _group_sum(o * o, bd_ref[...]) * (1.0 / B_VAL_DIM)
    r = r_ref[0]
    o_ref[0] = (o * lax.rsqrt(ms + EPS) * go_ref[...] * (r * jax.nn.sigmoid(r))).astype(o_ref.dtype)


def _gla(bq, bk, bv, bla, br, go, bd, layer, tg=512):
    b, l, _ = bq.shape
    tg = min(tg, l)
    tok = lambda w: pl.BlockSpec((1, tg, w), lambda bi, i: (bi, i, 0))
    return pl.pallas_call(
        functools.partial(_gla_body, tg),
        grid=(b, l // tg),
        in_specs=[tok(LANE), tok(LANE), tok(GROUP_WIDTH), tok(LANE), tok(GROUP_WIDTH),
                  _const_spec(go.shape, layer), _const_spec(bd.shape)],
        out_specs=tok(GROUP_WIDTH),
        out_shape=jax.ShapeDtypeStruct((b, l, GROUP_WIDTH), BF16),
        scratch_shapes=[pltpu.VMEM((B_HEADS, B_VAL_DIM, B_KEY_DIM), F32),
                        pltpu.VMEM((tg, GROUP_WIDTH), F32)],
        compiler_params=_cparams(("parallel", "arbitrary")),
        name="gla",
    )(bq, bk, bv, bla, br, go, bd)


CONV_HIST = 32


def _conv_body(tc, h_ref, w_ref, b_ref, g_ref, o_ref, buf):
    @pl.when(pl.program_id(1) == 0)
    def _():
        buf[0:CONV_HIST, :] = jnp.zeros((CONV_HIST, C_CHANNELS), F32)

    @pl.when(pl.program_id(1) > 0)
    def _():
        buf[0:CONV_HIST, :] = buf[tc:tc + CONV_HIST, :]

    buf[CONV_HIST:CONV_HIST + tc, :] = h_ref[0]
    acc = jnp.zeros((tc, C_CHANNELS), F32) + b_ref[...]
    base = CONV_HIST - (C_KERNEL - 1)
    hb = buf[...]
    rows = tc + CONV_HIST
    for r in range(SUBLANES):
        shifted = hb if r == 0 else pltpu.roll(hb, rows - r, 0)
        for j in range(C_KERNEL):
            if (base + j) % SUBLANES == r:
                a0 = base + j - r
                acc = acc + shifted[a0:a0 + tc, :] * w_ref[j:j + 1, :]
    ms = jnp.mean(acc * acc, axis=-1, keepdims=True)
    y = acc * lax.rsqrt(ms + EPS) * g_ref[...]
    o_ref[0] = (y * jax.nn.sigmoid(y)).astype(o_ref.dtype)


def _conv(ch, w, bias, g, layer, tc=512):
    b, l, c = ch.shape
    tc = min(tc, l)
    tok = pl.BlockSpec((1, tc, c), lambda bi, i: (bi, i, 0))
    return pl.pallas_call(
        functools.partial(_conv_body, tc),
        grid=(b, l // tc),
        in_specs=[tok] + [_const_spec(a.shape, layer) for a in (w, bias, g)],
        out_specs=tok,
        out_shape=jax.ShapeDtypeStruct((b, l, c), BF16),
        scratch_shapes=[pltpu.VMEM((tc + CONV_HIST, c), F32)],
        compiler_params=_cparams(("parallel", "arbitrary")),
        name="conv",
    )(ch, w, bias, g)


def _t5_bucket(dist):
    max_exact = REL_BUCKETS // 2
    d = jnp.maximum(dist, 0)
    df = jnp.maximum(d, 1).astype(F32)
    large = max_exact + (jnp.log(df / max_exact) / math.log(REL_MAX_DIST / max_exact)
                         * (REL_BUCKETS - max_exact)).astype(I32)
    large = jnp.minimum(large, REL_BUCKETS - 1)
    return jnp.where(d < max_exact, d, large)


def _pad_cols(w, width):
    return jnp.pad(w, ((0, 0), (0, width - w.shape[1])))


def _lane_rep(v):
    return jnp.broadcast_to(v[:, None], (v.shape[0], LANE))


def _pad_heads_rows(w, heads, dim):
    w = w.reshape(heads, dim, w.shape[1])
    return jnp.pad(w, ((0, 0), (0, HEAD_PAD - dim), (0, 0))).reshape(heads * HEAD_PAD, -1)


def _split_w_in(w_in):
    widths = (256, 256, 256, 256, 32, 8, 128, 128, 256, 16, 256, 512, 256, 128, 32)
    offs = np.cumsum((0,) + widths)
    return [w_in[:, offs[n]:offs[n + 1]] for n in range(len(widths))]


def _row(v):
    return v[None, :].astype(F32)


def _one_layer_params(w):
    (aq, ak, av, iq, ik, iw, bq, bk, bv, bg, br, cu, dcq, dckv, dkpe) = _split_w_in(w["w_in"])
    wn = jnp.concatenate([ak, _pad_cols(ik, LANE), bq, bk, bv, _pad_cols(bg, LANE), br, cu, dcq, dckv,
                          _pad_cols(dkpe, LANE)], axis=1).astype(BF16)
    wt = jnp.concatenate([_pad_heads_rows(aq.T, A_HEADS, A_HEAD_DIM), av.T, iq.T,
                          jnp.pad(iw.T, ((0, T_TOT - T_IW - IDX_HEADS), (0, 0)))], axis=0).astype(BF16)
    pad_to = lambda v, n: jnp.pad(v, (0, n - v.shape[0]))
    ukv = w["d_ukv"].reshape(D_KV_RANK, D_HEADS, D_NOPE + D_V)
    wuk = ukv[:, :, :D_NOPE].reshape(D_KV_RANK, D_HEADS * D_NOPE)
    wuv = ukv[:, :, D_NOPE:].reshape(D_KV_RANK, D_HEADS * D_V)
    gdk = w["d_k_norm"]
    return dict(
        gmix=_row(w["mix_norm"]), wn=wn, wt=wt,
        gaq=_lane_rep(pad_to(w["a_q_norm"], HEAD_PAD)),
        gak=_row(jnp.tile(w["a_k_norm"], A_HEADS)),
        wgu=jnp.pad(w["b_gate_up"], ((0, LANE - B_GATE_RANK), (0, 0))).astype(BF16),
        bgb=_row(w["b_gate_bias"]),
        gqa=_row(w["d_qa_norm"]),
        wuq=_pad_heads_rows(w["d_uq"].T, D_HEADS, D_QK).astype(BF16),
        gdq=_lane_rep(pad_to(w["d_q_norm"], HEAD_PAD)),
        gkva=_row(w["d_kva_norm"]), wuk=wuk.astype(BF16), wuvT=wuv.T.astype(BF16),
        gdk=_row(jnp.tile(gdk[:D_NOPE], D_HEADS)), gdkpe=_row(pad_to(gdk[D_NOPE:], LANE)),
        gbo=_row(jnp.tile(w["b_out_norm"], B_HEADS)),
        cw=jnp.pad(w["c_dw_w"][:, 0, :], ((0, CONV_HIST - C_KERNEL), (0, 0))).astype(F32),
        cb=_row(w["c_dw_b"]), cg=_row(w["c_norm"]),
    )


def _shared_tables(seq):
    hid = np.arange(GROUP_WIDTH) // 64
    bd = jnp.asarray(hid[:, None] == hid[None, :], dtype=BF16)
    half = D_ROPE // 2
    freqs = ROPE_THETA ** (-jnp.arange(half, dtype=F32) / half)
    ang = jnp.arange(seq).astype(F32)[:, None] * freqs[None, :]
    cos, sin = jnp.cos(ang), jnp.sin(ang)
    zeros = jnp.zeros((seq, LANE - D_ROPE), F32)
    cpe = jnp.concatenate([cos, cos, zeros], axis=1)
    spe = jnp.concatenate([-sin, sin, zeros], axis=1)
    return dict(bd=bd, cosT=cos.T, sinT=sin.T, cpe=cpe, spe=spe)


def _bias_tiles(rel_bias, blk):
    assert REL_MAX_DIST <= blk + 1
    kk = jnp.arange(blk)[:, None]
    qq = jnp.arange(blk)[None, :]
    rb = rel_bias.astype(F32).T

    def lookup(bucket):
        onehot = bucket[None, :, :, None] == jnp.arange(REL_BUCKETS)
        return jnp.sum(jnp.where(onehot, rb[:, None, None, :], 0.0), axis=-1) * LOG2E

    d0 = jnp.where(kk <= qq, lookup(_t5_bucket(qq - kk)), NEG)
    d1 = lookup(_t5_bucket(blk + qq - kk))
    far = jnp.broadcast_to(lookup(_t5_bucket(jnp.full((1, 1), 2 * blk, I32))), d1.shape)
    return jnp.stack([d0, d1, far])


def _key_bounds(knorm, bias_max):
    km = jnp.sqrt(jnp.max(knorm[..., 0], axis=1))
    kb_a = jnp.concatenate([km[:, :A_HEADS].reshape(-1), bias_max.reshape(1)])
    return kb_a, km[:, A_HEADS:].reshape(-1)


def kernel(x, ffn1_norm, ffn1_gate, ffn1_up, ffn1_down, mix_norm, w_in, a_q_norm, a_k_norm, rel_bias,
           b_gate_up, b_gate_bias, b_out_norm, c_dw_w, c_dw_b, c_norm, d_qa_norm, d_uq, d_kva_norm,
           d_ukv, d_q_norm, d_k_norm, w_out, ffn2_norm, ffn2_gate, ffn2_up, ffn2_down):
    w = dict(mix_norm=mix_norm, w_in=w_in, a_q_norm=a_q_norm, a_k_norm=a_k_norm, b_gate_up=b_gate_up,
             b_gate_bias=b_gate_bias, b_out_norm=b_out_norm, c_dw_w=c_dw_w, c_dw_b=c_dw_b, c_norm=c_norm,
             d_qa_norm=d_qa_norm, d_uq=d_uq, d_kva_norm=d_kva_norm, d_ukv=d_ukv,
             d_q_norm=d_q_norm, d_k_norm=d_k_norm)
    bsz, seq, dm = x.shape
    depth = w_in.shape[0]
    blk = min(ATT_BLK, seq)
    bt = _bias_tiles(rel_bias, blk)
    bias_max = jnp.max(bt)
    p = {**jax.vmap(_one_layer_params)(w), **_shared_tables(seq)}
    stacked_row = lambda v: v[:, None, :].astype(F32)
    ffn1 = (stacked_row(ffn1_norm), ffn1_gate, ffn1_up, ffn1_down)
    ffn2 = (stacked_row(ffn2_norm), ffn2_gate, ffn2_up, ffn2_down)
    wo = w_out.astype(BF16)
    x2 = x.reshape(bsz * seq, dm)
    for l in range(depth):
        x2 = _ffn(x2, *ffn1, l)
        (aqT, akh, avT, iqT, aik, iwT, bq, bk, bv, bla, br, ch, dqT, dkh, dvT, knorm) = _mix_in(
            x2.reshape(bsz, seq, dm), p, l, blk=blk)
        kb_a, kb_d = _key_bounds(knorm, bias_max)
        y_a = _attention(aqT, akh, avT, kb_a, dsa=(iqT, iwT, aik, bt), blk=blk)
        y_b = _gla(bq, bk, bv, bla, br, p["gbo"], p["bd"], l)
        y_c = _conv(ch, p["cw"], p["cb"], p["cg"], l)
        y_d = _attention(dqT, dkh, dvT, kb_d, blk=blk)
        ys = [y.reshape(bsz * seq, GROUP_WIDTH) for y in (y_a, y_b, y_c, y_d)]
        x2 = _ffn(x2, *ffn2, l, mix=(ys, wo))
    return x2.reshape(bsz, seq, dm)
```

```python
import functools
import math

import jax
import jax.numpy as jnp
import numpy as np
from jax import lax
from jax.experimental import pallas as pl
from jax.experimental.pallas import tpu as pltpu

F32 = jnp.float32
BF16 = jnp.bfloat16
I32 = jnp.int32
I16 = jnp.int16

EPS = 1e-6
GROUP_WIDTH = 256
A_HEADS, A_HEAD_DIM = 4, 64
IDX_HEADS, IDX_DIM = 8, 32
TOPK_MAX = 256
REL_BUCKETS, REL_MAX_DIST = 32, 128
B_HEADS, B_KEY_DIM, B_VAL_DIM, B_GATE_RANK = 4, 32, 64, 16
B_GATE_TAU = 16.0
B_CHUNK = 64
C_CHANNELS, C_KERNEL = 256, 31
D_HEADS, D_Q_RANK, D_KV_RANK, D_NOPE, D_ROPE, D_V = 4, 256, 128, 64, 32, 64
D_QK = D_NOPE + D_ROPE
ROPE_THETA = 10000.0

LANE = 128
SUBLANES = 8
HEAD_PAD = 128
ONES_ROWS = 16
ATT_BLK = 256
ATT_GROUP = 4
BOUND_SLACK = 1.01
FIXED_REF_MAX = 40.0
INT_MIN = -2 ** 31
I16_MIN = -2 ** 15
NEG = -1e30
LOG2E = math.log2(math.e)
VMEM_LIMIT = 56 * 1024 * 1024

N_AK, N_IK, N_BQ, N_BK, N_BV, N_BG, N_BR, N_CU, N_DCQ, N_DCKV, N_DKPE, N_TOT = (
    0, 256, 384, 512, 640, 896, 1024, 1280, 1792, 2048, 2176, 2304)
T_AQ, T_AV, T_IQ, T_IW, T_TOT = 0, 512, 768, 1024, 1040


def _dot(a, b):
    return jnp.dot(a, b, preferred_element_type=F32)


def _dot_nt(a, b):
    return lax.dot_general(a, b, (((1,), (1,)), ((), ())), preferred_element_type=F32)


def _dot_tn(a, b):
    return lax.dot_general(a, b, (((0,), (0,)), ((), ())), preferred_element_type=F32)


def _group_sum(x2, bd):
    hi = x2.astype(BF16)
    lo = (x2 - hi.astype(F32)).astype(BF16)
    return _dot(hi, bd) + _dot(lo, bd)


def _const_spec(shape, layer=None):
    nd = len(shape)
    if layer is None:
        return pl.BlockSpec(shape, lambda *_: (0,) * nd, pipeline_mode=pl.Buffered(1))
    return pl.BlockSpec((None,) + tuple(shape[1:]), lambda *_: (layer,) + (0,) * (nd - 1),
                        pipeline_mode=pl.Buffered(1))


def _cparams(sem):
    return pltpu.CompilerParams(dimension_semantics=sem, vmem_limit_bytes=VMEM_LIMIT)


def _ffn_body(has_mix, fc, *refs):
    if has_mix:
        x_ref, ya, yb, yc, yd, wo_ref, g_ref, wg_ref, wu_ref, wd_ref, o_ref, h_scr = refs
    else:
        x_ref, g_ref, wg_ref, wu_ref, wd_ref, o_ref, h_scr = refs
    x = x_ref[...]
    if has_mix:
        y = jnp.concatenate([ya[...], yb[...], yc[...], yd[...]], axis=-1)
        x = x + _dot(y, wo_ref[...])
    ms = jnp.mean(x * x, axis=-1, keepdims=True)
    xn = (x * lax.rsqrt(ms + EPS) * g_ref[...]).astype(BF16)
    d_ff = wg_ref.shape[1]
    for c in range(d_ff // fc):
        sl = slice(c * fc, (c + 1) * fc)
        gate = _dot(xn, wg_ref[:, sl].astype(BF16))
        up = _dot(xn, wu_ref[:, sl].astype(BF16))
        h_scr[:, sl] = (gate * jax.nn.sigmoid(gate) * up).astype(BF16)
    o_ref[...] = x + 0.5 * _dot(h_scr[...], wd_ref[...].astype(BF16))


def _ffn(x2, g, wg, wu, wd, layer, mix=None, tm=512, fc=256):
    m, d = x2.shape
    d_ff = wg.shape[2]
    tm = min(tm, m)
    row = lambda w: pl.BlockSpec((tm, w), lambda i: (i, 0))
    in_specs = [row(d)]
    args = [x2]
    if mix is not None:
        ys, wo = mix
        in_specs += [row(GROUP_WIDTH)] * 4 + [_const_spec(wo.shape, layer)]
        args += list(ys) + [wo]
    in_specs += [_const_spec(a.shape, layer) for a in (g, wg, wu, wd)]
    args += [g, wg, wu, wd]
    return pl.pallas_call(
        functools.partial(_ffn_body, mix is not None, fc),
        grid=(m // tm,),
        in_specs=in_specs,
        out_specs=row(d),
        out_shape=jax.ShapeDtypeStruct((m, d), F32),
        scratch_shapes=[pltpu.VMEM((tm, d_ff), BF16)],
        compiler_params=_cparams(("parallel",)),
        name="ffn_mix" if mix is not None else "ffn",
    )(*args)


def _mix_in_body(tm, blk,
                 x_ref, gmix_ref, wn_ref, wt_ref, bd_ref,
                 gaq_ref, gak_ref,
                 wgu_ref, bgb_ref,
                 gqa_ref, wuq_ref, gdq_ref, cosT_ref, sinT_ref,
                 gkva_ref, wuk_ref, wuvT_ref, gdk_ref, gdkpe_ref, cpe_ref, spe_ref,
                 aqT_ref, ak_ref, avT_ref, iqT_ref, ik_ref, iwT_ref,
                 bq_ref, bk_ref, bv_ref, bla_ref, br_ref,
                 ch_ref,
                 dqT_ref, dk_ref, dvT_ref, knorm_ref):
    nlt = tm // LANE
    x = x_ref[0]
    ms = jnp.mean(x * x, axis=-1, keepdims=True)
    xn = (x * lax.rsqrt(ms + EPS) * gmix_ref[...]).astype(BF16)
    bd = bd_ref[...]
    lane = lax.broadcasted_iota(I32, (tm, LANE), 1)

    def lanes(g):
        return jnp.tile(g, (1, nlt))

    z = _dot(xn, wn_ref[...])
    zt = _dot_nt(wt_ref[...], xn)

    def zs(off, width):
        return z[:, off:off + width]

    cq = zs(N_DCQ, D_Q_RANK)
    cq_ms = jnp.mean(cq * cq, axis=-1, keepdims=True)
    cqn = (cq * lax.rsqrt(cq_ms + EPS) * gqa_ref[...]).astype(BF16)
    ckv = zs(N_DCKV, D_KV_RANK)
    ckv_ms = jnp.mean(ckv * ckv, axis=-1, keepdims=True)
    ckvn = (ckv * lax.rsqrt(ckv_ms + EPS) * gkva_ref[...]).astype(BF16)
    dq = _dot_nt(wuq_ref[...], cqn).reshape(D_HEADS, HEAD_PAD, tm)
    kn = _dot(ckvn, wuk_ref[...])
    dv = _dot_nt(wuvT_ref[...], ckvn).astype(BF16)
    gate = _dot(zs(N_BG, LANE).astype(BF16), wgu_ref[...]) + bgb_ref[...]
    ak = zs(N_AK, GROUP_WIDTH)
    ak_ms = _group_sum(ak * ak, bd) * (1.0 / A_HEAD_DIM)
    kn_ss = _group_sum(kn * kn, bd)

    aq = zt[T_AQ:T_AQ + A_HEADS * HEAD_PAD].reshape(A_HEADS, HEAD_PAD, tm)
    aq_ms = jnp.sum(aq * aq, axis=1, keepdims=True) * (1.0 / A_HEAD_DIM)
    aq = aq * lax.rsqrt(aq_ms + EPS) * lanes(gaq_ref[...])[None] * (A_HEAD_DIM ** -0.5 * LOG2E)
    aqT_ref[0] = aq.reshape(A_HEADS * HEAD_PAD, tm).astype(BF16)
    def max_sq_norm(kh):
        f = kh.astype(F32)
        n2 = _dot((f * f).astype(BF16), jnp.ones((LANE, LANE), BF16))
        return jnp.max(n2, axis=0, keepdims=True)

    def with_ones_rows(vt, heads, dim):
        ones = jnp.ones((ONES_ROWS, tm), BF16)
        return jnp.concatenate([r for h in range(heads) for r in (vt[dim * h:dim * (h + 1)], ones)], axis=0)

    av = with_ones_rows(zt[T_AV:T_AV + GROUP_WIDTH].astype(BF16), A_HEADS, A_HEAD_DIM)
    for c in range(tm // blk):
        avT_ref[0, c] = av[:, c * blk:(c + 1) * blk]
    iqT_ref[0] = zt[T_IQ:T_IQ + IDX_HEADS * IDX_DIM].astype(BF16)
    iwT_ref[0] = zt[T_IW:T_IW + IDX_HEADS] * ((IDX_HEADS ** -0.5) * (IDX_DIM ** -0.5))

    ak = ak * lax.rsqrt(ak_ms + EPS) * gak_ref[...]
    for h in range(A_HEADS):
        pair = ak[:, LANE * (h // 2):LANE * (h // 2) + LANE]
        if h % 2 == 1:
            pair = pltpu.roll(pair, 64, 1)
        kh = jnp.where(lane < A_HEAD_DIM, pair, 0.0).astype(BF16)
        ak_ref[0, h] = kh
        knorm_ref[0, 0, h:h + 1, :] = max_sq_norm(kh)
    ik_ref[0] = zs(N_IK, LANE)[:, :IDX_DIM].astype(BF16)

    bq_ref[0] = zs(N_BQ, LANE) * (B_KEY_DIM ** -0.5)
    bk_ref[0] = zs(N_BK, LANE)
    bv_ref[0] = zs(N_BV, GROUP_WIDTH).astype(BF16)
    bla_ref[0] = (jnp.minimum(gate, 0.0) - jnp.log(1.0 + jnp.exp(-jnp.abs(gate)))) * (1.0 / B_GATE_TAU)
    br_ref[0] = zs(N_BR, GROUP_WIDTH)

    ca = zs(N_CU, C_CHANNELS)
    cg = zs(N_CU + C_CHANNELS, C_CHANNELS)
    ch_ref[0] = ca * jax.nn.sigmoid(cg)

    dq_ms = jnp.sum(dq * dq, axis=1, keepdims=True) * (1.0 / D_QK)
    dq = dq * lax.rsqrt(dq_ms + EPS) * lanes(gdq_ref[...])[None] * (D_QK ** -0.5 * LOG2E)
    half = D_ROPE // 2
    x1 = dq[:, D_NOPE:D_NOPE + half]
    x2 = dq[:, D_NOPE + half:D_QK]
    cs = cosT_ref[...][None]
    sn = sinT_ref[...][None]
    dq = jnp.concatenate([dq[:, :D_NOPE], x1 * cs - x2 * sn, x2 * cs + x1 * sn, dq[:, D_QK:]], axis=1)
    dqT_ref[0] = dq.reshape(D_HEADS * HEAD_PAD, tm).astype(BF16)

    dv = with_ones_rows(dv, D_HEADS, D_V)
    for c in range(tm // blk):
        dvT_ref[0, c] = dv[:, c * blk:(c + 1) * blk]
    kpe = zs(N_DKPE, LANE)
    ss = kn_ss + jnp.sum(kpe * kpe, axis=-1, keepdims=True)
    rinv = lax.rsqrt(ss * (1.0 / D_QK) + EPS)
    kn = kn * rinv * gdk_ref[...]
    pe = kpe * gdkpe_ref[...]
    partner = jnp.where(lane < half, pltpu.roll(pe, LANE - half, 1), pltpu.roll(pe, half, 1))
    pe = pe * cpe_ref[...] + partner * spe_ref[...]
    pe = pltpu.roll(pe, D_NOPE, 1)
    for h in range(D_HEADS):
        pair = kn[:, LANE * (h // 2):LANE * (h // 2) + LANE]
        rpair = rinv[:, LANE * (h // 2):LANE * (h // 2) + LANE]
        if h % 2 == 1:
            pair = pltpu.roll(pair, 64, 1)
        else:
            rpair = pltpu.roll(rpair, 64, 1)
        kh = jnp.where(lane < D_NOPE, pair, pe * rpair).astype(BF16)
        dk_ref[0, h] = kh
        knorm_ref[0, 0, A_HEADS + h:A_HEADS + h + 1, :] = max_sq_norm(kh)


def _mix_in(x3, p, layer, tm=512, blk=ATT_BLK):
    b, l, d = x3.shape
    tm = min(tm, l)
    grid = (b, l // tm)
    nck = l // blk
    tok = lambda w: pl.BlockSpec((1, tm, w), lambda bi, i: (bi, i, 0))
    tokT = lambda r: pl.BlockSpec((1, r, tm), lambda bi, i: (bi, 0, i))
    headk = pl.BlockSpec((1, 4, tm, HEAD_PAD), lambda bi, i: (bi, 0, i, 0))
    vrows = GROUP_WIDTH + 4 * ONES_ROWS
    chunkT = pl.BlockSpec((1, tm // blk, vrows, blk), lambda bi, i: (bi, i, 0, 0))
    postab = lambda r: pl.BlockSpec((r, tm), lambda bi, i: (0, i))
    posrow = pl.BlockSpec((tm, LANE), lambda bi, i: (i, 0))
    consts = [p["gmix"], p["wn"], p["wt"], p["bd"], p["gaq"], p["gak"], p["wgu"], p["bgb"],
              p["gqa"], p["wuq"], p["gdq"]]
    consts2 = [p["gkva"], p["wuk"], p["wuvT"], p["gdk"], p["gdkpe"]]
    lspec = lambda a: _const_spec(a.shape) if a is p["bd"] else _const_spec(a.shape, layer)
    in_specs = ([tok(d)] + [lspec(a) for a in consts]
                + [postab(D_ROPE // 2), postab(D_ROPE // 2)]
                + [lspec(a) for a in consts2] + [posrow, posrow])
    args = [x3] + consts + [p["cosT"], p["sinT"]] + consts2 + [p["cpe"], p["spe"]]
    sd = jax.ShapeDtypeStruct
    out_shape = [
        sd((b, A_HEADS * HEAD_PAD, l), BF16), sd((b, A_HEADS, l, HEAD_PAD), BF16),
        sd((b, nck, vrows, blk), BF16), sd((b, IDX_HEADS * IDX_DIM, l), BF16),
        sd((b, l, IDX_DIM), BF16), sd((b, IDX_HEADS, l), F32),
        sd((b, l, LANE), F32), sd((b, l, LANE), F32), sd((b, l, GROUP_WIDTH), BF16),
        sd((b, l, LANE), F32), sd((b, l, GROUP_WIDTH), F32),
        sd((b, l, C_CHANNELS), F32),
        sd((b, D_HEADS * HEAD_PAD, l), BF16), sd((b, D_HEADS, l, HEAD_PAD), BF16),
        sd((b, nck, vrows, blk), BF16),
        sd((b, l // tm, A_HEADS + D_HEADS, LANE), F32),
    ]
    out_specs = [
        tokT(A_HEADS * HEAD_PAD), headk, chunkT, tokT(IDX_HEADS * IDX_DIM),
        tok(IDX_DIM), tokT(IDX_HEADS),
        tok(LANE), tok(LANE), tok(GROUP_WIDTH), tok(LANE), tok(GROUP_WIDTH),
        tok(C_CHANNELS),
        tokT(D_HEADS * HEAD_PAD), headk, chunkT,
        pl.BlockSpec((1, 1, A_HEADS + D_HEADS, LANE), lambda bi, i: (bi, i, 0, 0)),
    ]
    return pl.pallas_call(
        functools.partial(_mix_in_body, tm, blk),
        grid=grid, in_specs=in_specs, out_specs=out_specs, out_shape=out_shape,
        compiler_params=_cparams(("parallel", "parallel")),
        name="mix_in",
    )(*args)


def _attn_body(nh, dv, blk, topk, is_dsa, *refs):
    if is_dsa:
        (kb_ref, qT_ref, k_ref, vT_ref, iqT_ref, wT_ref, ik_ref, bt_ref,
         o_ref, m_scr, acc_scr, key_scr, run_scr, plane_scr, active_scr) = refs
    else:
        kb_ref, qT_ref, k_ref, vT_ref, o_ref, m_scr, acc_scr = refs
    vr = dv + ONES_ROWS
    i = pl.program_id(1)
    t = blk
    row = lax.broadcasted_iota(I32, (t, t), 0)
    col = lax.broadcasted_iota(I32, (t, t), 1)
    causal_pen = jnp.where(row <= col, 0.0, NEG)

    m_scr[...] = jnp.full(m_scr.shape, NEG, F32)
    acc_scr[...] = jnp.zeros(acc_scr.shape, F32)

    if is_dsa:
        @pl.when(i == 0)
        def _():
            plane_scr[...] = jnp.zeros(plane_scr.shape, I32)

        def score_chunk(j, diag):
            r0 = pl.multiple_of(j * t, t)
            ikc = ik_ref[0, pl.ds(r0, t), :]
            s = jnp.zeros((t, t), F32)
            for h in range(IDX_HEADS):
                d = _dot(ikc, iqT_ref[0, IDX_DIM * h:IDX_DIM * (h + 1), :])
                s = s + jnp.maximum(d, 0.0) * wT_ref[0, h:h + 1, :]
            bits = lax.bitcast_convert_type(s, I32)
            key = jnp.where(bits < 0, bits ^ 0x7FFFFFFF, bits)
            if diag:
                key = jnp.where(row <= col, key, INT_MIN)
            key_scr[pl.ds(r0, t), :] = key
            u = key ^ INT_MIN
            w = [u[SUBLANES * r:SUBLANES * (r + 1), :] for r in range(32)]
            step, mask = 16, 0x0000FFFF
            while step:
                for lo in range(32):
                    if lo & step == 0:
                        hi = lo + step
                        swap = (w[lo] ^ jnp.right_shift(w[hi], step)) & mask
                        w[lo] = w[lo] ^ swap
                        w[hi] = w[hi] ^ jnp.left_shift(swap, step)
                step //= 2
                mask ^= (mask << step) & 0xFFFFFFFF
            c0 = pl.multiple_of(j * SUBLANES, SUBLANES)
            for p in range(32):
                plane_scr[p, pl.ds(c0, SUBLANES), :] = w[p]

        def score_pair(u, carry):
            score_chunk(2 * u, False)
            score_chunk(2 * u + 1, False)
            return carry

        lax.fori_loop(0, i // 2, score_pair, 0)

        @pl.when(i % 2 == 1)
        def _():
            score_chunk(i - 1, False)
            score_chunk(i, True)

        @pl.when(i % 2 == 0)
        def _():
            score_chunk(i, True)

        nrow = plane_scr.shape[1]
        in_range = lax.broadcasted_iota(I32, (nrow, t), 0) < (i + 1) * SUBLANES

        def col_count(words):
            pc = lax.population_count(words).reshape(nrow // SUBLANES, SUBLANES, t)
            return jnp.sum(jnp.sum(pc, axis=0), axis=0, keepdims=True)

        def decide(plane, active, n_gt, ans_u):
            ones = col_count(active & plane_scr[plane])
            take = n_gt + ones >= topk
            bit = lax.shift_right_logical(jnp.int32(INT_MIN), jnp.int32(plane))
            return (jnp.where(take, 0, -1), n_gt + jnp.where(take, 0, ones), ans_u | jnp.where(take, bit, 0))

        active_scr[...] = jnp.where(in_range, -1, 0)
        state = decide(0, active_scr[...], jnp.zeros((1, t), I32), jnp.zeros((1, t), I32))

        def plane_body(plane, state):
            flip, n_gt, ans_u = state
            active = active_scr[...] & (plane_scr[plane - 1] ^ flip)
            active_scr[...] = active
            return decide(plane, active, n_gt, ans_u)

        _, n_gt, ans_u = lax.fori_loop(1, 32, plane_body, state)
        ans = ans_u ^ INT_MIN
        need = (topk - n_gt).astype(F32)
        run_scr[...] = jnp.zeros(run_scr.shape, F32)
        stri = jnp.where(col < row, 1.0, 0.0).astype(BF16)

    def logits(h, r0):
        return _dot(k_ref[0, h, pl.ds(r0, t), :], qT_ref[0, HEAD_PAD * h:HEAD_PAD * (h + 1), :])

    def selection_pens(r0s):
        pens = []
        if is_dsa:
            for r0 in r0s:
                kc = key_scr[pl.ds(r0, t), :]
                eq = kc == ans
                eqf = jnp.where(eq, 1.0, 0.0)
                run = run_scr[0:1, :]
                rank = _dot(stri, eqf.astype(BF16)) + run
                run_scr[0:1, :] = run + jnp.sum(eqf, axis=0, keepdims=True)
                pens.append(jnp.where(kc > ans, 0.0, jnp.where(eq, jnp.where(rank < need, 0.0, NEG), NEG)))
        return pens

    b_idx = pl.program_id(0)
    bounds = []
    for h in range(nh):
        qh = qT_ref[0, HEAD_PAD * h:HEAD_PAD * (h + 1), :].astype(F32)
        bound = jnp.sqrt(jnp.sum(qh * qh, axis=0, keepdims=True)) * (kb_ref[b_idx * nh + h] * BOUND_SLACK)
        if is_dsa:
            bound = bound + kb_ref[kb_ref.shape[0] - 1]
        bounds.append(bound)
    bound_max = jnp.max(functools.reduce(jnp.maximum, bounds))
    fixed_ref_ok = bound_max <= FIXED_REF_MAX

    def fixed_ref_chunks(js, kinds):
        n = len(js)
        r0s = [pl.multiple_of(j * t, t) for j in js]
        lgs = [[logits(h, r0s[c]) for c in range(n)] for h in range(nh)]
        pens = selection_pens(r0s)
        ps = []
        for h in range(nh):
            row_ps = []
            if is_dsa:
                far_ref = bounds[h] - bt_ref[2, h, 0:1, :]
            for c in range(n):
                x, ref = lgs[h][c], bounds[h]
                if is_dsa and kinds[c] == "far":
                    x, ref = x + pens[c], far_ref
                elif is_dsa:
                    x = x + (pens[c] + bt_ref[0 if kinds[c] == "diag" else 1, h])
                elif kinds[c] == "diag":
                    x = x + causal_pen
                row_ps.append(jnp.exp2(x - ref).astype(BF16))
            ps.append(row_ps[0] if n == 1 else jnp.concatenate(row_ps, axis=0))
        for h in range(nh):
            vs = slice(vr * h, vr * (h + 1))
            vt = [vT_ref[0, js[c], vs, :] for c in range(n)]
            acc_scr[vs, :] += _dot(vt[0] if n == 1 else jnp.concatenate(vt, axis=1), ps[h])

    def super_chunk(js, kinds):
        n = len(js)
        r0s = [pl.multiple_of(j * t, t) for j in js]
        lgs = [[logits(h, r0s[c]) for c in range(n)] for h in range(nh)]
        pens = selection_pens(r0s)
        ps, alphas = [], []
        for h in range(nh):
            xs = []
            for c, j in enumerate(js):
                diag = kinds[c] == "diag"
                lg = lgs[h][c]
                if is_dsa:
                    tile = 0 if diag else jnp.minimum(i - j, 2)
                    lg = lg + (pens[c] + bt_ref[tile, h])
                elif diag:
                    lg = lg + causal_pen
                xs.append(lg)
            m_old = m_scr[h, 0:1, :]
            m_new = m_old
            for x in xs:
                m_new = jnp.maximum(m_new, jnp.max(x, axis=0, keepdims=True))
            alpha = jnp.exp2(m_old - m_new)
            m_scr[h, 0:1, :] = m_new
            ps.append([jnp.exp2((x - m_new).astype(BF16)) for x in xs])
            alphas.append(alpha)
        for h in range(nh):
            vs = slice(vr * h, vr * (h + 1))
            pv = _dot(vT_ref[0, js[0], vs, :], ps[h][0])
            for c in range(1, n):
                pv = pv + _dot(vT_ref[0, js[c], vs, :], ps[h][c])
            acc_scr[vs, :] = alphas[h] * acc_scr[vs, :] + pv

    tail = ["near", "diag"] if is_dsa else ["diag"]
    nfar = jnp.maximum(i + 1 - len(tail), 0)

    def run_chunks(step):
        def group_body(u, carry):
            step([ATT_GROUP * u + c for c in range(ATT_GROUP)], ["far"] * ATT_GROUP)
            return carry

        lax.fori_loop(0, nfar // ATT_GROUP, group_body, 0)
        for rem in range(ATT_GROUP):
            @pl.when(jnp.logical_and(i + 1 >= len(tail), nfar % ATT_GROUP == rem))
            def _():
                first = i + 1 - len(tail) - rem
                step([first + c for c in range(rem + len(tail))], ["far"] * rem + tail)

        if is_dsa:
            @pl.when(i == 0)
            def _():
                step([i], ["diag"])

    pl.when(fixed_ref_ok)(functools.partial(run_chunks, fixed_ref_chunks))
    pl.when(jnp.logical_not(fixed_ref_ok))(functools.partial(run_chunks, super_chunk))

    outs = []
    for h in range(nh):
        outs.append(acc_scr[vr * h:vr * h + dv, :] / acc_scr[vr * h + dv:vr * h + dv + 1, :])
    o_ref[0] = jnp.transpose(jnp.concatenate(outs, axis=0)).astype(o_ref.dtype)


def _attention(qT, k, vT, kbound, dsa=None, blk=ATT_BLK):
    b, nh, l, _ = k.shape
    vr = vT.shape[2] // nh
    dv = vr - ONES_ROWS
    grid = (b, l // blk)
    qspec = lambda r: pl.BlockSpec((1, r, blk), lambda bi, i: (bi, 0, i))
    kspec = pl.BlockSpec((1, nh, l, HEAD_PAD), lambda bi, i: (bi, 0, 0, 0))
    vspec = pl.BlockSpec((1, l // blk, nh * vr, blk), lambda bi, i: (bi, 0, 0, 0))
    in_specs = [pl.BlockSpec(memory_space=pltpu.SMEM), qspec(nh * HEAD_PAD), kspec, vspec]
    args = [kbound, qT, k, vT]
    scratch = [pltpu.VMEM((nh, 8, blk), F32), pltpu.VMEM((nh * vr, blk), F32)]
    topk = 0
    if dsa is not None:
        iqT, wT, ik, bt = dsa
        topk = min(TOPK_MAX, l // 4)
        in_specs += [qspec(IDX_HEADS * IDX_DIM), qspec(IDX_HEADS),
                     pl.BlockSpec((1, l, IDX_DIM), lambda bi, i: (bi, 0, 0)),
                     _const_spec(bt.shape)]
        args += [iqT, wT, ik, bt]
        assert blk == 32 * SUBLANES
        nrow = (l // blk) * SUBLANES
        scratch += [pltpu.VMEM((l, blk), I32), pltpu.VMEM((8, blk), F32),
                    pltpu.VMEM((32, nrow, blk), I32), pltpu.VMEM((nrow, blk), I32)]
    return pl.pallas_call(
        functools.partial(_attn_body, nh, dv, blk, topk, dsa is not None),
        grid=grid, in_specs=in_specs,
        out_specs=pl.BlockSpec((1, blk, nh * dv), lambda bi, i: (bi, i, 0)),
        out_shape=jax.ShapeDtypeStruct((b, l, nh * dv), BF16),
        scratch_shapes=scratch,
        compiler_params=_cparams(("parallel", "arbitrary")),
        name="dsa_attn" if dsa is not None else "mla_attn",
    )(*args)


def _gla_body(tg, q_ref, k_ref, v_ref, la_ref, r_ref, go_ref, bd_ref, o_ref, st_scr, o_scr):
    @pl.when(pl.program_id(1) == 0)
    def _():
        st_scr[...] = jnp.zeros(st_scr.shape, F32)

    cs = B_CHUNK
    la = la_ref[0]
    rl = lax.broadcasted_iota(I32, (tg, LANE), 0) & (cs - 1)
    b = la
    s = 1
    while s < cs:
        b = b + jnp.where(rl >= s, pltpu.roll(b, s, 0), 0.0)
        s *= 2
    q = q_ref[0]
    k = k_ref[0]
    qb = q * jnp.exp(b)
    ci = lax.broadcasted_iota(I32, (cs, cs), 0)
    cj = lax.broadcasted_iota(I32, (cs, cs), 1)
    for c in range(tg // cs):
        sl = slice(c * cs, (c + 1) * cs)
        bc = b[sl]
        mid = bc[cs // 2:cs // 2 + 1]
        last = bc[cs - 1:cs]
        qe = (q[sl] * jnp.exp(bc - mid)).astype(BF16)
        ke = (k[sl] * jnp.exp(mid - bc)).astype(BF16)
        kd = (k[sl] * jnp.exp(last - bc)).astype(BF16)
        qbc = qb[sl].astype(BF16)
        dl = jnp.exp(last)
        vc = v_ref[0, sl, :]
        for h in range(B_HEADS):
            ks = slice(B_KEY_DIM * h, B_KEY_DIM * (h + 1))
            vs = slice(B_VAL_DIM * h, B_VAL_DIM * (h + 1))
            a = jnp.where(cj <= ci, _dot_nt(qe[:, ks], ke[:, ks]), 0.0)
            st = st_scr[h]
            o_scr[sl, vs] = _dot(a.astype(BF16), vc[:, vs]) + _dot_nt(qbc[:, ks], st.astype(BF16))
            st_scr[h] = st * dl[:, ks] + _dot_tn(vc[:, vs], kd[:, ks])
    o = o_scr[...]
    ms = _group_sum(o * o, bd_ref[...]) * (1.0 / B_VAL_DIM)
    r = r_ref[0]
    o_ref[0] = (o * lax.rsqrt(ms + EPS) * go_ref[...] * (r * jax.nn.sigmoid(r))).astype(o_ref.dtype)


def _gla(bq, bk, bv, bla, br, go, bd, layer, tg=512):
    b, l, _ = bq.shape
    tg = min(tg, l)
    tok = lambda w: pl.BlockSpec((1, tg, w), lambda bi, i: (bi, i, 0))
    return pl.pallas_call(
        functools.partial(_gla_body, tg),
        grid=(b, l // tg),
        in_specs=[tok(LANE), tok(LANE), tok(GROUP_WIDTH), tok(LANE), tok(GROUP_WIDTH),
                  _const_spec(go.shape, layer), _const_spec(bd.shape)],
        out_specs=tok(GROUP_WIDTH),
        out_shape=jax.ShapeDtypeStruct((b, l, GROUP_WIDTH), BF16),
        scratch_shapes=[pltpu.VMEM((B_HEADS, B_VAL_DIM, B_KEY_DIM), F32),
                        pltpu.VMEM((tg, GROUP_WIDTH), F32)],
        compiler_params=_cparams(("parallel", "arbitrary")),
        name="gla",
    )(bq, bk, bv, bla, br, go, bd)


CONV_HIST = 32


def _conv_body(tc, h_ref, w_ref, b_ref, g_ref, o_ref, buf):
    @pl.when(pl.program_id(1) == 0)
    def _():
        buf[0:CONV_HIST, :] = jnp.zeros((CONV_HIST, C_CHANNELS), F32)

    @pl.when(pl.program_id(1) > 0)
    def _():
        buf[0:CONV_HIST, :] = buf[tc:tc + CONV_HIST, :]

    buf[CONV_HIST:CONV_HIST + tc, :] = h_ref[0]
    acc = jnp.zeros((tc, C_CHANNELS), F32) + b_ref[...]
    base = CONV_HIST - (C_KERNEL - 1)
    hb = buf[...]
    rows = tc + CONV_HIST
    for r in range(SUBLANES):
        shifted = hb if r == 0 else pltpu.roll(hb, rows - r, 0)
        for j in range(C_KERNEL):
            if (base + j) % SUBLANES == r:
                a0 = base + j - r
                acc = acc + shifted[a0:a0 + tc, :] * w_ref[j:j + 1, :]
    ms = jnp.mean(acc * acc, axis=-1, keepdims=True)
    y = acc * lax.rsqrt(ms + EPS) * g_ref[...]
    o_ref[0] = (y * jax.nn.sigmoid(y)).astype(o_ref.dtype)


def _conv(ch, w, bias, g, layer, tc=512):
    b, l, c = ch.shape
    tc = min(tc, l)
    tok = pl.BlockSpec((1, tc, c), lambda bi, i: (bi, i, 0))
    return pl.pallas_call(
        functools.partial(_conv_body, tc),
        grid=(b, l // tc),
        in_specs=[tok] + [_const_spec(a.shape, layer) for a in (w, bias, g)],
        out_specs=tok,
        out_shape=jax.ShapeDtypeStruct((b, l, c), BF16),
        scratch_shapes=[pltpu.VMEM((tc + CONV_HIST, c), F32)],
        compiler_params=_cparams(("parallel", "arbitrary")),
        name="conv",
    )(ch, w, bias, g)


def _t5_bucket(dist):
    max_exact = REL_BUCKETS // 2
    d = jnp.maximum(dist, 0)
    df = jnp.maximum(d, 1).astype(F32)
    large = max_exact + (jnp.log(df / max_exact) / math.log(REL_MAX_DIST / max_exact)
                         * (REL_BUCKETS - max_exact)).astype(I32)
    large = jnp.minimum(large, REL_BUCKETS - 1)
    return jnp.where(d < max_exact, d, large)


def _pad_cols(w, width):
    return jnp.pad(w, ((0, 0), (0, width - w.shape[1])))


def _lane_rep(v):
    return jnp.broadcast_to(v[:, None], (v.shape[0], LANE))


def _pad_heads_rows(w, heads, dim):
    w = w.reshape(heads, dim, w.shape[1])
    return jnp.pad(w, ((0, 0), (0, HEAD_PAD - dim), (0, 0))).reshape(heads * HEAD_PAD, -1)


def _split_w_in(w_in):
    widths = (256, 256, 256, 256, 32, 8, 128, 128, 256, 16, 256, 512, 256, 128, 32)
    offs = np.cumsum((0,) + widths)
    return [w_in[:, offs[n]:offs[n + 1]] for n in range(len(widths))]


def _row(v):
    return v[None, :].astype(F32)


def _one_layer_params(w):
    (aq, ak, av, iq, ik, iw, bq, bk, bv, bg, br, cu, dcq, dckv, dkpe) = _split_w_in(w["w_in"])
    wn = jnp.concatenate([ak, _pad_cols(ik, LANE), bq, bk, bv, _pad_cols(bg, LANE), br, cu, dcq, dckv,
                          _pad_cols(dkpe, LANE)], axis=1).astype(BF16)
    wt = jnp.concatenate([_pad_heads_rows(aq.T, A_HEADS, A_HEAD_DIM), av.T, iq.T,
                          jnp.pad(iw.T, ((0, T_TOT - T_IW - IDX_HEADS), (0, 0)))], axis=0).astype(BF16)
    pad_to = lambda v, n: jnp.pad(v, (0, n - v.shape[0]))
    ukv = w["d_ukv"].reshape(D_KV_RANK, D_HEADS, D_NOPE + D_V)
    wuk = ukv[:, :, :D_NOPE].reshape(D_KV_RANK, D_HEADS * D_NOPE)
    wuv = ukv[:, :, D_NOPE:].reshape(D_KV_RANK, D_HEADS * D_V)
    gdk = w["d_k_norm"]
    return dict(
        gmix=_row(w["mix_norm"]), wn=wn, wt=wt,
        gaq=_lane_rep(pad_to(w["a_q_norm"], HEAD_PAD)),
        gak=_row(jnp.tile(w["a_k_norm"], A_HEADS)),
        wgu=jnp.pad(w["b_gate_up"], ((0, LANE - B_GATE_RANK), (0, 0))).astype(BF16),
        bgb=_row(w["b_gate_bias"]),
        gqa=_row(w["d_qa_norm"]),
        wuq=_pad_heads_rows(w["d_uq"].T, D_HEADS, D_QK).astype(BF16),
        gdq=_lane_rep(pad_to(w["d_q_norm"], HEAD_PAD)),
        gkva=_row(w["d_kva_norm"]), wuk=wuk.astype(BF16), wuvT=wuv.T.astype(BF16),
        gdk=_row(jnp.tile(gdk[:D_NOPE], D_HEADS)), gdkpe=_row(pad_to(gdk[D_NOPE:], LANE)),
        gbo=_row(jnp.tile(w["b_out_norm"], B_HEADS)),
        cw=jnp.pad(w["c_dw_w"][:, 0, :], ((0, CONV_HIST - C_KERNEL), (0, 0))).astype(F32),
        cb=_row(w["c_dw_b"]), cg=_row(w["c_norm"]),
    )


def _shared_tables(seq):
    hid = np.arange(GROUP_WIDTH) // 64
    bd = jnp.asarray(hid[:, None] == hid[None, :], dtype=BF16)
    half = D_ROPE // 2
    freqs = ROPE_THETA ** (-jnp.arange(half, dtype=F32) / half)
    ang = jnp.arange(seq).astype(F32)[:, None] * freqs[None, :]
    cos, sin = jnp.cos(ang), jnp.sin(ang)
    zeros = jnp.zeros((seq, LANE - D_ROPE), F32)
    cpe = jnp.concatenate([cos, cos, zeros], axis=1)
    spe = jnp.concatenate([-sin, sin, zeros], axis=1)
    return dict(bd=bd, cosT=cos.T, sinT=sin.T, cpe=cpe, spe=spe)


def _bias_tiles(rel_bias, blk):
    assert REL_MAX_DIST <= blk + 1
    kk = jnp.arange(blk)[:, None]
    qq = jnp.arange(blk)[None, :]
    rb = rel_bias.astype(F32).T

    def lookup(bucket):
        onehot = bucket[None, :, :, None] == jnp.arange(REL_BUCKETS)
        return jnp.sum(jnp.where(onehot, rb[:, None, None, :], 0.0), axis=-1) * LOG2E

    d0 = jnp.where(kk <= qq, lookup(_t5_bucket(qq - kk)), NEG)
    d1 = lookup(_t5_bucket(blk + qq - kk))
    far = jnp.broadcast_to(lookup(_t5_bucket(jnp.full((1, 1), 2 * blk, I32))), d1.shape)
    return jnp.stack([d0, d1, far])


def _key_bounds(knorm, bias_max):
    km = jnp.sqrt(jnp.max(knorm[..., 0], axis=1))
    kb_a = jnp.concatenate([km[:, :A_HEADS].reshape(-1), bias_max.reshape(1)])
    return kb_a, km[:, A_HEADS:].reshape(-1)


def kernel(x, ffn1_norm, ffn1_gate, ffn1_up, ffn1_down, mix_norm, w_in, a_q_norm, a_k_norm, rel_bias,
           b_gate_up, b_gate_bias, b_out_norm, c_dw_w, c_dw_b, c_norm, d_qa_norm, d_uq, d_kva_norm,
           d_ukv, d_q_norm, d_k_norm, w_out, ffn2_norm, ffn2_gate, ffn2_up, ffn2_down):
    w = dict(mix_norm=mix_norm, w_in=w_in, a_q_norm=a_q_norm, a_k_norm=a_k_norm, b_gate_up=b_gate_up,
             b_gate_bias=b_gate_bias, b_out_norm=b_out_norm, c_dw_w=c_dw_w, c_dw_b=c_dw_b, c_norm=c_norm,
             d_qa_norm=d_qa_norm, d_uq=d_uq, d_kva_norm=d_kva_norm, d_ukv=d_ukv,
             d_q_norm=d_q_norm, d_k_norm=d_k_norm)
    bsz, seq, dm = x.shape
    depth = w_in.shape[0]
    blk = min(ATT_BLK, seq)
    bt = _bias_tiles(rel_bias, blk)
    bias_max = jnp.max(bt)
    p = {**jax.vmap(_one_layer_params)(w), **_shared_tables(seq)}
    stacked_row = lambda v: v[:, None, :].astype(F32)
    ffn1 = (stacked_row(ffn1_norm), ffn1_gate, ffn1_up, ffn1_down)
    ffn2 = (stacked_row(ffn2_norm), ffn2_gate, ffn2_up, ffn2_down)
    wo = w_out.astype(BF16)
    x2 = x.reshape(bsz * seq, dm)
    for l in range(depth):
        x2 = _ffn(x2, *ffn1, l)
        (aqT, akh, avT, iqT, aik, iwT, bq, bk, bv, bla, br, ch, dqT, dkh, dvT, knorm) = _mix_in(
            x2.reshape(bsz, seq, dm), p, l, blk=blk)
        kb_a, kb_d = _key_bounds(knorm, bias_max)
        y_a = _attention(aqT, akh, avT, kb_a, dsa=(iqT, iwT, aik, bt), blk=blk)
        y_b = _gla(bq, bk, bv, bla, br, p["gbo"], p["bd"], l)
        y_c = _conv(ch, p["cw"], p["cb"], p["cg"], l)
        y_d = _attention(dqT, dkh, dvT, kb_d, blk=blk)
        ys = [y.reshape(bsz * seq, GROUP_WIDTH) for y in (y_a, y_b, y_c, y_d)]
        x2 = _ffn(x2, *ffn2, l, mix=(ys, wo))
    return x2.reshape(bsz, seq, dm)
```

```python
import functools
import math

import jax
import jax.numpy as jnp
import numpy as np
from jax import lax
from jax.experimental import pallas as pl
from jax.experimental.pallas import tpu as pltpu

F32 = jnp.float32
BF16 = jnp.bfloat16
I32 = jnp.int32
I16 = jnp.int16

EPS = 1e-6
GROUP_WIDTH = 256
A_HEADS, A_HEAD_DIM = 4, 64
IDX_HEADS, IDX_DIM = 8, 32
TOPK_MAX = 256
REL_BUCKETS, REL_MAX_DIST = 32, 128
B_HEADS, B_KEY_DIM, B_VAL_DIM, B_GATE_RANK = 4, 32, 64, 16
B_GATE_TAU = 16.0
B_CHUNK = 64
C_CHANNELS, C_KERNEL = 256, 31
D_HEADS, D_Q_RANK, D_KV_RANK, D_NOPE, D_ROPE, D_V = 4, 256, 128, 64, 32, 64
D_QK = D_NOPE + D_ROPE
ROPE_THETA = 10000.0

LANE = 128
SUBLANES = 8
HEAD_PAD = 128
ONES_ROWS = 16
ATT_BLK = 256
ATT_GROUP = 4
BOUND_SLACK = 1.01
FIXED_REF_MAX = 40.0
INT_MIN = -2 ** 31
I16_MIN = -2 ** 15
NEG = -1e30
LOG2E = math.log2(math.e)
VMEM_LIMIT = 56 * 1024 * 1024

N_AK, N_IK, N_BQ, N_BK, N_BV, N_BG, N_BR, N_CU, N_DCQ, N_DCKV, N_DKPE, N_TOT = (
    0, 256, 384, 512, 640, 896, 1024, 1280, 1792, 2048, 2176, 2304)
T_AQ, T_AV, T_IQ, T_IW, T_TOT = 0, 512, 768, 1024, 1040


def _dot(a, b):
    return jnp.dot(a, b, preferred_element_type=F32)


def _dot_nt(a, b):
    return lax.dot_general(a, b, (((1,), (1,)), ((), ())), preferred_element_type=F32)


def _dot_tn(a, b):
    return lax.dot_general(a, b, (((0,), (0,)), ((), ())), preferred_element_type=F32)


def _group_sum(x2, bd):
    hi = x2.astype(BF16)
    lo = (x2 - hi.astype(F32)).astype(BF16)
    return _dot(hi, bd) + _dot(lo, bd)


def _const_spec(shape, layer=None):
    nd = len(shape)
    if layer is None:
        return pl.BlockSpec(shape, lambda *_: (0,) * nd, pipeline_mode=pl.Buffered(1))
    return pl.BlockSpec((None,) + tuple(shape[1:]), lambda *_: (layer,) + (0,) * (nd - 1),
                        pipeline_mode=pl.Buffered(1))


def _cparams(sem):
    return pltpu.CompilerParams(dimension_semantics=sem, vmem_limit_bytes=VMEM_LIMIT)


def _ffn_body(has_mix, fc, *refs):
    if has_mix:
        x_ref, ya, yb, yc, yd, wo_ref, g_ref, wg_ref, wu_ref, wd_ref, o_ref, h_scr = refs
    else:
        x_ref, g_ref, wg_ref, wu_ref, wd_ref, o_ref, h_scr = refs
    x = x_ref[...]
    if has_mix:
        y = jnp.concatenate([ya[...], yb[...], yc[...], yd[...]], axis=-1)
        x = x + _dot(y, wo_ref[...])
    ms = jnp.mean(x * x, axis=-1, keepdims=True)
    xn = (x * lax.rsqrt(ms + EPS) * g_ref[...]).astype(BF16)
    d_ff = wg_ref.shape[1]
    for c in range(d_ff // fc):
        sl = slice(c * fc, (c + 1) * fc)
        gate = _dot(xn, wg_ref[:, sl].astype(BF16))
        up = _dot(xn, wu_ref[:, sl].astype(BF16))
        h_scr[:, sl] = (gate * jax.nn.sigmoid(gate) * up).astype(BF16)
    o_ref[...] = x + 0.5 * _dot(h_scr[...], wd_ref[...].astype(BF16))


def _ffn(x2, g, wg, wu, wd, layer, mix=None, tm=512, fc=256):
    m, d = x2.shape
    d_ff = wg.shape[2]
    tm = min(tm, m)
    row = lambda w: pl.BlockSpec((tm, w), lambda i: (i, 0))
    in_specs = [row(d)]
    args = [x2]
    if mix is not None:
        ys, wo = mix
        in_specs += [row(GROUP_WIDTH)] * 4 + [_const_spec(wo.shape, layer)]
        args += list(ys) + [wo]
    in_specs += [_const_spec(a.shape, layer) for a in (g, wg, wu, wd)]
    args += [g, wg, wu, wd]
    return pl.pallas_call(
        functools.partial(_ffn_body, mix is not None, fc),
        grid=(m // tm,),
        in_specs=in_specs,
        out_specs=row(d),
        out_shape=jax.ShapeDtypeStruct((m, d), F32),
        scratch_shapes=[pltpu.VMEM((tm, d_ff), BF16)],
        compiler_params=_cparams(("parallel",)),
        name="ffn_mix" if mix is not None else "ffn",
    )(*args)


def _mix_in_body(tm, blk,
                 x_ref, gmix_ref, wn_ref, wt_ref, bd_ref,
                 gaq_ref, gak_ref,
                 wgu_ref, bgb_ref,
                 gqa_ref, wuq_ref, gdq_ref, cosT_ref, sinT_ref,
                 gkva_ref, wuk_ref, wuvT_ref, gdk_ref, gdkpe_ref, cpe_ref, spe_ref,
                 aqT_ref, ak_ref, avT_ref, iqT_ref, ik_ref, iwT_ref,
                 bq_ref, bk_ref, bv_ref, bla_ref, br_ref,
                 ch_ref,
                 dqT_ref, dk_ref, dvT_ref, knorm_ref):
    nlt = tm // LANE
    x = x_ref[0]
    ms = jnp.mean(x * x, axis=-1, keepdims=True)
    xn = (x * lax.rsqrt(ms + EPS) * gmix_ref[...]).astype(BF16)
    bd = bd_ref[...]
    lane = lax.broadcasted_iota(I32, (tm, LANE), 1)

    def lanes(g):
        return jnp.tile(g, (1, nlt))

    z = _dot(xn, wn_ref[...])
    zt = _dot_nt(wt_ref[...], xn)

    def zs(off, width):
        return z[:, off:off + width]

    cq = zs(N_DCQ, D_Q_RANK)
    cq_ms = jnp.mean(cq * cq, axis=-1, keepdims=True)
    cqn = (cq * lax.rsqrt(cq_ms + EPS) * gqa_ref[...]).astype(BF16)
    ckv = zs(N_DCKV, D_KV_RANK)
    ckv_ms = jnp.mean(ckv * ckv, axis=-1, keepdims=True)
    ckvn = (ckv * lax.rsqrt(ckv_ms + EPS) * gkva_ref[...]).astype(BF16)
    dq = _dot_nt(wuq_ref[...], cqn).reshape(D_HEADS, HEAD_PAD, tm)
    kn = _dot(ckvn, wuk_ref[...])
    dv = _dot_nt(wuvT_ref[...], ckvn).astype(BF16)
    gate = _dot(zs(N_BG, LANE).astype(BF16), wgu_ref[...]) + bgb_ref[...]
    ak = zs(N_AK, GROUP_WIDTH)
    ak_ms = _group_sum(ak * ak, bd) * (1.0 / A_HEAD_DIM)
    kn_ss = _group_sum(kn * kn, bd)

    aq = zt[T_AQ:T_AQ + A_HEADS * HEAD_PAD].reshape(A_HEADS, HEAD_PAD, tm)
    aq_ms = jnp.sum(aq * aq, axis=1, keepdims=True) * (1.0 / A_HEAD_DIM)
    aq = aq * lax.rsqrt(aq_ms + EPS) * lanes(gaq_ref[...])[None] * (A_HEAD_DIM ** -0.5 * LOG2E)
    aqT_ref[0] = aq.reshape(A_HEADS * HEAD_PAD, tm).astype(BF16)
    def max_sq_norm(kh):
        f = kh.astype(F32)
        n2 = _dot((f * f).astype(BF16), jnp.ones((LANE, LANE), BF16))
        return jnp.max(n2, axis=0, keepdims=True)

    def with_ones_rows(vt, heads, dim):
        ones = jnp.ones((ONES_ROWS, tm), BF16)
        return jnp.concatenate([r for h in range(heads) for r in (vt[dim * h:dim * (h + 1)], ones)], axis=0)

    av = with_ones_rows(zt[T_AV:T_AV + GROUP_WIDTH].astype(BF16), A_HEADS, A_HEAD_DIM)
    for c in range(tm // blk):
        avT_ref[0, c] = av[:, c * blk:(c + 1) * blk]
    iqT_ref[0] = zt[T_IQ:T_IQ + IDX_HEADS * IDX_DIM].astype(BF16)
    iwT_ref[0] = zt[T_IW:T_IW + IDX_HEADS] * ((IDX_HEADS ** -0.5) * (IDX_DIM ** -0.5))

    ak = ak * lax.rsqrt(ak_ms + EPS) * gak_ref[...]
    for h in range(A_HEADS):
        pair = ak[:, LANE * (h // 2):LANE * (h // 2) + LANE]
        if h % 2 == 1:
            pair = pltpu.roll(pair, 64, 1)
        kh = jnp.where(lane < A_HEAD_DIM, pair, 0.0).astype(BF16)
        ak_ref[0, h] = kh
        knorm_ref[0, 0, h:h + 1, :] = max_sq_norm(kh)
    ik_ref[0] = zs(N_IK, LANE)[:, :IDX_DIM].astype(BF16)

    bq_ref[0] = zs(N_BQ, LANE) * (B_KEY_DIM ** -0.5)
    bk_ref[0] = zs(N_BK, LANE)
    bv_ref[0] = zs(N_BV, GROUP_WIDTH).astype(BF16)
    bla_ref[0] = (jnp.minimum(gate, 0.0) - jnp.log(1.0 + jnp.exp(-jnp.abs(gate)))) * (1.0 / B_GATE_TAU)
    br_ref[0] = zs(N_BR, GROUP_WIDTH)

    ca = zs(N_CU, C_CHANNELS)
    cg = zs(N_CU + C_CHANNELS, C_CHANNELS)
    ch_ref[0] = ca * jax.nn.sigmoid(cg)

    dq_ms = jnp.sum(dq * dq, axis=1, keepdims=True) * (1.0 / D_QK)
    dq = dq * lax.rsqrt(dq_ms + EPS) * lanes(gdq_ref[...])[None] * (D_QK ** -0.5 * LOG2E)
    half = D_ROPE // 2
    x1 = dq[:, D_NOPE:D_NOPE + half]
    x2 = dq[:, D_NOPE + half:D_QK]
    cs = cosT_ref[...][None]
    sn = sinT_ref[...][None]
    dq = jnp.concatenate([dq[:, :D_NOPE], x1 * cs - x2 * sn, x2 * cs + x1 * sn, dq[:, D_QK:]], axis=1)
    dqT_ref[0] = dq.reshape(D_HEADS * HEAD_PAD, tm).astype(BF16)

    dv = with_ones_rows(dv, D_HEADS, D_V)
    for c in range(tm // blk):
        dvT_ref[0, c] = dv[:, c * blk:(c + 1) * blk]
    kpe = zs(N_DKPE, LANE)
    ss = kn_ss + jnp.sum(kpe * kpe, axis=-1, keepdims=True)
    rinv = lax.rsqrt(ss * (1.0 / D_QK) + EPS)
    kn = kn * rinv * gdk_ref[...]
    pe = kpe * gdkpe_ref[...]
    partner = jnp.where(lane < half, pltpu.roll(pe, LANE - half, 1), pltpu.roll(pe, half, 1))
    pe = pe * cpe_ref[...] + partner * spe_ref[...]
    pe = pltpu.roll(pe, D_NOPE, 1)
    for h in range(D_HEADS):
        pair = kn[:, LANE * (h // 2):LANE * (h // 2) + LANE]
        rpair = rinv[:, LANE * (h // 2):LANE * (h // 2) + LANE]
        if h % 2 == 1:
            pair = pltpu.roll(pair, 64, 1)
        else:
            rpair = pltpu.roll(rpair, 64, 1)
        kh = jnp.where(lane < D_NOPE, pair, pe * rpair).astype(BF16)
        dk_ref[0, h] = kh
        knorm_ref[0, 0, A_HEADS + h:A_HEADS + h + 1, :] = max_sq_norm(kh)


def _mix_in(x3, p, layer, tm=512, blk=ATT_BLK):
    b, l, d = x3.shape
    tm = min(tm, l)
    grid = (b, l // tm)
    nck = l // blk
    tok = lambda w: pl.BlockSpec((1, tm, w), lambda bi, i: (bi, i, 0))
    tokT = lambda r: pl.BlockSpec((1, r, tm), lambda bi, i: (bi, 0, i))
    headk = pl.BlockSpec((1, 4, tm, HEAD_PAD), lambda bi, i: (bi, 0, i, 0))
    vrows = GROUP_WIDTH + 4 * ONES_ROWS
    chunkT = pl.BlockSpec((1, tm // blk, vrows, blk), lambda bi, i: (bi, i, 0, 0))
    postab = lambda r: pl.BlockSpec((r, tm), lambda bi, i: (0, i))
    posrow = pl.BlockSpec((tm, LANE), lambda bi, i: (i, 0))
    consts = [p["gmix"], p["wn"], p["wt"], p["bd"], p["gaq"], p["gak"], p["wgu"], p["bgb"],
              p["gqa"], p["wuq"], p["gdq"]]
    consts2 = [p["gkva"], p["wuk"], p["wuvT"], p["gdk"], p["gdkpe"]]
    lspec = lambda a: _const_spec(a.shape) if a is p["bd"] else _const_spec(a.shape, layer)
    in_specs = ([tok(d)] + [lspec(a) for a in consts]
                + [postab(D_ROPE // 2), postab(D_ROPE // 2)]
                + [lspec(a) for a in consts2] + [posrow, posrow])
    args = [x3] + consts + [p["cosT"], p["sinT"]] + consts2 + [p["cpe"], p["spe"]]
    sd = jax.ShapeDtypeStruct
    out_shape = [
        sd((b, A_HEADS * HEAD_PAD, l), BF16), sd((b, A_HEADS, l, HEAD_PAD), BF16),
        sd((b, nck, vrows, blk), BF16), sd((b, IDX_HEADS * IDX_DIM, l), BF16),
        sd((b, l, IDX_DIM), BF16), sd((b, IDX_HEADS, l), F32),
        sd((b, l, LANE), F32), sd((b, l, LANE), F32), sd((b, l, GROUP_WIDTH), BF16),
        sd((b, l, LANE), F32), sd((b, l, GROUP_WIDTH), F32),
        sd((b, l, C_CHANNELS), F32),
        sd((b, D_HEADS * HEAD_PAD, l), BF16), sd((b, D_HEADS, l, HEAD_PAD), BF16),
        sd((b, nck, vrows, blk), BF16),
        sd((b, l // tm, A_HEADS + D_HEADS, LANE), F32),
    ]
    out_specs = [
        tokT(A_HEADS * HEAD_PAD), headk, chunkT, tokT(IDX_HEADS * IDX_DIM),
        tok(IDX_DIM), tokT(IDX_HEADS),
        tok(LANE), tok(LANE), tok(GROUP_WIDTH), tok(LANE), tok(GROUP_WIDTH),
        tok(C_CHANNELS),
        tokT(D_HEADS * HEAD_PAD), headk, chunkT,
        pl.BlockSpec((1, 1, A_HEADS + D_HEADS, LANE), lambda bi, i: (bi, i, 0, 0)),
    ]
    return pl.pallas_call(
        functools.partial(_mix_in_body, tm, blk),
        grid=grid, in_specs=in_specs, out_specs=out_specs, out_shape=out_shape,
        compiler_params=_cparams(("parallel", "parallel")),
        name="mix_in",
    )(*args)


def _attn_body(nh, dv, blk, topk, is_dsa, *refs):
    if is_dsa:
        (kb_ref, qT_ref, k_ref, vT_ref, iqT_ref, wT_ref, ik_ref, bt_ref,
         o_ref, m_scr, acc_scr, key_scr, run_scr, plane_scr, active_scr) = refs
    else:
        kb_ref, qT_ref, k_ref, vT_ref, o_ref, m_scr, acc_scr = refs
    vr = dv + ONES_ROWS
    i = pl.program_id(1)
    t = blk
    row = lax.broadcasted_iota(I32, (t, t), 0)
    col = lax.broadcasted_iota(I32, (t, t), 1)
    causal_pen = jnp.where(row <= col, 0.0, NEG)

    m_scr[...] = jnp.full(m_scr.shape, NEG, F32)
    acc_scr[...] = jnp.zeros(acc_scr.shape, F32)

    if is_dsa:
        @pl.when(i == 0)
        def _():
            plane_scr[...] = jnp.zeros(plane_scr.shape, I32)

        def score_chunk(j, diag):
            r0 = pl.multiple_of(j * t, t)
            ikc = ik_ref[0, pl.ds(r0, t), :]
            s = jnp.zeros((t, t), F32)
            for h in range(IDX_HEADS):
                d = _dot(ikc, iqT_ref[0, IDX_DIM * h:IDX_DIM * (h + 1), :])
                s = s + jnp.maximum(d, 0.0) * wT_ref[0, h:h + 1, :]
            bits = lax.bitcast_convert_type(s, I32)
            key = jnp.where(bits < 0, bits ^ 0x7FFFFFFF, bits)
            if diag:
                key = jnp.where(row <= col, key, INT_MIN)
            key_scr[pl.ds(r0, t), :] = key
            u = key ^ INT_MIN
            w = [u[SUBLANES * r:SUBLANES * (r + 1), :] for r in range(32)]
            step, mask = 16, 0x0000FFFF
            while step:
                for lo in range(32):
                    if lo & step == 0:
                        hi = lo + step
                        swap = (w[lo] ^ jnp.right_shift(w[hi], step)) & mask
                        w[lo] = w[lo] ^ swap
                        w[hi] = w[hi] ^ jnp.left_shift(swap, step)
                step //= 2
                mask ^= (mask << step) & 0xFFFFFFFF
            c0 = pl.multiple_of(j * SUBLANES, SUBLANES)
            for p in range(32):
                plane_scr[p, pl.ds(c0, SUBLANES), :] = w[p]

        def score_pair(u, carry):
            score_chunk(2 * u, False)
            score_chunk(2 * u + 1, False)
            return carry

        lax.fori_loop(0, i // 2, score_pair, 0)

        @pl.when(i % 2 == 1)
        def _():
            score_chunk(i - 1, False)
            score_chunk(i, True)

        @pl.when(i % 2 == 0)
        def _():
            score_chunk(i, True)

        nrow = plane_scr.shape[1]
        in_range = lax.broadcasted_iota(I32, (nrow, t), 0) < (i + 1) * SUBLANES

        def col_count(words):
            pc = lax.population_count(words).reshape(nrow // SUBLANES, SUBLANES, t)
            return jnp.sum(jnp.sum(pc, axis=0), axis=0, keepdims=True)

        def decide(plane, active, n_gt, ans_u):
            ones = col_count(active & plane_scr[plane])
            take = n_gt + ones >= topk
            bit = lax.shift_right_logical(jnp.int32(INT_MIN), jnp.int32(plane))
            return (jnp.where(take, 0, -1), n_gt + jnp.where(take, 0, ones), ans_u | jnp.where(take, bit, 0))

        active_scr[...] = jnp.where(in_range, -1, 0)
        state = decide(0, active_scr[...], jnp.zeros((1, t), I32), jnp.zeros((1, t), I32))

        def plane_body(plane, state):
            flip, n_gt, ans_u = state
            active = active_scr[...] & (plane_scr[plane - 1] ^ flip)
            active_scr[...] = active
            return decide(plane, active, n_gt, ans_u)

        flip, n_gt, ans_u = lax.fori_loop(1, 32, plane_body, state)
        ans = ans_u ^ INT_MIN
        need = (topk - n_gt).astype(F32)
        n_eq = col_count(active_scr[...] & (plane_scr[31] ^ flip))
        take_all = jnp.logical_or(topk - n_gt >= n_eq, ans == INT_MIN)
        plain_selection = jnp.min(jnp.where(take_all, 1, 0)) == 1
        run_scr[...] = jnp.zeros(run_scr.shape, F32)
        stri = jnp.where(col < row, 1.0, 0.0).astype(BF16)

    def logits(h, r0):
        return _dot(k_ref[0, h, pl.ds(r0, t), :], qT_ref[0, HEAD_PAD * h:HEAD_PAD * (h + 1), :])

    def selection_pens(r0s):
        pens = []
        if is_dsa:
            for r0 in r0s:
                kc = key_scr[pl.ds(r0, t), :]
                eq = kc == ans
                eqf = jnp.where(eq, 1.0, 0.0)
                run = run_scr[0:1, :]
                rank = _dot(stri, eqf.astype(BF16)) + run
                run_scr[0:1, :] = run + jnp.sum(eqf, axis=0, keepdims=True)
                pens.append(jnp.where(kc > ans, 0.0, jnp.where(eq, jnp.where(rank < need, 0.0, NEG), NEG)))
        return pens

    b_idx = pl.program_id(0)
    bounds = []
    for h in range(nh):
        qh = qT_ref[0, HEAD_PAD * h:HEAD_PAD * (h + 1), :].astype(F32)
        bound = jnp.sqrt(jnp.sum(qh * qh, axis=0, keepdims=True)) * (kb_ref[b_idx * nh + h] * BOUND_SLACK)
        if is_dsa:
            bound = bound + kb_ref[kb_ref.shape[0] - 1]
        bounds.append(bound)
    bound_max = jnp.max(functools.reduce(jnp.maximum, bounds))
    fixed_ref_ok = bound_max <= FIXED_REF_MAX
    if is_dsa:
        fixed_ref_ok = jnp.logical_and(fixed_ref_ok, plain_selection)

    def fixed_ref_chunks(js, kinds):
        n = len(js)
        r0s = [pl.multiple_of(j * t, t) for j in js]
        lgs = [[logits(h, r0s[c]) for c in range(n)] for h in range(nh)]
        pens = []
        if is_dsa:
            pens = [jnp.where(key_scr[pl.ds(r0, t), :] >= ans, 0.0, NEG) for r0 in r0s]
        ps = []
        for h in range(nh):
            row_ps = []
            if is_dsa:
                far_ref = bounds[h] - bt_ref[2, h, 0:1, :]
            for c in range(n):
                x, ref = lgs[h][c], bounds[h]
                if is_dsa and kinds[c] == "far":
                    x, ref = x + pens[c], far_ref
                elif is_dsa:
                    x = x + (pens[c] + bt_ref[0 if kinds[c] == "diag" else 1, h])
                elif kinds[c] == "diag":
                    x = x + causal_pen
                row_ps.append(jnp.exp2(x - ref).astype(BF16))
            ps.append(row_ps[0] if n == 1 else jnp.concatenate(row_ps, axis=0))
        for h in range(nh):
            vs = slice(vr * h, vr * (h + 1))
            vt = [vT_ref[0, js[c], vs, :] for c in range(n)]
            acc_scr[vs, :] += _dot(vt[0] if n == 1 else jnp.concatenate(vt, axis=1), ps[h])

    def super_chunk(js, kinds):
        n = len(js)
        r0s = [pl.multiple_of(j * t, t) for j in js]
        lgs = [[logits(h, r0s[c]) for c in range(n)] for h in range(nh)]
        pens = selection_pens(r0s)
        ps, alphas = [], []
        for h in range(nh):
            xs = []
            for c, j in enumerate(js):
                diag = kinds[c] == "diag"
                lg = lgs[h][c]
                if is_dsa:
                    tile = 0 if diag else jnp.minimum(i - j, 2)
                    lg = lg + (pens[c] + bt_ref[tile, h])
                elif diag:
                    lg = lg + causal_pen
                xs.append(lg)
            m_old = m_scr[h, 0:1, :]
            m_new = m_old
            for x in xs:
                m_new = jnp.maximum(m_new, jnp.max(x, axis=0, keepdims=True))
            alpha = jnp.exp2(m_old - m_new)
            m_scr[h, 0:1, :] = m_new
            ps.append([jnp.exp2((x - m_new).astype(BF16)) for x in xs])
            alphas.append(alpha)
        for h in range(nh):
            vs = slice(vr * h, vr * (h + 1))
            pv = _dot(vT_ref[0, js[0], vs, :], ps[h][0])
            for c in range(1, n):
                pv = pv + _dot(vT_ref[0, js[c], vs, :], ps[h][c])
            acc_scr[vs, :] = alphas[h] * acc_scr[vs, :] + pv

    tail = ["near", "diag"] if is_dsa else ["diag"]
    nfar = jnp.maximum(i + 1 - len(tail), 0)

    def run_chunks(step):
        def group_body(u, carry):
            step([ATT_GROUP * u + c for c in range(ATT_GROUP)], ["far"] * ATT_GROUP)
            return carry

        lax.fori_loop(0, nfar // ATT_GROUP, group_body, 0)
        for rem in range(ATT_GROUP):
            @pl.when(jnp.logical_and(i + 1 >= len(tail), nfar % ATT_GROUP == rem))
            def _():
                first = i + 1 - len(tail) - rem
                step([first + c for c in range(rem + len(tail))], ["far"] * rem + tail)

        if is_dsa:
            @pl.when(i == 0)
            def _():
                step([i], ["diag"])

    pl.when(fixed_ref_ok)(functools.partial(run_chunks, fixed_ref_chunks))
    pl.when(jnp.logical_not(fixed_ref_ok))(functools.partial(run_chunks, super_chunk))

    outs = []
    for h in range(nh):
        outs.append(acc_scr[vr * h:vr * h + dv, :] / acc_scr[vr * h + dv:vr * h + dv + 1, :])
    o_ref[0] = jnp.transpose(jnp.concatenate(outs, axis=0)).astype(o_ref.dtype)


def _attention(qT, k, vT, kbound, dsa=None, blk=ATT_BLK):
    b, nh, l, _ = k.shape
    vr = vT.shape[2] // nh
    dv = vr - ONES_ROWS
    grid = (b, l // blk)
    qspec = lambda r: pl.BlockSpec((1, r, blk), lambda bi, i: (bi, 0, i))
    kspec = pl.BlockSpec((1, nh, l, HEAD_PAD), lambda bi, i: (bi, 0, 0, 0))
    vspec = pl.BlockSpec((1, l // blk, nh * vr, blk), lambda bi, i: (bi, 0, 0, 0))
    in_specs = [pl.BlockSpec(memory_space=pltpu.SMEM), qspec(nh * HEAD_PAD), kspec, vspec]
    args = [kbound, qT, k, vT]
    scratch = [pltpu.VMEM((nh, 8, blk), F32), pltpu.VMEM((nh * vr, blk), F32)]
    topk = 0
    if dsa is not None:
        iqT, wT, ik, bt = dsa
        topk = min(TOPK_MAX, l // 4)
        in_specs += [qspec(IDX_HEADS * IDX_DIM), qspec(IDX_HEADS),
                     pl.BlockSpec((1, l, IDX_DIM), lambda bi, i: (bi, 0, 0)),
                     _const_spec(bt.shape)]
        args += [iqT, wT, ik, bt]
        assert blk == 32 * SUBLANES
        nrow = (l // blk) * SUBLANES
        scratch += [pltpu.VMEM((l, blk), I32), pltpu.VMEM((8, blk), F32),
                    pltpu.VMEM((32, nrow, blk), I32), pltpu.VMEM((nrow, blk), I32)]
    return pl.pallas_call(
        functools.partial(_attn_body, nh, dv, blk, topk, dsa is not None),
        grid=grid, in_specs=in_specs,
        out_specs=pl.BlockSpec((1, blk, nh * dv), lambda bi, i: (bi, i, 0)),
        out_shape=jax.ShapeDtypeStruct((b, l, nh * dv), BF16),
        scratch_shapes=scratch,
        compiler_params=_cparams(("parallel", "arbitrary")),
        name="dsa_attn" if dsa is not None else "mla_attn",
    )(*args)


def _gla_body(tg, q_ref, k_ref, v_ref, la_ref, r_ref, go_ref, bd_ref, o_ref, st_scr, o_scr):
    @pl.when(pl.program_id(1) == 0)
    def _():
        st_scr[...] = jnp.zeros(st_scr.shape, F32)

    cs = B_CHUNK
    la = la_ref[0]
    rl = lax.broadcasted_iota(I32, (tg, LANE), 0) & (cs - 1)
    b = la
    s = 1
    while s < cs:
        b = b + jnp.where(rl >= s, pltpu.roll(b, s, 0), 0.0)
        s *= 2
    q = q_ref[0]
    k = k_ref[0]
    qb = q * jnp.exp(b)
    ci = lax.broadcasted_iota(I32, (cs, cs), 0)
    cj = lax.broadcasted_iota(I32, (cs, cs), 1)
    for c in range(tg // cs):
        sl = slice(c * cs, (c + 1) * cs)
        bc = b[sl]
        mid = bc[cs // 2:cs // 2 + 1]
        last = bc[cs - 1:cs]
        qe = (q[sl] * jnp.exp(bc - mid)).astype(BF16)
        ke = (k[sl] * jnp.exp(mid - bc)).astype(BF16)
        kd = (k[sl] * jnp.exp(last - bc)).astype(BF16)
        qbc = qb[sl].astype(BF16)
        dl = jnp.exp(last)
        vc = v_ref[0, sl, :]
        for h in range(B_HEADS):
            ks = slice(B_KEY_DIM * h, B_KEY_DIM * (h + 1))
            vs = slice(B_VAL_DIM * h, B_VAL_DIM * (h + 1))
            a = jnp.where(cj <= ci, _dot_nt(qe[:, ks], ke[:, ks]), 0.0)
            st = st_scr[h]
            o_scr[sl, vs] = _dot(a.astype(BF16), vc[:, vs]) + _dot_nt(qbc[:, ks], st.astype(BF16))
            st_scr[h] = st * dl[:, ks] + _dot_tn(vc[:, vs], kd[:, ks])
    o = o_scr[...]
    ms = _group_sum(o * o, bd_ref[...]) * (1.0 / B_VAL_DIM)
    r = r_ref[0]
    o_ref[0] = (o * lax.rsqrt(ms + EPS) * go_ref[...] * (r * jax.nn.sigmoid(r))).astype(o_ref.dtype)


def _gla(bq, bk, bv, bla, br, go, bd, layer, tg=512):
    b, l, _ = bq.shape
    tg = min(tg, l)
    tok = lambda w: pl.BlockSpec((1, tg, w), lambda bi, i: (bi, i, 0))
    return pl.pallas_call(
        functools.partial(_gla_body, tg),
        grid=(b, l // tg),
        in_specs=[tok(LANE), tok(LANE), tok(GROUP_WIDTH), tok(LANE), tok(GROUP_WIDTH),
                  _const_spec(go.shape, layer), _const_spec(bd.shape)],
        out_specs=tok(GROUP_WIDTH),
        out_shape=jax.ShapeDtypeStruct((b, l, GROUP_WIDTH), BF16),
        scratch_shapes=[pltpu.VMEM((B_HEADS, B_VAL_DIM, B_KEY_DIM), F32),
                        pltpu.VMEM((tg, GROUP_WIDTH), F32)],
        compiler_params=_cparams(("parallel", "arbitrary")),
        name="gla",
    )(bq, bk, bv, bla, br, go, bd)


CONV_HIST = 32


def _conv_body(tc, h_ref, w_ref, b_ref, g_ref, o_ref, buf):
    @pl.when(pl.program_id(1) == 0)
    def _():
        buf[0:CONV_HIST, :] = jnp.zeros((CONV_HIST, C_CHANNELS), F32)

    @pl.when(pl.program_id(1) > 0)
    def _():
        buf[0:CONV_HIST, :] = buf[tc:tc + CONV_HIST, :]

    buf[CONV_HIST:CONV_HIST + tc, :] = h_ref[0]
    acc = jnp.zeros((tc, C_CHANNELS), F32) + b_ref[...]
    base = CONV_HIST - (C_KERNEL - 1)
    hb = buf[...]
    rows = tc + CONV_HIST
    for r in range(SUBLANES):
        shifted = hb if r == 0 else pltpu.roll(hb, rows - r, 0)
        for j in range(C_KERNEL):
            if (base + j) % SUBLANES == r:
                a0 = base + j - r
                acc = acc + shifted[a0:a0 + tc, :] * w_ref[j:j + 1, :]
    ms = jnp.mean(acc * acc, axis=-1, keepdims=True)
    y = acc * lax.rsqrt(ms + EPS) * g_ref[...]
    o_ref[0] = (y * jax.nn.sigmoid(y)).astype(o_ref.dtype)


def _conv(ch, w, bias, g, layer, tc=512):
    b, l, c = ch.shape
    tc = min(tc, l)
    tok = pl.BlockSpec((1, tc, c), lambda bi, i: (bi, i, 0))
    return pl.pallas_call(
        functools.partial(_conv_body, tc),
        grid=(b, l // tc),
        in_specs=[tok] + [_const_spec(a.shape, layer) for a in (w, bias, g)],
        out_specs=tok,
        out_shape=jax.ShapeDtypeStruct((b, l, c), BF16),
        scratch_shapes=[pltpu.VMEM((tc + CONV_HIST, c), F32)],
        compiler_params=_cparams(("parallel", "arbitrary")),
        name="conv",
    )(ch, w, bias, g)


def _t5_bucket(dist):
    max_exact = REL_BUCKETS // 2
    d = jnp.maximum(dist, 0)
    df = jnp.maximum(d, 1).astype(F32)
    large = max_exact + (jnp.log(df / max_exact) / math.log(REL_MAX_DIST / max_exact)
                         * (REL_BUCKETS - max_exact)).astype(I32)
    large = jnp.minimum(large, REL_BUCKETS - 1)
    return jnp.where(d < max_exact, d, large)


def _pad_cols(w, width):
    return jnp.pad(w, ((0, 0), (0, width - w.shape[1])))


def _lane_rep(v):
    return jnp.broadcast_to(v[:, None], (v.shape[0], LANE))


def _pad_heads_rows(w, heads, dim):
    w = w.reshape(heads, dim, w.shape[1])
    return jnp.pad(w, ((0, 0), (0, HEAD_PAD - dim), (0, 0))).reshape(heads * HEAD_PAD, -1)


def _split_w_in(w_in):
    widths = (256, 256, 256, 256, 32, 8, 128, 128, 256, 16, 256, 512, 256, 128, 32)
    offs = np.cumsum((0,) + widths)
    return [w_in[:, offs[n]:offs[n + 1]] for n in range(len(widths))]


def _row(v):
    return v[None, :].astype(F32)


def _one_layer_params(w):
    (aq, ak, av, iq, ik, iw, bq, bk, bv, bg, br, cu, dcq, dckv, dkpe) = _split_w_in(w["w_in"])
    wn = jnp.concatenate([ak, _pad_cols(ik, LANE), bq, bk, bv, _pad_cols(bg, LANE), br, cu, dcq, dckv,
                          _pad_cols(dkpe, LANE)], axis=1).astype(BF16)
    wt = jnp.concatenate([_pad_heads_rows(aq.T, A_HEADS, A_HEAD_DIM), av.T, iq.T,
                          jnp.pad(iw.T, ((0, T_TOT - T_IW - IDX_HEADS), (0, 0)))], axis=0).astype(BF16)
    pad_to = lambda v, n: jnp.pad(v, (0, n - v.shape[0]))
    ukv = w["d_ukv"].reshape(D_KV_RANK, D_HEADS, D_NOPE + D_V)
    wuk = ukv[:, :, :D_NOPE].reshape(D_KV_RANK, D_HEADS * D_NOPE)
    wuv = ukv[:, :, D_NOPE:].reshape(D_KV_RANK, D_HEADS * D_V)
    gdk = w["d_k_norm"]
    return dict(
        gmix=_row(w["mix_norm"]), wn=wn, wt=wt,
        gaq=_lane_rep(pad_to(w["a_q_norm"], HEAD_PAD)),
        gak=_row(jnp.tile(w["a_k_norm"], A_HEADS)),
        wgu=jnp.pad(w["b_gate_up"], ((0, LANE - B_GATE_RANK), (0, 0))).astype(BF16),
        bgb=_row(w["b_gate_bias"]),
        gqa=_row(w["d_qa_norm"]),
        wuq=_pad_heads_rows(w["d_uq"].T, D_HEADS, D_QK).astype(BF16),
        gdq=_lane_rep(pad_to(w["d_q_norm"], HEAD_PAD)),
        gkva=_row(w["d_kva_norm"]), wuk=wuk.astype(BF16), wuvT=wuv.T.astype(BF16),
        gdk=_row(jnp.tile(gdk[:D_NOPE], D_HEADS)), gdkpe=_row(pad_to(gdk[D_NOPE:], LANE)),
        gbo=_row(jnp.tile(w["b_out_norm"], B_HEADS)),
        cw=jnp.pad(w["c_dw_w"][:, 0, :], ((0, CONV_HIST - C_KERNEL), (0, 0))).astype(F32),
        cb=_row(w["c_dw_b"]), cg=_row(w["c_norm"]),
    )


def _shared_tables(seq):
    hid = np.arange(GROUP_WIDTH) // 64
    bd = jnp.asarray(hid[:, None] == hid[None, :], dtype=BF16)
    half = D_ROPE // 2
    freqs = ROPE_THETA ** (-jnp.arange(half, dtype=F32) / half)
    ang = jnp.arange(seq).astype(F32)[:, None] * freqs[None, :]
    cos, sin = jnp.cos(ang), jnp.sin(ang)
    zeros = jnp.zeros((seq, LANE - D_ROPE), F32)
    cpe = jnp.concatenate([cos, cos, zeros], axis=1)
    spe = jnp.concatenate([-sin, sin, zeros], axis=1)
    return dict(bd=bd, cosT=cos.T, sinT=sin.T, cpe=cpe, spe=spe)


def _bias_tiles(rel_bias, blk):
    assert REL_MAX_DIST <= blk + 1
    kk = jnp.arange(blk)[:, None]
    qq = jnp.arange(blk)[None, :]
    rb = rel_bias.astype(F32).T

    def lookup(bucket):
        onehot = bucket[None, :, :, None] == jnp.arange(REL_BUCKETS)
        return jnp.sum(jnp.where(onehot, rb[:, None, None, :], 0.0), axis=-1) * LOG2E

    d0 = jnp.where(kk <= qq, lookup(_t5_bucket(qq - kk)), NEG)
    d1 = lookup(_t5_bucket(blk + qq - kk))
    far = jnp.broadcast_to(lookup(_t5_bucket(jnp.full((1, 1), 2 * blk, I32))), d1.shape)
    return jnp.stack([d0, d1, far])


def _key_bounds(knorm, bias_max):
    km = jnp.sqrt(jnp.max(knorm[..., 0], axis=1))
    kb_a = jnp.concatenate([km[:, :A_HEADS].reshape(-1), bias_max.reshape(1)])
    return kb_a, km[:, A_HEADS:].reshape(-1)


def kernel(x, ffn1_norm, ffn1_gate, ffn1_up, ffn1_down, mix_norm, w_in, a_q_norm, a_k_norm, rel_bias,
           b_gate_up, b_gate_bias, b_out_norm, c_dw_w, c_dw_b, c_norm, d_qa_norm, d_uq, d_kva_norm,
           d_ukv, d_q_norm, d_k_norm, w_out, ffn2_norm, ffn2_gate, ffn2_up, ffn2_down):
    w = dict(mix_norm=mix_norm, w_in=w_in, a_q_norm=a_q_norm, a_k_norm=a_k_norm, b_gate_up=b_gate_up,
             b_gate_bias=b_gate_bias, b_out_norm=b_out_norm, c_dw_w=c_dw_w, c_dw_b=c_dw_b, c_norm=c_norm,
             d_qa_norm=d_qa_norm, d_uq=d_uq, d_kva_norm=d_kva_norm, d_ukv=d_ukv,
             d_q_norm=d_q_norm, d_k_norm=d_k_norm)
    bsz, seq, dm = x.shape
    depth = w_in.shape[0]
    blk = min(ATT_BLK, seq)
    bt = _bias_tiles(rel_bias, blk)
    bias_max = jnp.max(bt)
    p = {**jax.vmap(_one_layer_params)(w), **_shared_tables(seq)}
    stacked_row = lambda v: v[:, None, :].astype(F32)
    ffn1 = (stacked_row(ffn1_norm), ffn1_gate, ffn1_up, ffn1_down)
    ffn2 = (stacked_row(ffn2_norm), ffn2_gate, ffn2_up, ffn2_down)
    wo = w_out.astype(BF16)
    x2 = x.reshape(bsz * seq, dm)
    for l in range(depth):
        x2 = _ffn(x2, *ffn1, l)
        (aqT, akh, avT, iqT, aik, iwT, bq, bk, bv, bla, br, ch, dqT, dkh, dvT, knorm) = _mix_in(
            x2.reshape(bsz, seq, dm), p, l, blk=blk)
        kb_a, kb_d = _key_bounds(knorm, bias_max)
        y_a = _attention(aqT, akh, avT, kb_a, dsa=(iqT, iwT, aik, bt), blk=blk)
        y_b = _gla(bq, bk, bv, bla, br, p["gbo"], p["bd"], l)
        y_c = _conv(ch, p["cw"], p["cb"], p["cg"], l)
        y_d = _attention(dqT, dkh, dvT, kb_d, blk=blk)
        ys = [y.reshape(bsz * seq, GROUP_WIDTH) for y in (y_a, y_b, y_c, y_d)]
        x2 = _ffn(x2, *ffn2, l, mix=(ys, wo))
    return x2.reshape(bsz, seq, dm)
```

```python
import functools
import math

import jax
import jax.numpy as jnp
import numpy as np
from jax import lax
from jax.experimental import pallas as pl
from jax.experimental.pallas import tpu as pltpu

F32 = jnp.float32
BF16 = jnp.bfloat16
I32 = jnp.int32
I16 = jnp.int16

EPS = 1e-6
GROUP_WIDTH = 256
A_HEADS, A_HEAD_DIM = 4, 64
IDX_HEADS, IDX_DIM = 8, 32
TOPK_MAX = 256
REL_BUCKETS, REL_MAX_DIST = 32, 128
B_HEADS, B_KEY_DIM, B_VAL_DIM, B_GATE_RANK = 4, 32, 64, 16
B_GATE_TAU = 16.0
B_CHUNK = 64
C_CHANNELS, C_KERNEL = 256, 31
D_HEADS, D_Q_RANK, D_KV_RANK, D_NOPE, D_ROPE, D_V = 4, 256, 128, 64, 32, 64
D_QK = D_NOPE + D_ROPE
ROPE_THETA = 10000.0

LANE = 128
SUBLANES = 8
HEAD_PAD = 128
ONES_ROWS = 16
ATT_BLK = 256
ATT_GROUP = 4
BOUND_SLACK = 1.01
FIXED_REF_MAX = 40.0
INT_MIN = -2 ** 31
I16_MIN = -2 ** 15
NEG = -1e30
LOG2E = math.log2(math.e)
VMEM_LIMIT = 56 * 1024 * 1024

N_AK, N_IK, N_BQ, N_BK, N_BV, N_BG, N_BR, N_CU, N_DCQ, N_DCKV, N_DKPE, N_TOT = (
    0, 256, 384, 512, 640, 896, 1024, 1280, 1792, 2048, 2176, 2304)
T_AQ, T_AV, T_IQ, T_IW, T_TOT = 0, 512, 768, 1024, 1040


def _dot(a, b):
    return jnp.dot(a, b, preferred_element_type=F32)


def _dot_nt(a, b):
    return lax.dot_general(a, b, (((1,), (1,)), ((), ())), preferred_element_type=F32)


def _dot_tn(a, b):
    return lax.dot_general(a, b, (((0,), (0,)), ((), ())), preferred_element_type=F32)


def _group_sum(x2, bd):
    hi = x2.astype(BF16)
    lo = (x2 - hi.astype(F32)).astype(BF16)
    return _dot(hi, bd) + _dot(lo, bd)


def _const_spec(shape, layer=None):
    nd = len(shape)
    if layer is None:
        return pl.BlockSpec(shape, lambda *_: (0,) * nd, pipeline_mode=pl.Buffered(1))
    return pl.BlockSpec((None,) + tuple(shape[1:]), lambda *_: (layer,) + (0,) * (nd - 1),
                        pipeline_mode=pl.Buffered(1))


def _cparams(sem):
    return pltpu.CompilerParams(dimension_semantics=sem, vmem_limit_bytes=VMEM_LIMIT)


def _ffn_body(has_mix, fc, *refs):
    if has_mix:
        x_ref, ya, yb, yc, yd, wo_ref, g_ref, wg_ref, wu_ref, wd_ref, o_ref, h_scr = refs
    else:
        x_ref, g_ref, wg_ref, wu_ref, wd_ref, o_ref, h_scr = refs
    x = x_ref[...]
    if has_mix:
        y = jnp.concatenate([ya[...], yb[...], yc[...], yd[...]], axis=-1)
        x = x + _dot(y, wo_ref[...])
    ms = jnp.mean(x * x, axis=-1, keepdims=True)
    xn = (x * lax.rsqrt(ms + EPS) * g_ref[...]).astype(BF16)
    d_ff = wg_ref.shape[1]
    for c in range(d_ff // fc):
        sl = slice(c * fc, (c + 1) * fc)
        gate = _dot(xn, wg_ref[:, sl].astype(BF16))
        up = _dot(xn, wu_ref[:, sl].astype(BF16))
        h_scr[:, sl] = (gate * jax.nn.sigmoid(gate) * up).astype(BF16)
    o_ref[...] = x + 0.5 * _dot(h_scr[...], wd_ref[...].astype(BF16))


def _ffn(x2, g, wg, wu, wd, layer, mix=None, tm=512, fc=256):
    m, d = x2.shape
    d_ff = wg.shape[2]
    tm = min(tm, m)
    row = lambda w: pl.BlockSpec((tm, w), lambda i: (i, 0))
    in_specs = [row(d)]
    args = [x2]
    if mix is not None:
        ys, wo = mix
        in_specs += [row(GROUP_WIDTH)] * 4 + [_const_spec(wo.shape, layer)]
        args += list(ys) + [wo]
    in_specs += [_const_spec(a.shape, layer) for a in (g, wg, wu, wd)]
    args += [g, wg, wu, wd]
    return pl.pallas_call(
        functools.partial(_ffn_body, mix is not None, fc),
        grid=(m // tm,),
        in_specs=in_specs,
        out_specs=row(d),
        out_shape=jax.ShapeDtypeStruct((m, d), F32),
        scratch_shapes=[pltpu.VMEM((tm, d_ff), BF16)],
        compiler_params=_cparams(("parallel",)),
        name="ffn_mix" if mix is not None else "ffn",
    )(*args)


def _mix_in_body(tm, blk,
                 x_ref, gmix_ref, wn_ref, wt_ref, bd_ref,
                 gaq_ref, gak_ref,
                 wgu_ref, bgb_ref,
                 gqa_ref, wuq_ref, gdq_ref, cosT_ref, sinT_ref,
                 gkva_ref, wuk_ref, wuvT_ref, gdk_ref, gdkpe_ref, cpe_ref, spe_ref,
                 aqT_ref, ak_ref, avT_ref, iqT_ref, ik_ref, iwT_ref,
                 bq_ref, bk_ref, bv_ref, bla_ref, br_ref,
                 ch_ref,
                 dqT_ref, dk_ref, dvT_ref, knorm_ref):
    nlt = tm // LANE
    x = x_ref[0]
    ms = jnp.mean(x * x, axis=-1, keepdims=True)
    xn = (x * lax.rsqrt(ms + EPS) * gmix_ref[...]).astype(BF16)
    bd = bd_ref[...]
    lane = lax.broadcasted_iota(I32, (tm, LANE), 1)

    def lanes(g):
        return jnp.tile(g, (1, nlt))

    z = _dot(xn, wn_ref[...])
    zt = _dot_nt(wt_ref[...], xn)

    def zs(off, width):
        return z[:, off:off + width]

    cq = zs(N_DCQ, D_Q_RANK)
    cq_ms = jnp.mean(cq * cq, axis=-1, keepdims=True)
    cqn = (cq * lax.rsqrt(cq_ms + EPS) * gqa_ref[...]).astype(BF16)
    ckv = zs(N_DCKV, D_KV_RANK)
    ckv_ms = jnp.mean(ckv * ckv, axis=-1, keepdims=True)
    ckvn = (ckv * lax.rsqrt(ckv_ms + EPS) * gkva_ref[...]).astype(BF16)
    dq = _dot_nt(wuq_ref[...], cqn).reshape(D_HEADS, HEAD_PAD, tm)
    kn = _dot(ckvn, wuk_ref[...])
    dv = _dot_nt(wuvT_ref[...], ckvn).astype(BF16)
    gate = _dot(zs(N_BG, LANE).astype(BF16), wgu_ref[...]) + bgb_ref[...]
    ak = zs(N_AK, GROUP_WIDTH)
    ak_ms = _group_sum(ak * ak, bd) * (1.0 / A_HEAD_DIM)
    kn_ss = _group_sum(kn * kn, bd)

    aq = zt[T_AQ:T_AQ + A_HEADS * HEAD_PAD].reshape(A_HEADS, HEAD_PAD, tm)
    aq_ms = jnp.sum(aq * aq, axis=1, keepdims=True) * (1.0 / A_HEAD_DIM)
    aq = aq * lax.rsqrt(aq_ms + EPS) * lanes(gaq_ref[...])[None] * (A_HEAD_DIM ** -0.5 * LOG2E)
    aqT_ref[0] = aq.reshape(A_HEADS * HEAD_PAD, tm).astype(BF16)
    def max_sq_norm(kh):
        f = kh.astype(F32)
        n2 = _dot((f * f).astype(BF16), jnp.ones((LANE, LANE), BF16))
        return jnp.max(n2, axis=0, keepdims=True)

    def with_ones_rows(vt, heads, dim):
        ones = jnp.ones((ONES_ROWS, tm), BF16)
        return jnp.concatenate([r for h in range(heads) for r in (vt[dim * h:dim * (h + 1)], ones)], axis=0)

    av = with_ones_rows(zt[T_AV:T_AV + GROUP_WIDTH].astype(BF16), A_HEADS, A_HEAD_DIM)
    for c in range(tm // blk):
        avT_ref[0, c] = av[:, c * blk:(c + 1) * blk]
    iqT_ref[0] = zt[T_IQ:T_IQ + IDX_HEADS * IDX_DIM].astype(BF16)
    iwT_ref[0] = zt[T_IW:T_IW + IDX_HEADS] * ((IDX_HEADS ** -0.5) * (IDX_DIM ** -0.5))

    ak = ak * lax.rsqrt(ak_ms + EPS) * gak_ref[...]
    for h in range(A_HEADS):
        pair = ak[:, LANE * (h // 2):LANE * (h // 2) + LANE]
        if h % 2 == 1:
            pair = pltpu.roll(pair, 64, 1)
        kh = jnp.where(lane < A_HEAD_DIM, pair, 0.0).astype(BF16)
        ak_ref[0, h] = kh
        knorm_ref[0, 0, h:h + 1, :] = max_sq_norm(kh)
    ik_ref[0] = zs(N_IK, LANE)[:, :IDX_DIM].astype(BF16)

    bq_ref[0] = zs(N_BQ, LANE) * (B_KEY_DIM ** -0.5)
    bk_ref[0] = zs(N_BK, LANE)
    bv_ref[0] = zs(N_BV, GROUP_WIDTH).astype(BF16)
    bla_ref[0] = (jnp.minimum(gate, 0.0) - jnp.log(1.0 + jnp.exp(-jnp.abs(gate)))) * (1.0 / B_GATE_TAU)
    br_ref[0] = zs(N_BR, GROUP_WIDTH)

    ca = zs(N_CU, C_CHANNELS)
    cg = zs(N_CU + C_CHANNELS, C_CHANNELS)
    ch_ref[0] = ca * jax.nn.sigmoid(cg)

    dq_ms = jnp.sum(dq * dq, axis=1, keepdims=True) * (1.0 / D_QK)
    dq = dq * lax.rsqrt(dq_ms + EPS) * lanes(gdq_ref[...])[None] * (D_QK ** -0.5 * LOG2E)
    half = D_ROPE // 2
    x1 = dq[:, D_NOPE:D_NOPE + half]
    x2 = dq[:, D_NOPE + half:D_QK]
    cs = cosT_ref[...][None]
    sn = sinT_ref[...][None]
    dq = jnp.concatenate([dq[:, :D_NOPE], x1 * cs - x2 * sn, x2 * cs + x1 * sn, dq[:, D_QK:]], axis=1)
    dqT_ref[0] = dq.reshape(D_HEADS * HEAD_PAD, tm).astype(BF16)

    dv = with_ones_rows(dv, D_HEADS, D_V)
    for c in range(tm // blk):
        dvT_ref[0, c] = dv[:, c * blk:(c + 1) * blk]
    kpe = zs(N_DKPE, LANE)
    ss = kn_ss + jnp.sum(kpe * kpe, axis=-1, keepdims=True)
    rinv = lax.rsqrt(ss * (1.0 / D_QK) + EPS)
    kn = kn * rinv * gdk_ref[...]
    pe = kpe * gdkpe_ref[...]
    partner = jnp.where(lane < half, pltpu.roll(pe, LANE - half, 1), pltpu.roll(pe, half, 1))
    pe = pe * cpe_ref[...] + partner * spe_ref[...]
    pe = pltpu.roll(pe, D_NOPE, 1)
    for h in range(D_HEADS):
        pair = kn[:, LANE * (h // 2):LANE * (h // 2) + LANE]
        rpair = rinv[:, LANE * (h // 2):LANE * (h // 2) + LANE]
        if h % 2 == 1:
            pair = pltpu.roll(pair, 64, 1)
        else:
            rpair = pltpu.roll(rpair, 64, 1)
        kh = jnp.where(lane < D_NOPE, pair, pe * rpair).astype(BF16)
        dk_ref[0, h] = kh
        knorm_ref[0, 0, A_HEADS + h:A_HEADS + h + 1, :] = max_sq_norm(kh)


def _mix_in(x3, p, layer, tm=512, blk=ATT_BLK):
    b, l, d = x3.shape
    tm = min(tm, l)
    grid = (b, l // tm)
    nck = l // blk
    tok = lambda w: pl.BlockSpec((1, tm, w), lambda bi, i: (bi, i, 0))
    tokT = lambda r: pl.BlockSpec((1, r, tm), lambda bi, i: (bi, 0, i))
    headk = pl.BlockSpec((1, 4, tm, HEAD_PAD), lambda bi, i: (bi, 0, i, 0))
    vrows = GROUP_WIDTH + 4 * ONES_ROWS
    chunkT = pl.BlockSpec((1, tm // blk, vrows, blk), lambda bi, i: (bi, i, 0, 0))
    postab = lambda r: pl.BlockSpec((r, tm), lambda bi, i: (0, i))
    posrow = pl.BlockSpec((tm, LANE), lambda bi, i: (i, 0))
    consts = [p["gmix"], p["wn"], p["wt"], p["bd"], p["gaq"], p["gak"], p["wgu"], p["bgb"],
              p["gqa"], p["wuq"], p["gdq"]]
    consts2 = [p["gkva"], p["wuk"], p["wuvT"], p["gdk"], p["gdkpe"]]
    lspec = lambda a: _const_spec(a.shape) if a is p["bd"] else _const_spec(a.shape, layer)
    in_specs = ([tok(d)] + [lspec(a) for a in consts]
                + [postab(D_ROPE // 2), postab(D_ROPE // 2)]
                + [lspec(a) for a in consts2] + [posrow, posrow])
    args = [x3] + consts + [p["cosT"], p["sinT"]] + consts2 + [p["cpe"], p["spe"]]
    sd = jax.ShapeDtypeStruct
    out_shape = [
        sd((b, A_HEADS * HEAD_PAD, l), BF16), sd((b, A_HEADS, l, HEAD_PAD), BF16),
        sd((b, nck, vrows, blk), BF16), sd((b, IDX_HEADS * IDX_DIM, l), BF16),
        sd((b, l, IDX_DIM), BF16), sd((b, IDX_HEADS, l), F32),
        sd((b, l, LANE), F32), sd((b, l, LANE), F32), sd((b, l, GROUP_WIDTH), BF16),
        sd((b, l, LANE), F32), sd((b, l, GROUP_WIDTH), F32),
        sd((b, l, C_CHANNELS), F32),
        sd((b, D_HEADS * HEAD_PAD, l), BF16), sd((b, D_HEADS, l, HEAD_PAD), BF16),
        sd((b, nck, vrows, blk), BF16),
        sd((b, l // tm, A_HEADS + D_HEADS, LANE), F32),
    ]
    out_specs = [
        tokT(A_HEADS * HEAD_PAD), headk, chunkT, tokT(IDX_HEADS * IDX_DIM),
        tok(IDX_DIM), tokT(IDX_HEADS),
        tok(LANE), tok(LANE), tok(GROUP_WIDTH), tok(LANE), tok(GROUP_WIDTH),
        tok(C_CHANNELS),
        tokT(D_HEADS * HEAD_PAD), headk, chunkT,
        pl.BlockSpec((1, 1, A_HEADS + D_HEADS, LANE), lambda bi, i: (bi, i, 0, 0)),
    ]
    return pl.pallas_call(
        functools.partial(_mix_in_body, tm, blk),
        grid=grid, in_specs=in_specs, out_specs=out_specs, out_shape=out_shape,
        compiler_params=_cparams(("parallel", "parallel")),
        name="mix_in",
    )(*args)


def _attn_body(nh, dv, blk, topk, is_dsa, *refs):
    if is_dsa:
        (kb_ref, qT_ref, k_ref, vT_ref, iqT_ref, wT_ref, ik_ref, bt_ref,
         o_ref, m_scr, acc_scr, key_scr, run_scr, plane_scr, active_scr) = refs
    else:
        kb_ref, qT_ref, k_ref, vT_ref, o_ref, m_scr, acc_scr = refs
    vr = dv + ONES_ROWS
    i = pl.program_id(1)
    t = blk
    row = lax.broadcasted_iota(I32, (t, t), 0)
    col = lax.broadcasted_iota(I32, (t, t), 1)
    causal_pen = jnp.where(row <= col, 0.0, NEG)

    m_scr[...] = jnp.full(m_scr.shape, NEG, F32)
    acc_scr[...] = jnp.zeros(acc_scr.shape, F32)

    if is_dsa:
        @pl.when(i == 0)
        def _():
            plane_scr[...] = jnp.zeros(plane_scr.shape, I32)

        def score_chunk(j, diag):
            r0 = pl.multiple_of(j * t, t)
            ikc = ik_ref[0, pl.ds(r0, t), :]
            s = jnp.zeros((t, t), F32)
            for h in range(IDX_HEADS):
                d = _dot(ikc, iqT_ref[0, IDX_DIM * h:IDX_DIM * (h + 1), :])
                s = s + jnp.maximum(d, 0.0) * wT_ref[0, h:h + 1, :]
            bits = lax.bitcast_convert_type(s, I32)
            key = jnp.where(bits < 0, bits ^ 0x7FFFFFFF, bits)
            if diag:
                key = jnp.where(row <= col, key, INT_MIN)
            key_scr[pl.ds(r0, t), :] = key
            w = [key[SUBLANES * r:SUBLANES * (r + 1), :] for r in range(32)]
            step, mask = 16, 0x0000FFFF
            while step:
                for lo in range(32):
                    if lo & step == 0:
                        hi = lo + step
                        swap = (w[lo] ^ jnp.right_shift(w[hi], step)) & mask
                        w[lo] = w[lo] ^ swap
                        w[hi] = w[hi] ^ jnp.left_shift(swap, step)
                step //= 2
                mask ^= (mask << step) & 0xFFFFFFFF
            c0 = pl.multiple_of(j * SUBLANES, SUBLANES)
            w[0] = ~w[0]
            for p in range(32):
                plane_scr[p, pl.ds(c0, SUBLANES), :] = w[p]

        def score_pair(u, carry):
            score_chunk(2 * u, False)
            score_chunk(2 * u + 1, False)
            return carry

        lax.fori_loop(0, i // 2, score_pair, 0)

        @pl.when(i % 2 == 1)
        def _():
            score_chunk(i - 1, False)
            score_chunk(i, True)

        @pl.when(i % 2 == 0)
        def _():
            score_chunk(i, True)

        nrow = plane_scr.shape[1]
        in_range = lax.broadcasted_iota(I32, (nrow, t), 0) < (i + 1) * SUBLANES

        def col_count(words):
            pc = lax.population_count(words).reshape(nrow // SUBLANES, SUBLANES, t)
            return jnp.sum(jnp.sum(pc, axis=0), axis=0, keepdims=True)

        def decide(plane, active, n_gt, ans_u):
            ones = col_count(active & plane_scr[plane])
            take = n_gt + ones >= topk
            bit = lax.shift_right_logical(jnp.int32(INT_MIN), jnp.int32(plane))
            return (jnp.where(take, 0, -1), n_gt + jnp.where(take, 0, ones), ans_u | jnp.where(take, bit, 0))

        active_scr[...] = jnp.where(in_range, -1, 0)
        state = decide(0, active_scr[...], jnp.zeros((1, t), I32), jnp.zeros((1, t), I32))

        def plane_body(plane, state):
            flip, n_gt, ans_u = state
            active = active_scr[...] & (plane_scr[plane - 1] ^ flip)
            active_scr[...] = active
            return decide(plane, active, n_gt, ans_u)

        _, n_gt, ans_u = lax.fori_loop(1, 32, plane_body, state)
        ans = ans_u ^ INT_MIN
        need = (topk - n_gt).astype(F32)
        run_scr[...] = jnp.zeros(run_scr.shape, F32)
        stri = jnp.where(col < row, 1.0, 0.0).astype(BF16)

    def logits(h, r0):
        return _dot(k_ref[0, h, pl.ds(r0, t), :], qT_ref[0, HEAD_PAD * h:HEAD_PAD * (h + 1), :])

    def selection_pens(r0s):
        pens = []
        if is_dsa:
            for r0 in r0s:
                kc = key_scr[pl.ds(r0, t), :]
                eq = kc == ans
                eqf = jnp.where(eq, 1.0, 0.0)
                run = run_scr[0:1, :]
                rank = _dot(stri, eqf.astype(BF16)) + run
                run_scr[0:1, :] = run + jnp.sum(eqf, axis=0, keepdims=True)
                pens.append(jnp.where(kc > ans, 0.0, jnp.where(eq, jnp.where(rank < need, 0.0, NEG), NEG)))
        return pens

    b_idx = pl.program_id(0)
    bounds = []
    for h in range(nh):
        qh = qT_ref[0, HEAD_PAD * h:HEAD_PAD * (h + 1), :].astype(F32)
        bound = jnp.sqrt(jnp.sum(qh * qh, axis=0, keepdims=True)) * (kb_ref[b_idx * nh + h] * BOUND_SLACK)
        if is_dsa:
            bound = bound + kb_ref[kb_ref.shape[0] - 1]
        bounds.append(bound)
    bound_max = jnp.max(functools.reduce(jnp.maximum, bounds))
    fixed_ref_ok = bound_max <= FIXED_REF_MAX

    def fixed_ref_chunks(js, kinds):
        n = len(js)
        r0s = [pl.multiple_of(j * t, t) for j in js]
        lgs = [[logits(h, r0s[c]) for c in range(n)] for h in range(nh)]
        pens = selection_pens(r0s)
        ps = []
        for h in range(nh):
            row_ps = []
            if is_dsa:
                far_ref = bounds[h] - bt_ref[2, h, 0:1, :]
            for c in range(n):
                x, ref = lgs[h][c], bounds[h]
                if is_dsa and kinds[c] == "far":
                    x, ref = x + pens[c], far_ref
                elif is_dsa:
                    x = x + (pens[c] + bt_ref[0 if kinds[c] == "diag" else 1, h])
                elif kinds[c] == "diag":
                    x = x + causal_pen
                row_ps.append(jnp.exp2(x - ref).astype(BF16))
            ps.append(row_ps[0] if n == 1 else jnp.concatenate(row_ps, axis=0))
        for h in range(nh):
            vs = slice(vr * h, vr * (h + 1))
            vt = [vT_ref[0, js[c], vs, :] for c in range(n)]
            acc_scr[vs, :] += _dot(vt[0] if n == 1 else jnp.concatenate(vt, axis=1), ps[h])

    def super_chunk(js, kinds):
        n = len(js)
        r0s = [pl.multiple_of(j * t, t) for j in js]
        lgs = [[logits(h, r0s[c]) for c in range(n)] for h in range(nh)]
        pens = selection_pens(r0s)
        ps, alphas = [], []
        for h in range(nh):
            xs = []
            for c, j in enumerate(js):
                diag = kinds[c] == "diag"
                lg = lgs[h][c]
                if is_dsa:
                    tile = 0 if diag else jnp.minimum(i - j, 2)
                    lg = lg + (pens[c] + bt_ref[tile, h])
                elif diag:
                    lg = lg + causal_pen
                xs.append(lg)
            m_old = m_scr[h, 0:1, :]
            m_new = m_old
            for x in xs:
                m_new = jnp.maximum(m_new, jnp.max(x, axis=0, keepdims=True))
            alpha = jnp.exp2(m_old - m_new)
            m_scr[h, 0:1, :] = m_new
            ps.append([jnp.exp2((x - m_new).astype(BF16)) for x in xs])
            alphas.append(alpha)
        for h in range(nh):
            vs = slice(vr * h, vr * (h + 1))
            pv = _dot(vT_ref[0, js[0], vs, :], ps[h][0])
            for c in range(1, n):
                pv = pv + _dot(vT_ref[0, js[c], vs, :], ps[h][c])
            acc_scr[vs, :] = alphas[h] * acc_scr[vs, :] + pv

    tail = ["near", "diag"] if is_dsa else ["diag"]
    nfar = jnp.maximum(i + 1 - len(tail), 0)

    def run_chunks(step):
        def group_body(u, carry):
            step([ATT_GROUP * u + c for c in range(ATT_GROUP)], ["far"] * ATT_GROUP)
            return carry

        lax.fori_loop(0, nfar // ATT_GROUP, group_body, 0)
        for rem in range(ATT_GROUP):
            @pl.when(jnp.logical_and(i + 1 >= len(tail), nfar % ATT_GROUP == rem))
            def _():
                first = i + 1 - len(tail) - rem
                step([first + c for c in range(rem + len(tail))], ["far"] * rem + tail)

        if is_dsa:
            @pl.when(i == 0)
            def _():
                step([i], ["diag"])

    pl.when(fixed_ref_ok)(functools.partial(run_chunks, fixed_ref_chunks))
    pl.when(jnp.logical_not(fixed_ref_ok))(functools.partial(run_chunks, super_chunk))

    outs = []
    for h in range(nh):
        outs.append(acc_scr[vr * h:vr * h + dv, :] / acc_scr[vr * h + dv:vr * h + dv + 1, :])
    o_ref[0] = jnp.transpose(jnp.concatenate(outs, axis=0)).astype(o_ref.dtype)


def _attention(qT, k, vT, kbound, dsa=None, blk=ATT_BLK):
    b, nh, l, _ = k.shape
    vr = vT.shape[2] // nh
    dv = vr - ONES_ROWS
    grid = (b, l // blk)
    qspec = lambda r: pl.BlockSpec((1, r, blk), lambda bi, i: (bi, 0, i))
    kspec = pl.BlockSpec((1, nh, l, HEAD_PAD), lambda bi, i: (bi, 0, 0, 0))
    vspec = pl.BlockSpec((1, l // blk, nh * vr, blk), lambda bi, i: (bi, 0, 0, 0))
    in_specs = [pl.BlockSpec(memory_space=pltpu.SMEM), qspec(nh * HEAD_PAD), kspec, vspec]
    args = [kbound, qT, k, vT]
    scratch = [pltpu.VMEM((nh, 8, blk), F32), pltpu.VMEM((nh * vr, blk), F32)]
    topk = 0
    if dsa is not None:
        iqT, wT, ik, bt = dsa
        topk = min(TOPK_MAX, l // 4)
        in_specs += [qspec(IDX_HEADS * IDX_DIM), qspec(IDX_HEADS),
                     pl.BlockSpec((1, l, IDX_DIM), lambda bi, i: (bi, 0, 0)),
                     _const_spec(bt.shape)]
        args += [iqT, wT, ik, bt]
        assert blk == 32 * SUBLANES
        nrow = (l // blk) * SUBLANES
        scratch += [pltpu.VMEM((l, blk), I32), pltpu.VMEM((8, blk), F32),
                    pltpu.VMEM((32, nrow, blk), I32), pltpu.VMEM((nrow, blk), I32)]
    return pl.pallas_call(
        functools.partial(_attn_body, nh, dv, blk, topk, dsa is not None),
        grid=grid, in_specs=in_specs,
        out_specs=pl.BlockSpec((1, blk, nh * dv), lambda bi, i: (bi, i, 0)),
        out_shape=jax.ShapeDtypeStruct((b, l, nh * dv), BF16),
        scratch_shapes=scratch,
        compiler_params=_cparams(("parallel", "arbitrary")),
        name="dsa_attn" if dsa is not None else "mla_attn",
    )(*args)


def _gla_body(tg, q_ref, k_ref, v_ref, la_ref, r_ref, go_ref, bd_ref, o_ref, st_scr, o_scr):
    @pl.when(pl.program_id(1) == 0)
    def _():
        st_scr[...] = jnp.zeros(st_scr.shape, F32)

    cs = B_CHUNK
    la = la_ref[0]
    rl = lax.broadcasted_iota(I32, (tg, LANE), 0) & (cs - 1)
    b = la
    s = 1
    while s < cs:
        b = b + jnp.where(rl >= s, pltpu.roll(b, s, 0), 0.0)
        s *= 2
    q = q_ref[0]
    k = k_ref[0]
    qb = q * jnp.exp(b)
    tri = lax.broadcasted_iota(I32, (cs, cs), 1) <= lax.broadcasted_iota(I32, (cs, cs), 0)
    khead = lax.broadcasted_iota(I32, (cs, LANE), 1) // B_KEY_DIM
    vhead = lax.broadcasted_iota(I32, (cs, GROUP_WIDTH), 1) // B_VAL_DIM
    same_head = (lax.broadcasted_iota(I32, (GROUP_WIDTH, LANE), 0) // B_VAL_DIM
                 == lax.broadcasted_iota(I32, (GROUP_WIDTH, LANE), 1) // B_KEY_DIM)
    for c in range(tg // cs):
        sl = slice(c * cs, (c + 1) * cs)
        bc = b[sl]
        mid = bc[cs // 2:cs // 2 + 1]
        last = bc[cs - 1:cs]
        qe = q[sl] * jnp.exp(bc - mid)
        ke = (k[sl] * jnp.exp(mid - bc)).astype(BF16)
        kd = (k[sl] * jnp.exp(last - bc)).astype(BF16)
        qbc = qb[sl].astype(BF16)
        vc = v_ref[0, sl, :]
        st = st_scr[...]
        qe4 = jnp.concatenate([jnp.where(khead == h, qe, 0.0) for h in range(B_HEADS)], axis=0)
        a_all = _dot_nt(qe4.astype(BF16), ke)
        o = _dot_nt(qbc, st.astype(BF16))
        for h in range(B_HEADS):
            a = jnp.where(tri, a_all[cs * h:cs * (h + 1)], 0.0).astype(BF16)
            o = o + _dot(a, jnp.where(vhead == h, vc, jnp.zeros_like(vc)))
        o_scr[sl, :] = o
        st_scr[...] = st * jnp.exp(last) + jnp.where(same_head, _dot_tn(vc, kd), 0.0)
    o = o_scr[...]
    ms = _group_sum(o * o, bd_ref[...]) * (1.0 / B_VAL_DIM)
    r = r_ref[0]
    o_ref[0] = (o * lax.rsqrt(ms + EPS) * go_ref[...] * (r * jax.nn.sigmoid(r))).astype(o_ref.dtype)


def _gla(bq, bk, bv, bla, br, go, bd, layer, tg=1024):
    b, l, _ = bq.shape
    tg = min(tg, l)
    tok = lambda w: pl.BlockSpec((1, tg, w), lambda bi, i: (bi, i, 0))
    return pl.pallas_call(
        functools.partial(_gla_body, tg),
        grid=(b, l // tg),
        in_specs=[tok(LANE), tok(LANE), tok(GROUP_WIDTH), tok(LANE), tok(GROUP_WIDTH),
                  _const_spec(go.shape, layer), _const_spec(bd.shape)],
        out_specs=tok(GROUP_WIDTH),
        out_shape=jax.ShapeDtypeStruct((b, l, GROUP_WIDTH), BF16),
        scratch_shapes=[pltpu.VMEM((B_HEADS * B_VAL_DIM, B_HEADS * B_KEY_DIM), F32),
                        pltpu.VMEM((tg, GROUP_WIDTH), F32)],
        compiler_params=_cparams(("parallel", "arbitrary")),
        name="gla",
    )(bq, bk, bv, bla, br, go, bd)


CONV_HIST = 32


def _conv_body(tc, h_ref, w_ref, b_ref, g_ref, o_ref, buf):
    @pl.when(pl.program_id(1) == 0)
    def _():
        buf[0:CONV_HIST, :] = jnp.zeros((CONV_HIST, C_CHANNELS), F32)

    @pl.when(pl.program_id(1) > 0)
    def _():
        buf[0:CONV_HIST, :] = buf[tc:tc + CONV_HIST, :]

    buf[CONV_HIST:CONV_HIST + tc, :] = h_ref[0]
    acc = jnp.zeros((tc, C_CHANNELS), F32) + b_ref[...]
    base = CONV_HIST - (C_KERNEL - 1)
    hb = buf[...]
    rows = tc + CONV_HIST
    for r in range(SUBLANES):
        shifted = hb if r == 0 else pltpu.roll(hb, rows - r, 0)
        for j in range(C_KERNEL):
            if (base + j) % SUBLANES == r:
                a0 = base + j - r
                acc = acc + shifted[a0:a0 + tc, :] * w_ref[j:j + 1, :]
    ms = jnp.mean(acc * acc, axis=-1, keepdims=True)
    y = acc * lax.rsqrt(ms + EPS) * g_ref[...]
    o_ref[0] = (y * jax.nn.sigmoid(y)).astype(o_ref.dtype)


def _conv(ch, w, bias, g, layer, tc=512):
    b, l, c = ch.shape
    tc = min(tc, l)
    tok = pl.BlockSpec((1, tc, c), lambda bi, i: (bi, i, 0))
    return pl.pallas_call(
        functools.partial(_conv_body, tc),
        grid=(b, l // tc),
        in_specs=[tok] + [_const_spec(a.shape, layer) for a in (w, bias, g)],
        out_specs=tok,
        out_shape=jax.ShapeDtypeStruct((b, l, c), BF16),
        scratch_shapes=[pltpu.VMEM((tc + CONV_HIST, c), F32)],
        compiler_params=_cparams(("parallel", "arbitrary")),
        name="conv",
    )(ch, w, bias, g)


def _t5_bucket(dist):
    max_exact = REL_BUCKETS // 2
    d = jnp.maximum(dist, 0)
    df = jnp.maximum(d, 1).astype(F32)
    large = max_exact + (jnp.log(df / max_exact) / math.log(REL_MAX_DIST / max_exact)
                         * (REL_BUCKETS - max_exact)).astype(I32)
    large = jnp.minimum(large, REL_BUCKETS - 1)
    return jnp.where(d < max_exact, d, large)


def _pad_cols(w, width):
    return jnp.pad(w, ((0, 0), (0, width - w.shape[1])))


def _lane_rep(v):
    return jnp.broadcast_to(v[:, None], (v.shape[0], LANE))


def _pad_heads_rows(w, heads, dim):
    w = w.reshape(heads, dim, w.shape[1])
    return jnp.pad(w, ((0, 0), (0, HEAD_PAD - dim), (0, 0))).reshape(heads * HEAD_PAD, -1)


def _split_w_in(w_in):
    widths = (256, 256, 256, 256, 32, 8, 128, 128, 256, 16, 256, 512, 256, 128, 32)
    offs = np.cumsum((0,) + widths)
    return [w_in[:, offs[n]:offs[n + 1]] for n in range(len(widths))]


def _row(v):
    return v[None, :].astype(F32)


def _one_layer_params(w):
    (aq, ak, av, iq, ik, iw, bq, bk, bv, bg, br, cu, dcq, dckv, dkpe) = _split_w_in(w["w_in"])
    wn = jnp.concatenate([ak, _pad_cols(ik, LANE), bq, bk, bv, _pad_cols(bg, LANE), br, cu, dcq, dckv,
                          _pad_cols(dkpe, LANE)], axis=1).astype(BF16)
    wt = jnp.concatenate([_pad_heads_rows(aq.T, A_HEADS, A_HEAD_DIM), av.T, iq.T,
                          jnp.pad(iw.T, ((0, T_TOT - T_IW - IDX_HEADS), (0, 0)))], axis=0).astype(BF16)
    pad_to = lambda v, n: jnp.pad(v, (0, n - v.shape[0]))
    ukv = w["d_ukv"].reshape(D_KV_RANK, D_HEADS, D_NOPE + D_V)
    wuk = ukv[:, :, :D_NOPE].reshape(D_KV_RANK, D_HEADS * D_NOPE)
    wuv = ukv[:, :, D_NOPE:].reshape(D_KV_RANK, D_HEADS * D_V)
    gdk = w["d_k_norm"]
    return dict(
        gmix=_row(w["mix_norm"]), wn=wn, wt=wt,
        gaq=_lane_rep(pad_to(w["a_q_norm"], HEAD_PAD)),
        gak=_row(jnp.tile(w["a_k_norm"], A_HEADS)),
        wgu=jnp.pad(w["b_gate_up"], ((0, LANE - B_GATE_RANK), (0, 0))).astype(BF16),
        bgb=_row(w["b_gate_bias"]),
        gqa=_row(w["d_qa_norm"]),
        wuq=_pad_heads_rows(w["d_uq"].T, D_HEADS, D_QK).astype(BF16),
        gdq=_lane_rep(pad_to(w["d_q_norm"], HEAD_PAD)),
        gkva=_row(w["d_kva_norm"]), wuk=wuk.astype(BF16), wuvT=wuv.T.astype(BF16),
        gdk=_row(jnp.tile(gdk[:D_NOPE], D_HEADS)), gdkpe=_row(pad_to(gdk[D_NOPE:], LANE)),
        gbo=_row(jnp.tile(w["b_out_norm"], B_HEADS)),
        cw=jnp.pad(w["c_dw_w"][:, 0, :], ((0, CONV_HIST - C_KERNEL), (0, 0))).astype(F32),
        cb=_row(w["c_dw_b"]), cg=_row(w["c_norm"]),
    )


def _shared_tables(seq):
    hid = np.arange(GROUP_WIDTH) // 64
    bd = jnp.asarray(hid[:, None] == hid[None, :], dtype=BF16)
    half = D_ROPE // 2
    freqs = ROPE_THETA ** (-jnp.arange(half, dtype=F32) / half)
    ang = jnp.arange(seq).astype(F32)[:, None] * freqs[None, :]
    cos, sin = jnp.cos(ang), jnp.sin(ang)
    zeros = jnp.zeros((seq, LANE - D_ROPE), F32)
    cpe = jnp.concatenate([cos, cos, zeros], axis=1)
    spe = jnp.concatenate([-sin, sin, zeros], axis=1)
    return dict(bd=bd, cosT=cos.T, sinT=sin.T, cpe=cpe, spe=spe)


def _bias_tiles(rel_bias, blk):
    assert REL_MAX_DIST <= blk + 1
    kk = jnp.arange(blk)[:, None]
    qq = jnp.arange(blk)[None, :]
    rb = rel_bias.astype(F32).T

    def lookup(bucket):
        onehot = bucket[None, :, :, None] == jnp.arange(REL_BUCKETS)
        return jnp.sum(jnp.where(onehot, rb[:, None, None, :], 0.0), axis=-1) * LOG2E

    d0 = jnp.where(kk <= qq, lookup(_t5_bucket(qq - kk)), NEG)
    d1 = lookup(_t5_bucket(blk + qq - kk))
    far = jnp.broadcast_to(lookup(_t5_bucket(jnp.full((1, 1), 2 * blk, I32))), d1.shape)
    return jnp.stack([d0, d1, far])


def _key_bounds(knorm, bias_max):
    km = jnp.sqrt(jnp.max(knorm[..., 0], axis=1))
    kb_a = jnp.concatenate([km[:, :A_HEADS].reshape(-1), bias_max.reshape(1)])
    return kb_a, km[:, A_HEADS:].reshape(-1)


def kernel(x, ffn1_norm, ffn1_gate, ffn1_up, ffn1_down, mix_norm, w_in, a_q_norm, a_k_norm, rel_bias,
           b_gate_up, b_gate_bias, b_out_norm, c_dw_w, c_dw_b, c_norm, d_qa_norm, d_uq, d_kva_norm,
           d_ukv, d_q_norm, d_k_norm, w_out, ffn2_norm, ffn2_gate, ffn2_up, ffn2_down):
    w = dict(mix_norm=mix_norm, w_in=w_in, a_q_norm=a_q_norm, a_k_norm=a_k_norm, b_gate_up=b_gate_up,
             b_gate_bias=b_gate_bias, b_out_norm=b_out_norm, c_dw_w=c_dw_w, c_dw_b=c_dw_b, c_norm=c_norm,
             d_qa_norm=d_qa_norm, d_uq=d_uq, d_kva_norm=d_kva_norm, d_ukv=d_ukv,
             d_q_norm=d_q_norm, d_k_norm=d_k_norm)
    bsz, seq, dm = x.shape
    depth = w_in.shape[0]
    blk = min(ATT_BLK, seq)
    bt = _bias_tiles(rel_bias, blk)
    bias_max = jnp.max(bt)
    p = {**jax.vmap(_one_layer_params)(w), **_shared_tables(seq)}
    stacked_row = lambda v: v[:, None, :].astype(F32)
    ffn1 = (stacked_row(ffn1_norm), ffn1_gate, ffn1_up, ffn1_down)
    ffn2 = (stacked_row(ffn2_norm), ffn2_gate, ffn2_up, ffn2_down)
    wo = w_out.astype(BF16)
    x2 = x.reshape(bsz * seq, dm)
    for l in range(depth):
        x2 = _ffn(x2, *ffn1, l)
        (aqT, akh, avT, iqT, aik, iwT, bq, bk, bv, bla, br, ch, dqT, dkh, dvT, knorm) = _mix_in(
            x2.reshape(bsz, seq, dm), p, l, blk=blk)
        kb_a, kb_d = _key_bounds(knorm, bias_max)
        y_a = _attention(aqT, akh, avT, kb_a, dsa=(iqT, iwT, aik, bt), blk=blk)
        y_b = _gla(bq, bk, bv, bla, br, p["gbo"], p["bd"], l)
        y_c = _conv(ch, p["cw"], p["cb"], p["cg"], l)
        y_d = _attention(dqT, dkh, dvT, kb_d, blk=blk)
        ys = [y.reshape(bsz * seq, GROUP_WIDTH) for y in (y_a, y_b, y_c, y_d)]
        x2 = _ffn(x2, *ffn2, l, mix=(ys, wo))
    return x2.reshape(bsz, seq, dm)
```

```python
import functools
import math

import jax
import jax.numpy as jnp
import numpy as np
from jax import lax
from jax.experimental import pallas as pl
from jax.experimental.pallas import tpu as pltpu

F32 = jnp.float32
BF16 = jnp.bfloat16
I32 = jnp.int32
I16 = jnp.int16

EPS = 1e-6
GROUP_WIDTH = 256
A_HEADS, A_HEAD_DIM = 4, 64
IDX_HEADS, IDX_DIM = 8, 32
TOPK_MAX = 256
REL_BUCKETS, REL_MAX_DIST = 32, 128
B_HEADS, B_KEY_DIM, B_VAL_DIM, B_GATE_RANK = 4, 32, 64, 16
B_GATE_TAU = 16.0
B_CHUNK = 64
C_CHANNELS, C_KERNEL = 256, 31
D_HEADS, D_Q_RANK, D_KV_RANK, D_NOPE, D_ROPE, D_V = 4, 256, 128, 64, 32, 64
D_QK = D_NOPE + D_ROPE
ROPE_THETA = 10000.0

LANE = 128
SUBLANES = 8
HEAD_PAD = 128
ONES_ROWS = 16
ATT_BLK = 256
ATT_GROUP = 4
BOUND_SLACK = 1.01
FIXED_REF_MAX = 40.0
INT_MIN = -2 ** 31
I16_MIN = -2 ** 15
NEG = -1e30
LOG2E = math.log2(math.e)
VMEM_LIMIT = 56 * 1024 * 1024

N_AK, N_IK, N_BQ, N_BK, N_BV, N_BG, N_BR, N_CU, N_DCQ, N_DCKV, N_DKPE, N_TOT = (
    0, 256, 384, 512, 640, 896, 1024, 1280, 1792, 2048, 2176, 2304)
T_AQ, T_AV, T_IQ, T_IW, T_TOT = 0, 512, 768, 1024, 1040


def _dot(a, b):
    return jnp.dot(a, b, preferred_element_type=F32)


def _dot_nt(a, b):
    return lax.dot_general(a, b, (((1,), (1,)), ((), ())), preferred_element_type=F32)


def _dot_tn(a, b):
    return lax.dot_general(a, b, (((0,), (0,)), ((), ())), preferred_element_type=F32)


def _group_sum(x2, bd):
    hi = x2.astype(BF16)
    lo = (x2 - hi.astype(F32)).astype(BF16)
    return _dot(hi, bd) + _dot(lo, bd)


def _const_spec(shape, layer=None):
    nd = len(shape)
    if layer is None:
        return pl.BlockSpec(shape, lambda *_: (0,) * nd, pipeline_mode=pl.Buffered(1))
    return pl.BlockSpec((None,) + tuple(shape[1:]), lambda *_: (layer,) + (0,) * (nd - 1),
                        pipeline_mode=pl.Buffered(1))


def _cparams(sem):
    return pltpu.CompilerParams(dimension_semantics=sem, vmem_limit_bytes=VMEM_LIMIT)


def _ffn_body(has_mix, fc, *refs):
    if has_mix:
        x_ref, ya, yb, yc, yd, wo_ref, g_ref, wg_ref, wu_ref, wd_ref, o_ref, h_scr = refs
    else:
        x_ref, g_ref, wg_ref, wu_ref, wd_ref, o_ref, h_scr = refs
    x = x_ref[...]
    if has_mix:
        y = jnp.concatenate([ya[...], yb[...], yc[...], yd[...]], axis=-1)
        x = x + _dot(y, wo_ref[...])
    ms = jnp.mean(x * x, axis=-1, keepdims=True)
    xn = (x * lax.rsqrt(ms + EPS) * g_ref[...]).astype(BF16)
    d_ff = wg_ref.shape[1]
    for c in range(d_ff // fc):
        sl = slice(c * fc, (c + 1) * fc)
        gate = _dot(xn, wg_ref[:, sl].astype(BF16))
        up = _dot(xn, wu_ref[:, sl].astype(BF16))
        h_scr[:, sl] = (gate * jax.nn.sigmoid(gate) * up).astype(BF16)
    o_ref[...] = x + 0.5 * _dot(h_scr[...], wd_ref[...].astype(BF16))


def _ffn(x2, g, wg, wu, wd, layer, mix=None, tm=512, fc=256):
    m, d = x2.shape
    d_ff = wg.shape[2]
    tm = min(tm, m)
    row = lambda w: pl.BlockSpec((tm, w), lambda i: (i, 0))
    in_specs = [row(d)]
    args = [x2]
    if mix is not None:
        ys, wo = mix
        in_specs += [row(GROUP_WIDTH)] * 4 + [_const_spec(wo.shape, layer)]
        args += list(ys) + [wo]
    in_specs += [_const_spec(a.shape, layer) for a in (g, wg, wu, wd)]
    args += [g, wg, wu, wd]
    return pl.pallas_call(
        functools.partial(_ffn_body, mix is not None, fc),
        grid=(m // tm,),
        in_specs=in_specs,
        out_specs=row(d),
        out_shape=jax.ShapeDtypeStruct((m, d), F32),
        scratch_shapes=[pltpu.VMEM((tm, d_ff), BF16)],
        compiler_params=_cparams(("parallel",)),
        name="ffn_mix" if mix is not None else "ffn",
    )(*args)


def _mix_in_body(tm, blk,
                 x_ref, gmix_ref, wn_ref, wt_ref, bd_ref,
                 gaq_ref, gak_ref,
                 wgu_ref, bgb_ref,
                 gqa_ref, wuq_ref, gdq_ref, cosT_ref, sinT_ref,
                 gkva_ref, wuk_ref, wuvT_ref, gdk_ref, gdkpe_ref, cpe_ref, spe_ref,
                 aqT_ref, ak_ref, avT_ref, iqT_ref, ik_ref, iwT_ref,
                 bq_ref, bk_ref, bv_ref, bla_ref, br_ref,
                 ch_ref,
                 dqT_ref, dk_ref, dvT_ref, knorm_ref):
    nlt = tm // LANE
    x = x_ref[0]
    ms = jnp.mean(x * x, axis=-1, keepdims=True)
    xn = (x * lax.rsqrt(ms + EPS) * gmix_ref[...]).astype(BF16)
    bd = bd_ref[...]
    lane = lax.broadcasted_iota(I32, (tm, LANE), 1)

    def lanes(g):
        return jnp.tile(g, (1, nlt))

    z = _dot(xn, wn_ref[...])
    zt = _dot_nt(wt_ref[...], xn)

    def zs(off, width):
        return z[:, off:off + width]

    cq = zs(N_DCQ, D_Q_RANK)
    cq_ms = jnp.mean(cq * cq, axis=-1, keepdims=True)
    cqn = (cq * lax.rsqrt(cq_ms + EPS) * gqa_ref[...]).astype(BF16)
    ckv = zs(N_DCKV, D_KV_RANK)
    ckv_ms = jnp.mean(ckv * ckv, axis=-1, keepdims=True)
    ckvn = (ckv * lax.rsqrt(ckv_ms + EPS) * gkva_ref[...]).astype(BF16)
    dq = _dot_nt(wuq_ref[...], cqn).reshape(D_HEADS, HEAD_PAD, tm)
    kn = _dot(ckvn, wuk_ref[...])
    dv = _dot_nt(wuvT_ref[...], ckvn).astype(BF16)
    gate = _dot(zs(N_BG, LANE).astype(BF16), wgu_ref[...]) + bgb_ref[...]
    ak = zs(N_AK, GROUP_WIDTH)
    ak_ms = _group_sum(ak * ak, bd) * (1.0 / A_HEAD_DIM)
    kn_ss = _group_sum(kn * kn, bd)

    aq = zt[T_AQ:T_AQ + A_HEADS * HEAD_PAD].reshape(A_HEADS, HEAD_PAD, tm)
    aq_ms = jnp.sum(aq * aq, axis=1, keepdims=True) * (1.0 / A_HEAD_DIM)
    aq = aq * lax.rsqrt(aq_ms + EPS) * lanes(gaq_ref[...])[None] * (A_HEAD_DIM ** -0.5 * LOG2E)
    aqT_ref[0] = aq.reshape(A_HEADS * HEAD_PAD, tm).astype(BF16)
    def emit_key_norms(first_row, sq_norms):
        top = jnp.max(sq_norms, axis=0, keepdims=True)
        for h in range(4):
            knorm_ref[0, 0, first_row + h:first_row + h + 1, :] = jnp.broadcast_to(
                top[:, 64 * h:64 * h + 1], (1, LANE))

    def with_ones_rows(vt, heads, dim):
        ones = jnp.ones((ONES_ROWS, tm), BF16)
        return jnp.concatenate([r for h in range(heads) for r in (vt[dim * h:dim * (h + 1)], ones)], axis=0)

    av = with_ones_rows(zt[T_AV:T_AV + GROUP_WIDTH].astype(BF16), A_HEADS, A_HEAD_DIM)
    for c in range(tm // blk):
        avT_ref[0, c] = av[:, c * blk:(c + 1) * blk]
    iqT_ref[0] = zt[T_IQ:T_IQ + IDX_HEADS * IDX_DIM].astype(BF16)
    iwT_ref[0] = zt[T_IW:T_IW + IDX_HEADS] * ((IDX_HEADS ** -0.5) * (IDX_DIM ** -0.5))

    ak = ak * lax.rsqrt(ak_ms + EPS) * gak_ref[...]
    for h in range(A_HEADS):
        pair = ak[:, LANE * (h // 2):LANE * (h // 2) + LANE]
        if h % 2 == 1:
            pair = pltpu.roll(pair, 64, 1)
        kh = jnp.where(lane < A_HEAD_DIM, pair, 0.0).astype(BF16)
        ak_ref[0, h] = kh
    emit_key_norms(0, _dot((ak * ak).astype(BF16), bd))
    ik_ref[0] = zs(N_IK, LANE)[:, :IDX_DIM].astype(BF16)

    bq_ref[0] = zs(N_BQ, LANE) * (B_KEY_DIM ** -0.5)
    bk_ref[0] = zs(N_BK, LANE)
    bv_ref[0] = zs(N_BV, GROUP_WIDTH).astype(BF16)
    bla_ref[0] = (jnp.minimum(gate, 0.0) - jnp.log(1.0 + jnp.exp(-jnp.abs(gate)))) * (1.0 / B_GATE_TAU)
    br_ref[0] = zs(N_BR, GROUP_WIDTH)

    ca = zs(N_CU, C_CHANNELS)
    cg = zs(N_CU + C_CHANNELS, C_CHANNELS)
    ch_ref[0] = ca * jax.nn.sigmoid(cg)

    dq_ms = jnp.sum(dq * dq, axis=1, keepdims=True) * (1.0 / D_QK)
    dq = dq * lax.rsqrt(dq_ms + EPS) * lanes(gdq_ref[...])[None] * (D_QK ** -0.5 * LOG2E)
    half = D_ROPE // 2
    x1 = dq[:, D_NOPE:D_NOPE + half]
    x2 = dq[:, D_NOPE + half:D_QK]
    cs = cosT_ref[...][None]
    sn = sinT_ref[...][None]
    dq = jnp.concatenate([dq[:, :D_NOPE], x1 * cs - x2 * sn, x2 * cs + x1 * sn, dq[:, D_QK:]], axis=1)
    dqT_ref[0] = dq.reshape(D_HEADS * HEAD_PAD, tm).astype(BF16)

    dv = with_ones_rows(dv, D_HEADS, D_V)
    for c in range(tm // blk):
        dvT_ref[0, c] = dv[:, c * blk:(c + 1) * blk]
    kpe = zs(N_DKPE, LANE)
    ss = kn_ss + jnp.sum(kpe * kpe, axis=-1, keepdims=True)
    rinv = lax.rsqrt(ss * (1.0 / D_QK) + EPS)
    kn = kn * rinv * gdk_ref[...]
    pe = kpe * gdkpe_ref[...]
    partner = jnp.where(lane < half, pltpu.roll(pe, LANE - half, 1), pltpu.roll(pe, half, 1))
    pe = pe * cpe_ref[...] + partner * spe_ref[...]
    pe = pltpu.roll(pe, D_NOPE, 1)
    for h in range(D_HEADS):
        pair = kn[:, LANE * (h // 2):LANE * (h // 2) + LANE]
        rpair = rinv[:, LANE * (h // 2):LANE * (h // 2) + LANE]
        if h % 2 == 1:
            pair = pltpu.roll(pair, 64, 1)
        else:
            rpair = pltpu.roll(rpair, 64, 1)
        kh = jnp.where(lane < D_NOPE, pair, pe * rpair).astype(BF16)
        dk_ref[0, h] = kh
    pe_sq = _dot((pe * pe).astype(BF16), jnp.ones((LANE, LANE), BF16))
    emit_key_norms(A_HEADS, _dot((kn * kn).astype(BF16), bd) + jnp.tile(pe_sq, (1, 2)) * (rinv * rinv))


def _mix_in(x3, p, layer, tm=512, blk=ATT_BLK):
    b, l, d = x3.shape
    tm = min(tm, l)
    grid = (b, l // tm)
    nck = l // blk
    tok = lambda w: pl.BlockSpec((1, tm, w), lambda bi, i: (bi, i, 0))
    tokT = lambda r: pl.BlockSpec((1, r, tm), lambda bi, i: (bi, 0, i))
    headk = pl.BlockSpec((1, 4, tm, HEAD_PAD), lambda bi, i: (bi, 0, i, 0))
    vrows = GROUP_WIDTH + 4 * ONES_ROWS
    chunkT = pl.BlockSpec((1, tm // blk, vrows, blk), lambda bi, i: (bi, i, 0, 0))
    postab = lambda r: pl.BlockSpec((r, tm), lambda bi, i: (0, i))
    posrow = pl.BlockSpec((tm, LANE), lambda bi, i: (i, 0))
    consts = [p["gmix"], p["wn"], p["wt"], p["bd"], p["gaq"], p["gak"], p["wgu"], p["bgb"],
              p["gqa"], p["wuq"], p["gdq"]]
    consts2 = [p["gkva"], p["wuk"], p["wuvT"], p["gdk"], p["gdkpe"]]
    lspec = lambda a: _const_spec(a.shape) if a is p["bd"] else _const_spec(a.shape, layer)
    in_specs = ([tok(d)] + [lspec(a) for a in consts]
                + [postab(D_ROPE // 2), postab(D_ROPE // 2)]
                + [lspec(a) for a in consts2] + [posrow, posrow])
    args = [x3] + consts + [p["cosT"], p["sinT"]] + consts2 + [p["cpe"], p["spe"]]
    sd = jax.ShapeDtypeStruct
    out_shape = [
        sd((b, A_HEADS * HEAD_PAD, l), BF16), sd((b, A_HEADS, l, HEAD_PAD), BF16),
        sd((b, nck, vrows, blk), BF16), sd((b, IDX_HEADS * IDX_DIM, l), BF16),
        sd((b, l, IDX_DIM), BF16), sd((b, IDX_HEADS, l), F32),
        sd((b, l, LANE), F32), sd((b, l, LANE), F32), sd((b, l, GROUP_WIDTH), BF16),
        sd((b, l, LANE), F32), sd((b, l, GROUP_WIDTH), F32),
        sd((b, l, C_CHANNELS), F32),
        sd((b, D_HEADS * HEAD_PAD, l), BF16), sd((b, D_HEADS, l, HEAD_PAD), BF16),
        sd((b, nck, vrows, blk), BF16),
        sd((b, l // tm, A_HEADS + D_HEADS, LANE), F32),
    ]
    out_specs = [
        tokT(A_HEADS * HEAD_PAD), headk, chunkT, tokT(IDX_HEADS * IDX_DIM),
        tok(IDX_DIM), tokT(IDX_HEADS),
        tok(LANE), tok(LANE), tok(GROUP_WIDTH), tok(LANE), tok(GROUP_WIDTH),
        tok(C_CHANNELS),
        tokT(D_HEADS * HEAD_PAD), headk, chunkT,
        pl.BlockSpec((1, 1, A_HEADS + D_HEADS, LANE), lambda bi, i: (bi, i, 0, 0)),
    ]
    return pl.pallas_call(
        functools.partial(_mix_in_body, tm, blk),
        grid=grid, in_specs=in_specs, out_specs=out_specs, out_shape=out_shape,
        compiler_params=_cparams(("parallel", "parallel")),
        name="mix_in",
    )(*args)


def _attn_body(nh, dv, blk, topk, is_dsa, *refs):
    if is_dsa:
        (kb_ref, qT_ref, k_ref, vT_ref, iqT_ref, wT_ref, ik_ref, bt_ref,
         o_ref, m_scr, acc_scr, key_scr, run_scr, plane_scr, active_scr) = refs
    else:
        kb_ref, qT_ref, k_ref, vT_ref, o_ref, m_scr, acc_scr = refs
    vr = dv + ONES_ROWS
    i = pl.program_id(1)
    t = blk
    row = lax.broadcasted_iota(I32, (t, t), 0)
    col = lax.broadcasted_iota(I32, (t, t), 1)
    causal_pen = jnp.where(row <= col, 0.0, NEG)

    m_scr[...] = jnp.full(m_scr.shape, NEG, F32)
    acc_scr[...] = jnp.zeros(acc_scr.shape, F32)

    if is_dsa:
        @pl.when(i == 0)
        def _():
            plane_scr[...] = jnp.zeros(plane_scr.shape, I32)

        def score_chunk(j, diag):
            r0 = pl.multiple_of(j * t, t)
            ikc = ik_ref[0, pl.ds(r0, t), :]
            s = jnp.zeros((t, t), F32)
            for h in range(IDX_HEADS):
                d = _dot(ikc, iqT_ref[0, IDX_DIM * h:IDX_DIM * (h + 1), :])
                s = s + jnp.maximum(d, 0.0) * wT_ref[0, h:h + 1, :]
            bits = lax.bitcast_convert_type(s, I32)
            key = jnp.where(bits < 0, bits ^ 0x7FFFFFFF, bits)
            if diag:
                key = jnp.where(row <= col, key, INT_MIN)
            key_scr[pl.ds(r0, t), :] = key
            w = [key[SUBLANES * r:SUBLANES * (r + 1), :] for r in range(32)]
            step, mask = 16, 0x0000FFFF
            while step:
                for lo in range(32):
                    if lo & step == 0:
                        hi = lo + step
                        swap = (w[lo] ^ jnp.right_shift(w[hi], step)) & mask
                        w[lo] = w[lo] ^ swap
                        w[hi] = w[hi] ^ jnp.left_shift(swap, step)
                step //= 2
                mask ^= (mask << step) & 0xFFFFFFFF
            c0 = pl.multiple_of(j * SUBLANES, SUBLANES)
            w[0] = ~w[0]
            for p in range(32):
                plane_scr[p, pl.ds(c0, SUBLANES), :] = w[p]

        def score_pair(u, carry):
            score_chunk(2 * u, False)
            score_chunk(2 * u + 1, False)
            return carry

        lax.fori_loop(0, i // 2, score_pair, 0)

        @pl.when(i % 2 == 1)
        def _():
            score_chunk(i - 1, False)
            score_chunk(i, True)

        @pl.when(i % 2 == 0)
        def _():
            score_chunk(i, True)

        def kth_largest(nrow):
            rows = pl.ds(0, nrow)
            in_range = lax.broadcasted_iota(I32, (nrow, t), 0) < (i + 1) * SUBLANES

            def col_count(words):
                pc = lax.population_count(words).reshape(nrow // SUBLANES, SUBLANES, t)
                return jnp.sum(jnp.sum(pc, axis=0), axis=0, keepdims=True)

            def decide(plane, active, n_gt, ans_u):
                ones = col_count(active & plane_scr[plane, rows, :])
                take = n_gt + ones >= topk
                bit = lax.shift_right_logical(jnp.int32(INT_MIN), jnp.int32(plane))
                return (jnp.where(take, 0, -1), n_gt + jnp.where(take, 0, ones),
                        ans_u | jnp.where(take, bit, 0))

            active_scr[rows, :] = jnp.where(in_range, -1, 0)
            state = decide(0, active_scr[rows, :], jnp.zeros((1, t), I32), jnp.zeros((1, t), I32))

            def plane_body(plane, state):
                flip, n_gt, ans_u = state
                active = active_scr[rows, :] & (plane_scr[plane - 1, rows, :] ^ flip)
                active_scr[rows, :] = active
                return decide(plane, active, n_gt, ans_u)

            _, n_gt, ans_u = lax.fori_loop(1, 32, plane_body, state)
            return n_gt, ans_u

        nrow_all = plane_scr.shape[1]
        nrow_half = (nrow_all // SUBLANES // 2) * SUBLANES
        if nrow_half:
            n_gt, ans_u = lax.cond((i + 1) * SUBLANES <= nrow_half,
                                   functools.partial(kth_largest, nrow_half),
                                   functools.partial(kth_largest, nrow_all))
        else:
            n_gt, ans_u = kth_largest(nrow_all)
        ans = ans_u ^ INT_MIN
        need = (topk - n_gt).astype(F32)
        run_scr[...] = jnp.zeros(run_scr.shape, F32)
        stri = jnp.where(col < row, 1.0, 0.0).astype(BF16)

    def logits(h, r0):
        return _dot(k_ref[0, h, pl.ds(r0, t), :], qT_ref[0, HEAD_PAD * h:HEAD_PAD * (h + 1), :])

    def selection_pens(r0s):
        pens = []
        if is_dsa:
            for r0 in r0s:
                kc = key_scr[pl.ds(r0, t), :]
                eq = kc == ans
                eqf = jnp.where(eq, 1.0, 0.0)
                run = run_scr[0:1, :]
                rank = _dot(stri, eqf.astype(BF16)) + run
                run_scr[0:1, :] = run + jnp.sum(eqf, axis=0, keepdims=True)
                pens.append(jnp.where(kc > ans, 0.0, jnp.where(eq, jnp.where(rank < need, 0.0, NEG), NEG)))
        return pens

    b_idx = pl.program_id(0)
    bounds = []
    for h in range(nh):
        qh = qT_ref[0, HEAD_PAD * h:HEAD_PAD * (h + 1), :].astype(F32)
        bound = jnp.sqrt(jnp.sum(qh * qh, axis=0, keepdims=True)) * (kb_ref[b_idx * nh + h] * BOUND_SLACK)
        if is_dsa:
            bound = bound + kb_ref[kb_ref.shape[0] - 1]
        bounds.append(bound)
    bound_max = jnp.max(functools.reduce(jnp.maximum, bounds))
    fixed_ref_ok = bound_max <= FIXED_REF_MAX

    def fixed_ref_chunks(js, kinds):
        n = len(js)
        r0s = [pl.multiple_of(j * t, t) for j in js]
        lgs = [[logits(h, r0s[c]) for c in range(n)] for h in range(nh)]
        pens = selection_pens(r0s)
        ps = []
        for h in range(nh):
            row_ps = []
            if is_dsa:
                far_ref = bounds[h] - bt_ref[2, h, 0:1, :]
            for c in range(n):
                x, ref = lgs[h][c], bounds[h]
                if is_dsa and kinds[c] == "far":
                    x, ref = x + pens[c], far_ref
                elif is_dsa:
                    x = x + (pens[c] + bt_ref[0 if kinds[c] == "diag" else 1, h])
                elif kinds[c] == "diag":
                    x = x + causal_pen
                row_ps.append(jnp.exp2(x - ref).astype(BF16))
            ps.append(row_ps[0] if n == 1 else jnp.concatenate(row_ps, axis=0))
        for h in range(nh):
            vs = slice(vr * h, vr * (h + 1))
            vt = [vT_ref[0, js[c], vs, :] for c in range(n)]
            acc_scr[vs, :] += _dot(vt[0] if n == 1 else jnp.concatenate(vt, axis=1), ps[h])

    def super_chunk(js, kinds):
        n = len(js)
        r0s = [pl.multiple_of(j * t, t) for j in js]
        lgs = [[logits(h, r0s[c]) for c in range(n)] for h in range(nh)]
        pens = selection_pens(r0s)
        ps, alphas = [], []
        for h in range(nh):
            xs = []
            for c, j in enumerate(js):
                diag = kinds[c] == "diag"
                lg = lgs[h][c]
                if is_dsa:
                    tile = 0 if diag else jnp.minimum(i - j, 2)
                    lg = lg + (pens[c] + bt_ref[tile, h])
                elif diag:
                    lg = lg + causal_pen
                xs.append(lg)
            m_old = m_scr[h, 0:1, :]
            m_new = m_old
            for x in xs:
                m_new = jnp.maximum(m_new, jnp.max(x, axis=0, keepdims=True))
            alpha = jnp.exp2(m_old - m_new)
            m_scr[h, 0:1, :] = m_new
            ps.append([jnp.exp2((x - m_new).astype(BF16)) for x in xs])
            alphas.append(alpha)
        for h in range(nh):
            vs = slice(vr * h, vr * (h + 1))
            pv = _dot(vT_ref[0, js[0], vs, :], ps[h][0])
            for c in range(1, n):
                pv = pv + _dot(vT_ref[0, js[c], vs, :], ps[h][c])
            acc_scr[vs, :] = alphas[h] * acc_scr[vs, :] + pv

    tail = ["near", "diag"] if is_dsa else ["diag"]
    nfar = jnp.maximum(i + 1 - len(tail), 0)

    def run_chunks(step):
        def group_body(u, carry):
            step([ATT_GROUP * u + c for c in range(ATT_GROUP)], ["far"] * ATT_GROUP)
            return carry

        lax.fori_loop(0, nfar // ATT_GROUP, group_body, 0)
        for rem in range(ATT_GROUP):
            @pl.when(jnp.logical_and(i + 1 >= len(tail), nfar % ATT_GROUP == rem))
            def _():
                first = i + 1 - len(tail) - rem
                step([first + c for c in range(rem + len(tail))], ["far"] * rem + tail)

        if is_dsa:
            @pl.when(i == 0)
            def _():
                step([i], ["diag"])

    pl.when(fixed_ref_ok)(functools.partial(run_chunks, fixed_ref_chunks))
    pl.when(jnp.logical_not(fixed_ref_ok))(functools.partial(run_chunks, super_chunk))

    outs = []
    for h in range(nh):
        outs.append(acc_scr[vr * h:vr * h + dv, :] / acc_scr[vr * h + dv:vr * h + dv + 1, :])
    o_ref[0] = jnp.transpose(jnp.concatenate(outs, axis=0)).astype(o_ref.dtype)


def _attention(qT, k, vT, kbound, dsa=None, blk=ATT_BLK):
    b, nh, l, _ = k.shape
    vr = vT.shape[2] // nh
    dv = vr - ONES_ROWS
    grid = (b, l // blk)
    qspec = lambda r: pl.BlockSpec((1, r, blk), lambda bi, i: (bi, 0, i))
    kspec = pl.BlockSpec((1, nh, l, HEAD_PAD), lambda bi, i: (bi, 0, 0, 0))
    vspec = pl.BlockSpec((1, l // blk, nh * vr, blk), lambda bi, i: (bi, 0, 0, 0))
    in_specs = [pl.BlockSpec(memory_space=pltpu.SMEM), qspec(nh * HEAD_PAD), kspec, vspec]
    args = [kbound, qT, k, vT]
    scratch = [pltpu.VMEM((nh, 8, blk), F32), pltpu.VMEM((nh * vr, blk), F32)]
    topk = 0
    if dsa is not None:
        iqT, wT, ik, bt = dsa
        topk = min(TOPK_MAX, l // 4)
        in_specs += [qspec(IDX_HEADS * IDX_DIM), qspec(IDX_HEADS),
                     pl.BlockSpec((1, l, IDX_DIM), lambda bi, i: (bi, 0, 0)),
                     _const_spec(bt.shape)]
        args += [iqT, wT, ik, bt]
        assert blk == 32 * SUBLANES
        nrow = (l // blk) * SUBLANES
        scratch += [pltpu.VMEM((l, blk), I32), pltpu.VMEM((8, blk), F32),
                    pltpu.VMEM((32, nrow, blk), I32), pltpu.VMEM((nrow, blk), I32)]
    return pl.pallas_call(
        functools.partial(_attn_body, nh, dv, blk, topk, dsa is not None),
        grid=grid, in_specs=in_specs,
        out_specs=pl.BlockSpec((1, blk, nh * dv), lambda bi, i: (bi, i, 0)),
        out_shape=jax.ShapeDtypeStruct((b, l, nh * dv), BF16),
        scratch_shapes=scratch,
        compiler_params=_cparams(("parallel", "arbitrary")),
        name="dsa_attn" if dsa is not None else "mla_attn",
    )(*args)


def _gla_body(tg, q_ref, k_ref, v_ref, la_ref, r_ref, go_ref, bd_ref, o_ref, st_scr, o_scr):
    @pl.when(pl.program_id(1) == 0)
    def _():
        st_scr[...] = jnp.zeros(st_scr.shape, F32)

    cs = B_CHUNK
    la = la_ref[0]
    rl = lax.broadcasted_iota(I32, (tg, LANE), 0) & (cs - 1)
    b = la
    s = 1
    while s < cs:
        b = b + jnp.where(rl >= s, pltpu.roll(b, s, 0), 0.0)
        s *= 2
    q = q_ref[0]
    k = k_ref[0]
    qb = q * jnp.exp(b)
    tri = lax.broadcasted_iota(I32, (cs, cs), 1) <= lax.broadcasted_iota(I32, (cs, cs), 0)
    khead = lax.broadcasted_iota(I32, (cs, LANE), 1) // B_KEY_DIM
    vhead = lax.broadcasted_iota(I32, (cs, GROUP_WIDTH), 1) // B_VAL_DIM
    same_head = (lax.broadcasted_iota(I32, (GROUP_WIDTH, LANE), 0) // B_VAL_DIM
                 == lax.broadcasted_iota(I32, (GROUP_WIDTH, LANE), 1) // B_KEY_DIM)
    for c in range(tg // cs):
        sl = slice(c * cs, (c + 1) * cs)
        bc = b[sl]
        mid = bc[cs // 2:cs // 2 + 1]
        last = bc[cs - 1:cs]
        qe = q[sl] * jnp.exp(bc - mid)
        ke = (k[sl] * jnp.exp(mid - bc)).astype(BF16)
        kd = (k[sl] * jnp.exp(last - bc)).astype(BF16)
        qbc = qb[sl].astype(BF16)
        vc = v_ref[0, sl, :]
        st = st_scr[...]
        qe4 = jnp.concatenate([jnp.where(khead == h, qe, 0.0) for h in range(B_HEADS)], axis=0)
        a_all = _dot_nt(qe4.astype(BF16), ke)
        o = _dot_nt(qbc, st.astype(BF16))
        for h in range(B_HEADS):
            a = jnp.where(tri, a_all[cs * h:cs * (h + 1)], 0.0).astype(BF16)
            o = o + _dot(a, jnp.where(vhead == h, vc, jnp.zeros_like(vc)))
        o_scr[sl, :] = o
        st_scr[...] = st * jnp.exp(last) + jnp.where(same_head, _dot_tn(vc, kd), 0.0)
    o = o_scr[...]
    ms = _group_sum(o * o, bd_ref[...]) * (1.0 / B_VAL_DIM)
    r = r_ref[0]
    o_ref[0] = (o * lax.rsqrt(ms + EPS) * go_ref[...] * (r * jax.nn.sigmoid(r))).astype(o_ref.dtype)


def _gla(bq, bk, bv, bla, br, go, bd, layer, tg=1024):
    b, l, _ = bq.shape
    tg = min(tg, l)
    tok = lambda w: pl.BlockSpec((1, tg, w), lambda bi, i: (bi, i, 0))
    return pl.pallas_call(
        functools.partial(_gla_body, tg),
        grid=(b, l // tg),
        in_specs=[tok(LANE), tok(LANE), tok(GROUP_WIDTH), tok(LANE), tok(GROUP_WIDTH),
                  _const_spec(go.shape, layer), _const_spec(bd.shape)],
        out_specs=tok(GROUP_WIDTH),
        out_shape=jax.ShapeDtypeStruct((b, l, GROUP_WIDTH), BF16),
        scratch_shapes=[pltpu.VMEM((B_HEADS * B_VAL_DIM, B_HEADS * B_KEY_DIM), F32),
                        pltpu.VMEM((tg, GROUP_WIDTH), F32)],
        compiler_params=_cparams(("parallel", "arbitrary")),
        name="gla",
    )(bq, bk, bv, bla, br, go, bd)


CONV_HIST = 32


def _conv_body(tc, h_ref, w_ref, b_ref, g_ref, o_ref, buf):
    @pl.when(pl.program_id(1) == 0)
    def _():
        buf[0:CONV_HIST, :] = jnp.zeros((CONV_HIST, C_CHANNELS), F32)

    @pl.when(pl.program_id(1) > 0)
    def _():
        buf[0:CONV_HIST, :] = buf[tc:tc + CONV_HIST, :]

    buf[CONV_HIST:CONV_HIST + tc, :] = h_ref[0]
    acc = jnp.zeros((tc, C_CHANNELS), F32) + b_ref[...]
    base = CONV_HIST - (C_KERNEL - 1)
    hb = buf[...]
    rows = tc + CONV_HIST
    for r in range(SUBLANES):
        shifted = hb if r == 0 else pltpu.roll(hb, rows - r, 0)
        for j in range(C_KERNEL):
            if (base + j) % SUBLANES == r:
                a0 = base + j - r
                acc = acc + shifted[a0:a0 + tc, :] * w_ref[j:j + 1, :]
    ms = jnp.mean(acc * acc, axis=-1, keepdims=True)
    y = acc * lax.rsqrt(ms + EPS) * g_ref[...]
    o_ref[0] = (y * jax.nn.sigmoid(y)).astype(o_ref.dtype)


def _conv(ch, w, bias, g, layer, tc=512):
    b, l, c = ch.shape
    tc = min(tc, l)
    tok = pl.BlockSpec((1, tc, c), lambda bi, i: (bi, i, 0))
    return pl.pallas_call(
        functools.partial(_conv_body, tc),
        grid=(b, l // tc),
        in_specs=[tok] + [_const_spec(a.shape, layer) for a in (w, bias, g)],
        out_specs=tok,
        out_shape=jax.ShapeDtypeStruct((b, l, c), BF16),
        scratch_shapes=[pltpu.VMEM((tc + CONV_HIST, c), F32)],
        compiler_params=_cparams(("parallel", "arbitrary")),
        name="conv",
    )(ch, w, bias, g)


def _t5_bucket(dist):
    max_exact = REL_BUCKETS // 2
    d = jnp.maximum(dist, 0)
    df = jnp.maximum(d, 1).astype(F32)
    large = max_exact + (jnp.log(df / max_exact) / math.log(REL_MAX_DIST / max_exact)
                         * (REL_BUCKETS - max_exact)).astype(I32)
    large = jnp.minimum(large, REL_BUCKETS - 1)
    return jnp.where(d < max_exact, d, large)


def _pad_cols(w, width):
    return jnp.pad(w, ((0, 0), (0, width - w.shape[1])))


def _lane_rep(v):
    return jnp.broadcast_to(v[:, None], (v.shape[0], LANE))


def _pad_heads_rows(w, heads, dim):
    w = w.reshape(heads, dim, w.shape[1])
    return jnp.pad(w, ((0, 0), (0, HEAD_PAD - dim), (0, 0))).reshape(heads * HEAD_PAD, -1)


def _split_w_in(w_in):
    widths = (256, 256, 256, 256, 32, 8, 128, 128, 256, 16, 256, 512, 256, 128, 32)
    offs = np.cumsum((0,) + widths)
    return [w_in[:, offs[n]:offs[n + 1]] for n in range(len(widths))]


def _row(v):
    return v[None, :].astype(F32)


def _one_layer_params(w):
    (aq, ak, av, iq, ik, iw, bq, bk, bv, bg, br, cu, dcq, dckv, dkpe) = _split_w_in(w["w_in"])
    wn = jnp.concatenate([ak, _pad_cols(ik, LANE), bq, bk, bv, _pad_cols(bg, LANE), br, cu, dcq, dckv,
                          _pad_cols(dkpe, LANE)], axis=1).astype(BF16)
    wt = jnp.concatenate([_pad_heads_rows(aq.T, A_HEADS, A_HEAD_DIM), av.T, iq.T,
                          jnp.pad(iw.T, ((0, T_TOT - T_IW - IDX_HEADS), (0, 0)))], axis=0).astype(BF16)
    pad_to = lambda v, n: jnp.pad(v, (0, n - v.shape[0]))
    ukv = w["d_ukv"].reshape(D_KV_RANK, D_HEADS, D_NOPE + D_V)
    wuk = ukv[:, :, :D_NOPE].reshape(D_KV_RANK, D_HEADS * D_NOPE)
    wuv = ukv[:, :, D_NOPE:].reshape(D_KV_RANK, D_HEADS * D_V)
    gdk = w["d_k_norm"]
    return dict(
        gmix=_row(w["mix_norm"]), wn=wn, wt=wt,
        gaq=_lane_rep(pad_to(w["a_q_norm"], HEAD_PAD)),
        gak=_row(jnp.tile(w["a_k_norm"], A_HEADS)),
        wgu=jnp.pad(w["b_gate_up"], ((0, LANE - B_GATE_RANK), (0, 0))).astype(BF16),
        bgb=_row(w["b_gate_bias"]),
        gqa=_row(w["d_qa_norm"]),
        wuq=_pad_heads_rows(w["d_uq"].T, D_HEADS, D_QK).astype(BF16),
        gdq=_lane_rep(pad_to(w["d_q_norm"], HEAD_PAD)),
        gkva=_row(w["d_kva_norm"]), wuk=wuk.astype(BF16), wuvT=wuv.T.astype(BF16),
        gdk=_row(jnp.tile(gdk[:D_NOPE], D_HEADS)), gdkpe=_row(pad_to(gdk[D_NOPE:], LANE)),
        gbo=_row(jnp.tile(w["b_out_norm"], B_HEADS)),
        cw=jnp.pad(w["c_dw_w"][:, 0, :], ((0, CONV_HIST - C_KERNEL), (0, 0))).astype(F32),
        cb=_row(w["c_dw_b"]), cg=_row(w["c_norm"]),
    )


def _shared_tables(seq):
    hid = np.arange(GROUP_WIDTH) // 64
    bd = jnp.asarray(hid[:, None] == hid[None, :], dtype=BF16)
    half = D_ROPE // 2
    freqs = ROPE_THETA ** (-jnp.arange(half, dtype=F32) / half)
    ang = jnp.arange(seq).astype(F32)[:, None] * freqs[None, :]
    cos, sin = jnp.cos(ang), jnp.sin(ang)
    zeros = jnp.zeros((seq, LANE - D_ROPE), F32)
    cpe = jnp.concatenate([cos, cos, zeros], axis=1)
    spe = jnp.concatenate([-sin, sin, zeros], axis=1)
    return dict(bd=bd, cosT=cos.T, sinT=sin.T, cpe=cpe, spe=spe)


def _bias_tiles(rel_bias, blk):
    assert REL_MAX_DIST <= blk + 1
    kk = jnp.arange(blk)[:, None]
    qq = jnp.arange(blk)[None, :]
    rb = rel_bias.astype(F32).T

    def lookup(bucket):
        onehot = bucket[None, :, :, None] == jnp.arange(REL_BUCKETS)
        return jnp.sum(jnp.where(onehot, rb[:, None, None, :], 0.0), axis=-1) * LOG2E

    d0 = jnp.where(kk <= qq, lookup(_t5_bucket(qq - kk)), NEG)
    d1 = lookup(_t5_bucket(blk + qq - kk))
    far = jnp.broadcast_to(lookup(_t5_bucket(jnp.full((1, 1), 2 * blk, I32))), d1.shape)
    return jnp.stack([d0, d1, far])


def _key_bounds(knorm, bias_max):
    km = jnp.sqrt(jnp.max(knorm[..., 0], axis=1))
    kb_a = jnp.concatenate([km[:, :A_HEADS].reshape(-1), bias_max.reshape(1)])
    return kb_a, km[:, A_HEADS:].reshape(-1)


def kernel(x, ffn1_norm, ffn1_gate, ffn1_up, ffn1_down, mix_norm, w_in, a_q_norm, a_k_norm, rel_bias,
           b_gate_up, b_gate_bias, b_out_norm, c_dw_w, c_dw_b, c_norm, d_qa_norm, d_uq, d_kva_norm,
           d_ukv, d_q_norm, d_k_norm, w_out, ffn2_norm, ffn2_gate, ffn2_up, ffn2_down):
    w = dict(mix_norm=mix_norm, w_in=w_in, a_q_norm=a_q_norm, a_k_norm=a_k_norm, b_gate_up=b_gate_up,
             b_gate_bias=b_gate_bias, b_out_norm=b_out_norm, c_dw_w=c_dw_w, c_dw_b=c_dw_b, c_norm=c_norm,
             d_qa_norm=d_qa_norm, d_uq=d_uq, d_kva_norm=d_kva_norm, d_ukv=d_ukv,
             d_q_norm=d_q_norm, d_k_norm=d_k_norm)
    bsz, seq, dm = x.shape
    depth = w_in.shape[0]
    blk = min(ATT_BLK, seq)
    bt = _bias_tiles(rel_bias, blk)
    bias_max = jnp.max(bt)
    p = {**jax.vmap(_one_layer_params)(w), **_shared_tables(seq)}
    stacked_row = lambda v: v[:, None, :].astype(F32)
    ffn1 = (stacked_row(ffn1_norm), ffn1_gate, ffn1_up, ffn1_down)
    ffn2 = (stacked_row(ffn2_norm), ffn2_gate, ffn2_up, ffn2_down)
    wo = w_out.astype(BF16)
    x2 = x.reshape(bsz * seq, dm)
    for l in range(depth):
        x2 = _ffn(x2, *ffn1, l)
        (aqT, akh, avT, iqT, aik, iwT, bq, bk, bv, bla, br, ch, dqT, dkh, dvT, knorm) = _mix_in(
            x2.reshape(bsz, seq, dm), p, l, blk=blk)
        kb_a, kb_d = _key_bounds(knorm, bias_max)
        y_a = _attention(aqT, akh, avT, kb_a, dsa=(iqT, iwT, aik, bt), blk=blk)
        y_b = _gla(bq, bk, bv, bla, br, p["gbo"], p["bd"], l)
        y_c = _conv(ch, p["cw"], p["cb"], p["cg"], l)
        y_d = _attention(dqT, dkh, dvT, kb_d, blk=blk)
        ys = [y.reshape(bsz * seq, GROUP_WIDTH) for y in (y_a, y_b, y_c, y_d)]
        x2 = _ffn(x2, *ffn2, l, mix=(ys, wo))
    return x2.reshape(bsz, seq, dm)
```

```python
import functools
import math

import jax
import jax.numpy as jnp
import numpy as np
from jax import lax
from jax.experimental import pallas as pl
from jax.experimental.pallas import tpu as pltpu

F32 = jnp.float32
BF16 = jnp.bfloat16
I32 = jnp.int32

EPS = 1e-6
GROUP_WIDTH = 256
A_HEADS, A_HEAD_DIM = 4, 64
IDX_HEADS, IDX_DIM = 8, 32
TOPK_MAX = 256
REL_BUCKETS, REL_MAX_DIST = 32, 128
B_HEADS, B_KEY_DIM, B_VAL_DIM, B_GATE_RANK = 4, 32, 64, 16
B_GATE_TAU = 16.0
B_CHUNK = 64
C_CHANNELS, C_KERNEL = 256, 31
D_HEADS, D_Q_RANK, D_KV_RANK, D_NOPE, D_ROPE, D_V = 4, 256, 128, 64, 32, 64
D_QK = D_NOPE + D_ROPE
ROPE_THETA = 10000.0

LANE = 128
SUBLANES = 8
HEAD_PAD = 128
ONES_ROWS = 16
ATT_BLK = 256
ATT_GROUP = 4
SCORE_GROUP = 4
BOUND_SLACK = 1.01
FIXED_REF_MAX = 40.0
INT_MIN = -2 ** 31
NEG = -1e30
LOG2E = math.log2(math.e)
VMEM_LIMIT = 56 * 1024 * 1024

N_AK, N_IK, N_BQ, N_BK, N_BV, N_BG, N_BR, N_CU, N_DCQ, N_DCKV, N_DKPE, N_TOT = (
    0, 256, 384, 512, 640, 896, 1024, 1280, 1792, 2048, 2176, 2304)
T_AQ, T_AV, T_IQ, T_IW, T_TOT = 0, 512, 768, 1024, 1040


def _dot(a, b):
    return jnp.dot(a, b, preferred_element_type=F32)


def _dot_nt(a, b):
    return lax.dot_general(a, b, (((1,), (1,)), ((), ())), preferred_element_type=F32)


def _dot_tn(a, b):
    return lax.dot_general(a, b, (((0,), (0,)), ((), ())), preferred_element_type=F32)


def _group_sum(x2, bd):
    hi = x2.astype(BF16)
    lo = (x2 - hi.astype(F32)).astype(BF16)
    return _dot(hi, bd) + _dot(lo, bd)


def _const_spec(shape, layer=None):
    nd = len(shape)
    if layer is None:
        return pl.BlockSpec(shape, lambda *_: (0,) * nd, pipeline_mode=pl.Buffered(1))
    return pl.BlockSpec((None,) + tuple(shape[1:]), lambda *_: (layer,) + (0,) * (nd - 1),
                        pipeline_mode=pl.Buffered(1))


def _cparams(sem):
    return pltpu.CompilerParams(dimension_semantics=sem, vmem_limit_bytes=VMEM_LIMIT)


def _ffn_body(has_mix, fc, *refs):
    if has_mix:
        x_ref, ya, yb, yc, yd, wo_ref, g_ref, wg_ref, wu_ref, wd_ref, o_ref, h_scr = refs
    else:
        x_ref, g_ref, wg_ref, wu_ref, wd_ref, o_ref, h_scr = refs
    x = x_ref[...]
    if has_mix:
        y = jnp.concatenate([ya[...], yb[...], yc[...], yd[...]], axis=-1)
        x = x + _dot(y, wo_ref[...])
    ms = jnp.mean(x * x, axis=-1, keepdims=True)
    xn = (x * lax.rsqrt(ms + EPS) * g_ref[...]).astype(BF16)
    d_ff = wg_ref.shape[1]
    for c in range(d_ff // fc):
        sl = slice(c * fc, (c + 1) * fc)
        gate = _dot(xn, wg_ref[:, sl].astype(BF16))
        up = _dot(xn, wu_ref[:, sl].astype(BF16))
        h_scr[:, sl] = (gate * jax.nn.sigmoid(gate) * up).astype(BF16)
    o_ref[...] = x + 0.5 * _dot(h_scr[...], wd_ref[...].astype(BF16))


def _ffn(x2, g, wg, wu, wd, layer, mix=None, tm=512, fc=256):
    m, d = x2.shape
    d_ff = wg.shape[2]
    tm = min(tm, m)
    row = lambda w: pl.BlockSpec((tm, w), lambda i: (i, 0))
    in_specs = [row(d)]
    args = [x2]
    if mix is not None:
        ys, wo = mix
        in_specs += [row(GROUP_WIDTH)] * 4 + [_const_spec(wo.shape, layer)]
        args += list(ys) + [wo]
    in_specs += [_const_spec(a.shape, layer) for a in (g, wg, wu, wd)]
    args += [g, wg, wu, wd]
    return pl.pallas_call(
        functools.partial(_ffn_body, mix is not None, fc),
        grid=(m // tm,),
        in_specs=in_specs,
        out_specs=row(d),
        out_shape=jax.ShapeDtypeStruct((m, d), F32),
        scratch_shapes=[pltpu.VMEM((tm, d_ff), BF16)],
        compiler_params=_cparams(("parallel",)),
        name="ffn_mix" if mix is not None else "ffn",
    )(*args)


def _mix_in_body(tm, blk,
                 x_ref, gmix_ref, wn_ref, wt_ref, bd_ref,
                 gaq_ref, gak_ref,
                 wgu_ref, bgb_ref,
                 gqa_ref, wuq_ref, gdq_ref, cosT_ref, sinT_ref,
                 gkva_ref, wuk_ref, wuvT_ref, gdk_ref, gdkpe_ref, cpe_ref, spe_ref,
                 aqT_ref, ak_ref, avT_ref, iqT_ref, ik_ref, iwT_ref,
                 bq_ref, bk_ref, bv_ref, bla_ref, br_ref,
                 ch_ref,
                 dqT_ref, dk_ref, dvT_ref, knorm_ref):
    nlt = tm // LANE
    x = x_ref[0]
    ms = jnp.mean(x * x, axis=-1, keepdims=True)
    xn = (x * lax.rsqrt(ms + EPS) * gmix_ref[...]).astype(BF16)
    bd = bd_ref[...]
    lane = lax.broadcasted_iota(I32, (tm, LANE), 1)

    def lanes(g):
        return jnp.tile(g, (1, nlt))

    z = _dot(xn, wn_ref[...])
    zt = _dot_nt(wt_ref[...], xn)

    def zs(off, width):
        return z[:, off:off + width]

    cq = zs(N_DCQ, D_Q_RANK)
    cq_ms = jnp.mean(cq * cq, axis=-1, keepdims=True)
    cqn = (cq * lax.rsqrt(cq_ms + EPS) * gqa_ref[...]).astype(BF16)
    ckv = zs(N_DCKV, D_KV_RANK)
    ckv_ms = jnp.mean(ckv * ckv, axis=-1, keepdims=True)
    ckvn = (ckv * lax.rsqrt(ckv_ms + EPS) * gkva_ref[...]).astype(BF16)
    dq = _dot_nt(wuq_ref[...], cqn).reshape(D_HEADS, HEAD_PAD, tm)
    kn = _dot(ckvn, wuk_ref[...])
    dv = _dot_nt(wuvT_ref[...], ckvn).astype(BF16)
    gate = _dot(zs(N_BG, LANE).astype(BF16), wgu_ref[...]) + bgb_ref[...]
    ak = zs(N_AK, GROUP_WIDTH)
    ak_ms = _group_sum(ak * ak, bd) * (1.0 / A_HEAD_DIM)
    kn_ss = _group_sum(kn * kn, bd)

    aq = zt[T_AQ:T_AQ + A_HEADS * HEAD_PAD].reshape(A_HEADS, HEAD_PAD, tm)
    aq_ms = jnp.sum(aq * aq, axis=1, keepdims=True) * (1.0 / A_HEAD_DIM)
    aq = aq * lax.rsqrt(aq_ms + EPS) * lanes(gaq_ref[...])[None] * (A_HEAD_DIM ** -0.5 * LOG2E)
    aqT_ref[0] = aq.reshape(A_HEADS * HEAD_PAD, tm).astype(BF16)
    def emit_key_norms(first_row, sq_norms):
        top = jnp.max(sq_norms, axis=0, keepdims=True)
        for h in range(4):
            knorm_ref[0, 0, first_row + h:first_row + h + 1, :] = jnp.broadcast_to(
                top[:, 64 * h:64 * h + 1], (1, LANE))

    def with_ones_rows(vt, heads, dim):
        ones = jnp.ones((ONES_ROWS, tm), BF16)
        return jnp.concatenate([r for h in range(heads) for r in (vt[dim * h:dim * (h + 1)], ones)], axis=0)

    av = with_ones_rows(zt[T_AV:T_AV + GROUP_WIDTH].astype(BF16), A_HEADS, A_HEAD_DIM)
    for c in range(tm // blk):
        avT_ref[0, c] = av[:, c * blk:(c + 1) * blk]
    iqT_ref[0] = zt[T_IQ:T_IQ + IDX_HEADS * IDX_DIM].astype(BF16)
    iwT_ref[0] = zt[T_IW:T_IW + IDX_HEADS] * ((IDX_HEADS ** -0.5) * (IDX_DIM ** -0.5))

    ak = ak * lax.rsqrt(ak_ms + EPS) * gak_ref[...]
    for h in range(A_HEADS):
        pair = ak[:, LANE * (h // 2):LANE * (h // 2) + LANE]
        if h % 2 == 1:
            pair = pltpu.roll(pair, 64, 1)
        kh = jnp.where(lane < A_HEAD_DIM, pair, 0.0).astype(BF16)
        ak_ref[0, h] = kh
    emit_key_norms(0, _dot((ak * ak).astype(BF16), bd))
    ik_ref[0] = zs(N_IK, LANE)[:, :IDX_DIM].astype(BF16)

    bq_ref[0] = zs(N_BQ, LANE) * (B_KEY_DIM ** -0.5)
    bk_ref[0] = zs(N_BK, LANE)
    bv_ref[0] = zs(N_BV, GROUP_WIDTH).astype(BF16)
    bla_ref[0] = (jnp.minimum(gate, 0.0) - jnp.log(1.0 + jnp.exp(-jnp.abs(gate)))) * (1.0 / B_GATE_TAU)
    br_ref[0] = zs(N_BR, GROUP_WIDTH)

    ca = zs(N_CU, C_CHANNELS)
    cg = zs(N_CU + C_CHANNELS, C_CHANNELS)
    ch_ref[0] = ca * jax.nn.sigmoid(cg)

    dq_ms = jnp.sum(dq * dq, axis=1, keepdims=True) * (1.0 / D_QK)
    dq = dq * lax.rsqrt(dq_ms + EPS) * lanes(gdq_ref[...])[None] * (D_QK ** -0.5 * LOG2E)
    half = D_ROPE // 2
    x1 = dq[:, D_NOPE:D_NOPE + half]
    x2 = dq[:, D_NOPE + half:D_QK]
    cs = cosT_ref[...][None]
    sn = sinT_ref[...][None]
    dq = jnp.concatenate([dq[:, :D_NOPE], x1 * cs - x2 * sn, x2 * cs + x1 * sn, dq[:, D_QK:]], axis=1)
    dqT_ref[0] = dq.reshape(D_HEADS * HEAD_PAD, tm).astype(BF16)

    dv = with_ones_rows(dv, D_HEADS, D_V)
    for c in range(tm // blk):
        dvT_ref[0, c] = dv[:, c * blk:(c + 1) * blk]
    kpe = zs(N_DKPE, LANE)
    ss = kn_ss + jnp.sum(kpe * kpe, axis=-1, keepdims=True)
    rinv = lax.rsqrt(ss * (1.0 / D_QK) + EPS)
    kn = kn * rinv * gdk_ref[...]
    pe = kpe * gdkpe_ref[...]
    partner = jnp.where(lane < half, pltpu.roll(pe, LANE - half, 1), pltpu.roll(pe, half, 1))
    pe = pe * cpe_ref[...] + partner * spe_ref[...]
    pe = pltpu.roll(pe, D_NOPE, 1)
    for h in range(D_HEADS):
        pair = kn[:, LANE * (h // 2):LANE * (h // 2) + LANE]
        rpair = rinv[:, LANE * (h // 2):LANE * (h // 2) + LANE]
        if h % 2 == 1:
            pair = pltpu.roll(pair, 64, 1)
        else:
            rpair = pltpu.roll(rpair, 64, 1)
        kh = jnp.where(lane < D_NOPE, pair, pe * rpair).astype(BF16)
        dk_ref[0, h] = kh
    pe_sq = _dot((pe * pe).astype(BF16), jnp.ones((LANE, LANE), BF16))
    emit_key_norms(A_HEADS, _dot((kn * kn).astype(BF16), bd) + jnp.tile(pe_sq, (1, 2)) * (rinv * rinv))


def _mix_in(x3, p, layer, tm=512, blk=ATT_BLK):
    b, l, d = x3.shape
    tm = min(tm, l)
    grid = (b, l // tm)
    nck = l // blk
    tok = lambda w: pl.BlockSpec((1, tm, w), lambda bi, i: (bi, i, 0))
    tokT = lambda r: pl.BlockSpec((1, r, tm), lambda bi, i: (bi, 0, i))
    headk = pl.BlockSpec((1, 4, tm, HEAD_PAD), lambda bi, i: (bi, 0, i, 0))
    vrows = GROUP_WIDTH + 4 * ONES_ROWS
    chunkT = pl.BlockSpec((1, tm // blk, vrows, blk), lambda bi, i: (bi, i, 0, 0))
    postab = lambda r: pl.BlockSpec((r, tm), lambda bi, i: (0, i))
    posrow = pl.BlockSpec((tm, LANE), lambda bi, i: (i, 0))
    consts = [p["gmix"], p["wn"], p["wt"], p["bd"], p["gaq"], p["gak"], p["wgu"], p["bgb"],
              p["gqa"], p["wuq"], p["gdq"]]
    consts2 = [p["gkva"], p["wuk"], p["wuvT"], p["gdk"], p["gdkpe"]]
    lspec = lambda a: _const_spec(a.shape) if a is p["bd"] else _const_spec(a.shape, layer)
    in_specs = ([tok(d)] + [lspec(a) for a in consts]
                + [postab(D_ROPE // 2), postab(D_ROPE // 2)]
                + [lspec(a) for a in consts2] + [posrow, posrow])
    args = [x3] + consts + [p["cosT"], p["sinT"]] + consts2 + [p["cpe"], p["spe"]]
    sd = jax.ShapeDtypeStruct
    out_shape = [
        sd((b, A_HEADS * HEAD_PAD, l), BF16), sd((b, A_HEADS, l, HEAD_PAD), BF16),
        sd((b, nck, vrows, blk), BF16), sd((b, IDX_HEADS * IDX_DIM, l), BF16),
        sd((b, l, IDX_DIM), BF16), sd((b, IDX_HEADS, l), F32),
        sd((b, l, LANE), F32), sd((b, l, LANE), F32), sd((b, l, GROUP_WIDTH), BF16),
        sd((b, l, LANE), F32), sd((b, l, GROUP_WIDTH), F32),
        sd((b, l, C_CHANNELS), F32),
        sd((b, D_HEADS * HEAD_PAD, l), BF16), sd((b, D_HEADS, l, HEAD_PAD), BF16),
        sd((b, nck, vrows, blk), BF16),
        sd((b, l // tm, A_HEADS + D_HEADS, LANE), F32),
    ]
    out_specs = [
        tokT(A_HEADS * HEAD_PAD), headk, chunkT, tokT(IDX_HEADS * IDX_DIM),
        tok(IDX_DIM), tokT(IDX_HEADS),
        tok(LANE), tok(LANE), tok(GROUP_WIDTH), tok(LANE), tok(GROUP_WIDTH),
        tok(C_CHANNELS),
        tokT(D_HEADS * HEAD_PAD), headk, chunkT,
        pl.BlockSpec((1, 1, A_HEADS + D_HEADS, LANE), lambda bi, i: (bi, i, 0, 0)),
    ]
    return pl.pallas_call(
        functools.partial(_mix_in_body, tm, blk),
        grid=grid, in_specs=in_specs, out_specs=out_specs, out_shape=out_shape,
        compiler_params=_cparams(("parallel", "parallel")),
        name="mix_in",
    )(*args)


def _attn_body(nh, dv, blk, topk, is_dsa, *refs):
    if is_dsa:
        (kb_ref, qT_ref, k_ref, vT_ref, iqT_ref, wT_ref, ik_ref, bt_ref,
         o_ref, m_scr, acc_scr, key_scr, run_scr, plane_scr, active_scr) = refs
    else:
        kb_ref, qT_ref, k_ref, vT_ref, o_ref, m_scr, acc_scr = refs
    vr = dv + ONES_ROWS
    i = pl.program_id(1)
    t = blk
    row = lax.broadcasted_iota(I32, (t, t), 0)
    col = lax.broadcasted_iota(I32, (t, t), 1)
    causal_pen = jnp.where(row <= col, 0.0, NEG)

    m_scr[...] = jnp.full(m_scr.shape, NEG, F32)
    acc_scr[...] = jnp.zeros(acc_scr.shape, F32)

    if is_dsa:
        @pl.when(i == 0)
        def _():
            plane_scr[...] = jnp.zeros(plane_scr.shape, I32)

        def score_chunk(j, diag):
            r0 = pl.multiple_of(j * t, t)
            ikc = ik_ref[0, pl.ds(r0, t), :]
            s = jnp.zeros((t, t), F32)
            for h in range(IDX_HEADS):
                d = _dot(ikc, iqT_ref[0, IDX_DIM * h:IDX_DIM * (h + 1), :])
                s = s + jnp.maximum(d, 0.0) * wT_ref[0, h:h + 1, :]
            bits = lax.bitcast_convert_type(s, I32)
            key = jnp.where(bits < 0, bits ^ 0x7FFFFFFF, bits)
            if diag:
                key = jnp.where(row <= col, key, INT_MIN)
            key_scr[pl.ds(r0, t), :] = key
            w = [key[SUBLANES * r:SUBLANES * (r + 1), :] for r in range(32)]
            step, mask = 16, 0x0000FFFF
            while step:
                for lo in range(32):
                    if lo & step == 0:
                        hi = lo + step
                        swap = (w[lo] ^ jnp.right_shift(w[hi], step)) & mask
                        w[lo] = w[lo] ^ swap
                        w[hi] = w[hi] ^ jnp.left_shift(swap, step)
                step //= 2
                mask ^= (mask << step) & 0xFFFFFFFF
            c0 = pl.multiple_of(j * SUBLANES, SUBLANES)
            w[0] = ~w[0]
            for p in range(32):
                plane_scr[p, pl.ds(c0, SUBLANES), :] = w[p]

        def score_group(u, carry):
            for c in range(SCORE_GROUP):
                score_chunk(SCORE_GROUP * u + c, False)
            return carry

        lax.fori_loop(0, i // SCORE_GROUP, score_group, 0)
        for rem in range(SCORE_GROUP):
            @pl.when(i % SCORE_GROUP == rem)
            def _():
                for c in range(rem):
                    score_chunk(i - rem + c, False)
                score_chunk(i, True)

        def kth_largest(nrow):
            rows = pl.ds(0, nrow)
            in_range = lax.broadcasted_iota(I32, (nrow, t), 0) < (i + 1) * SUBLANES

            def col_count(words):
                pc = lax.population_count(words).reshape(nrow // SUBLANES, SUBLANES, t)
                return jnp.sum(jnp.sum(pc, axis=0), axis=0, keepdims=True)

            def decide(plane, active, n_gt, ans_u):
                ones = col_count(active & plane_scr[plane, rows, :])
                take = n_gt + ones >= topk
                bit = lax.shift_right_logical(jnp.int32(INT_MIN), jnp.int32(plane))
                return (jnp.where(take, 0, -1), n_gt + jnp.where(take, 0, ones),
                        ans_u | jnp.where(take, bit, 0))

            active_scr[rows, :] = jnp.where(in_range, -1, 0)
            state = decide(0, active_scr[rows, :], jnp.zeros((1, t), I32), jnp.zeros((1, t), I32))

            def plane_body(plane, state):
                flip, n_gt, ans_u = state
                active = active_scr[rows, :] & (plane_scr[plane - 1, rows, :] ^ flip)
                active_scr[rows, :] = active
                return decide(plane, active, n_gt, ans_u)

            _, n_gt, ans_u = lax.fori_loop(1, 32, plane_body, state)
            return n_gt, ans_u

        nrow_all = plane_scr.shape[1]
        nrow_half = (nrow_all // SUBLANES // 2) * SUBLANES
        if nrow_half:
            n_gt, ans_u = lax.cond((i + 1) * SUBLANES <= nrow_half,
                                   functools.partial(kth_largest, nrow_half),
                                   functools.partial(kth_largest, nrow_all))
        else:
            n_gt, ans_u = kth_largest(nrow_all)
        ans = ans_u ^ INT_MIN
        need = (topk - n_gt).astype(F32)
        run_scr[...] = jnp.zeros(run_scr.shape, F32)
        stri = jnp.where(col < row, 1.0, 0.0).astype(BF16)

    def logits(h, r0):
        return _dot(k_ref[0, h, pl.ds(r0, t), :], qT_ref[0, HEAD_PAD * h:HEAD_PAD * (h + 1), :])

    def selection_pens(r0s):
        pens = []
        if is_dsa:
            for r0 in r0s:
                kc = key_scr[pl.ds(r0, t), :]
                eq = kc == ans
                eqf = jnp.where(eq, 1.0, 0.0)
                run = run_scr[0:1, :]
                rank = _dot(stri, eqf.astype(BF16)) + run
                run_scr[0:1, :] = run + jnp.sum(eqf, axis=0, keepdims=True)
                pens.append(jnp.where(kc > ans, 0.0, jnp.where(eq, jnp.where(rank < need, 0.0, NEG), NEG)))
        return pens

    b_idx = pl.program_id(0)
    bounds = []
    for h in range(nh):
        qh = qT_ref[0, HEAD_PAD * h:HEAD_PAD * (h + 1), :].astype(F32)
        bound = jnp.sqrt(jnp.sum(qh * qh, axis=0, keepdims=True)) * (kb_ref[b_idx * nh + h] * BOUND_SLACK)
        if is_dsa:
            bound = bound + kb_ref[kb_ref.shape[0] - 1]
        bounds.append(bound)
    bound_max = jnp.max(functools.reduce(jnp.maximum, bounds))
    fixed_ref_ok = bound_max <= FIXED_REF_MAX

    def fixed_ref_chunks(js, kinds):
        n = len(js)
        r0s = [pl.multiple_of(j * t, t) for j in js]
        lgs = [[logits(h, r0s[c]) for c in range(n)] for h in range(nh)]
        pens = selection_pens(r0s)
        ps = []
        for h in range(nh):
            row_ps = []
            if is_dsa:
                far_ref = bounds[h] - bt_ref[2, h, 0:1, :]
            for c in range(n):
                x, ref = lgs[h][c], bounds[h]
                if is_dsa and kinds[c] == "far":
                    x, ref = x + pens[c], far_ref
                elif is_dsa:
                    x = x + (pens[c] + bt_ref[0 if kinds[c] == "diag" else 1, h])
                elif kinds[c] == "diag":
                    x = x + causal_pen
                row_ps.append(jnp.exp2(x - ref).astype(BF16))
            ps.append(row_ps[0] if n == 1 else jnp.concatenate(row_ps, axis=0))
        for h in range(nh):
            vs = slice(vr * h, vr * (h + 1))
            vt = [vT_ref[0, js[c], vs, :] for c in range(n)]
            acc_scr[vs, :] += _dot(vt[0] if n == 1 else jnp.concatenate(vt, axis=1), ps[h])

    def super_chunk(js, kinds):
        n = len(js)
        r0s = [pl.multiple_of(j * t, t) for j in js]
        lgs = [[logits(h, r0s[c]) for c in range(n)] for h in range(nh)]
        pens = selection_pens(r0s)
        ps, alphas = [], []
        for h in range(nh):
            xs = []
            for c, j in enumerate(js):
                diag = kinds[c] == "diag"
                lg = lgs[h][c]
                if is_dsa:
                    tile = 0 if diag else jnp.minimum(i - j, 2)
                    lg = lg + (pens[c] + bt_ref[tile, h])
                elif diag:
                    lg = lg + causal_pen
                xs.append(lg)
            m_old = m_scr[h, 0:1, :]
            m_new = m_old
            for x in xs:
                m_new = jnp.maximum(m_new, jnp.max(x, axis=0, keepdims=True))
            alpha = jnp.exp2(m_old - m_new)
            m_scr[h, 0:1, :] = m_new
            ps.append([jnp.exp2((x - m_new).astype(BF16)) for x in xs])
            alphas.append(alpha)
        for h in range(nh):
            vs = slice(vr * h, vr * (h + 1))
            pv = _dot(vT_ref[0, js[0], vs, :], ps[h][0])
            for c in range(1, n):
                pv = pv + _dot(vT_ref[0, js[c], vs, :], ps[h][c])
            acc_scr[vs, :] = alphas[h] * acc_scr[vs, :] + pv

    tail = ["near", "diag"] if is_dsa else ["diag"]
    nfar = jnp.maximum(i + 1 - len(tail), 0)

    def run_chunks(step):
        def group_body(u, carry):
            step([ATT_GROUP * u + c for c in range(ATT_GROUP)], ["far"] * ATT_GROUP)
            return carry

        lax.fori_loop(0, nfar // ATT_GROUP, group_body, 0)
        for rem in range(ATT_GROUP):
            @pl.when(jnp.logical_and(i + 1 >= len(tail), nfar % ATT_GROUP == rem))
            def _():
                first = i + 1 - len(tail) - rem
                step([first + c for c in range(rem + len(tail))], ["far"] * rem + tail)

        if is_dsa:
            @pl.when(i == 0)
            def _():
                step([i], ["diag"])

    pl.when(fixed_ref_ok)(functools.partial(run_chunks, fixed_ref_chunks))
    pl.when(jnp.logical_not(fixed_ref_ok))(functools.partial(run_chunks, super_chunk))

    outs = []
    for h in range(nh):
        outs.append(acc_scr[vr * h:vr * h + dv, :] / acc_scr[vr * h + dv:vr * h + dv + 1, :])
    o_ref[0] = jnp.transpose(jnp.concatenate(outs, axis=0)).astype(o_ref.dtype)


def _attention(qT, k, vT, kbound, dsa=None, blk=ATT_BLK):
    b, nh, l, _ = k.shape
    vr = vT.shape[2] // nh
    dv = vr - ONES_ROWS
    grid = (b, l // blk)
    qspec = lambda r: pl.BlockSpec((1, r, blk), lambda bi, i: (bi, 0, i))
    kspec = pl.BlockSpec((1, nh, l, HEAD_PAD), lambda bi, i: (bi, 0, 0, 0))
    vspec = pl.BlockSpec((1, l // blk, nh * vr, blk), lambda bi, i: (bi, 0, 0, 0))
    in_specs = [pl.BlockSpec(memory_space=pltpu.SMEM), qspec(nh * HEAD_PAD), kspec, vspec]
    args = [kbound, qT, k, vT]
    scratch = [pltpu.VMEM((nh, 8, blk), F32), pltpu.VMEM((nh * vr, blk), F32)]
    topk = 0
    if dsa is not None:
        iqT, wT, ik, bt = dsa
        topk = min(TOPK_MAX, l // 4)
        in_specs += [qspec(IDX_HEADS * IDX_DIM), qspec(IDX_HEADS),
                     pl.BlockSpec((1, l, IDX_DIM), lambda bi, i: (bi, 0, 0)),
                     _const_spec(bt.shape)]
        args += [iqT, wT, ik, bt]
        assert blk == 32 * SUBLANES
        nrow = (l // blk) * SUBLANES
        scratch += [pltpu.VMEM((l, blk), I32), pltpu.VMEM((8, blk), F32),
                    pltpu.VMEM((32, nrow, blk), I32), pltpu.VMEM((nrow, blk), I32)]
    return pl.pallas_call(
        functools.partial(_attn_body, nh, dv, blk, topk, dsa is not None),
        grid=grid, in_specs=in_specs,
        out_specs=pl.BlockSpec((1, blk, nh * dv), lambda bi, i: (bi, i, 0)),
        out_shape=jax.ShapeDtypeStruct((b, l, nh * dv), BF16),
        scratch_shapes=scratch,
        compiler_params=_cparams(("parallel", "arbitrary")),
        name="dsa_attn" if dsa is not None else "mla_attn",
    )(*args)


def _gla_body(tg, q_ref, k_ref, v_ref, la_ref, r_ref, go_ref, bd_ref, o_ref, st_scr, o_scr):
    @pl.when(pl.program_id(1) == 0)
    def _():
        st_scr[...] = jnp.zeros(st_scr.shape, F32)

    cs = B_CHUNK
    la = la_ref[0]
    rl = lax.broadcasted_iota(I32, (tg, LANE), 0) & (cs - 1)
    b = la
    s = 1
    while s < cs:
        b = b + jnp.where(rl >= s, pltpu.roll(b, s, 0), 0.0)
        s *= 2
    q = q_ref[0]
    k = k_ref[0]
    qb = q * jnp.exp(b)
    tri = lax.broadcasted_iota(I32, (cs, cs), 1) <= lax.broadcasted_iota(I32, (cs, cs), 0)
    khead = lax.broadcasted_iota(I32, (cs, LANE), 1) // B_KEY_DIM
    vhead = lax.broadcasted_iota(I32, (cs, GROUP_WIDTH), 1) // B_VAL_DIM
    same_head = (lax.broadcasted_iota(I32, (GROUP_WIDTH, LANE), 0) // B_VAL_DIM
                 == lax.broadcasted_iota(I32, (GROUP_WIDTH, LANE), 1) // B_KEY_DIM)
    for c in range(tg // cs):
        sl = slice(c * cs, (c + 1) * cs)
        bc = b[sl]
        mid = bc[cs // 2:cs // 2 + 1]
        last = bc[cs - 1:cs]
        qe = q[sl] * jnp.exp(bc - mid)
        ke = (k[sl] * jnp.exp(mid - bc)).astype(BF16)
        kd = (k[sl] * jnp.exp(last - bc)).astype(BF16)
        qbc = qb[sl].astype(BF16)
        vc = v_ref[0, sl, :]
        st = st_scr[...]
        qe4 = jnp.concatenate([jnp.where(khead == h, qe, 0.0) for h in range(B_HEADS)], axis=0)
        a_all = _dot_nt(qe4.astype(BF16), ke)
        o = _dot_nt(qbc, st.astype(BF16))
        for h in range(B_HEADS):
            a = jnp.where(tri, a_all[cs * h:cs * (h + 1)], 0.0).astype(BF16)
            o = o + _dot(a, jnp.where(vhead == h, vc, jnp.zeros_like(vc)))
        o_scr[sl, :] = o
        st_scr[...] = st * jnp.exp(last) + jnp.where(same_head, _dot_tn(vc, kd), 0.0)
    o = o_scr[...]
    ms = _group_sum(o * o, bd_ref[...]) * (1.0 / B_VAL_DIM)
    r = r_ref[0]
    o_ref[0] = (o * lax.rsqrt(ms + EPS) * go_ref[...] * (r * jax.nn.sigmoid(r))).astype(o_ref.dtype)


def _gla(bq, bk, bv, bla, br, go, bd, layer, tg=1024):
    b, l, _ = bq.shape
    tg = min(tg, l)
    tok = lambda w: pl.BlockSpec((1, tg, w), lambda bi, i: (bi, i, 0))
    return pl.pallas_call(
        functools.partial(_gla_body, tg),
        grid=(b, l // tg),
        in_specs=[tok(LANE), tok(LANE), tok(GROUP_WIDTH), tok(LANE), tok(GROUP_WIDTH),
                  _const_spec(go.shape, layer), _const_spec(bd.shape)],
        out_specs=tok(GROUP_WIDTH),
        out_shape=jax.ShapeDtypeStruct((b, l, GROUP_WIDTH), BF16),
        scratch_shapes=[pltpu.VMEM((B_HEADS * B_VAL_DIM, B_HEADS * B_KEY_DIM), F32),
                        pltpu.VMEM((tg, GROUP_WIDTH), F32)],
        compiler_params=_cparams(("parallel", "arbitrary")),
        name="gla",
    )(bq, bk, bv, bla, br, go, bd)


CONV_HIST = 32


def _conv_body(tc, h_ref, w_ref, b_ref, g_ref, o_ref, buf):
    @pl.when(pl.program_id(1) == 0)
    def _():
        buf[0:CONV_HIST, :] = jnp.zeros((CONV_HIST, C_CHANNELS), F32)

    @pl.when(pl.program_id(1) > 0)
    def _():
        buf[0:CONV_HIST, :] = buf[tc:tc + CONV_HIST, :]

    buf[CONV_HIST:CONV_HIST + tc, :] = h_ref[0]
    acc = jnp.zeros((tc, C_CHANNELS), F32) + b_ref[...]
    base = CONV_HIST - (C_KERNEL - 1)
    hb = buf[...]
    rows = tc + CONV_HIST
    for r in range(SUBLANES):
        shifted = hb if r == 0 else pltpu.roll(hb, rows - r, 0)
        for j in range(C_KERNEL):
            if (base + j) % SUBLANES == r:
                a0 = base + j - r
                acc = acc + shifted[a0:a0 + tc, :] * w_ref[j:j + 1, :]
    ms = jnp.mean(acc * acc, axis=-1, keepdims=True)
    y = acc * lax.rsqrt(ms + EPS) * g_ref[...]
    o_ref[0] = (y * jax.nn.sigmoid(y)).astype(o_ref.dtype)


def _conv(ch, w, bias, g, layer, tc=512):
    b, l, c = ch.shape
    tc = min(tc, l)
    tok = pl.BlockSpec((1, tc, c), lambda bi, i: (bi, i, 0))
    return pl.pallas_call(
        functools.partial(_conv_body, tc),
        grid=(b, l // tc),
        in_specs=[tok] + [_const_spec(a.shape, layer) for a in (w, bias, g)],
        out_specs=tok,
        out_shape=jax.ShapeDtypeStruct((b, l, c), BF16),
        scratch_shapes=[pltpu.VMEM((tc + CONV_HIST, c), F32)],
        compiler_params=_cparams(("parallel", "arbitrary")),
        name="conv",
    )(ch, w, bias, g)


def _t5_bucket(dist):
    max_exact = REL_BUCKETS // 2
    d = jnp.maximum(dist, 0)
    df = jnp.maximum(d, 1).astype(F32)
    large = max_exact + (jnp.log(df / max_exact) / math.log(REL_MAX_DIST / max_exact)
                         * (REL_BUCKETS - max_exact)).astype(I32)
    large = jnp.minimum(large, REL_BUCKETS - 1)
    return jnp.where(d < max_exact, d, large)


def _pad_cols(w, width):
    return jnp.pad(w, ((0, 0), (0, width - w.shape[1])))


def _lane_rep(v):
    return jnp.broadcast_to(v[:, None], (v.shape[0], LANE))


def _pad_heads_rows(w, heads, dim):
    w = w.reshape(heads, dim, w.shape[1])
    return jnp.pad(w, ((0, 0), (0, HEAD_PAD - dim), (0, 0))).reshape(heads * HEAD_PAD, -1)


def _split_w_in(w_in):
    widths = (256, 256, 256, 256, 32, 8, 128, 128, 256, 16, 256, 512, 256, 128, 32)
    offs = np.cumsum((0,) + widths)
    return [w_in[:, offs[n]:offs[n + 1]] for n in range(len(widths))]


def _row(v):
    return v[None, :].astype(F32)


def _one_layer_params(w):
    (aq, ak, av, iq, ik, iw, bq, bk, bv, bg, br, cu, dcq, dckv, dkpe) = _split_w_in(w["w_in"])
    wn = jnp.concatenate([ak, _pad_cols(ik, LANE), bq, bk, bv, _pad_cols(bg, LANE), br, cu, dcq, dckv,
                          _pad_cols(dkpe, LANE)], axis=1).astype(BF16)
    wt = jnp.concatenate([_pad_heads_rows(aq.T, A_HEADS, A_HEAD_DIM), av.T, iq.T,
                          jnp.pad(iw.T, ((0, T_TOT - T_IW - IDX_HEADS), (0, 0)))], axis=0).astype(BF16)
    pad_to = lambda v, n: jnp.pad(v, (0, n - v.shape[0]))
    ukv = w["d_ukv"].reshape(D_KV_RANK, D_HEADS, D_NOPE + D_V)
    wuk = ukv[:, :, :D_NOPE].reshape(D_KV_RANK, D_HEADS * D_NOPE)
    wuv = ukv[:, :, D_NOPE:].reshape(D_KV_RANK, D_HEADS * D_V)
    gdk = w["d_k_norm"]
    return dict(
        gmix=_row(w["mix_norm"]), wn=wn, wt=wt,
        gaq=_lane_rep(pad_to(w["a_q_norm"], HEAD_PAD)),
        gak=_row(jnp.tile(w["a_k_norm"], A_HEADS)),
        wgu=jnp.pad(w["b_gate_up"], ((0, LANE - B_GATE_RANK), (0, 0))).astype(BF16),
        bgb=_row(w["b_gate_bias"]),
        gqa=_row(w["d_qa_norm"]),
        wuq=_pad_heads_rows(w["d_uq"].T, D_HEADS, D_QK).astype(BF16),
        gdq=_lane_rep(pad_to(w["d_q_norm"], HEAD_PAD)),
        gkva=_row(w["d_kva_norm"]), wuk=wuk.astype(BF16), wuvT=wuv.T.astype(BF16),
        gdk=_row(jnp.tile(gdk[:D_NOPE], D_HEADS)), gdkpe=_row(pad_to(gdk[D_NOPE:], LANE)),
        gbo=_row(jnp.tile(w["b_out_norm"], B_HEADS)),
        cw=jnp.pad(w["c_dw_w"][:, 0, :], ((0, CONV_HIST - C_KERNEL), (0, 0))).astype(F32),
        cb=_row(w["c_dw_b"]), cg=_row(w["c_norm"]),
    )


def _shared_tables(seq):
    hid = np.arange(GROUP_WIDTH) // 64
    bd = jnp.asarray(hid[:, None] == hid[None, :], dtype=BF16)
    half = D_ROPE // 2
    freqs = ROPE_THETA ** (-jnp.arange(half, dtype=F32) / half)
    ang = jnp.arange(seq).astype(F32)[:, None] * freqs[None, :]
    cos, sin = jnp.cos(ang), jnp.sin(ang)
    zeros = jnp.zeros((seq, LANE - D_ROPE), F32)
    cpe = jnp.concatenate([cos, cos, zeros], axis=1)
    spe = jnp.concatenate([-sin, sin, zeros], axis=1)
    return dict(bd=bd, cosT=cos.T, sinT=sin.T, cpe=cpe, spe=spe)


def _bias_tiles(rel_bias, blk):
    assert REL_MAX_DIST <= blk + 1
    kk = jnp.arange(blk)[:, None]
    qq = jnp.arange(blk)[None, :]
    rb = rel_bias.astype(F32).T

    def lookup(bucket):
        onehot = bucket[None, :, :, None] == jnp.arange(REL_BUCKETS)
        return jnp.sum(jnp.where(onehot, rb[:, None, None, :], 0.0), axis=-1) * LOG2E

    d0 = jnp.where(kk <= qq, lookup(_t5_bucket(qq - kk)), NEG)
    d1 = lookup(_t5_bucket(blk + qq - kk))
    far = jnp.broadcast_to(lookup(_t5_bucket(jnp.full((1, 1), 2 * blk, I32))), d1.shape)
    return jnp.stack([d0, d1, far])


def _key_bounds(knorm, bias_max):
    km = jnp.sqrt(jnp.max(knorm[..., 0], axis=1))
    kb_a = jnp.concatenate([km[:, :A_HEADS].reshape(-1), bias_max.reshape(1)])
    return kb_a, km[:, A_HEADS:].reshape(-1)


def kernel(x, ffn1_norm, ffn1_gate, ffn1_up, ffn1_down, mix_norm, w_in, a_q_norm, a_k_norm, rel_bias,
           b_gate_up, b_gate_bias, b_out_norm, c_dw_w, c_dw_b, c_norm, d_qa_norm, d_uq, d_kva_norm,
           d_ukv, d_q_norm, d_k_norm, w_out, ffn2_norm, ffn2_gate, ffn2_up, ffn2_down):
    w = dict(mix_norm=mix_norm, w_in=w_in, a_q_norm=a_q_norm, a_k_norm=a_k_norm, b_gate_up=b_gate_up,
             b_gate_bias=b_gate_bias, b_out_norm=b_out_norm, c_dw_w=c_dw_w, c_dw_b=c_dw_b, c_norm=c_norm,
             d_qa_norm=d_qa_norm, d_uq=d_uq, d_kva_norm=d_kva_norm, d_ukv=d_ukv,
             d_q_norm=d_q_norm, d_k_norm=d_k_norm)
    bsz, seq, dm = x.shape
    depth = w_in.shape[0]
    blk = min(ATT_BLK, seq)
    bt = _bias_tiles(rel_bias, blk)
    bias_max = jnp.max(bt)
    p = {**jax.vmap(_one_layer_params)(w), **_shared_tables(seq)}
    stacked_row = lambda v: v[:, None, :].astype(F32)
    ffn1 = (stacked_row(ffn1_norm), ffn1_gate, ffn1_up, ffn1_down)
    ffn2 = (stacked_row(ffn2_norm), ffn2_gate, ffn2_up, ffn2_down)
    wo = w_out.astype(BF16)
    x2 = x.reshape(bsz * seq, dm)
    for l in range(depth):
        x2 = _ffn(x2, *ffn1, l)
        (aqT, akh, avT, iqT, aik, iwT, bq, bk, bv, bla, br, ch, dqT, dkh, dvT, knorm) = _mix_in(
            x2.reshape(bsz, seq, dm), p, l, blk=blk)
        kb_a, kb_d = _key_bounds(knorm, bias_max)
        y_a = _attention(aqT, akh, avT, kb_a, dsa=(iqT, iwT, aik, bt), blk=blk)
        y_b = _gla(bq, bk, bv, bla, br, p["gbo"], p["bd"], l)
        y_c = _conv(ch, p["cw"], p["cb"], p["cg"], l)
        y_d = _attention(dqT, dkh, dvT, kb_d, blk=blk)
        ys = [y.reshape(bsz * seq, GROUP_WIDTH) for y in (y_a, y_b, y_c, y_d)]
        x2 = _ffn(x2, *ffn2, l, mix=(ys, wo))
    return x2.reshape(bsz, seq, dm)
```

```python
import functools
import math

import jax
import jax.numpy as jnp
import numpy as np
from jax import lax
from jax.experimental import pallas as pl
from jax.experimental.pallas import tpu as pltpu

F32 = jnp.float32
BF16 = jnp.bfloat16
I32 = jnp.int32

EPS = 1e-6
GROUP_WIDTH = 256
A_HEADS, A_HEAD_DIM = 4, 64
IDX_HEADS, IDX_DIM = 8, 32
TOPK_MAX = 256
REL_BUCKETS, REL_MAX_DIST = 32, 128
B_HEADS, B_KEY_DIM, B_VAL_DIM, B_GATE_RANK = 4, 32, 64, 16
B_GATE_TAU = 16.0
B_CHUNK = 64
C_CHANNELS, C_KERNEL = 256, 31
D_HEADS, D_Q_RANK, D_KV_RANK, D_NOPE, D_ROPE, D_V = 4, 256, 128, 64, 32, 64
D_QK = D_NOPE + D_ROPE
ROPE_THETA = 10000.0

LANE = 128
SUBLANES = 8
HEAD_PAD = 128
ONES_ROWS = 16
ATT_BLK = 256
DSA_GROUP = 4
MLA_GROUP = 8
SCORE_GROUP = 8
BOUND_SLACK = 1.01
FIXED_REF_MAX = 40.0
INT_MIN = -2 ** 31
NEG = -1e30
LOG2E = math.log2(math.e)
VMEM_LIMIT = 56 * 1024 * 1024

N_AK, N_IK, N_BQ, N_BK, N_BV, N_BG, N_BR, N_CU, N_DCQ, N_DCKV, N_DKPE, N_TOT = (
    0, 256, 384, 512, 640, 896, 1024, 1280, 1792, 2048, 2176, 2304)
T_AQ, T_AV, T_IQ, T_IW, T_TOT = 0, 512, 768, 1024, 1040


def _dot(a, b):
    return jnp.dot(a, b, preferred_element_type=F32)


def _dot_nt(a, b):
    return lax.dot_general(a, b, (((1,), (1,)), ((), ())), preferred_element_type=F32)


def _dot_tn(a, b):
    return lax.dot_general(a, b, (((0,), (0,)), ((), ())), preferred_element_type=F32)


def _group_sum(x2, bd):
    hi = x2.astype(BF16)
    lo = (x2 - hi.astype(F32)).astype(BF16)
    return _dot(hi, bd) + _dot(lo, bd)


def _const_spec(shape, layer=None):
    nd = len(shape)
    if layer is None:
        return pl.BlockSpec(shape, lambda *_: (0,) * nd, pipeline_mode=pl.Buffered(1))
    return pl.BlockSpec((None,) + tuple(shape[1:]), lambda *_: (layer,) + (0,) * (nd - 1),
                        pipeline_mode=pl.Buffered(1))


def _cparams(sem):
    return pltpu.CompilerParams(dimension_semantics=sem, vmem_limit_bytes=VMEM_LIMIT)


def _ffn_body(has_mix, fc, *refs):
    if has_mix:
        x_ref, ya, yb, yc, yd, wo_ref, g_ref, wg_ref, wu_ref, wd_ref, o_ref, h_scr = refs
    else:
        x_ref, g_ref, wg_ref, wu_ref, wd_ref, o_ref, h_scr = refs
    x = x_ref[...]
    if has_mix:
        y = jnp.concatenate([ya[...], yb[...], yc[...], yd[...]], axis=-1)
        x = x + _dot(y, wo_ref[...])
    ms = jnp.mean(x * x, axis=-1, keepdims=True)
    xn = (x * lax.rsqrt(ms + EPS) * g_ref[...]).astype(BF16)
    d_ff = wg_ref.shape[1]
    for c in range(d_ff // fc):
        sl = slice(c * fc, (c + 1) * fc)
        gate = _dot(xn, wg_ref[:, sl].astype(BF16))
        up = _dot(xn, wu_ref[:, sl].astype(BF16))
        h_scr[:, sl] = (gate * jax.nn.sigmoid(gate) * up).astype(BF16)
    o_ref[...] = x + 0.5 * _dot(h_scr[...], wd_ref[...].astype(BF16))


def _ffn(x2, g, wg, wu, wd, layer, mix=None, tm=512, fc=256):
    m, d = x2.shape
    d_ff = wg.shape[2]
    tm = min(tm, m)
    row = lambda w: pl.BlockSpec((tm, w), lambda i: (i, 0))
    in_specs = [row(d)]
    args = [x2]
    if mix is not None:
        ys, wo = mix
        in_specs += [row(GROUP_WIDTH)] * 4 + [_const_spec(wo.shape, layer)]
        args += list(ys) + [wo]
    in_specs += [_const_spec(a.shape, layer) for a in (g, wg, wu, wd)]
    args += [g, wg, wu, wd]
    return pl.pallas_call(
        functools.partial(_ffn_body, mix is not None, fc),
        grid=(m // tm,),
        in_specs=in_specs,
        out_specs=row(d),
        out_shape=jax.ShapeDtypeStruct((m, d), F32),
        scratch_shapes=[pltpu.VMEM((tm, d_ff), BF16)],
        compiler_params=_cparams(("parallel",)),
        name="ffn_mix" if mix is not None else "ffn",
    )(*args)


def _mix_in_body(tm, blk,
                 x_ref, gmix_ref, wn_ref, wt_ref, bd_ref,
                 gaq_ref, gak_ref,
                 wgu_ref, bgb_ref,
                 gqa_ref, wuq_ref, gdq_ref, cosT_ref, sinT_ref,
                 gkva_ref, wuk_ref, wuvT_ref, gdk_ref, gdkpe_ref, cpe_ref, spe_ref,
                 aqT_ref, ak_ref, avT_ref, iqT_ref, ik_ref, iwT_ref,
                 bq_ref, bk_ref, bv_ref, bla_ref, br_ref,
                 ch_ref,
                 dqT_ref, dk_ref, dvT_ref, knorm_ref):
    nlt = tm // LANE
    x = x_ref[0]
    ms = jnp.mean(x * x, axis=-1, keepdims=True)
    xn = (x * lax.rsqrt(ms + EPS) * gmix_ref[...]).astype(BF16)
    bd = bd_ref[...]
    lane = lax.broadcasted_iota(I32, (tm, LANE), 1)

    def lanes(g):
        return jnp.tile(g, (1, nlt))

    z = _dot(xn, wn_ref[...])
    zt = _dot_nt(wt_ref[...], xn)

    def zs(off, width):
        return z[:, off:off + width]

    cq = zs(N_DCQ, D_Q_RANK)
    cq_ms = jnp.mean(cq * cq, axis=-1, keepdims=True)
    cqn = (cq * lax.rsqrt(cq_ms + EPS) * gqa_ref[...]).astype(BF16)
    ckv = zs(N_DCKV, D_KV_RANK)
    ckv_ms = jnp.mean(ckv * ckv, axis=-1, keepdims=True)
    ckvn = (ckv * lax.rsqrt(ckv_ms + EPS) * gkva_ref[...]).astype(BF16)
    dq = _dot_nt(wuq_ref[...], cqn).reshape(D_HEADS, HEAD_PAD, tm)
    kn = _dot(ckvn, wuk_ref[...])
    dv = _dot_nt(wuvT_ref[...], ckvn).astype(BF16)
    gate = _dot(zs(N_BG, LANE).astype(BF16), wgu_ref[...]) + bgb_ref[...]
    ak = zs(N_AK, GROUP_WIDTH)
    ak_ms = _group_sum(ak * ak, bd) * (1.0 / A_HEAD_DIM)
    kn_ss = _group_sum(kn * kn, bd)

    aq = zt[T_AQ:T_AQ + A_HEADS * HEAD_PAD].reshape(A_HEADS, HEAD_PAD, tm)
    aq_ms = jnp.sum(aq * aq, axis=1, keepdims=True) * (1.0 / A_HEAD_DIM)
    aq = aq * lax.rsqrt(aq_ms + EPS) * lanes(gaq_ref[...])[None] * (A_HEAD_DIM ** -0.5 * LOG2E)
    aqT_ref[0] = aq.reshape(A_HEADS * HEAD_PAD, tm).astype(BF16)
    def emit_key_norms(first_row, sq_norms):
        top = jnp.max(sq_norms, axis=0, keepdims=True)
        for h in range(4):
            knorm_ref[0, 0, first_row + h:first_row + h + 1, :] = jnp.broadcast_to(
                top[:, 64 * h:64 * h + 1], (1, LANE))

    def with_ones_rows(vt, heads, dim):
        ones = jnp.ones((ONES_ROWS, tm), BF16)
        return jnp.concatenate([r for h in range(heads) for r in (vt[dim * h:dim * (h + 1)], ones)], axis=0)

    av = with_ones_rows(zt[T_AV:T_AV + GROUP_WIDTH].astype(BF16), A_HEADS, A_HEAD_DIM)
    for c in range(tm // blk):
        avT_ref[0, c] = av[:, c * blk:(c + 1) * blk]
    iqT_ref[0] = zt[T_IQ:T_IQ + IDX_HEADS * IDX_DIM].astype(BF16)
    iwT_ref[0] = zt[T_IW:T_IW + IDX_HEADS] * ((IDX_HEADS ** -0.5) * (IDX_DIM ** -0.5))

    ak = ak * lax.rsqrt(ak_ms + EPS) * gak_ref[...]
    for h in range(A_HEADS):
        pair = ak[:, LANE * (h // 2):LANE * (h // 2) + LANE]
        if h % 2 == 1:
            pair = pltpu.roll(pair, 64, 1)
        kh = jnp.where(lane < A_HEAD_DIM, pair, 0.0).astype(BF16)
        ak_ref[0, h] = kh
    emit_key_norms(0, _dot((ak * ak).astype(BF16), bd))
    ik_ref[0] = zs(N_IK, LANE)[:, :IDX_DIM].astype(BF16)

    bq_ref[0] = zs(N_BQ, LANE) * (B_KEY_DIM ** -0.5)
    bk_ref[0] = zs(N_BK, LANE)
    bv_ref[0] = zs(N_BV, GROUP_WIDTH).astype(BF16)
    bla_ref[0] = (jnp.minimum(gate, 0.0) - jnp.log(1.0 + jnp.exp(-jnp.abs(gate)))) * (1.0 / B_GATE_TAU)
    br_ref[0] = zs(N_BR, GROUP_WIDTH)

    ca = zs(N_CU, C_CHANNELS)
    cg = zs(N_CU + C_CHANNELS, C_CHANNELS)
    ch_ref[0] = ca * jax.nn.sigmoid(cg)

    dq_ms = jnp.sum(dq * dq, axis=1, keepdims=True) * (1.0 / D_QK)
    dq = dq * lax.rsqrt(dq_ms + EPS) * lanes(gdq_ref[...])[None] * (D_QK ** -0.5 * LOG2E)
    half = D_ROPE // 2
    x1 = dq[:, D_NOPE:D_NOPE + half]
    x2 = dq[:, D_NOPE + half:D_QK]
    cs = cosT_ref[...][None]
    sn = sinT_ref[...][None]
    dq = jnp.concatenate([dq[:, :D_NOPE], x1 * cs - x2 * sn, x2 * cs + x1 * sn, dq[:, D_QK:]], axis=1)
    dqT_ref[0] = dq.reshape(D_HEADS * HEAD_PAD, tm).astype(BF16)

    dv = with_ones_rows(dv, D_HEADS, D_V)
    for c in range(tm // blk):
        dvT_ref[0, c] = dv[:, c * blk:(c + 1) * blk]
    kpe = zs(N_DKPE, LANE)
    ss = kn_ss + jnp.sum(kpe * kpe, axis=-1, keepdims=True)
    rinv = lax.rsqrt(ss * (1.0 / D_QK) + EPS)
    kn = kn * rinv * gdk_ref[...]
    pe = kpe * gdkpe_ref[...]
    partner = jnp.where(lane < half, pltpu.roll(pe, LANE - half, 1), pltpu.roll(pe, half, 1))
    pe = pe * cpe_ref[...] + partner * spe_ref[...]
    pe = pltpu.roll(pe, D_NOPE, 1)
    for h in range(D_HEADS):
        pair = kn[:, LANE * (h // 2):LANE * (h // 2) + LANE]
        rpair = rinv[:, LANE * (h // 2):LANE * (h // 2) + LANE]
        if h % 2 == 1:
            pair = pltpu.roll(pair, 64, 1)
        else:
            rpair = pltpu.roll(rpair, 64, 1)
        kh = jnp.where(lane < D_NOPE, pair, pe * rpair).astype(BF16)
        dk_ref[0, h] = kh
    pe_sq = _dot((pe * pe).astype(BF16), jnp.ones((LANE, LANE), BF16))
    emit_key_norms(A_HEADS, _dot((kn * kn).astype(BF16), bd) + jnp.tile(pe_sq, (1, 2)) * (rinv * rinv))


def _mix_in(x3, p, layer, tm=512, blk=ATT_BLK):
    b, l, d = x3.shape
    tm = min(tm, l)
    grid = (b, l // tm)
    nck = l // blk
    tok = lambda w: pl.BlockSpec((1, tm, w), lambda bi, i: (bi, i, 0))
    tokT = lambda r: pl.BlockSpec((1, r, tm), lambda bi, i: (bi, 0, i))
    headk = pl.BlockSpec((1, 4, tm, HEAD_PAD), lambda bi, i: (bi, 0, i, 0))
    vrows = GROUP_WIDTH + 4 * ONES_ROWS
    chunkT = pl.BlockSpec((1, tm // blk, vrows, blk), lambda bi, i: (bi, i, 0, 0))
    postab = lambda r: pl.BlockSpec((r, tm), lambda bi, i: (0, i))
    posrow = pl.BlockSpec((tm, LANE), lambda bi, i: (i, 0))
    consts = [p["gmix"], p["wn"], p["wt"], p["bd"], p["gaq"], p["gak"], p["wgu"], p["bgb"],
              p["gqa"], p["wuq"], p["gdq"]]
    consts2 = [p["gkva"], p["wuk"], p["wuvT"], p["gdk"], p["gdkpe"]]
    lspec = lambda a: _const_spec(a.shape) if a is p["bd"] else _const_spec(a.shape, layer)
    in_specs = ([tok(d)] + [lspec(a) for a in consts]
                + [postab(D_ROPE // 2), postab(D_ROPE // 2)]
                + [lspec(a) for a in consts2] + [posrow, posrow])
    args = [x3] + consts + [p["cosT"], p["sinT"]] + consts2 + [p["cpe"], p["spe"]]
    sd = jax.ShapeDtypeStruct
    out_shape = [
        sd((b, A_HEADS * HEAD_PAD, l), BF16), sd((b, A_HEADS, l, HEAD_PAD), BF16),
        sd((b, nck, vrows, blk), BF16), sd((b, IDX_HEADS * IDX_DIM, l), BF16),
        sd((b, l, IDX_DIM), BF16), sd((b, IDX_HEADS, l), F32),
        sd((b, l, LANE), F32), sd((b, l, LANE), F32), sd((b, l, GROUP_WIDTH), BF16),
        sd((b, l, LANE), F32), sd((b, l, GROUP_WIDTH), F32),
        sd((b, l, C_CHANNELS), F32),
        sd((b, D_HEADS * HEAD_PAD, l), BF16), sd((b, D_HEADS, l, HEAD_PAD), BF16),
        sd((b, nck, vrows, blk), BF16),
        sd((b, l // tm, A_HEADS + D_HEADS, LANE), F32),
    ]
    out_specs = [
        tokT(A_HEADS * HEAD_PAD), headk, chunkT, tokT(IDX_HEADS * IDX_DIM),
        tok(IDX_DIM), tokT(IDX_HEADS),
        tok(LANE), tok(LANE), tok(GROUP_WIDTH), tok(LANE), tok(GROUP_WIDTH),
        tok(C_CHANNELS),
        tokT(D_HEADS * HEAD_PAD), headk, chunkT,
        pl.BlockSpec((1, 1, A_HEADS + D_HEADS, LANE), lambda bi, i: (bi, i, 0, 0)),
    ]
    return pl.pallas_call(
        functools.partial(_mix_in_body, tm, blk),
        grid=grid, in_specs=in_specs, out_specs=out_specs, out_shape=out_shape,
        compiler_params=_cparams(("parallel", "parallel")),
        name="mix_in",
    )(*args)


def _attn_body(nh, dv, blk, topk, is_dsa, *refs):
    if is_dsa:
        (kb_ref, qT_ref, k_ref, vT_ref, iqT_ref, wT_ref, ik_ref, bt_ref,
         o_ref, m_scr, acc_scr, key_scr, run_scr, plane_scr, active_scr) = refs
    else:
        kb_ref, qT_ref, k_ref, vT_ref, o_ref, m_scr, acc_scr = refs
    vr = dv + ONES_ROWS
    i = pl.program_id(1)
    t = blk
    row = lax.broadcasted_iota(I32, (t, t), 0)
    col = lax.broadcasted_iota(I32, (t, t), 1)
    causal_pen = jnp.where(row <= col, 0.0, NEG)

    m_scr[...] = jnp.full(m_scr.shape, NEG, F32)
    acc_scr[...] = jnp.zeros(acc_scr.shape, F32)

    if is_dsa:
        @pl.when(i == 0)
        def _():
            plane_scr[...] = jnp.zeros(plane_scr.shape, I32)

        def score_chunk(j, diag):
            r0 = pl.multiple_of(j * t, t)
            ikc = ik_ref[0, pl.ds(r0, t), :]
            s = jnp.zeros((t, t), F32)
            for h in range(IDX_HEADS):
                d = _dot(ikc, iqT_ref[0, IDX_DIM * h:IDX_DIM * (h + 1), :])
                s = s + jnp.maximum(d, 0.0) * wT_ref[0, h:h + 1, :]
            bits = lax.bitcast_convert_type(s, I32)
            key = jnp.where(bits < 0, bits ^ 0x7FFFFFFF, bits)
            if diag:
                key = jnp.where(row <= col, key, INT_MIN)
            key_scr[pl.ds(r0, t), :] = key
            w = [key[SUBLANES * r:SUBLANES * (r + 1), :] for r in range(32)]
            step, mask = 16, 0x0000FFFF
            while step:
                for lo in range(32):
                    if lo & step == 0:
                        hi = lo + step
                        swap = (w[lo] ^ jnp.right_shift(w[hi], step)) & mask
                        w[lo] = w[lo] ^ swap
                        w[hi] = w[hi] ^ jnp.left_shift(swap, step)
                step //= 2
                mask ^= (mask << step) & 0xFFFFFFFF
            c0 = pl.multiple_of(j * SUBLANES, SUBLANES)
            w[0] = ~w[0]
            for p in range(32):
                plane_scr[p, pl.ds(c0, SUBLANES), :] = w[p]

        def score_group(u, carry):
            for c in range(SCORE_GROUP):
                score_chunk(SCORE_GROUP * u + c, False)
            return carry

        lax.fori_loop(0, i // SCORE_GROUP, score_group, 0)
        for rem in range(SCORE_GROUP):
            @pl.when(i % SCORE_GROUP == rem)
            def _():
                for c in range(rem):
                    score_chunk(i - rem + c, False)
                score_chunk(i, True)

        def kth_largest(nrow):
            rows = pl.ds(0, nrow)
            in_range = lax.broadcasted_iota(I32, (nrow, t), 0) < (i + 1) * SUBLANES

            def col_count(words):
                pc = lax.population_count(words).reshape(nrow // SUBLANES, SUBLANES, t)
                return jnp.sum(jnp.sum(pc, axis=0), axis=0, keepdims=True)

            def decide(plane, active, n_gt, ans_u):
                ones = col_count(active & plane_scr[plane, rows, :])
                take = n_gt + ones >= topk
                bit = lax.shift_right_logical(jnp.int32(INT_MIN), jnp.int32(plane))
                return (jnp.where(take, 0, -1), n_gt + jnp.where(take, 0, ones),
                        ans_u | jnp.where(take, bit, 0))

            active_scr[rows, :] = jnp.where(in_range, -1, 0)
            state = decide(0, active_scr[rows, :], jnp.zeros((1, t), I32), jnp.zeros((1, t), I32))

            def plane_body(plane, state):
                flip, n_gt, ans_u = state
                active = active_scr[rows, :] & (plane_scr[plane - 1, rows, :] ^ flip)
                active_scr[rows, :] = active
                return decide(plane, active, n_gt, ans_u)

            _, n_gt, ans_u = lax.fori_loop(1, 32, plane_body, state)
            return n_gt, ans_u

        nrow_all = plane_scr.shape[1]
        nrow_half = (nrow_all // SUBLANES // 2) * SUBLANES
        if nrow_half:
            n_gt, ans_u = lax.cond((i + 1) * SUBLANES <= nrow_half,
                                   functools.partial(kth_largest, nrow_half),
                                   functools.partial(kth_largest, nrow_all))
        else:
            n_gt, ans_u = kth_largest(nrow_all)
        ans = ans_u ^ INT_MIN
        need = (topk - n_gt).astype(F32)
        run_scr[...] = jnp.zeros(run_scr.shape, F32)
        stri = jnp.where(col < row, 1.0, 0.0).astype(BF16)

    def logits(h, r0):
        return _dot(k_ref[0, h, pl.ds(r0, t), :], qT_ref[0, HEAD_PAD * h:HEAD_PAD * (h + 1), :])

    def selection_pens(r0s):
        pens = []
        if is_dsa:
            for r0 in r0s:
                kc = key_scr[pl.ds(r0, t), :]
                eq = kc == ans
                eqf = jnp.where(eq, 1.0, 0.0)
                run = run_scr[0:1, :]
                rank = _dot(stri, eqf.astype(BF16)) + run
                run_scr[0:1, :] = run + jnp.sum(eqf, axis=0, keepdims=True)
                pens.append(jnp.where(kc > ans, 0.0, jnp.where(eq, jnp.where(rank < need, 0.0, NEG), NEG)))
        return pens

    b_idx = pl.program_id(0)
    bounds = []
    for h in range(nh):
        qh = qT_ref[0, HEAD_PAD * h:HEAD_PAD * (h + 1), :].astype(F32)
        bound = jnp.sqrt(jnp.sum(qh * qh, axis=0, keepdims=True)) * (kb_ref[b_idx * nh + h] * BOUND_SLACK)
        if is_dsa:
            bound = bound + kb_ref[kb_ref.shape[0] - 1]
        bounds.append(bound)
    bound_max = jnp.max(functools.reduce(jnp.maximum, bounds))
    fixed_ref_ok = bound_max <= FIXED_REF_MAX

    def fixed_ref_chunks(js, kinds):
        n = len(js)
        r0s = [pl.multiple_of(j * t, t) for j in js]
        lgs = [[logits(h, r0s[c]) for c in range(n)] for h in range(nh)]
        pens = selection_pens(r0s)
        ps = []
        for h in range(nh):
            row_ps = []
            if is_dsa:
                far_ref = bounds[h] - bt_ref[2, h, 0:1, :]
            for c in range(n):
                x, ref = lgs[h][c], bounds[h]
                if is_dsa and kinds[c] == "far":
                    x, ref = x + pens[c], far_ref
                elif is_dsa:
                    x = x + (pens[c] + bt_ref[0 if kinds[c] == "diag" else 1, h])
                elif kinds[c] == "diag":
                    x = x + causal_pen
                row_ps.append(jnp.exp2(x - ref).astype(BF16))
            ps.append(row_ps[0] if n == 1 else jnp.concatenate(row_ps, axis=0))
        for h in range(nh):
            vs = slice(vr * h, vr * (h + 1))
            vt = [vT_ref[0, js[c], vs, :] for c in range(n)]
            acc_scr[vs, :] += _dot(vt[0] if n == 1 else jnp.concatenate(vt, axis=1), ps[h])

    def super_chunk(js, kinds):
        n = len(js)
        r0s = [pl.multiple_of(j * t, t) for j in js]
        lgs = [[logits(h, r0s[c]) for c in range(n)] for h in range(nh)]
        pens = selection_pens(r0s)
        ps, alphas = [], []
        for h in range(nh):
            xs = []
            for c, j in enumerate(js):
                diag = kinds[c] == "diag"
                lg = lgs[h][c]
                if is_dsa:
                    tile = 0 if diag else jnp.minimum(i - j, 2)
                    lg = lg + (pens[c] + bt_ref[tile, h])
                elif diag:
                    lg = lg + causal_pen
                xs.append(lg)
            m_old = m_scr[h, 0:1, :]
            m_new = m_old
            for x in xs:
                m_new = jnp.maximum(m_new, jnp.max(x, axis=0, keepdims=True))
            alpha = jnp.exp2(m_old - m_new)
            m_scr[h, 0:1, :] = m_new
            ps.append([jnp.exp2((x - m_new).astype(BF16)) for x in xs])
            alphas.append(alpha)
        for h in range(nh):
            vs = slice(vr * h, vr * (h + 1))
            pv = _dot(vT_ref[0, js[0], vs, :], ps[h][0])
            for c in range(1, n):
                pv = pv + _dot(vT_ref[0, js[c], vs, :], ps[h][c])
            acc_scr[vs, :] = alphas[h] * acc_scr[vs, :] + pv

    ATT_GROUP = DSA_GROUP if is_dsa else MLA_GROUP
    tail = ["near", "diag"] if is_dsa else ["diag"]
    nfar = jnp.maximum(i + 1 - len(tail), 0)

    def run_chunks(step):
        def group_body(u, carry):
            step([ATT_GROUP * u + c for c in range(ATT_GROUP)], ["far"] * ATT_GROUP)
            return carry

        lax.fori_loop(0, nfar // ATT_GROUP, group_body, 0)
        for rem in range(ATT_GROUP):
            @pl.when(jnp.logical_and(i + 1 >= len(tail), nfar % ATT_GROUP == rem))
            def _():
                first = i + 1 - len(tail) - rem
                step([first + c for c in range(rem + len(tail))], ["far"] * rem + tail)

        if is_dsa:
            @pl.when(i == 0)
            def _():
                step([i], ["diag"])

    pl.when(fixed_ref_ok)(functools.partial(run_chunks, fixed_ref_chunks))
    pl.when(jnp.logical_not(fixed_ref_ok))(functools.partial(run_chunks, super_chunk))

    outs = []
    for h in range(nh):
        outs.append(acc_scr[vr * h:vr * h + dv, :] / acc_scr[vr * h + dv:vr * h + dv + 1, :])
    o_ref[0] = jnp.transpose(jnp.concatenate(outs, axis=0)).astype(o_ref.dtype)


def _attention(qT, k, vT, kbound, dsa=None, blk=ATT_BLK):
    b, nh, l, _ = k.shape
    vr = vT.shape[2] // nh
    dv = vr - ONES_ROWS
    grid = (b, l // blk)
    qspec = lambda r: pl.BlockSpec((1, r, blk), lambda bi, i: (bi, 0, i))
    kspec = pl.BlockSpec((1, nh, l, HEAD_PAD), lambda bi, i: (bi, 0, 0, 0))
    vspec = pl.BlockSpec((1, l // blk, nh * vr, blk), lambda bi, i: (bi, 0, 0, 0))
    in_specs = [pl.BlockSpec(memory_space=pltpu.SMEM), qspec(nh * HEAD_PAD), kspec, vspec]
    args = [kbound, qT, k, vT]
    scratch = [pltpu.VMEM((nh, 8, blk), F32), pltpu.VMEM((nh * vr, blk), F32)]
    topk = 0
    if dsa is not None:
        iqT, wT, ik, bt = dsa
        topk = min(TOPK_MAX, l // 4)
        in_specs += [qspec(IDX_HEADS * IDX_DIM), qspec(IDX_HEADS),
                     pl.BlockSpec((1, l, IDX_DIM), lambda bi, i: (bi, 0, 0)),
                     _const_spec(bt.shape)]
        args += [iqT, wT, ik, bt]
        assert blk == 32 * SUBLANES
        nrow = (l // blk) * SUBLANES
        scratch += [pltpu.VMEM((l, blk), I32), pltpu.VMEM((8, blk), F32),
                    pltpu.VMEM((32, nrow, blk), I32), pltpu.VMEM((nrow, blk), I32)]
    return pl.pallas_call(
        functools.partial(_attn_body, nh, dv, blk, topk, dsa is not None),
        grid=grid, in_specs=in_specs,
        out_specs=pl.BlockSpec((1, blk, nh * dv), lambda bi, i: (bi, i, 0)),
        out_shape=jax.ShapeDtypeStruct((b, l, nh * dv), BF16),
        scratch_shapes=scratch,
        compiler_params=_cparams(("parallel", "arbitrary")),
        name="dsa_attn" if dsa is not None else "mla_attn",
    )(*args)


def _gla_body(tg, q_ref, k_ref, v_ref, la_ref, r_ref, go_ref, bd_ref, o_ref, st_scr, o_scr):
    @pl.when(pl.program_id(1) == 0)
    def _():
        st_scr[...] = jnp.zeros(st_scr.shape, F32)

    cs = B_CHUNK
    la = la_ref[0]
    rl = lax.broadcasted_iota(I32, (tg, LANE), 0) & (cs - 1)
    b = la
    s = 1
    while s < cs:
        b = b + jnp.where(rl >= s, pltpu.roll(b, s, 0), 0.0)
        s *= 2
    q = q_ref[0]
    k = k_ref[0]
    qb = q * jnp.exp(b)
    tri = lax.broadcasted_iota(I32, (cs, cs), 1) <= lax.broadcasted_iota(I32, (cs, cs), 0)
    khead = lax.broadcasted_iota(I32, (cs, LANE), 1) // B_KEY_DIM
    vhead = lax.broadcasted_iota(I32, (cs, GROUP_WIDTH), 1) // B_VAL_DIM
    same_head = (lax.broadcasted_iota(I32, (GROUP_WIDTH, LANE), 0) // B_VAL_DIM
                 == lax.broadcasted_iota(I32, (GROUP_WIDTH, LANE), 1) // B_KEY_DIM)
    for c in range(tg // cs):
        sl = slice(c * cs, (c + 1) * cs)
        bc = b[sl]
        mid = bc[cs // 2:cs // 2 + 1]
        last = bc[cs - 1:cs]
        qe = q[sl] * jnp.exp(bc - mid)
        ke = (k[sl] * jnp.exp(mid - bc)).astype(BF16)
        kd = (k[sl] * jnp.exp(last - bc)).astype(BF16)
        qbc = qb[sl].astype(BF16)
        vc = v_ref[0, sl, :]
        st = st_scr[...]
        qe4 = jnp.concatenate([jnp.where(khead == h, qe, 0.0) for h in range(B_HEADS)], axis=0)
        a_all = _dot_nt(qe4.astype(BF16), ke)
        o = _dot_nt(qbc, st.astype(BF16))
        for h in range(B_HEADS):
            a = jnp.where(tri, a_all[cs * h:cs * (h + 1)], 0.0).astype(BF16)
            o = o + _dot(a, jnp.where(vhead == h, vc, jnp.zeros_like(vc)))
        o_scr[sl, :] = o
        st_scr[...] = st * jnp.exp(last) + jnp.where(same_head, _dot_tn(vc, kd), 0.0)
    o = o_scr[...]
    ms = _group_sum(o * o, bd_ref[...]) * (1.0 / B_VAL_DIM)
    r = r_ref[0]
    o_ref[0] = (o * lax.rsqrt(ms + EPS) * go_ref[...] * (r * jax.nn.sigmoid(r))).astype(o_ref.dtype)


def _gla(bq, bk, bv, bla, br, go, bd, layer, tg=1024):
    b, l, _ = bq.shape
    tg = min(tg, l)
    tok = lambda w: pl.BlockSpec((1, tg, w), lambda bi, i: (bi, i, 0))
    return pl.pallas_call(
        functools.partial(_gla_body, tg),
        grid=(b, l // tg),
        in_specs=[tok(LANE), tok(LANE), tok(GROUP_WIDTH), tok(LANE), tok(GROUP_WIDTH),
                  _const_spec(go.shape, layer), _const_spec(bd.shape)],
        out_specs=tok(GROUP_WIDTH),
        out_shape=jax.ShapeDtypeStruct((b, l, GROUP_WIDTH), BF16),
        scratch_shapes=[pltpu.VMEM((B_HEADS * B_VAL_DIM, B_HEADS * B_KEY_DIM), F32),
                        pltpu.VMEM((tg, GROUP_WIDTH), F32)],
        compiler_params=_cparams(("parallel", "arbitrary")),
        name="gla",
    )(bq, bk, bv, bla, br, go, bd)


CONV_HIST = 32


def _conv_body(tc, h_ref, w_ref, b_ref, g_ref, o_ref, buf):
    @pl.when(pl.program_id(1) == 0)
    def _():
        buf[0:CONV_HIST, :] = jnp.zeros((CONV_HIST, C_CHANNELS), F32)

    @pl.when(pl.program_id(1) > 0)
    def _():
        buf[0:CONV_HIST, :] = buf[tc:tc + CONV_HIST, :]

    buf[CONV_HIST:CONV_HIST + tc, :] = h_ref[0]
    acc = jnp.zeros((tc, C_CHANNELS), F32) + b_ref[...]
    base = CONV_HIST - (C_KERNEL - 1)
    hb = buf[...]
    rows = tc + CONV_HIST
    for r in range(SUBLANES):
        shifted = hb if r == 0 else pltpu.roll(hb, rows - r, 0)
        for j in range(C_KERNEL):
            if (base + j) % SUBLANES == r:
                a0 = base + j - r
                acc = acc + shifted[a0:a0 + tc, :] * w_ref[j:j + 1, :]
    ms = jnp.mean(acc * acc, axis=-1, keepdims=True)
    y = acc * lax.rsqrt(ms + EPS) * g_ref[...]
    o_ref[0] = (y * jax.nn.sigmoid(y)).astype(o_ref.dtype)


def _conv(ch, w, bias, g, layer, tc=512):
    b, l, c = ch.shape
    tc = min(tc, l)
    tok = pl.BlockSpec((1, tc, c), lambda bi, i: (bi, i, 0))
    return pl.pallas_call(
        functools.partial(_conv_body, tc),
        grid=(b, l // tc),
        in_specs=[tok] + [_const_spec(a.shape, layer) for a in (w, bias, g)],
        out_specs=tok,
        out_shape=jax.ShapeDtypeStruct((b, l, c), BF16),
        scratch_shapes=[pltpu.VMEM((tc + CONV_HIST, c), F32)],
        compiler_params=_cparams(("parallel", "arbitrary")),
        name="conv",
    )(ch, w, bias, g)


def _t5_bucket(dist):
    max_exact = REL_BUCKETS // 2
    d = jnp.maximum(dist, 0)
    df = jnp.maximum(d, 1).astype(F32)
    large = max_exact + (jnp.log(df / max_exact) / math.log(REL_MAX_DIST / max_exact)
                         * (REL_BUCKETS - max_exact)).astype(I32)
    large = jnp.minimum(large, REL_BUCKETS - 1)
    return jnp.where(d < max_exact, d, large)


def _pad_cols(w, width):
    return jnp.pad(w, ((0, 0), (0, width - w.shape[1])))


def _lane_rep(v):
    return jnp.broadcast_to(v[:, None], (v.shape[0], LANE))


def _pad_heads_rows(w, heads, dim):
    w = w.reshape(heads, dim, w.shape[1])
    return jnp.pad(w, ((0, 0), (0, HEAD_PAD - dim), (0, 0))).reshape(heads * HEAD_PAD, -1)


def _split_w_in(w_in):
    widths = (256, 256, 256, 256, 32, 8, 128, 128, 256, 16, 256, 512, 256, 128, 32)
    offs = np.cumsum((0,) + widths)
    return [w_in[:, offs[n]:offs[n + 1]] for n in range(len(widths))]


def _row(v):
    return v[None, :].astype(F32)


def _one_layer_params(w):
    (aq, ak, av, iq, ik, iw, bq, bk, bv, bg, br, cu, dcq, dckv, dkpe) = _split_w_in(w["w_in"])
    wn = jnp.concatenate([ak, _pad_cols(ik, LANE), bq, bk, bv, _pad_cols(bg, LANE), br, cu, dcq, dckv,
                          _pad_cols(dkpe, LANE)], axis=1).astype(BF16)
    wt = jnp.concatenate([_pad_heads_rows(aq.T, A_HEADS, A_HEAD_DIM), av.T, iq.T,
                          jnp.pad(iw.T, ((0, T_TOT - T_IW - IDX_HEADS), (0, 0)))], axis=0).astype(BF16)
    pad_to = lambda v, n: jnp.pad(v, (0, n - v.shape[0]))
    ukv = w["d_ukv"].reshape(D_KV_RANK, D_HEADS, D_NOPE + D_V)
    wuk = ukv[:, :, :D_NOPE].reshape(D_KV_RANK, D_HEADS * D_NOPE)
    wuv = ukv[:, :, D_NOPE:].reshape(D_KV_RANK, D_HEADS * D_V)
    gdk = w["d_k_norm"]
    return dict(
        gmix=_row(w["mix_norm"]), wn=wn, wt=wt,
        gaq=_lane_rep(pad_to(w["a_q_norm"], HEAD_PAD)),
        gak=_row(jnp.tile(w["a_k_norm"], A_HEADS)),
        wgu=jnp.pad(w["b_gate_up"], ((0, LANE - B_GATE_RANK), (0, 0))).astype(BF16),
        bgb=_row(w["b_gate_bias"]),
        gqa=_row(w["d_qa_norm"]),
        wuq=_pad_heads_rows(w["d_uq"].T, D_HEADS, D_QK).astype(BF16),
        gdq=_lane_rep(pad_to(w["d_q_norm"], HEAD_PAD)),
        gkva=_row(w["d_kva_norm"]), wuk=wuk.astype(BF16), wuvT=wuv.T.astype(BF16),
        gdk=_row(jnp.tile(gdk[:D_NOPE], D_HEADS)), gdkpe=_row(pad_to(gdk[D_NOPE:], LANE)),
        gbo=_row(jnp.tile(w["b_out_norm"], B_HEADS)),
        cw=jnp.pad(w["c_dw_w"][:, 0, :], ((0, CONV_HIST - C_KERNEL), (0, 0))).astype(F32),
        cb=_row(w["c_dw_b"]), cg=_row(w["c_norm"]),
    )


def _shared_tables(seq):
    hid = np.arange(GROUP_WIDTH) // 64
    bd = jnp.asarray(hid[:, None] == hid[None, :], dtype=BF16)
    half = D_ROPE // 2
    freqs = ROPE_THETA ** (-jnp.arange(half, dtype=F32) / half)
    ang = jnp.arange(seq).astype(F32)[:, None] * freqs[None, :]
    cos, sin = jnp.cos(ang), jnp.sin(ang)
    zeros = jnp.zeros((seq, LANE - D_ROPE), F32)
    cpe = jnp.concatenate([cos, cos, zeros], axis=1)
    spe = jnp.concatenate([-sin, sin, zeros], axis=1)
    return dict(bd=bd, cosT=cos.T, sinT=sin.T, cpe=cpe, spe=spe)


def _bias_tiles(rel_bias, blk):
    assert REL_MAX_DIST <= blk + 1
    kk = jnp.arange(blk)[:, None]
    qq = jnp.arange(blk)[None, :]
    rb = rel_bias.astype(F32).T

    def lookup(bucket):
        onehot = bucket[None, :, :, None] == jnp.arange(REL_BUCKETS)
        return jnp.sum(jnp.where(onehot, rb[:, None, None, :], 0.0), axis=-1) * LOG2E

    d0 = jnp.where(kk <= qq, lookup(_t5_bucket(qq - kk)), NEG)
    d1 = lookup(_t5_bucket(blk + qq - kk))
    far = jnp.broadcast_to(lookup(_t5_bucket(jnp.full((1, 1), 2 * blk, I32))), d1.shape)
    return jnp.stack([d0, d1, far])


def _key_bounds(knorm, bias_max):
    km = jnp.sqrt(jnp.max(knorm[..., 0], axis=1))
    kb_a = jnp.concatenate([km[:, :A_HEADS].reshape(-1), bias_max.reshape(1)])
    return kb_a, km[:, A_HEADS:].reshape(-1)


def kernel(x, ffn1_norm, ffn1_gate, ffn1_up, ffn1_down, mix_norm, w_in, a_q_norm, a_k_norm, rel_bias,
           b_gate_up, b_gate_bias, b_out_norm, c_dw_w, c_dw_b, c_norm, d_qa_norm, d_uq, d_kva_norm,
           d_ukv, d_q_norm, d_k_norm, w_out, ffn2_norm, ffn2_gate, ffn2_up, ffn2_down):
    w = dict(mix_norm=mix_norm, w_in=w_in, a_q_norm=a_q_norm, a_k_norm=a_k_norm, b_gate_up=b_gate_up,
             b_gate_bias=b_gate_bias, b_out_norm=b_out_norm, c_dw_w=c_dw_w, c_dw_b=c_dw_b, c_norm=c_norm,
             d_qa_norm=d_qa_norm, d_uq=d_uq, d_kva_norm=d_kva_norm, d_ukv=d_ukv,
             d_q_norm=d_q_norm, d_k_norm=d_k_norm)
    bsz, seq, dm = x.shape
    depth = w_in.shape[0]
    blk = min(ATT_BLK, seq)
    bt = _bias_tiles(rel_bias, blk)
    bias_max = jnp.max(bt)
    p = {**jax.vmap(_one_layer_params)(w), **_shared_tables(seq)}
    stacked_row = lambda v: v[:, None, :].astype(F32)
    ffn1 = (stacked_row(ffn1_norm), ffn1_gate, ffn1_up, ffn1_down)
    ffn2 = (stacked_row(ffn2_norm), ffn2_gate, ffn2_up, ffn2_down)
    wo = w_out.astype(BF16)
    x2 = x.reshape(bsz * seq, dm)
    for l in range(depth):
        x2 = _ffn(x2, *ffn1, l)
        (aqT, akh, avT, iqT, aik, iwT, bq, bk, bv, bla, br, ch, dqT, dkh, dvT, knorm) = _mix_in(
            x2.reshape(bsz, seq, dm), p, l, blk=blk)
        kb_a, kb_d = _key_bounds(knorm, bias_max)
        y_a = _attention(aqT, akh, avT, kb_a, dsa=(iqT, iwT, aik, bt), blk=blk)
        y_b = _gla(bq, bk, bv, bla, br, p["gbo"], p["bd"], l)
        y_c = _conv(ch, p["cw"], p["cb"], p["cg"], l)
        y_d = _attention(dqT, dkh, dvT, kb_d, blk=blk)
        ys = [y.reshape(bsz * seq, GROUP_WIDTH) for y in (y_a, y_b, y_c, y_d)]
        x2 = _ffn(x2, *ffn2, l, mix=(ys, wo))
    return x2.reshape(bsz, seq, dm)
```

```python
import functools
import math

import jax
import jax.numpy as jnp
import numpy as np
from jax import lax
from jax.experimental import pallas as pl
from jax.experimental.pallas import tpu as pltpu

F32 = jnp.float32
BF16 = jnp.bfloat16
I32 = jnp.int32

EPS = 1e-6
GROUP_WIDTH = 256
A_HEADS, A_HEAD_DIM = 4, 64
IDX_HEADS, IDX_DIM = 8, 32
TOPK_MAX = 256
REL_BUCKETS, REL_MAX_DIST = 32, 128
B_HEADS, B_KEY_DIM, B_VAL_DIM, B_GATE_RANK = 4, 32, 64, 16
B_GATE_TAU = 16.0
B_CHUNK = 64
C_CHANNELS, C_KERNEL = 256, 31
D_HEADS, D_Q_RANK, D_KV_RANK, D_NOPE, D_ROPE, D_V = 4, 256, 128, 64, 32, 64
D_QK = D_NOPE + D_ROPE
ROPE_THETA = 10000.0

LANE = 128
SUBLANES = 8
HEAD_PAD = 128
ONES_ROWS = 16
ATT_BLK = 256
ATT_GROUP = 2
SCORE_GROUP = 4
BOUND_SLACK = 1.01
FIXED_REF_MAX = 40.0
INT_MIN = -2 ** 31
NEG = -1e30
LOG2E = math.log2(math.e)
VMEM_LIMIT = 56 * 1024 * 1024

N_AK, N_IK, N_BQ, N_BK, N_BV, N_BG, N_BR, N_CU, N_DCQ, N_DCKV, N_DKPE, N_TOT = (
    0, 256, 384, 512, 640, 896, 1024, 1280, 1792, 2048, 2176, 2304)
T_AQ, T_AV, T_IQ, T_IW, T_TOT = 0, 512, 768, 1024, 1040


def _dot(a, b):
    return jnp.dot(a, b, preferred_element_type=F32)


def _dot_nt(a, b):
    return lax.dot_general(a, b, (((1,), (1,)), ((), ())), preferred_element_type=F32)


def _dot_tn(a, b):
    return lax.dot_general(a, b, (((0,), (0,)), ((), ())), preferred_element_type=F32)


def _group_sum(x2, bd):
    hi = x2.astype(BF16)
    lo = (x2 - hi.astype(F32)).astype(BF16)
    return _dot(hi, bd) + _dot(lo, bd)


def _const_spec(shape, layer=None):
    nd = len(shape)
    if layer is None:
        return pl.BlockSpec(shape, lambda *_: (0,) * nd, pipeline_mode=pl.Buffered(1))
    return pl.BlockSpec((None,) + tuple(shape[1:]), lambda *_: (layer,) + (0,) * (nd - 1),
                        pipeline_mode=pl.Buffered(1))


def _cparams(sem):
    return pltpu.CompilerParams(dimension_semantics=sem, vmem_limit_bytes=VMEM_LIMIT)


def _ffn_body(has_mix, fc, *refs):
    if has_mix:
        x_ref, ya, yb, yc, yd, wo_ref, g_ref, wg_ref, wu_ref, wd_ref, o_ref, h_scr = refs
    else:
        x_ref, g_ref, wg_ref, wu_ref, wd_ref, o_ref, h_scr = refs
    x = x_ref[...]
    if has_mix:
        y = jnp.concatenate([ya[...], yb[...], yc[...], yd[...]], axis=-1)
        x = x + _dot(y, wo_ref[...])
    ms = jnp.mean(x * x, axis=-1, keepdims=True)
    xn = (x * lax.rsqrt(ms + EPS) * g_ref[...]).astype(BF16)
    d_ff = wg_ref.shape[1]
    for c in range(d_ff // fc):
        sl = slice(c * fc, (c + 1) * fc)
        gate = _dot(xn, wg_ref[:, sl].astype(BF16))
        up = _dot(xn, wu_ref[:, sl].astype(BF16))
        h_scr[:, sl] = (gate * jax.nn.sigmoid(gate) * up).astype(BF16)
    o_ref[...] = x + 0.5 * _dot(h_scr[...], wd_ref[...].astype(BF16))


def _ffn(x2, g, wg, wu, wd, layer, mix=None, tm=512, fc=256):
    m, d = x2.shape
    d_ff = wg.shape[2]
    tm = min(tm, m)
    row = lambda w: pl.BlockSpec((tm, w), lambda i: (i, 0))
    in_specs = [row(d)]
    args = [x2]
    if mix is not None:
        ys, wo = mix
        in_specs += [row(GROUP_WIDTH)] * 4 + [_const_spec(wo.shape, layer)]
        args += list(ys) + [wo]
    in_specs += [_const_spec(a.shape, layer) for a in (g, wg, wu, wd)]
    args += [g, wg, wu, wd]
    return pl.pallas_call(
        functools.partial(_ffn_body, mix is not None, fc),
        grid=(m // tm,),
        in_specs=in_specs,
        out_specs=row(d),
        out_shape=jax.ShapeDtypeStruct((m, d), F32),
        scratch_shapes=[pltpu.VMEM((tm, d_ff), BF16)],
        compiler_params=_cparams(("parallel",)),
        name="ffn_mix" if mix is not None else "ffn",
    )(*args)


def _mix_in_body(tm, blk,
                 x_ref, gmix_ref, wn_ref, wt_ref, bd_ref,
                 gaq_ref, gak_ref,
                 wgu_ref, bgb_ref,
                 gqa_ref, wuq_ref, gdq_ref, cosT_ref, sinT_ref,
                 gkva_ref, wuk_ref, wuvT_ref, gdk_ref, gdkpe_ref, cpe_ref, spe_ref,
                 aqT_ref, ak_ref, avT_ref, iqT_ref, ik_ref, iwT_ref,
                 bq_ref, bk_ref, bv_ref, bla_ref, br_ref,
                 ch_ref,
                 dqT_ref, dk_ref, dvT_ref, knorm_ref):
    nlt = tm // LANE
    x = x_ref[0]
    ms = jnp.mean(x * x, axis=-1, keepdims=True)
    xn = (x * lax.rsqrt(ms + EPS) * gmix_ref[...]).astype(BF16)
    bd = bd_ref[...]
    lane = lax.broadcasted_iota(I32, (tm, LANE), 1)

    def lanes(g):
        return jnp.tile(g, (1, nlt))

    z = _dot(xn, wn_ref[...])
    zt = _dot_nt(wt_ref[...], xn)

    def zs(off, width):
        return z[:, off:off + width]

    cq = zs(N_DCQ, D_Q_RANK)
    cq_ms = jnp.mean(cq * cq, axis=-1, keepdims=True)
    cqn = (cq * lax.rsqrt(cq_ms + EPS) * gqa_ref[...]).astype(BF16)
    ckv = zs(N_DCKV, D_KV_RANK)
    ckv_ms = jnp.mean(ckv * ckv, axis=-1, keepdims=True)
    ckvn = (ckv * lax.rsqrt(ckv_ms + EPS) * gkva_ref[...]).astype(BF16)
    dq = _dot_nt(wuq_ref[...], cqn).reshape(D_HEADS, HEAD_PAD, tm)
    kn = _dot(ckvn, wuk_ref[...])
    dv = _dot_nt(wuvT_ref[...], ckvn).astype(BF16)
    gate = _dot(zs(N_BG, LANE).astype(BF16), wgu_ref[...]) + bgb_ref[...]
    ak = zs(N_AK, GROUP_WIDTH)
    ak_ms = _group_sum(ak * ak, bd) * (1.0 / A_HEAD_DIM)
    kn_ss = _group_sum(kn * kn, bd)

    aq = zt[T_AQ:T_AQ + A_HEADS * HEAD_PAD].reshape(A_HEADS, HEAD_PAD, tm)
    aq_ms = jnp.sum(aq * aq, axis=1, keepdims=True) * (1.0 / A_HEAD_DIM)
    aq = aq * lax.rsqrt(aq_ms + EPS) * lanes(gaq_ref[...])[None] * (A_HEAD_DIM ** -0.5 * LOG2E)
    aqT_ref[0] = aq.reshape(A_HEADS * HEAD_PAD, tm).astype(BF16)
    def emit_key_norms(first_row, sq_norms):
        top = jnp.max(sq_norms, axis=0, keepdims=True)
        for h in range(4):
            knorm_ref[0, 0, first_row + h:first_row + h + 1, :] = jnp.broadcast_to(
                top[:, 64 * h:64 * h + 1], (1, LANE))

    def with_ones_rows(vt, heads, dim):
        ones = jnp.ones((ONES_ROWS, tm), BF16)
        return jnp.concatenate([r for h in range(heads) for r in (vt[dim * h:dim * (h + 1)], ones)], axis=0)

    av = with_ones_rows(zt[T_AV:T_AV + GROUP_WIDTH].astype(BF16), A_HEADS, A_HEAD_DIM)
    for c in range(tm // blk):
        avT_ref[0, c] = av[:, c * blk:(c + 1) * blk]
    iqT_ref[0] = zt[T_IQ:T_IQ + IDX_HEADS * IDX_DIM].astype(BF16)
    iwT_ref[0] = zt[T_IW:T_IW + IDX_HEADS] * ((IDX_HEADS ** -0.5) * (IDX_DIM ** -0.5))

    ak = ak * lax.rsqrt(ak_ms + EPS) * gak_ref[...]
    for h in range(A_HEADS):
        pair = ak[:, LANE * (h // 2):LANE * (h // 2) + LANE]
        if h % 2 == 1:
            pair = pltpu.roll(pair, 64, 1)
        kh = jnp.where(lane < A_HEAD_DIM, pair, 0.0).astype(BF16)
        ak_ref[0, h] = kh
    emit_key_norms(0, _dot((ak * ak).astype(BF16), bd))
    ik_ref[0] = zs(N_IK, LANE)[:, :IDX_DIM].astype(BF16)

    bq_ref[0] = zs(N_BQ, LANE) * (B_KEY_DIM ** -0.5)
    bk_ref[0] = zs(N_BK, LANE)
    bv_ref[0] = zs(N_BV, GROUP_WIDTH).astype(BF16)
    bla_ref[0] = (jnp.minimum(gate, 0.0) - jnp.log(1.0 + jnp.exp(-jnp.abs(gate)))) * (1.0 / B_GATE_TAU)
    br_ref[0] = zs(N_BR, GROUP_WIDTH)

    ca = zs(N_CU, C_CHANNELS)
    cg = zs(N_CU + C_CHANNELS, C_CHANNELS)
    ch_ref[0] = ca * jax.nn.sigmoid(cg)

    dq_ms = jnp.sum(dq * dq, axis=1, keepdims=True) * (1.0 / D_QK)
    dq = dq * lax.rsqrt(dq_ms + EPS) * lanes(gdq_ref[...])[None] * (D_QK ** -0.5 * LOG2E)
    half = D_ROPE // 2
    x1 = dq[:, D_NOPE:D_NOPE + half]
    x2 = dq[:, D_NOPE + half:D_QK]
    cs = cosT_ref[...][None]
    sn = sinT_ref[...][None]
    dq = jnp.concatenate([dq[:, :D_NOPE], x1 * cs - x2 * sn, x2 * cs + x1 * sn, dq[:, D_QK:]], axis=1)
    dqT_ref[0] = dq.reshape(D_HEADS * HEAD_PAD, tm).astype(BF16)

    dv = with_ones_rows(dv, D_HEADS, D_V)
    for c in range(tm // blk):
        dvT_ref[0, c] = dv[:, c * blk:(c + 1) * blk]
    kpe = zs(N_DKPE, LANE)
    ss = kn_ss + jnp.sum(kpe * kpe, axis=-1, keepdims=True)
    rinv = lax.rsqrt(ss * (1.0 / D_QK) + EPS)
    kn = kn * rinv * gdk_ref[...]
    pe = kpe * gdkpe_ref[...]
    partner = jnp.where(lane < half, pltpu.roll(pe, LANE - half, 1), pltpu.roll(pe, half, 1))
    pe = pe * cpe_ref[...] + partner * spe_ref[...]
    pe = pltpu.roll(pe, D_NOPE, 1)
    for h in range(D_HEADS):
        pair = kn[:, LANE * (h // 2):LANE * (h // 2) + LANE]
        rpair = rinv[:, LANE * (h // 2):LANE * (h // 2) + LANE]
        if h % 2 == 1:
            pair = pltpu.roll(pair, 64, 1)
        else:
            rpair = pltpu.roll(rpair, 64, 1)
        kh = jnp.where(lane < D_NOPE, pair, pe * rpair).astype(BF16)
        dk_ref[0, h] = kh
    pe_sq = _dot((pe * pe).astype(BF16), jnp.ones((LANE, LANE), BF16))
    emit_key_norms(A_HEADS, _dot((kn * kn).astype(BF16), bd) + jnp.tile(pe_sq, (1, 2)) * (rinv * rinv))


def _mix_in(x3, p, layer, tm=512, blk=ATT_BLK):
    b, l, d = x3.shape
    tm = min(tm, l)
    grid = (b, l // tm)
    nck = l // blk
    tok = lambda w: pl.BlockSpec((1, tm, w), lambda bi, i: (bi, i, 0))
    tokT = lambda r: pl.BlockSpec((1, r, tm), lambda bi, i: (bi, 0, i))
    headk = pl.BlockSpec((1, 4, tm, HEAD_PAD), lambda bi, i: (bi, 0, i, 0))
    vrows = GROUP_WIDTH + 4 * ONES_ROWS
    chunkT = pl.BlockSpec((1, tm // blk, vrows, blk), lambda bi, i: (bi, i, 0, 0))
    postab = lambda r: pl.BlockSpec((r, tm), lambda bi, i: (0, i))
    posrow = pl.BlockSpec((tm, LANE), lambda bi, i: (i, 0))
    consts = [p["gmix"], p["wn"], p["wt"], p["bd"], p["gaq"], p["gak"], p["wgu"], p["bgb"],
              p["gqa"], p["wuq"], p["gdq"]]
    consts2 = [p["gkva"], p["wuk"], p["wuvT"], p["gdk"], p["gdkpe"]]
    lspec = lambda a: _const_spec(a.shape) if a is p["bd"] else _const_spec(a.shape, layer)
    in_specs = ([tok(d)] + [lspec(a) for a in consts]
                + [postab(D_ROPE // 2), postab(D_ROPE // 2)]
                + [lspec(a) for a in consts2] + [posrow, posrow])
    args = [x3] + consts + [p["cosT"], p["sinT"]] + consts2 + [p["cpe"], p["spe"]]
    sd = jax.ShapeDtypeStruct
    out_shape = [
        sd((b, A_HEADS * HEAD_PAD, l), BF16), sd((b, A_HEADS, l, HEAD_PAD), BF16),
        sd((b, nck, vrows, blk), BF16), sd((b, IDX_HEADS * IDX_DIM, l), BF16),
        sd((b, l, IDX_DIM), BF16), sd((b, IDX_HEADS, l), F32),
        sd((b, l, LANE), F32), sd((b, l, LANE), F32), sd((b, l, GROUP_WIDTH), BF16),
        sd((b, l, LANE), F32), sd((b, l, GROUP_WIDTH), F32),
        sd((b, l, C_CHANNELS), F32),
        sd((b, D_HEADS * HEAD_PAD, l), BF16), sd((b, D_HEADS, l, HEAD_PAD), BF16),
        sd((b, nck, vrows, blk), BF16),
        sd((b, l // tm, A_HEADS + D_HEADS, LANE), F32),
    ]
    out_specs = [
        tokT(A_HEADS * HEAD_PAD), headk, chunkT, tokT(IDX_HEADS * IDX_DIM),
        tok(IDX_DIM), tokT(IDX_HEADS),
        tok(LANE), tok(LANE), tok(GROUP_WIDTH), tok(LANE), tok(GROUP_WIDTH),
        tok(C_CHANNELS),
        tokT(D_HEADS * HEAD_PAD), headk, chunkT,
        pl.BlockSpec((1, 1, A_HEADS + D_HEADS, LANE), lambda bi, i: (bi, i, 0, 0)),
    ]
    return pl.pallas_call(
        functools.partial(_mix_in_body, tm, blk),
        grid=grid, in_specs=in_specs, out_specs=out_specs, out_shape=out_shape,
        compiler_params=_cparams(("parallel", "parallel")),
        name="mix_in",
    )(*args)


def _attn_body(nh, dv, blk, topk, is_dsa, *refs):
    if is_dsa:
        (kb_ref, qT_ref, k_ref, vT_ref, iqT_ref, wT_ref, ik_ref, bt_ref,
         o_ref, m_scr, acc_scr, key_scr, run_scr, plane_scr, active_scr) = refs
    else:
        kb_ref, qT_ref, k_ref, vT_ref, o_ref, m_scr, acc_scr = refs
    vr = dv + ONES_ROWS
    i = pl.program_id(1)
    t = blk
    row = lax.broadcasted_iota(I32, (t, t), 0)
    col = lax.broadcasted_iota(I32, (t, t), 1)
    causal_pen = jnp.where(row <= col, 0.0, NEG)

    m_scr[...] = jnp.full(m_scr.shape, NEG, F32)
    acc_scr[...] = jnp.zeros(acc_scr.shape, F32)

    if is_dsa:
        @pl.when(i == 0)
        def _():
            plane_scr[...] = jnp.zeros(plane_scr.shape, I32)

        def score_chunk(j, diag):
            r0 = pl.multiple_of(j * t, t)
            ikc = ik_ref[0, pl.ds(r0, t), :]
            s = jnp.zeros((t, t), F32)
            for h in range(IDX_HEADS):
                d = _dot(ikc, iqT_ref[0, IDX_DIM * h:IDX_DIM * (h + 1), :])
                s = s + jnp.maximum(d, 0.0) * wT_ref[0, h:h + 1, :]
            bits = lax.bitcast_convert_type(s, I32)
            key = jnp.where(bits < 0, bits ^ 0x7FFFFFFF, bits)
            if diag:
                key = jnp.where(row <= col, key, INT_MIN)
            key_scr[pl.ds(r0, t), :] = key
            w = [key[SUBLANES * r:SUBLANES * (r + 1), :] for r in range(32)]
            step, mask = 16, 0x0000FFFF
            while step:
                for lo in range(32):
                    if lo & step == 0:
                        hi = lo + step
                        swap = (w[lo] ^ jnp.right_shift(w[hi], step)) & mask
                        w[lo] = w[lo] ^ swap
                        w[hi] = w[hi] ^ jnp.left_shift(swap, step)
                step //= 2
                mask ^= (mask << step) & 0xFFFFFFFF
            c0 = pl.multiple_of(j * SUBLANES, SUBLANES)
            w[0] = ~w[0]
            for p in range(32):
                plane_scr[p, pl.ds(c0, SUBLANES), :] = w[p]

        def score_group(u, carry):
            for c in range(SCORE_GROUP):
                score_chunk(SCORE_GROUP * u + c, False)
            return carry

        lax.fori_loop(0, i // SCORE_GROUP, score_group, 0)
        for rem in range(SCORE_GROUP):
            @pl.when(i % SCORE_GROUP == rem)
            def _():
                for c in range(rem):
                    score_chunk(i - rem + c, False)
                score_chunk(i, True)

        def kth_largest(nrow):
            rows = pl.ds(0, nrow)
            in_range = lax.broadcasted_iota(I32, (nrow, t), 0) < (i + 1) * SUBLANES

            def col_count(words):
                pc = lax.population_count(words).reshape(nrow // SUBLANES, SUBLANES, t)
                return jnp.sum(jnp.sum(pc, axis=0), axis=0, keepdims=True)

            def decide(plane, active, n_gt, ans_u):
                ones = col_count(active & plane_scr[plane, rows, :])
                take = n_gt + ones >= topk
                bit = lax.shift_right_logical(jnp.int32(INT_MIN), jnp.int32(plane))
                return (jnp.where(take, 0, -1), n_gt + jnp.where(take, 0, ones),
                        ans_u | jnp.where(take, bit, 0))

            active_scr[rows, :] = jnp.where(in_range, -1, 0)
            state = decide(0, active_scr[rows, :], jnp.zeros((1, t), I32), jnp.zeros((1, t), I32))

            def plane_body(plane, state):
                flip, n_gt, ans_u = state
                active = active_scr[rows, :] & (plane_scr[plane - 1, rows, :] ^ flip)
                active_scr[rows, :] = active
                return decide(plane, active, n_gt, ans_u)

            _, n_gt, ans_u = lax.fori_loop(1, 32, plane_body, state)
            return n_gt, ans_u

        nrow_all = plane_scr.shape[1]
        nrow_half = (nrow_all // SUBLANES // 2) * SUBLANES
        if nrow_half:
            n_gt, ans_u = lax.cond((i + 1) * SUBLANES <= nrow_half,
                                   functools.partial(kth_largest, nrow_half),
                                   functools.partial(kth_largest, nrow_all))
        else:
            n_gt, ans_u = kth_largest(nrow_all)
        ans = ans_u ^ INT_MIN
        need = (topk - n_gt).astype(F32)
        run_scr[...] = jnp.zeros(run_scr.shape, F32)
        stri = jnp.where(col < row, 1.0, 0.0).astype(BF16)

    def logits(h, r0):
        return _dot(k_ref[0, h, pl.ds(r0, t), :], qT_ref[0, HEAD_PAD * h:HEAD_PAD * (h + 1), :])

    def selection_pens(r0s):
        pens = []
        if is_dsa:
            for r0 in r0s:
                kc = key_scr[pl.ds(r0, t), :]
                eq = kc == ans
                eqf = jnp.where(eq, 1.0, 0.0)
                run = run_scr[0:1, :]
                rank = _dot(stri, eqf.astype(BF16)) + run
                run_scr[0:1, :] = run + jnp.sum(eqf, axis=0, keepdims=True)
                pens.append(jnp.where(kc > ans, 0.0, jnp.where(eq, jnp.where(rank < need, 0.0, NEG), NEG)))
        return pens

    b_idx = pl.program_id(0)
    bounds = []
    for h in range(nh):
        qh = qT_ref[0, HEAD_PAD * h:HEAD_PAD * (h + 1), :].astype(F32)
        bound = jnp.sqrt(jnp.sum(qh * qh, axis=0, keepdims=True)) * (kb_ref[b_idx * nh + h] * BOUND_SLACK)
        if is_dsa:
            bound = bound + kb_ref[kb_ref.shape[0] - 1]
        bounds.append(bound)
    bound_max = jnp.max(functools.reduce(jnp.maximum, bounds))
    fixed_ref_ok = bound_max <= FIXED_REF_MAX

    def fixed_ref_chunks(js, kinds):
        n = len(js)
        r0s = [pl.multiple_of(j * t, t) for j in js]
        lgs = [[logits(h, r0s[c]) for c in range(n)] for h in range(nh)]
        pens = selection_pens(r0s)
        ps = []
        for h in range(nh):
            row_ps = []
            if is_dsa:
                far_ref = bounds[h] - bt_ref[2, h, 0:1, :]
            for c in range(n):
                x, ref = lgs[h][c], bounds[h]
                if is_dsa and kinds[c] == "far":
                    x, ref = x + pens[c], far_ref
                elif is_dsa:
                    x = x + (pens[c] + bt_ref[0 if kinds[c] == "diag" else 1, h])
                elif kinds[c] == "diag":
                    x = x + causal_pen
                row_ps.append(jnp.exp2(x - ref).astype(BF16))
            ps.append(row_ps[0] if n == 1 else jnp.concatenate(row_ps, axis=0))
        for h in range(nh):
            vs = slice(vr * h, vr * (h + 1))
            vt = [vT_ref[0, js[c], vs, :] for c in range(n)]
            acc_scr[vs, :] += _dot(vt[0] if n == 1 else jnp.concatenate(vt, axis=1), ps[h])

    def super_chunk(js, kinds):
        n = len(js)
        r0s = [pl.multiple_of(j * t, t) for j in js]
        lgs = [[logits(h, r0s[c]) for c in range(n)] for h in range(nh)]
        pens = selection_pens(r0s)
        ps, alphas = [], []
        for h in range(nh):
            xs = []
            for c, j in enumerate(js):
                diag = kinds[c] == "diag"
                lg = lgs[h][c]
                if is_dsa:
                    tile = 0 if diag else jnp.minimum(i - j, 2)
                    lg = lg + (pens[c] + bt_ref[tile, h])
                elif diag:
                    lg = lg + causal_pen
                xs.append(lg)
            m_old = m_scr[h, 0:1, :]
            m_new = m_old
            for x in xs:
                m_new = jnp.maximum(m_new, jnp.max(x, axis=0, keepdims=True))
            alpha = jnp.exp2(m_old - m_new)
            m_scr[h, 0:1, :] = m_new
            ps.append([jnp.exp2((x - m_new).astype(BF16)) for x in xs])
            alphas.append(alpha)
        for h in range(nh):
            vs = slice(vr * h, vr * (h + 1))
            pv = _dot(vT_ref[0, js[0], vs, :], ps[h][0])
            for c in range(1, n):
                pv = pv + _dot(vT_ref[0, js[c], vs, :], ps[h][c])
            acc_scr[vs, :] = alphas[h] * acc_scr[vs, :] + pv

    tail = ["near", "diag"] if is_dsa else ["diag"]
    nfar = jnp.maximum(i + 1 - len(tail), 0)

    def run_chunks(step):
        def group_body(u, carry):
            step([ATT_GROUP * u + c for c in range(ATT_GROUP)], ["far"] * ATT_GROUP)
            return carry

        lax.fori_loop(0, nfar // ATT_GROUP, group_body, 0)
        for rem in range(ATT_GROUP):
            @pl.when(jnp.logical_and(i + 1 >= len(tail), nfar % ATT_GROUP == rem))
            def _():
                first = i + 1 - len(tail) - rem
                step([first + c for c in range(rem + len(tail))], ["far"] * rem + tail)

        if is_dsa:
            @pl.when(i == 0)
            def _():
                step([i], ["diag"])

    pl.when(fixed_ref_ok)(functools.partial(run_chunks, fixed_ref_chunks))
    pl.when(jnp.logical_not(fixed_ref_ok))(functools.partial(run_chunks, super_chunk))

    outs = []
    for h in range(nh):
        outs.append(acc_scr[vr * h:vr * h + dv, :] / acc_scr[vr * h + dv:vr * h + dv + 1, :])
    o_ref[0] = jnp.transpose(jnp.concatenate(outs, axis=0)).astype(o_ref.dtype)


def _attention(qT, k, vT, kbound, dsa=None, blk=ATT_BLK):
    b, nh, l, _ = k.shape
    vr = vT.shape[2] // nh
    dv = vr - ONES_ROWS
    grid = (b, l // blk)
    qspec = lambda r: pl.BlockSpec((1, r, blk), lambda bi, i: (bi, 0, i))
    kspec = pl.BlockSpec((1, nh, l, HEAD_PAD), lambda bi, i: (bi, 0, 0, 0))
    vspec = pl.BlockSpec((1, l // blk, nh * vr, blk), lambda bi, i: (bi, 0, 0, 0))
    in_specs = [pl.BlockSpec(memory_space=pltpu.SMEM), qspec(nh * HEAD_PAD), kspec, vspec]
    args = [kbound, qT, k, vT]
    scratch = [pltpu.VMEM((nh, 8, blk), F32), pltpu.VMEM((nh * vr, blk), F32)]
    topk = 0
    if dsa is not None:
        iqT, wT, ik, bt = dsa
        topk = min(TOPK_MAX, l // 4)
        in_specs += [qspec(IDX_HEADS * IDX_DIM), qspec(IDX_HEADS),
                     pl.BlockSpec((1, l, IDX_DIM), lambda bi, i: (bi, 0, 0)),
                     _const_spec(bt.shape)]
        args += [iqT, wT, ik, bt]
        assert blk == 32 * SUBLANES
        nrow = (l // blk) * SUBLANES
        scratch += [pltpu.VMEM((l, blk), I32), pltpu.VMEM((8, blk), F32),
                    pltpu.VMEM((32, nrow, blk), I32), pltpu.VMEM((nrow, blk), I32)]
    return pl.pallas_call(
        functools.partial(_attn_body, nh, dv, blk, topk, dsa is not None),
        grid=grid, in_specs=in_specs,
        out_specs=pl.BlockSpec((1, blk, nh * dv), lambda bi, i: (bi, i, 0)),
        out_shape=jax.ShapeDtypeStruct((b, l, nh * dv), BF16),
        scratch_shapes=scratch,
        compiler_params=_cparams(("parallel", "arbitrary")),
        name="dsa_attn" if dsa is not None else "mla_attn",
    )(*args)


def _gla_body(tg, q_ref, k_ref, v_ref, la_ref, r_ref, go_ref, bd_ref, o_ref, st_scr, o_scr):
    @pl.when(pl.program_id(1) == 0)
    def _():
        st_scr[...] = jnp.zeros(st_scr.shape, F32)

    cs = B_CHUNK
    la = la_ref[0]
    rl = lax.broadcasted_iota(I32, (tg, LANE), 0) & (cs - 1)
    b = la
    s = 1
    while s < cs:
        b = b + jnp.where(rl >= s, pltpu.roll(b, s, 0), 0.0)
        s *= 2
    q = q_ref[0]
    k = k_ref[0]
    qb = q * jnp.exp(b)
    tri = lax.broadcasted_iota(I32, (cs, cs), 1) <= lax.broadcasted_iota(I32, (cs, cs), 0)
    khead = lax.broadcasted_iota(I32, (cs, LANE), 1) // B_KEY_DIM
    vhead = lax.broadcasted_iota(I32, (cs, GROUP_WIDTH), 1) // B_VAL_DIM
    same_head = (lax.broadcasted_iota(I32, (GROUP_WIDTH, LANE), 0) // B_VAL_DIM
                 == lax.broadcasted_iota(I32, (GROUP_WIDTH, LANE), 1) // B_KEY_DIM)
    for c in range(tg // cs):
        sl = slice(c * cs, (c + 1) * cs)
        bc = b[sl]
        mid = bc[cs // 2:cs // 2 + 1]
        last = bc[cs - 1:cs]
        qe = q[sl] * jnp.exp(bc - mid)
        ke = (k[sl] * jnp.exp(mid - bc)).astype(BF16)
        kd = (k[sl] * jnp.exp(last - bc)).astype(BF16)
        qbc = qb[sl].astype(BF16)
        vc = v_ref[0, sl, :]
        st = st_scr[...]
        qe4 = jnp.concatenate([jnp.where(khead == h, qe, 0.0) for h in range(B_HEADS)], axis=0)
        a_all = _dot_nt(qe4.astype(BF16), ke)
        o = _dot_nt(qbc, st.astype(BF16))
        for h in range(B_HEADS):
            a = jnp.where(tri, a_all[cs * h:cs * (h + 1)], 0.0).astype(BF16)
            o = o + _dot(a, jnp.where(vhead == h, vc, jnp.zeros_like(vc)))
        o_scr[sl, :] = o
        st_scr[...] = st * jnp.exp(last) + jnp.where(same_head, _dot_tn(vc, kd), 0.0)
    o = o_scr[...]
    ms = _group_sum(o * o, bd_ref[...]) * (1.0 / B_VAL_DIM)
    r = r_ref[0]
    o_ref[0] = (o * lax.rsqrt(ms + EPS) * go_ref[...] * (r * jax.nn.sigmoid(r))).astype(o_ref.dtype)


def _gla(bq, bk, bv, bla, br, go, bd, layer, tg=1024):
    b, l, _ = bq.shape
    tg = min(tg, l)
    tok = lambda w: pl.BlockSpec((1, tg, w), lambda bi, i: (bi, i, 0))
    return pl.pallas_call(
        functools.partial(_gla_body, tg),
        grid=(b, l // tg),
        in_specs=[tok(LANE), tok(LANE), tok(GROUP_WIDTH), tok(LANE), tok(GROUP_WIDTH),
                  _const_spec(go.shape, layer), _const_spec(bd.shape)],
        out_specs=tok(GROUP_WIDTH),
        out_shape=jax.ShapeDtypeStruct((b, l, GROUP_WIDTH), BF16),
        scratch_shapes=[pltpu.VMEM((B_HEADS * B_VAL_DIM, B_HEADS * B_KEY_DIM), F32),
                        pltpu.VMEM((tg, GROUP_WIDTH), F32)],
        compiler_params=_cparams(("parallel", "arbitrary")),
        name="gla",
    )(bq, bk, bv, bla, br, go, bd)


CONV_HIST = 32


def _conv_body(tc, h_ref, w_ref, b_ref, g_ref, o_ref, buf):
    @pl.when(pl.program_id(1) == 0)
    def _():
        buf[0:CONV_HIST, :] = jnp.zeros((CONV_HIST, C_CHANNELS), F32)

    @pl.when(pl.program_id(1) > 0)
    def _():
        buf[0:CONV_HIST, :] = buf[tc:tc + CONV_HIST, :]

    buf[CONV_HIST:CONV_HIST + tc, :] = h_ref[0]
    acc = jnp.zeros((tc, C_CHANNELS), F32) + b_ref[...]
    base = CONV_HIST - (C_KERNEL - 1)
    hb = buf[...]
    rows = tc + CONV_HIST
    for r in range(SUBLANES):
        shifted = hb if r == 0 else pltpu.roll(hb, rows - r, 0)
        for j in range(C_KERNEL):
            if (base + j) % SUBLANES == r:
                a0 = base + j - r
                acc = acc + shifted[a0:a0 + tc, :] * w_ref[j:j + 1, :]
    ms = jnp.mean(acc * acc, axis=-1, keepdims=True)
    y = acc * lax.rsqrt(ms + EPS) * g_ref[...]
    o_ref[0] = (y * jax.nn.sigmoid(y)).astype(o_ref.dtype)


def _conv(ch, w, bias, g, layer, tc=512):
    b, l, c = ch.shape
    tc = min(tc, l)
    tok = pl.BlockSpec((1, tc, c), lambda bi, i: (bi, i, 0))
    return pl.pallas_call(
        functools.partial(_conv_body, tc),
        grid=(b, l // tc),
        in_specs=[tok] + [_const_spec(a.shape, layer) for a in (w, bias, g)],
        out_specs=tok,
        out_shape=jax.ShapeDtypeStruct((b, l, c), BF16),
        scratch_shapes=[pltpu.VMEM((tc + CONV_HIST, c), F32)],
        compiler_params=_cparams(("parallel", "arbitrary")),
        name="conv",
    )(ch, w, bias, g)


def _t5_bucket(dist):
    max_exact = REL_BUCKETS // 2
    d = jnp.maximum(dist, 0)
    df = jnp.maximum(d, 1).astype(F32)
    large = max_exact + (jnp.log(df / max_exact) / math.log(REL_MAX_DIST / max_exact)
                         * (REL_BUCKETS - max_exact)).astype(I32)
    large = jnp.minimum(large, REL_BUCKETS - 1)
    return jnp.where(d < max_exact, d, large)


def _pad_cols(w, width):
    return jnp.pad(w, ((0, 0), (0, width - w.shape[1])))


def _lane_rep(v):
    return jnp.broadcast_to(v[:, None], (v.shape[0], LANE))


def _pad_heads_rows(w, heads, dim):
    w = w.reshape(heads, dim, w.shape[1])
    return jnp.pad(w, ((0, 0), (0, HEAD_PAD - dim), (0, 0))).reshape(heads * HEAD_PAD, -1)


def _split_w_in(w_in):
    widths = (256, 256, 256, 256, 32, 8, 128, 128, 256, 16, 256, 512, 256, 128, 32)
    offs = np.cumsum((0,) + widths)
    return [w_in[:, offs[n]:offs[n + 1]] for n in range(len(widths))]


def _row(v):
    return v[None, :].astype(F32)


def _one_layer_params(w):
    (aq, ak, av, iq, ik, iw, bq, bk, bv, bg, br, cu, dcq, dckv, dkpe) = _split_w_in(w["w_in"])
    wn = jnp.concatenate([ak, _pad_cols(ik, LANE), bq, bk, bv, _pad_cols(bg, LANE), br, cu, dcq, dckv,
                          _pad_cols(dkpe, LANE)], axis=1).astype(BF16)
    wt = jnp.concatenate([_pad_heads_rows(aq.T, A_HEADS, A_HEAD_DIM), av.T, iq.T,
                          jnp.pad(iw.T, ((0, T_TOT - T_IW - IDX_HEADS), (0, 0)))], axis=0).astype(BF16)
    pad_to = lambda v, n: jnp.pad(v, (0, n - v.shape[0]))
    ukv = w["d_ukv"].reshape(D_KV_RANK, D_HEADS, D_NOPE + D_V)
    wuk = ukv[:, :, :D_NOPE].reshape(D_KV_RANK, D_HEADS * D_NOPE)
    wuv = ukv[:, :, D_NOPE:].reshape(D_KV_RANK, D_HEADS * D_V)
    gdk = w["d_k_norm"]
    return dict(
        gmix=_row(w["mix_norm"]), wn=wn, wt=wt,
        gaq=_lane_rep(pad_to(w["a_q_norm"], HEAD_PAD)),
        gak=_row(jnp.tile(w["a_k_norm"], A_HEADS)),
        wgu=jnp.pad(w["b_gate_up"], ((0, LANE - B_GATE_RANK), (0, 0))).astype(BF16),
        bgb=_row(w["b_gate_bias"]),
        gqa=_row(w["d_qa_norm"]),
        wuq=_pad_heads_rows(w["d_uq"].T, D_HEADS, D_QK).astype(BF16),
        gdq=_lane_rep(pad_to(w["d_q_norm"], HEAD_PAD)),
        gkva=_row(w["d_kva_norm"]), wuk=wuk.astype(BF16), wuvT=wuv.T.astype(BF16),
        gdk=_row(jnp.tile(gdk[:D_NOPE], D_HEADS)), gdkpe=_row(pad_to(gdk[D_NOPE:], LANE)),
        gbo=_row(jnp.tile(w["b_out_norm"], B_HEADS)),
        cw=jnp.pad(w["c_dw_w"][:, 0, :], ((0, CONV_HIST - C_KERNEL), (0, 0))).astype(F32),
        cb=_row(w["c_dw_b"]), cg=_row(w["c_norm"]),
    )


def _shared_tables(seq):
    hid = np.arange(GROUP_WIDTH) // 64
    bd = jnp.asarray(hid[:, None] == hid[None, :], dtype=BF16)
    half = D_ROPE // 2
    freqs = ROPE_THETA ** (-jnp.arange(half, dtype=F32) / half)
    ang = jnp.arange(seq).astype(F32)[:, None] * freqs[None, :]
    cos, sin = jnp.cos(ang), jnp.sin(ang)
    zeros = jnp.zeros((seq, LANE - D_ROPE), F32)
    cpe = jnp.concatenate([cos, cos, zeros], axis=1)
    spe = jnp.concatenate([-sin, sin, zeros], axis=1)
    return dict(bd=bd, cosT=cos.T, sinT=sin.T, cpe=cpe, spe=spe)


def _bias_tiles(rel_bias, blk):
    assert REL_MAX_DIST <= blk + 1
    kk = jnp.arange(blk)[:, None]
    qq = jnp.arange(blk)[None, :]
    rb = rel_bias.astype(F32).T

    def lookup(bucket):
        onehot = bucket[None, :, :, None] == jnp.arange(REL_BUCKETS)
        return jnp.sum(jnp.where(onehot, rb[:, None, None, :], 0.0), axis=-1) * LOG2E

    d0 = jnp.where(kk <= qq, lookup(_t5_bucket(qq - kk)), NEG)
    d1 = lookup(_t5_bucket(blk + qq - kk))
    far = jnp.broadcast_to(lookup(_t5_bucket(jnp.full((1, 1), 2 * blk, I32))), d1.shape)
    return jnp.stack([d0, d1, far])


def _key_bounds(knorm, bias_max):
    km = jnp.sqrt(jnp.max(knorm[..., 0], axis=1))
    kb_a = jnp.concatenate([km[:, :A_HEADS].reshape(-1), bias_max.reshape(1)])
    return kb_a, km[:, A_HEADS:].reshape(-1)


def kernel(x, ffn1_norm, ffn1_gate, ffn1_up, ffn1_down, mix_norm, w_in, a_q_norm, a_k_norm, rel_bias,
           b_gate_up, b_gate_bias, b_out_norm, c_dw_w, c_dw_b, c_norm, d_qa_norm, d_uq, d_kva_norm,
           d_ukv, d_q_norm, d_k_norm, w_out, ffn2_norm, ffn2_gate, ffn2_up, ffn2_down):
    w = dict(mix_norm=mix_norm, w_in=w_in, a_q_norm=a_q_norm, a_k_norm=a_k_norm, b_gate_up=b_gate_up,
             b_gate_bias=b_gate_bias, b_out_norm=b_out_norm, c_dw_w=c_dw_w, c_dw_b=c_dw_b, c_norm=c_norm,
             d_qa_norm=d_qa_norm, d_uq=d_uq, d_kva_norm=d_kva_norm, d_ukv=d_ukv,
             d_q_norm=d_q_norm, d_k_norm=d_k_norm)
    bsz, seq, dm = x.shape
    depth = w_in.shape[0]
    blk = min(ATT_BLK, seq)
    bt = _bias_tiles(rel_bias, blk)
    bias_max = jnp.max(bt)
    p = {**jax.vmap(_one_layer_params)(w), **_shared_tables(seq)}
    stacked_row = lambda v: v[:, None, :].astype(F32)
    ffn1 = (stacked_row(ffn1_norm), ffn1_gate, ffn1_up, ffn1_down)
    ffn2 = (stacked_row(ffn2_norm), ffn2_gate, ffn2_up, ffn2_down)
    wo = w_out.astype(BF16)
    x2 = x.reshape(bsz * seq, dm)
    for l in range(depth):
        x2 = _ffn(x2, *ffn1, l)
        (aqT, akh, avT, iqT, aik, iwT, bq, bk, bv, bla, br, ch, dqT, dkh, dvT, knorm) = _mix_in(
            x2.reshape(bsz, seq, dm), p, l, blk=blk)
        kb_a, kb_d = _key_bounds(knorm, bias_max)
        y_a = _attention(aqT, akh, avT, kb_a, dsa=(iqT, iwT, aik, bt), blk=blk)
        y_b = _gla(bq, bk, bv, bla, br, p["gbo"], p["bd"], l)
        y_c = _conv(ch, p["cw"], p["cb"], p["cg"], l)
        y_d = _attention(dqT, dkh, dvT, kb_d, blk=blk)
        ys = [y.reshape(bsz * seq, GROUP_WIDTH) for y in (y_a, y_b, y_c, y_d)]
        x2 = _ffn(x2, *ffn2, l, mix=(ys, wo))
    return x2.reshape(bsz, seq, dm)
```

```python
import functools
import math

import jax
import jax.numpy as jnp
import numpy as np
from jax import lax
from jax.experimental import pallas as pl
from jax.experimental.pallas import tpu as pltpu

F32 = jnp.float32
BF16 = jnp.bfloat16
I32 = jnp.int32

EPS = 1e-6
GROUP_WIDTH = 256
A_HEADS, A_HEAD_DIM = 4, 64
IDX_HEADS, IDX_DIM = 8, 32
TOPK_MAX = 256
REL_BUCKETS, REL_MAX_DIST = 32, 128
B_HEADS, B_KEY_DIM, B_VAL_DIM, B_GATE_RANK = 4, 32, 64, 16
B_GATE_TAU = 16.0
B_CHUNK = 64
C_CHANNELS, C_KERNEL = 256, 31
D_HEADS, D_Q_RANK, D_KV_RANK, D_NOPE, D_ROPE, D_V = 4, 256, 128, 64, 32, 64
D_QK = D_NOPE + D_ROPE
ROPE_THETA = 10000.0

LANE = 128
SUBLANES = 8
HEAD_PAD = 128
ONES_ROWS = 16
ATT_BLK = 256
ATT_GROUP = 4
SCORE_GROUP = 4
BOUND_SLACK = 1.01
FIXED_REF_MAX = 40.0
INT_MIN = -2 ** 31
NEG = -1e30
LOG2E = math.log2(math.e)
VMEM_LIMIT = 56 * 1024 * 1024

N_AK, N_IK, N_BQ, N_BK, N_BV, N_BG, N_BR, N_CU, N_DCQ, N_DCKV, N_DKPE, N_TOT = (
    0, 256, 384, 512, 640, 896, 1024, 1280, 1792, 2048, 2176, 2304)
T_AQ, T_AV, T_IQ, T_IW, T_TOT = 0, 512, 768, 1024, 1040


def _dot(a, b):
    return jnp.dot(a, b, preferred_element_type=F32)


def _dot_nt(a, b):
    return lax.dot_general(a, b, (((1,), (1,)), ((), ())), preferred_element_type=F32)


def _dot_tn(a, b):
    return lax.dot_general(a, b, (((0,), (0,)), ((), ())), preferred_element_type=F32)


def _group_sum(x2, bd):
    hi = x2.astype(BF16)
    lo = (x2 - hi.astype(F32)).astype(BF16)
    return _dot(hi, bd) + _dot(lo, bd)


def _const_spec(shape, layer=None):
    nd = len(shape)
    if layer is None:
        return pl.BlockSpec(shape, lambda *_: (0,) * nd, pipeline_mode=pl.Buffered(1))
    return pl.BlockSpec((None,) + tuple(shape[1:]), lambda *_: (layer,) + (0,) * (nd - 1),
                        pipeline_mode=pl.Buffered(1))


def _cparams(sem):
    return pltpu.CompilerParams(dimension_semantics=sem, vmem_limit_bytes=VMEM_LIMIT)


def _ffn_body(has_mix, fc, *refs):
    if has_mix:
        x_ref, ya, yb, yc, yd, wo_ref, g_ref, wg_ref, wu_ref, wd_ref, o_ref, h_scr = refs
    else:
        x_ref, g_ref, wg_ref, wu_ref, wd_ref, o_ref, h_scr = refs
    x = x_ref[...]
    if has_mix:
        y = jnp.concatenate([ya[...], yb[...], yc[...], yd[...]], axis=-1)
        x = x + _dot(y, wo_ref[...])
    ms = jnp.mean(x * x, axis=-1, keepdims=True)
    xn = (x * lax.rsqrt(ms + EPS) * g_ref[...]).astype(BF16)
    d_ff = wg_ref.shape[1]
    for c in range(d_ff // fc):
        sl = slice(c * fc, (c + 1) * fc)
        gate = _dot(xn, wg_ref[:, sl].astype(BF16))
        up = _dot(xn, wu_ref[:, sl].astype(BF16))
        h_scr[:, sl] = (gate * jax.nn.sigmoid(gate) * up).astype(BF16)
    o_ref[...] = x + 0.5 * _dot(h_scr[...], wd_ref[...].astype(BF16))


def _ffn(x2, g, wg, wu, wd, layer, mix=None, tm=512, fc=256):
    m, d = x2.shape
    d_ff = wg.shape[2]
    tm = min(tm, m)
    row = lambda w: pl.BlockSpec((tm, w), lambda i: (i, 0))
    in_specs = [row(d)]
    args = [x2]
    if mix is not None:
        ys, wo = mix
        in_specs += [row(GROUP_WIDTH)] * 4 + [_const_spec(wo.shape, layer)]
        args += list(ys) + [wo]
    in_specs += [_const_spec(a.shape, layer) for a in (g, wg, wu, wd)]
    args += [g, wg, wu, wd]
    return pl.pallas_call(
        functools.partial(_ffn_body, mix is not None, fc),
        grid=(m // tm,),
        in_specs=in_specs,
        out_specs=row(d),
        out_shape=jax.ShapeDtypeStruct((m, d), F32),
        scratch_shapes=[pltpu.VMEM((tm, d_ff), BF16)],
        compiler_params=_cparams(("parallel",)),
        name="ffn_mix" if mix is not None else "ffn",
    )(*args)


def _mix_in_body(tm, blk,
                 x_ref, gmix_ref, wn_ref, wt_ref, bd_ref,
                 gaq_ref, gak_ref,
                 wgu_ref, bgb_ref,
                 gqa_ref, wuq_ref, gdq_ref, cosT_ref, sinT_ref,
                 gkva_ref, wuk_ref, wuvT_ref, gdk_ref, gdkpe_ref, cpe_ref, spe_ref,
                 aqT_ref, ak_ref, avT_ref, iqT_ref, ik_ref, iwT_ref,
                 bq_ref, bk_ref, bv_ref, bla_ref, br_ref,
                 ch_ref,
                 dqT_ref, dk_ref, dvT_ref, knorm_ref):
    nlt = tm // LANE
    x = x_ref[0]
    ms = jnp.mean(x * x, axis=-1, keepdims=True)
    xn = (x * lax.rsqrt(ms + EPS) * gmix_ref[...]).astype(BF16)
    bd = bd_ref[...]
    lane = lax.broadcasted_iota(I32, (tm, LANE), 1)

    def lanes(g):
        return jnp.tile(g, (1, nlt))

    z = _dot(xn, wn_ref[...])
    zt = _dot_nt(wt_ref[...], xn)

    def zs(off, width):
        return z[:, off:off + width]

    cq = zs(N_DCQ, D_Q_RANK)
    cq_ms = jnp.mean(cq * cq, axis=-1, keepdims=True)
    cqn = (cq * lax.rsqrt(cq_ms + EPS) * gqa_ref[...]).astype(BF16)
    ckv = zs(N_DCKV, D_KV_RANK)
    ckv_ms = jnp.mean(ckv * ckv, axis=-1, keepdims=True)
    ckvn = (ckv * lax.rsqrt(ckv_ms + EPS) * gkva_ref[...]).astype(BF16)
    dq = _dot_nt(wuq_ref[...], cqn).reshape(D_HEADS, HEAD_PAD, tm)
    kn = _dot(ckvn, wuk_ref[...])
    dv = _dot_nt(wuvT_ref[...], ckvn).astype(BF16)
    gate = _dot(zs(N_BG, LANE).astype(BF16), wgu_ref[...]) + bgb_ref[...]
    ak = zs(N_AK, GROUP_WIDTH)
    ak_ms = _group_sum(ak * ak, bd) * (1.0 / A_HEAD_DIM)
    kn_ss = _group_sum(kn * kn, bd)

    aq = zt[T_AQ:T_AQ + A_HEADS * HEAD_PAD].reshape(A_HEADS, HEAD_PAD, tm)
    aq_ms = jnp.sum(aq * aq, axis=1, keepdims=True) * (1.0 / A_HEAD_DIM)
    aq = aq * lax.rsqrt(aq_ms + EPS) * lanes(gaq_ref[...])[None] * (A_HEAD_DIM ** -0.5 * LOG2E)
    aqT_ref[0] = aq.reshape(A_HEADS * HEAD_PAD, tm).astype(BF16)
    def emit_key_norms(first_row, sq_norms):
        top = jnp.max(sq_norms, axis=0, keepdims=True)
        for h in range(4):
            knorm_ref[0, 0, first_row + h:first_row + h + 1, :] = jnp.broadcast_to(
                top[:, 64 * h:64 * h + 1], (1, LANE))

    def with_ones_rows(vt, heads, dim):
        ones = jnp.ones((ONES_ROWS, tm), BF16)
        return jnp.concatenate([r for h in range(heads) for r in (vt[dim * h:dim * (h + 1)], ones)], axis=0)

    av = with_ones_rows(zt[T_AV:T_AV + GROUP_WIDTH].astype(BF16), A_HEADS, A_HEAD_DIM)
    for c in range(tm // blk):
        avT_ref[0, c] = av[:, c * blk:(c + 1) * blk]
    iqT_ref[0] = zt[T_IQ:T_IQ + IDX_HEADS * IDX_DIM].astype(BF16)
    iwT_ref[0] = zt[T_IW:T_IW + IDX_HEADS] * ((IDX_HEADS ** -0.5) * (IDX_DIM ** -0.5))

    ak = ak * lax.rsqrt(ak_ms + EPS) * gak_ref[...]
    for h in range(A_HEADS):
        pair = ak[:, LANE * (h // 2):LANE * (h // 2) + LANE]
        if h % 2 == 1:
            pair = pltpu.roll(pair, 64, 1)
        kh = jnp.where(lane < A_HEAD_DIM, pair, 0.0).astype(BF16)
        ak_ref[0, h] = kh
    emit_key_norms(0, _dot((ak * ak).astype(BF16), bd))
    ik_ref[0] = zs(N_IK, LANE)[:, :IDX_DIM].astype(BF16)

    bq_ref[0] = zs(N_BQ, LANE) * (B_KEY_DIM ** -0.5)
    bk_ref[0] = zs(N_BK, LANE)
    bv_ref[0] = zs(N_BV, GROUP_WIDTH).astype(BF16)
    bla_ref[0] = (jnp.minimum(gate, 0.0) - jnp.log(1.0 + jnp.exp(-jnp.abs(gate)))) * (1.0 / B_GATE_TAU)
    br_ref[0] = zs(N_BR, GROUP_WIDTH)

    ca = zs(N_CU, C_CHANNELS)
    cg = zs(N_CU + C_CHANNELS, C_CHANNELS)
    ch_ref[0] = ca * jax.nn.sigmoid(cg)

    dq_ms = jnp.sum(dq * dq, axis=1, keepdims=True) * (1.0 / D_QK)
    dq = dq * lax.rsqrt(dq_ms + EPS) * lanes(gdq_ref[...])[None] * (D_QK ** -0.5 * LOG2E)
    half = D_ROPE // 2
    x1 = dq[:, D_NOPE:D_NOPE + half]
    x2 = dq[:, D_NOPE + half:D_QK]
    cs = cosT_ref[...][None]
    sn = sinT_ref[...][None]
    dq = jnp.concatenate([dq[:, :D_NOPE], x1 * cs - x2 * sn, x2 * cs + x1 * sn, dq[:, D_QK:]], axis=1)
    dqT_ref[0] = dq.reshape(D_HEADS * HEAD_PAD, tm).astype(BF16)

    dv = with_ones_rows(dv, D_HEADS, D_V)
    for c in range(tm // blk):
        dvT_ref[0, c] = dv[:, c * blk:(c + 1) * blk]
    kpe = zs(N_DKPE, LANE)
    ss = kn_ss + jnp.sum(kpe * kpe, axis=-1, keepdims=True)
    rinv = lax.rsqrt(ss * (1.0 / D_QK) + EPS)
    kn = kn * rinv * gdk_ref[...]
    pe = kpe * gdkpe_ref[...]
    partner = jnp.where(lane < half, pltpu.roll(pe, LANE - half, 1), pltpu.roll(pe, half, 1))
    pe = pe * cpe_ref[...] + partner * spe_ref[...]
    pe = pltpu.roll(pe, D_NOPE, 1)
    for h in range(D_HEADS):
        pair = kn[:, LANE * (h // 2):LANE * (h // 2) + LANE]
        rpair = rinv[:, LANE * (h // 2):LANE * (h // 2) + LANE]
        if h % 2 == 1:
            pair = pltpu.roll(pair, 64, 1)
        else:
            rpair = pltpu.roll(rpair, 64, 1)
        kh = jnp.where(lane < D_NOPE, pair, pe * rpair).astype(BF16)
        dk_ref[0, h] = kh
    pe_sq = _dot((pe * pe).astype(BF16), jnp.ones((LANE, LANE), BF16))
    emit_key_norms(A_HEADS, _dot((kn * kn).astype(BF16), bd) + jnp.tile(pe_sq, (1, 2)) * (rinv * rinv))


def _mix_in(x3, p, layer, tm=512, blk=ATT_BLK):
    b, l, d = x3.shape
    tm = min(tm, l)
    grid = (b, l // tm)
    nck = l // blk
    tok = lambda w: pl.BlockSpec((1, tm, w), lambda bi, i: (bi, i, 0))
    tokT = lambda r: pl.BlockSpec((1, r, tm), lambda bi, i: (bi, 0, i))
    headk = pl.BlockSpec((1, 4, tm, HEAD_PAD), lambda bi, i: (bi, 0, i, 0))
    vrows = GROUP_WIDTH + 4 * ONES_ROWS
    chunkT = pl.BlockSpec((1, tm // blk, vrows, blk), lambda bi, i: (bi, i, 0, 0))
    postab = lambda r: pl.BlockSpec((r, tm), lambda bi, i: (0, i))
    posrow = pl.BlockSpec((tm, LANE), lambda bi, i: (i, 0))
    consts = [p["gmix"], p["wn"], p["wt"], p["bd"], p["gaq"], p["gak"], p["wgu"], p["bgb"],
              p["gqa"], p["wuq"], p["gdq"]]
    consts2 = [p["gkva"], p["wuk"], p["wuvT"], p["gdk"], p["gdkpe"]]
    lspec = lambda a: _const_spec(a.shape) if a is p["bd"] else _const_spec(a.shape, layer)
    in_specs = ([tok(d)] + [lspec(a) for a in consts]
                + [postab(D_ROPE // 2), postab(D_ROPE // 2)]
                + [lspec(a) for a in consts2] + [posrow, posrow])
    args = [x3] + consts + [p["cosT"], p["sinT"]] + consts2 + [p["cpe"], p["spe"]]
    sd = jax.ShapeDtypeStruct
    out_shape = [
        sd((b, A_HEADS * HEAD_PAD, l), BF16), sd((b, A_HEADS, l, HEAD_PAD), BF16),
        sd((b, nck, vrows, blk), BF16), sd((b, IDX_HEADS * IDX_DIM, l), BF16),
        sd((b, l, IDX_DIM), BF16), sd((b, IDX_HEADS, l), F32),
        sd((b, l, LANE), F32), sd((b, l, LANE), F32), sd((b, l, GROUP_WIDTH), BF16),
        sd((b, l, LANE), F32), sd((b, l, GROUP_WIDTH), F32),
        sd((b, l, C_CHANNELS), F32),
        sd((b, D_HEADS * HEAD_PAD, l), BF16), sd((b, D_HEADS, l, HEAD_PAD), BF16),
        sd((b, nck, vrows, blk), BF16),
        sd((b, l // tm, A_HEADS + D_HEADS, LANE), F32),
    ]
    out_specs = [
        tokT(A_HEADS * HEAD_PAD), headk, chunkT, tokT(IDX_HEADS * IDX_DIM),
        tok(IDX_DIM), tokT(IDX_HEADS),
        tok(LANE), tok(LANE), tok(GROUP_WIDTH), tok(LANE), tok(GROUP_WIDTH),
        tok(C_CHANNELS),
        tokT(D_HEADS * HEAD_PAD), headk, chunkT,
        pl.BlockSpec((1, 1, A_HEADS + D_HEADS, LANE), lambda bi, i: (bi, i, 0, 0)),
    ]
    return pl.pallas_call(
        functools.partial(_mix_in_body, tm, blk),
        grid=grid, in_specs=in_specs, out_specs=out_specs, out_shape=out_shape,
        compiler_params=_cparams(("parallel", "parallel")),
        name="mix_in",
    )(*args)


def _attn_body(nh, dv, blk, topk, is_dsa, *refs):
    if is_dsa:
        (kb_ref, qT_ref, k_ref, vT_ref, iqT_ref, wT_ref, ik_ref, bt_ref,
         o_ref, m_scr, acc_scr, key_scr, run_scr, plane_scr, active_scr) = refs
    else:
        kb_ref, qT_ref, k_ref, vT_ref, o_ref, m_scr, acc_scr = refs
    vr = dv + ONES_ROWS
    i = pl.program_id(1)
    t = blk
    row = lax.broadcasted_iota(I32, (t, t), 0)
    col = lax.broadcasted_iota(I32, (t, t), 1)
    causal_pen = jnp.where(row <= col, 0.0, NEG)

    m_scr[...] = jnp.full(m_scr.shape, NEG, F32)
    acc_scr[...] = jnp.zeros(acc_scr.shape, F32)

    if is_dsa:
        @pl.when(i == 0)
        def _():
            plane_scr[...] = jnp.zeros(plane_scr.shape, I32)

        def score_chunk(j, diag):
            r0 = pl.multiple_of(j * t, t)
            ikc = ik_ref[0, pl.ds(r0, t), :]
            s = jnp.zeros((t, t), F32)
            for h in range(IDX_HEADS):
                d = _dot(ikc, iqT_ref[0, IDX_DIM * h:IDX_DIM * (h + 1), :])
                s = s + jnp.maximum(d, 0.0) * wT_ref[0, h:h + 1, :]
            bits = lax.bitcast_convert_type(s, I32)
            key = jnp.where(bits < 0, bits ^ 0x7FFFFFFF, bits)
            if diag:
                key = jnp.where(row <= col, key, INT_MIN)
            key_scr[pl.ds(r0, t), :] = key
            w = [key[SUBLANES * r:SUBLANES * (r + 1), :] for r in range(32)]
            step, mask = 16, 0x0000FFFF
            while step:
                for lo in range(32):
                    if lo & step == 0:
                        hi = lo + step
                        swap = (w[lo] ^ jnp.right_shift(w[hi], step)) & mask
                        w[lo] = w[lo] ^ swap
                        w[hi] = w[hi] ^ jnp.left_shift(swap, step)
                step //= 2
                mask ^= (mask << step) & 0xFFFFFFFF
            c0 = pl.multiple_of(j * SUBLANES, SUBLANES)
            w[0] = ~w[0]
            for p in range(32):
                plane_scr[p, pl.ds(c0, SUBLANES), :] = w[p]

        def score_group(u, carry):
            for c in range(SCORE_GROUP):
                score_chunk(SCORE_GROUP * u + c, False)
            return carry

        lax.fori_loop(0, i // SCORE_GROUP, score_group, 0)

        def score_one(j, carry):
            score_chunk(j, False)
            return carry

        lax.fori_loop(i - i % SCORE_GROUP, i, score_one, 0)
        score_chunk(i, True)

        def kth_largest(nrow):
            rows = pl.ds(0, nrow)
            in_range = lax.broadcasted_iota(I32, (nrow, t), 0) < (i + 1) * SUBLANES

            def col_count(words):
                pc = lax.population_count(words).reshape(nrow // SUBLANES, SUBLANES, t)
                return jnp.sum(jnp.sum(pc, axis=0), axis=0, keepdims=True)

            def decide(plane, active, n_gt, ans_u):
                ones = col_count(active & plane_scr[plane, rows, :])
                take = n_gt + ones >= topk
                bit = lax.shift_right_logical(jnp.int32(INT_MIN), jnp.int32(plane))
                return (jnp.where(take, 0, -1), n_gt + jnp.where(take, 0, ones),
                        ans_u | jnp.where(take, bit, 0))

            active_scr[rows, :] = jnp.where(in_range, -1, 0)
            state = decide(0, active_scr[rows, :], jnp.zeros((1, t), I32), jnp.zeros((1, t), I32))

            def plane_body(plane, state):
                flip, n_gt, ans_u = state
                active = active_scr[rows, :] & (plane_scr[plane - 1, rows, :] ^ flip)
                active_scr[rows, :] = active
                return decide(plane, active, n_gt, ans_u)

            _, n_gt, ans_u = lax.fori_loop(1, 32, plane_body, state)
            return n_gt, ans_u

        nrow_all = plane_scr.shape[1]
        nrow_half = (nrow_all // SUBLANES // 2) * SUBLANES
        if nrow_half:
            n_gt, ans_u = lax.cond((i + 1) * SUBLANES <= nrow_half,
                                   functools.partial(kth_largest, nrow_half),
                                   functools.partial(kth_largest, nrow_all))
        else:
            n_gt, ans_u = kth_largest(nrow_all)
        ans = ans_u ^ INT_MIN
        need = (topk - n_gt).astype(F32)
        run_scr[...] = jnp.zeros(run_scr.shape, F32)
        stri = jnp.where(col < row, 1.0, 0.0).astype(BF16)

    def logits(h, r0):
        return _dot(k_ref[0, h, pl.ds(r0, t), :], qT_ref[0, HEAD_PAD * h:HEAD_PAD * (h + 1), :])

    def selection_pens(r0s):
        pens = []
        if is_dsa:
            for r0 in r0s:
                kc = key_scr[pl.ds(r0, t), :]
                eq = kc == ans
                eqf = jnp.where(eq, 1.0, 0.0)
                run = run_scr[0:1, :]
                rank = _dot(stri, eqf.astype(BF16)) + run
                run_scr[0:1, :] = run + jnp.sum(eqf, axis=0, keepdims=True)
                pens.append(jnp.where(kc > ans, 0.0, jnp.where(eq, jnp.where(rank < need, 0.0, NEG), NEG)))
        return pens

    b_idx = pl.program_id(0)
    bounds = []
    for h in range(nh):
        qh = qT_ref[0, HEAD_PAD * h:HEAD_PAD * (h + 1), :].astype(F32)
        bound = jnp.sqrt(jnp.sum(qh * qh, axis=0, keepdims=True)) * (kb_ref[b_idx * nh + h] * BOUND_SLACK)
        if is_dsa:
            bound = bound + kb_ref[kb_ref.shape[0] - 1]
        bounds.append(bound)
    bound_max = jnp.max(functools.reduce(jnp.maximum, bounds))
    fixed_ref_ok = bound_max <= FIXED_REF_MAX

    def fixed_ref_chunks(js, kinds):
        n = len(js)
        r0s = [pl.multiple_of(j * t, t) for j in js]
        lgs = [[logits(h, r0s[c]) for c in range(n)] for h in range(nh)]
        pens = selection_pens(r0s)
        ps = []
        for h in range(nh):
            row_ps = []
            if is_dsa:
                far_ref = bounds[h] - bt_ref[2, h, 0:1, :]
            for c in range(n):
                x, ref = lgs[h][c], bounds[h]
                if is_dsa and kinds[c] == "far":
                    x, ref = x + pens[c], far_ref
                elif is_dsa:
                    x = x + (pens[c] + bt_ref[0 if kinds[c] == "diag" else 1, h])
                elif kinds[c] == "diag":
                    x = x + causal_pen
                row_ps.append(jnp.exp2(x - ref).astype(BF16))
            ps.append(row_ps[0] if n == 1 else jnp.concatenate(row_ps, axis=0))
        for h in range(nh):
            vs = slice(vr * h, vr * (h + 1))
            vt = [vT_ref[0, js[c], vs, :] for c in range(n)]
            acc_scr[vs, :] += _dot(vt[0] if n == 1 else jnp.concatenate(vt, axis=1), ps[h])

    def super_chunk(js, kinds):
        n = len(js)
        r0s = [pl.multiple_of(j * t, t) for j in js]
        lgs = [[logits(h, r0s[c]) for c in range(n)] for h in range(nh)]
        pens = selection_pens(r0s)
        ps, alphas = [], []
        for h in range(nh):
            xs = []
            for c, j in enumerate(js):
                diag = kinds[c] == "diag"
                lg = lgs[h][c]
                if is_dsa:
                    tile = 0 if diag else jnp.minimum(i - j, 2)
                    lg = lg + (pens[c] + bt_ref[tile, h])
                elif diag:
                    lg = lg + causal_pen
                xs.append(lg)
            m_old = m_scr[h, 0:1, :]
            m_new = m_old
            for x in xs:
                m_new = jnp.maximum(m_new, jnp.max(x, axis=0, keepdims=True))
            alpha = jnp.exp2(m_old - m_new)
            m_scr[h, 0:1, :] = m_new
            ps.append([jnp.exp2((x - m_new).astype(BF16)) for x in xs])
            alphas.append(alpha)
        for h in range(nh):
            vs = slice(vr * h, vr * (h + 1))
            pv = _dot(vT_ref[0, js[0], vs, :], ps[h][0])
            for c in range(1, n):
                pv = pv + _dot(vT_ref[0, js[c], vs, :], ps[h][c])
            acc_scr[vs, :] = alphas[h] * acc_scr[vs, :] + pv

    tail = ["near", "diag"] if is_dsa else ["diag"]
    nfar = jnp.maximum(i + 1 - len(tail), 0)

    def run_chunks(step):
        def group_body(u, carry):
            step([ATT_GROUP * u + c for c in range(ATT_GROUP)], ["far"] * ATT_GROUP)
            return carry

        lax.fori_loop(0, nfar // ATT_GROUP, group_body, 0)
        for rem in range(ATT_GROUP):
            @pl.when(jnp.logical_and(i + 1 >= len(tail), nfar % ATT_GROUP == rem))
            def _():
                first = i + 1 - len(tail) - rem
                step([first + c for c in range(rem + len(tail))], ["far"] * rem + tail)

        if is_dsa:
            @pl.when(i == 0)
            def _():
                step([i], ["diag"])

    pl.when(fixed_ref_ok)(functools.partial(run_chunks, fixed_ref_chunks))
    pl.when(jnp.logical_not(fixed_ref_ok))(functools.partial(run_chunks, super_chunk))

    outs = []
    for h in range(nh):
        outs.append(acc_scr[vr * h:vr * h + dv, :] / acc_scr[vr * h + dv:vr * h + dv + 1, :])
    o_ref[0] = jnp.transpose(jnp.concatenate(outs, axis=0)).astype(o_ref.dtype)


def _attention(qT, k, vT, kbound, dsa=None, blk=ATT_BLK):
    b, nh, l, _ = k.shape
    vr = vT.shape[2] // nh
    dv = vr - ONES_ROWS
    grid = (b, l // blk)
    qspec = lambda r: pl.BlockSpec((1, r, blk), lambda bi, i: (bi, 0, i))
    kspec = pl.BlockSpec((1, nh, l, HEAD_PAD), lambda bi, i: (bi, 0, 0, 0))
    vspec = pl.BlockSpec((1, l // blk, nh * vr, blk), lambda bi, i: (bi, 0, 0, 0))
    in_specs = [pl.BlockSpec(memory_space=pltpu.SMEM), qspec(nh * HEAD_PAD), kspec, vspec]
    args = [kbound, qT, k, vT]
    scratch = [pltpu.VMEM((nh, 8, blk), F32), pltpu.VMEM((nh * vr, blk), F32)]
    topk = 0
    if dsa is not None:
        iqT, wT, ik, bt = dsa
        topk = min(TOPK_MAX, l // 4)
        in_specs += [qspec(IDX_HEADS * IDX_DIM), qspec(IDX_HEADS),
                     pl.BlockSpec((1, l, IDX_DIM), lambda bi, i: (bi, 0, 0)),
                     _const_spec(bt.shape)]
        args += [iqT, wT, ik, bt]
        assert blk == 32 * SUBLANES
        nrow = (l // blk) * SUBLANES
        scratch += [pltpu.VMEM((l, blk), I32), pltpu.VMEM((8, blk), F32),
                    pltpu.VMEM((32, nrow, blk), I32), pltpu.VMEM((nrow, blk), I32)]
    return pl.pallas_call(
        functools.partial(_attn_body, nh, dv, blk, topk, dsa is not None),
        grid=grid, in_specs=in_specs,
        out_specs=pl.BlockSpec((1, blk, nh * dv), lambda bi, i: (bi, i, 0)),
        out_shape=jax.ShapeDtypeStruct((b, l, nh * dv), BF16),
        scratch_shapes=scratch,
        compiler_params=_cparams(("parallel", "arbitrary")),
        name="dsa_attn" if dsa is not None else "mla_attn",
    )(*args)


def _gla_body(tg, q_ref, k_ref, v_ref, la_ref, r_ref, go_ref, bd_ref, o_ref, st_scr, o_scr):
    @pl.when(pl.program_id(1) == 0)
    def _():
        st_scr[...] = jnp.zeros(st_scr.shape, F32)

    cs = B_CHUNK
    la = la_ref[0]
    rl = lax.broadcasted_iota(I32, (tg, LANE), 0) & (cs - 1)
    b = la
    s = 1
    while s < cs:
        b = b + jnp.where(rl >= s, pltpu.roll(b, s, 0), 0.0)
        s *= 2
    q = q_ref[0]
    k = k_ref[0]
    qb = q * jnp.exp(b)
    tri = lax.broadcasted_iota(I32, (cs, cs), 1) <= lax.broadcasted_iota(I32, (cs, cs), 0)
    khead = lax.broadcasted_iota(I32, (cs, LANE), 1) // B_KEY_DIM
    vhead = lax.broadcasted_iota(I32, (cs, GROUP_WIDTH), 1) // B_VAL_DIM
    same_head = (lax.broadcasted_iota(I32, (GROUP_WIDTH, LANE), 0) // B_VAL_DIM
                 == lax.broadcasted_iota(I32, (GROUP_WIDTH, LANE), 1) // B_KEY_DIM)
    for c in range(tg // cs):
        sl = slice(c * cs, (c + 1) * cs)
        bc = b[sl]
        mid = bc[cs // 2:cs // 2 + 1]
        last = bc[cs - 1:cs]
        qe = q[sl] * jnp.exp(bc - mid)
        ke = (k[sl] * jnp.exp(mid - bc)).astype(BF16)
        kd = (k[sl] * jnp.exp(last - bc)).astype(BF16)
        qbc = qb[sl].astype(BF16)
        vc = v_ref[0, sl, :]
        st = st_scr[...]
        qe4 = jnp.concatenate([jnp.where(khead == h, qe, 0.0) for h in range(B_HEADS)], axis=0)
        a_all = _dot_nt(qe4.astype(BF16), ke)
        o = _dot_nt(qbc, st.astype(BF16))
        for h in range(B_HEADS):
            a = jnp.where(tri, a_all[cs * h:cs * (h + 1)], 0.0).astype(BF16)
            o = o + _dot(a, jnp.where(vhead == h, vc, jnp.zeros_like(vc)))
        o_scr[sl, :] = o
        st_scr[...] = st * jnp.exp(last) + jnp.where(same_head, _dot_tn(vc, kd), 0.0)
    o = o_scr[...]
    ms = _group_sum(o * o, bd_ref[...]) * (1.0 / B_VAL_DIM)
    r = r_ref[0]
    o_ref[0] = (o * lax.rsqrt(ms + EPS) * go_ref[...] * (r * jax.nn.sigmoid(r))).astype(o_ref.dtype)


def _gla(bq, bk, bv, bla, br, go, bd, layer, tg=1024):
    b, l, _ = bq.shape
    tg = min(tg, l)
    tok = lambda w: pl.BlockSpec((1, tg, w), lambda bi, i: (bi, i, 0))
    return pl.pallas_call(
        functools.partial(_gla_body, tg),
        grid=(b, l // tg),
        in_specs=[tok(LANE), tok(LANE), tok(GROUP_WIDTH), tok(LANE), tok(GROUP_WIDTH),
                  _const_spec(go.shape, layer), _const_spec(bd.shape)],
        out_specs=tok(GROUP_WIDTH),
        out_shape=jax.ShapeDtypeStruct((b, l, GROUP_WIDTH), BF16),
        scratch_shapes=[pltpu.VMEM((B_HEADS * B_VAL_DIM, B_HEADS * B_KEY_DIM), F32),
                        pltpu.VMEM((tg, GROUP_WIDTH), F32)],
        compiler_params=_cparams(("parallel", "arbitrary")),
        name="gla",
    )(bq, bk, bv, bla, br, go, bd)


CONV_HIST = 32


def _conv_body(tc, h_ref, w_ref, b_ref, g_ref, o_ref, buf):
    @pl.when(pl.program_id(1) == 0)
    def _():
        buf[0:CONV_HIST, :] = jnp.zeros((CONV_HIST, C_CHANNELS), F32)

    @pl.when(pl.program_id(1) > 0)
    def _():
        buf[0:CONV_HIST, :] = buf[tc:tc + CONV_HIST, :]

    buf[CONV_HIST:CONV_HIST + tc, :] = h_ref[0]
    acc = jnp.zeros((tc, C_CHANNELS), F32) + b_ref[...]
    base = CONV_HIST - (C_KERNEL - 1)
    hb = buf[...]
    rows = tc + CONV_HIST
    for r in range(SUBLANES):
        shifted = hb if r == 0 else pltpu.roll(hb, rows - r, 0)
        for j in range(C_KERNEL):
            if (base + j) % SUBLANES == r:
                a0 = base + j - r
                acc = acc + shifted[a0:a0 + tc, :] * w_ref[j:j + 1, :]
    ms = jnp.mean(acc * acc, axis=-1, keepdims=True)
    y = acc * lax.rsqrt(ms + EPS) * g_ref[...]
    o_ref[0] = (y * jax.nn.sigmoid(y)).astype(o_ref.dtype)


def _conv(ch, w, bias, g, layer, tc=512):
    b, l, c = ch.shape
    tc = min(tc, l)
    tok = pl.BlockSpec((1, tc, c), lambda bi, i: (bi, i, 0))
    return pl.pallas_call(
        functools.partial(_conv_body, tc),
        grid=(b, l // tc),
        in_specs=[tok] + [_const_spec(a.shape, layer) for a in (w, bias, g)],
        out_specs=tok,
        out_shape=jax.ShapeDtypeStruct((b, l, c), BF16),
        scratch_shapes=[pltpu.VMEM((tc + CONV_HIST, c), F32)],
        compiler_params=_cparams(("parallel", "arbitrary")),
        name="conv",
    )(ch, w, bias, g)


def _t5_bucket(dist):
    max_exact = REL_BUCKETS // 2
    d = jnp.maximum(dist, 0)
    df = jnp.maximum(d, 1).astype(F32)
    large = max_exact + (jnp.log(df / max_exact) / math.log(REL_MAX_DIST / max_exact)
                         * (REL_BUCKETS - max_exact)).astype(I32)
    large = jnp.minimum(large, REL_BUCKETS - 1)
    return jnp.where(d < max_exact, d, large)


def _pad_cols(w, width):
    return jnp.pad(w, ((0, 0), (0, width - w.shape[1])))


def _lane_rep(v):
    return jnp.broadcast_to(v[:, None], (v.shape[0], LANE))


def _pad_heads_rows(w, heads, dim):
    w = w.reshape(heads, dim, w.shape[1])
    return jnp.pad(w, ((0, 0), (0, HEAD_PAD - dim), (0, 0))).reshape(heads * HEAD_PAD, -1)


def _split_w_in(w_in):
    widths = (256, 256, 256, 256, 32, 8, 128, 128, 256, 16, 256, 512, 256, 128, 32)
    offs = np.cumsum((0,) + widths)
    return [w_in[:, offs[n]:offs[n + 1]] for n in range(len(widths))]


def _row(v):
    return v[None, :].astype(F32)


def _one_layer_params(w):
    (aq, ak, av, iq, ik, iw, bq, bk, bv, bg, br, cu, dcq, dckv, dkpe) = _split_w_in(w["w_in"])
    wn = jnp.concatenate([ak, _pad_cols(ik, LANE), bq, bk, bv, _pad_cols(bg, LANE), br, cu, dcq, dckv,
                          _pad_cols(dkpe, LANE)], axis=1).astype(BF16)
    wt = jnp.concatenate([_pad_heads_rows(aq.T, A_HEADS, A_HEAD_DIM), av.T, iq.T,
                          jnp.pad(iw.T, ((0, T_TOT - T_IW - IDX_HEADS), (0, 0)))], axis=0).astype(BF16)
    pad_to = lambda v, n: jnp.pad(v, (0, n - v.shape[0]))
    ukv = w["d_ukv"].reshape(D_KV_RANK, D_HEADS, D_NOPE + D_V)
    wuk = ukv[:, :, :D_NOPE].reshape(D_KV_RANK, D_HEADS * D_NOPE)
    wuv = ukv[:, :, D_NOPE:].reshape(D_KV_RANK, D_HEADS * D_V)
    gdk = w["d_k_norm"]
    return dict(
        gmix=_row(w["mix_norm"]), wn=wn, wt=wt,
        gaq=_lane_rep(pad_to(w["a_q_norm"], HEAD_PAD)),
        gak=_row(jnp.tile(w["a_k_norm"], A_HEADS)),
        wgu=jnp.pad(w["b_gate_up"], ((0, LANE - B_GATE_RANK), (0, 0))).astype(BF16),
        bgb=_row(w["b_gate_bias"]),
        gqa=_row(w["d_qa_norm"]),
        wuq=_pad_heads_rows(w["d_uq"].T, D_HEADS, D_QK).astype(BF16),
        gdq=_lane_rep(pad_to(w["d_q_norm"], HEAD_PAD)),
        gkva=_row(w["d_kva_norm"]), wuk=wuk.astype(BF16), wuvT=wuv.T.astype(BF16),
        gdk=_row(jnp.tile(gdk[:D_NOPE], D_HEADS)), gdkpe=_row(pad_to(gdk[D_NOPE:], LANE)),
        gbo=_row(jnp.tile(w["b_out_norm"], B_HEADS)),
        cw=jnp.pad(w["c_dw_w"][:, 0, :], ((0, CONV_HIST - C_KERNEL), (0, 0))).astype(F32),
        cb=_row(w["c_dw_b"]), cg=_row(w["c_norm"]),
    )


def _shared_tables(seq):
    hid = np.arange(GROUP_WIDTH) // 64
    bd = jnp.asarray(hid[:, None] == hid[None, :], dtype=BF16)
    half = D_ROPE // 2
    freqs = ROPE_THETA ** (-jnp.arange(half, dtype=F32) / half)
    ang = jnp.arange(seq).astype(F32)[:, None] * freqs[None, :]
    cos, sin = jnp.cos(ang), jnp.sin(ang)
    zeros = jnp.zeros((seq, LANE - D_ROPE), F32)
    cpe = jnp.concatenate([cos, cos, zeros], axis=1)
    spe = jnp.concatenate([-sin, sin, zeros], axis=1)
    return dict(bd=bd, cosT=cos.T, sinT=sin.T, cpe=cpe, spe=spe)


def _bias_tiles(rel_bias, blk):
    assert REL_MAX_DIST <= blk + 1
    kk = jnp.arange(blk)[:, None]
    qq = jnp.arange(blk)[None, :]
    rb = rel_bias.astype(F32).T

    def lookup(bucket):
        onehot = bucket[None, :, :, None] == jnp.arange(REL_BUCKETS)
        return jnp.sum(jnp.where(onehot, rb[:, None, None, :], 0.0), axis=-1) * LOG2E

    d0 = jnp.where(kk <= qq, lookup(_t5_bucket(qq - kk)), NEG)
    d1 = lookup(_t5_bucket(blk + qq - kk))
    far = jnp.broadcast_to(lookup(_t5_bucket(jnp.full((1, 1), 2 * blk, I32))), d1.shape)
    return jnp.stack([d0, d1, far])


def _key_bounds(knorm, bias_max):
    km = jnp.sqrt(jnp.max(knorm[..., 0], axis=1))
    kb_a = jnp.concatenate([km[:, :A_HEADS].reshape(-1), bias_max.reshape(1)])
    return kb_a, km[:, A_HEADS:].reshape(-1)


def kernel(x, ffn1_norm, ffn1_gate, ffn1_up, ffn1_down, mix_norm, w_in, a_q_norm, a_k_norm, rel_bias,
           b_gate_up, b_gate_bias, b_out_norm, c_dw_w, c_dw_b, c_norm, d_qa_norm, d_uq, d_kva_norm,
           d_ukv, d_q_norm, d_k_norm, w_out, ffn2_norm, ffn2_gate, ffn2_up, ffn2_down):
    w = dict(mix_norm=mix_norm, w_in=w_in, a_q_norm=a_q_norm, a_k_norm=a_k_norm, b_gate_up=b_gate_up,
             b_gate_bias=b_gate_bias, b_out_norm=b_out_norm, c_dw_w=c_dw_w, c_dw_b=c_dw_b, c_norm=c_norm,
             d_qa_norm=d_qa_norm, d_uq=d_uq, d_kva_norm=d_kva_norm, d_ukv=d_ukv,
             d_q_norm=d_q_norm, d_k_norm=d_k_norm)
    bsz, seq, dm = x.shape
    depth = w_in.shape[0]
    blk = min(ATT_BLK, seq)
    bt = _bias_tiles(rel_bias, blk)
    bias_max = jnp.max(bt)
    p = {**jax.vmap(_one_layer_params)(w), **_shared_tables(seq)}
    stacked_row = lambda v: v[:, None, :].astype(F32)
    ffn1 = (stacked_row(ffn1_norm), ffn1_gate, ffn1_up, ffn1_down)
    ffn2 = (stacked_row(ffn2_norm), ffn2_gate, ffn2_up, ffn2_down)
    wo = w_out.astype(BF16)
    x2 = x.reshape(bsz * seq, dm)
    for l in range(depth):
        x2 = _ffn(x2, *ffn1, l)
        (aqT, akh, avT, iqT, aik, iwT, bq, bk, bv, bla, br, ch, dqT, dkh, dvT, knorm) = _mix_in(
            x2.reshape(bsz, seq, dm), p, l, blk=blk)
        kb_a, kb_d = _key_bounds(knorm, bias_max)
        y_a = _attention(aqT, akh, avT, kb_a, dsa=(iqT, iwT, aik, bt), blk=blk)
        y_b = _gla(bq, bk, bv, bla, br, p["gbo"], p["bd"], l)
        y_c = _conv(ch, p["cw"], p["cb"], p["cg"], l)
        y_d = _attention(dqT, dkh, dvT, kb_d, blk=blk)
        ys = [y.reshape(bsz * seq, GROUP_WIDTH) for y in (y_a, y_b, y_c, y_d)]
        x2 = _ffn(x2, *ffn2, l, mix=(ys, wo))
    return x2.reshape(bsz, seq, dm)
```

```python
import functools
import math

import jax
import jax.numpy as jnp
import numpy as np
from jax import lax
from jax.experimental import pallas as pl
from jax.experimental.pallas import tpu as pltpu

F32 = jnp.float32
BF16 = jnp.bfloat16
I32 = jnp.int32

EPS = 1e-6
GROUP_WIDTH = 256
A_HEADS, A_HEAD_DIM = 4, 64
IDX_HEADS, IDX_DIM = 8, 32
TOPK_MAX = 256
REL_BUCKETS, REL_MAX_DIST = 32, 128
B_HEADS, B_KEY_DIM, B_VAL_DIM, B_GATE_RANK = 4, 32, 64, 16
B_GATE_TAU = 16.0
B_CHUNK = 64
C_CHANNELS, C_KERNEL = 256, 31
D_HEADS, D_Q_RANK, D_KV_RANK, D_NOPE, D_ROPE, D_V = 4, 256, 128, 64, 32, 64
D_QK = D_NOPE + D_ROPE
ROPE_THETA = 10000.0

LANE = 128
SUBLANES = 8
HEAD_PAD = 128
ONES_ROWS = 16
ATT_BLK = 256
ATT_GROUP = 4
SCORE_GROUP = 4
BOUND_SLACK = 1.01
FIXED_REF_MAX = 40.0
INT_MIN = -2 ** 31
NEG = -1e30
LOG2E = math.log2(math.e)
VMEM_LIMIT = 56 * 1024 * 1024

N_AK, N_IK, N_BQ, N_BK, N_BV, N_BG, N_BR, N_CU, N_DCQ, N_DCKV, N_DKPE, N_TOT = (
    0, 256, 384, 512, 640, 896, 1024, 1280, 1792, 2048, 2176, 2304)
T_AQ, T_AV, T_IQ, T_IW, T_TOT = 0, 512, 768, 1024, 1040


def _dot(a, b):
    return jnp.dot(a, b, preferred_element_type=F32)


def _dot_nt(a, b):
    return lax.dot_general(a, b, (((1,), (1,)), ((), ())), preferred_element_type=F32)


def _dot_tn(a, b):
    return lax.dot_general(a, b, (((0,), (0,)), ((), ())), preferred_element_type=F32)


def _group_sum(x2, bd):
    hi = x2.astype(BF16)
    lo = (x2 - hi.astype(F32)).astype(BF16)
    return _dot(hi, bd) + _dot(lo, bd)


def _const_spec(shape, layer=None):
    nd = len(shape)
    if layer is None:
        return pl.BlockSpec(shape, lambda *_: (0,) * nd, pipeline_mode=pl.Buffered(1))
    return pl.BlockSpec((None,) + tuple(shape[1:]), lambda *_: (layer,) + (0,) * (nd - 1),
                        pipeline_mode=pl.Buffered(1))


def _cparams(sem):
    return pltpu.CompilerParams(dimension_semantics=sem, vmem_limit_bytes=VMEM_LIMIT)


def _ffn_body(has_mix, fc, *refs):
    if has_mix:
        x_ref, ya, yb, yc, yd, wo_ref, g_ref, wg_ref, wu_ref, wd_ref, o_ref, h_scr = refs
    else:
        x_ref, g_ref, wg_ref, wu_ref, wd_ref, o_ref, h_scr = refs
    x = x_ref[...]
    if has_mix:
        y = jnp.concatenate([ya[...], yb[...], yc[...], yd[...]], axis=-1)
        x = x + _dot(y, wo_ref[...])
    ms = jnp.mean(x * x, axis=-1, keepdims=True)
    xn = (x * lax.rsqrt(ms + EPS) * g_ref[...]).astype(BF16)
    d_ff = wg_ref.shape[1]
    for c in range(d_ff // fc):
        sl = slice(c * fc, (c + 1) * fc)
        gate = _dot(xn, wg_ref[:, sl].astype(BF16))
        up = _dot(xn, wu_ref[:, sl].astype(BF16))
        h_scr[:, sl] = (gate * jax.nn.sigmoid(gate) * up).astype(BF16)
    o_ref[...] = x + 0.5 * _dot(h_scr[...], wd_ref[...].astype(BF16))


def _ffn(x2, g, wg, wu, wd, layer, mix=None, tm=512, fc=256):
    m, d = x2.shape
    d_ff = wg.shape[2]
    tm = min(tm, m)
    row = lambda w: pl.BlockSpec((tm, w), lambda i: (i, 0))
    in_specs = [row(d)]
    args = [x2]
    if mix is not None:
        ys, wo = mix
        in_specs += [row(GROUP_WIDTH)] * 4 + [_const_spec(wo.shape, layer)]
        args += list(ys) + [wo]
    in_specs += [_const_spec(a.shape, layer) for a in (g, wg, wu, wd)]
    args += [g, wg, wu, wd]
    return pl.pallas_call(
        functools.partial(_ffn_body, mix is not None, fc),
        grid=(m // tm,),
        in_specs=in_specs,
        out_specs=row(d),
        out_shape=jax.ShapeDtypeStruct((m, d), F32),
        scratch_shapes=[pltpu.VMEM((tm, d_ff), BF16)],
        compiler_params=_cparams(("parallel",)),
        name="ffn_mix" if mix is not None else "ffn",
    )(*args)


def _mix_in_body(tm, blk,
                 x_ref, gmix_ref, wn_ref, wt_ref, bd_ref,
                 gaq_ref, gak_ref,
                 wgu_ref, bgb_ref,
                 gqa_ref, wuq_ref, gdq_ref, cosT_ref, sinT_ref,
                 gkva_ref, wuk_ref, wuvT_ref, gdk_ref, gdkpe_ref, cpe_ref, spe_ref,
                 aqT_ref, ak_ref, avT_ref, iqT_ref, ik_ref, iwT_ref,
                 bq_ref, bk_ref, bv_ref, bla_ref, br_ref,
                 ch_ref,
                 dqT_ref, dk_ref, dvT_ref, knorm_ref):
    nlt = tm // LANE
    x = x_ref[0]
    ms = jnp.mean(x * x, axis=-1, keepdims=True)
    xn = (x * lax.rsqrt(ms + EPS) * gmix_ref[...]).astype(BF16)
    bd = bd_ref[...]
    lane = lax.broadcasted_iota(I32, (tm, LANE), 1)

    def lanes(g):
        return jnp.tile(g, (1, nlt))

    z = _dot(xn, wn_ref[...])
    zt = _dot_nt(wt_ref[...], xn)

    def zs(off, width):
        return z[:, off:off + width]

    cq = zs(N_DCQ, D_Q_RANK)
    cq_ms = jnp.mean(cq * cq, axis=-1, keepdims=True)
    cqn = (cq * lax.rsqrt(cq_ms + EPS) * gqa_ref[...]).astype(BF16)
    ckv = zs(N_DCKV, D_KV_RANK)
    ckv_ms = jnp.mean(ckv * ckv, axis=-1, keepdims=True)
    ckvn = (ckv * lax.rsqrt(ckv_ms + EPS) * gkva_ref[...]).astype(BF16)
    dq = _dot_nt(wuq_ref[...], cqn).reshape(D_HEADS, HEAD_PAD, tm)
    kn = _dot(ckvn, wuk_ref[...])
    dv = _dot_nt(wuvT_ref[...], ckvn).astype(BF16)
    gate = _dot(zs(N_BG, LANE).astype(BF16), wgu_ref[...]) + bgb_ref[...]
    ak = zs(N_AK, GROUP_WIDTH)
    ak_ms = _group_sum(ak * ak, bd) * (1.0 / A_HEAD_DIM)
    kn_ss = _group_sum(kn * kn, bd)

    aq = zt[T_AQ:T_AQ + A_HEADS * HEAD_PAD].reshape(A_HEADS, HEAD_PAD, tm)
    aq_ms = jnp.sum(aq * aq, axis=1, keepdims=True) * (1.0 / A_HEAD_DIM)
    aq = aq * lax.rsqrt(aq_ms + EPS) * lanes(gaq_ref[...])[None] * (A_HEAD_DIM ** -0.5 * LOG2E)
    aqT_ref[0] = aq.reshape(A_HEADS * HEAD_PAD, tm).astype(BF16)
    def emit_key_norms(first_row, sq_norms):
        top = jnp.max(sq_norms, axis=0, keepdims=True)
        for h in range(4):
            knorm_ref[0, 0, first_row + h:first_row + h + 1, :] = jnp.broadcast_to(
                top[:, 64 * h:64 * h + 1], (1, LANE))

    def with_ones_rows(vt, heads, dim):
        ones = jnp.ones((ONES_ROWS, tm), BF16)
        return jnp.concatenate([r for h in range(heads) for r in (vt[dim * h:dim * (h + 1)], ones)], axis=0)

    av = with_ones_rows(zt[T_AV:T_AV + GROUP_WIDTH].astype(BF16), A_HEADS, A_HEAD_DIM)
    for c in range(tm // blk):
        avT_ref[0, c] = av[:, c * blk:(c + 1) * blk]
    iqT_ref[0] = zt[T_IQ:T_IQ + IDX_HEADS * IDX_DIM].astype(BF16)
    iwT_ref[0] = zt[T_IW:T_IW + IDX_HEADS] * ((IDX_HEADS ** -0.5) * (IDX_DIM ** -0.5))

    ak = ak * lax.rsqrt(ak_ms + EPS) * gak_ref[...]
    for h in range(A_HEADS):
        pair = ak[:, LANE * (h // 2):LANE * (h // 2) + LANE]
        if h % 2 == 1:
            pair = pltpu.roll(pair, 64, 1)
        kh = jnp.where(lane < A_HEAD_DIM, pair, 0.0).astype(BF16)
        ak_ref[0, h] = kh
    emit_key_norms(0, _dot((ak * ak).astype(BF16), bd))
    ik_ref[0] = zs(N_IK, LANE)[:, :IDX_DIM].astype(BF16)

    bq_ref[0] = zs(N_BQ, LANE) * (B_KEY_DIM ** -0.5)
    bk_ref[0] = zs(N_BK, LANE)
    bv_ref[0] = zs(N_BV, GROUP_WIDTH).astype(BF16)
    bla_ref[0] = (jnp.minimum(gate, 0.0) - jnp.log(1.0 + jnp.exp(-jnp.abs(gate)))) * (1.0 / B_GATE_TAU)
    br_ref[0] = zs(N_BR, GROUP_WIDTH)

    ca = zs(N_CU, C_CHANNELS)
    cg = zs(N_CU + C_CHANNELS, C_CHANNELS)
    ch_ref[0] = ca * jax.nn.sigmoid(cg)

    dq_ms = jnp.sum(dq * dq, axis=1, keepdims=True) * (1.0 / D_QK)
    dq = dq * lax.rsqrt(dq_ms + EPS) * lanes(gdq_ref[...])[None] * (D_QK ** -0.5 * LOG2E)
    half = D_ROPE // 2
    x1 = dq[:, D_NOPE:D_NOPE + half]
    x2 = dq[:, D_NOPE + half:D_QK]
    cs = cosT_ref[...][None]
    sn = sinT_ref[...][None]
    dq = jnp.concatenate([dq[:, :D_NOPE], x1 * cs - x2 * sn, x2 * cs + x1 * sn, dq[:, D_QK:]], axis=1)
    dqT_ref[0] = dq.reshape(D_HEADS * HEAD_PAD, tm).astype(BF16)

    dv = with_ones_rows(dv, D_HEADS, D_V)
    for c in range(tm // blk):
        dvT_ref[0, c] = dv[:, c * blk:(c + 1) * blk]
    kpe = zs(N_DKPE, LANE)
    ss = kn_ss + jnp.sum(kpe * kpe, axis=-1, keepdims=True)
    rinv = lax.rsqrt(ss * (1.0 / D_QK) + EPS)
    kn = kn * rinv * gdk_ref[...]
    pe = kpe * gdkpe_ref[...]
    partner = jnp.where(lane < half, pltpu.roll(pe, LANE - half, 1), pltpu.roll(pe, half, 1))
    pe = pe * cpe_ref[...] + partner * spe_ref[...]
    pe = pltpu.roll(pe, D_NOPE, 1)
    for h in range(D_HEADS):
        pair = kn[:, LANE * (h // 2):LANE * (h // 2) + LANE]
        rpair = rinv[:, LANE * (h // 2):LANE * (h // 2) + LANE]
        if h % 2 == 1:
            pair = pltpu.roll(pair, 64, 1)
        else:
            rpair = pltpu.roll(rpair, 64, 1)
        kh = jnp.where(lane < D_NOPE, pair, pe * rpair).astype(BF16)
        dk_ref[0, h] = kh
    pe_sq = _dot((pe * pe).astype(BF16), jnp.ones((LANE, LANE), BF16))
    emit_key_norms(A_HEADS, _dot((kn * kn).astype(BF16), bd) + jnp.tile(pe_sq, (1, 2)) * (rinv * rinv))


def _mix_in(x3, p, layer, tm=512, blk=ATT_BLK):
    b, l, d = x3.shape
    tm = min(tm, l)
    grid = (b, l // tm)
    nck = l // blk
    tok = lambda w: pl.BlockSpec((1, tm, w), lambda bi, i: (bi, i, 0))
    tokT = lambda r: pl.BlockSpec((1, r, tm), lambda bi, i: (bi, 0, i))
    headk = pl.BlockSpec((1, 4, tm, HEAD_PAD), lambda bi, i: (bi, 0, i, 0))
    vrows = GROUP_WIDTH + 4 * ONES_ROWS
    chunkT = pl.BlockSpec((1, tm // blk, vrows, blk), lambda bi, i: (bi, i, 0, 0))
    postab = lambda r: pl.BlockSpec((r, tm), lambda bi, i: (0, i))
    posrow = pl.BlockSpec((tm, LANE), lambda bi, i: (i, 0))
    consts = [p["gmix"], p["wn"], p["wt"], p["bd"], p["gaq"], p["gak"], p["wgu"], p["bgb"],
              p["gqa"], p["wuq"], p["gdq"]]
    consts2 = [p["gkva"], p["wuk"], p["wuvT"], p["gdk"], p["gdkpe"]]
    lspec = lambda a: _const_spec(a.shape) if a is p["bd"] else _const_spec(a.shape, layer)
    in_specs = ([tok(d)] + [lspec(a) for a in consts]
                + [postab(D_ROPE // 2), postab(D_ROPE // 2)]
                + [lspec(a) for a in consts2] + [posrow, posrow])
    args = [x3] + consts + [p["cosT"], p["sinT"]] + consts2 + [p["cpe"], p["spe"]]
    sd = jax.ShapeDtypeStruct
    out_shape = [
        sd((b, A_HEADS * HEAD_PAD, l), BF16), sd((b, A_HEADS, l, HEAD_PAD), BF16),
        sd((b, nck, vrows, blk), BF16), sd((b, IDX_HEADS * IDX_DIM, l), BF16),
        sd((b, l, IDX_DIM), BF16), sd((b, IDX_HEADS, l), F32),
        sd((b, l, LANE), F32), sd((b, l, LANE), F32), sd((b, l, GROUP_WIDTH), BF16),
        sd((b, l, LANE), F32), sd((b, l, GROUP_WIDTH), F32),
        sd((b, l, C_CHANNELS), F32),
        sd((b, D_HEADS * HEAD_PAD, l), BF16), sd((b, D_HEADS, l, HEAD_PAD), BF16),
        sd((b, nck, vrows, blk), BF16),
        sd((b, l // tm, A_HEADS + D_HEADS, LANE), F32),
    ]
    out_specs = [
        tokT(A_HEADS * HEAD_PAD), headk, chunkT, tokT(IDX_HEADS * IDX_DIM),
        tok(IDX_DIM), tokT(IDX_HEADS),
        tok(LANE), tok(LANE), tok(GROUP_WIDTH), tok(LANE), tok(GROUP_WIDTH),
        tok(C_CHANNELS),
        tokT(D_HEADS * HEAD_PAD), headk, chunkT,
        pl.BlockSpec((1, 1, A_HEADS + D_HEADS, LANE), lambda bi, i: (bi, i, 0, 0)),
    ]
    return pl.pallas_call(
        functools.partial(_mix_in_body, tm, blk),
        grid=grid, in_specs=in_specs, out_specs=out_specs, out_shape=out_shape,
        compiler_params=_cparams(("parallel", "parallel")),
        name="mix_in",
    )(*args)


def _attn_body(nh, dv, blk, topk, is_dsa, *refs):
    if is_dsa:
        (kb_ref, qT_ref, k_ref, vT_ref, iqT_ref, wT_ref, ik_ref, bt_ref,
         o_ref, m_scr, acc_scr, key_scr, run_scr, plane_scr, active_scr) = refs
    else:
        kb_ref, qT_ref, k_ref, vT_ref, o_ref, m_scr, acc_scr = refs
    vr = dv + ONES_ROWS
    i = pl.program_id(1)
    t = blk
    row = lax.broadcasted_iota(I32, (t, t), 0)
    col = lax.broadcasted_iota(I32, (t, t), 1)
    causal_pen = jnp.where(row <= col, 0.0, NEG)

    m_scr[...] = jnp.full(m_scr.shape, NEG, F32)
    acc_scr[...] = jnp.zeros(acc_scr.shape, F32)

    if is_dsa:
        @pl.when(i == 0)
        def _():
            plane_scr[...] = jnp.zeros(plane_scr.shape, I32)

        def score_chunk(j, diag):
            r0 = pl.multiple_of(j * t, t)
            ikc = ik_ref[0, pl.ds(r0, t), :]
            s = jnp.zeros((t, t), F32)
            for h in range(IDX_HEADS):
                d = _dot(ikc, iqT_ref[0, IDX_DIM * h:IDX_DIM * (h + 1), :])
                s = s + jnp.maximum(d, 0.0) * wT_ref[0, h:h + 1, :]
            bits = lax.bitcast_convert_type(s, I32)
            key = jnp.where(bits < 0, bits ^ 0x7FFFFFFF, bits)
            if diag:
                key = jnp.where(row <= col, key, INT_MIN)
            key_scr[pl.ds(r0, t), :] = key
            w = [key[SUBLANES * r:SUBLANES * (r + 1), :] for r in range(32)]
            step, mask = 16, 0x0000FFFF
            while step:
                for lo in range(32):
                    if lo & step == 0:
                        hi = lo + step
                        swap = (w[lo] ^ jnp.right_shift(w[hi], step)) & mask
                        w[lo] = w[lo] ^ swap
                        w[hi] = w[hi] ^ jnp.left_shift(swap, step)
                step //= 2
                mask ^= (mask << step) & 0xFFFFFFFF
            c0 = pl.multiple_of(j * SUBLANES, SUBLANES)
            w[0] = ~w[0]
            for p in range(32):
                plane_scr[p, pl.ds(c0, SUBLANES), :] = w[p]

        def score_group(u, carry):
            for c in range(SCORE_GROUP):
                score_chunk(SCORE_GROUP * u + c, False)
            return carry

        lax.fori_loop(0, i // SCORE_GROUP, score_group, 0)
        for rem in range(SCORE_GROUP):
            @pl.when(i % SCORE_GROUP == rem)
            def _():
                for c in range(rem):
                    score_chunk(i - rem + c, False)
                score_chunk(i, True)

        def kth_largest(nrow):
            rows = pl.ds(0, nrow)
            in_range = lax.broadcasted_iota(I32, (nrow, t), 0) < (i + 1) * SUBLANES

            def col_count(words):
                pc = lax.population_count(words).reshape(nrow // SUBLANES, SUBLANES, t)
                return jnp.sum(jnp.sum(pc, axis=0), axis=0, keepdims=True)

            def decide(plane, active, n_gt, ans_u):
                ones = col_count(active & plane_scr[plane, rows, :])
                take = n_gt + ones >= topk
                bit = lax.shift_right_logical(jnp.int32(INT_MIN), jnp.int32(plane))
                return (jnp.where(take, 0, -1), n_gt + jnp.where(take, 0, ones),
                        ans_u | jnp.where(take, bit, 0))

            active_scr[rows, :] = jnp.where(in_range, -1, 0)
            state = decide(0, active_scr[rows, :], jnp.zeros((1, t), I32), jnp.zeros((1, t), I32))

            def plane_body(plane, state):
                flip, n_gt, ans_u = state
                active = active_scr[rows, :] & (plane_scr[plane - 1, rows, :] ^ flip)
                active_scr[rows, :] = active
                return decide(plane, active, n_gt, ans_u)

            _, n_gt, ans_u = lax.fori_loop(1, 32, plane_body, state)
            return n_gt, ans_u

        nrow_all = plane_scr.shape[1]
        nrow_half = (nrow_all // SUBLANES // 2) * SUBLANES
        if nrow_half:
            n_gt, ans_u = lax.cond((i + 1) * SUBLANES <= nrow_half,
                                   functools.partial(kth_largest, nrow_half),
                                   functools.partial(kth_largest, nrow_all))
        else:
            n_gt, ans_u = kth_largest(nrow_all)
        ans = ans_u ^ INT_MIN
        need = (topk - n_gt).astype(F32)
        run_scr[...] = jnp.zeros(run_scr.shape, F32)
        stri = jnp.where(col < row, 1.0, 0.0).astype(BF16)

    def logits(h, r0):
        return _dot(k_ref[0, h, pl.ds(r0, t), :], qT_ref[0, HEAD_PAD * h:HEAD_PAD * (h + 1), :])

    def selection_pens(r0s):
        pens = []
        if is_dsa:
            for r0 in r0s:
                kc = key_scr[pl.ds(r0, t), :]
                eq = kc == ans
                eqf = jnp.where(eq, 1.0, 0.0)
                run = run_scr[0:1, :]
                rank = _dot(stri, eqf.astype(BF16)) + run
                run_scr[0:1, :] = run + jnp.sum(eqf, axis=0, keepdims=True)
                pens.append(jnp.where(kc > ans, 0.0, jnp.where(eq, jnp.where(rank < need, 0.0, NEG), NEG)))
        return pens

    b_idx = pl.program_id(0)
    bounds = []
    for h in range(nh):
        qh = qT_ref[0, HEAD_PAD * h:HEAD_PAD * (h + 1), :].astype(F32)
        bound = jnp.sqrt(jnp.sum(qh * qh, axis=0, keepdims=True)) * (kb_ref[b_idx * nh + h] * BOUND_SLACK)
        if is_dsa:
            bound = bound + kb_ref[kb_ref.shape[0] - 1]
        bounds.append(bound)
    bound_max = jnp.max(functools.reduce(jnp.maximum, bounds))
    fixed_ref_ok = bound_max <= FIXED_REF_MAX

    def fixed_ref_chunks(js, kinds):
        n = len(js)
        r0s = [pl.multiple_of(j * t, t) for j in js]
        lgs = [[logits(h, r0s[c]) for c in range(n)] for h in range(nh)]
        pens = selection_pens(r0s)
        ps = []
        for h in range(nh):
            row_ps = []
            if is_dsa:
                far_ref = bounds[h] - bt_ref[2, h, 0:1, :]
            for c in range(n):
                x, ref = lgs[h][c], bounds[h]
                if is_dsa and kinds[c] == "far":
                    x, ref = x + pens[c], far_ref
                elif is_dsa:
                    x = x + (pens[c] + bt_ref[0 if kinds[c] == "diag" else 1, h])
                elif kinds[c] == "diag":
                    x = x + causal_pen
                row_ps.append(jnp.exp2(x - ref).astype(BF16))
            ps.append(row_ps[0] if n == 1 else jnp.concatenate(row_ps, axis=0))
        for h in range(nh):
            vs = slice(vr * h, vr * (h + 1))
            vt = [vT_ref[0, js[c], vs, :] for c in range(n)]
            acc_scr[vs, :] += _dot(vt[0] if n == 1 else jnp.concatenate(vt, axis=1), ps[h])

    def super_chunk(js, kinds):
        n = len(js)
        r0s = [pl.multiple_of(j * t, t) for j in js]
        lgs = [[logits(h, r0s[c]) for c in range(n)] for h in range(nh)]
        pens = selection_pens(r0s)
        ps, alphas = [], []
        for h in range(nh):
            xs = []
            for c, j in enumerate(js):
                diag = kinds[c] == "diag"
                lg = lgs[h][c]
                if is_dsa:
                    tile = 0 if diag else jnp.minimum(i - j, 2)
                    lg = lg + (pens[c] + bt_ref[tile, h])
                elif diag:
                    lg = lg + causal_pen
                xs.append(lg)
            m_old = m_scr[h, 0:1, :]
            m_new = m_old
            for x in xs:
                m_new = jnp.maximum(m_new, jnp.max(x, axis=0, keepdims=True))
            alpha = jnp.exp2(m_old - m_new)
            m_scr[h, 0:1, :] = m_new
            ps.append([jnp.exp2((x - m_new).astype(BF16)) for x in xs])
            alphas.append(alpha)
        for h in range(nh):
            vs = slice(vr * h, vr * (h + 1))
            pv = _dot(vT_ref[0, js[0], vs, :], ps[h][0])
            for c in range(1, n):
                pv = pv + _dot(vT_ref[0, js[c], vs, :], ps[h][c])
            acc_scr[vs, :] = alphas[h] * acc_scr[vs, :] + pv

    tail = ["near", "diag"] if is_dsa else ["diag"]
    nfar = jnp.maximum(i + 1 - len(tail), 0)

    def run_chunks(step):
        def group_body(u, carry):
            step([ATT_GROUP * u + c for c in range(ATT_GROUP)], ["far"] * ATT_GROUP)
            return carry

        lax.fori_loop(0, nfar // ATT_GROUP, group_body, 0)
        for rem in range(ATT_GROUP):
            @pl.when(jnp.logical_and(i + 1 >= len(tail), nfar % ATT_GROUP == rem))
            def _():
                first = i + 1 - len(tail) - rem
                step([first + c for c in range(rem + len(tail))], ["far"] * rem + tail)

        if is_dsa:
            @pl.when(i == 0)
            def _():
                step([i], ["diag"])

    pl.when(fixed_ref_ok)(functools.partial(run_chunks, fixed_ref_chunks))
    pl.when(jnp.logical_not(fixed_ref_ok))(functools.partial(run_chunks, super_chunk))

    outs = []
    for h in range(nh):
        outs.append(acc_scr[vr * h:vr * h + dv, :] / acc_scr[vr * h + dv:vr * h + dv + 1, :])
    o_ref[0] = jnp.transpose(jnp.concatenate(outs, axis=0)).astype(o_ref.dtype)


def _attention(qT, k, vT, kbound, dsa=None, blk=ATT_BLK):
    b, nh, l, _ = k.shape
    vr = vT.shape[2] // nh
    dv = vr - ONES_ROWS
    grid = (b, l // blk)
    qspec = lambda r: pl.BlockSpec((1, r, blk), lambda bi, i: (bi, 0, i))
    kspec = pl.BlockSpec((1, nh, l, HEAD_PAD), lambda bi, i: (bi, 0, 0, 0))
    vspec = pl.BlockSpec((1, l // blk, nh * vr, blk), lambda bi, i: (bi, 0, 0, 0))
    in_specs = [pl.BlockSpec(memory_space=pltpu.SMEM), qspec(nh * HEAD_PAD), kspec, vspec]
    args = [kbound, qT, k, vT]
    scratch = [pltpu.VMEM((nh, 8, blk), F32), pltpu.VMEM((nh * vr, blk), F32)]
    topk = 0
    if dsa is not None:
        iqT, wT, ik, bt = dsa
        topk = min(TOPK_MAX, l // 4)
        in_specs += [qspec(IDX_HEADS * IDX_DIM), qspec(IDX_HEADS),
                     pl.BlockSpec((1, l, IDX_DIM), lambda bi, i: (bi, 0, 0)),
                     _const_spec(bt.shape)]
        args += [iqT, wT, ik, bt]
        assert blk == 32 * SUBLANES
        nrow = (l // blk) * SUBLANES
        scratch += [pltpu.VMEM((l, blk), I32), pltpu.VMEM((8, blk), F32),
                    pltpu.VMEM((32, nrow, blk), I32), pltpu.VMEM((nrow, blk), I32)]
    return pl.pallas_call(
        functools.partial(_attn_body, nh, dv, blk, topk, dsa is not None),
        grid=grid, in_specs=in_specs,
        out_specs=pl.BlockSpec((1, blk, nh * dv), lambda bi, i: (bi, i, 0)),
        out_shape=jax.ShapeDtypeStruct((b, l, nh * dv), BF16),
        scratch_shapes=scratch,
        compiler_params=_cparams(("parallel", "arbitrary")),
        name="dsa_attn" if dsa is not None else "mla_attn",
    )(*args)


def _gla_body(tg, q_ref, k_ref, v_ref, la_ref, r_ref, go_ref, bd_ref, o_ref, st_scr, o_scr):
    @pl.when(pl.program_id(1) == 0)
    def _():
        st_scr[...] = jnp.zeros(st_scr.shape, F32)

    cs = B_CHUNK
    la = la_ref[0]
    rl = lax.broadcasted_iota(I32, (tg, LANE), 0) & (cs - 1)
    b = la
    s = 1
    while s < cs:
        b = b + jnp.where(rl >= s, pltpu.roll(b, s, 0), 0.0)
        s *= 2
    q = q_ref[0]
    k = k_ref[0]
    qb = q * jnp.exp(b)
    tri = lax.broadcasted_iota(I32, (cs, cs), 1) <= lax.broadcasted_iota(I32, (cs, cs), 0)
    khead = lax.broadcasted_iota(I32, (cs, LANE), 1) // B_KEY_DIM
    vhead = lax.broadcasted_iota(I32, (cs, GROUP_WIDTH), 1) // B_VAL_DIM
    same_head = (lax.broadcasted_iota(I32, (GROUP_WIDTH, LANE), 0) // B_VAL_DIM
                 == lax.broadcasted_iota(I32, (GROUP_WIDTH, LANE), 1) // B_KEY_DIM)
    for c in range(tg // cs):
        sl = slice(c * cs, (c + 1) * cs)
        bc = b[sl]
        mid = bc[cs // 2:cs // 2 + 1]
        last = bc[cs - 1:cs]
        qe = q[sl] * jnp.exp(bc - mid)
        ke = (k[sl] * jnp.exp(mid - bc)).astype(BF16)
        kd = (k[sl] * jnp.exp(last - bc)).astype(BF16)
        qbc = qb[sl].astype(BF16)
        vc = v_ref[0, sl, :]
        st = st_scr[...]
        qe4 = jnp.concatenate([jnp.where(khead == h, qe, 0.0) for h in range(B_HEADS)], axis=0)
        a_all = _dot_nt(qe4.astype(BF16), ke)
        o = _dot_nt(qbc, st.astype(BF16))
        for h in range(B_HEADS):
            a = jnp.where(tri, a_all[cs * h:cs * (h + 1)], 0.0).astype(BF16)
            o = o + _dot(a, jnp.where(vhead == h, vc, jnp.zeros_like(vc)))
        o_scr[sl, :] = o
        st_scr[...] = st * jnp.exp(last) + jnp.where(same_head, _dot_tn(vc, kd), 0.0)
    o = o_scr[...]
    ms = _group_sum(o * o, bd_ref[...]) * (1.0 / B_VAL_DIM)
    r = r_ref[0]
    o_ref[0] = (o * lax.rsqrt(ms + EPS) * go_ref[...] * (r * jax.nn.sigmoid(r))).astype(o_ref.dtype)


def _gla(bq, bk, bv, bla, br, go, bd, layer, tg=1024):
    b, l, _ = bq.shape
    tg = min(tg, l)
    tok = lambda w: pl.BlockSpec((1, tg, w), lambda bi, i: (bi, i, 0))
    return pl.pallas_call(
        functools.partial(_gla_body, tg),
        grid=(b, l // tg),
        in_specs=[tok(LANE), tok(LANE), tok(GROUP_WIDTH), tok(LANE), tok(GROUP_WIDTH),
                  _const_spec(go.shape, layer), _const_spec(bd.shape)],
        out_specs=tok(GROUP_WIDTH),
        out_shape=jax.ShapeDtypeStruct((b, l, GROUP_WIDTH), BF16),
        scratch_shapes=[pltpu.VMEM((B_HEADS * B_VAL_DIM, B_HEADS * B_KEY_DIM), F32),
                        pltpu.VMEM((tg, GROUP_WIDTH), F32)],
        compiler_params=_cparams(("parallel", "arbitrary")),
        name="gla",
    )(bq, bk, bv, bla, br, go, bd)


CONV_HIST = 32


def _conv_body(tc, h_ref, w_ref, b_ref, g_ref, o_ref, buf):
    @pl.when(pl.program_id(1) == 0)
    def _():
        buf[0:CONV_HIST, :] = jnp.zeros((CONV_HIST, C_CHANNELS), F32)

    @pl.when(pl.program_id(1) > 0)
    def _():
        buf[0:CONV_HIST, :] = buf[tc:tc + CONV_HIST, :]

    buf[CONV_HIST:CONV_HIST + tc, :] = h_ref[0]
    acc = jnp.zeros((tc, C_CHANNELS), F32) + b_ref[...]
    base = CONV_HIST - (C_KERNEL - 1)
    hb = buf[...]
    rows = tc + CONV_HIST
    for r in range(SUBLANES):
        shifted = hb if r == 0 else pltpu.roll(hb, rows - r, 0)
        for j in range(C_KERNEL):
            if (base + j) % SUBLANES == r:
                a0 = base + j - r
                acc = acc + shifted[a0:a0 + tc, :] * w_ref[j:j + 1, :]
    ms = jnp.mean(acc * acc, axis=-1, keepdims=True)
    y = acc * lax.rsqrt(ms + EPS) * g_ref[...]
    o_ref[0] = (y * jax.nn.sigmoid(y)).astype(o_ref.dtype)


def _conv(ch, w, bias, g, layer, tc=512):
    b, l, c = ch.shape
    tc = min(tc, l)
    tok = pl.BlockSpec((1, tc, c), lambda bi, i: (bi, i, 0))
    return pl.pallas_call(
        functools.partial(_conv_body, tc),
        grid=(b, l // tc),
        in_specs=[tok] + [_const_spec(a.shape, layer) for a in (w, bias, g)],
        out_specs=tok,
        out_shape=jax.ShapeDtypeStruct((b, l, c), BF16),
        scratch_shapes=[pltpu.VMEM((tc + CONV_HIST, c), F32)],
        compiler_params=_cparams(("parallel", "arbitrary")),
        name="conv",
    )(ch, w, bias, g)


def _t5_bucket(dist):
    max_exact = REL_BUCKETS // 2
    d = jnp.maximum(dist, 0)
    df = jnp.maximum(d, 1).astype(F32)
    large = max_exact + (jnp.log(df / max_exact) / math.log(REL_MAX_DIST / max_exact)
                         * (REL_BUCKETS - max_exact)).astype(I32)
    large = jnp.minimum(large, REL_BUCKETS - 1)
    return jnp.where(d < max_exact, d, large)


def _pad_cols(w, width):
    return jnp.pad(w, ((0, 0), (0, width - w.shape[1])))


def _lane_rep(v):
    return jnp.broadcast_to(v[:, None], (v.shape[0], LANE))


def _pad_heads_rows(w, heads, dim):
    w = w.reshape(heads, dim, w.shape[1])
    return jnp.pad(w, ((0, 0), (0, HEAD_PAD - dim), (0, 0))).reshape(heads * HEAD_PAD, -1)


def _split_w_in(w_in):
    widths = (256, 256, 256, 256, 32, 8, 128, 128, 256, 16, 256, 512, 256, 128, 32)
    offs = np.cumsum((0,) + widths)
    return [w_in[:, offs[n]:offs[n + 1]] for n in range(len(widths))]


def _row(v):
    return v[None, :].astype(F32)


def _one_layer_params(w):
    (aq, ak, av, iq, ik, iw, bq, bk, bv, bg, br, cu, dcq, dckv, dkpe) = _split_w_in(w["w_in"])
    wn = jnp.concatenate([ak, _pad_cols(ik, LANE), bq, bk, bv, _pad_cols(bg, LANE), br, cu, dcq, dckv,
                          _pad_cols(dkpe, LANE)], axis=1).astype(BF16)
    wt = jnp.concatenate([_pad_heads_rows(aq.T, A_HEADS, A_HEAD_DIM), av.T, iq.T,
                          jnp.pad(iw.T, ((0, T_TOT - T_IW - IDX_HEADS), (0, 0)))], axis=0).astype(BF16)
    pad_to = lambda v, n: jnp.pad(v, (0, n - v.shape[0]))
    ukv = w["d_ukv"].reshape(D_KV_RANK, D_HEADS, D_NOPE + D_V)
    wuk = ukv[:, :, :D_NOPE].reshape(D_KV_RANK, D_HEADS * D_NOPE)
    wuv = ukv[:, :, D_NOPE:].reshape(D_KV_RANK, D_HEADS * D_V)
    gdk = w["d_k_norm"]
    return dict(
        gmix=_row(w["mix_norm"]), wn=wn, wt=wt,
        gaq=_lane_rep(pad_to(w["a_q_norm"], HEAD_PAD)),
        gak=_row(jnp.tile(w["a_k_norm"], A_HEADS)),
        wgu=jnp.pad(w["b_gate_up"], ((0, LANE - B_GATE_RANK), (0, 0))).astype(BF16),
        bgb=_row(w["b_gate_bias"]),
        gqa=_row(w["d_qa_norm"]),
        wuq=_pad_heads_rows(w["d_uq"].T, D_HEADS, D_QK).astype(BF16),
        gdq=_lane_rep(pad_to(w["d_q_norm"], HEAD_PAD)),
        gkva=_row(w["d_kva_norm"]), wuk=wuk.astype(BF16), wuvT=wuv.T.astype(BF16),
        gdk=_row(jnp.tile(gdk[:D_NOPE], D_HEADS)), gdkpe=_row(pad_to(gdk[D_NOPE:], LANE)),
        gbo=_row(jnp.tile(w["b_out_norm"], B_HEADS)),
        cw=jnp.pad(w["c_dw_w"][:, 0, :], ((0, CONV_HIST - C_KERNEL), (0, 0))).astype(F32),
        cb=_row(w["c_dw_b"]), cg=_row(w["c_norm"]),
    )


def _shared_tables(seq):
    hid = np.arange(GROUP_WIDTH) // 64
    bd = jnp.asarray(hid[:, None] == hid[None, :], dtype=BF16)
    half = D_ROPE // 2
    freqs = ROPE_THETA ** (-jnp.arange(half, dtype=F32) / half)
    ang = jnp.arange(seq).astype(F32)[:, None] * freqs[None, :]
    cos, sin = jnp.cos(ang), jnp.sin(ang)
    zeros = jnp.zeros((seq, LANE - D_ROPE), F32)
    cpe = jnp.concatenate([cos, cos, zeros], axis=1)
    spe = jnp.concatenate([-sin, sin, zeros], axis=1)
    return dict(bd=bd, cosT=cos.T, sinT=sin.T, cpe=cpe, spe=spe)


def _bias_tiles(rel_bias, blk):
    assert REL_MAX_DIST <= blk + 1
    kk = jnp.arange(blk)[:, None]
    qq = jnp.arange(blk)[None, :]
    rb = rel_bias.astype(F32).T
    m = 2 * blk + 1
    onehot = _t5_bucket(jnp.arange(m))[None, :, None] == jnp.arange(REL_BUCKETS)
    by_dist = jnp.sum(jnp.where(onehot, rb[:, None, :], 0.0), axis=-1) * LOG2E
    toe = jnp.tile(by_dist, (1, blk))[:, :blk * (m - 1)].reshape(-1, blk, m - 1)
    d0 = jnp.where(kk <= qq, toe[:, :, :blk], NEG)
    d1 = toe[:, :, blk:2 * blk]
    far = jnp.broadcast_to(by_dist[:, m - 1][:, None, None], d1.shape)
    return jnp.stack([d0, d1, far])


def _key_bounds(knorm, bias_max):
    km = jnp.sqrt(jnp.max(knorm[..., 0], axis=1))
    kb_a = jnp.concatenate([km[:, :A_HEADS].reshape(-1), bias_max.reshape(1)])
    return kb_a, km[:, A_HEADS:].reshape(-1)


def kernel(x, ffn1_norm, ffn1_gate, ffn1_up, ffn1_down, mix_norm, w_in, a_q_norm, a_k_norm, rel_bias,
           b_gate_up, b_gate_bias, b_out_norm, c_dw_w, c_dw_b, c_norm, d_qa_norm, d_uq, d_kva_norm,
           d_ukv, d_q_norm, d_k_norm, w_out, ffn2_norm, ffn2_gate, ffn2_up, ffn2_down):
    w = dict(mix_norm=mix_norm, w_in=w_in, a_q_norm=a_q_norm, a_k_norm=a_k_norm, b_gate_up=b_gate_up,
             b_gate_bias=b_gate_bias, b_out_norm=b_out_norm, c_dw_w=c_dw_w, c_dw_b=c_dw_b, c_norm=c_norm,
             d_qa_norm=d_qa_norm, d_uq=d_uq, d_kva_norm=d_kva_norm, d_ukv=d_ukv,
             d_q_norm=d_q_norm, d_k_norm=d_k_norm)
    bsz, seq, dm = x.shape
    depth = w_in.shape[0]
    blk = min(ATT_BLK, seq)
    bt = _bias_tiles(rel_bias, blk)
    bias_max = jnp.max(bt)
    p = {**jax.vmap(_one_layer_params)(w), **_shared_tables(seq)}
    stacked_row = lambda v: v[:, None, :].astype(F32)
    ffn1 = (stacked_row(ffn1_norm), ffn1_gate, ffn1_up, ffn1_down)
    ffn2 = (stacked_row(ffn2_norm), ffn2_gate, ffn2_up, ffn2_down)
    wo = w_out.astype(BF16)
    x2 = x.reshape(bsz * seq, dm)
    for l in range(depth):
        x2 = _ffn(x2, *ffn1, l)
        (aqT, akh, avT, iqT, aik, iwT, bq, bk, bv, bla, br, ch, dqT, dkh, dvT, knorm) = _mix_in(
            x2.reshape(bsz, seq, dm), p, l, blk=blk)
        kb_a, kb_d = _key_bounds(knorm, bias_max)
        y_a = _attention(aqT, akh, avT, kb_a, dsa=(iqT, iwT, aik, bt), blk=blk)
        y_b = _gla(bq, bk, bv, bla, br, p["gbo"], p["bd"], l)
        y_c = _conv(ch, p["cw"], p["cb"], p["cg"], l)
        y_d = _attention(dqT, dkh, dvT, kb_d, blk=blk)
        ys = [y.reshape(bsz * seq, GROUP_WIDTH) for y in (y_a, y_b, y_c, y_d)]
        x2 = _ffn(x2, *ffn2, l, mix=(ys, wo))
    return x2.reshape(bsz, seq, dm)
```

```python
import functools
import math

import jax
import jax.numpy as jnp
import numpy as np
from jax import lax
from jax.experimental import pallas as pl
from jax.experimental.pallas import tpu as pltpu

F32 = jnp.float32
BF16 = jnp.bfloat16
I32 = jnp.int32

EPS = 1e-6
GROUP_WIDTH = 256
A_HEADS, A_HEAD_DIM = 4, 64
IDX_HEADS, IDX_DIM = 8, 32
TOPK_MAX = 256
REL_BUCKETS, REL_MAX_DIST = 32, 128
B_HEADS, B_KEY_DIM, B_VAL_DIM, B_GATE_RANK = 4, 32, 64, 16
B_GATE_TAU = 16.0
B_CHUNK = 64
C_CHANNELS, C_KERNEL = 256, 31
D_HEADS, D_Q_RANK, D_KV_RANK, D_NOPE, D_ROPE, D_V = 4, 256, 128, 64, 32, 64
D_QK = D_NOPE + D_ROPE
ROPE_THETA = 10000.0

LANE = 128
SUBLANES = 8
HEAD_PAD = 128
ONES_ROWS = 16
ATT_BLK = 256
ATT_GROUP = 4
SCORE_GROUP = 4
BOUND_SLACK = 1.01
FIXED_REF_MAX = 40.0
INT_MIN = -2 ** 31
NEG = -1e30
LOG2E = math.log2(math.e)
VMEM_LIMIT = 56 * 1024 * 1024

N_AK, N_IK, N_BQ, N_BK, N_BV, N_BG, N_BR, N_CU, N_DCQ, N_DCKV, N_DKPE, N_TOT = (
    0, 256, 384, 512, 640, 896, 1024, 1280, 1792, 2048, 2176, 2304)
T_AQ, T_AV, T_IQ, T_IW, T_TOT = 0, 512, 768, 1024, 1040


def _dot(a, b):
    return jnp.dot(a, b, preferred_element_type=F32)


def _dot_nt(a, b):
    return lax.dot_general(a, b, (((1,), (1,)), ((), ())), preferred_element_type=F32)


def _dot_tn(a, b):
    return lax.dot_general(a, b, (((0,), (0,)), ((), ())), preferred_element_type=F32)


def _group_sum(x2, bd):
    hi = x2.astype(BF16)
    lo = (x2 - hi.astype(F32)).astype(BF16)
    return _dot(hi, bd) + _dot(lo, bd)


def _const_spec(shape, layer=None):
    nd = len(shape)
    if layer is None:
        return pl.BlockSpec(shape, lambda *_: (0,) * nd, pipeline_mode=pl.Buffered(1))
    return pl.BlockSpec((None,) + tuple(shape[1:]), lambda *_: (layer,) + (0,) * (nd - 1),
                        pipeline_mode=pl.Buffered(1))


def _cparams(sem):
    return pltpu.CompilerParams(dimension_semantics=sem, vmem_limit_bytes=VMEM_LIMIT)


def _ffn_body(has_mix, fc, *refs):
    if has_mix:
        x_ref, ya, yb, yc, yd, wo_ref, g_ref, wg_ref, wu_ref, wd_ref, o_ref, h_scr = refs
    else:
        x_ref, g_ref, wg_ref, wu_ref, wd_ref, o_ref, h_scr = refs
    x = x_ref[...]
    if has_mix:
        y = jnp.concatenate([ya[...], yb[...], yc[...], yd[...]], axis=-1)
        x = x + _dot(y, wo_ref[...])
    ms = jnp.mean(x * x, axis=-1, keepdims=True)
    xn = (x * lax.rsqrt(ms + EPS) * g_ref[...]).astype(BF16)
    d_ff = wg_ref.shape[1]
    for c in range(d_ff // fc):
        sl = slice(c * fc, (c + 1) * fc)
        gate = _dot(xn, wg_ref[:, sl].astype(BF16))
        up = _dot(xn, wu_ref[:, sl].astype(BF16))
        h_scr[:, sl] = (gate * jax.nn.sigmoid(gate) * up).astype(BF16)
    o_ref[...] = x + 0.5 * _dot(h_scr[...], wd_ref[...].astype(BF16))


def _ffn(x2, g, wg, wu, wd, layer, mix=None, tm=512, fc=256):
    m, d = x2.shape
    d_ff = wg.shape[2]
    tm = min(tm, m)
    row = lambda w: pl.BlockSpec((tm, w), lambda i: (i, 0))
    in_specs = [row(d)]
    args = [x2]
    if mix is not None:
        ys, wo = mix
        in_specs += [row(GROUP_WIDTH)] * 4 + [_const_spec(wo.shape, layer)]
        args += list(ys) + [wo]
    in_specs += [_const_spec(a.shape, layer) for a in (g, wg, wu, wd)]
    args += [g, wg, wu, wd]
    return pl.pallas_call(
        functools.partial(_ffn_body, mix is not None, fc),
        grid=(m // tm,),
        in_specs=in_specs,
        out_specs=row(d),
        out_shape=jax.ShapeDtypeStruct((m, d), F32),
        scratch_shapes=[pltpu.VMEM((tm, d_ff), BF16)],
        compiler_params=_cparams(("parallel",)),
        name="ffn_mix" if mix is not None else "ffn",
    )(*args)


def _mix_in_body(tm, blk,
                 x_ref, gmix_ref, wn_ref, wt_ref, bd_ref,
                 gaq_ref, gak_ref,
                 wgu_ref, bgb_ref,
                 gqa_ref, wuq_ref, gdq_ref, cosT_ref, sinT_ref,
                 gkva_ref, wuk_ref, wuvT_ref, gdk_ref, gdkpe_ref, cpe_ref, spe_ref,
                 aqT_ref, ak_ref, avT_ref, iqT_ref, ik_ref, iwT_ref,
                 bq_ref, bk_ref, bv_ref, bla_ref, br_ref,
                 ch_ref,
                 dqT_ref, dk_ref, dvT_ref, knorm_ref):
    nlt = tm // LANE
    x = x_ref[0]
    ms = jnp.mean(x * x, axis=-1, keepdims=True)
    xn = (x * lax.rsqrt(ms + EPS) * gmix_ref[...]).astype(BF16)
    bd = bd_ref[...]
    lane = lax.broadcasted_iota(I32, (tm, LANE), 1)

    def lanes(g):
        return jnp.tile(g, (1, nlt))

    z = _dot(xn, wn_ref[...])
    zt = _dot_nt(wt_ref[...], xn)

    def zs(off, width):
        return z[:, off:off + width]

    cq = zs(N_DCQ, D_Q_RANK)
    cq_ms = jnp.mean(cq * cq, axis=-1, keepdims=True)
    cqn = (cq * lax.rsqrt(cq_ms + EPS) * gqa_ref[...]).astype(BF16)
    ckv = zs(N_DCKV, D_KV_RANK)
    ckv_ms = jnp.mean(ckv * ckv, axis=-1, keepdims=True)
    ckvn = (ckv * lax.rsqrt(ckv_ms + EPS) * gkva_ref[...]).astype(BF16)
    dq = _dot_nt(wuq_ref[...], cqn).reshape(D_HEADS, HEAD_PAD, tm)
    kn = _dot(ckvn, wuk_ref[...])
    dv = _dot_nt(wuvT_ref[...], ckvn).astype(BF16)
    gate = _dot(zs(N_BG, LANE).astype(BF16), wgu_ref[...]) + bgb_ref[...]
    ak = zs(N_AK, GROUP_WIDTH)
    ak_ms = _group_sum(ak * ak, bd) * (1.0 / A_HEAD_DIM)
    kn_ss = _group_sum(kn * kn, bd)

    aq = zt[T_AQ:T_AQ + A_HEADS * HEAD_PAD].reshape(A_HEADS, HEAD_PAD, tm)
    aq_ms = jnp.sum(aq * aq, axis=1, keepdims=True) * (1.0 / A_HEAD_DIM)
    aq = aq * lax.rsqrt(aq_ms + EPS) * lanes(gaq_ref[...])[None] * (A_HEAD_DIM ** -0.5 * LOG2E)
    aqT_ref[0] = aq.reshape(A_HEADS * HEAD_PAD, tm).astype(BF16)
    def emit_key_norms(first_row, sq_norms):
        top = jnp.max(sq_norms, axis=0, keepdims=True)
        for h in range(4):
            knorm_ref[0, 0, first_row + h:first_row + h + 1, :] = jnp.broadcast_to(
                top[:, 64 * h:64 * h + 1], (1, LANE))

    def with_ones_rows(vt, heads, dim):
        ones = jnp.ones((ONES_ROWS, tm), BF16)
        return jnp.concatenate([r for h in range(heads) for r in (vt[dim * h:dim * (h + 1)], ones)], axis=0)

    av = with_ones_rows(zt[T_AV:T_AV + GROUP_WIDTH].astype(BF16), A_HEADS, A_HEAD_DIM)
    for c in range(tm // blk):
        avT_ref[0, c] = av[:, c * blk:(c + 1) * blk]
    iqT_ref[0] = zt[T_IQ:T_IQ + IDX_HEADS * IDX_DIM].astype(BF16)
    iwT_ref[0] = zt[T_IW:T_IW + IDX_HEADS] * ((IDX_HEADS ** -0.5) * (IDX_DIM ** -0.5))

    ak = ak * lax.rsqrt(ak_ms + EPS) * gak_ref[...]
    for h in range(A_HEADS):
        pair = ak[:, LANE * (h // 2):LANE * (h // 2) + LANE]
        if h % 2 == 1:
            pair = pltpu.roll(pair, 64, 1)
        kh = jnp.where(lane < A_HEAD_DIM, pair, 0.0).astype(BF16)
        ak_ref[0, h] = kh
    emit_key_norms(0, _dot((ak * ak).astype(BF16), bd))
    ik_ref[0] = zs(N_IK, LANE)[:, :IDX_DIM].astype(BF16)

    bq_ref[0] = zs(N_BQ, LANE) * (B_KEY_DIM ** -0.5)
    bk_ref[0] = zs(N_BK, LANE)
    bv_ref[0] = zs(N_BV, GROUP_WIDTH).astype(BF16)
    bla_ref[0] = (jnp.minimum(gate, 0.0) - jnp.log(1.0 + jnp.exp(-jnp.abs(gate)))) * (1.0 / B_GATE_TAU)
    br_ref[0] = zs(N_BR, GROUP_WIDTH)

    ca = zs(N_CU, C_CHANNELS)
    cg = zs(N_CU + C_CHANNELS, C_CHANNELS)
    ch_ref[0] = ca * jax.nn.sigmoid(cg)

    dq_ms = jnp.sum(dq * dq, axis=1, keepdims=True) * (1.0 / D_QK)
    dq = dq * lax.rsqrt(dq_ms + EPS) * lanes(gdq_ref[...])[None] * (D_QK ** -0.5 * LOG2E)
    half = D_ROPE // 2
    x1 = dq[:, D_NOPE:D_NOPE + half]
    x2 = dq[:, D_NOPE + half:D_QK]
    cs = cosT_ref[...][None]
    sn = sinT_ref[...][None]
    dq = jnp.concatenate([dq[:, :D_NOPE], x1 * cs - x2 * sn, x2 * cs + x1 * sn, dq[:, D_QK:]], axis=1)
    dqT_ref[0] = dq.reshape(D_HEADS * HEAD_PAD, tm).astype(BF16)

    dv = with_ones_rows(dv, D_HEADS, D_V)
    for c in range(tm // blk):
        dvT_ref[0, c] = dv[:, c * blk:(c + 1) * blk]
    kpe = zs(N_DKPE, LANE)
    ss = kn_ss + jnp.sum(kpe * kpe, axis=-1, keepdims=True)
    rinv = lax.rsqrt(ss * (1.0 / D_QK) + EPS)
    kn = kn * rinv * gdk_ref[...]
    pe = kpe * gdkpe_ref[...]
    partner = jnp.where(lane < half, pltpu.roll(pe, LANE - half, 1), pltpu.roll(pe, half, 1))
    pe = pe * cpe_ref[...] + partner * spe_ref[...]
    pe = pltpu.roll(pe, D_NOPE, 1)
    for h in range(D_HEADS):
        pair = kn[:, LANE * (h // 2):LANE * (h // 2) + LANE]
        rpair = rinv[:, LANE * (h // 2):LANE * (h // 2) + LANE]
        if h % 2 == 1:
            pair = pltpu.roll(pair, 64, 1)
        else:
            rpair = pltpu.roll(rpair, 64, 1)
        kh = jnp.where(lane < D_NOPE, pair, pe * rpair).astype(BF16)
        dk_ref[0, h] = kh
    pe_sq = _dot((pe * pe).astype(BF16), jnp.ones((LANE, LANE), BF16))
    emit_key_norms(A_HEADS, _dot((kn * kn).astype(BF16), bd) + jnp.tile(pe_sq, (1, 2)) * (rinv * rinv))


def _mix_in(x3, p, layer, tm=512, blk=ATT_BLK):
    b, l, d = x3.shape
    tm = min(tm, l)
    grid = (b, l // tm)
    nck = l // blk
    tok = lambda w: pl.BlockSpec((1, tm, w), lambda bi, i: (bi, i, 0))
    tokT = lambda r: pl.BlockSpec((1, r, tm), lambda bi, i: (bi, 0, i))
    headk = pl.BlockSpec((1, 4, tm, HEAD_PAD), lambda bi, i: (bi, 0, i, 0))
    vrows = GROUP_WIDTH + 4 * ONES_ROWS
    chunkT = pl.BlockSpec((1, tm // blk, vrows, blk), lambda bi, i: (bi, i, 0, 0))
    postab = lambda r: pl.BlockSpec((r, tm), lambda bi, i: (0, i))
    posrow = pl.BlockSpec((tm, LANE), lambda bi, i: (i, 0))
    consts = [p["gmix"], p["wn"], p["wt"], p["bd"], p["gaq"], p["gak"], p["wgu"], p["bgb"],
              p["gqa"], p["wuq"], p["gdq"]]
    consts2 = [p["gkva"], p["wuk"], p["wuvT"], p["gdk"], p["gdkpe"]]
    lspec = lambda a: _const_spec(a.shape) if a is p["bd"] else _const_spec(a.shape, layer)
    in_specs = ([tok(d)] + [lspec(a) for a in consts]
                + [postab(D_ROPE // 2), postab(D_ROPE // 2)]
                + [lspec(a) for a in consts2] + [posrow, posrow])
    args = [x3] + consts + [p["cosT"], p["sinT"]] + consts2 + [p["cpe"], p["spe"]]
    sd = jax.ShapeDtypeStruct
    out_shape = [
        sd((b, A_HEADS * HEAD_PAD, l), BF16), sd((b, A_HEADS, l, HEAD_PAD), BF16),
        sd((b, nck, vrows, blk), BF16), sd((b, IDX_HEADS * IDX_DIM, l), BF16),
        sd((b, l, IDX_DIM), BF16), sd((b, IDX_HEADS, l), F32),
        sd((b, l, LANE), F32), sd((b, l, LANE), F32), sd((b, l, GROUP_WIDTH), BF16),
        sd((b, l, LANE), F32), sd((b, l, GROUP_WIDTH), F32),
        sd((b, l, C_CHANNELS), F32),
        sd((b, D_HEADS * HEAD_PAD, l), BF16), sd((b, D_HEADS, l, HEAD_PAD), BF16),
        sd((b, nck, vrows, blk), BF16),
        sd((b, l // tm, A_HEADS + D_HEADS, LANE), F32),
    ]
    out_specs = [
        tokT(A_HEADS * HEAD_PAD), headk, chunkT, tokT(IDX_HEADS * IDX_DIM),
        tok(IDX_DIM), tokT(IDX_HEADS),
        tok(LANE), tok(LANE), tok(GROUP_WIDTH), tok(LANE), tok(GROUP_WIDTH),
        tok(C_CHANNELS),
        tokT(D_HEADS * HEAD_PAD), headk, chunkT,
        pl.BlockSpec((1, 1, A_HEADS + D_HEADS, LANE), lambda bi, i: (bi, i, 0, 0)),
    ]
    return pl.pallas_call(
        functools.partial(_mix_in_body, tm, blk),
        grid=grid, in_specs=in_specs, out_specs=out_specs, out_shape=out_shape,
        compiler_params=_cparams(("parallel", "parallel")),
        name="mix_in",
    )(*args)


def _attn_body(nh, dv, blk, topk, is_dsa, *refs):
    if is_dsa:
        (kb_ref, qT_ref, k_ref, vT_ref, iqT_ref, wT_ref, ik_ref, bt_ref,
         o_ref, m_scr, acc_scr, key_scr, run_scr, plane_scr, active_scr) = refs
    else:
        kb_ref, qT_ref, k_ref, vT_ref, o_ref, m_scr, acc_scr = refs
    vr = dv + ONES_ROWS
    i = pl.program_id(1)
    t = blk
    row = lax.broadcasted_iota(I32, (t, t), 0)
    col = lax.broadcasted_iota(I32, (t, t), 1)
    causal_pen = jnp.where(row <= col, 0.0, NEG)

    m_scr[...] = jnp.full(m_scr.shape, NEG, F32)
    acc_scr[...] = jnp.zeros(acc_scr.shape, F32)

    if is_dsa:
        @pl.when(i == 0)
        def _():
            plane_scr[...] = jnp.zeros(plane_scr.shape, I32)

        def score_chunk(j, diag):
            r0 = pl.multiple_of(j * t, t)
            ikc = ik_ref[0, pl.ds(r0, t), :]
            s = jnp.zeros((t, t), F32)
            for h in range(IDX_HEADS):
                d = _dot(ikc, iqT_ref[0, IDX_DIM * h:IDX_DIM * (h + 1), :])
                s = s + jnp.maximum(d, 0.0) * wT_ref[0, h:h + 1, :]
            bits = lax.bitcast_convert_type(s, I32)
            key = jnp.where(bits < 0, bits ^ 0x7FFFFFFF, bits)
            if diag:
                key = jnp.where(row <= col, key, INT_MIN)
            key_scr[pl.ds(r0, t), :] = key
            w = [key[SUBLANES * r:SUBLANES * (r + 1), :] for r in range(32)]
            step, mask = 16, 0x0000FFFF
            while step:
                for lo in range(32):
                    if lo & step == 0:
                        hi = lo + step
                        swap = (w[lo] ^ jnp.right_shift(w[hi], step)) & mask
                        w[lo] = w[lo] ^ swap
                        w[hi] = w[hi] ^ jnp.left_shift(swap, step)
                step //= 2
                mask ^= (mask << step) & 0xFFFFFFFF
            c0 = pl.multiple_of(j * SUBLANES, SUBLANES)
            w[0] = ~w[0]
            for p in range(32):
                plane_scr[p, pl.ds(c0, SUBLANES), :] = w[p]

        def score_group(u, carry):
            for c in range(SCORE_GROUP):
                score_chunk(SCORE_GROUP * u + c, False)
            return carry

        lax.fori_loop(0, i // SCORE_GROUP, score_group, 0)
        for rem in range(SCORE_GROUP):
            @pl.when(i % SCORE_GROUP == rem)
            def _():
                for c in range(rem):
                    score_chunk(i - rem + c, False)
                score_chunk(i, True)

        def kth_largest(nrow):
            rows = pl.ds(0, nrow)
            in_range = lax.broadcasted_iota(I32, (nrow, t), 0) < (i + 1) * SUBLANES

            def col_count(words):
                pc = lax.population_count(words).reshape(nrow // SUBLANES, SUBLANES, t)
                return jnp.sum(jnp.sum(pc, axis=0), axis=0, keepdims=True)

            def decide(plane, active, n_gt, ans_u):
                ones = col_count(active & plane_scr[plane, rows, :])
                take = n_gt + ones >= topk
                bit = lax.shift_right_logical(jnp.int32(INT_MIN), jnp.int32(plane))
                return (jnp.where(take, 0, -1), n_gt + jnp.where(take, 0, ones),
                        ans_u | jnp.where(take, bit, 0))

            active_scr[rows, :] = jnp.where(in_range, -1, 0)
            state = decide(0, active_scr[rows, :], jnp.zeros((1, t), I32), jnp.zeros((1, t), I32))

            def plane_body(plane, state):
                flip, n_gt, ans_u = state
                active = active_scr[rows, :] & (plane_scr[plane - 1, rows, :] ^ flip)
                active_scr[rows, :] = active
                return decide(plane, active, n_gt, ans_u)

            flip, n_gt, ans_u = lax.fori_loop(1, 32, plane_body, state)
            n_eq = col_count(active_scr[rows, :] & (plane_scr[31, rows, :] ^ flip))
            return n_gt, ans_u, n_eq

        nrow_all = plane_scr.shape[1]
        nrow_half = (nrow_all // SUBLANES // 2) * SUBLANES
        if nrow_half:
            n_gt, ans_u, n_eq = lax.cond((i + 1) * SUBLANES <= nrow_half,
                                         functools.partial(kth_largest, nrow_half),
                                         functools.partial(kth_largest, nrow_all))
        else:
            n_gt, ans_u, n_eq = kth_largest(nrow_all)
        ans = ans_u ^ INT_MIN
        need = (topk - n_gt).astype(F32)
        takes_all = jnp.logical_or(topk - n_gt >= n_eq, ans == INT_MIN)
        plain_selection = jnp.min(jnp.where(takes_all, 1, 0)) == 1
        run_scr[...] = jnp.zeros(run_scr.shape, F32)
        stri = jnp.where(col < row, 1.0, 0.0).astype(BF16)

    def logits(h, r0):
        return _dot(k_ref[0, h, pl.ds(r0, t), :], qT_ref[0, HEAD_PAD * h:HEAD_PAD * (h + 1), :])

    def selection_pens(r0s):
        pens = []
        if is_dsa:
            for r0 in r0s:
                kc = key_scr[pl.ds(r0, t), :]
                eq = kc == ans
                eqf = jnp.where(eq, 1.0, 0.0)
                run = run_scr[0:1, :]
                rank = _dot(stri, eqf.astype(BF16)) + run
                run_scr[0:1, :] = run + jnp.sum(eqf, axis=0, keepdims=True)
                pens.append(jnp.where(kc > ans, 0.0, jnp.where(eq, jnp.where(rank < need, 0.0, NEG), NEG)))
        return pens

    b_idx = pl.program_id(0)
    bounds = []
    for h in range(nh):
        qh = qT_ref[0, HEAD_PAD * h:HEAD_PAD * (h + 1), :].astype(F32)
        bound = jnp.sqrt(jnp.sum(qh * qh, axis=0, keepdims=True)) * (kb_ref[b_idx * nh + h] * BOUND_SLACK)
        if is_dsa:
            bound = bound + kb_ref[kb_ref.shape[0] - 1]
        bounds.append(bound)
    bound_max = jnp.max(functools.reduce(jnp.maximum, bounds))
    fixed_ref_ok = bound_max <= FIXED_REF_MAX

    def fixed_ref_chunks(plain, js, kinds):
        n = len(js)
        r0s = [pl.multiple_of(j * t, t) for j in js]
        lgs = [[logits(h, r0s[c]) for c in range(n)] for h in range(nh)]
        if is_dsa and plain:
            pens = [jnp.where(key_scr[pl.ds(r0, t), :] >= ans, 0.0, NEG) for r0 in r0s]
        else:
            pens = selection_pens(r0s)
        ps = []
        for h in range(nh):
            row_ps = []
            if is_dsa:
                far_ref = bounds[h] - bt_ref[2, h, 0:1, :]
            for c in range(n):
                x, ref = lgs[h][c], bounds[h]
                if is_dsa and kinds[c] == "far":
                    x, ref = x + pens[c], far_ref
                elif is_dsa:
                    x = x + (pens[c] + bt_ref[0 if kinds[c] == "diag" else 1, h])
                elif kinds[c] == "diag":
                    x = x + causal_pen
                row_ps.append(jnp.exp2(x - ref).astype(BF16))
            ps.append(row_ps[0] if n == 1 else jnp.concatenate(row_ps, axis=0))
        for h in range(nh):
            vs = slice(vr * h, vr * (h + 1))
            vt = [vT_ref[0, js[c], vs, :] for c in range(n)]
            acc_scr[vs, :] += _dot(vt[0] if n == 1 else jnp.concatenate(vt, axis=1), ps[h])

    def super_chunk(js, kinds):
        n = len(js)
        r0s = [pl.multiple_of(j * t, t) for j in js]
        lgs = [[logits(h, r0s[c]) for c in range(n)] for h in range(nh)]
        pens = selection_pens(r0s)
        ps, alphas = [], []
        for h in range(nh):
            xs = []
            for c, j in enumerate(js):
                diag = kinds[c] == "diag"
                lg = lgs[h][c]
                if is_dsa:
                    tile = 0 if diag else jnp.minimum(i - j, 2)
                    lg = lg + (pens[c] + bt_ref[tile, h])
                elif diag:
                    lg = lg + causal_pen
                xs.append(lg)
            m_old = m_scr[h, 0:1, :]
            m_new = m_old
            for x in xs:
                m_new = jnp.maximum(m_new, jnp.max(x, axis=0, keepdims=True))
            alpha = jnp.exp2(m_old - m_new)
            m_scr[h, 0:1, :] = m_new
            ps.append([jnp.exp2((x - m_new).astype(BF16)) for x in xs])
            alphas.append(alpha)
        for h in range(nh):
            vs = slice(vr * h, vr * (h + 1))
            pv = _dot(vT_ref[0, js[0], vs, :], ps[h][0])
            for c in range(1, n):
                pv = pv + _dot(vT_ref[0, js[c], vs, :], ps[h][c])
            acc_scr[vs, :] = alphas[h] * acc_scr[vs, :] + pv

    tail = ["near", "diag"] if is_dsa else ["diag"]
    nfar = jnp.maximum(i + 1 - len(tail), 0)

    def run_chunks(step):
        def group_body(u, carry):
            step([ATT_GROUP * u + c for c in range(ATT_GROUP)], ["far"] * ATT_GROUP)
            return carry

        lax.fori_loop(0, nfar // ATT_GROUP, group_body, 0)
        for rem in range(ATT_GROUP):
            @pl.when(jnp.logical_and(i + 1 >= len(tail), nfar % ATT_GROUP == rem))
            def _():
                first = i + 1 - len(tail) - rem
                step([first + c for c in range(rem + len(tail))], ["far"] * rem + tail)

        if is_dsa:
            @pl.when(i == 0)
            def _():
                step([i], ["diag"])

    if is_dsa:
        for plain in (True, False):
            cond = jnp.logical_and(fixed_ref_ok, plain_selection == plain)
            pl.when(cond)(functools.partial(run_chunks, functools.partial(fixed_ref_chunks, plain)))
    else:
        pl.when(fixed_ref_ok)(functools.partial(run_chunks, functools.partial(fixed_ref_chunks, True)))
    pl.when(jnp.logical_not(fixed_ref_ok))(functools.partial(run_chunks, super_chunk))

    outs = []
    for h in range(nh):
        outs.append(acc_scr[vr * h:vr * h + dv, :] / acc_scr[vr * h + dv:vr * h + dv + 1, :])
    o_ref[0] = jnp.transpose(jnp.concatenate(outs, axis=0)).astype(o_ref.dtype)


def _attention(qT, k, vT, kbound, dsa=None, blk=ATT_BLK):
    b, nh, l, _ = k.shape
    vr = vT.shape[2] // nh
    dv = vr - ONES_ROWS
    grid = (b, l // blk)
    qspec = lambda r: pl.BlockSpec((1, r, blk), lambda bi, i: (bi, 0, i))
    kspec = pl.BlockSpec((1, nh, l, HEAD_PAD), lambda bi, i: (bi, 0, 0, 0))
    vspec = pl.BlockSpec((1, l // blk, nh * vr, blk), lambda bi, i: (bi, 0, 0, 0))
    in_specs = [pl.BlockSpec(memory_space=pltpu.SMEM), qspec(nh * HEAD_PAD), kspec, vspec]
    args = [kbound, qT, k, vT]
    scratch = [pltpu.VMEM((nh, 8, blk), F32), pltpu.VMEM((nh * vr, blk), F32)]
    topk = 0
    if dsa is not None:
        iqT, wT, ik, bt = dsa
        topk = min(TOPK_MAX, l // 4)
        in_specs += [qspec(IDX_HEADS * IDX_DIM), qspec(IDX_HEADS),
                     pl.BlockSpec((1, l, IDX_DIM), lambda bi, i: (bi, 0, 0)),
                     _const_spec(bt.shape)]
        args += [iqT, wT, ik, bt]
        assert blk == 32 * SUBLANES
        nrow = (l // blk) * SUBLANES
        scratch += [pltpu.VMEM((l, blk), I32), pltpu.VMEM((8, blk), F32),
                    pltpu.VMEM((32, nrow, blk), I32), pltpu.VMEM((nrow, blk), I32)]
    return pl.pallas_call(
        functools.partial(_attn_body, nh, dv, blk, topk, dsa is not None),
        grid=grid, in_specs=in_specs,
        out_specs=pl.BlockSpec((1, blk, nh * dv), lambda bi, i: (bi, i, 0)),
        out_shape=jax.ShapeDtypeStruct((b, l, nh * dv), BF16),
        scratch_shapes=scratch,
        compiler_params=_cparams(("parallel", "arbitrary")),
        name="dsa_attn" if dsa is not None else "mla_attn",
    )(*args)


def _gla_body(tg, q_ref, k_ref, v_ref, la_ref, r_ref, go_ref, bd_ref, o_ref, st_scr, o_scr):
    @pl.when(pl.program_id(1) == 0)
    def _():
        st_scr[...] = jnp.zeros(st_scr.shape, F32)

    cs = B_CHUNK
    la = la_ref[0]
    rl = lax.broadcasted_iota(I32, (tg, LANE), 0) & (cs - 1)
    b = la
    s = 1
    while s < cs:
        b = b + jnp.where(rl >= s, pltpu.roll(b, s, 0), 0.0)
        s *= 2
    q = q_ref[0]
    k = k_ref[0]
    qb = q * jnp.exp(b)
    tri = lax.broadcasted_iota(I32, (cs, cs), 1) <= lax.broadcasted_iota(I32, (cs, cs), 0)
    khead = lax.broadcasted_iota(I32, (cs, LANE), 1) // B_KEY_DIM
    vhead = lax.broadcasted_iota(I32, (cs, GROUP_WIDTH), 1) // B_VAL_DIM
    same_head = (lax.broadcasted_iota(I32, (GROUP_WIDTH, LANE), 0) // B_VAL_DIM
                 == lax.broadcasted_iota(I32, (GROUP_WIDTH, LANE), 1) // B_KEY_DIM)
    for c in range(tg // cs):
        sl = slice(c * cs, (c + 1) * cs)
        bc = b[sl]
        mid = bc[cs // 2:cs // 2 + 1]
        last = bc[cs - 1:cs]
        qe = q[sl] * jnp.exp(bc - mid)
        ke = (k[sl] * jnp.exp(mid - bc)).astype(BF16)
        kd = (k[sl] * jnp.exp(last - bc)).astype(BF16)
        qbc = qb[sl].astype(BF16)
        vc = v_ref[0, sl, :]
        st = st_scr[...]
        qe4 = jnp.concatenate([jnp.where(khead == h, qe, 0.0) for h in range(B_HEADS)], axis=0)
        a_all = _dot_nt(qe4.astype(BF16), ke)
        o = _dot_nt(qbc, st.astype(BF16))
        for h in range(B_HEADS):
            a = jnp.where(tri, a_all[cs * h:cs * (h + 1)], 0.0).astype(BF16)
            o = o + _dot(a, jnp.where(vhead == h, vc, jnp.zeros_like(vc)))
        o_scr[sl, :] = o
        st_scr[...] = st * jnp.exp(last) + jnp.where(same_head, _dot_tn(vc, kd), 0.0)
    o = o_scr[...]
    ms = _group_sum(o * o, bd_ref[...]) * (1.0 / B_VAL_DIM)
    r = r_ref[0]
    o_ref[0] = (o * lax.rsqrt(ms + EPS) * go_ref[...] * (r * jax.nn.sigmoid(r))).astype(o_ref.dtype)


def _gla(bq, bk, bv, bla, br, go, bd, layer, tg=1024):
    b, l, _ = bq.shape
    tg = min(tg, l)
    tok = lambda w: pl.BlockSpec((1, tg, w), lambda bi, i: (bi, i, 0))
    return pl.pallas_call(
        functools.partial(_gla_body, tg),
        grid=(b, l // tg),
        in_specs=[tok(LANE), tok(LANE), tok(GROUP_WIDTH), tok(LANE), tok(GROUP_WIDTH),
                  _const_spec(go.shape, layer), _const_spec(bd.shape)],
        out_specs=tok(GROUP_WIDTH),
        out_shape=jax.ShapeDtypeStruct((b, l, GROUP_WIDTH), BF16),
        scratch_shapes=[pltpu.VMEM((B_HEADS * B_VAL_DIM, B_HEADS * B_KEY_DIM), F32),
                        pltpu.VMEM((tg, GROUP_WIDTH), F32)],
        compiler_params=_cparams(("parallel", "arbitrary")),
        name="gla",
    )(bq, bk, bv, bla, br, go, bd)


CONV_HIST = 32


def _conv_body(tc, h_ref, w_ref, b_ref, g_ref, o_ref, buf):
    @pl.when(pl.program_id(1) == 0)
    def _():
        buf[0:CONV_HIST, :] = jnp.zeros((CONV_HIST, C_CHANNELS), F32)

    @pl.when(pl.program_id(1) > 0)
    def _():
        buf[0:CONV_HIST, :] = buf[tc:tc + CONV_HIST, :]

    buf[CONV_HIST:CONV_HIST + tc, :] = h_ref[0]
    acc = jnp.zeros((tc, C_CHANNELS), F32) + b_ref[...]
    base = CONV_HIST - (C_KERNEL - 1)
    hb = buf[...]
    rows = tc + CONV_HIST
    for r in range(SUBLANES):
        shifted = hb if r == 0 else pltpu.roll(hb, rows - r, 0)
        for j in range(C_KERNEL):
            if (base + j) % SUBLANES == r:
                a0 = base + j - r
                acc = acc + shifted[a0:a0 + tc, :] * w_ref[j:j + 1, :]
    ms = jnp.mean(acc * acc, axis=-1, keepdims=True)
    y = acc * lax.rsqrt(ms + EPS) * g_ref[...]
    o_ref[0] = (y * jax.nn.sigmoid(y)).astype(o_ref.dtype)


def _conv(ch, w, bias, g, layer, tc=512):
    b, l, c = ch.shape
    tc = min(tc, l)
    tok = pl.BlockSpec((1, tc, c), lambda bi, i: (bi, i, 0))
    return pl.pallas_call(
        functools.partial(_conv_body, tc),
        grid=(b, l // tc),
        in_specs=[tok] + [_const_spec(a.shape, layer) for a in (w, bias, g)],
        out_specs=tok,
        out_shape=jax.ShapeDtypeStruct((b, l, c), BF16),
        scratch_shapes=[pltpu.VMEM((tc + CONV_HIST, c), F32)],
        compiler_params=_cparams(("parallel", "arbitrary")),
        name="conv",
    )(ch, w, bias, g)


def _t5_bucket(dist):
    max_exact = REL_BUCKETS // 2
    d = jnp.maximum(dist, 0)
    df = jnp.maximum(d, 1).astype(F32)
    large = max_exact + (jnp.log(df / max_exact) / math.log(REL_MAX_DIST / max_exact)
                         * (REL_BUCKETS - max_exact)).astype(I32)
    large = jnp.minimum(large, REL_BUCKETS - 1)
    return jnp.where(d < max_exact, d, large)


def _pad_cols(w, width):
    return jnp.pad(w, ((0, 0), (0, width - w.shape[1])))


def _lane_rep(v):
    return jnp.broadcast_to(v[:, None], (v.shape[0], LANE))


def _pad_heads_rows(w, heads, dim):
    w = w.reshape(heads, dim, w.shape[1])
    return jnp.pad(w, ((0, 0), (0, HEAD_PAD - dim), (0, 0))).reshape(heads * HEAD_PAD, -1)


def _split_w_in(w_in):
    widths = (256, 256, 256, 256, 32, 8, 128, 128, 256, 16, 256, 512, 256, 128, 32)
    offs = np.cumsum((0,) + widths)
    return [w_in[:, offs[n]:offs[n + 1]] for n in range(len(widths))]


def _row(v):
    return v[None, :].astype(F32)


def _one_layer_params(w):
    (aq, ak, av, iq, ik, iw, bq, bk, bv, bg, br, cu, dcq, dckv, dkpe) = _split_w_in(w["w_in"])
    wn = jnp.concatenate([ak, _pad_cols(ik, LANE), bq, bk, bv, _pad_cols(bg, LANE), br, cu, dcq, dckv,
                          _pad_cols(dkpe, LANE)], axis=1).astype(BF16)
    wt = jnp.concatenate([_pad_heads_rows(aq.T, A_HEADS, A_HEAD_DIM), av.T, iq.T,
                          jnp.pad(iw.T, ((0, T_TOT - T_IW - IDX_HEADS), (0, 0)))], axis=0).astype(BF16)
    pad_to = lambda v, n: jnp.pad(v, (0, n - v.shape[0]))
    ukv = w["d_ukv"].reshape(D_KV_RANK, D_HEADS, D_NOPE + D_V)
    wuk = ukv[:, :, :D_NOPE].reshape(D_KV_RANK, D_HEADS * D_NOPE)
    wuv = ukv[:, :, D_NOPE:].reshape(D_KV_RANK, D_HEADS * D_V)
    gdk = w["d_k_norm"]
    return dict(
        gmix=_row(w["mix_norm"]), wn=wn, wt=wt,
        gaq=_lane_rep(pad_to(w["a_q_norm"], HEAD_PAD)),
        gak=_row(jnp.tile(w["a_k_norm"], A_HEADS)),
        wgu=jnp.pad(w["b_gate_up"], ((0, LANE - B_GATE_RANK), (0, 0))).astype(BF16),
        bgb=_row(w["b_gate_bias"]),
        gqa=_row(w["d_qa_norm"]),
        wuq=_pad_heads_rows(w["d_uq"].T, D_HEADS, D_QK).astype(BF16),
        gdq=_lane_rep(pad_to(w["d_q_norm"], HEAD_PAD)),
        gkva=_row(w["d_kva_norm"]), wuk=wuk.astype(BF16), wuvT=wuv.T.astype(BF16),
        gdk=_row(jnp.tile(gdk[:D_NOPE], D_HEADS)), gdkpe=_row(pad_to(gdk[D_NOPE:], LANE)),
        gbo=_row(jnp.tile(w["b_out_norm"], B_HEADS)),
        cw=jnp.pad(w["c_dw_w"][:, 0, :], ((0, CONV_HIST - C_KERNEL), (0, 0))).astype(F32),
        cb=_row(w["c_dw_b"]), cg=_row(w["c_norm"]),
    )


def _shared_tables(seq):
    hid = np.arange(GROUP_WIDTH) // 64
    bd = jnp.asarray(hid[:, None] == hid[None, :], dtype=BF16)
    half = D_ROPE // 2
    freqs = ROPE_THETA ** (-jnp.arange(half, dtype=F32) / half)
    ang = jnp.arange(seq).astype(F32)[:, None] * freqs[None, :]
    cos, sin = jnp.cos(ang), jnp.sin(ang)
    zeros = jnp.zeros((seq, LANE - D_ROPE), F32)
    cpe = jnp.concatenate([cos, cos, zeros], axis=1)
    spe = jnp.concatenate([-sin, sin, zeros], axis=1)
    return dict(bd=bd, cosT=cos.T, sinT=sin.T, cpe=cpe, spe=spe)


def _bias_tiles(rel_bias, blk):
    assert REL_MAX_DIST <= blk + 1
    kk = jnp.arange(blk)[:, None]
    qq = jnp.arange(blk)[None, :]
    rb = rel_bias.astype(F32).T
    m = 2 * blk + 1
    onehot = _t5_bucket(jnp.arange(m))[None, :, None] == jnp.arange(REL_BUCKETS)
    by_dist = jnp.sum(jnp.where(onehot, rb[:, None, :], 0.0), axis=-1) * LOG2E
    toe = jnp.tile(by_dist, (1, blk))[:, :blk * (m - 1)].reshape(-1, blk, m - 1)
    d0 = jnp.where(kk <= qq, toe[:, :, :blk], NEG)
    d1 = toe[:, :, blk:2 * blk]
    far = jnp.broadcast_to(by_dist[:, m - 1][:, None, None], d1.shape)
    return jnp.stack([d0, d1, far])


def _key_bounds(knorm, bias_max):
    km = jnp.sqrt(jnp.max(knorm[..., 0], axis=1))
    kb_a = jnp.concatenate([km[:, :A_HEADS].reshape(-1), bias_max.reshape(1)])
    return kb_a, km[:, A_HEADS:].reshape(-1)


def kernel(x, ffn1_norm, ffn1_gate, ffn1_up, ffn1_down, mix_norm, w_in, a_q_norm, a_k_norm, rel_bias,
           b_gate_up, b_gate_bias, b_out_norm, c_dw_w, c_dw_b, c_norm, d_qa_norm, d_uq, d_kva_norm,
           d_ukv, d_q_norm, d_k_norm, w_out, ffn2_norm, ffn2_gate, ffn2_up, ffn2_down):
    w = dict(mix_norm=mix_norm, w_in=w_in, a_q_norm=a_q_norm, a_k_norm=a_k_norm, b_gate_up=b_gate_up,
             b_gate_bias=b_gate_bias, b_out_norm=b_out_norm, c_dw_w=c_dw_w, c_dw_b=c_dw_b, c_norm=c_norm,
             d_qa_norm=d_qa_norm, d_uq=d_uq, d_kva_norm=d_kva_norm, d_ukv=d_ukv,
             d_q_norm=d_q_norm, d_k_norm=d_k_norm)
    bsz, seq, dm = x.shape
    depth = w_in.shape[0]
    blk = min(ATT_BLK, seq)
    bt = _bias_tiles(rel_bias, blk)
    bias_max = jnp.max(bt)
    p = {**jax.vmap(_one_layer_params)(w), **_shared_tables(seq)}
    stacked_row = lambda v: v[:, None, :].astype(F32)
    ffn1 = (stacked_row(ffn1_norm), ffn1_gate, ffn1_up, ffn1_down)
    ffn2 = (stacked_row(ffn2_norm), ffn2_gate, ffn2_up, ffn2_down)
    wo = w_out.astype(BF16)
    x2 = x.reshape(bsz * seq, dm)
    for l in range(depth):
        x2 = _ffn(x2, *ffn1, l)
        (aqT, akh, avT, iqT, aik, iwT, bq, bk, bv, bla, br, ch, dqT, dkh, dvT, knorm) = _mix_in(
            x2.reshape(bsz, seq, dm), p, l, blk=blk)
        kb_a, kb_d = _key_bounds(knorm, bias_max)
        y_a = _attention(aqT, akh, avT, kb_a, dsa=(iqT, iwT, aik, bt), blk=blk)
        y_b = _gla(bq, bk, bv, bla, br, p["gbo"], p["bd"], l)
        y_c = _conv(ch, p["cw"], p["cb"], p["cg"], l)
        y_d = _attention(dqT, dkh, dvT, kb_d, blk=blk)
        ys = [y.reshape(bsz * seq, GROUP_WIDTH) for y in (y_a, y_b, y_c, y_d)]
        x2 = _ffn(x2, *ffn2, l, mix=(ys, wo))
    return x2.reshape(bsz, seq, dm)
```

```python
import functools
import math

import jax
import jax.numpy as jnp
import numpy as np
from jax import lax
from jax.experimental import pallas as pl
from jax.experimental.pallas import tpu as pltpu

F32 = jnp.float32
BF16 = jnp.bfloat16
I32 = jnp.int32

EPS = 1e-6
GROUP_WIDTH = 256
A_HEADS, A_HEAD_DIM = 4, 64
IDX_HEADS, IDX_DIM = 8, 32
TOPK_MAX = 256
REL_BUCKETS, REL_MAX_DIST = 32, 128
B_HEADS, B_KEY_DIM, B_VAL_DIM, B_GATE_RANK = 4, 32, 64, 16
B_GATE_TAU = 16.0
B_CHUNK = 64
C_CHANNELS, C_KERNEL = 256, 31
D_HEADS, D_Q_RANK, D_KV_RANK, D_NOPE, D_ROPE, D_V = 4, 256, 128, 64, 32, 64
D_QK = D_NOPE + D_ROPE
ROPE_THETA = 10000.0

LANE = 128
SUBLANES = 8
HEAD_PAD = 128
ONES_ROWS = 16
ATT_BLK = 256
ATT_GROUP = 4
SCORE_GROUP = 4
BOUND_SLACK = 1.01
FIXED_REF_MAX = 40.0
INT_MIN = -2 ** 31
NEG = -1e30
LOG2E = math.log2(math.e)
VMEM_LIMIT = 56 * 1024 * 1024

N_AK, N_IK, N_BQ, N_BK, N_BV, N_BG, N_BR, N_CU, N_DCQ, N_DCKV, N_DKPE, N_TOT = (
    0, 256, 384, 512, 640, 896, 1024, 1280, 1792, 2048, 2176, 2304)
T_AQ, T_AV, T_IQ, T_IW, T_TOT = 0, 512, 768, 1024, 1040


def _dot(a, b):
    return jnp.dot(a, b, preferred_element_type=F32)


def _dot_nt(a, b):
    return lax.dot_general(a, b, (((1,), (1,)), ((), ())), preferred_element_type=F32)


def _dot_tn(a, b):
    return lax.dot_general(a, b, (((0,), (0,)), ((), ())), preferred_element_type=F32)


def _group_sum(x2, bd):
    hi = x2.astype(BF16)
    lo = (x2 - hi.astype(F32)).astype(BF16)
    return _dot(hi, bd) + _dot(lo, bd)


def _const_spec(shape, layer=None):
    nd = len(shape)
    if layer is None:
        return pl.BlockSpec(shape, lambda *_: (0,) * nd, pipeline_mode=pl.Buffered(1))
    return pl.BlockSpec((None,) + tuple(shape[1:]), lambda *_: (layer,) + (0,) * (nd - 1),
                        pipeline_mode=pl.Buffered(1))


def _cparams(sem):
    return pltpu.CompilerParams(dimension_semantics=sem, vmem_limit_bytes=VMEM_LIMIT)


def _ffn_body(has_mix, fc, *refs):
    if has_mix:
        x_ref, ya, yb, yc, yd, wo_ref, g_ref, wg_ref, wu_ref, wd_ref, o_ref, h_scr = refs
    else:
        x_ref, g_ref, wg_ref, wu_ref, wd_ref, o_ref, h_scr = refs
    x = x_ref[...]
    if has_mix:
        y = jnp.concatenate([ya[...], yb[...], yc[...], yd[...]], axis=-1)
        x = x + _dot(y, wo_ref[...])
    ms = jnp.mean(x * x, axis=-1, keepdims=True)
    xn = (x * lax.rsqrt(ms + EPS) * g_ref[...]).astype(BF16)
    d_ff = wg_ref.shape[1]
    for c in range(d_ff // fc):
        sl = slice(c * fc, (c + 1) * fc)
        gate = _dot(xn, wg_ref[:, sl].astype(BF16))
        up = _dot(xn, wu_ref[:, sl].astype(BF16))
        h_scr[:, sl] = (gate * jax.nn.sigmoid(gate) * up).astype(BF16)
    o_ref[...] = x + 0.5 * _dot(h_scr[...], wd_ref[...].astype(BF16))


def _ffn(x2, g, wg, wu, wd, layer, mix=None, tm=512, fc=256):
    m, d = x2.shape
    d_ff = wg.shape[2]
    tm = min(tm, m)
    row = lambda w: pl.BlockSpec((tm, w), lambda i: (i, 0))
    in_specs = [row(d)]
    args = [x2]
    if mix is not None:
        ys, wo = mix
        in_specs += [row(GROUP_WIDTH)] * 4 + [_const_spec(wo.shape, layer)]
        args += list(ys) + [wo]
    in_specs += [_const_spec(a.shape, layer) for a in (g, wg, wu, wd)]
    args += [g, wg, wu, wd]
    return pl.pallas_call(
        functools.partial(_ffn_body, mix is not None, fc),
        grid=(m // tm,),
        in_specs=in_specs,
        out_specs=row(d),
        out_shape=jax.ShapeDtypeStruct((m, d), F32),
        scratch_shapes=[pltpu.VMEM((tm, d_ff), BF16)],
        compiler_params=_cparams(("parallel",)),
        name="ffn_mix" if mix is not None else "ffn",
    )(*args)


def _mix_in_body(tm, blk,
                 x_ref, gmix_ref, wn_ref, wt_ref, bd_ref,
                 gaq_ref, gak_ref,
                 wgu_ref, bgb_ref,
                 gqa_ref, wuq_ref, gdq_ref, cosT_ref, sinT_ref,
                 gkva_ref, wuk_ref, wuvT_ref, gdk_ref, gdkpe_ref, cpe_ref, spe_ref,
                 aqT_ref, ak_ref, avT_ref, iqT_ref, ik_ref, iwT_ref,
                 bq_ref, bk_ref, bv_ref, bla_ref, br_ref,
                 ch_ref,
                 dqT_ref, dk_ref, dvT_ref, knorm_ref, qnorm_ref):
    nlt = tm // LANE
    x = x_ref[0]
    ms = jnp.mean(x * x, axis=-1, keepdims=True)
    xn = (x * lax.rsqrt(ms + EPS) * gmix_ref[...]).astype(BF16)
    bd = bd_ref[...]
    lane = lax.broadcasted_iota(I32, (tm, LANE), 1)

    def lanes(g):
        return jnp.tile(g, (1, nlt))

    z = _dot(xn, wn_ref[...])
    zt = _dot_nt(wt_ref[...], xn)

    def zs(off, width):
        return z[:, off:off + width]

    cq = zs(N_DCQ, D_Q_RANK)
    cq_ms = jnp.mean(cq * cq, axis=-1, keepdims=True)
    cqn = (cq * lax.rsqrt(cq_ms + EPS) * gqa_ref[...]).astype(BF16)
    ckv = zs(N_DCKV, D_KV_RANK)
    ckv_ms = jnp.mean(ckv * ckv, axis=-1, keepdims=True)
    ckvn = (ckv * lax.rsqrt(ckv_ms + EPS) * gkva_ref[...]).astype(BF16)
    dq = _dot_nt(wuq_ref[...], cqn).reshape(D_HEADS, HEAD_PAD, tm)
    kn = _dot(ckvn, wuk_ref[...])
    dv = _dot_nt(wuvT_ref[...], ckvn).astype(BF16)
    gate = _dot(zs(N_BG, LANE).astype(BF16), wgu_ref[...]) + bgb_ref[...]
    ak = zs(N_AK, GROUP_WIDTH)
    ak_ms = _group_sum(ak * ak, bd) * (1.0 / A_HEAD_DIM)
    kn_ss = _group_sum(kn * kn, bd)

    aq = zt[T_AQ:T_AQ + A_HEADS * HEAD_PAD].reshape(A_HEADS, HEAD_PAD, tm)
    aq_ms = jnp.sum(aq * aq, axis=1, keepdims=True) * (1.0 / A_HEAD_DIM)
    aq = aq * lax.rsqrt(aq_ms + EPS) * lanes(gaq_ref[...])[None] * (A_HEAD_DIM ** -0.5 * LOG2E)
    aqT_ref[0] = aq.reshape(A_HEADS * HEAD_PAD, tm).astype(BF16)
    qnorm_ref[0, 0:A_HEADS, :] = jnp.sqrt(jnp.sum(aq * aq, axis=1))
    def emit_key_norms(first_row, sq_norms):
        top = jnp.max(sq_norms, axis=0, keepdims=True)
        for h in range(4):
            knorm_ref[0, 0, first_row + h:first_row + h + 1, :] = jnp.broadcast_to(
                top[:, 64 * h:64 * h + 1], (1, LANE))

    def with_ones_rows(vt, heads, dim):
        ones = jnp.ones((ONES_ROWS, tm), BF16)
        return jnp.concatenate([r for h in range(heads) for r in (vt[dim * h:dim * (h + 1)], ones)], axis=0)

    av = with_ones_rows(zt[T_AV:T_AV + GROUP_WIDTH].astype(BF16), A_HEADS, A_HEAD_DIM)
    for c in range(tm // blk):
        avT_ref[0, c] = av[:, c * blk:(c + 1) * blk]
    iqT_ref[0] = zt[T_IQ:T_IQ + IDX_HEADS * IDX_DIM].astype(BF16)
    iwT_ref[0] = zt[T_IW:T_IW + IDX_HEADS] * ((IDX_HEADS ** -0.5) * (IDX_DIM ** -0.5))

    ak = ak * lax.rsqrt(ak_ms + EPS) * gak_ref[...]
    for h in range(A_HEADS):
        pair = ak[:, LANE * (h // 2):LANE * (h // 2) + LANE]
        if h % 2 == 1:
            pair = pltpu.roll(pair, 64, 1)
        kh = jnp.where(lane < A_HEAD_DIM, pair, 0.0).astype(BF16)
        ak_ref[0, h] = kh
    emit_key_norms(0, _dot((ak * ak).astype(BF16), bd))
    ik_ref[0] = zs(N_IK, LANE)[:, :IDX_DIM].astype(BF16)

    bq_ref[0] = zs(N_BQ, LANE) * (B_KEY_DIM ** -0.5)
    bk_ref[0] = zs(N_BK, LANE)
    bv_ref[0] = zs(N_BV, GROUP_WIDTH).astype(BF16)
    bla_ref[0] = (jnp.minimum(gate, 0.0) - jnp.log(1.0 + jnp.exp(-jnp.abs(gate)))) * (1.0 / B_GATE_TAU)
    br_ref[0] = zs(N_BR, GROUP_WIDTH)

    ca = zs(N_CU, C_CHANNELS)
    cg = zs(N_CU + C_CHANNELS, C_CHANNELS)
    ch_ref[0] = ca * jax.nn.sigmoid(cg)

    dq_ms = jnp.sum(dq * dq, axis=1, keepdims=True) * (1.0 / D_QK)
    dq = dq * lax.rsqrt(dq_ms + EPS) * lanes(gdq_ref[...])[None] * (D_QK ** -0.5 * LOG2E)
    half = D_ROPE // 2
    x1 = dq[:, D_NOPE:D_NOPE + half]
    x2 = dq[:, D_NOPE + half:D_QK]
    cs = cosT_ref[...][None]
    sn = sinT_ref[...][None]
    dq = jnp.concatenate([dq[:, :D_NOPE], x1 * cs - x2 * sn, x2 * cs + x1 * sn, dq[:, D_QK:]], axis=1)
    dqT_ref[0] = dq.reshape(D_HEADS * HEAD_PAD, tm).astype(BF16)
    qnorm_ref[0, A_HEADS:A_HEADS + D_HEADS, :] = jnp.sqrt(jnp.sum(dq * dq, axis=1))

    dv = with_ones_rows(dv, D_HEADS, D_V)
    for c in range(tm // blk):
        dvT_ref[0, c] = dv[:, c * blk:(c + 1) * blk]
    kpe = zs(N_DKPE, LANE)
    ss = kn_ss + jnp.sum(kpe * kpe, axis=-1, keepdims=True)
    rinv = lax.rsqrt(ss * (1.0 / D_QK) + EPS)
    kn = kn * rinv * gdk_ref[...]
    pe = kpe * gdkpe_ref[...]
    partner = jnp.where(lane < half, pltpu.roll(pe, LANE - half, 1), pltpu.roll(pe, half, 1))
    pe = pe * cpe_ref[...] + partner * spe_ref[...]
    pe = pltpu.roll(pe, D_NOPE, 1)
    for h in range(D_HEADS):
        pair = kn[:, LANE * (h // 2):LANE * (h // 2) + LANE]
        rpair = rinv[:, LANE * (h // 2):LANE * (h // 2) + LANE]
        if h % 2 == 1:
            pair = pltpu.roll(pair, 64, 1)
        else:
            rpair = pltpu.roll(rpair, 64, 1)
        kh = jnp.where(lane < D_NOPE, pair, pe * rpair).astype(BF16)
        dk_ref[0, h] = kh
    pe_sq = _dot((pe * pe).astype(BF16), jnp.ones((LANE, LANE), BF16))
    emit_key_norms(A_HEADS, _dot((kn * kn).astype(BF16), bd) + jnp.tile(pe_sq, (1, 2)) * (rinv * rinv))


def _mix_in(x3, p, layer, tm=512, blk=ATT_BLK):
    b, l, d = x3.shape
    tm = min(tm, l)
    grid = (b, l // tm)
    nck = l // blk
    tok = lambda w: pl.BlockSpec((1, tm, w), lambda bi, i: (bi, i, 0))
    tokT = lambda r: pl.BlockSpec((1, r, tm), lambda bi, i: (bi, 0, i))
    headk = pl.BlockSpec((1, 4, tm, HEAD_PAD), lambda bi, i: (bi, 0, i, 0))
    vrows = GROUP_WIDTH + 4 * ONES_ROWS
    chunkT = pl.BlockSpec((1, tm // blk, vrows, blk), lambda bi, i: (bi, i, 0, 0))
    postab = lambda r: pl.BlockSpec((r, tm), lambda bi, i: (0, i))
    posrow = pl.BlockSpec((tm, LANE), lambda bi, i: (i, 0))
    consts = [p["gmix"], p["wn"], p["wt"], p["bd"], p["gaq"], p["gak"], p["wgu"], p["bgb"],
              p["gqa"], p["wuq"], p["gdq"]]
    consts2 = [p["gkva"], p["wuk"], p["wuvT"], p["gdk"], p["gdkpe"]]
    lspec = lambda a: _const_spec(a.shape) if a is p["bd"] else _const_spec(a.shape, layer)
    in_specs = ([tok(d)] + [lspec(a) for a in consts]
                + [postab(D_ROPE // 2), postab(D_ROPE // 2)]
                + [lspec(a) for a in consts2] + [posrow, posrow])
    args = [x3] + consts + [p["cosT"], p["sinT"]] + consts2 + [p["cpe"], p["spe"]]
    sd = jax.ShapeDtypeStruct
    out_shape = [
        sd((b, A_HEADS * HEAD_PAD, l), BF16), sd((b, A_HEADS, l, HEAD_PAD), BF16),
        sd((b, nck, vrows, blk), BF16), sd((b, IDX_HEADS * IDX_DIM, l), BF16),
        sd((b, l, IDX_DIM), BF16), sd((b, IDX_HEADS, l), F32),
        sd((b, l, LANE), F32), sd((b, l, LANE), F32), sd((b, l, GROUP_WIDTH), BF16),
        sd((b, l, LANE), F32), sd((b, l, GROUP_WIDTH), F32),
        sd((b, l, C_CHANNELS), F32),
        sd((b, D_HEADS * HEAD_PAD, l), BF16), sd((b, D_HEADS, l, HEAD_PAD), BF16),
        sd((b, nck, vrows, blk), BF16),
        sd((b, l // tm, A_HEADS + D_HEADS, LANE), F32),
        sd((b, A_HEADS + D_HEADS, l), F32),
    ]
    out_specs = [
        tokT(A_HEADS * HEAD_PAD), headk, chunkT, tokT(IDX_HEADS * IDX_DIM),
        tok(IDX_DIM), tokT(IDX_HEADS),
        tok(LANE), tok(LANE), tok(GROUP_WIDTH), tok(LANE), tok(GROUP_WIDTH),
        tok(C_CHANNELS),
        tokT(D_HEADS * HEAD_PAD), headk, chunkT,
        pl.BlockSpec((1, 1, A_HEADS + D_HEADS, LANE), lambda bi, i: (bi, i, 0, 0)),
        tokT(A_HEADS + D_HEADS),
    ]
    return pl.pallas_call(
        functools.partial(_mix_in_body, tm, blk),
        grid=grid, in_specs=in_specs, out_specs=out_specs, out_shape=out_shape,
        compiler_params=_cparams(("parallel", "parallel")),
        name="mix_in",
    )(*args)


def _attn_body(nh, dv, blk, topk, is_dsa, qn_row, *refs):
    if is_dsa:
        (kb_ref, qn_ref, qT_ref, k_ref, vT_ref, iqT_ref, wT_ref, ik_ref, bt_ref,
         o_ref, m_scr, acc_scr, key_scr, run_scr, plane_scr, active_scr) = refs
    else:
        kb_ref, qn_ref, qT_ref, k_ref, vT_ref, o_ref, m_scr, acc_scr = refs
    vr = dv + ONES_ROWS
    i = pl.program_id(1)
    t = blk
    row = lax.broadcasted_iota(I32, (t, t), 0)
    col = lax.broadcasted_iota(I32, (t, t), 1)
    causal_pen = jnp.where(row <= col, 0.0, NEG)

    m_scr[...] = jnp.full(m_scr.shape, NEG, F32)
    acc_scr[...] = jnp.zeros(acc_scr.shape, F32)

    if is_dsa:
        @pl.when(i == 0)
        def _():
            plane_scr[...] = jnp.zeros(plane_scr.shape, I32)

        def score_chunk(j, diag):
            r0 = pl.multiple_of(j * t, t)
            ikc = ik_ref[0, pl.ds(r0, t), :]
            s = jnp.zeros((t, t), F32)
            for h in range(IDX_HEADS):
                d = _dot(ikc, iqT_ref[0, IDX_DIM * h:IDX_DIM * (h + 1), :])
                s = s + jnp.maximum(d, 0.0) * wT_ref[0, h:h + 1, :]
            bits = lax.bitcast_convert_type(s, I32)
            key = jnp.where(bits < 0, bits ^ 0x7FFFFFFF, bits)
            if diag:
                key = jnp.where(row <= col, key, INT_MIN)
            key_scr[pl.ds(r0, t), :] = key
            w = [key[SUBLANES * r:SUBLANES * (r + 1), :] for r in range(32)]
            step, mask = 16, 0x0000FFFF
            while step:
                for lo in range(32):
                    if lo & step == 0:
                        hi = lo + step
                        swap = (w[lo] ^ jnp.right_shift(w[hi], step)) & mask
                        w[lo] = w[lo] ^ swap
                        w[hi] = w[hi] ^ jnp.left_shift(swap, step)
                step //= 2
                mask ^= (mask << step) & 0xFFFFFFFF
            c0 = pl.multiple_of(j * SUBLANES, SUBLANES)
            w[0] = ~w[0]
            for p in range(32):
                plane_scr[p, pl.ds(c0, SUBLANES), :] = w[p]

        def score_group(u, carry):
            for c in range(SCORE_GROUP):
                score_chunk(SCORE_GROUP * u + c, False)
            return carry

        lax.fori_loop(0, i // SCORE_GROUP, score_group, 0)
        for rem in range(SCORE_GROUP):
            @pl.when(i % SCORE_GROUP == rem)
            def _():
                for c in range(rem):
                    score_chunk(i - rem + c, False)
                score_chunk(i, True)

        def kth_largest(nrow):
            rows = pl.ds(0, nrow)
            in_range = lax.broadcasted_iota(I32, (nrow, t), 0) < (i + 1) * SUBLANES

            def col_count(words):
                pc = lax.population_count(words).reshape(nrow // SUBLANES, SUBLANES, t)
                return jnp.sum(jnp.sum(pc, axis=0), axis=0, keepdims=True)

            def decide(plane, active, n_gt, ans_u):
                ones = col_count(active & plane_scr[plane, rows, :])
                take = n_gt + ones >= topk
                bit = lax.shift_right_logical(jnp.int32(INT_MIN), jnp.int32(plane))
                return (jnp.where(take, 0, -1), n_gt + jnp.where(take, 0, ones),
                        ans_u | jnp.where(take, bit, 0))

            active_scr[rows, :] = jnp.where(in_range, -1, 0)
            state = decide(0, active_scr[rows, :], jnp.zeros((1, t), I32), jnp.zeros((1, t), I32))

            def plane_body(plane, state):
                flip, n_gt, ans_u = state
                active = active_scr[rows, :] & (plane_scr[plane - 1, rows, :] ^ flip)
                active_scr[rows, :] = active
                return decide(plane, active, n_gt, ans_u)

            _, n_gt, ans_u = lax.fori_loop(1, 32, plane_body, state)
            return n_gt, ans_u

        nrow_all = plane_scr.shape[1]
        nrow_half = (nrow_all // SUBLANES // 2) * SUBLANES
        if nrow_half:
            n_gt, ans_u = lax.cond((i + 1) * SUBLANES <= nrow_half,
                                   functools.partial(kth_largest, nrow_half),
                                   functools.partial(kth_largest, nrow_all))
        else:
            n_gt, ans_u = kth_largest(nrow_all)
        ans = ans_u ^ INT_MIN
        need = (topk - n_gt).astype(F32)
        run_scr[...] = jnp.zeros(run_scr.shape, F32)
        stri = jnp.where(col < row, 1.0, 0.0).astype(BF16)

    def logits(h, r0):
        return _dot(k_ref[0, h, pl.ds(r0, t), :], qT_ref[0, HEAD_PAD * h:HEAD_PAD * (h + 1), :])

    def selection_pens(r0s):
        pens = []
        if is_dsa:
            for r0 in r0s:
                kc = key_scr[pl.ds(r0, t), :]
                eq = kc == ans
                eqf = jnp.where(eq, 1.0, 0.0)
                run = run_scr[0:1, :]
                rank = _dot(stri, eqf.astype(BF16)) + run
                run_scr[0:1, :] = run + jnp.sum(eqf, axis=0, keepdims=True)
                pens.append(jnp.where(kc > ans, 0.0, jnp.where(eq, jnp.where(rank < need, 0.0, NEG), NEG)))
        return pens

    b_idx = pl.program_id(0)
    bounds = []
    for h in range(nh):
        bound = qn_ref[0, qn_row + h:qn_row + h + 1, :] * (kb_ref[b_idx * nh + h] * BOUND_SLACK)
        if is_dsa:
            bound = bound + kb_ref[kb_ref.shape[0] - 1]
        bounds.append(bound)
    bound_max = jnp.max(functools.reduce(jnp.maximum, bounds))
    fixed_ref_ok = bound_max <= FIXED_REF_MAX

    def fixed_ref_chunks(js, kinds):
        n = len(js)
        r0s = [pl.multiple_of(j * t, t) for j in js]
        lgs = [[logits(h, r0s[c]) for c in range(n)] for h in range(nh)]
        pens = selection_pens(r0s)
        ps = []
        for h in range(nh):
            row_ps = []
            if is_dsa:
                far_ref = bounds[h] - bt_ref[2, h, 0:1, :]
            for c in range(n):
                x, ref = lgs[h][c], bounds[h]
                if is_dsa and kinds[c] == "far":
                    x, ref = x + pens[c], far_ref
                elif is_dsa:
                    x = x + (pens[c] + bt_ref[0 if kinds[c] == "diag" else 1, h])
                elif kinds[c] == "diag":
                    x = x + causal_pen
                row_ps.append(jnp.exp2(x - ref).astype(BF16))
            ps.append(row_ps[0] if n == 1 else jnp.concatenate(row_ps, axis=0))
        for h in range(nh):
            vs = slice(vr * h, vr * (h + 1))
            vt = [vT_ref[0, js[c], vs, :] for c in range(n)]
            acc_scr[vs, :] += _dot(vt[0] if n == 1 else jnp.concatenate(vt, axis=1), ps[h])

    def super_chunk(js, kinds):
        n = len(js)
        r0s = [pl.multiple_of(j * t, t) for j in js]
        lgs = [[logits(h, r0s[c]) for c in range(n)] for h in range(nh)]
        pens = selection_pens(r0s)
        ps, alphas = [], []
        for h in range(nh):
            xs = []
            for c, j in enumerate(js):
                diag = kinds[c] == "diag"
                lg = lgs[h][c]
                if is_dsa:
                    tile = 0 if diag else jnp.minimum(i - j, 2)
                    lg = lg + (pens[c] + bt_ref[tile, h])
                elif diag:
                    lg = lg + causal_pen
                xs.append(lg)
            m_old = m_scr[h, 0:1, :]
            m_new = m_old
            for x in xs:
                m_new = jnp.maximum(m_new, jnp.max(x, axis=0, keepdims=True))
            alpha = jnp.exp2(m_old - m_new)
            m_scr[h, 0:1, :] = m_new
            ps.append([jnp.exp2((x - m_new).astype(BF16)) for x in xs])
            alphas.append(alpha)
        for h in range(nh):
            vs = slice(vr * h, vr * (h + 1))
            pv = _dot(vT_ref[0, js[0], vs, :], ps[h][0])
            for c in range(1, n):
                pv = pv + _dot(vT_ref[0, js[c], vs, :], ps[h][c])
            acc_scr[vs, :] = alphas[h] * acc_scr[vs, :] + pv

    tail = ["near", "diag"] if is_dsa else ["diag"]
    nfar = jnp.maximum(i + 1 - len(tail), 0)

    def run_chunks(step):
        def group_body(u, carry):
            step([ATT_GROUP * u + c for c in range(ATT_GROUP)], ["far"] * ATT_GROUP)
            return carry

        lax.fori_loop(0, nfar // ATT_GROUP, group_body, 0)
        for rem in range(ATT_GROUP):
            @pl.when(jnp.logical_and(i + 1 >= len(tail), nfar % ATT_GROUP == rem))
            def _():
                first = i + 1 - len(tail) - rem
                step([first + c for c in range(rem + len(tail))], ["far"] * rem + tail)

        if is_dsa:
            @pl.when(i == 0)
            def _():
                step([i], ["diag"])

    pl.when(fixed_ref_ok)(functools.partial(run_chunks, fixed_ref_chunks))
    pl.when(jnp.logical_not(fixed_ref_ok))(functools.partial(run_chunks, super_chunk))

    outs = []
    for h in range(nh):
        outs.append(acc_scr[vr * h:vr * h + dv, :] / acc_scr[vr * h + dv:vr * h + dv + 1, :])
    o_ref[0] = jnp.transpose(jnp.concatenate(outs, axis=0)).astype(o_ref.dtype)


def _attention(qT, k, vT, kbound, qnorm, qn_row, dsa=None, blk=ATT_BLK):
    b, nh, l, _ = k.shape
    vr = vT.shape[2] // nh
    dv = vr - ONES_ROWS
    grid = (b, l // blk)
    qspec = lambda r: pl.BlockSpec((1, r, blk), lambda bi, i: (bi, 0, i))
    kspec = pl.BlockSpec((1, nh, l, HEAD_PAD), lambda bi, i: (bi, 0, 0, 0))
    vspec = pl.BlockSpec((1, l // blk, nh * vr, blk), lambda bi, i: (bi, 0, 0, 0))
    in_specs = [pl.BlockSpec(memory_space=pltpu.SMEM), qspec(qnorm.shape[1]), qspec(nh * HEAD_PAD), kspec, vspec]
    args = [kbound, qnorm, qT, k, vT]
    scratch = [pltpu.VMEM((nh, 8, blk), F32), pltpu.VMEM((nh * vr, blk), F32)]
    topk = 0
    if dsa is not None:
        iqT, wT, ik, bt = dsa
        topk = min(TOPK_MAX, l // 4)
        in_specs += [qspec(IDX_HEADS * IDX_DIM), qspec(IDX_HEADS),
                     pl.BlockSpec((1, l, IDX_DIM), lambda bi, i: (bi, 0, 0)),
                     _const_spec(bt.shape)]
        args += [iqT, wT, ik, bt]
        assert blk == 32 * SUBLANES
        nrow = (l // blk) * SUBLANES
        scratch += [pltpu.VMEM((l, blk), I32), pltpu.VMEM((8, blk), F32),
                    pltpu.VMEM((32, nrow, blk), I32), pltpu.VMEM((nrow, blk), I32)]
    return pl.pallas_call(
        functools.partial(_attn_body, nh, dv, blk, topk, dsa is not None, qn_row),
        grid=grid, in_specs=in_specs,
        out_specs=pl.BlockSpec((1, blk, nh * dv), lambda bi, i: (bi, i, 0)),
        out_shape=jax.ShapeDtypeStruct((b, l, nh * dv), BF16),
        scratch_shapes=scratch,
        compiler_params=_cparams(("parallel", "arbitrary")),
        name="dsa_attn" if dsa is not None else "mla_attn",
    )(*args)


def _gla_body(tg, q_ref, k_ref, v_ref, la_ref, r_ref, go_ref, bd_ref, o_ref, st_scr, o_scr):
    @pl.when(pl.program_id(1) == 0)
    def _():
        st_scr[...] = jnp.zeros(st_scr.shape, F32)

    cs = B_CHUNK
    la = la_ref[0]
    rl = lax.broadcasted_iota(I32, (tg, LANE), 0) & (cs - 1)
    b = la
    s = 1
    while s < cs:
        b = b + jnp.where(rl >= s, pltpu.roll(b, s, 0), 0.0)
        s *= 2
    q = q_ref[0]
    k = k_ref[0]
    qb = q * jnp.exp(b)
    tri = lax.broadcasted_iota(I32, (cs, cs), 1) <= lax.broadcasted_iota(I32, (cs, cs), 0)
    khead = lax.broadcasted_iota(I32, (cs, LANE), 1) // B_KEY_DIM
    vhead = lax.broadcasted_iota(I32, (cs, GROUP_WIDTH), 1) // B_VAL_DIM
    same_head = (lax.broadcasted_iota(I32, (GROUP_WIDTH, LANE), 0) // B_VAL_DIM
                 == lax.broadcasted_iota(I32, (GROUP_WIDTH, LANE), 1) // B_KEY_DIM)
    for c in range(tg // cs):
        sl = slice(c * cs, (c + 1) * cs)
        bc = b[sl]
        mid = bc[cs // 2:cs // 2 + 1]
        last = bc[cs - 1:cs]
        qe = q[sl] * jnp.exp(bc - mid)
        ke = (k[sl] * jnp.exp(mid - bc)).astype(BF16)
        kd = (k[sl] * jnp.exp(last - bc)).astype(BF16)
        qbc = qb[sl].astype(BF16)
        vc = v_ref[0, sl, :]
        st = st_scr[...]
        qe4 = jnp.concatenate([jnp.where(khead == h, qe, 0.0) for h in range(B_HEADS)], axis=0)
        a_all = _dot_nt(qe4.astype(BF16), ke)
        o = _dot_nt(qbc, st.astype(BF16))
        for h in range(B_HEADS):
            a = jnp.where(tri, a_all[cs * h:cs * (h + 1)], 0.0).astype(BF16)
            o = o + _dot(a, jnp.where(vhead == h, vc, jnp.zeros_like(vc)))
        o_scr[sl, :] = o
        st_scr[...] = st * jnp.exp(last) + jnp.where(same_head, _dot_tn(vc, kd), 0.0)
    o = o_scr[...]
    ms = _group_sum(o * o, bd_ref[...]) * (1.0 / B_VAL_DIM)
    r = r_ref[0]
    o_ref[0] = (o * lax.rsqrt(ms + EPS) * go_ref[...] * (r * jax.nn.sigmoid(r))).astype(o_ref.dtype)


def _gla(bq, bk, bv, bla, br, go, bd, layer, tg=1024):
    b, l, _ = bq.shape
    tg = min(tg, l)
    tok = lambda w: pl.BlockSpec((1, tg, w), lambda bi, i: (bi, i, 0))
    return pl.pallas_call(
        functools.partial(_gla_body, tg),
        grid=(b, l // tg),
        in_specs=[tok(LANE), tok(LANE), tok(GROUP_WIDTH), tok(LANE), tok(GROUP_WIDTH),
                  _const_spec(go.shape, layer), _const_spec(bd.shape)],
        out_specs=tok(GROUP_WIDTH),
        out_shape=jax.ShapeDtypeStruct((b, l, GROUP_WIDTH), BF16),
        scratch_shapes=[pltpu.VMEM((B_HEADS * B_VAL_DIM, B_HEADS * B_KEY_DIM), F32),
                        pltpu.VMEM((tg, GROUP_WIDTH), F32)],
        compiler_params=_cparams(("parallel", "arbitrary")),
        name="gla",
    )(bq, bk, bv, bla, br, go, bd)


CONV_HIST = 32


def _conv_body(tc, h_ref, w_ref, b_ref, g_ref, o_ref, buf):
    @pl.when(pl.program_id(1) == 0)
    def _():
        buf[0:CONV_HIST, :] = jnp.zeros((CONV_HIST, C_CHANNELS), F32)

    @pl.when(pl.program_id(1) > 0)
    def _():
        buf[0:CONV_HIST, :] = buf[tc:tc + CONV_HIST, :]

    buf[CONV_HIST:CONV_HIST + tc, :] = h_ref[0]
    acc = jnp.zeros((tc, C_CHANNELS), F32) + b_ref[...]
    base = CONV_HIST - (C_KERNEL - 1)
    hb = buf[...]
    rows = tc + CONV_HIST
    for r in range(SUBLANES):
        shifted = hb if r == 0 else pltpu.roll(hb, rows - r, 0)
        for j in range(C_KERNEL):
            if (base + j) % SUBLANES == r:
                a0 = base + j - r
                acc = acc + shifted[a0:a0 + tc, :] * w_ref[j:j + 1, :]
    ms = jnp.mean(acc * acc, axis=-1, keepdims=True)
    y = acc * lax.rsqrt(ms + EPS) * g_ref[...]
    o_ref[0] = (y * jax.nn.sigmoid(y)).astype(o_ref.dtype)


def _conv(ch, w, bias, g, layer, tc=512):
    b, l, c = ch.shape
    tc = min(tc, l)
    tok = pl.BlockSpec((1, tc, c), lambda bi, i: (bi, i, 0))
    return pl.pallas_call(
        functools.partial(_conv_body, tc),
        grid=(b, l // tc),
        in_specs=[tok] + [_const_spec(a.shape, layer) for a in (w, bias, g)],
        out_specs=tok,
        out_shape=jax.ShapeDtypeStruct((b, l, c), BF16),
        scratch_shapes=[pltpu.VMEM((tc + CONV_HIST, c), F32)],
        compiler_params=_cparams(("parallel", "arbitrary")),
        name="conv",
    )(ch, w, bias, g)


def _t5_bucket(dist):
    max_exact = REL_BUCKETS // 2
    d = jnp.maximum(dist, 0)
    df = jnp.maximum(d, 1).astype(F32)
    large = max_exact + (jnp.log(df / max_exact) / math.log(REL_MAX_DIST / max_exact)
                         * (REL_BUCKETS - max_exact)).astype(I32)
    large = jnp.minimum(large, REL_BUCKETS - 1)
    return jnp.where(d < max_exact, d, large)


def _pad_cols(w, width):
    return jnp.pad(w, ((0, 0), (0, width - w.shape[1])))


def _lane_rep(v):
    return jnp.broadcast_to(v[:, None], (v.shape[0], LANE))


def _pad_heads_rows(w, heads, dim):
    w = w.reshape(heads, dim, w.shape[1])
    return jnp.pad(w, ((0, 0), (0, HEAD_PAD - dim), (0, 0))).reshape(heads * HEAD_PAD, -1)


def _split_w_in(w_in):
    widths = (256, 256, 256, 256, 32, 8, 128, 128, 256, 16, 256, 512, 256, 128, 32)
    offs = np.cumsum((0,) + widths)
    return [w_in[:, offs[n]:offs[n + 1]] for n in range(len(widths))]


def _row(v):
    return v[None, :].astype(F32)


def _one_layer_params(w):
    (aq, ak, av, iq, ik, iw, bq, bk, bv, bg, br, cu, dcq, dckv, dkpe) = _split_w_in(w["w_in"])
    wn = jnp.concatenate([ak, _pad_cols(ik, LANE), bq, bk, bv, _pad_cols(bg, LANE), br, cu, dcq, dckv,
                          _pad_cols(dkpe, LANE)], axis=1).astype(BF16)
    wt = jnp.concatenate([_pad_heads_rows(aq.T, A_HEADS, A_HEAD_DIM), av.T, iq.T,
                          jnp.pad(iw.T, ((0, T_TOT - T_IW - IDX_HEADS), (0, 0)))], axis=0).astype(BF16)
    pad_to = lambda v, n: jnp.pad(v, (0, n - v.shape[0]))
    ukv = w["d_ukv"].reshape(D_KV_RANK, D_HEADS, D_NOPE + D_V)
    wuk = ukv[:, :, :D_NOPE].reshape(D_KV_RANK, D_HEADS * D_NOPE)
    wuv = ukv[:, :, D_NOPE:].reshape(D_KV_RANK, D_HEADS * D_V)
    gdk = w["d_k_norm"]
    return dict(
        gmix=_row(w["mix_norm"]), wn=wn, wt=wt,
        gaq=_lane_rep(pad_to(w["a_q_norm"], HEAD_PAD)),
        gak=_row(jnp.tile(w["a_k_norm"], A_HEADS)),
        wgu=jnp.pad(w["b_gate_up"], ((0, LANE - B_GATE_RANK), (0, 0))).astype(BF16),
        bgb=_row(w["b_gate_bias"]),
        gqa=_row(w["d_qa_norm"]),
        wuq=_pad_heads_rows(w["d_uq"].T, D_HEADS, D_QK).astype(BF16),
        gdq=_lane_rep(pad_to(w["d_q_norm"], HEAD_PAD)),
        gkva=_row(w["d_kva_norm"]), wuk=wuk.astype(BF16), wuvT=wuv.T.astype(BF16),
        gdk=_row(jnp.tile(gdk[:D_NOPE], D_HEADS)), gdkpe=_row(pad_to(gdk[D_NOPE:], LANE)),
        gbo=_row(jnp.tile(w["b_out_norm"], B_HEADS)),
        cw=jnp.pad(w["c_dw_w"][:, 0, :], ((0, CONV_HIST - C_KERNEL), (0, 0))).astype(F32),
        cb=_row(w["c_dw_b"]), cg=_row(w["c_norm"]),
    )


def _shared_tables(seq):
    hid = np.arange(GROUP_WIDTH) // 64
    bd = jnp.asarray(hid[:, None] == hid[None, :], dtype=BF16)
    half = D_ROPE // 2
    freqs = ROPE_THETA ** (-jnp.arange(half, dtype=F32) / half)
    ang = jnp.arange(seq).astype(F32)[:, None] * freqs[None, :]
    cos, sin = jnp.cos(ang), jnp.sin(ang)
    zeros = jnp.zeros((seq, LANE - D_ROPE), F32)
    cpe = jnp.concatenate([cos, cos, zeros], axis=1)
    spe = jnp.concatenate([-sin, sin, zeros], axis=1)
    return dict(bd=bd, cosT=cos.T, sinT=sin.T, cpe=cpe, spe=spe)


def _bias_tiles(rel_bias, blk):
    assert REL_MAX_DIST <= blk + 1
    kk = jnp.arange(blk)[:, None]
    qq = jnp.arange(blk)[None, :]
    rb = rel_bias.astype(F32).T
    m = 2 * blk + 1
    onehot = _t5_bucket(jnp.arange(m))[None, :, None] == jnp.arange(REL_BUCKETS)
    by_dist = jnp.sum(jnp.where(onehot, rb[:, None, :], 0.0), axis=-1) * LOG2E
    toe = jnp.tile(by_dist, (1, blk))[:, :blk * (m - 1)].reshape(-1, blk, m - 1)
    d0 = jnp.where(kk <= qq, toe[:, :, :blk], NEG)
    d1 = toe[:, :, blk:2 * blk]
    far = jnp.broadcast_to(by_dist[:, m - 1][:, None, None], d1.shape)
    return jnp.stack([d0, d1, far])


def _key_bounds(knorm, bias_max):
    km = jnp.sqrt(jnp.max(knorm[..., 0], axis=1))
    kb_a = jnp.concatenate([km[:, :A_HEADS].reshape(-1), bias_max.reshape(1)])
    return kb_a, km[:, A_HEADS:].reshape(-1)


def kernel(x, ffn1_norm, ffn1_gate, ffn1_up, ffn1_down, mix_norm, w_in, a_q_norm, a_k_norm, rel_bias,
           b_gate_up, b_gate_bias, b_out_norm, c_dw_w, c_dw_b, c_norm, d_qa_norm, d_uq, d_kva_norm,
           d_ukv, d_q_norm, d_k_norm, w_out, ffn2_norm, ffn2_gate, ffn2_up, ffn2_down):
    w = dict(mix_norm=mix_norm, w_in=w_in, a_q_norm=a_q_norm, a_k_norm=a_k_norm, b_gate_up=b_gate_up,
             b_gate_bias=b_gate_bias, b_out_norm=b_out_norm, c_dw_w=c_dw_w, c_dw_b=c_dw_b, c_norm=c_norm,
             d_qa_norm=d_qa_norm, d_uq=d_uq, d_kva_norm=d_kva_norm, d_ukv=d_ukv,
             d_q_norm=d_q_norm, d_k_norm=d_k_norm)
    bsz, seq, dm = x.shape
    depth = w_in.shape[0]
    blk = min(ATT_BLK, seq)
    bt = _bias_tiles(rel_bias, blk)
    bias_max = jnp.max(bt)
    p = {**jax.vmap(_one_layer_params)(w), **_shared_tables(seq)}
    stacked_row = lambda v: v[:, None, :].astype(F32)
    ffn1 = (stacked_row(ffn1_norm), ffn1_gate, ffn1_up, ffn1_down)
    ffn2 = (stacked_row(ffn2_norm), ffn2_gate, ffn2_up, ffn2_down)
    wo = w_out.astype(BF16)
    x2 = x.reshape(bsz * seq, dm)
    for l in range(depth):
        x2 = _ffn(x2, *ffn1, l)
        (aqT, akh, avT, iqT, aik, iwT, bq, bk, bv, bla, br, ch, dqT, dkh, dvT, knorm, qnorm) = _mix_in(
            x2.reshape(bsz, seq, dm), p, l, blk=blk)
        kb_a, kb_d = _key_bounds(knorm, bias_max)
        y_a = _attention(aqT, akh, avT, kb_a, qnorm, 0, dsa=(iqT, iwT, aik, bt), blk=blk)
        y_b = _gla(bq, bk, bv, bla, br, p["gbo"], p["bd"], l)
        y_c = _conv(ch, p["cw"], p["cb"], p["cg"], l)
        y_d = _attention(dqT, dkh, dvT, kb_d, qnorm, A_HEADS, blk=blk)
        ys = [y.reshape(bsz * seq, GROUP_WIDTH) for y in (y_a, y_b, y_c, y_d)]
        x2 = _ffn(x2, *ffn2, l, mix=(ys, wo))
    return x2.reshape(bsz, seq, dm)
```

```python
import functools
import math

import jax
import jax.numpy as jnp
import numpy as np
from jax import lax
from jax.experimental import pallas as pl
from jax.experimental.pallas import tpu as pltpu

F32 = jnp.float32
BF16 = jnp.bfloat16
I32 = jnp.int32

EPS = 1e-6
GROUP_WIDTH = 256
A_HEADS, A_HEAD_DIM = 4, 64
IDX_HEADS, IDX_DIM = 8, 32
TOPK_MAX = 256
REL_BUCKETS, REL_MAX_DIST = 32, 128
B_HEADS, B_KEY_DIM, B_VAL_DIM, B_GATE_RANK = 4, 32, 64, 16
B_GATE_TAU = 16.0
B_CHUNK = 64
C_CHANNELS, C_KERNEL = 256, 31
D_HEADS, D_Q_RANK, D_KV_RANK, D_NOPE, D_ROPE, D_V = 4, 256, 128, 64, 32, 64
D_QK = D_NOPE + D_ROPE
ROPE_THETA = 10000.0

LANE = 128
SUBLANES = 8
HEAD_PAD = 128
ONES_ROWS = 16
ATT_BLK = 256
ATT_GROUP = 3
SCORE_GROUP = 4
BOUND_SLACK = 1.01
FIXED_REF_MAX = 40.0
INT_MIN = -2 ** 31
NEG = -1e30
LOG2E = math.log2(math.e)
VMEM_LIMIT = 56 * 1024 * 1024

N_AK, N_IK, N_BQ, N_BK, N_BV, N_BG, N_BR, N_CU, N_DCQ, N_DCKV, N_DKPE, N_TOT = (
    0, 256, 384, 512, 640, 896, 1024, 1280, 1792, 2048, 2176, 2304)
T_AQ, T_AV, T_IQ, T_IW, T_TOT = 0, 512, 768, 1024, 1040


def _dot(a, b):
    return jnp.dot(a, b, preferred_element_type=F32)


def _dot_nt(a, b):
    return lax.dot_general(a, b, (((1,), (1,)), ((), ())), preferred_element_type=F32)


def _dot_tn(a, b):
    return lax.dot_general(a, b, (((0,), (0,)), ((), ())), preferred_element_type=F32)


def _group_sum(x2, bd):
    hi = x2.astype(BF16)
    lo = (x2 - hi.astype(F32)).astype(BF16)
    return _dot(hi, bd) + _dot(lo, bd)


def _const_spec(shape, layer=None):
    nd = len(shape)
    if layer is None:
        return pl.BlockSpec(shape, lambda *_: (0,) * nd, pipeline_mode=pl.Buffered(1))
    return pl.BlockSpec((None,) + tuple(shape[1:]), lambda *_: (layer,) + (0,) * (nd - 1),
                        pipeline_mode=pl.Buffered(1))


def _cparams(sem):
    return pltpu.CompilerParams(dimension_semantics=sem, vmem_limit_bytes=VMEM_LIMIT)


def _ffn_body(has_mix, fc, *refs):
    if has_mix:
        x_ref, ya, yb, yc, yd, wo_ref, g_ref, wg_ref, wu_ref, wd_ref, o_ref, h_scr = refs
    else:
        x_ref, g_ref, wg_ref, wu_ref, wd_ref, o_ref, h_scr = refs
    x = x_ref[...]
    if has_mix:
        y = jnp.concatenate([ya[...], yb[...], yc[...], yd[...]], axis=-1)
        x = x + _dot(y, wo_ref[...])
    ms = jnp.mean(x * x, axis=-1, keepdims=True)
    xn = (x * lax.rsqrt(ms + EPS) * g_ref[...]).astype(BF16)
    d_ff = wg_ref.shape[1]
    for c in range(d_ff // fc):
        sl = slice(c * fc, (c + 1) * fc)
        gate = _dot(xn, wg_ref[:, sl].astype(BF16))
        up = _dot(xn, wu_ref[:, sl].astype(BF16))
        h_scr[:, sl] = (gate * jax.nn.sigmoid(gate) * up).astype(BF16)
    o_ref[...] = x + 0.5 * _dot(h_scr[...], wd_ref[...].astype(BF16))


def _ffn(x2, g, wg, wu, wd, layer, mix=None, tm=512, fc=256):
    m, d = x2.shape
    d_ff = wg.shape[2]
    tm = min(tm, m)
    row = lambda w: pl.BlockSpec((tm, w), lambda i: (i, 0))
    in_specs = [row(d)]
    args = [x2]
    if mix is not None:
        ys, wo = mix
        in_specs += [row(GROUP_WIDTH)] * 4 + [_const_spec(wo.shape, layer)]
        args += list(ys) + [wo]
    in_specs += [_const_spec(a.shape, layer) for a in (g, wg, wu, wd)]
    args += [g, wg, wu, wd]
    return pl.pallas_call(
        functools.partial(_ffn_body, mix is not None, fc),
        grid=(m // tm,),
        in_specs=in_specs,
        out_specs=row(d),
        out_shape=jax.ShapeDtypeStruct((m, d), F32),
        scratch_shapes=[pltpu.VMEM((tm, d_ff), BF16)],
        compiler_params=_cparams(("parallel",)),
        name="ffn_mix" if mix is not None else "ffn",
    )(*args)


def _mix_in_body(tm, blk,
                 x_ref, gmix_ref, wn_ref, wt_ref, bd_ref,
                 gaq_ref, gak_ref,
                 wgu_ref, bgb_ref,
                 gqa_ref, wuq_ref, gdq_ref, cosT_ref, sinT_ref,
                 gkva_ref, wuk_ref, wuvT_ref, gdk_ref, gdkpe_ref, cpe_ref, spe_ref,
                 aqT_ref, ak_ref, avT_ref, iqT_ref, ik_ref, iwT_ref,
                 bq_ref, bk_ref, bv_ref, bla_ref, br_ref,
                 ch_ref,
                 dqT_ref, dk_ref, dvT_ref, knorm_ref):
    nlt = tm // LANE
    x = x_ref[0]
    ms = jnp.mean(x * x, axis=-1, keepdims=True)
    xn = (x * lax.rsqrt(ms + EPS) * gmix_ref[...]).astype(BF16)
    bd = bd_ref[...]
    lane = lax.broadcasted_iota(I32, (tm, LANE), 1)

    def lanes(g):
        return jnp.tile(g, (1, nlt))

    z = _dot(xn, wn_ref[...])
    zt = _dot_nt(wt_ref[...], xn)

    def zs(off, width):
        return z[:, off:off + width]

    cq = zs(N_DCQ, D_Q_RANK)
    cq_ms = jnp.mean(cq * cq, axis=-1, keepdims=True)
    cqn = (cq * lax.rsqrt(cq_ms + EPS) * gqa_ref[...]).astype(BF16)
    ckv = zs(N_DCKV, D_KV_RANK)
    ckv_ms = jnp.mean(ckv * ckv, axis=-1, keepdims=True)
    ckvn = (ckv * lax.rsqrt(ckv_ms + EPS) * gkva_ref[...]).astype(BF16)
    dq = _dot_nt(wuq_ref[...], cqn).reshape(D_HEADS, HEAD_PAD, tm)
    kn = _dot(ckvn, wuk_ref[...])
    dv = _dot_nt(wuvT_ref[...], ckvn).astype(BF16)
    gate = _dot(zs(N_BG, LANE).astype(BF16), wgu_ref[...]) + bgb_ref[...]
    ak = zs(N_AK, GROUP_WIDTH)
    ak_ms = _group_sum(ak * ak, bd) * (1.0 / A_HEAD_DIM)
    kn_ss = _group_sum(kn * kn, bd)

    aq = zt[T_AQ:T_AQ + A_HEADS * HEAD_PAD].reshape(A_HEADS, HEAD_PAD, tm)
    aq_ms = jnp.sum(aq * aq, axis=1, keepdims=True) * (1.0 / A_HEAD_DIM)
    aq = aq * lax.rsqrt(aq_ms + EPS) * lanes(gaq_ref[...])[None] * (A_HEAD_DIM ** -0.5 * LOG2E)
    aqT_ref[0] = aq.reshape(A_HEADS * HEAD_PAD, tm).astype(BF16)
    def emit_key_norms(first_row, sq_norms):
        top = jnp.max(sq_norms, axis=0, keepdims=True)
        for h in range(4):
            knorm_ref[0, 0, first_row + h:first_row + h + 1, :] = jnp.broadcast_to(
                top[:, 64 * h:64 * h + 1], (1, LANE))

    def with_ones_rows(vt, heads, dim):
        ones = jnp.ones((ONES_ROWS, tm), BF16)
        return jnp.concatenate([r for h in range(heads) for r in (vt[dim * h:dim * (h + 1)], ones)], axis=0)

    av = with_ones_rows(zt[T_AV:T_AV + GROUP_WIDTH].astype(BF16), A_HEADS, A_HEAD_DIM)
    for c in range(tm // blk):
        avT_ref[0, c] = av[:, c * blk:(c + 1) * blk]
    iqT_ref[0] = zt[T_IQ:T_IQ + IDX_HEADS * IDX_DIM].astype(BF16)
    iwT_ref[0] = zt[T_IW:T_IW + IDX_HEADS] * ((IDX_HEADS ** -0.5) * (IDX_DIM ** -0.5))

    ak = ak * lax.rsqrt(ak_ms + EPS) * gak_ref[...]
    for h in range(A_HEADS):
        pair = ak[:, LANE * (h // 2):LANE * (h // 2) + LANE]
        if h % 2 == 1:
            pair = pltpu.roll(pair, 64, 1)
        kh = jnp.where(lane < A_HEAD_DIM, pair, 0.0).astype(BF16)
        ak_ref[0, h] = kh
    emit_key_norms(0, _dot((ak * ak).astype(BF16), bd))
    ik_ref[0] = zs(N_IK, LANE)[:, :IDX_DIM].astype(BF16)

    bq_ref[0] = zs(N_BQ, LANE) * (B_KEY_DIM ** -0.5)
    bk_ref[0] = zs(N_BK, LANE)
    bv_ref[0] = zs(N_BV, GROUP_WIDTH).astype(BF16)
    bla_ref[0] = (jnp.minimum(gate, 0.0) - jnp.log(1.0 + jnp.exp(-jnp.abs(gate)))) * (1.0 / B_GATE_TAU)
    br_ref[0] = zs(N_BR, GROUP_WIDTH)

    ca = zs(N_CU, C_CHANNELS)
    cg = zs(N_CU + C_CHANNELS, C_CHANNELS)
    ch_ref[0] = ca * jax.nn.sigmoid(cg)

    dq_ms = jnp.sum(dq * dq, axis=1, keepdims=True) * (1.0 / D_QK)
    dq = dq * lax.rsqrt(dq_ms + EPS) * lanes(gdq_ref[...])[None] * (D_QK ** -0.5 * LOG2E)
    half = D_ROPE // 2
    x1 = dq[:, D_NOPE:D_NOPE + half]
    x2 = dq[:, D_NOPE + half:D_QK]
    cs = cosT_ref[...][None]
    sn = sinT_ref[...][None]
    dq = jnp.concatenate([dq[:, :D_NOPE], x1 * cs - x2 * sn, x2 * cs + x1 * sn, dq[:, D_QK:]], axis=1)
    dqT_ref[0] = dq.reshape(D_HEADS * HEAD_PAD, tm).astype(BF16)

    dv = with_ones_rows(dv, D_HEADS, D_V)
    for c in range(tm // blk):
        dvT_ref[0, c] = dv[:, c * blk:(c + 1) * blk]
    kpe = zs(N_DKPE, LANE)
    ss = kn_ss + jnp.sum(kpe * kpe, axis=-1, keepdims=True)
    rinv = lax.rsqrt(ss * (1.0 / D_QK) + EPS)
    kn = kn * rinv * gdk_ref[...]
    pe = kpe * gdkpe_ref[...]
    partner = jnp.where(lane < half, pltpu.roll(pe, LANE - half, 1), pltpu.roll(pe, half, 1))
    pe = pe * cpe_ref[...] + partner * spe_ref[...]
    pe = pltpu.roll(pe, D_NOPE, 1)
    for h in range(D_HEADS):
        pair = kn[:, LANE * (h // 2):LANE * (h // 2) + LANE]
        rpair = rinv[:, LANE * (h // 2):LANE * (h // 2) + LANE]
        if h % 2 == 1:
            pair = pltpu.roll(pair, 64, 1)
        else:
            rpair = pltpu.roll(rpair, 64, 1)
        kh = jnp.where(lane < D_NOPE, pair, pe * rpair).astype(BF16)
        dk_ref[0, h] = kh
    pe_sq = _dot((pe * pe).astype(BF16), jnp.ones((LANE, LANE), BF16))
    emit_key_norms(A_HEADS, _dot((kn * kn).astype(BF16), bd) + jnp.tile(pe_sq, (1, 2)) * (rinv * rinv))


def _mix_in(x3, p, layer, tm=512, blk=ATT_BLK):
    b, l, d = x3.shape
    tm = min(tm, l)
    grid = (b, l // tm)
    nck = l // blk
    tok = lambda w: pl.BlockSpec((1, tm, w), lambda bi, i: (bi, i, 0))
    tokT = lambda r: pl.BlockSpec((1, r, tm), lambda bi, i: (bi, 0, i))
    headk = pl.BlockSpec((1, 4, tm, HEAD_PAD), lambda bi, i: (bi, 0, i, 0))
    vrows = GROUP_WIDTH + 4 * ONES_ROWS
    chunkT = pl.BlockSpec((1, tm // blk, vrows, blk), lambda bi, i: (bi, i, 0, 0))
    postab = lambda r: pl.BlockSpec((r, tm), lambda bi, i: (0, i))
    posrow = pl.BlockSpec((tm, LANE), lambda bi, i: (i, 0))
    consts = [p["gmix"], p["wn"], p["wt"], p["bd"], p["gaq"], p["gak"], p["wgu"], p["bgb"],
              p["gqa"], p["wuq"], p["gdq"]]
    consts2 = [p["gkva"], p["wuk"], p["wuvT"], p["gdk"], p["gdkpe"]]
    lspec = lambda a: _const_spec(a.shape) if a is p["bd"] else _const_spec(a.shape, layer)
    in_specs = ([tok(d)] + [lspec(a) for a in consts]
                + [postab(D_ROPE // 2), postab(D_ROPE // 2)]
                + [lspec(a) for a in consts2] + [posrow, posrow])
    args = [x3] + consts + [p["cosT"], p["sinT"]] + consts2 + [p["cpe"], p["spe"]]
    sd = jax.ShapeDtypeStruct
    out_shape = [
        sd((b, A_HEADS * HEAD_PAD, l), BF16), sd((b, A_HEADS, l, HEAD_PAD), BF16),
        sd((b, nck, vrows, blk), BF16), sd((b, IDX_HEADS * IDX_DIM, l), BF16),
        sd((b, l, IDX_DIM), BF16), sd((b, IDX_HEADS, l), F32),
        sd((b, l, LANE), F32), sd((b, l, LANE), F32), sd((b, l, GROUP_WIDTH), BF16),
        sd((b, l, LANE), F32), sd((b, l, GROUP_WIDTH), F32),
        sd((b, l, C_CHANNELS), F32),
        sd((b, D_HEADS * HEAD_PAD, l), BF16), sd((b, D_HEADS, l, HEAD_PAD), BF16),
        sd((b, nck, vrows, blk), BF16),
        sd((b, l // tm, A_HEADS + D_HEADS, LANE), F32),
    ]
    out_specs = [
        tokT(A_HEADS * HEAD_PAD), headk, chunkT, tokT(IDX_HEADS * IDX_DIM),
        tok(IDX_DIM), tokT(IDX_HEADS),
        tok(LANE), tok(LANE), tok(GROUP_WIDTH), tok(LANE), tok(GROUP_WIDTH),
        tok(C_CHANNELS),
        tokT(D_HEADS * HEAD_PAD), headk, chunkT,
        pl.BlockSpec((1, 1, A_HEADS + D_HEADS, LANE), lambda bi, i: (bi, i, 0, 0)),
    ]
    return pl.pallas_call(
        functools.partial(_mix_in_body, tm, blk),
        grid=grid, in_specs=in_specs, out_specs=out_specs, out_shape=out_shape,
        compiler_params=_cparams(("parallel", "parallel")),
        name="mix_in",
    )(*args)


def _attn_body(nh, dv, blk, topk, is_dsa, *refs):
    if is_dsa:
        (kb_ref, qT_ref, k_ref, vT_ref, iqT_ref, wT_ref, ik_ref, bt_ref,
         o_ref, m_scr, acc_scr, key_scr, run_scr, plane_scr, active_scr) = refs
    else:
        kb_ref, qT_ref, k_ref, vT_ref, o_ref, m_scr, acc_scr = refs
    vr = dv + ONES_ROWS
    i = pl.program_id(1)
    t = blk
    row = lax.broadcasted_iota(I32, (t, t), 0)
    col = lax.broadcasted_iota(I32, (t, t), 1)
    causal_pen = jnp.where(row <= col, 0.0, NEG)

    m_scr[...] = jnp.full(m_scr.shape, NEG, F32)
    acc_scr[...] = jnp.zeros(acc_scr.shape, F32)

    if is_dsa:
        @pl.when(i == 0)
        def _():
            plane_scr[...] = jnp.zeros(plane_scr.shape, I32)

        def score_chunk(j, diag):
            r0 = pl.multiple_of(j * t, t)
            ikc = ik_ref[0, pl.ds(r0, t), :]
            s = jnp.zeros((t, t), F32)
            for h in range(IDX_HEADS):
                d = _dot(ikc, iqT_ref[0, IDX_DIM * h:IDX_DIM * (h + 1), :])
                s = s + jnp.maximum(d, 0.0) * wT_ref[0, h:h + 1, :]
            bits = lax.bitcast_convert_type(s, I32)
            key = jnp.where(bits < 0, bits ^ 0x7FFFFFFF, bits)
            if diag:
                key = jnp.where(row <= col, key, INT_MIN)
            key_scr[pl.ds(r0, t), :] = key
            w = [key[SUBLANES * r:SUBLANES * (r + 1), :] for r in range(32)]
            step, mask = 16, 0x0000FFFF
            while step:
                for lo in range(32):
                    if lo & step == 0:
                        hi = lo + step
                        swap = (w[lo] ^ jnp.right_shift(w[hi], step)) & mask
                        w[lo] = w[lo] ^ swap
                        w[hi] = w[hi] ^ jnp.left_shift(swap, step)
                step //= 2
                mask ^= (mask << step) & 0xFFFFFFFF
            c0 = pl.multiple_of(j * SUBLANES, SUBLANES)
            w[0] = ~w[0]
            for p in range(32):
                plane_scr[p, pl.ds(c0, SUBLANES), :] = w[p]

        def score_group(u, carry):
            for c in range(SCORE_GROUP):
                score_chunk(SCORE_GROUP * u + c, False)
            return carry

        lax.fori_loop(0, i // SCORE_GROUP, score_group, 0)
        for rem in range(SCORE_GROUP):
            @pl.when(i % SCORE_GROUP == rem)
            def _():
                for c in range(rem):
                    score_chunk(i - rem + c, False)
                score_chunk(i, True)

        def kth_largest(nrow):
            rows = pl.ds(0, nrow)
            in_range = lax.broadcasted_iota(I32, (nrow, t), 0) < (i + 1) * SUBLANES

            def col_count(words):
                pc = lax.population_count(words).reshape(nrow // SUBLANES, SUBLANES, t)
                return jnp.sum(jnp.sum(pc, axis=0), axis=0, keepdims=True)

            def decide(plane, active, n_gt, ans_u):
                ones = col_count(active & plane_scr[plane, rows, :])
                take = n_gt + ones >= topk
                bit = lax.shift_right_logical(jnp.int32(INT_MIN), jnp.int32(plane))
                return (jnp.where(take, 0, -1), n_gt + jnp.where(take, 0, ones),
                        ans_u | jnp.where(take, bit, 0))

            active_scr[rows, :] = jnp.where(in_range, -1, 0)
            state = decide(0, active_scr[rows, :], jnp.zeros((1, t), I32), jnp.zeros((1, t), I32))

            def plane_body(plane, state):
                flip, n_gt, ans_u = state
                active = active_scr[rows, :] & (plane_scr[plane - 1, rows, :] ^ flip)
                active_scr[rows, :] = active
                return decide(plane, active, n_gt, ans_u)

            _, n_gt, ans_u = lax.fori_loop(1, 32, plane_body, state)
            return n_gt, ans_u

        nrow_all = plane_scr.shape[1]
        nrow_half = (nrow_all // SUBLANES // 2) * SUBLANES
        if nrow_half:
            n_gt, ans_u = lax.cond((i + 1) * SUBLANES <= nrow_half,
                                   functools.partial(kth_largest, nrow_half),
                                   functools.partial(kth_largest, nrow_all))
        else:
            n_gt, ans_u = kth_largest(nrow_all)
        ans = ans_u ^ INT_MIN
        need = (topk - n_gt).astype(F32)
        run_scr[...] = jnp.zeros(run_scr.shape, F32)
        stri = jnp.where(col < row, 1.0, 0.0).astype(BF16)

    def logits(h, r0):
        return _dot(k_ref[0, h, pl.ds(r0, t), :], qT_ref[0, HEAD_PAD * h:HEAD_PAD * (h + 1), :])

    def selection_pens(r0s):
        pens = []
        if is_dsa:
            for r0 in r0s:
                kc = key_scr[pl.ds(r0, t), :]
                eq = kc == ans
                eqf = jnp.where(eq, 1.0, 0.0)
                run = run_scr[0:1, :]
                rank = _dot(stri, eqf.astype(BF16)) + run
                run_scr[0:1, :] = run + jnp.sum(eqf, axis=0, keepdims=True)
                pens.append(jnp.where(kc > ans, 0.0, jnp.where(eq, jnp.where(rank < need, 0.0, NEG), NEG)))
        return pens

    b_idx = pl.program_id(0)
    bounds = []
    for h in range(nh):
        qh = qT_ref[0, HEAD_PAD * h:HEAD_PAD * (h + 1), :].astype(F32)
        bound = jnp.sqrt(jnp.sum(qh * qh, axis=0, keepdims=True)) * (kb_ref[b_idx * nh + h] * BOUND_SLACK)
        if is_dsa:
            bound = bound + kb_ref[kb_ref.shape[0] - 1]
        bounds.append(bound)
    bound_max = jnp.max(functools.reduce(jnp.maximum, bounds))
    fixed_ref_ok = bound_max <= FIXED_REF_MAX

    def fixed_ref_chunks(js, kinds):
        n = len(js)
        r0s = [pl.multiple_of(j * t, t) for j in js]
        lgs = [[logits(h, r0s[c]) for c in range(n)] for h in range(nh)]
        pens = selection_pens(r0s)
        ps = []
        for h in range(nh):
            row_ps = []
            if is_dsa:
                far_ref = bounds[h] - bt_ref[2, h, 0:1, :]
            for c in range(n):
                x, ref = lgs[h][c], bounds[h]
                if is_dsa and kinds[c] == "far":
                    x, ref = x + pens[c], far_ref
                elif is_dsa:
                    x = x + (pens[c] + bt_ref[0 if kinds[c] == "diag" else 1, h])
                elif kinds[c] == "diag":
                    x = x + causal_pen
                row_ps.append(jnp.exp2(x - ref).astype(BF16))
            ps.append(row_ps[0] if n == 1 else jnp.concatenate(row_ps, axis=0))
        for h in range(nh):
            vs = slice(vr * h, vr * (h + 1))
            vt = [vT_ref[0, js[c], vs, :] for c in range(n)]
            acc_scr[vs, :] += _dot(vt[0] if n == 1 else jnp.concatenate(vt, axis=1), ps[h])

    def super_chunk(js, kinds):
        n = len(js)
        r0s = [pl.multiple_of(j * t, t) for j in js]
        lgs = [[logits(h, r0s[c]) for c in range(n)] for h in range(nh)]
        pens = selection_pens(r0s)
        ps, alphas = [], []
        for h in range(nh):
            xs = []
            for c, j in enumerate(js):
                diag = kinds[c] == "diag"
                lg = lgs[h][c]
                if is_dsa:
                    tile = 0 if diag else jnp.minimum(i - j, 2)
                    lg = lg + (pens[c] + bt_ref[tile, h])
                elif diag:
                    lg = lg + causal_pen
                xs.append(lg)
            m_old = m_scr[h, 0:1, :]
            m_new = m_old
            for x in xs:
                m_new = jnp.maximum(m_new, jnp.max(x, axis=0, keepdims=True))
            alpha = jnp.exp2(m_old - m_new)
            m_scr[h, 0:1, :] = m_new
            ps.append([jnp.exp2((x - m_new).astype(BF16)) for x in xs])
            alphas.append(alpha)
        for h in range(nh):
            vs = slice(vr * h, vr * (h + 1))
            pv = _dot(vT_ref[0, js[0], vs, :], ps[h][0])
            for c in range(1, n):
                pv = pv + _dot(vT_ref[0, js[c], vs, :], ps[h][c])
            acc_scr[vs, :] = alphas[h] * acc_scr[vs, :] + pv

    tail = ["near", "diag"] if is_dsa else ["diag"]
    nfar = jnp.maximum(i + 1 - len(tail), 0)

    def run_chunks(step):
        def group_body(u, carry):
            step([ATT_GROUP * u + c for c in range(ATT_GROUP)], ["far"] * ATT_GROUP)
            return carry

        lax.fori_loop(0, nfar // ATT_GROUP, group_body, 0)
        for rem in range(ATT_GROUP):
            @pl.when(jnp.logical_and(i + 1 >= len(tail), nfar % ATT_GROUP == rem))
            def _():
                first = i + 1 - len(tail) - rem
                step([first + c for c in range(rem + len(tail))], ["far"] * rem + tail)

        if is_dsa:
            @pl.when(i == 0)
            def _():
                step([i], ["diag"])

    pl.when(fixed_ref_ok)(functools.partial(run_chunks, fixed_ref_chunks))
    pl.when(jnp.logical_not(fixed_ref_ok))(functools.partial(run_chunks, super_chunk))

    outs = []
    for h in range(nh):
        outs.append(acc_scr[vr * h:vr * h + dv, :] / acc_scr[vr * h + dv:vr * h + dv + 1, :])
    o_ref[0] = jnp.transpose(jnp.concatenate(outs, axis=0)).astype(o_ref.dtype)


def _attention(qT, k, vT, kbound, dsa=None, blk=ATT_BLK):
    b, nh, l, _ = k.shape
    vr = vT.shape[2] // nh
    dv = vr - ONES_ROWS
    grid = (b, l // blk)
    qspec = lambda r: pl.BlockSpec((1, r, blk), lambda bi, i: (bi, 0, i))
    kspec = pl.BlockSpec((1, nh, l, HEAD_PAD), lambda bi, i: (bi, 0, 0, 0))
    vspec = pl.BlockSpec((1, l // blk, nh * vr, blk), lambda bi, i: (bi, 0, 0, 0))
    in_specs = [pl.BlockSpec(memory_space=pltpu.SMEM), qspec(nh * HEAD_PAD), kspec, vspec]
    args = [kbound, qT, k, vT]
    scratch = [pltpu.VMEM((nh, 8, blk), F32), pltpu.VMEM((nh * vr, blk), F32)]
    topk = 0
    if dsa is not None:
        iqT, wT, ik, bt = dsa
        topk = min(TOPK_MAX, l // 4)
        in_specs += [qspec(IDX_HEADS * IDX_DIM), qspec(IDX_HEADS),
                     pl.BlockSpec((1, l, IDX_DIM), lambda bi, i: (bi, 0, 0)),
                     _const_spec(bt.shape)]
        args += [iqT, wT, ik, bt]
        assert blk == 32 * SUBLANES
        nrow = (l // blk) * SUBLANES
        scratch += [pltpu.VMEM((l, blk), I32), pltpu.VMEM((8, blk), F32),
                    pltpu.VMEM((32, nrow, blk), I32), pltpu.VMEM((nrow, blk), I32)]
    return pl.pallas_call(
        functools.partial(_attn_body, nh, dv, blk, topk, dsa is not None),
        grid=grid, in_specs=in_specs,
        out_specs=pl.BlockSpec((1, blk, nh * dv), lambda bi, i: (bi, i, 0)),
        out_shape=jax.ShapeDtypeStruct((b, l, nh * dv), BF16),
        scratch_shapes=scratch,
        compiler_params=_cparams(("parallel", "arbitrary")),
        name="dsa_attn" if dsa is not None else "mla_attn",
    )(*args)


def _gla_body(tg, q_ref, k_ref, v_ref, la_ref, r_ref, go_ref, bd_ref, o_ref, st_scr, o_scr):
    @pl.when(pl.program_id(1) == 0)
    def _():
        st_scr[...] = jnp.zeros(st_scr.shape, F32)

    cs = B_CHUNK
    la = la_ref[0]
    rl = lax.broadcasted_iota(I32, (tg, LANE), 0) & (cs - 1)
    b = la
    s = 1
    while s < cs:
        b = b + jnp.where(rl >= s, pltpu.roll(b, s, 0), 0.0)
        s *= 2
    q = q_ref[0]
    k = k_ref[0]
    qb = q * jnp.exp(b)
    tri = lax.broadcasted_iota(I32, (cs, cs), 1) <= lax.broadcasted_iota(I32, (cs, cs), 0)
    khead = lax.broadcasted_iota(I32, (cs, LANE), 1) // B_KEY_DIM
    vhead = lax.broadcasted_iota(I32, (cs, GROUP_WIDTH), 1) // B_VAL_DIM
    same_head = (lax.broadcasted_iota(I32, (GROUP_WIDTH, LANE), 0) // B_VAL_DIM
                 == lax.broadcasted_iota(I32, (GROUP_WIDTH, LANE), 1) // B_KEY_DIM)
    for c in range(tg // cs):
        sl = slice(c * cs, (c + 1) * cs)
        bc = b[sl]
        mid = bc[cs // 2:cs // 2 + 1]
        last = bc[cs - 1:cs]
        qe = q[sl] * jnp.exp(bc - mid)
        ke = (k[sl] * jnp.exp(mid - bc)).astype(BF16)
        kd = (k[sl] * jnp.exp(last - bc)).astype(BF16)
        qbc = qb[sl].astype(BF16)
        vc = v_ref[0, sl, :]
        st = st_scr[...]
        qe4 = jnp.concatenate([jnp.where(khead == h, qe, 0.0) for h in range(B_HEADS)], axis=0)
        a_all = _dot_nt(qe4.astype(BF16), ke)
        o = _dot_nt(qbc, st.astype(BF16))
        for h in range(B_HEADS):
            a = jnp.where(tri, a_all[cs * h:cs * (h + 1)], 0.0).astype(BF16)
            o = o + _dot(a, jnp.where(vhead == h, vc, jnp.zeros_like(vc)))
        o_scr[sl, :] = o
        st_scr[...] = st * jnp.exp(last) + jnp.where(same_head, _dot_tn(vc, kd), 0.0)
    o = o_scr[...]
    ms = _group_sum(o * o, bd_ref[...]) * (1.0 / B_VAL_DIM)
    r = r_ref[0]
    o_ref[0] = (o * lax.rsqrt(ms + EPS) * go_ref[...] * (r * jax.nn.sigmoid(r))).astype(o_ref.dtype)


def _gla(bq, bk, bv, bla, br, go, bd, layer, tg=1024):
    b, l, _ = bq.shape
    tg = min(tg, l)
    tok = lambda w: pl.BlockSpec((1, tg, w), lambda bi, i: (bi, i, 0))
    return pl.pallas_call(
        functools.partial(_gla_body, tg),
        grid=(b, l // tg),
        in_specs=[tok(LANE), tok(LANE), tok(GROUP_WIDTH), tok(LANE), tok(GROUP_WIDTH),
                  _const_spec(go.shape, layer), _const_spec(bd.shape)],
        out_specs=tok(GROUP_WIDTH),
        out_shape=jax.ShapeDtypeStruct((b, l, GROUP_WIDTH), BF16),
        scratch_shapes=[pltpu.VMEM((B_HEADS * B_VAL_DIM, B_HEADS * B_KEY_DIM), F32),
                        pltpu.VMEM((tg, GROUP_WIDTH), F32)],
        compiler_params=_cparams(("parallel", "arbitrary")),
        name="gla",
    )(bq, bk, bv, bla, br, go, bd)


CONV_HIST = 32


def _conv_body(tc, h_ref, w_ref, b_ref, g_ref, o_ref, buf):
    @pl.when(pl.program_id(1) == 0)
    def _():
        buf[0:CONV_HIST, :] = jnp.zeros((CONV_HIST, C_CHANNELS), F32)

    @pl.when(pl.program_id(1) > 0)
    def _():
        buf[0:CONV_HIST, :] = buf[tc:tc + CONV_HIST, :]

    buf[CONV_HIST:CONV_HIST + tc, :] = h_ref[0]
    acc = jnp.zeros((tc, C_CHANNELS), F32) + b_ref[...]
    base = CONV_HIST - (C_KERNEL - 1)
    hb = buf[...]
    rows = tc + CONV_HIST
    for r in range(SUBLANES):
        shifted = hb if r == 0 else pltpu.roll(hb, rows - r, 0)
        for j in range(C_KERNEL):
            if (base + j) % SUBLANES == r:
                a0 = base + j - r
                acc = acc + shifted[a0:a0 + tc, :] * w_ref[j:j + 1, :]
    ms = jnp.mean(acc * acc, axis=-1, keepdims=True)
    y = acc * lax.rsqrt(ms + EPS) * g_ref[...]
    o_ref[0] = (y * jax.nn.sigmoid(y)).astype(o_ref.dtype)


def _conv(ch, w, bias, g, layer, tc=512):
    b, l, c = ch.shape
    tc = min(tc, l)
    tok = pl.BlockSpec((1, tc, c), lambda bi, i: (bi, i, 0))
    return pl.pallas_call(
        functools.partial(_conv_body, tc),
        grid=(b, l // tc),
        in_specs=[tok] + [_const_spec(a.shape, layer) for a in (w, bias, g)],
        out_specs=tok,
        out_shape=jax.ShapeDtypeStruct((b, l, c), BF16),
        scratch_shapes=[pltpu.VMEM((tc + CONV_HIST, c), F32)],
        compiler_params=_cparams(("parallel", "arbitrary")),
        name="conv",
    )(ch, w, bias, g)


def _t5_bucket(dist):
    max_exact = REL_BUCKETS // 2
    d = jnp.maximum(dist, 0)
    df = jnp.maximum(d, 1).astype(F32)
    large = max_exact + (jnp.log(df / max_exact) / math.log(REL_MAX_DIST / max_exact)
                         * (REL_BUCKETS - max_exact)).astype(I32)
    large = jnp.minimum(large, REL_BUCKETS - 1)
    return jnp.where(d < max_exact, d, large)


def _pad_cols(w, width):
    return jnp.pad(w, ((0, 0), (0, width - w.shape[1])))


def _lane_rep(v):
    return jnp.broadcast_to(v[:, None], (v.shape[0], LANE))


def _pad_heads_rows(w, heads, dim):
    w = w.reshape(heads, dim, w.shape[1])
    return jnp.pad(w, ((0, 0), (0, HEAD_PAD - dim), (0, 0))).reshape(heads * HEAD_PAD, -1)


def _split_w_in(w_in):
    widths = (256, 256, 256, 256, 32, 8, 128, 128, 256, 16, 256, 512, 256, 128, 32)
    offs = np.cumsum((0,) + widths)
    return [w_in[:, offs[n]:offs[n + 1]] for n in range(len(widths))]


def _row(v):
    return v[None, :].astype(F32)


def _one_layer_params(w):
    (aq, ak, av, iq, ik, iw, bq, bk, bv, bg, br, cu, dcq, dckv, dkpe) = _split_w_in(w["w_in"])
    wn = jnp.concatenate([ak, _pad_cols(ik, LANE), bq, bk, bv, _pad_cols(bg, LANE), br, cu, dcq, dckv,
                          _pad_cols(dkpe, LANE)], axis=1).astype(BF16)
    wt = jnp.concatenate([_pad_heads_rows(aq.T, A_HEADS, A_HEAD_DIM), av.T, iq.T,
                          jnp.pad(iw.T, ((0, T_TOT - T_IW - IDX_HEADS), (0, 0)))], axis=0).astype(BF16)
    pad_to = lambda v, n: jnp.pad(v, (0, n - v.shape[0]))
    ukv = w["d_ukv"].reshape(D_KV_RANK, D_HEADS, D_NOPE + D_V)
    wuk = ukv[:, :, :D_NOPE].reshape(D_KV_RANK, D_HEADS * D_NOPE)
    wuv = ukv[:, :, D_NOPE:].reshape(D_KV_RANK, D_HEADS * D_V)
    gdk = w["d_k_norm"]
    return dict(
        gmix=_row(w["mix_norm"]), wn=wn, wt=wt,
        gaq=_lane_rep(pad_to(w["a_q_norm"], HEAD_PAD)),
        gak=_row(jnp.tile(w["a_k_norm"], A_HEADS)),
        wgu=jnp.pad(w["b_gate_up"], ((0, LANE - B_GATE_RANK), (0, 0))).astype(BF16),
        bgb=_row(w["b_gate_bias"]),
        gqa=_row(w["d_qa_norm"]),
        wuq=_pad_heads_rows(w["d_uq"].T, D_HEADS, D_QK).astype(BF16),
        gdq=_lane_rep(pad_to(w["d_q_norm"], HEAD_PAD)),
        gkva=_row(w["d_kva_norm"]), wuk=wuk.astype(BF16), wuvT=wuv.T.astype(BF16),
        gdk=_row(jnp.tile(gdk[:D_NOPE], D_HEADS)), gdkpe=_row(pad_to(gdk[D_NOPE:], LANE)),
        gbo=_row(jnp.tile(w["b_out_norm"], B_HEADS)),
        cw=jnp.pad(w["c_dw_w"][:, 0, :], ((0, CONV_HIST - C_KERNEL), (0, 0))).astype(F32),
        cb=_row(w["c_dw_b"]), cg=_row(w["c_norm"]),
    )


def _shared_tables(seq):
    hid = np.arange(GROUP_WIDTH) // 64
    bd = jnp.asarray(hid[:, None] == hid[None, :], dtype=BF16)
    half = D_ROPE // 2
    freqs = ROPE_THETA ** (-jnp.arange(half, dtype=F32) / half)
    ang = jnp.arange(seq).astype(F32)[:, None] * freqs[None, :]
    cos, sin = jnp.cos(ang), jnp.sin(ang)
    zeros = jnp.zeros((seq, LANE - D_ROPE), F32)
    cpe = jnp.concatenate([cos, cos, zeros], axis=1)
    spe = jnp.concatenate([-sin, sin, zeros], axis=1)
    return dict(bd=bd, cosT=cos.T, sinT=sin.T, cpe=cpe, spe=spe)


def _bias_tiles(rel_bias, blk):
    assert REL_MAX_DIST <= blk + 1
    kk = jnp.arange(blk)[:, None]
    qq = jnp.arange(blk)[None, :]
    rb = rel_bias.astype(F32).T
    m = 2 * blk + 1
    onehot = _t5_bucket(jnp.arange(m))[None, :, None] == jnp.arange(REL_BUCKETS)
    by_dist = jnp.sum(jnp.where(onehot, rb[:, None, :], 0.0), axis=-1) * LOG2E
    toe = jnp.tile(by_dist, (1, blk))[:, :blk * (m - 1)].reshape(-1, blk, m - 1)
    d0 = jnp.where(kk <= qq, toe[:, :, :blk], NEG)
    d1 = toe[:, :, blk:2 * blk]
    far = jnp.broadcast_to(by_dist[:, m - 1][:, None, None], d1.shape)
    return jnp.stack([d0, d1, far])


def _key_bounds(knorm, bias_max):
    km = jnp.sqrt(jnp.max(knorm[..., 0], axis=1))
    kb_a = jnp.concatenate([km[:, :A_HEADS].reshape(-1), bias_max.reshape(1)])
    return kb_a, km[:, A_HEADS:].reshape(-1)


def kernel(x, ffn1_norm, ffn1_gate, ffn1_up, ffn1_down, mix_norm, w_in, a_q_norm, a_k_norm, rel_bias,
           b_gate_up, b_gate_bias, b_out_norm, c_dw_w, c_dw_b, c_norm, d_qa_norm, d_uq, d_kva_norm,
           d_ukv, d_q_norm, d_k_norm, w_out, ffn2_norm, ffn2_gate, ffn2_up, ffn2_down):
    w = dict(mix_norm=mix_norm, w_in=w_in, a_q_norm=a_q_norm, a_k_norm=a_k_norm, b_gate_up=b_gate_up,
             b_gate_bias=b_gate_bias, b_out_norm=b_out_norm, c_dw_w=c_dw_w, c_dw_b=c_dw_b, c_norm=c_norm,
             d_qa_norm=d_qa_norm, d_uq=d_uq, d_kva_norm=d_kva_norm, d_ukv=d_ukv,
             d_q_norm=d_q_norm, d_k_norm=d_k_norm)
    bsz, seq, dm = x.shape
    depth = w_in.shape[0]
    blk = min(ATT_BLK, seq)
    bt = _bias_tiles(rel_bias, blk)
    bias_max = jnp.max(bt)
    p = {**jax.vmap(_one_layer_params)(w), **_shared_tables(seq)}
    stacked_row = lambda v: v[:, None, :].astype(F32)
    ffn1 = (stacked_row(ffn1_norm), ffn1_gate, ffn1_up, ffn1_down)
    ffn2 = (stacked_row(ffn2_norm), ffn2_gate, ffn2_up, ffn2_down)
    wo = w_out.astype(BF16)
    x2 = x.reshape(bsz * seq, dm)
    for l in range(depth):
        x2 = _ffn(x2, *ffn1, l)
        (aqT, akh, avT, iqT, aik, iwT, bq, bk, bv, bla, br, ch, dqT, dkh, dvT, knorm) = _mix_in(
            x2.reshape(bsz, seq, dm), p, l, blk=blk)
        kb_a, kb_d = _key_bounds(knorm, bias_max)
        y_a = _attention(aqT, akh, avT, kb_a, dsa=(iqT, iwT, aik, bt), blk=blk)
        y_b = _gla(bq, bk, bv, bla, br, p["gbo"], p["bd"], l)
        y_c = _conv(ch, p["cw"], p["cb"], p["cg"], l)
        y_d = _attention(dqT, dkh, dvT, kb_d, blk=blk)
        ys = [y.reshape(bsz * seq, GROUP_WIDTH) for y in (y_a, y_b, y_c, y_d)]
        x2 = _ffn(x2, *ffn2, l, mix=(ys, wo))
    return x2.reshape(bsz, seq, dm)
```

```python
import functools
import math

import jax
import jax.numpy as jnp
import numpy as np
from jax import lax
from jax.experimental import pallas as pl
from jax.experimental.pallas import tpu as pltpu

F32 = jnp.float32
BF16 = jnp.bfloat16
I32 = jnp.int32

EPS = 1e-6
GROUP_WIDTH = 256
A_HEADS, A_HEAD_DIM = 4, 64
IDX_HEADS, IDX_DIM = 8, 32
TOPK_MAX = 256
REL_BUCKETS, REL_MAX_DIST = 32, 128
B_HEADS, B_KEY_DIM, B_VAL_DIM, B_GATE_RANK = 4, 32, 64, 16
B_GATE_TAU = 16.0
B_CHUNK = 64
C_CHANNELS, C_KERNEL = 256, 31
D_HEADS, D_Q_RANK, D_KV_RANK, D_NOPE, D_ROPE, D_V = 4, 256, 128, 64, 32, 64
D_QK = D_NOPE + D_ROPE
ROPE_THETA = 10000.0

LANE = 128
SUBLANES = 8
HEAD_PAD = 128
ONES_ROWS = 16
ATT_BLK = 256
ATT_GROUP = 4
SCORE_GROUP = 4
BOUND_SLACK = 1.01
FIXED_REF_MAX = 40.0
INT_MIN = -2 ** 31
NEG = -1e30
LOG2E = math.log2(math.e)
VMEM_LIMIT = 56 * 1024 * 1024

N_AK, N_IK, N_BQ, N_BK, N_BV, N_BG, N_BR, N_CU, N_DCQ, N_DCKV, N_DKPE, N_TOT = (
    0, 256, 384, 512, 640, 896, 1024, 1280, 1792, 2048, 2176, 2304)
T_AQ, T_AV, T_IQ, T_IW, T_TOT = 0, 512, 768, 1024, 1040


def _dot(a, b):
    return jnp.dot(a, b, preferred_element_type=F32)


def _dot_nt(a, b):
    return lax.dot_general(a, b, (((1,), (1,)), ((), ())), preferred_element_type=F32)


def _dot_tn(a, b):
    return lax.dot_general(a, b, (((0,), (0,)), ((), ())), preferred_element_type=F32)


def _group_sum(x2, bd):
    hi = x2.astype(BF16)
    lo = (x2 - hi.astype(F32)).astype(BF16)
    return _dot(hi, bd) + _dot(lo, bd)


def _const_spec(shape, layer=None):
    nd = len(shape)
    if layer is None:
        return pl.BlockSpec(shape, lambda *_: (0,) * nd, pipeline_mode=pl.Buffered(1))
    return pl.BlockSpec((None,) + tuple(shape[1:]), lambda *_: (layer,) + (0,) * (nd - 1),
                        pipeline_mode=pl.Buffered(1))


def _cparams(sem):
    return pltpu.CompilerParams(dimension_semantics=sem, vmem_limit_bytes=VMEM_LIMIT)


def _ffn_body(has_mix, fc, layer, *refs):
    if has_mix:
        x_ref, ya, yb, yc, yd, wo_ref, g_ref, wg_hbm, wu_hbm, wd_hbm, o_ref, h_scr, wg_buf, wu_buf, wd_buf, sems = refs
    else:
        x_ref, g_ref, wg_hbm, wu_hbm, wd_hbm, o_ref, h_scr, wg_buf, wu_buf, wd_buf, sems = refs
    d_ff = wg_buf.shape[1]
    nchunk = d_ff // fc
    first = pl.program_id(0) == 0

    def weight_copies(c):
        cols = pl.ds(c * fc, fc)
        return (pltpu.make_async_copy(wg_hbm.at[layer, :, cols], wg_buf.at[:, cols], sems.at[0, c]),
                pltpu.make_async_copy(wu_hbm.at[layer, :, cols], wu_buf.at[:, cols], sems.at[1, c]),
                pltpu.make_async_copy(wd_hbm.at[layer, cols, :], wd_buf.at[cols, :], sems.at[2, c]))

    @pl.when(first)
    def _():
        for c in range(nchunk):
            gate_cp, up_cp, _ = weight_copies(c)
            gate_cp.start()
            up_cp.start()
        for c in range(nchunk):
            weight_copies(c)[2].start()

    def tile(wait_for_weights):
        x = x_ref[...]
        if has_mix:
            y = jnp.concatenate([ya[...], yb[...], yc[...], yd[...]], axis=-1)
            x = x + _dot(y, wo_ref[...])
        ms = jnp.mean(x * x, axis=-1, keepdims=True)
        xn = (x * lax.rsqrt(ms + EPS) * g_ref[...]).astype(BF16)
        for c in range(nchunk):
            sl = slice(c * fc, (c + 1) * fc)
            if wait_for_weights:
                gate_cp, up_cp, _ = weight_copies(c)
                gate_cp.wait()
                up_cp.wait()
            gate = _dot(xn, wg_buf[:, sl].astype(BF16))
            up = _dot(xn, wu_buf[:, sl].astype(BF16))
            h_scr[:, sl] = (gate * jax.nn.sigmoid(gate) * up).astype(BF16)
        if wait_for_weights:
            for c in range(nchunk):
                weight_copies(c)[2].wait()
        o_ref[...] = x + 0.5 * _dot(h_scr[...], wd_buf[...].astype(BF16))

    pl.when(first)(functools.partial(tile, True))
    pl.when(jnp.logical_not(first))(functools.partial(tile, False))


def _ffn(x2, g, wg, wu, wd, layer, mix=None, tm=512, fc=256):
    m, d = x2.shape
    d_ff = wg.shape[2]
    tm = min(tm, m)
    row = lambda w: pl.BlockSpec((tm, w), lambda i: (i, 0))
    in_specs = [row(d)]
    args = [x2]
    if mix is not None:
        ys, wo = mix
        in_specs += [row(GROUP_WIDTH)] * 4 + [_const_spec(wo.shape, layer)]
        args += list(ys) + [wo]
    in_specs += [_const_spec(g.shape, layer)] + [pl.BlockSpec(memory_space=pl.ANY)] * 3
    args += [g, wg, wu, wd]
    return pl.pallas_call(
        functools.partial(_ffn_body, mix is not None, fc, layer),
        grid=(m // tm,),
        in_specs=in_specs,
        out_specs=row(d),
        out_shape=jax.ShapeDtypeStruct((m, d), F32),
        scratch_shapes=[pltpu.VMEM((tm, d_ff), BF16), pltpu.VMEM((d, d_ff), F32), pltpu.VMEM((d, d_ff), F32),
                        pltpu.VMEM((d_ff, d), F32), pltpu.SemaphoreType.DMA((3, d_ff // fc))],
        compiler_params=_cparams(("arbitrary",)),
        name="ffn_mix" if mix is not None else "ffn",
    )(*args)


def _mix_in_body(tm, blk,
                 x_ref, gmix_ref, wn_ref, wt_ref, bd_ref,
                 gaq_ref, gak_ref,
                 wgu_ref, bgb_ref,
                 gqa_ref, wuq_ref, gdq_ref, cosT_ref, sinT_ref,
                 gkva_ref, wuk_ref, wuvT_ref, gdk_ref, gdkpe_ref, cpe_ref, spe_ref,
                 aqT_ref, ak_ref, avT_ref, iqT_ref, ik_ref, iwT_ref,
                 bq_ref, bk_ref, bv_ref, bla_ref, br_ref,
                 ch_ref,
                 dqT_ref, dk_ref, dvT_ref, knorm_ref):
    nlt = tm // LANE
    x = x_ref[0]
    ms = jnp.mean(x * x, axis=-1, keepdims=True)
    xn = (x * lax.rsqrt(ms + EPS) * gmix_ref[...]).astype(BF16)
    bd = bd_ref[...]
    lane = lax.broadcasted_iota(I32, (tm, LANE), 1)

    def lanes(g):
        return jnp.tile(g, (1, nlt))

    z = _dot(xn, wn_ref[...])
    zt = _dot_nt(wt_ref[...], xn)

    def zs(off, width):
        return z[:, off:off + width]

    cq = zs(N_DCQ, D_Q_RANK)
    cq_ms = jnp.mean(cq * cq, axis=-1, keepdims=True)
    cqn = (cq * lax.rsqrt(cq_ms + EPS) * gqa_ref[...]).astype(BF16)
    ckv = zs(N_DCKV, D_KV_RANK)
    ckv_ms = jnp.mean(ckv * ckv, axis=-1, keepdims=True)
    ckvn = (ckv * lax.rsqrt(ckv_ms + EPS) * gkva_ref[...]).astype(BF16)
    dq = _dot_nt(wuq_ref[...], cqn).reshape(D_HEADS, HEAD_PAD, tm)
    kn = _dot(ckvn, wuk_ref[...])
    dv = _dot_nt(wuvT_ref[...], ckvn).astype(BF16)
    gate = _dot(zs(N_BG, LANE).astype(BF16), wgu_ref[...]) + bgb_ref[...]
    ak = zs(N_AK, GROUP_WIDTH)
    ak_ms = _group_sum(ak * ak, bd) * (1.0 / A_HEAD_DIM)
    kn_ss = _group_sum(kn * kn, bd)

    aq = zt[T_AQ:T_AQ + A_HEADS * HEAD_PAD].reshape(A_HEADS, HEAD_PAD, tm)
    aq_ms = jnp.sum(aq * aq, axis=1, keepdims=True) * (1.0 / A_HEAD_DIM)
    aq = aq * lax.rsqrt(aq_ms + EPS) * lanes(gaq_ref[...])[None] * (A_HEAD_DIM ** -0.5 * LOG2E)
    aqT_ref[0] = aq.reshape(A_HEADS * HEAD_PAD, tm).astype(BF16)
    def emit_key_norms(first_row, sq_norms):
        top = jnp.max(sq_norms, axis=0, keepdims=True)
        for h in range(4):
            knorm_ref[0, 0, first_row + h:first_row + h + 1, :] = jnp.broadcast_to(
                top[:, 64 * h:64 * h + 1], (1, LANE))

    def with_ones_rows(vt, heads, dim):
        ones = jnp.ones((ONES_ROWS, tm), BF16)
        return jnp.concatenate([r for h in range(heads) for r in (vt[dim * h:dim * (h + 1)], ones)], axis=0)

    av = with_ones_rows(zt[T_AV:T_AV + GROUP_WIDTH].astype(BF16), A_HEADS, A_HEAD_DIM)
    for c in range(tm // blk):
        avT_ref[0, c] = av[:, c * blk:(c + 1) * blk]
    iqT_ref[0] = zt[T_IQ:T_IQ + IDX_HEADS * IDX_DIM].astype(BF16)
    iwT_ref[0] = zt[T_IW:T_IW + IDX_HEADS] * ((IDX_HEADS ** -0.5) * (IDX_DIM ** -0.5))

    ak = ak * lax.rsqrt(ak_ms + EPS) * gak_ref[...]
    for h in range(A_HEADS):
        pair = ak[:, LANE * (h // 2):LANE * (h // 2) + LANE]
        if h % 2 == 1:
            pair = pltpu.roll(pair, 64, 1)
        kh = jnp.where(lane < A_HEAD_DIM, pair, 0.0).astype(BF16)
        ak_ref[0, h] = kh
    emit_key_norms(0, _dot((ak * ak).astype(BF16), bd))
    ik_ref[0] = zs(N_IK, LANE)[:, :IDX_DIM].astype(BF16)

    bq_ref[0] = zs(N_BQ, LANE) * (B_KEY_DIM ** -0.5)
    bk_ref[0] = zs(N_BK, LANE)
    bv_ref[0] = zs(N_BV, GROUP_WIDTH).astype(BF16)
    bla_ref[0] = (jnp.minimum(gate, 0.0) - jnp.log(1.0 + jnp.exp(-jnp.abs(gate)))) * (1.0 / B_GATE_TAU)
    br_ref[0] = zs(N_BR, GROUP_WIDTH)

    ca = zs(N_CU, C_CHANNELS)
    cg = zs(N_CU + C_CHANNELS, C_CHANNELS)
    ch_ref[0] = ca * jax.nn.sigmoid(cg)

    dq_ms = jnp.sum(dq * dq, axis=1, keepdims=True) * (1.0 / D_QK)
    dq = dq * lax.rsqrt(dq_ms + EPS) * lanes(gdq_ref[...])[None] * (D_QK ** -0.5 * LOG2E)
    half = D_ROPE // 2
    x1 = dq[:, D_NOPE:D_NOPE + half]
    x2 = dq[:, D_NOPE + half:D_QK]
    cs = cosT_ref[...][None]
    sn = sinT_ref[...][None]
    dq = jnp.concatenate([dq[:, :D_NOPE], x1 * cs - x2 * sn, x2 * cs + x1 * sn, dq[:, D_QK:]], axis=1)
    dqT_ref[0] = dq.reshape(D_HEADS * HEAD_PAD, tm).astype(BF16)

    dv = with_ones_rows(dv, D_HEADS, D_V)
    for c in range(tm // blk):
        dvT_ref[0, c] = dv[:, c * blk:(c + 1) * blk]
    kpe = zs(N_DKPE, LANE)
    ss = kn_ss + jnp.sum(kpe * kpe, axis=-1, keepdims=True)
    rinv = lax.rsqrt(ss * (1.0 / D_QK) + EPS)
    kn = kn * rinv * gdk_ref[...]
    pe = kpe * gdkpe_ref[...]
    partner = jnp.where(lane < half, pltpu.roll(pe, LANE - half, 1), pltpu.roll(pe, half, 1))
    pe = pe * cpe_ref[...] + partner * spe_ref[...]
    pe = pltpu.roll(pe, D_NOPE, 1)
    for h in range(D_HEADS):
        pair = kn[:, LANE * (h // 2):LANE * (h // 2) + LANE]
        rpair = rinv[:, LANE * (h // 2):LANE * (h // 2) + LANE]
        if h % 2 == 1:
            pair = pltpu.roll(pair, 64, 1)
        else:
            rpair = pltpu.roll(rpair, 64, 1)
        kh = jnp.where(lane < D_NOPE, pair, pe * rpair).astype(BF16)
        dk_ref[0, h] = kh
    pe_sq = _dot((pe * pe).astype(BF16), jnp.ones((LANE, LANE), BF16))
    emit_key_norms(A_HEADS, _dot((kn * kn).astype(BF16), bd) + jnp.tile(pe_sq, (1, 2)) * (rinv * rinv))


def _mix_in(x3, p, layer, tm=512, blk=ATT_BLK):
    b, l, d = x3.shape
    tm = min(tm, l)
    grid = (b, l // tm)
    nck = l // blk
    tok = lambda w: pl.BlockSpec((1, tm, w), lambda bi, i: (bi, i, 0))
    tokT = lambda r: pl.BlockSpec((1, r, tm), lambda bi, i: (bi, 0, i))
    headk = pl.BlockSpec((1, 4, tm, HEAD_PAD), lambda bi, i: (bi, 0, i, 0))
    vrows = GROUP_WIDTH + 4 * ONES_ROWS
    chunkT = pl.BlockSpec((1, tm // blk, vrows, blk), lambda bi, i: (bi, i, 0, 0))
    postab = lambda r: pl.BlockSpec((r, tm), lambda bi, i: (0, i))
    posrow = pl.BlockSpec((tm, LANE), lambda bi, i: (i, 0))
    consts = [p["gmix"], p["wn"], p["wt"], p["bd"], p["gaq"], p["gak"], p["wgu"], p["bgb"],
              p["gqa"], p["wuq"], p["gdq"]]
    consts2 = [p["gkva"], p["wuk"], p["wuvT"], p["gdk"], p["gdkpe"]]
    lspec = lambda a: _const_spec(a.shape) if a is p["bd"] else _const_spec(a.shape, layer)
    in_specs = ([tok(d)] + [lspec(a) for a in consts]
                + [postab(D_ROPE // 2), postab(D_ROPE // 2)]
                + [lspec(a) for a in consts2] + [posrow, posrow])
    args = [x3] + consts + [p["cosT"], p["sinT"]] + consts2 + [p["cpe"], p["spe"]]
    sd = jax.ShapeDtypeStruct
    out_shape = [
        sd((b, A_HEADS * HEAD_PAD, l), BF16), sd((b, A_HEADS, l, HEAD_PAD), BF16),
        sd((b, nck, vrows, blk), BF16), sd((b, IDX_HEADS * IDX_DIM, l), BF16),
        sd((b, l, IDX_DIM), BF16), sd((b, IDX_HEADS, l), F32),
        sd((b, l, LANE), F32), sd((b, l, LANE), F32), sd((b, l, GROUP_WIDTH), BF16),
        sd((b, l, LANE), F32), sd((b, l, GROUP_WIDTH), F32),
        sd((b, l, C_CHANNELS), F32),
        sd((b, D_HEADS * HEAD_PAD, l), BF16), sd((b, D_HEADS, l, HEAD_PAD), BF16),
        sd((b, nck, vrows, blk), BF16),
        sd((b, l // tm, A_HEADS + D_HEADS, LANE), F32),
    ]
    out_specs = [
        tokT(A_HEADS * HEAD_PAD), headk, chunkT, tokT(IDX_HEADS * IDX_DIM),
        tok(IDX_DIM), tokT(IDX_HEADS),
        tok(LANE), tok(LANE), tok(GROUP_WIDTH), tok(LANE), tok(GROUP_WIDTH),
        tok(C_CHANNELS),
        tokT(D_HEADS * HEAD_PAD), headk, chunkT,
        pl.BlockSpec((1, 1, A_HEADS + D_HEADS, LANE), lambda bi, i: (bi, i, 0, 0)),
    ]
    return pl.pallas_call(
        functools.partial(_mix_in_body, tm, blk),
        grid=grid, in_specs=in_specs, out_specs=out_specs, out_shape=out_shape,
        compiler_params=_cparams(("parallel", "parallel")),
        name="mix_in",
    )(*args)


def _attn_body(nh, dv, blk, topk, is_dsa, *refs):
    if is_dsa:
        (kb_ref, qT_ref, k_ref, vT_ref, iqT_ref, wT_ref, ik_ref, bt_ref,
         o_ref, m_scr, acc_scr, key_scr, run_scr, plane_scr, active_scr) = refs
    else:
        kb_ref, qT_ref, k_ref, vT_ref, o_ref, m_scr, acc_scr = refs
    vr = dv + ONES_ROWS
    i = pl.program_id(1)
    t = blk
    row = lax.broadcasted_iota(I32, (t, t), 0)
    col = lax.broadcasted_iota(I32, (t, t), 1)
    causal_pen = jnp.where(row <= col, 0.0, NEG)

    m_scr[...] = jnp.full(m_scr.shape, NEG, F32)
    acc_scr[...] = jnp.zeros(acc_scr.shape, F32)

    if is_dsa:
        @pl.when(i == 0)
        def _():
            plane_scr[...] = jnp.zeros(plane_scr.shape, I32)

        def score_chunk(j, diag):
            r0 = pl.multiple_of(j * t, t)
            ikc = ik_ref[0, pl.ds(r0, t), :]
            s = jnp.zeros((t, t), F32)
            for h in range(IDX_HEADS):
                d = _dot(ikc, iqT_ref[0, IDX_DIM * h:IDX_DIM * (h + 1), :])
                s = s + jnp.maximum(d, 0.0) * wT_ref[0, h:h + 1, :]
            bits = lax.bitcast_convert_type(s, I32)
            key = jnp.where(bits < 0, bits ^ 0x7FFFFFFF, bits)
            if diag:
                key = jnp.where(row <= col, key, INT_MIN)
            key_scr[pl.ds(r0, t), :] = key
            w = [key[SUBLANES * r:SUBLANES * (r + 1), :] for r in range(32)]
            step, mask = 16, 0x0000FFFF
            while step:
                for lo in range(32):
                    if lo & step == 0:
                        hi = lo + step
                        swap = (w[lo] ^ jnp.right_shift(w[hi], step)) & mask
                        w[lo] = w[lo] ^ swap
                        w[hi] = w[hi] ^ jnp.left_shift(swap, step)
                step //= 2
                mask ^= (mask << step) & 0xFFFFFFFF
            c0 = pl.multiple_of(j * SUBLANES, SUBLANES)
            w[0] = ~w[0]
            for p in range(32):
                plane_scr[p, pl.ds(c0, SUBLANES), :] = w[p]

        def score_group(u, carry):
            for c in range(SCORE_GROUP):
                score_chunk(SCORE_GROUP * u + c, False)
            return carry

        lax.fori_loop(0, i // SCORE_GROUP, score_group, 0)
        for rem in range(SCORE_GROUP):
            @pl.when(i % SCORE_GROUP == rem)
            def _():
                for c in range(rem):
                    score_chunk(i - rem + c, False)
                score_chunk(i, True)

        def kth_largest(nrow):
            rows = pl.ds(0, nrow)
            in_range = lax.broadcasted_iota(I32, (nrow, t), 0) < (i + 1) * SUBLANES

            def col_count(words):
                pc = lax.population_count(words).reshape(nrow // SUBLANES, SUBLANES, t)
                return jnp.sum(jnp.sum(pc, axis=0), axis=0, keepdims=True)

            def decide(plane, active, n_gt, ans_u):
                ones = col_count(active & plane_scr[plane, rows, :])
                take = n_gt + ones >= topk
                bit = lax.shift_right_logical(jnp.int32(INT_MIN), jnp.int32(plane))
                return (jnp.where(take, 0, -1), n_gt + jnp.where(take, 0, ones),
                        ans_u | jnp.where(take, bit, 0))

            active_scr[rows, :] = jnp.where(in_range, -1, 0)
            state = decide(0, active_scr[rows, :], jnp.zeros((1, t), I32), jnp.zeros((1, t), I32))

            def plane_body(plane, state):
                flip, n_gt, ans_u = state
                active = active_scr[rows, :] & (plane_scr[plane - 1, rows, :] ^ flip)
                active_scr[rows, :] = active
                return decide(plane, active, n_gt, ans_u)

            _, n_gt, ans_u = lax.fori_loop(1, 32, plane_body, state)
            return n_gt, ans_u

        nrow_all = plane_scr.shape[1]
        nrow_half = (nrow_all // SUBLANES // 2) * SUBLANES
        if nrow_half:
            n_gt, ans_u = lax.cond((i + 1) * SUBLANES <= nrow_half,
                                   functools.partial(kth_largest, nrow_half),
                                   functools.partial(kth_largest, nrow_all))
        else:
            n_gt, ans_u = kth_largest(nrow_all)
        ans = ans_u ^ INT_MIN
        need = (topk - n_gt).astype(F32)
        run_scr[...] = jnp.zeros(run_scr.shape, F32)
        stri = jnp.where(col < row, 1.0, 0.0).astype(BF16)

    def logits(h, r0):
        return _dot(k_ref[0, h, pl.ds(r0, t), :], qT_ref[0, HEAD_PAD * h:HEAD_PAD * (h + 1), :])

    def selection_pens(r0s):
        pens = []
        if is_dsa:
            for r0 in r0s:
                kc = key_scr[pl.ds(r0, t), :]
                eq = kc == ans
                eqf = jnp.where(eq, 1.0, 0.0)
                run = run_scr[0:1, :]
                rank = _dot(stri, eqf.astype(BF16)) + run
                run_scr[0:1, :] = run + jnp.sum(eqf, axis=0, keepdims=True)
                pens.append(jnp.where(kc > ans, 0.0, jnp.where(eq, jnp.where(rank < need, 0.0, NEG), NEG)))
        return pens

    b_idx = pl.program_id(0)
    bounds = []
    for h in range(nh):
        qh = qT_ref[0, HEAD_PAD * h:HEAD_PAD * (h + 1), :].astype(F32)
        bound = jnp.sqrt(jnp.sum(qh * qh, axis=0, keepdims=True)) * (kb_ref[b_idx * nh + h] * BOUND_SLACK)
        if is_dsa:
            bound = bound + kb_ref[kb_ref.shape[0] - 1]
        bounds.append(bound)
    bound_max = jnp.max(functools.reduce(jnp.maximum, bounds))
    fixed_ref_ok = bound_max <= FIXED_REF_MAX

    def fixed_ref_chunks(js, kinds):
        n = len(js)
        r0s = [pl.multiple_of(j * t, t) for j in js]
        lgs = [[logits(h, r0s[c]) for c in range(n)] for h in range(nh)]
        pens = selection_pens(r0s)
        ps = []
        for h in range(nh):
            row_ps = []
            if is_dsa:
                far_ref = bounds[h] - bt_ref[2, h, 0:1, :]
            for c in range(n):
                x, ref = lgs[h][c], bounds[h]
                if is_dsa and kinds[c] == "far":
                    x, ref = x + pens[c], far_ref
                elif is_dsa:
                    x = x + (pens[c] + bt_ref[0 if kinds[c] == "diag" else 1, h])
                elif kinds[c] == "diag":
                    x = x + causal_pen
                row_ps.append(jnp.exp2(x - ref).astype(BF16))
            ps.append(row_ps[0] if n == 1 else jnp.concatenate(row_ps, axis=0))
        for h in range(nh):
            vs = slice(vr * h, vr * (h + 1))
            vt = [vT_ref[0, js[c], vs, :] for c in range(n)]
            acc_scr[vs, :] += _dot(vt[0] if n == 1 else jnp.concatenate(vt, axis=1), ps[h])

    def super_chunk(js, kinds):
        n = len(js)
        r0s = [pl.multiple_of(j * t, t) for j in js]
        lgs = [[logits(h, r0s[c]) for c in range(n)] for h in range(nh)]
        pens = selection_pens(r0s)
        ps, alphas = [], []
        for h in range(nh):
            xs = []
            for c, j in enumerate(js):
                diag = kinds[c] == "diag"
                lg = lgs[h][c]
                if is_dsa:
                    tile = 0 if diag else jnp.minimum(i - j, 2)
                    lg = lg + (pens[c] + bt_ref[tile, h])
                elif diag:
                    lg = lg + causal_pen
                xs.append(lg)
            m_old = m_scr[h, 0:1, :]
            m_new = m_old
            for x in xs:
                m_new = jnp.maximum(m_new, jnp.max(x, axis=0, keepdims=True))
            alpha = jnp.exp2(m_old - m_new)
            m_scr[h, 0:1, :] = m_new
            ps.append([jnp.exp2((x - m_new).astype(BF16)) for x in xs])
            alphas.append(alpha)
        for h in range(nh):
            vs = slice(vr * h, vr * (h + 1))
            pv = _dot(vT_ref[0, js[0], vs, :], ps[h][0])
            for c in range(1, n):
                pv = pv + _dot(vT_ref[0, js[c], vs, :], ps[h][c])
            acc_scr[vs, :] = alphas[h] * acc_scr[vs, :] + pv

    tail = ["near", "diag"] if is_dsa else ["diag"]
    nfar = jnp.maximum(i + 1 - len(tail), 0)

    def run_chunks(step):
        def group_body(u, carry):
            step([ATT_GROUP * u + c for c in range(ATT_GROUP)], ["far"] * ATT_GROUP)
            return carry

        lax.fori_loop(0, nfar // ATT_GROUP, group_body, 0)
        for rem in range(ATT_GROUP):
            @pl.when(jnp.logical_and(i + 1 >= len(tail), nfar % ATT_GROUP == rem))
            def _():
                first = i + 1 - len(tail) - rem
                step([first + c for c in range(rem + len(tail))], ["far"] * rem + tail)

        if is_dsa:
            @pl.when(i == 0)
            def _():
                step([i], ["diag"])

    pl.when(fixed_ref_ok)(functools.partial(run_chunks, fixed_ref_chunks))
    pl.when(jnp.logical_not(fixed_ref_ok))(functools.partial(run_chunks, super_chunk))

    outs = []
    for h in range(nh):
        outs.append(acc_scr[vr * h:vr * h + dv, :] / acc_scr[vr * h + dv:vr * h + dv + 1, :])
    o_ref[0] = jnp.transpose(jnp.concatenate(outs, axis=0)).astype(o_ref.dtype)


def _attention(qT, k, vT, kbound, dsa=None, blk=ATT_BLK):
    b, nh, l, _ = k.shape
    vr = vT.shape[2] // nh
    dv = vr - ONES_ROWS
    grid = (b, l // blk)
    qspec = lambda r: pl.BlockSpec((1, r, blk), lambda bi, i: (bi, 0, i))
    kspec = pl.BlockSpec((1, nh, l, HEAD_PAD), lambda bi, i: (bi, 0, 0, 0))
    vspec = pl.BlockSpec((1, l // blk, nh * vr, blk), lambda bi, i: (bi, 0, 0, 0))
    in_specs = [pl.BlockSpec(memory_space=pltpu.SMEM), qspec(nh * HEAD_PAD), kspec, vspec]
    args = [kbound, qT, k, vT]
    scratch = [pltpu.VMEM((nh, 8, blk), F32), pltpu.VMEM((nh * vr, blk), F32)]
    topk = 0
    if dsa is not None:
        iqT, wT, ik, bt = dsa
        topk = min(TOPK_MAX, l // 4)
        in_specs += [qspec(IDX_HEADS * IDX_DIM), qspec(IDX_HEADS),
                     pl.BlockSpec((1, l, IDX_DIM), lambda bi, i: (bi, 0, 0)),
                     _const_spec(bt.shape)]
        args += [iqT, wT, ik, bt]
        assert blk == 32 * SUBLANES
        nrow = (l // blk) * SUBLANES
        scratch += [pltpu.VMEM((l, blk), I32), pltpu.VMEM((8, blk), F32),
                    pltpu.VMEM((32, nrow, blk), I32), pltpu.VMEM((nrow, blk), I32)]
    return pl.pallas_call(
        functools.partial(_attn_body, nh, dv, blk, topk, dsa is not None),
        grid=grid, in_specs=in_specs,
        out_specs=pl.BlockSpec((1, blk, nh * dv), lambda bi, i: (bi, i, 0)),
        out_shape=jax.ShapeDtypeStruct((b, l, nh * dv), BF16),
        scratch_shapes=scratch,
        compiler_params=_cparams(("parallel", "arbitrary")),
        name="dsa_attn" if dsa is not None else "mla_attn",
    )(*args)


def _gla_body(tg, q_ref, k_ref, v_ref, la_ref, r_ref, go_ref, bd_ref, o_ref, st_scr, o_scr):
    @pl.when(pl.program_id(1) == 0)
    def _():
        st_scr[...] = jnp.zeros(st_scr.shape, F32)

    cs = B_CHUNK
    la = la_ref[0]
    rl = lax.broadcasted_iota(I32, (tg, LANE), 0) & (cs - 1)
    b = la
    s = 1
    while s < cs:
        b = b + jnp.where(rl >= s, pltpu.roll(b, s, 0), 0.0)
        s *= 2
    q = q_ref[0]
    k = k_ref[0]
    qb = q * jnp.exp(b)
    tri = lax.broadcasted_iota(I32, (cs, cs), 1) <= lax.broadcasted_iota(I32, (cs, cs), 0)
    khead = lax.broadcasted_iota(I32, (cs, LANE), 1) // B_KEY_DIM
    vhead = lax.broadcasted_iota(I32, (cs, GROUP_WIDTH), 1) // B_VAL_DIM
    same_head = (lax.broadcasted_iota(I32, (GROUP_WIDTH, LANE), 0) // B_VAL_DIM
                 == lax.broadcasted_iota(I32, (GROUP_WIDTH, LANE), 1) // B_KEY_DIM)
    for c in range(tg // cs):
        sl = slice(c * cs, (c + 1) * cs)
        bc = b[sl]
        mid = bc[cs // 2:cs // 2 + 1]
        last = bc[cs - 1:cs]
        qe = q[sl] * jnp.exp(bc - mid)
        ke = (k[sl] * jnp.exp(mid - bc)).astype(BF16)
        kd = (k[sl] * jnp.exp(last - bc)).astype(BF16)
        qbc = qb[sl].astype(BF16)
        vc = v_ref[0, sl, :]
        st = st_scr[...]
        qe4 = jnp.concatenate([jnp.where(khead == h, qe, 0.0) for h in range(B_HEADS)], axis=0)
        a_all = _dot_nt(qe4.astype(BF16), ke)
        o = _dot_nt(qbc, st.astype(BF16))
        for h in range(B_HEADS):
            a = jnp.where(tri, a_all[cs * h:cs * (h + 1)], 0.0).astype(BF16)
            o = o + _dot(a, jnp.where(vhead == h, vc, jnp.zeros_like(vc)))
        o_scr[sl, :] = o
        st_scr[...] = st * jnp.exp(last) + jnp.where(same_head, _dot_tn(vc, kd), 0.0)
    o = o_scr[...]
    ms = _group_sum(o * o, bd_ref[...]) * (1.0 / B_VAL_DIM)
    r = r_ref[0]
    o_ref[0] = (o * lax.rsqrt(ms + EPS) * go_ref[...] * (r * jax.nn.sigmoid(r))).astype(o_ref.dtype)


def _gla(bq, bk, bv, bla, br, go, bd, layer, tg=1024):
    b, l, _ = bq.shape
    tg = min(tg, l)
    tok = lambda w: pl.BlockSpec((1, tg, w), lambda bi, i: (bi, i, 0))
    return pl.pallas_call(
        functools.partial(_gla_body, tg),
        grid=(b, l // tg),
        in_specs=[tok(LANE), tok(LANE), tok(GROUP_WIDTH), tok(LANE), tok(GROUP_WIDTH),
                  _const_spec(go.shape, layer), _const_spec(bd.shape)],
        out_specs=tok(GROUP_WIDTH),
        out_shape=jax.ShapeDtypeStruct((b, l, GROUP_WIDTH), BF16),
        scratch_shapes=[pltpu.VMEM((B_HEADS * B_VAL_DIM, B_HEADS * B_KEY_DIM), F32),
                        pltpu.VMEM((tg, GROUP_WIDTH), F32)],
        compiler_params=_cparams(("parallel", "arbitrary")),
        name="gla",
    )(bq, bk, bv, bla, br, go, bd)


CONV_HIST = 32


def _conv_body(tc, h_ref, w_ref, b_ref, g_ref, o_ref, buf):
    @pl.when(pl.program_id(1) == 0)
    def _():
        buf[0:CONV_HIST, :] = jnp.zeros((CONV_HIST, C_CHANNELS), F32)

    @pl.when(pl.program_id(1) > 0)
    def _():
        buf[0:CONV_HIST, :] = buf[tc:tc + CONV_HIST, :]

    buf[CONV_HIST:CONV_HIST + tc, :] = h_ref[0]
    acc = jnp.zeros((tc, C_CHANNELS), F32) + b_ref[...]
    base = CONV_HIST - (C_KERNEL - 1)
    hb = buf[...]
    rows = tc + CONV_HIST
    for r in range(SUBLANES):
        shifted = hb if r == 0 else pltpu.roll(hb, rows - r, 0)
        for j in range(C_KERNEL):
            if (base + j) % SUBLANES == r:
                a0 = base + j - r
                acc = acc + shifted[a0:a0 + tc, :] * w_ref[j:j + 1, :]
    ms = jnp.mean(acc * acc, axis=-1, keepdims=True)
    y = acc * lax.rsqrt(ms + EPS) * g_ref[...]
    o_ref[0] = (y * jax.nn.sigmoid(y)).astype(o_ref.dtype)


def _conv(ch, w, bias, g, layer, tc=512):
    b, l, c = ch.shape
    tc = min(tc, l)
    tok = pl.BlockSpec((1, tc, c), lambda bi, i: (bi, i, 0))
    return pl.pallas_call(
        functools.partial(_conv_body, tc),
        grid=(b, l // tc),
        in_specs=[tok] + [_const_spec(a.shape, layer) for a in (w, bias, g)],
        out_specs=tok,
        out_shape=jax.ShapeDtypeStruct((b, l, c), BF16),
        scratch_shapes=[pltpu.VMEM((tc + CONV_HIST, c), F32)],
        compiler_params=_cparams(("parallel", "arbitrary")),
        name="conv",
    )(ch, w, bias, g)


def _t5_bucket(dist):
    max_exact = REL_BUCKETS // 2
    d = jnp.maximum(dist, 0)
    df = jnp.maximum(d, 1).astype(F32)
    large = max_exact + (jnp.log(df / max_exact) / math.log(REL_MAX_DIST / max_exact)
                         * (REL_BUCKETS - max_exact)).astype(I32)
    large = jnp.minimum(large, REL_BUCKETS - 1)
    return jnp.where(d < max_exact, d, large)


def _pad_cols(w, width):
    return jnp.pad(w, ((0, 0), (0, width - w.shape[1])))


def _lane_rep(v):
    return jnp.broadcast_to(v[:, None], (v.shape[0], LANE))


def _pad_heads_rows(w, heads, dim):
    w = w.reshape(heads, dim, w.shape[1])
    return jnp.pad(w, ((0, 0), (0, HEAD_PAD - dim), (0, 0))).reshape(heads * HEAD_PAD, -1)


def _split_w_in(w_in):
    widths = (256, 256, 256, 256, 32, 8, 128, 128, 256, 16, 256, 512, 256, 128, 32)
    offs = np.cumsum((0,) + widths)
    return [w_in[:, offs[n]:offs[n + 1]] for n in range(len(widths))]


def _row(v):
    return v[None, :].astype(F32)


def _one_layer_params(w):
    (aq, ak, av, iq, ik, iw, bq, bk, bv, bg, br, cu, dcq, dckv, dkpe) = _split_w_in(w["w_in"])
    wn = jnp.concatenate([ak, _pad_cols(ik, LANE), bq, bk, bv, _pad_cols(bg, LANE), br, cu, dcq, dckv,
                          _pad_cols(dkpe, LANE)], axis=1).astype(BF16)
    wt = jnp.concatenate([_pad_heads_rows(aq.T, A_HEADS, A_HEAD_DIM), av.T, iq.T,
                          jnp.pad(iw.T, ((0, T_TOT - T_IW - IDX_HEADS), (0, 0)))], axis=0).astype(BF16)
    pad_to = lambda v, n: jnp.pad(v, (0, n - v.shape[0]))
    ukv = w["d_ukv"].reshape(D_KV_RANK, D_HEADS, D_NOPE + D_V)
    wuk = ukv[:, :, :D_NOPE].reshape(D_KV_RANK, D_HEADS * D_NOPE)
    wuv = ukv[:, :, D_NOPE:].reshape(D_KV_RANK, D_HEADS * D_V)
    gdk = w["d_k_norm"]
    return dict(
        gmix=_row(w["mix_norm"]), wn=wn, wt=wt,
        gaq=_lane_rep(pad_to(w["a_q_norm"], HEAD_PAD)),
        gak=_row(jnp.tile(w["a_k_norm"], A_HEADS)),
        wgu=jnp.pad(w["b_gate_up"], ((0, LANE - B_GATE_RANK), (0, 0))).astype(BF16),
        bgb=_row(w["b_gate_bias"]),
        gqa=_row(w["d_qa_norm"]),
        wuq=_pad_heads_rows(w["d_uq"].T, D_HEADS, D_QK).astype(BF16),
        gdq=_lane_rep(pad_to(w["d_q_norm"], HEAD_PAD)),
        gkva=_row(w["d_kva_norm"]), wuk=wuk.astype(BF16), wuvT=wuv.T.astype(BF16),
        gdk=_row(jnp.tile(gdk[:D_NOPE], D_HEADS)), gdkpe=_row(pad_to(gdk[D_NOPE:], LANE)),
        gbo=_row(jnp.tile(w["b_out_norm"], B_HEADS)),
        cw=jnp.pad(w["c_dw_w"][:, 0, :], ((0, CONV_HIST - C_KERNEL), (0, 0))).astype(F32),
        cb=_row(w["c_dw_b"]), cg=_row(w["c_norm"]),
    )


def _shared_tables(seq):
    hid = np.arange(GROUP_WIDTH) // 64
    bd = jnp.asarray(hid[:, None] == hid[None, :], dtype=BF16)
    half = D_ROPE // 2
    freqs = ROPE_THETA ** (-jnp.arange(half, dtype=F32) / half)
    ang = jnp.arange(seq).astype(F32)[:, None] * freqs[None, :]
    cos, sin = jnp.cos(ang), jnp.sin(ang)
    zeros = jnp.zeros((seq, LANE - D_ROPE), F32)
    cpe = jnp.concatenate([cos, cos, zeros], axis=1)
    spe = jnp.concatenate([-sin, sin, zeros], axis=1)
    return dict(bd=bd, cosT=cos.T, sinT=sin.T, cpe=cpe, spe=spe)


def _bias_tiles(rel_bias, blk):
    assert REL_MAX_DIST <= blk + 1
    kk = jnp.arange(blk)[:, None]
    qq = jnp.arange(blk)[None, :]
    rb = rel_bias.astype(F32).T
    m = 2 * blk + 1
    onehot = _t5_bucket(jnp.arange(m))[None, :, None] == jnp.arange(REL_BUCKETS)
    by_dist = jnp.sum(jnp.where(onehot, rb[:, None, :], 0.0), axis=-1) * LOG2E
    toe = jnp.tile(by_dist, (1, blk))[:, :blk * (m - 1)].reshape(-1, blk, m - 1)
    d0 = jnp.where(kk <= qq, toe[:, :, :blk], NEG)
    d1 = toe[:, :, blk:2 * blk]
    far = jnp.broadcast_to(by_dist[:, m - 1][:, None, None], d1.shape)
    return jnp.stack([d0, d1, far])


def _key_bounds(knorm, bias_max):
    km = jnp.sqrt(jnp.max(knorm[..., 0], axis=1))
    kb_a = jnp.concatenate([km[:, :A_HEADS].reshape(-1), bias_max.reshape(1)])
    return kb_a, km[:, A_HEADS:].reshape(-1)


def kernel(x, ffn1_norm, ffn1_gate, ffn1_up, ffn1_down, mix_norm, w_in, a_q_norm, a_k_norm, rel_bias,
           b_gate_up, b_gate_bias, b_out_norm, c_dw_w, c_dw_b, c_norm, d_qa_norm, d_uq, d_kva_norm,
           d_ukv, d_q_norm, d_k_norm, w_out, ffn2_norm, ffn2_gate, ffn2_up, ffn2_down):
    w = dict(mix_norm=mix_norm, w_in=w_in, a_q_norm=a_q_norm, a_k_norm=a_k_norm, b_gate_up=b_gate_up,
             b_gate_bias=b_gate_bias, b_out_norm=b_out_norm, c_dw_w=c_dw_w, c_dw_b=c_dw_b, c_norm=c_norm,
             d_qa_norm=d_qa_norm, d_uq=d_uq, d_kva_norm=d_kva_norm, d_ukv=d_ukv,
             d_q_norm=d_q_norm, d_k_norm=d_k_norm)
    bsz, seq, dm = x.shape
    depth = w_in.shape[0]
    blk = min(ATT_BLK, seq)
    bt = _bias_tiles(rel_bias, blk)
    bias_max = jnp.max(bt)
    p = {**jax.vmap(_one_layer_params)(w), **_shared_tables(seq)}
    stacked_row = lambda v: v[:, None, :].astype(F32)
    ffn1 = (stacked_row(ffn1_norm), ffn1_gate, ffn1_up, ffn1_down)
    ffn2 = (stacked_row(ffn2_norm), ffn2_gate, ffn2_up, ffn2_down)
    wo = w_out.astype(BF16)
    x2 = x.reshape(bsz * seq, dm)
    for l in range(depth):
        x2 = _ffn(x2, *ffn1, l)
        (aqT, akh, avT, iqT, aik, iwT, bq, bk, bv, bla, br, ch, dqT, dkh, dvT, knorm) = _mix_in(
            x2.reshape(bsz, seq, dm), p, l, blk=blk)
        kb_a, kb_d = _key_bounds(knorm, bias_max)
        y_a = _attention(aqT, akh, avT, kb_a, dsa=(iqT, iwT, aik, bt), blk=blk)
        y_b = _gla(bq, bk, bv, bla, br, p["gbo"], p["bd"], l)
        y_c = _conv(ch, p["cw"], p["cb"], p["cg"], l)
        y_d = _attention(dqT, dkh, dvT, kb_d, blk=blk)
        ys = [y.reshape(bsz * seq, GROUP_WIDTH) for y in (y_a, y_b, y_c, y_d)]
        x2 = _ffn(x2, *ffn2, l, mix=(ys, wo))
    return x2.reshape(bsz, seq, dm)
```

```python
import functools
import math

import jax
import jax.numpy as jnp
import numpy as np
from jax import lax
from jax.experimental import pallas as pl
from jax.experimental.pallas import tpu as pltpu

F32 = jnp.float32
BF16 = jnp.bfloat16
I32 = jnp.int32

EPS = 1e-6
GROUP_WIDTH = 256
A_HEADS, A_HEAD_DIM = 4, 64
IDX_HEADS, IDX_DIM = 8, 32
TOPK_MAX = 256
REL_BUCKETS, REL_MAX_DIST = 32, 128
B_HEADS, B_KEY_DIM, B_VAL_DIM, B_GATE_RANK = 4, 32, 64, 16
B_GATE_TAU = 16.0
B_CHUNK = 64
C_CHANNELS, C_KERNEL = 256, 31
D_HEADS, D_Q_RANK, D_KV_RANK, D_NOPE, D_ROPE, D_V = 4, 256, 128, 64, 32, 64
D_QK = D_NOPE + D_ROPE
ROPE_THETA = 10000.0

LANE = 128
SUBLANES = 8
HEAD_PAD = 128
ONES_ROWS = 16
ATT_BLK = 256
ATT_GROUP = 4
SCORE_GROUP = 4
BOUND_SLACK = 1.01
FIXED_REF_MAX = 40.0
INT_MIN = -2 ** 31
NEG = -1e30
LOG2E = math.log2(math.e)
VMEM_LIMIT = 56 * 1024 * 1024

N_AK, N_IK, N_BQ, N_BK, N_BV, N_BG, N_BR, N_CU, N_DCQ, N_DCKV, N_DKPE, N_TOT = (
    0, 256, 384, 512, 640, 896, 1024, 1280, 1792, 2048, 2176, 2304)
T_AQ, T_AV, T_IQ, T_IW, T_TOT = 0, 512, 768, 1024, 1040


def _dot(a, b):
    return jnp.dot(a, b, preferred_element_type=F32)


def _dot_nt(a, b):
    return lax.dot_general(a, b, (((1,), (1,)), ((), ())), preferred_element_type=F32)


def _dot_tn(a, b):
    return lax.dot_general(a, b, (((0,), (0,)), ((), ())), preferred_element_type=F32)


def _group_sum(x2, bd):
    hi = x2.astype(BF16)
    lo = (x2 - hi.astype(F32)).astype(BF16)
    return _dot(hi, bd) + _dot(lo, bd)


def _const_spec(shape, layer=None):
    nd = len(shape)
    if layer is None:
        return pl.BlockSpec(shape, lambda *_: (0,) * nd, pipeline_mode=pl.Buffered(1))
    return pl.BlockSpec((None,) + tuple(shape[1:]), lambda *_: (layer,) + (0,) * (nd - 1),
                        pipeline_mode=pl.Buffered(1))


def _cparams(sem):
    return pltpu.CompilerParams(dimension_semantics=sem, vmem_limit_bytes=VMEM_LIMIT)


def _ffn_body(has_mix, fc, layer, *refs):
    if has_mix:
        x_ref, ya, yb, yc, yd, wo_ref, g_ref, wg_hbm, wu_hbm, wd_hbm, o_ref, h_scr, wg_buf, wu_buf, wd_buf, sems = refs
    else:
        x_ref, g_ref, wg_hbm, wu_hbm, wd_hbm, o_ref, h_scr, wg_buf, wu_buf, wd_buf, sems = refs
    d_ff = wg_buf.shape[1]
    nchunk = d_ff // fc
    first = pl.program_id(0) == 0

    def weight_copies(c):
        cols = pl.ds(c * fc, fc)
        return (pltpu.make_async_copy(wg_hbm.at[layer, :, cols], wg_buf.at[:, cols], sems.at[0, c]),
                pltpu.make_async_copy(wu_hbm.at[layer, :, cols], wu_buf.at[:, cols], sems.at[1, c]),
                pltpu.make_async_copy(wd_hbm.at[layer, cols, :], wd_buf.at[cols, :], sems.at[2, c]))

    @pl.when(first)
    def _():
        for c in range(nchunk):
            gate_cp, up_cp, _ = weight_copies(c)
            gate_cp.start(priority=0)
            up_cp.start(priority=1)
        for c in range(nchunk):
            weight_copies(c)[2].start(priority=c % 2)

    def tile(wait_for_weights):
        x = x_ref[...]
        if has_mix:
            y = jnp.concatenate([ya[...], yb[...], yc[...], yd[...]], axis=-1)
            x = x + _dot(y, wo_ref[...])
        ms = jnp.mean(x * x, axis=-1, keepdims=True)
        xn = (x * lax.rsqrt(ms + EPS) * g_ref[...]).astype(BF16)
        for c in range(nchunk):
            sl = slice(c * fc, (c + 1) * fc)
            if wait_for_weights:
                gate_cp, up_cp, _ = weight_copies(c)
                gate_cp.wait()
                up_cp.wait()
            gate = _dot(xn, wg_buf[:, sl].astype(BF16))
            up = _dot(xn, wu_buf[:, sl].astype(BF16))
            h_scr[:, sl] = (gate * jax.nn.sigmoid(gate) * up).astype(BF16)
        if wait_for_weights:
            for c in range(nchunk):
                weight_copies(c)[2].wait()
        o_ref[...] = x + 0.5 * _dot(h_scr[...], wd_buf[...].astype(BF16))

    pl.when(first)(functools.partial(tile, True))
    pl.when(jnp.logical_not(first))(functools.partial(tile, False))


def _ffn(x2, g, wg, wu, wd, layer, mix=None, tm=512, fc=256):
    m, d = x2.shape
    d_ff = wg.shape[2]
    tm = min(tm, m)
    row = lambda w: pl.BlockSpec((tm, w), lambda i: (i, 0))
    in_specs = [row(d)]
    args = [x2]
    if mix is not None:
        ys, wo = mix
        in_specs += [row(GROUP_WIDTH)] * 4 + [_const_spec(wo.shape, layer)]
        args += list(ys) + [wo]
    in_specs += [_const_spec(g.shape, layer)] + [pl.BlockSpec(memory_space=pl.ANY)] * 3
    args += [g, wg, wu, wd]
    return pl.pallas_call(
        functools.partial(_ffn_body, mix is not None, fc, layer),
        grid=(m // tm,),
        in_specs=in_specs,
        out_specs=row(d),
        out_shape=jax.ShapeDtypeStruct((m, d), F32),
        scratch_shapes=[pltpu.VMEM((tm, d_ff), BF16), pltpu.VMEM((d, d_ff), F32), pltpu.VMEM((d, d_ff), F32),
                        pltpu.VMEM((d_ff, d), F32), pltpu.SemaphoreType.DMA((3, d_ff // fc))],
        compiler_params=_cparams(("arbitrary",)),
        name="ffn_mix" if mix is not None else "ffn",
    )(*args)


def _mix_in_body(tm, blk,
                 x_ref, gmix_ref, wn_ref, wt_ref, bd_ref,
                 gaq_ref, gak_ref,
                 wgu_ref, bgb_ref,
                 gqa_ref, wuq_ref, gdq_ref, cosT_ref, sinT_ref,
                 gkva_ref, wuk_ref, wuvT_ref, gdk_ref, gdkpe_ref, cpe_ref, spe_ref,
                 aqT_ref, ak_ref, avT_ref, iqT_ref, ik_ref, iwT_ref,
                 bq_ref, bk_ref, bv_ref, bla_ref, br_ref,
                 ch_ref,
                 dqT_ref, dk_ref, dvT_ref, knorm_ref):
    nlt = tm // LANE
    x = x_ref[0]
    ms = jnp.mean(x * x, axis=-1, keepdims=True)
    xn = (x * lax.rsqrt(ms + EPS) * gmix_ref[...]).astype(BF16)
    bd = bd_ref[...]
    lane = lax.broadcasted_iota(I32, (tm, LANE), 1)

    def lanes(g):
        return jnp.tile(g, (1, nlt))

    z = _dot(xn, wn_ref[...])
    zt = _dot_nt(wt_ref[...], xn)

    def zs(off, width):
        return z[:, off:off + width]

    cq = zs(N_DCQ, D_Q_RANK)
    cq_ms = jnp.mean(cq * cq, axis=-1, keepdims=True)
    cqn = (cq * lax.rsqrt(cq_ms + EPS) * gqa_ref[...]).astype(BF16)
    ckv = zs(N_DCKV, D_KV_RANK)
    ckv_ms = jnp.mean(ckv * ckv, axis=-1, keepdims=True)
    ckvn = (ckv * lax.rsqrt(ckv_ms + EPS) * gkva_ref[...]).astype(BF16)
    dq = _dot_nt(wuq_ref[...], cqn).reshape(D_HEADS, HEAD_PAD, tm)
    kn = _dot(ckvn, wuk_ref[...])
    dv = _dot_nt(wuvT_ref[...], ckvn).astype(BF16)
    gate = _dot(zs(N_BG, LANE).astype(BF16), wgu_ref[...]) + bgb_ref[...]
    ak = zs(N_AK, GROUP_WIDTH)
    ak_ms = _group_sum(ak * ak, bd) * (1.0 / A_HEAD_DIM)
    kn_ss = _group_sum(kn * kn, bd)

    aq = zt[T_AQ:T_AQ + A_HEADS * HEAD_PAD].reshape(A_HEADS, HEAD_PAD, tm)
    aq_ms = jnp.sum(aq * aq, axis=1, keepdims=True) * (1.0 / A_HEAD_DIM)
    aq = aq * lax.rsqrt(aq_ms + EPS) * lanes(gaq_ref[...])[None] * (A_HEAD_DIM ** -0.5 * LOG2E)
    aqT_ref[0] = aq.reshape(A_HEADS * HEAD_PAD, tm).astype(BF16)
    def emit_key_norms(first_row, sq_norms):
        top = jnp.max(sq_norms, axis=0, keepdims=True)
        for h in range(4):
            knorm_ref[0, 0, first_row + h:first_row + h + 1, :] = jnp.broadcast_to(
                top[:, 64 * h:64 * h + 1], (1, LANE))

    def with_ones_rows(vt, heads, dim):
        ones = jnp.ones((ONES_ROWS, tm), BF16)
        return jnp.concatenate([r for h in range(heads) for r in (vt[dim * h:dim * (h + 1)], ones)], axis=0)

    av = with_ones_rows(zt[T_AV:T_AV + GROUP_WIDTH].astype(BF16), A_HEADS, A_HEAD_DIM)
    for c in range(tm // blk):
        avT_ref[0, c] = av[:, c * blk:(c + 1) * blk]
    iqT_ref[0] = zt[T_IQ:T_IQ + IDX_HEADS * IDX_DIM].astype(BF16)
    iwT_ref[0] = zt[T_IW:T_IW + IDX_HEADS] * ((IDX_HEADS ** -0.5) * (IDX_DIM ** -0.5))

    ak = ak * lax.rsqrt(ak_ms + EPS) * gak_ref[...]
    for h in range(A_HEADS):
        pair = ak[:, LANE * (h // 2):LANE * (h // 2) + LANE]
        if h % 2 == 1:
            pair = pltpu.roll(pair, 64, 1)
        kh = jnp.where(lane < A_HEAD_DIM, pair, 0.0).astype(BF16)
        ak_ref[0, h] = kh
    emit_key_norms(0, _dot((ak * ak).astype(BF16), bd))
    ik_ref[0] = zs(N_IK, LANE)[:, :IDX_DIM].astype(BF16)

    bq_ref[0] = zs(N_BQ, LANE) * (B_KEY_DIM ** -0.5)
    bk_ref[0] = zs(N_BK, LANE)
    bv_ref[0] = zs(N_BV, GROUP_WIDTH).astype(BF16)
    bla_ref[0] = (jnp.minimum(gate, 0.0) - jnp.log(1.0 + jnp.exp(-jnp.abs(gate)))) * (1.0 / B_GATE_TAU)
    br_ref[0] = zs(N_BR, GROUP_WIDTH)

    ca = zs(N_CU, C_CHANNELS)
    cg = zs(N_CU + C_CHANNELS, C_CHANNELS)
    ch_ref[0] = ca * jax.nn.sigmoid(cg)

    dq_ms = jnp.sum(dq * dq, axis=1, keepdims=True) * (1.0 / D_QK)
    dq = dq * lax.rsqrt(dq_ms + EPS) * lanes(gdq_ref[...])[None] * (D_QK ** -0.5 * LOG2E)
    half = D_ROPE // 2
    x1 = dq[:, D_NOPE:D_NOPE + half]
    x2 = dq[:, D_NOPE + half:D_QK]
    cs = cosT_ref[...][None]
    sn = sinT_ref[...][None]
    dq = jnp.concatenate([dq[:, :D_NOPE], x1 * cs - x2 * sn, x2 * cs + x1 * sn, dq[:, D_QK:]], axis=1)
    dqT_ref[0] = dq.reshape(D_HEADS * HEAD_PAD, tm).astype(BF16)

    dv = with_ones_rows(dv, D_HEADS, D_V)
    for c in range(tm // blk):
        dvT_ref[0, c] = dv[:, c * blk:(c + 1) * blk]
    kpe = zs(N_DKPE, LANE)
    ss = kn_ss + jnp.sum(kpe * kpe, axis=-1, keepdims=True)
    rinv = lax.rsqrt(ss * (1.0 / D_QK) + EPS)
    kn = kn * rinv * gdk_ref[...]
    pe = kpe * gdkpe_ref[...]
    partner = jnp.where(lane < half, pltpu.roll(pe, LANE - half, 1), pltpu.roll(pe, half, 1))
    pe = pe * cpe_ref[...] + partner * spe_ref[...]
    pe = pltpu.roll(pe, D_NOPE, 1)
    for h in range(D_HEADS):
        pair = kn[:, LANE * (h // 2):LANE * (h // 2) + LANE]
        rpair = rinv[:, LANE * (h // 2):LANE * (h // 2) + LANE]
        if h % 2 == 1:
            pair = pltpu.roll(pair, 64, 1)
        else:
            rpair = pltpu.roll(rpair, 64, 1)
        kh = jnp.where(lane < D_NOPE, pair, pe * rpair).astype(BF16)
        dk_ref[0, h] = kh
    pe_sq = _dot((pe * pe).astype(BF16), jnp.ones((LANE, LANE), BF16))
    emit_key_norms(A_HEADS, _dot((kn * kn).astype(BF16), bd) + jnp.tile(pe_sq, (1, 2)) * (rinv * rinv))


def _mix_in(x3, p, layer, tm=512, blk=ATT_BLK):
    b, l, d = x3.shape
    tm = min(tm, l)
    grid = (b, l // tm)
    nck = l // blk
    tok = lambda w: pl.BlockSpec((1, tm, w), lambda bi, i: (bi, i, 0))
    tokT = lambda r: pl.BlockSpec((1, r, tm), lambda bi, i: (bi, 0, i))
    headk = pl.BlockSpec((1, 4, tm, HEAD_PAD), lambda bi, i: (bi, 0, i, 0))
    vrows = GROUP_WIDTH + 4 * ONES_ROWS
    chunkT = pl.BlockSpec((1, tm // blk, vrows, blk), lambda bi, i: (bi, i, 0, 0))
    postab = lambda r: pl.BlockSpec((r, tm), lambda bi, i: (0, i))
    posrow = pl.BlockSpec((tm, LANE), lambda bi, i: (i, 0))
    consts = [p["gmix"], p["wn"], p["wt"], p["bd"], p["gaq"], p["gak"], p["wgu"], p["bgb"],
              p["gqa"], p["wuq"], p["gdq"]]
    consts2 = [p["gkva"], p["wuk"], p["wuvT"], p["gdk"], p["gdkpe"]]
    lspec = lambda a: _const_spec(a.shape) if a is p["bd"] else _const_spec(a.shape, layer)
    in_specs = ([tok(d)] + [lspec(a) for a in consts]
                + [postab(D_ROPE // 2), postab(D_ROPE // 2)]
                + [lspec(a) for a in consts2] + [posrow, posrow])
    args = [x3] + consts + [p["cosT"], p["sinT"]] + consts2 + [p["cpe"], p["spe"]]
    sd = jax.ShapeDtypeStruct
    out_shape = [
        sd((b, A_HEADS * HEAD_PAD, l), BF16), sd((b, A_HEADS, l, HEAD_PAD), BF16),
        sd((b, nck, vrows, blk), BF16), sd((b, IDX_HEADS * IDX_DIM, l), BF16),
        sd((b, l, IDX_DIM), BF16), sd((b, IDX_HEADS, l), F32),
        sd((b, l, LANE), F32), sd((b, l, LANE), F32), sd((b, l, GROUP_WIDTH), BF16),
        sd((b, l, LANE), F32), sd((b, l, GROUP_WIDTH), F32),
        sd((b, l, C_CHANNELS), F32),
        sd((b, D_HEADS * HEAD_PAD, l), BF16), sd((b, D_HEADS, l, HEAD_PAD), BF16),
        sd((b, nck, vrows, blk), BF16),
        sd((b, l // tm, A_HEADS + D_HEADS, LANE), F32),
    ]
    out_specs = [
        tokT(A_HEADS * HEAD_PAD), headk, chunkT, tokT(IDX_HEADS * IDX_DIM),
        tok(IDX_DIM), tokT(IDX_HEADS),
        tok(LANE), tok(LANE), tok(GROUP_WIDTH), tok(LANE), tok(GROUP_WIDTH),
        tok(C_CHANNELS),
        tokT(D_HEADS * HEAD_PAD), headk, chunkT,
        pl.BlockSpec((1, 1, A_HEADS + D_HEADS, LANE), lambda bi, i: (bi, i, 0, 0)),
    ]
    return pl.pallas_call(
        functools.partial(_mix_in_body, tm, blk),
        grid=grid, in_specs=in_specs, out_specs=out_specs, out_shape=out_shape,
        compiler_params=_cparams(("parallel", "parallel")),
        name="mix_in",
    )(*args)


def _attn_body(nh, dv, blk, topk, is_dsa, *refs):
    if is_dsa:
        (kb_ref, qT_ref, k_ref, vT_ref, iqT_ref, wT_ref, ik_ref, bt_ref,
         o_ref, m_scr, acc_scr, key_scr, run_scr, plane_scr, active_scr) = refs
    else:
        kb_ref, qT_ref, k_ref, vT_ref, o_ref, m_scr, acc_scr = refs
    vr = dv + ONES_ROWS
    i = pl.program_id(1)
    t = blk
    row = lax.broadcasted_iota(I32, (t, t), 0)
    col = lax.broadcasted_iota(I32, (t, t), 1)
    causal_pen = jnp.where(row <= col, 0.0, NEG)

    m_scr[...] = jnp.full(m_scr.shape, NEG, F32)
    acc_scr[...] = jnp.zeros(acc_scr.shape, F32)

    if is_dsa:
        @pl.when(i == 0)
        def _():
            plane_scr[...] = jnp.zeros(plane_scr.shape, I32)

        def score_chunk(j, diag):
            r0 = pl.multiple_of(j * t, t)
            ikc = ik_ref[0, pl.ds(r0, t), :]
            s = jnp.zeros((t, t), F32)
            for h in range(IDX_HEADS):
                d = _dot(ikc, iqT_ref[0, IDX_DIM * h:IDX_DIM * (h + 1), :])
                s = s + jnp.maximum(d, 0.0) * wT_ref[0, h:h + 1, :]
            bits = lax.bitcast_convert_type(s, I32)
            key = jnp.where(bits < 0, bits ^ 0x7FFFFFFF, bits)
            if diag:
                key = jnp.where(row <= col, key, INT_MIN)
            key_scr[pl.ds(r0, t), :] = key
            w = [key[SUBLANES * r:SUBLANES * (r + 1), :] for r in range(32)]
            step, mask = 16, 0x0000FFFF
            while step:
                for lo in range(32):
                    if lo & step == 0:
                        hi = lo + step
                        swap = (w[lo] ^ jnp.right_shift(w[hi], step)) & mask
                        w[lo] = w[lo] ^ swap
                        w[hi] = w[hi] ^ jnp.left_shift(swap, step)
                step //= 2
                mask ^= (mask << step) & 0xFFFFFFFF
            c0 = pl.multiple_of(j * SUBLANES, SUBLANES)
            w[0] = ~w[0]
            for p in range(32):
                plane_scr[p, pl.ds(c0, SUBLANES), :] = w[p]

        def score_group(u, carry):
            for c in range(SCORE_GROUP):
                score_chunk(SCORE_GROUP * u + c, False)
            return carry

        lax.fori_loop(0, i // SCORE_GROUP, score_group, 0)
        for rem in range(SCORE_GROUP):
            @pl.when(i % SCORE_GROUP == rem)
            def _():
                for c in range(rem):
                    score_chunk(i - rem + c, False)
                score_chunk(i, True)

        def kth_largest(nrow):
            rows = pl.ds(0, nrow)
            in_range = lax.broadcasted_iota(I32, (nrow, t), 0) < (i + 1) * SUBLANES

            def col_count(words):
                pc = lax.population_count(words).reshape(nrow // SUBLANES, SUBLANES, t)
                return jnp.sum(jnp.sum(pc, axis=0), axis=0, keepdims=True)

            def decide(plane, active, n_gt, ans_u):
                ones = col_count(active & plane_scr[plane, rows, :])
                take = n_gt + ones >= topk
                bit = lax.shift_right_logical(jnp.int32(INT_MIN), jnp.int32(plane))
                return (jnp.where(take, 0, -1), n_gt + jnp.where(take, 0, ones),
                        ans_u | jnp.where(take, bit, 0))

            active_scr[rows, :] = jnp.where(in_range, -1, 0)
            state = decide(0, active_scr[rows, :], jnp.zeros((1, t), I32), jnp.zeros((1, t), I32))

            def plane_body(plane, state):
                flip, n_gt, ans_u = state
                active = active_scr[rows, :] & (plane_scr[plane - 1, rows, :] ^ flip)
                active_scr[rows, :] = active
                return decide(plane, active, n_gt, ans_u)

            _, n_gt, ans_u = lax.fori_loop(1, 32, plane_body, state)
            return n_gt, ans_u

        nrow_all = plane_scr.shape[1]
        nrow_half = (nrow_all // SUBLANES // 2) * SUBLANES
        if nrow_half:
            n_gt, ans_u = lax.cond((i + 1) * SUBLANES <= nrow_half,
                                   functools.partial(kth_largest, nrow_half),
                                   functools.partial(kth_largest, nrow_all))
        else:
            n_gt, ans_u = kth_largest(nrow_all)
        ans = ans_u ^ INT_MIN
        need = (topk - n_gt).astype(F32)
        run_scr[...] = jnp.zeros(run_scr.shape, F32)
        stri = jnp.where(col < row, 1.0, 0.0).astype(BF16)

    def logits(h, r0):
        return _dot(k_ref[0, h, pl.ds(r0, t), :], qT_ref[0, HEAD_PAD * h:HEAD_PAD * (h + 1), :])

    def selection_pens(r0s):
        pens = []
        if is_dsa:
            for r0 in r0s:
                kc = key_scr[pl.ds(r0, t), :]
                eq = kc == ans
                eqf = jnp.where(eq, 1.0, 0.0)
                run = run_scr[0:1, :]
                rank = _dot(stri, eqf.astype(BF16)) + run
                run_scr[0:1, :] = run + jnp.sum(eqf, axis=0, keepdims=True)
                pens.append(jnp.where(kc > ans, 0.0, jnp.where(eq, jnp.where(rank < need, 0.0, NEG), NEG)))
        return pens

    b_idx = pl.program_id(0)
    bounds = []
    for h in range(nh):
        qh = qT_ref[0, HEAD_PAD * h:HEAD_PAD * (h + 1), :].astype(F32)
        bound = jnp.sqrt(jnp.sum(qh * qh, axis=0, keepdims=True)) * (kb_ref[b_idx * nh + h] * BOUND_SLACK)
        if is_dsa:
            bound = bound + kb_ref[kb_ref.shape[0] - 1]
        bounds.append(bound)
    bound_max = jnp.max(functools.reduce(jnp.maximum, bounds))
    fixed_ref_ok = bound_max <= FIXED_REF_MAX

    def fixed_ref_chunks(js, kinds):
        n = len(js)
        r0s = [pl.multiple_of(j * t, t) for j in js]
        lgs = [[logits(h, r0s[c]) for c in range(n)] for h in range(nh)]
        pens = selection_pens(r0s)
        ps = []
        for h in range(nh):
            row_ps = []
            if is_dsa:
                far_ref = bounds[h] - bt_ref[2, h, 0:1, :]
            for c in range(n):
                x, ref = lgs[h][c], bounds[h]
                if is_dsa and kinds[c] == "far":
                    x, ref = x + pens[c], far_ref
                elif is_dsa:
                    x = x + (pens[c] + bt_ref[0 if kinds[c] == "diag" else 1, h])
                elif kinds[c] == "diag":
                    x = x + causal_pen
                row_ps.append(jnp.exp2(x - ref).astype(BF16))
            ps.append(row_ps[0] if n == 1 else jnp.concatenate(row_ps, axis=0))
        for h in range(nh):
            vs = slice(vr * h, vr * (h + 1))
            vt = [vT_ref[0, js[c], vs, :] for c in range(n)]
            acc_scr[vs, :] += _dot(vt[0] if n == 1 else jnp.concatenate(vt, axis=1), ps[h])

    def super_chunk(js, kinds):
        n = len(js)
        r0s = [pl.multiple_of(j * t, t) for j in js]
        lgs = [[logits(h, r0s[c]) for c in range(n)] for h in range(nh)]
        pens = selection_pens(r0s)
        ps, alphas = [], []
        for h in range(nh):
            xs = []
            for c, j in enumerate(js):
                diag = kinds[c] == "diag"
                lg = lgs[h][c]
                if is_dsa:
                    tile = 0 if diag else jnp.minimum(i - j, 2)
                    lg = lg + (pens[c] + bt_ref[tile, h])
                elif diag:
                    lg = lg + causal_pen
                xs.append(lg)
            m_old = m_scr[h, 0:1, :]
            m_new = m_old
            for x in xs:
                m_new = jnp.maximum(m_new, jnp.max(x, axis=0, keepdims=True))
            alpha = jnp.exp2(m_old - m_new)
            m_scr[h, 0:1, :] = m_new
            ps.append([jnp.exp2((x - m_new).astype(BF16)) for x in xs])
            alphas.append(alpha)
        for h in range(nh):
            vs = slice(vr * h, vr * (h + 1))
            pv = _dot(vT_ref[0, js[0], vs, :], ps[h][0])
            for c in range(1, n):
                pv = pv + _dot(vT_ref[0, js[c], vs, :], ps[h][c])
            acc_scr[vs, :] = alphas[h] * acc_scr[vs, :] + pv

    tail = ["near", "diag"] if is_dsa else ["diag"]
    nfar = jnp.maximum(i + 1 - len(tail), 0)

    def run_chunks(step):
        def group_body(u, carry):
            step([ATT_GROUP * u + c for c in range(ATT_GROUP)], ["far"] * ATT_GROUP)
            return carry

        lax.fori_loop(0, nfar // ATT_GROUP, group_body, 0)
        for rem in range(ATT_GROUP):
            @pl.when(jnp.logical_and(i + 1 >= len(tail), nfar % ATT_GROUP == rem))
            def _():
                first = i + 1 - len(tail) - rem
                step([first + c for c in range(rem + len(tail))], ["far"] * rem + tail)

        if is_dsa:
            @pl.when(i == 0)
            def _():
                step([i], ["diag"])

    pl.when(fixed_ref_ok)(functools.partial(run_chunks, fixed_ref_chunks))
    pl.when(jnp.logical_not(fixed_ref_ok))(functools.partial(run_chunks, super_chunk))

    outs = []
    for h in range(nh):
        outs.append(acc_scr[vr * h:vr * h + dv, :] / acc_scr[vr * h + dv:vr * h + dv + 1, :])
    o_ref[0] = jnp.transpose(jnp.concatenate(outs, axis=0)).astype(o_ref.dtype)


def _attention(qT, k, vT, kbound, dsa=None, blk=ATT_BLK):
    b, nh, l, _ = k.shape
    vr = vT.shape[2] // nh
    dv = vr - ONES_ROWS
    grid = (b, l // blk)
    qspec = lambda r: pl.BlockSpec((1, r, blk), lambda bi, i: (bi, 0, i))
    kspec = pl.BlockSpec((1, nh, l, HEAD_PAD), lambda bi, i: (bi, 0, 0, 0))
    vspec = pl.BlockSpec((1, l // blk, nh * vr, blk), lambda bi, i: (bi, 0, 0, 0))
    in_specs = [pl.BlockSpec(memory_space=pltpu.SMEM), qspec(nh * HEAD_PAD), kspec, vspec]
    args = [kbound, qT, k, vT]
    scratch = [pltpu.VMEM((nh, 8, blk), F32), pltpu.VMEM((nh * vr, blk), F32)]
    topk = 0
    if dsa is not None:
        iqT, wT, ik, bt = dsa
        topk = min(TOPK_MAX, l // 4)
        in_specs += [qspec(IDX_HEADS * IDX_DIM), qspec(IDX_HEADS),
                     pl.BlockSpec((1, l, IDX_DIM), lambda bi, i: (bi, 0, 0)),
                     _const_spec(bt.shape)]
        args += [iqT, wT, ik, bt]
        assert blk == 32 * SUBLANES
        nrow = (l // blk) * SUBLANES
        scratch += [pltpu.VMEM((l, blk), I32), pltpu.VMEM((8, blk), F32),
                    pltpu.VMEM((32, nrow, blk), I32), pltpu.VMEM((nrow, blk), I32)]
    return pl.pallas_call(
        functools.partial(_attn_body, nh, dv, blk, topk, dsa is not None),
        grid=grid, in_specs=in_specs,
        out_specs=pl.BlockSpec((1, blk, nh * dv), lambda bi, i: (bi, i, 0)),
        out_shape=jax.ShapeDtypeStruct((b, l, nh * dv), BF16),
        scratch_shapes=scratch,
        compiler_params=_cparams(("parallel", "arbitrary")),
        name="dsa_attn" if dsa is not None else "mla_attn",
    )(*args)


def _gla_body(tg, q_ref, k_ref, v_ref, la_ref, r_ref, go_ref, bd_ref, o_ref, st_scr, o_scr):
    @pl.when(pl.program_id(1) == 0)
    def _():
        st_scr[...] = jnp.zeros(st_scr.shape, F32)

    cs = B_CHUNK
    la = la_ref[0]
    rl = lax.broadcasted_iota(I32, (tg, LANE), 0) & (cs - 1)
    b = la
    s = 1
    while s < cs:
        b = b + jnp.where(rl >= s, pltpu.roll(b, s, 0), 0.0)
        s *= 2
    q = q_ref[0]
    k = k_ref[0]
    qb = q * jnp.exp(b)
    tri = lax.broadcasted_iota(I32, (cs, cs), 1) <= lax.broadcasted_iota(I32, (cs, cs), 0)
    khead = lax.broadcasted_iota(I32, (cs, LANE), 1) // B_KEY_DIM
    vhead = lax.broadcasted_iota(I32, (cs, GROUP_WIDTH), 1) // B_VAL_DIM
    same_head = (lax.broadcasted_iota(I32, (GROUP_WIDTH, LANE), 0) // B_VAL_DIM
                 == lax.broadcasted_iota(I32, (GROUP_WIDTH, LANE), 1) // B_KEY_DIM)
    for c in range(tg // cs):
        sl = slice(c * cs, (c + 1) * cs)
        bc = b[sl]
        mid = bc[cs // 2:cs // 2 + 1]
        last = bc[cs - 1:cs]
        qe = q[sl] * jnp.exp(bc - mid)
        ke = (k[sl] * jnp.exp(mid - bc)).astype(BF16)
        kd = (k[sl] * jnp.exp(last - bc)).astype(BF16)
        qbc = qb[sl].astype(BF16)
        vc = v_ref[0, sl, :]
        st = st_scr[...]
        qe4 = jnp.concatenate([jnp.where(khead == h, qe, 0.0) for h in range(B_HEADS)], axis=0)
        a_all = _dot_nt(qe4.astype(BF16), ke)
        o = _dot_nt(qbc, st.astype(BF16))
        for h in range(B_HEADS):
            a = jnp.where(tri, a_all[cs * h:cs * (h + 1)], 0.0).astype(BF16)
            o = o + _dot(a, jnp.where(vhead == h, vc, jnp.zeros_like(vc)))
        o_scr[sl, :] = o
        st_scr[...] = st * jnp.exp(last) + jnp.where(same_head, _dot_tn(vc, kd), 0.0)
    o = o_scr[...]
    ms = _group_sum(o * o, bd_ref[...]) * (1.0 / B_VAL_DIM)
    r = r_ref[0]
    o_ref[0] = (o * lax.rsqrt(ms + EPS) * go_ref[...] * (r * jax.nn.sigmoid(r))).astype(o_ref.dtype)


def _gla(bq, bk, bv, bla, br, go, bd, layer, tg=1024):
    b, l, _ = bq.shape
    tg = min(tg, l)
    tok = lambda w: pl.BlockSpec((1, tg, w), lambda bi, i: (bi, i, 0))
    return pl.pallas_call(
        functools.partial(_gla_body, tg),
        grid=(b, l // tg),
        in_specs=[tok(LANE), tok(LANE), tok(GROUP_WIDTH), tok(LANE), tok(GROUP_WIDTH),
                  _const_spec(go.shape, layer), _const_spec(bd.shape)],
        out_specs=tok(GROUP_WIDTH),
        out_shape=jax.ShapeDtypeStruct((b, l, GROUP_WIDTH), BF16),
        scratch_shapes=[pltpu.VMEM((B_HEADS * B_VAL_DIM, B_HEADS * B_KEY_DIM), F32),
                        pltpu.VMEM((tg, GROUP_WIDTH), F32)],
        compiler_params=_cparams(("parallel", "arbitrary")),
        name="gla",
    )(bq, bk, bv, bla, br, go, bd)


CONV_HIST = 32


def _conv_body(tc, h_ref, w_ref, b_ref, g_ref, o_ref, buf):
    @pl.when(pl.program_id(1) == 0)
    def _():
        buf[0:CONV_HIST, :] = jnp.zeros((CONV_HIST, C_CHANNELS), F32)

    @pl.when(pl.program_id(1) > 0)
    def _():
        buf[0:CONV_HIST, :] = buf[tc:tc + CONV_HIST, :]

    buf[CONV_HIST:CONV_HIST + tc, :] = h_ref[0]
    acc = jnp.zeros((tc, C_CHANNELS), F32) + b_ref[...]
    base = CONV_HIST - (C_KERNEL - 1)
    hb = buf[...]
    rows = tc + CONV_HIST
    for r in range(SUBLANES):
        shifted = hb if r == 0 else pltpu.roll(hb, rows - r, 0)
        for j in range(C_KERNEL):
            if (base + j) % SUBLANES == r:
                a0 = base + j - r
                acc = acc + shifted[a0:a0 + tc, :] * w_ref[j:j + 1, :]
    ms = jnp.mean(acc * acc, axis=-1, keepdims=True)
    y = acc * lax.rsqrt(ms + EPS) * g_ref[...]
    o_ref[0] = (y * jax.nn.sigmoid(y)).astype(o_ref.dtype)


def _conv(ch, w, bias, g, layer, tc=512):
    b, l, c = ch.shape
    tc = min(tc, l)
    tok = pl.BlockSpec((1, tc, c), lambda bi, i: (bi, i, 0))
    return pl.pallas_call(
        functools.partial(_conv_body, tc),
        grid=(b, l // tc),
        in_specs=[tok] + [_const_spec(a.shape, layer) for a in (w, bias, g)],
        out_specs=tok,
        out_shape=jax.ShapeDtypeStruct((b, l, c), BF16),
        scratch_shapes=[pltpu.VMEM((tc + CONV_HIST, c), F32)],
        compiler_params=_cparams(("parallel", "arbitrary")),
        name="conv",
    )(ch, w, bias, g)


def _t5_bucket(dist):
    max_exact = REL_BUCKETS // 2
    d = jnp.maximum(dist, 0)
    df = jnp.maximum(d, 1).astype(F32)
    large = max_exact + (jnp.log(df / max_exact) / math.log(REL_MAX_DIST / max_exact)
                         * (REL_BUCKETS - max_exact)).astype(I32)
    large = jnp.minimum(large, REL_BUCKETS - 1)
    return jnp.where(d < max_exact, d, large)


def _pad_cols(w, width):
    return jnp.pad(w, ((0, 0), (0, width - w.shape[1])))


def _lane_rep(v):
    return jnp.broadcast_to(v[:, None], (v.shape[0], LANE))


def _pad_heads_rows(w, heads, dim):
    w = w.reshape(heads, dim, w.shape[1])
    return jnp.pad(w, ((0, 0), (0, HEAD_PAD - dim), (0, 0))).reshape(heads * HEAD_PAD, -1)


def _split_w_in(w_in):
    widths = (256, 256, 256, 256, 32, 8, 128, 128, 256, 16, 256, 512, 256, 128, 32)
    offs = np.cumsum((0,) + widths)
    return [w_in[:, offs[n]:offs[n + 1]] for n in range(len(widths))]


def _row(v):
    return v[None, :].astype(F32)


def _one_layer_params(w):
    (aq, ak, av, iq, ik, iw, bq, bk, bv, bg, br, cu, dcq, dckv, dkpe) = _split_w_in(w["w_in"])
    wn = jnp.concatenate([ak, _pad_cols(ik, LANE), bq, bk, bv, _pad_cols(bg, LANE), br, cu, dcq, dckv,
                          _pad_cols(dkpe, LANE)], axis=1).astype(BF16)
    wt = jnp.concatenate([_pad_heads_rows(aq.T, A_HEADS, A_HEAD_DIM), av.T, iq.T,
                          jnp.pad(iw.T, ((0, T_TOT - T_IW - IDX_HEADS), (0, 0)))], axis=0).astype(BF16)
    pad_to = lambda v, n: jnp.pad(v, (0, n - v.shape[0]))
    ukv = w["d_ukv"].reshape(D_KV_RANK, D_HEADS, D_NOPE + D_V)
    wuk = ukv[:, :, :D_NOPE].reshape(D_KV_RANK, D_HEADS * D_NOPE)
    wuv = ukv[:, :, D_NOPE:].reshape(D_KV_RANK, D_HEADS * D_V)
    gdk = w["d_k_norm"]
    return dict(
        gmix=_row(w["mix_norm"]), wn=wn, wt=wt,
        gaq=_lane_rep(pad_to(w["a_q_norm"], HEAD_PAD)),
        gak=_row(jnp.tile(w["a_k_norm"], A_HEADS)),
        wgu=jnp.pad(w["b_gate_up"], ((0, LANE - B_GATE_RANK), (0, 0))).astype(BF16),
        bgb=_row(w["b_gate_bias"]),
        gqa=_row(w["d_qa_norm"]),
        wuq=_pad_heads_rows(w["d_uq"].T, D_HEADS, D_QK).astype(BF16),
        gdq=_lane_rep(pad_to(w["d_q_norm"], HEAD_PAD)),
        gkva=_row(w["d_kva_norm"]), wuk=wuk.astype(BF16), wuvT=wuv.T.astype(BF16),
        gdk=_row(jnp.tile(gdk[:D_NOPE], D_HEADS)), gdkpe=_row(pad_to(gdk[D_NOPE:], LANE)),
        gbo=_row(jnp.tile(w["b_out_norm"], B_HEADS)),
        cw=jnp.pad(w["c_dw_w"][:, 0, :], ((0, CONV_HIST - C_KERNEL), (0, 0))).astype(F32),
        cb=_row(w["c_dw_b"]), cg=_row(w["c_norm"]),
    )


def _shared_tables(seq):
    hid = np.arange(GROUP_WIDTH) // 64
    bd = jnp.asarray(hid[:, None] == hid[None, :], dtype=BF16)
    half = D_ROPE // 2
    freqs = ROPE_THETA ** (-jnp.arange(half, dtype=F32) / half)
    ang = jnp.arange(seq).astype(F32)[:, None] * freqs[None, :]
    cos, sin = jnp.cos(ang), jnp.sin(ang)
    zeros = jnp.zeros((seq, LANE - D_ROPE), F32)
    cpe = jnp.concatenate([cos, cos, zeros], axis=1)
    spe = jnp.concatenate([-sin, sin, zeros], axis=1)
    return dict(bd=bd, cosT=cos.T, sinT=sin.T, cpe=cpe, spe=spe)


def _bias_tiles(rel_bias, blk):
    assert REL_MAX_DIST <= blk + 1
    kk = jnp.arange(blk)[:, None]
    qq = jnp.arange(blk)[None, :]
    rb = rel_bias.astype(F32).T
    m = 2 * blk + 1
    onehot = _t5_bucket(jnp.arange(m))[None, :, None] == jnp.arange(REL_BUCKETS)
    by_dist = jnp.sum(jnp.where(onehot, rb[:, None, :], 0.0), axis=-1) * LOG2E
    toe = jnp.tile(by_dist, (1, blk))[:, :blk * (m - 1)].reshape(-1, blk, m - 1)
    d0 = jnp.where(kk <= qq, toe[:, :, :blk], NEG)
    d1 = toe[:, :, blk:2 * blk]
    far = jnp.broadcast_to(by_dist[:, m - 1][:, None, None], d1.shape)
    return jnp.stack([d0, d1, far])


def _key_bounds(knorm, bias_max):
    km = jnp.sqrt(jnp.max(knorm[..., 0], axis=1))
    kb_a = jnp.concatenate([km[:, :A_HEADS].reshape(-1), bias_max.reshape(1)])
    return kb_a, km[:, A_HEADS:].reshape(-1)


def kernel(x, ffn1_norm, ffn1_gate, ffn1_up, ffn1_down, mix_norm, w_in, a_q_norm, a_k_norm, rel_bias,
           b_gate_up, b_gate_bias, b_out_norm, c_dw_w, c_dw_b, c_norm, d_qa_norm, d_uq, d_kva_norm,
           d_ukv, d_q_norm, d_k_norm, w_out, ffn2_norm, ffn2_gate, ffn2_up, ffn2_down):
    w = dict(mix_norm=mix_norm, w_in=w_in, a_q_norm=a_q_norm, a_k_norm=a_k_norm, b_gate_up=b_gate_up,
             b_gate_bias=b_gate_bias, b_out_norm=b_out_norm, c_dw_w=c_dw_w, c_dw_b=c_dw_b, c_norm=c_norm,
             d_qa_norm=d_qa_norm, d_uq=d_uq, d_kva_norm=d_kva_norm, d_ukv=d_ukv,
             d_q_norm=d_q_norm, d_k_norm=d_k_norm)
    bsz, seq, dm = x.shape
    depth = w_in.shape[0]
    blk = min(ATT_BLK, seq)
    bt = _bias_tiles(rel_bias, blk)
    bias_max = jnp.max(bt)
    p = {**jax.vmap(_one_layer_params)(w), **_shared_tables(seq)}
    stacked_row = lambda v: v[:, None, :].astype(F32)
    ffn1 = (stacked_row(ffn1_norm), ffn1_gate, ffn1_up, ffn1_down)
    ffn2 = (stacked_row(ffn2_norm), ffn2_gate, ffn2_up, ffn2_down)
    wo = w_out.astype(BF16)
    x2 = x.reshape(bsz * seq, dm)
    for l in range(depth):
        x2 = _ffn(x2, *ffn1, l)
        (aqT, akh, avT, iqT, aik, iwT, bq, bk, bv, bla, br, ch, dqT, dkh, dvT, knorm) = _mix_in(
            x2.reshape(bsz, seq, dm), p, l, blk=blk)
        kb_a, kb_d = _key_bounds(knorm, bias_max)
        y_a = _attention(aqT, akh, avT, kb_a, dsa=(iqT, iwT, aik, bt), blk=blk)
        y_b = _gla(bq, bk, bv, bla, br, p["gbo"], p["bd"], l)
        y_c = _conv(ch, p["cw"], p["cb"], p["cg"], l)
        y_d = _attention(dqT, dkh, dvT, kb_d, blk=blk)
        ys = [y.reshape(bsz * seq, GROUP_WIDTH) for y in (y_a, y_b, y_c, y_d)]
        x2 = _ffn(x2, *ffn2, l, mix=(ys, wo))
    return x2.reshape(bsz, seq, dm)
```
